```python
import jax, jax.numpy as jnp
from jax import lax
import numpy as np

D_MODEL = 1024
BATCH = 1
SEQ = 16384
DEPTH = 4

CTX_LEN = 256
GRID_W = 64
N_MIXERS = 3
EPS = 1e-6
NEG_INF = -1e30

ATTN_HEADS = 16
ATTN_KV_HEADS = 4
ATTN_HEAD_DIM = D_MODEL // ATTN_HEADS
WINDOW = 128
BLOCK = 128
ROPE_BASE = 10000.0

RET_HEADS = 4
RET_DK = D_MODEL // RET_HEADS
RET_DV = 2 * D_MODEL // RET_HEADS
RET_CHUNK = 128

POOL_WINDOWS = (2, 4, 8, 16)
POOL_GROUP = D_MODEL // len(POOL_WINDOWS)

D_FF = 2816
N_EXPERTS = 8
TOP_K = 2
D_FF_EXPERT = 3584

kernel_name = "hybrid_diffusion_interleaved_trunk"


def rms_norm(x, g):
    xf = x.astype(jnp.float32)
    y = xf * lax.rsqrt(jnp.mean(xf * xf, axis=-1, keepdims=True) + EPS)
    return (y * g.astype(jnp.float32)).astype(x.dtype)


def modulate(h, shift, scale):
    return h * (1.0 + scale) + shift


def ada(cvec, w, b):
    m = (jax.nn.silu(cvec) @ w + b).reshape(-1, 1, 6 * D_MODEL)
    return jnp.split(m, 6, axis=-1)


def _rope_half(x, pos):
    n = x.shape[-1] // 2
    inv = ROPE_BASE ** (-jnp.arange(n, dtype=jnp.float32) / n)
    ang = pos.astype(jnp.float32)[:, None] * inv[None, :]
    cos = jnp.cos(ang)[None, :, None, :]
    sin = jnp.sin(ang)[None, :, None, :]
    xf = x.astype(jnp.float32)
    x1, x2 = xf[..., :n], xf[..., n:]
    return jnp.concatenate([x1 * cos - x2 * sin, x2 * cos + x1 * sin], axis=-1).astype(x.dtype)


def rope_2d(x, row_pos, col_pos):
    half = x.shape[-1] // 2
    return jnp.concatenate([_rope_half(x[..., :half], row_pos), _rope_half(x[..., half:], col_pos)], axis=-1)


def attention_mixer(h_lat, h_ctx, w_qkv, w_o, sink, row_pos, col_pos, with_ctx):
    B, S, _ = h_lat.shape
    H, KV, Dh = ATTN_HEADS, ATTN_KV_HEADS, ATTN_HEAD_DIM
    G = H // KV
    scale = Dh ** -0.5

    def proj(h):
        q, k, v = jnp.split(h @ w_qkv, [H * Dh, H * Dh + KV * Dh], axis=-1)
        T = h.shape[1]
        return q.reshape(B, T, H, Dh), k.reshape(B, T, KV, Dh), v.reshape(B, T, KV, Dh)

    q, k, v = proj(h_lat)
    q = rope_2d(q, row_pos, col_pos).reshape(B, S, KV, G, Dh)
    k = rope_2d(k, row_pos, col_pos)
    qc, kc, vc = proj(h_ctx)
    qc = qc.reshape(B, -1, KV, G, Dh)
    Lc = kc.shape[1]
    sink_f = sink.astype(jnp.float32).reshape(1, KV, G, 1, 1)

    kp = jnp.pad(k, ((0, 0), (BLOCK, BLOCK), (0, 0), (0, 0)))
    vp = jnp.pad(v, ((0, 0), (BLOCK, BLOCK), (0, 0), (0, 0)))
    nb = S // BLOCK
    qb = q.reshape(B, nb, BLOCK, KV, G, Dh).swapaxes(0, 1)
    qi = jnp.arange(BLOCK)
    kj = jnp.arange(3 * BLOCK)

    def block_fn(args):
        b, qblk = args
        kblk = lax.dynamic_slice_in_dim(kp, b * BLOCK, 3 * BLOCK, axis=1)
        vblk = lax.dynamic_slice_in_dim(vp, b * BLOCK, 3 * BLOCK, axis=1)
        s_loc = jnp.einsum('bqkgd,bnkd->bkgqn', qblk, kblk).astype(jnp.float32) * scale
        qpos = b * BLOCK + qi
        kpos = b * BLOCK - BLOCK + kj
        valid = (jnp.abs(qpos[:, None] - kpos[None, :]) <= WINDOW) & (kpos[None, :] >= 0) & (kpos[None, :] < S)
        s_loc = jnp.where(valid, s_loc, NEG_INF)
        s_ctx = jnp.einsum('bqkgd,bnkd->bkgqn', qblk, kc).astype(jnp.float32) * scale
        s_sink = jnp.broadcast_to(sink_f, (B, KV, G, BLOCK, 1))
        p = jax.nn.softmax(jnp.concatenate([s_sink, s_ctx, s_loc], axis=-1), axis=-1)
        p_ctx = p[..., 1:1 + Lc].astype(vc.dtype)
        p_loc = p[..., 1 + Lc:].astype(vblk.dtype)
        return (jnp.einsum('bkgqn,bnkd->bqkgd', p_ctx, vc)
                + jnp.einsum('bkgqn,bnkd->bqkgd', p_loc, vblk))

    o = lax.map(block_fn, (jnp.arange(nb), qb))
    y_lat = o.swapaxes(0, 1).reshape(B, S, H * Dh) @ w_o

    y_ctx = None
    if with_ctx:
        s = jnp.einsum('bqkgd,bnkd->bkgqn', qc, kc).astype(jnp.float32) * scale
        s = jnp.concatenate([jnp.broadcast_to(sink_f, s.shape[:-1] + (1,)), s], axis=-1)
        p = jax.nn.softmax(s, axis=-1)[..., 1:].astype(vc.dtype)
        y_ctx = jnp.einsum('bkgqn,bnkd->bqkgd', p, vc).reshape(B, Lc, H * Dh) @ w_o
    return y_lat, y_ctx


def retention_chunks(q, k, v, log_gamma, s0):
    B, T, H, dk = q.shape
    dv = v.shape[-1]
    C = RET_CHUNK
    nc = T // C
    idx = jnp.arange(C, dtype=jnp.float32)
    lg = log_gamma[:, None]
    diff = idx[:, None] - idx[None, :]
    intra = jnp.where(diff >= 0, jnp.exp(lg[:, :, None] * jnp.maximum(diff, 0.0)), 0.0)
    q_dec = jnp.exp(lg * (idx + 1.0))[None, :, :, None]
    k_dec = jnp.exp(lg * (C - 1.0 - idx))[None, :, :, None]
    c_dec = jnp.exp(lg * C).reshape(1, H, 1, 1)

    def chunked(a):
        return a.astype(jnp.float32).reshape(B, nc, C, H, a.shape[-1]).transpose(1, 0, 3, 2, 4)

    def step(state, xs):
        qc_, kc_, vc_ = xs
        sc = jnp.einsum('bhid,bhjd->bhij', qc_, kc_) * intra
        o = jnp.einsum('bhij,bhjv->bhiv', sc, vc_) + jnp.einsum('bhid,bhdv->bhiv', qc_ * q_dec, state)
        state = state * c_dec + jnp.einsum('bhjd,bhjv->bhdv', kc_ * k_dec, vc_)
        return state, o

    s_final, o = lax.scan(step, s0, (chunked(q), chunked(k), chunked(v)))
    return o.transpose(1, 0, 3, 2, 4).reshape(B, T, H, dv), s_final


def head_group_norm(o):
    mu = jnp.mean(o, axis=-1, keepdims=True)
    var = jnp.mean(jnp.square(o - mu), axis=-1, keepdims=True)
    return (o - mu) * lax.rsqrt(var + EPS)


def retention_mixer(h_lat, h_ctx, w_in, w_o, log_decay, with_ctx):
    B = h_lat.shape[0]
    H, dk, dv = RET_HEADS, RET_DK, RET_DV
    lg_f = -jnp.exp(log_decay[0].astype(jnp.float32))
    lg_b = -jnp.exp(log_decay[1].astype(jnp.float32))

    def proj(h):
        T = h.shape[1]
        q, k, v, gf, gb = jnp.split(h @ w_in, [H * dk, 2 * H * dk, 2 * H * dk + H * dv, 2 * H * dk + 2 * H * dv], axis=-1)
        return (q.reshape(B, T, H, dk), k.reshape(B, T, H, dk) * (dk ** -0.5), v.reshape(B, T, H, dv),
                gf.reshape(B, T, H, dv), gb.reshape(B, T, H, dv))

    def flip(a):
        return jnp.flip(a, axis=1)

    def combine(of, ob, gf, gb, dtype):
        y = (jax.nn.silu(gf.astype(jnp.float32)) * head_group_norm(of)
             + jax.nn.silu(gb.astype(jnp.float32)) * head_group_norm(ob))
        return y.reshape(y.shape[0], y.shape[1], H * dv).astype(dtype) @ w_o

    zeros = jnp.zeros((B, H, dk, dv), jnp.float32)
    qc, kc, vc, gfc, gbc = proj(h_ctx)
    of_c, s_f = retention_chunks(qc, kc, vc, lg_f, zeros)
    ob_c, s_b = retention_chunks(flip(qc), flip(kc), flip(vc), lg_b, zeros)

    q, k, v, gf, gb = proj(h_lat)
    of_l, _ = retention_chunks(q, k, v, lg_f, s_f)
    ob_l, _ = retention_chunks(flip(q), flip(k), flip(v), lg_b, s_b)
    y_lat = combine(of_l, flip(ob_l), gf, gb, h_lat.dtype)
    y_ctx = combine(of_c, flip(ob_c), gfc, gbc, h_ctx.dtype) if with_ctx else None
    return y_lat, y_ctx


def pool_tokens(h, w_pool, scale):
    T = h.shape[1]
    hf = h.astype(jnp.float32)
    cs = jnp.pad(jnp.cumsum(hf, axis=1), ((0, 0), (1, 0), (0, 0)))
    t = jnp.arange(T)
    outs = []
    for g, w in enumerate(POOL_WINDOWS):
        lo = jnp.clip(t - w // 2, 0, T)
        hi = jnp.clip(t + w // 2, 0, T)
        sl = slice(g * POOL_GROUP, (g + 1) * POOL_GROUP)
        csg = cs[..., sl]
        mean = (jnp.take(csg, hi, axis=1) - jnp.take(csg, lo, axis=1)) / (hi - lo).astype(jnp.float32)[None, :, None]
        outs.append((mean - hf[..., sl]).astype(h.dtype) @ w_pool[g])
    return jnp.concatenate(outs, axis=-1) * scale


def pool_mixer(h_lat, h_ctx, w_pool, scale, with_ctx):
    y_lat = pool_tokens(h_lat, w_pool, scale)
    y_ctx = pool_tokens(h_ctx, w_pool, scale) if with_ctx else None
    return y_lat, y_ctx


def swiglu(h, w1, w3, w2):
    return (jax.nn.silu(h @ w1) * (h @ w3)) @ w2


def moe_swiglu(h, w_router, b_router, w1, w3, w2):
    logits = (h @ w_router).astype(jnp.float32) + b_router.astype(jnp.float32)
    top_v, top_i = lax.top_k(logits, TOP_K)
    top_w = jax.nn.softmax(top_v, axis=-1)
    combine = jnp.sum(jax.nn.one_hot(top_i, N_EXPERTS, dtype=jnp.float32) * top_w[..., None], axis=-2)
    out = jnp.zeros_like(h)
    for e in range(N_EXPERTS):
        out = out + combine[..., e:e + 1].astype(h.dtype) * swiglu(h, w1[e], w3[e], w2[e])
    return out


def setup_inputs(seed: int = 0) -> dict:
    key = jax.random.key(seed)
    ks = iter(jax.random.split(key, 32))
    D = D_MODEL
    n_attn = len(range(0, DEPTH, N_MIXERS))
    n_ret = len(range(1, DEPTH, N_MIXERS))
    n_pool = len(range(2, DEPTH, N_MIXERS))
    n_dense = len(range(0, DEPTH, 2))
    n_moe = len(range(1, DEPTH, 2))
    qkv_w = (ATTN_HEADS + 2 * ATTN_KV_HEADS) * ATTN_HEAD_DIM
    ret_in_w = 2 * RET_HEADS * RET_DK + 3 * RET_HEADS * RET_DV
    base_decay = jnp.log(-jnp.log1p(-(2.0 ** (-5.0 - jnp.arange(RET_HEADS, dtype=jnp.float32)))))

    def nrm(shape, s):
        return jax.random.normal(next(ks), shape, jnp.float32) * s

    return {
        "x": nrm((BATCH, SEQ, D), 1.0),
        "c": nrm((BATCH, D), 1.0),
        "ctx": nrm((BATCH, CTX_LEN, D), 1.0),
        "c_ctx": nrm((D,), 1.0),
        "w_ada": nrm((DEPTH, D, 6 * D), 0.3 * D ** -0.5),
        "b_ada": nrm((DEPTH, 6 * D), 0.02),
        "norm_g": 1.0 + nrm((DEPTH, 4, D), 0.05),
        "attn_w_qkv": nrm((n_attn, D, qkv_w), D ** -0.5),
        "attn_w_o": nrm((n_attn, ATTN_HEADS * ATTN_HEAD_DIM, D), (ATTN_HEADS * ATTN_HEAD_DIM) ** -0.5),
        "attn_sink": nrm((n_attn, ATTN_HEADS), 0.5),
        "ret_w_in": nrm((n_ret, D, ret_in_w), D ** -0.5),
        "ret_w_o": nrm((n_ret, RET_HEADS * RET_DV, D), (RET_HEADS * RET_DV) ** -0.5),
        "ret_log_decay": base_decay + nrm((n_ret, 2, RET_HEADS), 0.1),
        "pool_w": nrm((n_pool, len(POOL_WINDOWS), POOL_GROUP, POOL_GROUP), POOL_GROUP ** -0.5),
        "pool_scale": 1.0 + nrm((n_pool, D), 0.1),
        "ffn_w1": nrm((n_dense, D, D_FF), D ** -0.5),
        "ffn_w3": nrm((n_dense, D, D_FF), D ** -0.5),
        "ffn_w2": nrm((n_dense, D_FF, D), D_FF ** -0.5),
        "moe_w_router": nrm((n_moe, D, N_EXPERTS), D ** -0.5),
        "moe_b_router": nrm((n_moe, N_EXPERTS), 0.01),
        "moe_w1": nrm((n_moe, N_EXPERTS, D, D_FF_EXPERT), D ** -0.5),
        "moe_w3": nrm((n_moe, N_EXPERTS, D, D_FF_EXPERT), D ** -0.5),
        "moe_w2": nrm((n_moe, N_EXPERTS, D_FF_EXPERT, D), D_FF_EXPERT ** -0.5),
    }


def reference(x, c, ctx, c_ctx, w_ada, b_ada, norm_g, attn_w_qkv, attn_w_o, attn_sink,
              ret_w_in, ret_w_o, ret_log_decay, pool_w, pool_scale, ffn_w1, ffn_w3, ffn_w2,
              moe_w_router, moe_b_router, moe_w1, moe_w3, moe_w2):
    n_lat = x.shape[1]
    rows = n_lat // GRID_W
    row_pos = jnp.repeat(jnp.arange(rows), GRID_W)
    col_pos = jnp.tile(jnp.arange(GRID_W), rows)
    xc = ctx
    for i in range(DEPTH):
        last = i == DEPTH - 1
        g = norm_g[i]
        sh_l, sc_l, ga_l, sh2_l, sc2_l, ga2_l = ada(c, w_ada[i], b_ada[i])
        sh_c, sc_c, ga_c, sh2_c, sc2_c, ga2_c = ada(c_ctx[None, :], w_ada[i], b_ada[i])

        h_lat = modulate(rms_norm(x, g[0]), sh_l, sc_l)
        h_ctx = modulate(rms_norm(xc, g[0]), sh_c, sc_c)
        kind, j = i % N_MIXERS, i // N_MIXERS
        if kind == 0:
            y_lat, y_ctx = attention_mixer(h_lat, h_ctx, attn_w_qkv[j], attn_w_o[j], attn_sink[j], row_pos, col_pos, not last)
        elif kind == 1:
            y_lat, y_ctx = retention_mixer(h_lat, h_ctx, ret_w_in[j], ret_w_o[j], ret_log_decay[j], not last)
        else:
            y_lat, y_ctx = pool_mixer(h_lat, h_ctx, pool_w[j], pool_scale[j], not last)
        x = x + ga_l * rms_norm(y_lat, g[1])
        if not last:
            xc = xc + ga_c * rms_norm(y_ctx, g[1])

        f = i // 2
        if i % 2 == 0:
            def channel_mix(h):
                return swiglu(h, ffn_w1[f], ffn_w3[f], ffn_w2[f])
        else:
            def channel_mix(h):
                return moe_swiglu(h, moe_w_router[f], moe_b_router[f], moe_w1[f], moe_w3[f], moe_w2[f])
        x = x + ga2_l * rms_norm(channel_mix(modulate(rms_norm(x, g[2]), sh2_l, sc2_l)), g[3])
        if not last:
            xc = xc + ga2_c * rms_norm(channel_mix(modulate(rms_norm(xc, g[2]), sh2_c, sc2_c)), g[3])
    return x
```

```python
import functools

import jax
import jax.numpy as jnp
from jax import lax
from jax.experimental import pallas as pl
from jax.experimental.pallas import tpu as pltpu

F32 = jnp.float32
BF16 = jnp.bfloat16

EPS = 1e-6
NEG_INF = -1e30
LANES = 128
VMEM_LIMIT = 56 * 1024 * 1024

GRID_W = 64
N_MIXERS = 3
ATTN_HEADS = 16
ATTN_KV_HEADS = 4
WINDOW = 128
ROPE_BASE = 10000.0
RET_HEADS = 4
RET_CHUNK = 256
POOL_WINDOWS = (2, 4, 8, 16)
POOL_HALO = 8
N_EXPERTS = 8

SH1, SC1, GA1, SH2, SC2, GA2 = range(6)
LAT_ROW, CTX_ROW = 0, 1


def _cparams(*sem):
    return pltpu.CompilerParams(dimension_semantics=sem, vmem_limit_bytes=VMEM_LIMIT)


def _rms(x, g):
    return x * lax.rsqrt(jnp.mean(x * x, axis=-1, keepdims=True) + EPS) * g


def _mod_slice(mod_ref, row, slot, d):
    return mod_ref[row:row + 1, slot * d:(slot + 1) * d]


def _norm_mod(x, g, shift, scale):
    return _rms(x, g) * (1.0 + scale) + shift


def _ada_kernel(c_ref, w_ref, b_ref, o_ref):
    c = c_ref[...]
    s = c * jax.nn.sigmoid(c)
    o_ref[...] = jnp.dot(s, w_ref[...], preferred_element_type=F32,
                         precision=lax.Precision.HIGHEST) + b_ref[...]


def ada_table(cvecs, w_ada, b_ada):
    depth, d, n = w_ada.shape
    tn = 1536
    return pl.pallas_call(
        _ada_kernel,
        grid=(depth, n // tn),
        in_specs=[
            pl.BlockSpec((8, d), lambda i, j: (0, 0)),
            pl.BlockSpec((None, d, tn), lambda i, j: (i, 0, j)),
            pl.BlockSpec((None, 1, tn), lambda i, j: (i, 0, j)),
        ],
        out_specs=pl.BlockSpec((None, 8, tn), lambda i, j: (i, 0, j)),
        out_shape=jax.ShapeDtypeStruct((depth, 8, n), F32),
        compiler_params=_cparams("parallel", "parallel"),
        name="ada_table",
    )(cvecs, w_ada, b_ada.reshape(depth, 1, n))


def _rope(a, cos, sin):
    lane = lax.broadcasted_iota(jnp.int32, (a.shape[0], LANES), 1)
    first = (lane % 32) < 16
    outs = []
    for cb in range(a.shape[1] // LANES):
        blk = a[:, cb * LANES:(cb + 1) * LANES]
        partner = jnp.where(first, pltpu.roll(blk, LANES - 16, 1), pltpu.roll(blk, 16, 1))
        outs.append(blk * cos + partner * sin)
    return jnp.concatenate(outs, axis=1)


def _proj_kernel(x_ref, g_ref, mod_ref, w_ref, *rest, g_row, mod_row, slot, rope_tiles):
    if rope_tiles:
        cos_ref, sin_ref, o_ref, h_ref = rest
    else:
        o_ref, h_ref = rest
    d = x_ref.shape[1]
    j = pl.program_id(1)

    @pl.when(j == 0)
    def _():
        h = _norm_mod(x_ref[...], g_ref[g_row:g_row + 1, :],
                      _mod_slice(mod_ref, mod_row, slot, d), _mod_slice(mod_ref, mod_row, slot + 1, d))
        h_ref[...] = h.astype(BF16)

    acc = jnp.dot(h_ref[...], w_ref[...], preferred_element_type=F32)
    if rope_tiles:
        @pl.when(j < rope_tiles)
        def _():
            o_ref[...] = _rope(acc, cos_ref[...], sin_ref[...]).astype(o_ref.dtype)

        @pl.when(j >= rope_tiles)
        def _():
            o_ref[...] = acc.astype(o_ref.dtype)
    else:
        o_ref[...] = acc.astype(o_ref.dtype)


def norm_mod_matmul(x, norm_g, mods, layer, w, *, g_row, mod_row, slot, tm, tn, rope=None, rope_tiles=0):
    m, d = x.shape
    n = w.shape[1]
    tm = min(tm, m)
    in_specs = [
        pl.BlockSpec((tm, d), lambda i, j: (i, 0)),
        pl.BlockSpec((None,) + norm_g.shape[1:], lambda i, j: (layer, 0, 0)),
        pl.BlockSpec((None,) + mods.shape[1:], lambda i, j: (layer, 0, 0)),
        pl.BlockSpec((d, tn), lambda i, j: (0, j)),
    ]
    args = [x, norm_g, mods, w]
    if rope_tiles:
        in_specs += [pl.BlockSpec((tm, LANES), lambda i, j: (i, 0))] * 2
        args += list(rope)
    return pl.pallas_call(
        functools.partial(_proj_kernel, g_row=g_row, mod_row=mod_row, slot=slot, rope_tiles=rope_tiles),
        grid=(m // tm, n // tn),
        in_specs=in_specs,
        out_specs=pl.BlockSpec((tm, tn), lambda i, j: (i, j)),
        out_shape=jax.ShapeDtypeStruct((m, n), BF16),
        scratch_shapes=[pltpu.VMEM((tm, d), BF16)],
        compiler_params=_cparams("parallel", "arbitrary"),
        name="norm_mod_matmul",
    )(*args)


def _outproj_kernel(x_ref, g_ref, mod_ref, w_ref, *rest, n_y, g_row, mod_row, slot):
    y_refs, o_ref = rest[:n_y], rest[n_y]
    d = x_ref.shape[1]
    y = y_refs[0][...]
    if n_y == 2:
        y = (y.astype(F32) + y_refs[1][...].astype(F32)).astype(BF16)
    t = jnp.dot(y, w_ref[...], preferred_element_type=F32)
    gate = _mod_slice(mod_ref, mod_row, slot, d)
    o_ref[...] = x_ref[...] + gate * _rms(t, g_ref[g_row:g_row + 1, :])


def outproj_residual(x, norm_g, mods, layer, w, ys, *, g_row, mod_row, slot, tm):
    m, d = x.shape
    k = w.shape[0]
    tm = min(tm, m)
    in_specs = [
        pl.BlockSpec((tm, d), lambda i: (i, 0)),
        pl.BlockSpec((None,) + norm_g.shape[1:], lambda i: (layer, 0, 0)),
        pl.BlockSpec((None,) + mods.shape[1:], lambda i: (layer, 0, 0)),
        pl.BlockSpec((k, d), lambda i: (0, 0)),
    ] + [pl.BlockSpec((tm, k), lambda i: (i, 0))] * len(ys)
    return pl.pallas_call(
        functools.partial(_outproj_kernel, n_y=len(ys), g_row=g_row, mod_row=mod_row, slot=slot),
        grid=(m // tm,),
        in_specs=in_specs,
        out_specs=pl.BlockSpec((tm, d), lambda i: (i, 0)),
        out_shape=jax.ShapeDtypeStruct((m, d), F32),
        input_output_aliases={0: 0},
        compiler_params=_cparams("parallel"),
        name="outproj_residual",
    )(x, norm_g, mods, w, *ys)


def _dot_nt(a, b):
    return lax.dot_general(a, b, (((1,), (1,)), ((), ())), preferred_element_type=F32)


def _attn_kernel(sink_ref, q_ref, *rest, tq, seq, kv_heads, group, dh, has_local):
    if has_local:
        kp_ref, kq_ref, kn_ref, vp_ref, vq_ref, vn_ref, kc_ref, vc_ref, o_ref = rest
    else:
        kc_ref, vc_ref, o_ref = rest
    b = pl.program_id(0)
    rows = group * tq
    head_of_row = lax.broadcasted_iota(jnp.int32, (rows, 1), 0) // tq
    if has_local:
        r = lax.broadcasted_iota(jnp.int32, (rows, 3 * tq), 0) % tq
        c = lax.broadcasted_iota(jnp.int32, (rows, 3 * tq), 1)
        kpos = (b - 1) * tq + c
        rel = c - tq - r
        valid = (jnp.abs(rel) <= WINDOW) & (kpos >= 0) & (kpos < seq)
    for j in range(kv_heads):
        hs = slice(j * dh, (j + 1) * dh)
        qg = jnp.concatenate(
            [q_ref[:, (j * group + g) * dh:(j * group + g + 1) * dh] for g in range(group)], axis=0)
        sink = jnp.full((rows, 1), sink_ref[j * group], F32)
        for g in range(1, group):
            sink = jnp.where(head_of_row == g, sink_ref[j * group + g], sink)
        s_ctx = _dot_nt(qg, kc_ref[:, hs])
        m = jnp.maximum(jnp.max(s_ctx, axis=-1, keepdims=True), sink)
        if has_local:
            k_loc = jnp.concatenate([kp_ref[:, hs], kq_ref[:, hs], kn_ref[:, hs]], axis=0)
            v_loc = jnp.concatenate([vp_ref[:, hs], vq_ref[:, hs], vn_ref[:, hs]], axis=0)
            s_loc = jnp.where(valid, _dot_nt(qg, k_loc), NEG_INF)
            m = jnp.maximum(m, jnp.max(s_loc, axis=-1, keepdims=True))
        p_ctx = jnp.exp(s_ctx - m)
        den = jnp.sum(p_ctx, axis=-1, keepdims=True) + jnp.exp(sink - m)
        o = jnp.dot(p_ctx.astype(BF16), vc_ref[:, hs], preferred_element_type=F32)
        if has_local:
            p_loc = jnp.exp(s_loc - m)
            den = den + jnp.sum(p_loc, axis=-1, keepdims=True)
            o = o + jnp.dot(p_loc.astype(BF16), v_loc, preferred_element_type=F32)
        o = o / den
        for g in range(group):
            h = j * group + g
            o_ref[:, h * dh:(h + 1) * dh] = o[g * tq:(g + 1) * tq].astype(o_ref.dtype)


def attention(sink, qkv, qkv_ctx, *, has_local):
    s = qkv.shape[0]
    dh = qkv.shape[1] // (ATTN_HEADS + 2 * ATTN_KV_HEADS)
    hd = ATTN_HEADS * dh
    kvd = ATTN_KV_HEADS * dh
    kcol, vcol = hd // kvd, hd // kvd + 1
    tq = WINDOW
    nb = s // tq
    lc = qkv_ctx.shape[0]
    in_specs = [
        pl.BlockSpec(memory_space=pltpu.SMEM),
        pl.BlockSpec((tq, hd), lambda b: (b, 0)),
    ]
    args = [sink, qkv]
    if has_local:
        for col in (kcol, vcol):
            in_specs += [
                pl.BlockSpec((tq, kvd), lambda b, col=col: (jnp.maximum(b - 1, 0), col)),
                pl.BlockSpec((tq, kvd), lambda b, col=col: (b, col)),
                pl.BlockSpec((tq, kvd), lambda b, col=col: (jnp.minimum(b + 1, nb - 1), col)),
            ]
            args += [qkv] * 3
    in_specs += [pl.BlockSpec((lc, kvd), lambda b: (0, kcol)), pl.BlockSpec((lc, kvd), lambda b: (0, vcol))]
    args += [qkv_ctx, qkv_ctx]
    return pl.pallas_call(
        functools.partial(_attn_kernel, tq=tq, seq=s, kv_heads=ATTN_KV_HEADS,
                          group=ATTN_HEADS // ATTN_KV_HEADS, dh=dh, has_local=has_local),
        grid=(nb,),
        in_specs=in_specs,
        out_specs=pl.BlockSpec((tq, hd), lambda b: (b, 0)),
        out_shape=jax.ShapeDtypeStruct((s, hd), BF16),
        compiler_params=_cparams("parallel"),
        name="attention",
    )(*args)


def _ret_kernel(q_ref, k_ref, v_ref, gate_ref, intra_ref, qdec_ref, kdec_ref, cdec_ref, s0_ref,
                z_ref, sfin_ref, state_ref, *, heads, dk, dv):
    i = pl.program_id(0)

    @pl.when(i == 0)
    def _():
        state_ref[...] = s0_ref[...]

    for h in range(heads):
        q = q_ref[:, h * dk:(h + 1) * dk]
        k = k_ref[:, h * dk:(h + 1) * dk]
        v = v_ref[:, h * dv:(h + 1) * dv]
        state = state_ref[h]
        sc = _dot_nt(q, k) * intra_ref[h]
        qd = (q.astype(F32) * qdec_ref[h]).astype(BF16)
        o = (jnp.dot(sc.astype(BF16), v, preferred_element_type=F32)
             + jnp.dot(qd, state.astype(BF16), preferred_element_type=F32))
        kd = (k.astype(F32) * kdec_ref[h]).astype(BF16)
        state_ref[h] = state * cdec_ref[h] + lax.dot_general(
            kd, v, (((0,), (0,)), ((), ())), preferred_element_type=F32)
        mu = jnp.mean(o, axis=-1, keepdims=True)
        oc = o - mu
        var = jnp.mean(oc * oc, axis=-1, keepdims=True)
        gate = gate_ref[:, h * dv:(h + 1) * dv].astype(F32)
        z = gate * jax.nn.sigmoid(gate) * (oc * lax.rsqrt(var + EPS))
        z_ref[:, h * dv:(h + 1) * dv] = z.astype(z_ref.dtype)

    @pl.when(i == pl.num_programs(0) - 1)
    def _():
        sfin_ref[...] = state_ref[...]


def retention_scan(proj, tables, s0, *, reverse):
    m, n = proj.shape
    d = n // 8
    heads = RET_HEADS
    dk, dv = d // heads, 2 * d // heads
    c = min(RET_CHUNK, m)
    nc = m // c
    intra, qdec, kdec, cdec = tables

    def row(i):
        return nc - 1 - i if reverse else i

    gate_blk = 3 if reverse else 2
    full = lambda a: pl.BlockSpec(a.shape, lambda i: (0,) * a.ndim)
    z, sfin = pl.pallas_call(
        functools.partial(_ret_kernel, heads=heads, dk=dk, dv=dv),
        grid=(nc,),
        in_specs=[
            pl.BlockSpec((c, d), lambda i: (row(i), 0)),
            pl.BlockSpec((c, d), lambda i: (row(i), 1)),
            pl.BlockSpec((c, 2 * d), lambda i: (row(i), 1)),
            pl.BlockSpec((c, 2 * d), lambda i: (row(i), gate_blk)),
            full(intra), full(qdec), full(kdec), full(cdec), full(s0),
        ],
        out_specs=[
            pl.BlockSpec((c, 2 * d), lambda i: (row(i), 0)),
            full(s0),
        ],
        out_shape=[jax.ShapeDtypeStruct((m, 2 * d), BF16), jax.ShapeDtypeStruct(s0.shape, F32)],
        scratch_shapes=[pltpu.VMEM(s0.shape, F32)],
        compiler_params=_cparams("arbitrary"),
        name="retention_scan",
    )(proj, proj, proj, proj, intra, qdec, kdec, cdec, s0)
    return z, sfin


def retention_tables(log_decay_row, c, reverse):
    lg = -jnp.exp(log_decay_row.astype(F32))
    idx = jnp.arange(c, dtype=F32)
    diff = idx[:, None] - idx[None, :]
    if reverse:
        diff = -diff
    intra = jnp.where(diff >= 0, jnp.exp(lg[:, None, None] * jnp.maximum(diff, 0.0)), 0.0)
    fwd_idx = (c - 1.0 - idx) if reverse else idx
    qdec = jnp.exp(lg[:, None] * (fwd_idx + 1.0))[:, :, None]
    kdec = jnp.exp(lg[:, None] * (c - 1.0 - fwd_idx))[:, :, None]
    cdec = jnp.exp(lg * c)[:, None, None]
    return intra, qdec, kdec, cdec


def _pool_kernel(x_ref, xp_ref, xn_ref, g_ref, mod_ref, w_ref, ps_ref, o_ref, h_ref, *, mod_row, seq):
    tm, d = x_ref.shape
    halo = POOL_HALO
    i = pl.program_id(0)
    g0 = g_ref[0:1, :]
    shift = _mod_slice(mod_ref, mod_row, SH1, d)
    scale = _mod_slice(mod_ref, mod_row, SC1, d)
    hp = _norm_mod(xp_ref[...], g0, shift, scale)
    hn = _norm_mod(xn_ref[...], g0, shift, scale)
    h_ref[0:halo, :] = jnp.where(i > 0, hp, 0.0)
    h_ref[halo:halo + tm, :] = _norm_mod(x_ref[...], g0, shift, scale)
    h_ref[halo + tm:, :] = jnp.where(i < pl.num_programs(0) - 1, hn, 0.0)

    t = i * tm + lax.broadcasted_iota(jnp.int32, (tm, 1), 0)
    gw = d // len(POOL_WINDOWS)
    ys = []
    for g, w in enumerate(POOL_WINDOWS):
        cols = slice(g * gw, (g + 1) * gw)
        tot = h_ref[halo - w // 2:halo - w // 2 + tm, cols]
        for off in range(-w // 2 + 1, w // 2):
            tot = tot + h_ref[halo + off:halo + off + tm, cols]
        cnt = (jnp.minimum(t + w // 2, seq) - jnp.maximum(t - w // 2, 0)).astype(F32)
        dm = tot / cnt - h_ref[halo:halo + tm, cols]
        ys.append(jnp.dot(dm.astype(BF16), w_ref[g], preferred_element_type=F32))
    y = jnp.concatenate(ys, axis=1) * ps_ref[...]
    o_ref[...] = x_ref[...] + _mod_slice(mod_ref, mod_row, GA1, d) * _rms(y, g_ref[1:2, :])


def pool_layer(x, norm_g, mods, layer, w_pool, pool_scale, *, mod_row, tm):
    m, d = x.shape
    tm = min(tm, m)
    nt = m // tm
    hb = tm // POOL_HALO
    return pl.pallas_call(
        functools.partial(_pool_kernel, mod_row=mod_row, seq=m),
        grid=(nt,),
        in_specs=[
            pl.BlockSpec((tm, d), lambda i: (i, 0)),
            pl.BlockSpec((POOL_HALO, d), lambda i: (jnp.maximum(i * hb - 1, 0), 0)),
            pl.BlockSpec((POOL_HALO, d), lambda i: (jnp.minimum((i + 1) * hb, nt * hb - 1), 0)),
            pl.BlockSpec((None,) + norm_g.shape[1:], lambda i: (layer, 0, 0)),
            pl.BlockSpec((None,) + mods.shape[1:], lambda i: (layer, 0, 0)),
            pl.BlockSpec(w_pool.shape, lambda i: (0, 0, 0)),
            pl.BlockSpec((1, d), lambda i: (0, 0)),
        ],
        out_specs=pl.BlockSpec((tm, d), lambda i: (i, 0)),
        out_shape=jax.ShapeDtypeStruct((m, d), F32),
        scratch_shapes=[pltpu.VMEM((tm + 2 * POOL_HALO, d), F32)],
        compiler_params=_cparams("parallel"),
        name="pool_layer",
    )(x, x, x, norm_g, mods, w_pool, pool_scale.reshape(1, d))


def _swiglu_partial(h, w1, w3, w2):
    a = jnp.dot(h, w1, preferred_element_type=F32)
    b = jnp.dot(h, w3, preferred_element_type=F32)
    u = (a * jax.nn.sigmoid(a) * b).astype(BF16)
    return jnp.dot(u, w2, preferred_element_type=F32)


def _ffn_kernel(x_ref, g_ref, mod_ref, w1_ref, w3_ref, w2_ref, o_ref, h_ref, acc_ref, *, mod_row):
    d = x_ref.shape[1]
    f = pl.program_id(1)

    @pl.when(f == 0)
    def _():
        h = _norm_mod(x_ref[...], g_ref[2:3, :], _mod_slice(mod_ref, mod_row, SH2, d),
                      _mod_slice(mod_ref, mod_row, SC2, d))
        h_ref[...] = h.astype(BF16)

    part = _swiglu_partial(h_ref[...], w1_ref[...], w3_ref[...], w2_ref[...])

    @pl.when(f == 0)
    def _():
        acc_ref[...] = part

    @pl.when(f > 0)
    def _():
        acc_ref[...] += part

    @pl.when(f == pl.num_programs(1) - 1)
    def _():
        o_ref[...] = x_ref[...] + _mod_slice(mod_ref, mod_row, GA2, d) * _rms(acc_ref[...], g_ref[3:4, :])


def ffn_layer(x, norm_g, mods, layer, w1, w3, w2, *, mod_row, tm, tf):
    m, d = x.shape
    ff = w1.shape[1]
    tm = min(tm, m)
    return pl.pallas_call(
        functools.partial(_ffn_kernel, mod_row=mod_row),
        grid=(m // tm, ff // tf),
        in_specs=[
            pl.BlockSpec((tm, d), lambda i, f: (i, 0)),
            pl.BlockSpec((None,) + norm_g.shape[1:], lambda i, f: (layer, 0, 0)),
            pl.BlockSpec((None,) + mods.shape[1:], lambda i, f: (layer, 0, 0)),
            pl.BlockSpec((d, tf), lambda i, f: (0, f)),
            pl.BlockSpec((d, tf), lambda i, f: (0, f)),
            pl.BlockSpec((tf, d), lambda i, f: (f, 0)),
        ],
        out_specs=pl.BlockSpec((tm, d), lambda i, f: (i, 0)),
        out_shape=jax.ShapeDtypeStruct((m, d), F32),
        scratch_shapes=[pltpu.VMEM((tm, d), BF16), pltpu.VMEM((tm, d), F32)],
        input_output_aliases={0: 0},
        compiler_params=_cparams("parallel", "arbitrary"),
        name="ffn_layer",
    )(x, norm_g, mods, w1, w3, w2)


def _route(logits):
    e = logits.shape[1]
    lane = lax.broadcasted_iota(jnp.int32, logits.shape, 1)
    v1 = jnp.max(logits, axis=-1, keepdims=True)
    i1 = jnp.min(jnp.where(logits == v1, lane, e), axis=-1, keepdims=True)
    rest = jnp.where(lane == i1, -jnp.inf, logits)
    v2 = jnp.max(rest, axis=-1, keepdims=True)
    i2 = jnp.min(jnp.where(rest == v2, lane, e), axis=-1, keepdims=True)
    e2 = jnp.exp(v2 - v1)
    den = 1.0 + e2
    return jnp.where(lane == i1, 1.0 / den, 0.0) + jnp.where(lane == i2, e2 / den, 0.0)


def _moe_kernel(x_ref, g_ref, mod_ref, wr_ref, br_ref, w1_ref, w3_ref, w2_ref, o_ref,
                h_ref, acc_ref, cw_ref, *, mod_row):
    d = x_ref.shape[1]
    e = pl.program_id(1)
    f = pl.program_id(2)
    first = (e == 0) & (f == 0)
    last = (e == pl.num_programs(1) - 1) & (f == pl.num_programs(2) - 1)

    @pl.when(first)
    def _():
        h = _norm_mod(x_ref[...], g_ref[2:3, :], _mod_slice(mod_ref, mod_row, SH2, d),
                      _mod_slice(mod_ref, mod_row, SC2, d))
        h_ref[...] = h.astype(BF16)
        logits = jnp.dot(h, wr_ref[...], preferred_element_type=F32,
                         precision=lax.Precision.HIGHEST) + br_ref[...]
        cw = _route(logits)
        for ee in range(cw.shape[1]):
            cw_ref[ee] = cw[:, ee:ee + 1]

    part = cw_ref[e] * _swiglu_partial(h_ref[...], w1_ref[...], w3_ref[...], w2_ref[...])

    @pl.when(first)
    def _():
        acc_ref[...] = part

    @pl.when(jnp.logical_not(first))
    def _():
        acc_ref[...] += part

    @pl.when(last)
    def _():
        o_ref[...] = x_ref[...] + _mod_slice(mod_ref, mod_row, GA2, d) * _rms(acc_ref[...], g_ref[3:4, :])


def moe_layer(x, norm_g, mods, layer, w_router, b_router, w1, w3, w2, *, mod_row, tm, tf):
    m, d = x.shape
    ne, _, ff = w1.shape
    tm = min(tm, m)
    return pl.pallas_call(
        functools.partial(_moe_kernel, mod_row=mod_row),
        grid=(m // tm, ne, ff // tf),
        in_specs=[
            pl.BlockSpec((tm, d), lambda i, e, f: (i, 0)),
            pl.BlockSpec((None,) + norm_g.shape[1:], lambda i, e, f: (layer, 0, 0)),
            pl.BlockSpec((None,) + mods.shape[1:], lambda i, e, f: (layer, 0, 0)),
            pl.BlockSpec((d, ne), lambda i, e, f: (0, 0)),
            pl.BlockSpec((1, ne), lambda i, e, f: (0, 0)),
            pl.BlockSpec((None, d, tf), lambda i, e, f: (e, 0, f)),
            pl.BlockSpec((None, d, tf), lambda i, e, f: (e, 0, f)),
            pl.BlockSpec((None, tf, d), lambda i, e, f: (e, f, 0)),
        ],
        out_specs=pl.BlockSpec((tm, d), lambda i, e, f: (i, 0)),
        out_shape=jax.ShapeDtypeStruct((m, d), F32),
        scratch_shapes=[pltpu.VMEM((tm, d), BF16), pltpu.VMEM((tm, d), F32), pltpu.VMEM((ne, tm, 1), F32)],
        input_output_aliases={0: 0},
        compiler_params=_cparams("parallel", "arbitrary", "arbitrary"),
        name="moe_layer",
    )(x, norm_g, mods, w_router, b_router.reshape(1, ne), w1, w3, w2)


def _rope_tables(seq):
    n = 16
    inv = ROPE_BASE ** (-jnp.arange(n, dtype=F32) / n)
    t = jnp.arange(seq)
    row_ang = (t // GRID_W).astype(F32)[:, None] * inv[None, :]
    col_ang = (t % GRID_W).astype(F32)[:, None] * inv[None, :]
    cos = jnp.concatenate([jnp.cos(row_ang)] * 2 + [jnp.cos(col_ang)] * 2, axis=1)
    sin = jnp.concatenate([-jnp.sin(row_ang), jnp.sin(row_ang), -jnp.sin(col_ang), jnp.sin(col_ang)], axis=1)
    return jnp.tile(cos, (1, 2)), jnp.tile(sin, (1, 2))


def kernel(x, c, ctx, c_ctx, w_ada, b_ada, norm_g, attn_w_qkv, attn_w_o, attn_sink, ret_w_in, ret_w_o,
           ret_log_decay, pool_w, pool_scale, ffn_w1, ffn_w3, ffn_w2, moe_w_router, moe_b_router,
           moe_w1, moe_w3, moe_w2):
    batch, seq, d = x.shape
    assert batch == 1 and c.shape[0] == 1
    depth = w_ada.shape[0]
    lc = ctx.shape[1]
    xl = x.reshape(seq, d)
    xc = ctx.reshape(lc, d)

    cvecs = jnp.zeros((8, d), F32).at[LAT_ROW].set(c[0]).at[CTX_ROW].set(c_ctx)
    mods = ada_table(cvecs, w_ada, b_ada)
    rope = _rope_tables(seq)

    hd = attn_w_o.shape[1]
    dh = hd // ATTN_HEADS
    qkv_scale = jnp.concatenate([jnp.full((hd,), dh ** -0.5, F32),
                                 jnp.ones((attn_w_qkv.shape[2] - hd,), F32)])
    dk = d // RET_HEADS
    ret_scale = jnp.concatenate([jnp.ones((d,), F32), jnp.full((d,), dk ** -0.5, F32),
                                 jnp.ones((ret_w_in.shape[2] - 2 * d,), F32)])

    for i in range(depth):
        last = i == depth - 1
        kind, j = i % N_MIXERS, i // N_MIXERS
        proj = functools.partial(norm_mod_matmul, norm_g=norm_g, mods=mods, layer=i, g_row=0, slot=SH1)
        if kind == 0:
            w_qkv = (attn_w_qkv[j] * qkv_scale).astype(BF16)
            w_o = attn_w_o[j].astype(BF16)
            kvd = (w_qkv.shape[1] - hd) // 2
            qkv_l = proj(xl, w=w_qkv, mod_row=LAT_ROW, tm=1024, tn=256, rope=rope, rope_tiles=(hd + kvd) // 256)
            qkv_c = proj(xc, w=w_qkv, mod_row=CTX_ROW, tm=256, tn=256)
            o_l = attention(attn_sink[j], qkv_l, qkv_c, has_local=True)
            xl = outproj_residual(xl, norm_g, mods, i, w_o, [o_l], g_row=1, mod_row=LAT_ROW, slot=GA1, tm=512)
            if not last:
                o_c = attention(attn_sink[j], qkv_c, qkv_c, has_local=False)
                xc = outproj_residual(xc, norm_g, mods, i, w_o, [o_c], g_row=1, mod_row=CTX_ROW, slot=GA1, tm=256)
        elif kind == 1:
            w_in = (ret_w_in[j] * ret_scale).astype(BF16)
            w_o = ret_w_o[j].astype(BF16)
            p_c = proj(xc, w=w_in, mod_row=CTX_ROW, tm=256, tn=1024)
            p_l = proj(xl, w=w_in, mod_row=LAT_ROW, tm=1024, tn=1024)
            s0 = jnp.zeros((RET_HEADS, dk, 2 * dk), F32)
            tabs_f = retention_tables(ret_log_decay[j, 0], min(RET_CHUNK, lc), False)
            tabs_b = retention_tables(ret_log_decay[j, 1], min(RET_CHUNK, lc), True)
            zf_c, s_f = retention_scan(p_c, tabs_f, s0, reverse=False)
            zb_c, s_b = retention_scan(p_c, tabs_b, s0, reverse=True)
            zf_l, _ = retention_scan(p_l, tabs_f, s_f, reverse=False)
            zb_l, _ = retention_scan(p_l, tabs_b, s_b, reverse=True)
            xl = outproj_residual(xl, norm_g, mods, i, w_o, [zf_l, zb_l], g_row=1, mod_row=LAT_ROW, slot=GA1, tm=512)
            if not last:
                xc = outproj_residual(xc, norm_g, mods, i, w_o, [zf_c, zb_c], g_row=1, mod_row=CTX_ROW, slot=GA1, tm=256)
        else:
            w_p = pool_w[j].astype(BF16)
            xl = pool_layer(xl, norm_g, mods, i, w_p, pool_scale[j], mod_row=LAT_ROW, tm=512)
            if not last:
                xc = pool_layer(xc, norm_g, mods, i, w_p, pool_scale[j], mod_row=CTX_ROW, tm=256)

        f = i // 2
        if i % 2 == 0:
            w1, w3, w2 = ffn_w1[f].astype(BF16), ffn_w3[f].astype(BF16), ffn_w2[f].astype(BF16)
            tf = w1.shape[1] // 2
            xl = ffn_layer(xl, norm_g, mods, i, w1, w3, w2, mod_row=LAT_ROW, tm=512, tf=tf)
            if not last:
                xc = ffn_layer(xc, norm_g, mods, i, w1, w3, w2, mod_row=CTX_ROW, tm=256, tf=tf)
        else:
            w1, w3, w2 = moe_w1[f].astype(BF16), moe_w3[f].astype(BF16), moe_w2[f].astype(BF16)
            tf = w1.shape[2] // 2
            xl = moe_layer(xl, norm_g, mods, i, moe_w_router[f], moe_b_router[f], w1, w3, w2,
                           mod_row=LAT_ROW, tm=512, tf=tf)
            if not last:
                xc = moe_layer(xc, norm_g, mods, i, moe_w_router[f], moe_b_router[f], w1, w3, w2,
                               mod_row=CTX_ROW, tm=256, tf=tf)
    return xl.reshape(batch, seq, d)
```

```python
import functools

import jax
import jax.numpy as jnp
from jax import lax
from jax.experimental import pallas as pl
from jax.experimental.pallas import tpu as pltpu

F32 = jnp.float32
BF16 = jnp.bfloat16

EPS = 1e-6
NEG_INF = -1e30
LANES = 128
VMEM_LIMIT = 56 * 1024 * 1024

GRID_W = 64
N_MIXERS = 3
ATTN_HEADS = 16
ATTN_KV_HEADS = 4
WINDOW = 128
ROPE_BASE = 10000.0
RET_HEADS = 4
RET_CHUNK = 256
POOL_WINDOWS = (2, 4, 8, 16)
POOL_HALO = 8
N_EXPERTS = 8

SH1, SC1, GA1, SH2, SC2, GA2 = range(6)
LAT_ROW, CTX_ROW = 0, 1


def _cparams(*sem):
    return pltpu.CompilerParams(dimension_semantics=sem, vmem_limit_bytes=VMEM_LIMIT)


def _rms(x, g):
    return x * lax.rsqrt(jnp.mean(x * x, axis=-1, keepdims=True) + EPS) * g


def _mod_slice(mod_ref, row, slot, d):
    return mod_ref[row:row + 1, slot * d:(slot + 1) * d]


def _norm_mod(x, g, shift, scale):
    return _rms(x, g) * (1.0 + scale) + shift


def _ada_kernel(c_ref, w_ref, b_ref, o_ref):
    c = c_ref[...]
    s = c * jax.nn.sigmoid(c)
    o_ref[...] = jnp.dot(s, w_ref[...], preferred_element_type=F32,
                         precision=lax.Precision.HIGHEST) + b_ref[...]


def ada_table(cvecs, w_ada, b_ada):
    depth, d, n = w_ada.shape
    tn = 1536
    return pl.pallas_call(
        _ada_kernel,
        grid=(depth, n // tn),
        in_specs=[
            pl.BlockSpec((8, d), lambda i, j: (0, 0)),
            pl.BlockSpec((None, d, tn), lambda i, j: (i, 0, j)),
            pl.BlockSpec((None, 1, tn), lambda i, j: (i, 0, j)),
        ],
        out_specs=pl.BlockSpec((None, 8, tn), lambda i, j: (i, 0, j)),
        out_shape=jax.ShapeDtypeStruct((depth, 8, n), F32),
        compiler_params=_cparams("parallel", "parallel"),
        name="ada_table",
    )(cvecs, w_ada, b_ada.reshape(depth, 1, n))


def _rope(a, cos, sin):
    lane = lax.broadcasted_iota(jnp.int32, (a.shape[0], LANES), 1)
    first = (lane % 32) < 16
    outs = []
    for cb in range(a.shape[1] // LANES):
        blk = a[:, cb * LANES:(cb + 1) * LANES]
        partner = jnp.where(first, pltpu.roll(blk, LANES - 16, 1), pltpu.roll(blk, 16, 1))
        outs.append(blk * cos + partner * sin)
    return jnp.concatenate(outs, axis=1)


def _proj_kernel(x_ref, g_ref, mod_ref, w_ref, *rest, g_row, mod_row, slot, rope_tiles):
    if rope_tiles:
        cos_ref, sin_ref, o_ref, h_ref = rest
    else:
        o_ref, h_ref = rest
    d = x_ref.shape[1]
    j = pl.program_id(1)

    @pl.when(j == 0)
    def _():
        h = _norm_mod(x_ref[...], g_ref[g_row:g_row + 1, :],
                      _mod_slice(mod_ref, mod_row, slot, d), _mod_slice(mod_ref, mod_row, slot + 1, d))
        h_ref[...] = h.astype(BF16)

    acc = jnp.dot(h_ref[...], w_ref[...], preferred_element_type=F32)
    if rope_tiles:
        @pl.when(j < rope_tiles)
        def _():
            o_ref[...] = _rope(acc, cos_ref[...], sin_ref[...]).astype(o_ref.dtype)

        @pl.when(j >= rope_tiles)
        def _():
            o_ref[...] = acc.astype(o_ref.dtype)
    else:
        o_ref[...] = acc.astype(o_ref.dtype)


def norm_mod_matmul(x, norm_g, mods, layer, w, *, g_row, mod_row, slot, tm, tn, rope=None, rope_tiles=0):
    m, d = x.shape
    n = w.shape[1]
    tm = min(tm, m)
    in_specs = [
        pl.BlockSpec((tm, d), lambda i, j: (i, 0)),
        pl.BlockSpec((None,) + norm_g.shape[1:], lambda i, j: (layer, 0, 0)),
        pl.BlockSpec((None,) + mods.shape[1:], lambda i, j: (layer, 0, 0)),
        pl.BlockSpec((d, tn), lambda i, j: (0, j)),
    ]
    args = [x, norm_g, mods, w]
    if rope_tiles:
        in_specs += [pl.BlockSpec((tm, LANES), lambda i, j: (i, 0))] * 2
        args += list(rope)
    return pl.pallas_call(
        functools.partial(_proj_kernel, g_row=g_row, mod_row=mod_row, slot=slot, rope_tiles=rope_tiles),
        grid=(m // tm, n // tn),
        in_specs=in_specs,
        out_specs=pl.BlockSpec((tm, tn), lambda i, j: (i, j)),
        out_shape=jax.ShapeDtypeStruct((m, n), BF16),
        scratch_shapes=[pltpu.VMEM((tm, d), BF16)],
        compiler_params=_cparams("parallel", "arbitrary"),
        name="norm_mod_matmul",
    )(*args)


def _outproj_kernel(x_ref, g_ref, mod_ref, w_ref, *rest, n_y, g_row, mod_row, slot):
    y_refs, o_ref = rest[:n_y], rest[n_y]
    d = x_ref.shape[1]
    y = y_refs[0][...]
    if n_y == 2:
        y = (y.astype(F32) + y_refs[1][...].astype(F32)).astype(BF16)
    t = jnp.dot(y, w_ref[...], preferred_element_type=F32)
    gate = _mod_slice(mod_ref, mod_row, slot, d)
    o_ref[...] = x_ref[...] + gate * _rms(t, g_ref[g_row:g_row + 1, :])


def outproj_residual(x, norm_g, mods, layer, w, ys, *, g_row, mod_row, slot, tm):
    m, d = x.shape
    k = w.shape[0]
    tm = min(tm, m)
    in_specs = [
        pl.BlockSpec((tm, d), lambda i: (i, 0)),
        pl.BlockSpec((None,) + norm_g.shape[1:], lambda i: (layer, 0, 0)),
        pl.BlockSpec((None,) + mods.shape[1:], lambda i: (layer, 0, 0)),
        pl.BlockSpec((k, d), lambda i: (0, 0)),
    ] + [pl.BlockSpec((tm, k), lambda i: (i, 0))] * len(ys)
    return pl.pallas_call(
        functools.partial(_outproj_kernel, n_y=len(ys), g_row=g_row, mod_row=mod_row, slot=slot),
        grid=(m // tm,),
        in_specs=in_specs,
        out_specs=pl.BlockSpec((tm, d), lambda i: (i, 0)),
        out_shape=jax.ShapeDtypeStruct((m, d), F32),
        input_output_aliases={0: 0},
        compiler_params=_cparams("parallel"),
        name="outproj_residual",
    )(x, norm_g, mods, w, *ys)


def _dot_nt(a, b):
    return lax.dot_general(a, b, (((1,), (1,)), ((), ())), preferred_element_type=F32)


def _attn_kernel(sink_ref, q_ref, *rest, tq, seq, kv_heads, group, dh, has_local):
    if has_local:
        kp_ref, kq_ref, kn_ref, vp_ref, vq_ref, vn_ref, kc_ref, vc_ref, o_ref = rest
    else:
        kc_ref, vc_ref, o_ref = rest
    b = pl.program_id(0)
    rows = group * tq
    head_of_row = lax.broadcasted_iota(jnp.int32, (rows, 1), 0) // tq
    if has_local:
        r = lax.broadcasted_iota(jnp.int32, (rows, 3 * tq), 0) % tq
        c = lax.broadcasted_iota(jnp.int32, (rows, 3 * tq), 1)
        kpos = (b - 1) * tq + c
        rel = c - tq - r
        valid = (jnp.abs(rel) <= WINDOW) & (kpos >= 0) & (kpos < seq)
    for j in range(kv_heads):
        hs = slice(j * dh, (j + 1) * dh)
        qg = jnp.concatenate(
            [q_ref[:, (j * group + g) * dh:(j * group + g + 1) * dh] for g in range(group)], axis=0)
        sink = jnp.full((rows, 1), sink_ref[j * group], F32)
        for g in range(1, group):
            sink = jnp.where(head_of_row == g, sink_ref[j * group + g], sink)
        s_ctx = _dot_nt(qg, kc_ref[:, hs])
        m = jnp.maximum(jnp.max(s_ctx, axis=-1, keepdims=True), sink)
        if has_local:
            k_loc = jnp.concatenate([kp_ref[:, hs], kq_ref[:, hs], kn_ref[:, hs]], axis=0)
            v_loc = jnp.concatenate([vp_ref[:, hs], vq_ref[:, hs], vn_ref[:, hs]], axis=0)
            s_loc = jnp.where(valid, _dot_nt(qg, k_loc), NEG_INF)
            m = jnp.maximum(m, jnp.max(s_loc, axis=-1, keepdims=True))
        p_ctx = jnp.exp(s_ctx - m)
        den = jnp.sum(p_ctx, axis=-1, keepdims=True) + jnp.exp(sink - m)
        o = jnp.dot(p_ctx.astype(BF16), vc_ref[:, hs], preferred_element_type=F32)
        if has_local:
            p_loc = jnp.exp(s_loc - m)
            den = den + jnp.sum(p_loc, axis=-1, keepdims=True)
            o = o + jnp.dot(p_loc.astype(BF16), v_loc, preferred_element_type=F32)
        o = o / den
        for g in range(group):
            h = j * group + g
            o_ref[:, h * dh:(h + 1) * dh] = o[g * tq:(g + 1) * tq].astype(o_ref.dtype)


def attention(sink, qkv, qkv_ctx, *, has_local):
    s = qkv.shape[0]
    dh = qkv.shape[1] // (ATTN_HEADS + 2 * ATTN_KV_HEADS)
    hd = ATTN_HEADS * dh
    kvd = ATTN_KV_HEADS * dh
    kcol, vcol = hd // kvd, hd // kvd + 1
    tq = WINDOW
    nb = s // tq
    lc = qkv_ctx.shape[0]
    in_specs = [
        pl.BlockSpec(memory_space=pltpu.SMEM),
        pl.BlockSpec((tq, hd), lambda b: (b, 0)),
    ]
    args = [sink, qkv]
    if has_local:
        for col in (kcol, vcol):
            in_specs += [
                pl.BlockSpec((tq, kvd), lambda b, col=col: (jnp.maximum(b - 1, 0), col)),
                pl.BlockSpec((tq, kvd), lambda b, col=col: (b, col)),
                pl.BlockSpec((tq, kvd), lambda b, col=col: (jnp.minimum(b + 1, nb - 1), col)),
            ]
            args += [qkv] * 3
    in_specs += [pl.BlockSpec((lc, kvd), lambda b: (0, kcol)), pl.BlockSpec((lc, kvd), lambda b: (0, vcol))]
    args += [qkv_ctx, qkv_ctx]
    return pl.pallas_call(
        functools.partial(_attn_kernel, tq=tq, seq=s, kv_heads=ATTN_KV_HEADS,
                          group=ATTN_HEADS // ATTN_KV_HEADS, dh=dh, has_local=has_local),
        grid=(nb,),
        in_specs=in_specs,
        out_specs=pl.BlockSpec((tq, hd), lambda b: (b, 0)),
        out_shape=jax.ShapeDtypeStruct((s, hd), BF16),
        compiler_params=_cparams("parallel"),
        name="attention",
    )(*args)


def _ret_kernel(q_ref, k_ref, v_ref, gate_ref, intra_ref, qdec_ref, kdec_ref, cdec_ref, s0_ref,
                z_ref, sfin_ref, state_ref, *, heads, dk, dv):
    i = pl.program_id(0)

    @pl.when(i == 0)
    def _():
        state_ref[...] = s0_ref[...]

    for h in range(heads):
        q = q_ref[:, h * dk:(h + 1) * dk]
        k = k_ref[:, h * dk:(h + 1) * dk]
        v = v_ref[:, h * dv:(h + 1) * dv]
        state = state_ref[h]
        sc = _dot_nt(q, k) * intra_ref[h]
        qd = (q.astype(F32) * qdec_ref[h]).astype(BF16)
        o = (jnp.dot(sc.astype(BF16), v, preferred_element_type=F32)
             + jnp.dot(qd, state.astype(BF16), preferred_element_type=F32))
        kd = (k.astype(F32) * kdec_ref[h]).astype(BF16)
        state_ref[h] = state * cdec_ref[h] + lax.dot_general(
            kd, v, (((0,), (0,)), ((), ())), preferred_element_type=F32)
        mu = jnp.mean(o, axis=-1, keepdims=True)
        oc = o - mu
        var = jnp.mean(oc * oc, axis=-1, keepdims=True)
        gate = gate_ref[:, h * dv:(h + 1) * dv].astype(F32)
        z = gate * jax.nn.sigmoid(gate) * (oc * lax.rsqrt(var + EPS))
        z_ref[:, h * dv:(h + 1) * dv] = z.astype(z_ref.dtype)

    @pl.when(i == pl.num_programs(0) - 1)
    def _():
        sfin_ref[...] = state_ref[...]


def retention_scan(proj, tables, s0, *, reverse):
    m, n = proj.shape
    d = n // 8
    heads = RET_HEADS
    dk, dv = d // heads, 2 * d // heads
    c = min(RET_CHUNK, m)
    nc = m // c
    intra, qdec, kdec, cdec = tables

    def row(i):
        return nc - 1 - i if reverse else i

    gate_blk = 3 if reverse else 2
    full = lambda a: pl.BlockSpec(a.shape, lambda i: (0,) * a.ndim)
    z, sfin = pl.pallas_call(
        functools.partial(_ret_kernel, heads=heads, dk=dk, dv=dv),
        grid=(nc,),
        in_specs=[
            pl.BlockSpec((c, d), lambda i: (row(i), 0)),
            pl.BlockSpec((c, d), lambda i: (row(i), 1)),
            pl.BlockSpec((c, 2 * d), lambda i: (row(i), 1)),
            pl.BlockSpec((c, 2 * d), lambda i: (row(i), gate_blk)),
            full(intra), full(qdec), full(kdec), full(cdec), full(s0),
        ],
        out_specs=[
            pl.BlockSpec((c, 2 * d), lambda i: (row(i), 0)),
            full(s0),
        ],
        out_shape=[jax.ShapeDtypeStruct((m, 2 * d), BF16), jax.ShapeDtypeStruct(s0.shape, F32)],
        scratch_shapes=[pltpu.VMEM(s0.shape, F32)],
        compiler_params=_cparams("arbitrary"),
        name="retention_scan",
    )(proj, proj, proj, proj, intra, qdec, kdec, cdec, s0)
    return z, sfin


def retention_tables(log_decay_row, c, reverse):
    lg = -jnp.exp(log_decay_row.astype(F32))
    idx = jnp.arange(c, dtype=F32)
    diff = idx[:, None] - idx[None, :]
    if reverse:
        diff = -diff
    intra = jnp.where(diff >= 0, jnp.exp(lg[:, None, None] * jnp.maximum(diff, 0.0)), 0.0)
    fwd_idx = (c - 1.0 - idx) if reverse else idx
    qdec = jnp.exp(lg[:, None] * (fwd_idx + 1.0))[:, :, None]
    kdec = jnp.exp(lg[:, None] * (c - 1.0 - fwd_idx))[:, :, None]
    cdec = jnp.exp(lg * c)[:, None, None]
    return intra, qdec, kdec, cdec


def _pool_kernel(x_ref, xp_ref, xn_ref, g_ref, mod_ref, w_ref, ps_ref, o_ref, h_ref, *, mod_row, seq):
    tm, d = x_ref.shape
    halo = POOL_HALO
    i = pl.program_id(0)
    g0 = g_ref[0:1, :]
    shift = _mod_slice(mod_ref, mod_row, SH1, d)
    scale = _mod_slice(mod_ref, mod_row, SC1, d)
    hp = _norm_mod(xp_ref[...], g0, shift, scale)
    hn = _norm_mod(xn_ref[...], g0, shift, scale)
    h_ref[0:halo, :] = jnp.where(i > 0, hp, 0.0)
    h_ref[halo:halo + tm, :] = _norm_mod(x_ref[...], g0, shift, scale)
    h_ref[halo + tm:, :] = jnp.where(i < pl.num_programs(0) - 1, hn, 0.0)

    t = i * tm + lax.broadcasted_iota(jnp.int32, (tm, 1), 0)
    gw = d // len(POOL_WINDOWS)
    ys = []
    for g, w in enumerate(POOL_WINDOWS):
        cols = slice(g * gw, (g + 1) * gw)
        tot = h_ref[halo - w // 2:halo - w // 2 + tm, cols]
        for off in range(-w // 2 + 1, w // 2):
            tot = tot + h_ref[halo + off:halo + off + tm, cols]
        cnt = (jnp.minimum(t + w // 2, seq) - jnp.maximum(t - w // 2, 0)).astype(F32)
        dm = tot / cnt - h_ref[halo:halo + tm, cols]
        ys.append(jnp.dot(dm.astype(BF16), w_ref[g], preferred_element_type=F32))
    y = jnp.concatenate(ys, axis=1) * ps_ref[...]
    o_ref[...] = x_ref[...] + _mod_slice(mod_ref, mod_row, GA1, d) * _rms(y, g_ref[1:2, :])


def pool_layer(x, norm_g, mods, layer, w_pool, pool_scale, *, mod_row, tm):
    m, d = x.shape
    tm = min(tm, m)
    nt = m // tm
    hb = tm // POOL_HALO
    return pl.pallas_call(
        functools.partial(_pool_kernel, mod_row=mod_row, seq=m),
        grid=(nt,),
        in_specs=[
            pl.BlockSpec((tm, d), lambda i: (i, 0)),
            pl.BlockSpec((POOL_HALO, d), lambda i: (jnp.maximum(i * hb - 1, 0), 0)),
            pl.BlockSpec((POOL_HALO, d), lambda i: (jnp.minimum((i + 1) * hb, nt * hb - 1), 0)),
            pl.BlockSpec((None,) + norm_g.shape[1:], lambda i: (layer, 0, 0)),
            pl.BlockSpec((None,) + mods.shape[1:], lambda i: (layer, 0, 0)),
            pl.BlockSpec(w_pool.shape, lambda i: (0, 0, 0)),
            pl.BlockSpec((1, d), lambda i: (0, 0)),
        ],
        out_specs=pl.BlockSpec((tm, d), lambda i: (i, 0)),
        out_shape=jax.ShapeDtypeStruct((m, d), F32),
        scratch_shapes=[pltpu.VMEM((tm + 2 * POOL_HALO, d), F32)],
        compiler_params=_cparams("parallel"),
        name="pool_layer",
    )(x, x, x, norm_g, mods, w_pool, pool_scale.reshape(1, d))


def _swiglu_partial(h, w1, w3, w2):
    a = jnp.dot(h, w1, preferred_element_type=F32)
    b = jnp.dot(h, w3, preferred_element_type=F32)
    u = (a * jax.nn.sigmoid(a) * b).astype(BF16)
    return jnp.dot(u, w2, preferred_element_type=F32)


def _ffn_kernel(x_ref, g_ref, mod_ref, w1_ref, w3_ref, w2_ref, o_ref, h_ref, acc_ref, *, mod_row):
    d = x_ref.shape[1]
    f = pl.program_id(1)

    @pl.when(f == 0)
    def _():
        h = _norm_mod(x_ref[...], g_ref[2:3, :], _mod_slice(mod_ref, mod_row, SH2, d),
                      _mod_slice(mod_ref, mod_row, SC2, d))
        h_ref[...] = h.astype(BF16)

    part = _swiglu_partial(h_ref[...], w1_ref[...], w3_ref[...], w2_ref[...])

    @pl.when(f == 0)
    def _():
        acc_ref[...] = part

    @pl.when(f > 0)
    def _():
        acc_ref[...] += part

    @pl.when(f == pl.num_programs(1) - 1)
    def _():
        o_ref[...] = x_ref[...] + _mod_slice(mod_ref, mod_row, GA2, d) * _rms(acc_ref[...], g_ref[3:4, :])


def ffn_layer(x, norm_g, mods, layer, w1, w3, w2, *, mod_row, tm, tf):
    m, d = x.shape
    ff = w1.shape[1]
    tm = min(tm, m)
    return pl.pallas_call(
        functools.partial(_ffn_kernel, mod_row=mod_row),
        grid=(m // tm, ff // tf),
        in_specs=[
            pl.BlockSpec((tm, d), lambda i, f: (i, 0)),
            pl.BlockSpec((None,) + norm_g.shape[1:], lambda i, f: (layer, 0, 0)),
            pl.BlockSpec((None,) + mods.shape[1:], lambda i, f: (layer, 0, 0)),
            pl.BlockSpec((d, tf), lambda i, f: (0, f)),
            pl.BlockSpec((d, tf), lambda i, f: (0, f)),
            pl.BlockSpec((tf, d), lambda i, f: (f, 0)),
        ],
        out_specs=pl.BlockSpec((tm, d), lambda i, f: (i, 0)),
        out_shape=jax.ShapeDtypeStruct((m, d), F32),
        scratch_shapes=[pltpu.VMEM((tm, d), BF16), pltpu.VMEM((tm, d), F32)],
        input_output_aliases={0: 0},
        compiler_params=_cparams("parallel", "arbitrary"),
        name="ffn_layer",
    )(x, norm_g, mods, w1, w3, w2)


def _route(logits):
    e = logits.shape[1]
    lane = lax.broadcasted_iota(jnp.int32, logits.shape, 1)
    v1 = jnp.max(logits, axis=-1, keepdims=True)
    i1 = jnp.min(jnp.where(logits == v1, lane, e), axis=-1, keepdims=True)
    rest = jnp.where(lane == i1, -jnp.inf, logits)
    v2 = jnp.max(rest, axis=-1, keepdims=True)
    i2 = jnp.min(jnp.where(rest == v2, lane, e), axis=-1, keepdims=True)
    e2 = jnp.exp(v2 - v1)
    den = 1.0 + e2
    return jnp.where(lane == i1, 1.0 / den, 0.0) + jnp.where(lane == i2, e2 / den, 0.0)


def _moe_kernel(x_ref, g_ref, mod_ref, wr_ref, br_ref, w1_ref, w3_ref, w2_ref, o_ref,
                h_ref, acc_ref, cw_ref, *, mod_row):
    d = x_ref.shape[1]
    e = pl.program_id(1)
    f = pl.program_id(2)
    first = (e == 0) & (f == 0)
    last = (e == pl.num_programs(1) - 1) & (f == pl.num_programs(2) - 1)

    @pl.when(first)
    def _():
        h = _norm_mod(x_ref[...], g_ref[2:3, :], _mod_slice(mod_ref, mod_row, SH2, d),
                      _mod_slice(mod_ref, mod_row, SC2, d))
        h_ref[...] = h.astype(BF16)
        logits = jnp.dot(h, wr_ref[...], preferred_element_type=F32,
                         precision=lax.Precision.HIGHEST) + br_ref[...]
        cw = _route(logits)
        for ee in range(cw.shape[1]):
            cw_ref[ee] = cw[:, ee:ee + 1]

    part = cw_ref[e] * _swiglu_partial(h_ref[...], w1_ref[...], w3_ref[...], w2_ref[...])

    @pl.when(first)
    def _():
        acc_ref[...] = part

    @pl.when(jnp.logical_not(first))
    def _():
        acc_ref[...] += part

    @pl.when(last)
    def _():
        o_ref[...] = x_ref[...] + _mod_slice(mod_ref, mod_row, GA2, d) * _rms(acc_ref[...], g_ref[3:4, :])


def moe_layer(x, norm_g, mods, layer, w_router, b_router, w1, w3, w2, *, mod_row, tm, tf):
    m, d = x.shape
    ne, _, ff = w1.shape
    tm = min(tm, m)
    return pl.pallas_call(
        functools.partial(_moe_kernel, mod_row=mod_row),
        grid=(m // tm, ne, ff // tf),
        in_specs=[
            pl.BlockSpec((tm, d), lambda i, e, f: (i, 0)),
            pl.BlockSpec((None,) + norm_g.shape[1:], lambda i, e, f: (layer, 0, 0)),
            pl.BlockSpec((None,) + mods.shape[1:], lambda i, e, f: (layer, 0, 0)),
            pl.BlockSpec((d, ne), lambda i, e, f: (0, 0)),
            pl.BlockSpec((1, ne), lambda i, e, f: (0, 0)),
            pl.BlockSpec((None, d, tf), lambda i, e, f: (e, 0, f)),
            pl.BlockSpec((None, d, tf), lambda i, e, f: (e, 0, f)),
            pl.BlockSpec((None, tf, d), lambda i, e, f: (e, f, 0)),
        ],
        out_specs=pl.BlockSpec((tm, d), lambda i, e, f: (i, 0)),
        out_shape=jax.ShapeDtypeStruct((m, d), F32),
        scratch_shapes=[pltpu.VMEM((tm, d), BF16), pltpu.VMEM((tm, d), F32), pltpu.VMEM((ne, tm, 1), F32)],
        input_output_aliases={0: 0},
        compiler_params=_cparams("parallel", "arbitrary", "arbitrary"),
        name="moe_layer",
    )(x, norm_g, mods, w_router, b_router.reshape(1, ne), w1, w3, w2)


def _to_slabs(dst_ref, val):
    for k in range(val.shape[1] // LANES):
        dst_ref[:, k, :] = val[:, k * LANES:(k + 1) * LANES].astype(dst_ref.dtype)


def _from_slabs(src_ref):
    return jnp.concatenate([src_ref[:, k, :] for k in range(src_ref.shape[1])], axis=1)


def _router_kernel(x_ref, g_ref, mod_ref, wr_ref, br_ref, h3_ref, ri_ref, rw_ref, cnt_ref, run_ref, *, mod_row):
    tm, d = x_ref.shape
    ne = wr_ref.shape[1]
    i = pl.program_id(0)

    @pl.when(i == 0)
    def _():
        run_ref[...] = jnp.zeros_like(run_ref)

    h = _norm_mod(x_ref[...], g_ref[2:3, :], _mod_slice(mod_ref, mod_row, SH2, d),
                  _mod_slice(mod_ref, mod_row, SC2, d))
    _to_slabs(h3_ref, h)
    logits = jnp.dot(h, wr_ref[...], preferred_element_type=F32,
                     precision=lax.Precision.HIGHEST) + br_ref[...]
    lane = lax.broadcasted_iota(jnp.int32, logits.shape, 1)
    v1 = jnp.max(logits, axis=-1, keepdims=True)
    i1 = jnp.min(jnp.where(logits == v1, lane, ne), axis=-1, keepdims=True)
    rest = jnp.where(lane == i1, -jnp.inf, logits)
    v2 = jnp.max(rest, axis=-1, keepdims=True)
    i2 = jnp.min(jnp.where(rest == v2, lane, ne), axis=-1, keepdims=True)
    e2 = jnp.exp(v2 - v1)
    den = 1.0 + e2
    oh1 = (lane == i1).astype(F32)
    oh2 = (lane == i2).astype(F32)
    both = oh1 + oh2
    r = lax.broadcasted_iota(jnp.int32, (tm, tm), 0)
    c = lax.broadcasted_iota(jnp.int32, (tm, tm), 1)
    tri = jnp.where(c < r, 1.0, 0.0).astype(BF16)
    rank_all = jnp.dot(tri, both.astype(BF16), preferred_element_type=F32) + run_ref[...]
    rank1 = jnp.sum(rank_all * oh1, axis=-1, keepdims=True).astype(jnp.int32)
    rank2 = jnp.sum(rank_all * oh2, axis=-1, keepdims=True).astype(jnp.int32)
    run_ref[...] += jnp.sum(both, axis=0, keepdims=True)
    col = lax.broadcasted_iota(jnp.int32, ri_ref.shape, 1)
    ri_ref[...] = jnp.where(col == 0, i1, jnp.where(col == 1, i2, jnp.where(col == 2, rank1, rank2)))
    colw = lax.broadcasted_iota(jnp.int32, rw_ref.shape, 1)
    rw_ref[...] = jnp.where(colw == 0, 1.0 / den, e2 / den)
    cnt_ref[...] = run_ref[...].astype(jnp.int32)


def moe_router(x, norm_g, mods, layer, w_router, b_router, *, mod_row, tm):
    m, d = x.shape
    ne = w_router.shape[1]
    return pl.pallas_call(
        functools.partial(_router_kernel, mod_row=mod_row),
        grid=(m // tm,),
        in_specs=[
            pl.BlockSpec((tm, d), lambda i: (i, 0)),
            pl.BlockSpec((None,) + norm_g.shape[1:], lambda i: (layer, 0, 0)),
            pl.BlockSpec((None,) + mods.shape[1:], lambda i: (layer, 0, 0)),
            pl.BlockSpec((d, ne), lambda i: (0, 0)),
            pl.BlockSpec((1, ne), lambda i: (0, 0)),
        ],
        out_specs=[
            pl.BlockSpec((tm, d // LANES, LANES), lambda i: (i, 0, 0)),
            pl.BlockSpec((tm, 4), lambda i: (i, 0)),
            pl.BlockSpec((tm, 2), lambda i: (i, 0)),
            pl.BlockSpec((1, ne), lambda i: (0, 0)),
        ],
        out_shape=[
            jax.ShapeDtypeStruct((m, d // LANES, LANES), F32),
            jax.ShapeDtypeStruct((m, 4), jnp.int32),
            jax.ShapeDtypeStruct((m, 2), F32),
            jax.ShapeDtypeStruct((1, ne), jnp.int32),
        ],
        scratch_shapes=[pltpu.VMEM((1, ne), F32)],
        compiler_params=_cparams("arbitrary"),
        name="moe_router",
    )(x, norm_g, mods, w_router, b_router.reshape(1, ne))


def _dispatch_kernel(pos1_ref, pos2_ref, h3_ref, xs_in_ref, xs_ref, sem):
    del xs_in_ref
    tm = h3_ref.shape[0]
    base = pl.program_id(0) * tm

    def issue(t, carry):
        pltpu.make_async_copy(h3_ref.at[t], xs_ref.at[pos1_ref[base + t]], sem).start()
        pltpu.make_async_copy(h3_ref.at[t], xs_ref.at[pos2_ref[base + t]], sem).start()
        return carry

    lax.fori_loop(0, tm, issue, 0, unroll=8)
    pltpu.make_async_copy(xs_ref.at[pl.ds(0, 2 * tm)], xs_ref.at[pl.ds(0, 2 * tm)], sem).wait()


def moe_dispatch(h3, pos1, pos2, rows, *, tm):
    m, s, l = h3.shape
    xs0 = jnp.zeros((rows, s, l), F32)
    return pl.pallas_call(
        _dispatch_kernel,
        grid_spec=pltpu.PrefetchScalarGridSpec(
            num_scalar_prefetch=2,
            grid=(m // tm,),
            in_specs=[
                pl.BlockSpec((tm, s, l), lambda i, p1, p2: (i, 0, 0)),
                pl.BlockSpec(memory_space=pl.ANY),
            ],
            out_specs=pl.BlockSpec(memory_space=pl.ANY),
            scratch_shapes=[pltpu.SemaphoreType.DMA],
        ),
        out_shape=jax.ShapeDtypeStruct((rows, s, l), F32),
        input_output_aliases={3: 0},
        compiler_params=_cparams("arbitrary"),
        name="moe_dispatch",
    )(pos1, pos2, h3, xs0)


def _experts_kernel(te_ref, nu_ref, x3_ref, w1_ref, w3_ref, w2_ref, y3_ref, h_ref, acc_ref):
    del te_ref
    i = pl.program_id(0)
    f = pl.program_id(1)
    nf = pl.num_programs(1)
    used = i < nu_ref[0]

    @pl.when(used)
    def _():
        @pl.when(f == 0)
        def _():
            for k in range(x3_ref.shape[1]):
                h_ref[:, k * LANES:(k + 1) * LANES] = x3_ref[:, k, :].astype(BF16)

        part = _swiglu_partial(h_ref[...], w1_ref[...], w3_ref[...], w2_ref[...])

        @pl.when(f == 0)
        def _():
            acc_ref[...] = part

        @pl.when(f > 0)
        def _():
            acc_ref[...] += part

        @pl.when(f == nf - 1)
        def _():
            _to_slabs(y3_ref, acc_ref[...])

    @pl.when(jnp.logical_not(used) & (f == nf - 1))
    def _():
        y3_ref[...] = jnp.zeros_like(y3_ref)


def moe_experts(xs, tile_expert, n_used, w1, w3, w2, *, tm, tf):
    rows, s, l = xs.shape
    d = s * l
    ff = w1.shape[2]
    nf = ff // tf

    def wmap_in(i, f, te, nu):
        live = i < nu[0]
        return (te[i], 0, jnp.where(live, f, nf - 1))

    def wmap_out(i, f, te, nu):
        live = i < nu[0]
        return (te[i], jnp.where(live, f, nf - 1), 0)

    def xmap(i, f, te, nu):
        return (jnp.minimum(i, nu[0] - 1), 0, 0)

    return pl.pallas_call(
        _experts_kernel,
        grid_spec=pltpu.PrefetchScalarGridSpec(
            num_scalar_prefetch=2,
            grid=(rows // tm, nf),
            in_specs=[
                pl.BlockSpec((tm, s, l), xmap),
                pl.BlockSpec((None, d, tf), wmap_in),
                pl.BlockSpec((None, d, tf), wmap_in),
                pl.BlockSpec((None, tf, d), wmap_out),
            ],
            out_specs=pl.BlockSpec((tm, s, l), lambda i, f, te, nu: (i, 0, 0)),
            scratch_shapes=[pltpu.VMEM((tm, d), BF16), pltpu.VMEM((tm, d), F32)],
        ),
        out_shape=jax.ShapeDtypeStruct((rows, s, l), F32),
        compiler_params=_cparams("arbitrary", "arbitrary"),
        name="moe_experts",
    )(tile_expert, n_used, xs, w1, w3, w2)


def _combine_kernel(pos1_ref, pos2_ref, x_ref, rw_ref, g_ref, mod_ref, ys_ref, o_ref, ybuf, sem, *, mod_row):
    tm, d = x_ref.shape
    i = pl.program_id(0)
    n = pl.num_programs(0)

    def issue(step, slot):
        base = step * tm

        def body(t, carry):
            pltpu.make_async_copy(ys_ref.at[pos1_ref[base + t]], ybuf.at[slot, 0, t], sem.at[slot]).start()
            pltpu.make_async_copy(ys_ref.at[pos2_ref[base + t]], ybuf.at[slot, 1, t], sem.at[slot]).start()
            return carry

        lax.fori_loop(0, tm, body, 0, unroll=8)

    @pl.when(i == 0)
    def _():
        issue(0, 0)

    @pl.when(i + 1 < n)
    def _():
        issue(i + 1, (i + 1) % 2)

    slot = i % 2
    for k in range(2):
        pltpu.make_async_copy(ys_ref.at[pl.ds(0, tm)], ybuf.at[slot, k], sem.at[slot]).wait()
    y = (rw_ref[:, 0:1] * _from_slabs(ybuf.at[slot, 0]) + rw_ref[:, 1:2] * _from_slabs(ybuf.at[slot, 1]))
    o_ref[...] = x_ref[...] + _mod_slice(mod_ref, mod_row, GA2, d) * _rms(y, g_ref[3:4, :])


def moe_combine(x, norm_g, mods, layer, route_w, ys, pos1, pos2, *, mod_row, tm):
    m, d = x.shape
    s, l = ys.shape[1:]
    return pl.pallas_call(
        functools.partial(_combine_kernel, mod_row=mod_row),
        grid_spec=pltpu.PrefetchScalarGridSpec(
            num_scalar_prefetch=2,
            grid=(m // tm,),
            in_specs=[
                pl.BlockSpec((tm, d), lambda i, p1, p2: (i, 0)),
                pl.BlockSpec((tm, 2), lambda i, p1, p2: (i, 0)),
                pl.BlockSpec((None,) + norm_g.shape[1:], lambda i, p1, p2: (layer, 0, 0)),
                pl.BlockSpec((None,) + mods.shape[1:], lambda i, p1, p2: (layer, 0, 0)),
                pl.BlockSpec(memory_space=pl.ANY),
            ],
            out_specs=pl.BlockSpec((tm, d), lambda i, p1, p2: (i, 0)),
            scratch_shapes=[pltpu.VMEM((2, 2, tm, s, l), F32), pltpu.SemaphoreType.DMA((2,))],
        ),
        out_shape=jax.ShapeDtypeStruct((m, d), F32),
        input_output_aliases={2: 0},
        compiler_params=_cparams("arbitrary"),
        name="moe_combine",
    )(pos1, pos2, x, route_w, norm_g, mods, ys)


def sparse_moe_layer(x, norm_g, mods, layer, w_router, b_router, w1, w3, w2, *, mod_row, tm_tok, tm_exp, tf):
    m, d = x.shape
    ne = w_router.shape[1]
    h3, route_i, route_w, counts = moe_router(x, norm_g, mods, layer, w_router, b_router, mod_row=mod_row, tm=tm_tok)
    counts = counts[0]
    padded = (counts + tm_exp - 1) // tm_exp * tm_exp
    ends = jnp.cumsum(padded)
    starts = ends - padded
    sel = lambda e: jnp.sum(jnp.where(e[:, None] == jnp.arange(ne)[None, :], starts[None, :], 0), axis=1)
    pos1 = sel(route_i[:, 0]) + route_i[:, 2]
    pos2 = sel(route_i[:, 1]) + route_i[:, 3]
    n_tiles = (2 * m) // tm_exp + ne
    tile_start = jnp.arange(n_tiles, dtype=jnp.int32) * tm_exp
    n_used = (ends[-1] // tm_exp).astype(jnp.int32).reshape(1)
    tile_expert = jnp.sum(tile_start[:, None] >= ends[None, :], axis=1).astype(jnp.int32)
    tile_expert = jnp.minimum(tile_expert, tile_expert[jnp.maximum(n_used[0] - 1, 0)])
    xs = moe_dispatch(h3, pos1, pos2, n_tiles * tm_exp, tm=tm_tok)
    ys = moe_experts(xs, tile_expert, n_used, w1, w3, w2, tm=tm_exp, tf=tf)
    return moe_combine(x, norm_g, mods, layer, route_w, ys, pos1, pos2, mod_row=mod_row, tm=tm_tok)


def _rope_tables(seq):
    n = 16
    inv = ROPE_BASE ** (-jnp.arange(n, dtype=F32) / n)
    t = jnp.arange(seq)
    row_ang = (t // GRID_W).astype(F32)[:, None] * inv[None, :]
    col_ang = (t % GRID_W).astype(F32)[:, None] * inv[None, :]
    cos = jnp.concatenate([jnp.cos(row_ang)] * 2 + [jnp.cos(col_ang)] * 2, axis=1)
    sin = jnp.concatenate([-jnp.sin(row_ang), jnp.sin(row_ang), -jnp.sin(col_ang), jnp.sin(col_ang)], axis=1)
    return jnp.tile(cos, (1, 2)), jnp.tile(sin, (1, 2))


def kernel(x, c, ctx, c_ctx, w_ada, b_ada, norm_g, attn_w_qkv, attn_w_o, attn_sink, ret_w_in, ret_w_o,
           ret_log_decay, pool_w, pool_scale, ffn_w1, ffn_w3, ffn_w2, moe_w_router, moe_b_router,
           moe_w1, moe_w3, moe_w2):
    batch, seq, d = x.shape
    assert batch == 1 and c.shape[0] == 1
    depth = w_ada.shape[0]
    lc = ctx.shape[1]
    xl = x.reshape(seq, d)
    xc = ctx.reshape(lc, d)

    cvecs = jnp.zeros((8, d), F32).at[LAT_ROW].set(c[0]).at[CTX_ROW].set(c_ctx)
    mods = ada_table(cvecs, w_ada, b_ada)
    rope = _rope_tables(seq)

    hd = attn_w_o.shape[1]
    dh = hd // ATTN_HEADS
    qkv_scale = jnp.concatenate([jnp.full((hd,), dh ** -0.5, F32),
                                 jnp.ones((attn_w_qkv.shape[2] - hd,), F32)])
    dk = d // RET_HEADS
    ret_scale = jnp.concatenate([jnp.ones((d,), F32), jnp.full((d,), dk ** -0.5, F32),
                                 jnp.ones((ret_w_in.shape[2] - 2 * d,), F32)])

    for i in range(depth):
        last = i == depth - 1
        kind, j = i % N_MIXERS, i // N_MIXERS
        proj = functools.partial(norm_mod_matmul, norm_g=norm_g, mods=mods, layer=i, g_row=0, slot=SH1)
        if kind == 0:
            w_qkv = (attn_w_qkv[j] * qkv_scale).astype(BF16)
            w_o = attn_w_o[j].astype(BF16)
            kvd = (w_qkv.shape[1] - hd) // 2
            qkv_l = proj(xl, w=w_qkv, mod_row=LAT_ROW, tm=1024, tn=256, rope=rope, rope_tiles=(hd + kvd) // 256)
            qkv_c = proj(xc, w=w_qkv, mod_row=CTX_ROW, tm=256, tn=256)
            o_l = attention(attn_sink[j], qkv_l, qkv_c, has_local=True)
            xl = outproj_residual(xl, norm_g, mods, i, w_o, [o_l], g_row=1, mod_row=LAT_ROW, slot=GA1, tm=512)
            if not last:
                o_c = attention(attn_sink[j], qkv_c, qkv_c, has_local=False)
                xc = outproj_residual(xc, norm_g, mods, i, w_o, [o_c], g_row=1, mod_row=CTX_ROW, slot=GA1, tm=256)
        elif kind == 1:
            w_in = (ret_w_in[j] * ret_scale).astype(BF16)
            w_o = ret_w_o[j].astype(BF16)
            p_c = proj(xc, w=w_in, mod_row=CTX_ROW, tm=256, tn=1024)
            p_l = proj(xl, w=w_in, mod_row=LAT_ROW, tm=1024, tn=1024)
            s0 = jnp.zeros((RET_HEADS, dk, 2 * dk), F32)
            tabs_f = retention_tables(ret_log_decay[j, 0], min(RET_CHUNK, lc), False)
            tabs_b = retention_tables(ret_log_decay[j, 1], min(RET_CHUNK, lc), True)
            zf_c, s_f = retention_scan(p_c, tabs_f, s0, reverse=False)
            zb_c, s_b = retention_scan(p_c, tabs_b, s0, reverse=True)
            zf_l, _ = retention_scan(p_l, tabs_f, s_f, reverse=False)
            zb_l, _ = retention_scan(p_l, tabs_b, s_b, reverse=True)
            xl = outproj_residual(xl, norm_g, mods, i, w_o, [zf_l, zb_l], g_row=1, mod_row=LAT_ROW, slot=GA1, tm=512)
            if not last:
                xc = outproj_residual(xc, norm_g, mods, i, w_o, [zf_c, zb_c], g_row=1, mod_row=CTX_ROW, slot=GA1, tm=256)
        else:
            w_p = pool_w[j].astype(BF16)
            xl = pool_layer(xl, norm_g, mods, i, w_p, pool_scale[j], mod_row=LAT_ROW, tm=512)
            if not last:
                xc = pool_layer(xc, norm_g, mods, i, w_p, pool_scale[j], mod_row=CTX_ROW, tm=256)

        f = i // 2
        if i % 2 == 0:
            w1, w3, w2 = ffn_w1[f].astype(BF16), ffn_w3[f].astype(BF16), ffn_w2[f].astype(BF16)
            tf = w1.shape[1] // 2
            xl = ffn_layer(xl, norm_g, mods, i, w1, w3, w2, mod_row=LAT_ROW, tm=512, tf=tf)
            if not last:
                xc = ffn_layer(xc, norm_g, mods, i, w1, w3, w2, mod_row=CTX_ROW, tm=256, tf=tf)
        else:
            w1, w3, w2 = moe_w1[f].astype(BF16), moe_w3[f].astype(BF16), moe_w2[f].astype(BF16)
            tf = w1.shape[2] // 2
            xl = sparse_moe_layer(xl, norm_g, mods, i, moe_w_router[f], moe_b_router[f], w1, w3, w2,
                                  mod_row=LAT_ROW, tm_tok=min(512, seq), tm_exp=min(512, seq), tf=tf)
            if not last:
                xc = moe_layer(xc, norm_g, mods, i, moe_w_router[f], moe_b_router[f], w1, w3, w2,
                               mod_row=CTX_ROW, tm=256, tf=tf)
    return xl.reshape(batch, seq, d)
```

```python
import functools

import jax
import jax.numpy as jnp
from jax import lax
from jax.experimental import pallas as pl
from jax.experimental.pallas import tpu as pltpu

F32 = jnp.float32
BF16 = jnp.bfloat16

EPS = 1e-6
NEG_INF = -1e30
LANES = 128
VMEM_LIMIT = 56 * 1024 * 1024

GRID_W = 64
N_MIXERS = 3
ATTN_HEADS = 16
ATTN_KV_HEADS = 4
WINDOW = 128
ROPE_BASE = 10000.0
RET_HEADS = 4
RET_CHUNK = 256
POOL_WINDOWS = (2, 4, 8, 16)
POOL_HALO = 8
N_EXPERTS = 8

SH1, SC1, GA1, SH2, SC2, GA2 = range(6)
LAT_ROW, CTX_ROW = 0, 1


def _cparams(*sem):
    return pltpu.CompilerParams(dimension_semantics=sem, vmem_limit_bytes=VMEM_LIMIT)


def _rms(x, g):
    return x * lax.rsqrt(jnp.mean(x * x, axis=-1, keepdims=True) + EPS) * g


def _mod_slice(mod_ref, row, slot, d):
    return mod_ref[row:row + 1, slot * d:(slot + 1) * d]


def _norm_mod(x, g, shift, scale):
    return _rms(x, g) * (1.0 + scale) + shift


def _ada_kernel(c_ref, w_ref, b_ref, o_ref):
    c = c_ref[...]
    s = c * jax.nn.sigmoid(c)
    o_ref[...] = jnp.dot(s, w_ref[...], preferred_element_type=F32,
                         precision=lax.Precision.HIGHEST) + b_ref[...]


def ada_table(cvecs, w_ada, b_ada):
    depth, d, n = w_ada.shape
    tn = 1536
    return pl.pallas_call(
        _ada_kernel,
        grid=(depth, n // tn),
        in_specs=[
            pl.BlockSpec((8, d), lambda i, j: (0, 0)),
            pl.BlockSpec((None, d, tn), lambda i, j: (i, 0, j)),
            pl.BlockSpec((None, 1, tn), lambda i, j: (i, 0, j)),
        ],
        out_specs=pl.BlockSpec((None, 8, tn), lambda i, j: (i, 0, j)),
        out_shape=jax.ShapeDtypeStruct((depth, 8, n), F32),
        compiler_params=_cparams("parallel", "parallel"),
        name="ada_table",
    )(cvecs, w_ada, b_ada.reshape(depth, 1, n))


def _rope(a, cos, sin):
    lane = lax.broadcasted_iota(jnp.int32, (a.shape[0], LANES), 1)
    first = (lane % 32) < 16
    outs = []
    for cb in range(a.shape[1] // LANES):
        blk = a[:, cb * LANES:(cb + 1) * LANES]
        partner = jnp.where(first, pltpu.roll(blk, LANES - 16, 1), pltpu.roll(blk, 16, 1))
        outs.append(blk * cos + partner * sin)
    return jnp.concatenate(outs, axis=1)


def _proj_kernel(x_ref, g_ref, mod_ref, w_ref, *rest, g_row, mod_row, slot, rope_tiles):
    if rope_tiles:
        cos_ref, sin_ref, o_ref, h_ref = rest
    else:
        o_ref, h_ref = rest
    d = x_ref.shape[1]
    j = pl.program_id(1)

    @pl.when(j == 0)
    def _():
        h = _norm_mod(x_ref[...], g_ref[g_row:g_row + 1, :],
                      _mod_slice(mod_ref, mod_row, slot, d), _mod_slice(mod_ref, mod_row, slot + 1, d))
        h_ref[...] = h.astype(BF16)

    acc = jnp.dot(h_ref[...], w_ref[...], preferred_element_type=F32)
    if rope_tiles:
        @pl.when(j < rope_tiles)
        def _():
            o_ref[...] = _rope(acc, cos_ref[...], sin_ref[...]).astype(o_ref.dtype)

        @pl.when(j >= rope_tiles)
        def _():
            o_ref[...] = acc.astype(o_ref.dtype)
    else:
        o_ref[...] = acc.astype(o_ref.dtype)


def norm_mod_matmul(x, norm_g, mods, layer, w, *, g_row, mod_row, slot, tm, tn, rope=None, rope_tiles=0):
    m, d = x.shape
    n = w.shape[1]
    tm = min(tm, m)
    in_specs = [
        pl.BlockSpec((tm, d), lambda i, j: (i, 0)),
        pl.BlockSpec((None,) + norm_g.shape[1:], lambda i, j: (layer, 0, 0)),
        pl.BlockSpec((None,) + mods.shape[1:], lambda i, j: (layer, 0, 0)),
        pl.BlockSpec((d, tn), lambda i, j: (0, j)),
    ]
    args = [x, norm_g, mods, w]
    if rope_tiles:
        in_specs += [pl.BlockSpec((tm, LANES), lambda i, j: (i, 0))] * 2
        args += list(rope)
    return pl.pallas_call(
        functools.partial(_proj_kernel, g_row=g_row, mod_row=mod_row, slot=slot, rope_tiles=rope_tiles),
        grid=(m // tm, n // tn),
        in_specs=in_specs,
        out_specs=pl.BlockSpec((tm, tn), lambda i, j: (i, j)),
        out_shape=jax.ShapeDtypeStruct((m, n), BF16),
        scratch_shapes=[pltpu.VMEM((tm, d), BF16)],
        compiler_params=_cparams("parallel", "arbitrary"),
        name="norm_mod_matmul",
    )(*args)


def _outproj_kernel(x_ref, g_ref, mod_ref, w_ref, *rest, n_y, g_row, mod_row, slot):
    y_refs, o_ref = rest[:n_y], rest[n_y]
    d = x_ref.shape[1]
    y = y_refs[0][...]
    if n_y == 2:
        y = (y.astype(F32) + y_refs[1][...].astype(F32)).astype(BF16)
    t = jnp.dot(y, w_ref[...], preferred_element_type=F32)
    gate = _mod_slice(mod_ref, mod_row, slot, d)
    o_ref[...] = x_ref[...] + gate * _rms(t, g_ref[g_row:g_row + 1, :])


def outproj_residual(x, norm_g, mods, layer, w, ys, *, g_row, mod_row, slot, tm):
    m, d = x.shape
    k = w.shape[0]
    tm = min(tm, m)
    in_specs = [
        pl.BlockSpec((tm, d), lambda i: (i, 0)),
        pl.BlockSpec((None,) + norm_g.shape[1:], lambda i: (layer, 0, 0)),
        pl.BlockSpec((None,) + mods.shape[1:], lambda i: (layer, 0, 0)),
        pl.BlockSpec((k, d), lambda i: (0, 0)),
    ] + [pl.BlockSpec((tm, k), lambda i: (i, 0))] * len(ys)
    return pl.pallas_call(
        functools.partial(_outproj_kernel, n_y=len(ys), g_row=g_row, mod_row=mod_row, slot=slot),
        grid=(m // tm,),
        in_specs=in_specs,
        out_specs=pl.BlockSpec((tm, d), lambda i: (i, 0)),
        out_shape=jax.ShapeDtypeStruct((m, d), F32),
        input_output_aliases={0: 0},
        compiler_params=_cparams("parallel"),
        name="outproj_residual",
    )(x, norm_g, mods, w, *ys)


def _dot_nt(a, b):
    return lax.dot_general(a, b, (((1,), (1,)), ((), ())), preferred_element_type=F32)


def _attn_kernel(sink_ref, q_ref, *rest, tq, seq, kv_heads, group, dh, has_local):
    if has_local:
        kp_ref, kq_ref, kn_ref, vp_ref, vq_ref, vn_ref, kc_ref, vc_ref, o_ref = rest
    else:
        kc_ref, vc_ref, o_ref = rest
    b = pl.program_id(0)
    rows = group * tq
    head_of_row = lax.broadcasted_iota(jnp.int32, (rows, 1), 0) // tq
    if has_local:
        r = lax.broadcasted_iota(jnp.int32, (rows, 3 * tq), 0) % tq
        c = lax.broadcasted_iota(jnp.int32, (rows, 3 * tq), 1)
        kpos = (b - 1) * tq + c
        rel = c - tq - r
        valid = (jnp.abs(rel) <= WINDOW) & (kpos >= 0) & (kpos < seq)
    for j in range(kv_heads):
        hs = slice(j * dh, (j + 1) * dh)
        qg = jnp.concatenate(
            [q_ref[:, (j * group + g) * dh:(j * group + g + 1) * dh] for g in range(group)], axis=0)
        sink = jnp.full((rows, 1), sink_ref[j * group], F32)
        for g in range(1, group):
            sink = jnp.where(head_of_row == g, sink_ref[j * group + g], sink)
        s_ctx = _dot_nt(qg, kc_ref[:, hs])
        m = jnp.maximum(jnp.max(s_ctx, axis=-1, keepdims=True), sink)
        if has_local:
            k_loc = jnp.concatenate([kp_ref[:, hs], kq_ref[:, hs], kn_ref[:, hs]], axis=0)
            v_loc = jnp.concatenate([vp_ref[:, hs], vq_ref[:, hs], vn_ref[:, hs]], axis=0)
            s_loc = jnp.where(valid, _dot_nt(qg, k_loc), NEG_INF)
            m = jnp.maximum(m, jnp.max(s_loc, axis=-1, keepdims=True))
        p_ctx = jnp.exp(s_ctx - m)
        den = jnp.sum(p_ctx, axis=-1, keepdims=True) + jnp.exp(sink - m)
        o = jnp.dot(p_ctx.astype(BF16), vc_ref[:, hs], preferred_element_type=F32)
        if has_local:
            p_loc = jnp.exp(s_loc - m)
            den = den + jnp.sum(p_loc, axis=-1, keepdims=True)
            o = o + jnp.dot(p_loc.astype(BF16), v_loc, preferred_element_type=F32)
        o = o / den
        for g in range(group):
            h = j * group + g
            o_ref[:, h * dh:(h + 1) * dh] = o[g * tq:(g + 1) * tq].astype(o_ref.dtype)


def attention(sink, qkv, qkv_ctx, *, has_local):
    s = qkv.shape[0]
    dh = qkv.shape[1] // (ATTN_HEADS + 2 * ATTN_KV_HEADS)
    hd = ATTN_HEADS * dh
    kvd = ATTN_KV_HEADS * dh
    kcol, vcol = hd // kvd, hd // kvd + 1
    tq = WINDOW
    nb = s // tq
    lc = qkv_ctx.shape[0]
    in_specs = [
        pl.BlockSpec(memory_space=pltpu.SMEM),
        pl.BlockSpec((tq, hd), lambda b: (b, 0)),
    ]
    args = [sink, qkv]
    if has_local:
        for col in (kcol, vcol):
            in_specs += [
                pl.BlockSpec((tq, kvd), lambda b, col=col: (jnp.maximum(b - 1, 0), col)),
                pl.BlockSpec((tq, kvd), lambda b, col=col: (b, col)),
                pl.BlockSpec((tq, kvd), lambda b, col=col: (jnp.minimum(b + 1, nb - 1), col)),
            ]
            args += [qkv] * 3
    in_specs += [pl.BlockSpec((lc, kvd), lambda b: (0, kcol)), pl.BlockSpec((lc, kvd), lambda b: (0, vcol))]
    args += [qkv_ctx, qkv_ctx]
    return pl.pallas_call(
        functools.partial(_attn_kernel, tq=tq, seq=s, kv_heads=ATTN_KV_HEADS,
                          group=ATTN_HEADS // ATTN_KV_HEADS, dh=dh, has_local=has_local),
        grid=(nb,),
        in_specs=in_specs,
        out_specs=pl.BlockSpec((tq, hd), lambda b: (b, 0)),
        out_shape=jax.ShapeDtypeStruct((s, hd), BF16),
        compiler_params=_cparams("parallel"),
        name="attention",
    )(*args)


def _ret_kernel(q_ref, k_ref, v_ref, gate_ref, intra_ref, qdec_ref, kdec_ref, cdec_ref, s0_ref,
                z_ref, sfin_ref, state_ref, *, heads, dk, dv):
    i = pl.program_id(0)

    @pl.when(i == 0)
    def _():
        state_ref[...] = s0_ref[...]

    for h in range(heads):
        q = q_ref[:, h * dk:(h + 1) * dk]
        k = k_ref[:, h * dk:(h + 1) * dk]
        v = v_ref[:, h * dv:(h + 1) * dv]
        state = state_ref[h]
        sc = _dot_nt(q, k) * intra_ref[h]
        qd = (q.astype(F32) * qdec_ref[h]).astype(BF16)
        o = (jnp.dot(sc.astype(BF16), v, preferred_element_type=F32)
             + jnp.dot(qd, state.astype(BF16), preferred_element_type=F32))
        kd = (k.astype(F32) * kdec_ref[h]).astype(BF16)
        state_ref[h] = state * cdec_ref[h] + lax.dot_general(
            kd, v, (((0,), (0,)), ((), ())), preferred_element_type=F32)
        mu = jnp.mean(o, axis=-1, keepdims=True)
        oc = o - mu
        var = jnp.mean(oc * oc, axis=-1, keepdims=True)
        gate = gate_ref[:, h * dv:(h + 1) * dv].astype(F32)
        z = gate * jax.nn.sigmoid(gate) * (oc * lax.rsqrt(var + EPS))
        z_ref[:, h * dv:(h + 1) * dv] = z.astype(z_ref.dtype)

    @pl.when(i == pl.num_programs(0) - 1)
    def _():
        sfin_ref[...] = state_ref[...]


def retention_scan(proj, tables, s0, *, reverse):
    m, n = proj.shape
    d = n // 8
    heads = RET_HEADS
    dk, dv = d // heads, 2 * d // heads
    c = min(RET_CHUNK, m)
    nc = m // c
    intra, qdec, kdec, cdec = tables

    def row(i):
        return nc - 1 - i if reverse else i

    gate_blk = 3 if reverse else 2
    full = lambda a: pl.BlockSpec(a.shape, lambda i: (0,) * a.ndim)
    z, sfin = pl.pallas_call(
        functools.partial(_ret_kernel, heads=heads, dk=dk, dv=dv),
        grid=(nc,),
        in_specs=[
            pl.BlockSpec((c, d), lambda i: (row(i), 0)),
            pl.BlockSpec((c, d), lambda i: (row(i), 1)),
            pl.BlockSpec((c, 2 * d), lambda i: (row(i), 1)),
            pl.BlockSpec((c, 2 * d), lambda i: (row(i), gate_blk)),
            full(intra), full(qdec), full(kdec), full(cdec), full(s0),
        ],
        out_specs=[
            pl.BlockSpec((c, 2 * d), lambda i: (row(i), 0)),
            full(s0),
        ],
        out_shape=[jax.ShapeDtypeStruct((m, 2 * d), BF16), jax.ShapeDtypeStruct(s0.shape, F32)],
        scratch_shapes=[pltpu.VMEM(s0.shape, F32)],
        compiler_params=_cparams("arbitrary"),
        name="retention_scan",
    )(proj, proj, proj, proj, intra, qdec, kdec, cdec, s0)
    return z, sfin


def retention_tables(log_decay_row, c, reverse):
    lg = -jnp.exp(log_decay_row.astype(F32))
    idx = jnp.arange(c, dtype=F32)
    diff = idx[:, None] - idx[None, :]
    if reverse:
        diff = -diff
    intra = jnp.where(diff >= 0, jnp.exp(lg[:, None, None] * jnp.maximum(diff, 0.0)), 0.0)
    fwd_idx = (c - 1.0 - idx) if reverse else idx
    qdec = jnp.exp(lg[:, None] * (fwd_idx + 1.0))[:, :, None]
    kdec = jnp.exp(lg[:, None] * (c - 1.0 - fwd_idx))[:, :, None]
    cdec = jnp.exp(lg * c)[:, None, None]
    return intra, qdec, kdec, cdec


def _pool_kernel(x_ref, xp_ref, xn_ref, g_ref, mod_ref, w_ref, ps_ref, o_ref, h_ref, *, mod_row, seq):
    tm, d = x_ref.shape
    halo = POOL_HALO
    i = pl.program_id(0)
    g0 = g_ref[0:1, :]
    shift = _mod_slice(mod_ref, mod_row, SH1, d)
    scale = _mod_slice(mod_ref, mod_row, SC1, d)
    hp = _norm_mod(xp_ref[...], g0, shift, scale)
    hn = _norm_mod(xn_ref[...], g0, shift, scale)
    h_ref[0:halo, :] = jnp.where(i > 0, hp, 0.0)
    h_ref[halo:halo + tm, :] = _norm_mod(x_ref[...], g0, shift, scale)
    h_ref[halo + tm:, :] = jnp.where(i < pl.num_programs(0) - 1, hn, 0.0)

    t = i * tm + lax.broadcasted_iota(jnp.int32, (tm, 1), 0)
    gw = d // len(POOL_WINDOWS)
    ys = []
    for g, w in enumerate(POOL_WINDOWS):
        cols = slice(g * gw, (g + 1) * gw)
        tot = h_ref[halo - w // 2:halo - w // 2 + tm, cols]
        for off in range(-w // 2 + 1, w // 2):
            tot = tot + h_ref[halo + off:halo + off + tm, cols]
        cnt = (jnp.minimum(t + w // 2, seq) - jnp.maximum(t - w // 2, 0)).astype(F32)
        dm = tot / cnt - h_ref[halo:halo + tm, cols]
        ys.append(jnp.dot(dm.astype(BF16), w_ref[g], preferred_element_type=F32))
    y = jnp.concatenate(ys, axis=1) * ps_ref[...]
    o_ref[...] = x_ref[...] + _mod_slice(mod_ref, mod_row, GA1, d) * _rms(y, g_ref[1:2, :])


def pool_layer(x, norm_g, mods, layer, w_pool, pool_scale, *, mod_row, tm):
    m, d = x.shape
    tm = min(tm, m)
    nt = m // tm
    hb = tm // POOL_HALO
    return pl.pallas_call(
        functools.partial(_pool_kernel, mod_row=mod_row, seq=m),
        grid=(nt,),
        in_specs=[
            pl.BlockSpec((tm, d), lambda i: (i, 0)),
            pl.BlockSpec((POOL_HALO, d), lambda i: (jnp.maximum(i * hb - 1, 0), 0)),
            pl.BlockSpec((POOL_HALO, d), lambda i: (jnp.minimum((i + 1) * hb, nt * hb - 1), 0)),
            pl.BlockSpec((None,) + norm_g.shape[1:], lambda i: (layer, 0, 0)),
            pl.BlockSpec((None,) + mods.shape[1:], lambda i: (layer, 0, 0)),
            pl.BlockSpec(w_pool.shape, lambda i: (0, 0, 0)),
            pl.BlockSpec((1, d), lambda i: (0, 0)),
        ],
        out_specs=pl.BlockSpec((tm, d), lambda i: (i, 0)),
        out_shape=jax.ShapeDtypeStruct((m, d), F32),
        scratch_shapes=[pltpu.VMEM((tm + 2 * POOL_HALO, d), F32)],
        compiler_params=_cparams("parallel"),
        name="pool_layer",
    )(x, x, x, norm_g, mods, w_pool, pool_scale.reshape(1, d))


def _swiglu_partial(h, w1, w3, w2):
    a = jnp.dot(h, w1, preferred_element_type=F32)
    b = jnp.dot(h, w3, preferred_element_type=F32)
    u = (a * jax.nn.sigmoid(a) * b).astype(BF16)
    return jnp.dot(u, w2, preferred_element_type=F32)


def _ffn_kernel(x_ref, g_ref, mod_ref, w1_ref, w3_ref, w2_ref, *rest, mod_row, n_cast):
    cast_src, o_ref, cast_dst = rest[:n_cast], rest[n_cast], rest[n_cast + 1:2 * n_cast + 1]
    h_ref, acc_ref = rest[2 * n_cast + 1:]
    d = x_ref.shape[1]
    f = pl.program_id(1)
    for src, dst in zip(cast_src, cast_dst):
        dst[...] = src[...].astype(dst.dtype)

    @pl.when(f == 0)
    def _():
        h = _norm_mod(x_ref[...], g_ref[2:3, :], _mod_slice(mod_ref, mod_row, SH2, d),
                      _mod_slice(mod_ref, mod_row, SC2, d))
        h_ref[...] = h.astype(BF16)

    part = _swiglu_partial(h_ref[...], w1_ref[...], w3_ref[...], w2_ref[...])

    @pl.when(f == 0)
    def _():
        acc_ref[...] = part

    @pl.when(f > 0)
    def _():
        acc_ref[...] += part

    @pl.when(f == pl.num_programs(1) - 1)
    def _():
        o_ref[...] = x_ref[...] + _mod_slice(mod_ref, mod_row, GA2, d) * _rms(acc_ref[...], g_ref[3:4, :])


def ffn_layer(x, norm_g, mods, layer, w1, w3, w2, *, mod_row, tm, tf, cast=None):
    m, d = x.shape
    ff = w1.shape[1]
    tm = min(tm, m)
    nf = ff // tf
    steps = (m // tm) * nf
    in_specs = [
        pl.BlockSpec((tm, d), lambda i, f: (i, 0)),
        pl.BlockSpec((None,) + norm_g.shape[1:], lambda i, f: (layer, 0, 0)),
        pl.BlockSpec((None,) + mods.shape[1:], lambda i, f: (layer, 0, 0)),
        pl.BlockSpec((d, tf), lambda i, f: (0, f)),
        pl.BlockSpec((d, tf), lambda i, f: (0, f)),
        pl.BlockSpec((tf, d), lambda i, f: (f, 0)),
    ]
    out_specs = [pl.BlockSpec((tm, d), lambda i, f: (i, 0))]
    out_shape = [jax.ShapeDtypeStruct((m, d), F32)]
    cast_arrays, cast_idx = cast if cast is not None else ((), 0)
    for a in cast_arrays:
        _, ne, r, c = a.shape
        per_expert = steps // ne
        assert per_expert * ne == steps and r % (per_expert * 16) == 0
        rows = r // per_expert
        in_specs.append(pl.BlockSpec(
            (None, None, rows, c),
            lambda i, f, pe=per_expert: (cast_idx, (i * nf + f) // pe, (i * nf + f) % pe, 0)))
        out_specs.append(pl.BlockSpec(
            (None, rows, c), lambda i, f, pe=per_expert: ((i * nf + f) // pe, (i * nf + f) % pe, 0)))
        out_shape.append(jax.ShapeDtypeStruct((ne, r, c), BF16))
    outs = pl.pallas_call(
        functools.partial(_ffn_kernel, mod_row=mod_row, n_cast=len(cast_arrays)),
        grid=(m // tm, nf),
        in_specs=in_specs,
        out_specs=out_specs,
        out_shape=out_shape,
        scratch_shapes=[pltpu.VMEM((tm, d), BF16), pltpu.VMEM((tm, d), F32)],
        input_output_aliases={0: 0},
        compiler_params=_cparams("arbitrary", "arbitrary"),
        name="ffn_layer",
    )(x, norm_g, mods, w1, w3, w2, *cast_arrays)
    return outs[0] if cast is None else (outs[0], tuple(outs[1:]))


def _route(logits):
    e = logits.shape[1]
    lane = lax.broadcasted_iota(jnp.int32, logits.shape, 1)
    v1 = jnp.max(logits, axis=-1, keepdims=True)
    i1 = jnp.min(jnp.where(logits == v1, lane, e), axis=-1, keepdims=True)
    rest = jnp.where(lane == i1, -jnp.inf, logits)
    v2 = jnp.max(rest, axis=-1, keepdims=True)
    i2 = jnp.min(jnp.where(rest == v2, lane, e), axis=-1, keepdims=True)
    e2 = jnp.exp(v2 - v1)
    den = 1.0 + e2
    return jnp.where(lane == i1, 1.0 / den, 0.0) + jnp.where(lane == i2, e2 / den, 0.0)


def _moe_kernel(x_ref, g_ref, mod_ref, wr_ref, br_ref, w1_ref, w3_ref, w2_ref, o_ref,
                h_ref, acc_ref, cw_ref, *, mod_row):
    d = x_ref.shape[1]
    e = pl.program_id(1)
    f = pl.program_id(2)
    first = (e == 0) & (f == 0)
    last = (e == pl.num_programs(1) - 1) & (f == pl.num_programs(2) - 1)

    @pl.when(first)
    def _():
        h = _norm_mod(x_ref[...], g_ref[2:3, :], _mod_slice(mod_ref, mod_row, SH2, d),
                      _mod_slice(mod_ref, mod_row, SC2, d))
        h_ref[...] = h.astype(BF16)
        logits = jnp.dot(h, wr_ref[...], preferred_element_type=F32,
                         precision=lax.Precision.HIGHEST) + br_ref[...]
        cw = _route(logits)
        for ee in range(cw.shape[1]):
            cw_ref[ee] = cw[:, ee:ee + 1]

    part = cw_ref[e] * _swiglu_partial(h_ref[...], w1_ref[...], w3_ref[...], w2_ref[...])

    @pl.when(first)
    def _():
        acc_ref[...] = part

    @pl.when(jnp.logical_not(first))
    def _():
        acc_ref[...] += part

    @pl.when(last)
    def _():
        o_ref[...] = x_ref[...] + _mod_slice(mod_ref, mod_row, GA2, d) * _rms(acc_ref[...], g_ref[3:4, :])


def moe_layer(x, norm_g, mods, layer, w_router, b_router, w1, w3, w2, *, mod_row, tm, tf):
    m, d = x.shape
    ne, _, ff = w1.shape
    tm = min(tm, m)
    return pl.pallas_call(
        functools.partial(_moe_kernel, mod_row=mod_row),
        grid=(m // tm, ne, ff // tf),
        in_specs=[
            pl.BlockSpec((tm, d), lambda i, e, f: (i, 0)),
            pl.BlockSpec((None,) + norm_g.shape[1:], lambda i, e, f: (layer, 0, 0)),
            pl.BlockSpec((None,) + mods.shape[1:], lambda i, e, f: (layer, 0, 0)),
            pl.BlockSpec((d, ne), lambda i, e, f: (0, 0)),
            pl.BlockSpec((1, ne), lambda i, e, f: (0, 0)),
            pl.BlockSpec((None, d, tf), lambda i, e, f: (e, 0, f)),
            pl.BlockSpec((None, d, tf), lambda i, e, f: (e, 0, f)),
            pl.BlockSpec((None, tf, d), lambda i, e, f: (e, f, 0)),
        ],
        out_specs=pl.BlockSpec((tm, d), lambda i, e, f: (i, 0)),
        out_shape=jax.ShapeDtypeStruct((m, d), F32),
        scratch_shapes=[pltpu.VMEM((tm, d), BF16), pltpu.VMEM((tm, d), F32), pltpu.VMEM((ne, tm, 1), F32)],
        input_output_aliases={0: 0},
        compiler_params=_cparams("parallel", "arbitrary", "arbitrary"),
        name="moe_layer",
    )(x, norm_g, mods, w_router, b_router.reshape(1, ne), w1, w3, w2)


RUN_ALIGN = 16
RUN_BITS = tuple(range(9, 3, -1))


def _run_dmas(tile, ne, lpad_ref, loff_ref, base_ref, buf, hbm_ref, sem, *, to_hbm, start):
    for e in range(ne):
        length = lpad_ref[tile * ne + e]
        lo = loff_ref[tile * ne + e]
        gb = base_ref[tile * ne + e]
        for b in RUN_BITS:
            size = 1 << b
            off = (length >> (b + 1)) << (b + 1)

            @pl.when(((length >> b) & 1) == 1)
            def _():
                v = buf.at[pl.ds(pl.multiple_of(lo + off, RUN_ALIGN), size)]
                g = hbm_ref.at[pl.ds(pl.multiple_of(gb + off, RUN_ALIGN), size)]
                cp = pltpu.make_async_copy(v, g, sem) if to_hbm else pltpu.make_async_copy(g, v, sem)
                if start:
                    cp.start()
                else:
                    cp.wait()


def _local_rows(ri_ref, loff_ref, tile, ne):
    e1, e2 = ri_ref[:, 0:1], ri_ref[:, 1:2]
    lo1, lo2 = ri_ref[:, 2:3], ri_ref[:, 3:4]
    for e in range(ne):
        off = loff_ref[tile * ne + e]
        lo1 = lo1 + jnp.where(e1 == e, off, 0)
        lo2 = lo2 + jnp.where(e2 == e, off, 0)
    return lo1, lo2


def _router_kernel(x_ref, g_ref, mod_ref, wr_ref, br_ref, ri_ref, rw_ref, cnt_ref, *, mod_row):
    tm, d = x_ref.shape
    ne = wr_ref.shape[1]
    h = _norm_mod(x_ref[...], g_ref[2:3, :], _mod_slice(mod_ref, mod_row, SH2, d),
                  _mod_slice(mod_ref, mod_row, SC2, d))
    logits = jnp.dot(h, wr_ref[...], preferred_element_type=F32,
                     precision=lax.Precision.HIGHEST) + br_ref[...]
    lane = lax.broadcasted_iota(jnp.int32, logits.shape, 1)
    v1 = jnp.max(logits, axis=-1, keepdims=True)
    i1 = jnp.min(jnp.where(logits == v1, lane, ne), axis=-1, keepdims=True)
    rest = jnp.where(lane == i1, -jnp.inf, logits)
    v2 = jnp.max(rest, axis=-1, keepdims=True)
    i2 = jnp.min(jnp.where(rest == v2, lane, ne), axis=-1, keepdims=True)
    e2 = jnp.exp(v2 - v1)
    den = 1.0 + e2
    oh1 = (lane == i1).astype(F32)
    oh2 = (lane == i2).astype(F32)
    both = oh1 + oh2
    r = lax.broadcasted_iota(jnp.int32, (tm, tm), 0)
    c = lax.broadcasted_iota(jnp.int32, (tm, tm), 1)
    tri = jnp.where(c < r, 1.0, 0.0).astype(BF16)
    rank_all = jnp.dot(tri, both.astype(BF16), preferred_element_type=F32)
    rank1 = jnp.sum(rank_all * oh1, axis=-1, keepdims=True).astype(jnp.int32)
    rank2 = jnp.sum(rank_all * oh2, axis=-1, keepdims=True).astype(jnp.int32)
    col = lax.broadcasted_iota(jnp.int32, ri_ref.shape, 1)
    ri_ref[...] = jnp.where(col == 0, i1, jnp.where(col == 1, i2, jnp.where(col == 2, rank1, rank2)))
    colw = lax.broadcasted_iota(jnp.int32, rw_ref.shape, 1)
    rw_ref[...] = jnp.where(colw == 0, 1.0 / den, e2 / den)
    cnt_ref[...] = jnp.sum(both, axis=0, keepdims=True).astype(jnp.int32)


def moe_router(x, norm_g, mods, layer, w_router, b_router, *, mod_row, tm):
    m, d = x.shape
    ne = w_router.shape[1]
    return pl.pallas_call(
        functools.partial(_router_kernel, mod_row=mod_row),
        grid=(m // tm,),
        in_specs=[
            pl.BlockSpec((tm, d), lambda i: (i, 0)),
            pl.BlockSpec((None,) + norm_g.shape[1:], lambda i: (layer, 0, 0)),
            pl.BlockSpec((None,) + mods.shape[1:], lambda i: (layer, 0, 0)),
            pl.BlockSpec((d, ne), lambda i: (0, 0)),
            pl.BlockSpec((1, ne), lambda i: (0, 0)),
        ],
        out_specs=[
            pl.BlockSpec((tm, 4), lambda i: (i, 0)),
            pl.BlockSpec((tm, 2), lambda i: (i, 0)),
            pl.BlockSpec((None, 1, ne), lambda i: (i, 0, 0)),
        ],
        out_shape=[
            jax.ShapeDtypeStruct((m, 4), jnp.int32),
            jax.ShapeDtypeStruct((m, 2), F32),
            jax.ShapeDtypeStruct((m // tm, 1, ne), jnp.int32),
        ],
        compiler_params=_cparams("parallel"),
        name="moe_router",
    )(x, norm_g, mods, w_router, b_router.reshape(1, ne))


def _dispatch_kernel(lpad_ref, loff_ref, base_ref, x_ref, g_ref, mod_ref, ri_ref, xs_in_ref, xs_ref,
                     cbuf, sem, *, mod_row, ne):
    del xs_in_ref
    tm, d = x_ref.shape
    rc = cbuf.shape[1]
    i = pl.program_id(0)
    slot = i % 2
    h = _norm_mod(x_ref[...], g_ref[2:3, :], _mod_slice(mod_ref, mod_row, SH2, d),
                  _mod_slice(mod_ref, mod_row, SC2, d)).astype(BF16)
    lo1, lo2 = _local_rows(ri_ref, loff_ref, i, ne)
    lane = lax.broadcasted_iota(jnp.int32, (tm, rc), 1)
    onehot = jnp.where((lane == lo1) | (lane == lo2), 1.0, 0.0).astype(BF16)
    sorted_rows = lax.dot_general(onehot, h, (((0,), (0,)), ((), ())), preferred_element_type=F32)
    cbuf[slot] = sorted_rows.astype(BF16)
    moves = functools.partial(_run_dmas, ne=ne, lpad_ref=lpad_ref, loff_ref=loff_ref, base_ref=base_ref,
                              hbm_ref=xs_ref, to_hbm=True)
    moves(i, buf=cbuf.at[slot], sem=sem.at[slot], start=True)

    @pl.when(i > 0)
    def _():
        moves(i - 1, buf=cbuf.at[1 - slot], sem=sem.at[1 - slot], start=False)

    @pl.when(i == pl.num_programs(0) - 1)
    def _():
        moves(i, buf=cbuf.at[slot], sem=sem.at[slot], start=False)


def moe_dispatch(x, norm_g, mods, layer, route_i, lpad, loff, base, rows, *, mod_row, tm, ne):
    m, d = x.shape
    rc = 2 * tm + ne * RUN_ALIGN
    xs0 = jnp.zeros((rows, d), BF16)
    return pl.pallas_call(
        functools.partial(_dispatch_kernel, mod_row=mod_row, ne=ne),
        grid_spec=pltpu.PrefetchScalarGridSpec(
            num_scalar_prefetch=3,
            grid=(m // tm,),
            in_specs=[
                pl.BlockSpec((tm, d), lambda i, *_: (i, 0)),
                pl.BlockSpec((None,) + norm_g.shape[1:], lambda i, *_: (layer, 0, 0)),
                pl.BlockSpec((None,) + mods.shape[1:], lambda i, *_: (layer, 0, 0)),
                pl.BlockSpec((tm, 4), lambda i, *_: (i, 0)),
                pl.BlockSpec(memory_space=pl.ANY),
            ],
            out_specs=pl.BlockSpec(memory_space=pl.ANY),
            scratch_shapes=[pltpu.VMEM((2, rc, d), BF16), pltpu.SemaphoreType.DMA((2,))],
        ),
        out_shape=jax.ShapeDtypeStruct((rows, d), BF16),
        input_output_aliases={7: 0},
        compiler_params=_cparams("arbitrary"),
        name="moe_dispatch",
    )(lpad, loff, base, x, norm_g, mods, route_i, xs0)


def _experts_kernel(te_ref, nu_ref, x_ref, w1_ref, w3_ref, w2_ref, y_ref, acc_ref):
    del te_ref
    i = pl.program_id(0)
    f = pl.program_id(1)
    nf = pl.num_programs(1)
    used = i < nu_ref[0]

    @pl.when(used)
    def _():
        part = _swiglu_partial(x_ref[...], w1_ref[...], w3_ref[...], w2_ref[...])

        @pl.when(f == 0)
        def _():
            acc_ref[...] = part

        @pl.when((f > 0) & (f < nf - 1))
        def _():
            acc_ref[...] += part

        @pl.when(f == nf - 1)
        def _():
            y_ref[...] = (acc_ref[...] + part).astype(y_ref.dtype)

    @pl.when(jnp.logical_not(used) & (f == nf - 1))
    def _():
        y_ref[...] = jnp.zeros_like(y_ref)


def moe_experts(xs, tile_expert, n_used, w1, w3, w2, *, tm, tf):
    rows, d = xs.shape
    ff = w1.shape[2]
    nf = ff // tf
    assert nf >= 2

    def wmap_in(i, f, te, nu):
        live = i < nu[0]
        return (te[i], 0, jnp.where(live, f, nf - 1))

    def wmap_out(i, f, te, nu):
        live = i < nu[0]
        return (te[i], jnp.where(live, f, nf - 1), 0)

    def xmap(i, f, te, nu):
        return (jnp.minimum(i, nu[0] - 1), 0)

    return pl.pallas_call(
        _experts_kernel,
        grid_spec=pltpu.PrefetchScalarGridSpec(
            num_scalar_prefetch=2,
            grid=(rows // tm, nf),
            in_specs=[
                pl.BlockSpec((tm, d), xmap),
                pl.BlockSpec((None, d, tf), wmap_in),
                pl.BlockSpec((None, d, tf), wmap_in),
                pl.BlockSpec((None, tf, d), wmap_out),
            ],
            out_specs=pl.BlockSpec((tm, d), lambda i, f, te, nu: (i, 0)),
            scratch_shapes=[pltpu.VMEM((tm, d), F32)],
        ),
        out_shape=jax.ShapeDtypeStruct((rows, d), BF16),
        compiler_params=_cparams("arbitrary", "arbitrary"),
        name="moe_experts",
    )(tile_expert, n_used, xs, w1, w3, w2)


def _combine_kernel(lpad_ref, loff_ref, base_ref, x_ref, ri_ref, rw_ref, g_ref, mod_ref, ys_ref, o_ref,
                    ybuf, sem, *, mod_row, ne):
    tm, d = x_ref.shape
    rc = ybuf.shape[1]
    i = pl.program_id(0)
    slot = i % 2
    moves = functools.partial(_run_dmas, ne=ne, lpad_ref=lpad_ref, loff_ref=loff_ref, base_ref=base_ref,
                              hbm_ref=ys_ref, to_hbm=False)

    @pl.when(i == 0)
    def _():
        ybuf[...] = jnp.zeros_like(ybuf)
        moves(0, buf=ybuf.at[0], sem=sem.at[0], start=True)

    @pl.when(i + 1 < pl.num_programs(0))
    def _():
        moves(i + 1, buf=ybuf.at[1 - slot], sem=sem.at[1 - slot], start=True)

    moves(i, buf=ybuf.at[slot], sem=sem.at[slot], start=False)
    lo1, lo2 = _local_rows(ri_ref, loff_ref, i, ne)
    lane = lax.broadcasted_iota(jnp.int32, (tm, rc), 1)
    rows = ybuf[slot]
    y1 = jnp.dot(jnp.where(lane == lo1, 1.0, 0.0).astype(BF16), rows, preferred_element_type=F32)
    y2 = jnp.dot(jnp.where(lane == lo2, 1.0, 0.0).astype(BF16), rows, preferred_element_type=F32)
    y = rw_ref[:, 0:1] * y1 + rw_ref[:, 1:2] * y2
    o_ref[...] = x_ref[...] + _mod_slice(mod_ref, mod_row, GA2, d) * _rms(y, g_ref[3:4, :])


def moe_combine(x, norm_g, mods, layer, route_i, route_w, ys, lpad, loff, base, *, mod_row, tm, ne):
    m, d = x.shape
    rc = 2 * tm + ne * RUN_ALIGN
    return pl.pallas_call(
        functools.partial(_combine_kernel, mod_row=mod_row, ne=ne),
        grid_spec=pltpu.PrefetchScalarGridSpec(
            num_scalar_prefetch=3,
            grid=(m // tm,),
            in_specs=[
                pl.BlockSpec((tm, d), lambda i, *_: (i, 0)),
                pl.BlockSpec((tm, 4), lambda i, *_: (i, 0)),
                pl.BlockSpec((tm, 2), lambda i, *_: (i, 0)),
                pl.BlockSpec((None,) + norm_g.shape[1:], lambda i, *_: (layer, 0, 0)),
                pl.BlockSpec((None,) + mods.shape[1:], lambda i, *_: (layer, 0, 0)),
                pl.BlockSpec(memory_space=pl.ANY),
            ],
            out_specs=pl.BlockSpec((tm, d), lambda i, *_: (i, 0)),
            scratch_shapes=[pltpu.VMEM((2, rc, d), BF16), pltpu.SemaphoreType.DMA((2,))],
        ),
        out_shape=jax.ShapeDtypeStruct((m, d), F32),
        input_output_aliases={3: 0},
        compiler_params=_cparams("arbitrary"),
        name="moe_combine",
    )(lpad, loff, base, x, route_i, route_w, norm_g, mods, ys)


def sparse_moe_layer(x, norm_g, mods, layer, w_router, b_router, w1, w3, w2, *, mod_row, tm_tok, tm_exp, tf):
    m, d = x.shape
    ne = w_router.shape[1]
    nt = m // tm_tok
    route_i, route_w, counts = moe_router(x, norm_g, mods, layer, w_router, b_router, mod_row=mod_row, tm=tm_tok)
    counts = counts[:, 0, :]
    lpad = (counts + RUN_ALIGN - 1) // RUN_ALIGN * RUN_ALIGN
    loff = jnp.cumsum(lpad, axis=1) - lpad
    group = jnp.sum(lpad, axis=0)
    gpad = (group + tm_exp - 1) // tm_exp * tm_exp
    ends = jnp.cumsum(gpad)
    base = (ends - gpad)[None, :] + jnp.cumsum(lpad, axis=0) - lpad
    n_tiles = -(-(2 * m + nt * ne * (RUN_ALIGN - 1)) // tm_exp) + ne
    tile_start = jnp.arange(n_tiles, dtype=jnp.int32) * tm_exp
    n_used = (ends[-1] // tm_exp).astype(jnp.int32).reshape(1)
    tile_expert = jnp.sum(tile_start[:, None] >= ends[None, :], axis=1).astype(jnp.int32)
    tile_expert = jnp.minimum(tile_expert, tile_expert[jnp.maximum(n_used[0] - 1, 0)])
    tabs = [a.reshape(-1).astype(jnp.int32) for a in (lpad, loff, base)]
    xs = moe_dispatch(x, norm_g, mods, layer, route_i, *tabs, n_tiles * tm_exp, mod_row=mod_row, tm=tm_tok, ne=ne)
    ys = moe_experts(xs, tile_expert, n_used, w1, w3, w2, tm=tm_exp, tf=tf)
    return moe_combine(x, norm_g, mods, layer, route_i, route_w, ys, *tabs, mod_row=mod_row, tm=tm_tok, ne=ne)


def _rope_tables(seq):
    n = 16
    inv = ROPE_BASE ** (-jnp.arange(n, dtype=F32) / n)
    t = jnp.arange(seq)
    row_ang = (t // GRID_W).astype(F32)[:, None] * inv[None, :]
    col_ang = (t % GRID_W).astype(F32)[:, None] * inv[None, :]
    cos = jnp.concatenate([jnp.cos(row_ang)] * 2 + [jnp.cos(col_ang)] * 2, axis=1)
    sin = jnp.concatenate([-jnp.sin(row_ang), jnp.sin(row_ang), -jnp.sin(col_ang), jnp.sin(col_ang)], axis=1)
    return jnp.tile(cos, (1, 2)), jnp.tile(sin, (1, 2))


def kernel(x, c, ctx, c_ctx, w_ada, b_ada, norm_g, attn_w_qkv, attn_w_o, attn_sink, ret_w_in, ret_w_o,
           ret_log_decay, pool_w, pool_scale, ffn_w1, ffn_w3, ffn_w2, moe_w_router, moe_b_router,
           moe_w1, moe_w3, moe_w2):
    batch, seq, d = x.shape
    assert batch == 1 and c.shape[0] == 1
    depth = w_ada.shape[0]
    lc = ctx.shape[1]
    xl = x.reshape(seq, d)
    xc = ctx.reshape(lc, d)

    cvecs = jnp.zeros((8, d), F32).at[LAT_ROW].set(c[0]).at[CTX_ROW].set(c_ctx)
    mods = ada_table(cvecs, w_ada, b_ada)
    rope = _rope_tables(seq)

    hd = attn_w_o.shape[1]
    dh = hd // ATTN_HEADS
    qkv_scale = jnp.concatenate([jnp.full((hd,), dh ** -0.5, F32),
                                 jnp.ones((attn_w_qkv.shape[2] - hd,), F32)])
    dk = d // RET_HEADS
    ret_scale = jnp.concatenate([jnp.ones((d,), F32), jnp.full((d,), dk ** -0.5, F32),
                                 jnp.ones((ret_w_in.shape[2] - 2 * d,), F32)])

    moe_bf16 = None
    for i in range(depth):
        last = i == depth - 1
        kind, j = i % N_MIXERS, i // N_MIXERS
        proj = functools.partial(norm_mod_matmul, norm_g=norm_g, mods=mods, layer=i, g_row=0, slot=SH1)
        if kind == 0:
            w_qkv = (attn_w_qkv[j] * qkv_scale).astype(BF16)
            w_o = attn_w_o[j].astype(BF16)
            kvd = (w_qkv.shape[1] - hd) // 2
            qkv_l = proj(xl, w=w_qkv, mod_row=LAT_ROW, tm=1024, tn=256, rope=rope, rope_tiles=(hd + kvd) // 256)
            qkv_c = proj(xc, w=w_qkv, mod_row=CTX_ROW, tm=256, tn=256)
            o_l = attention(attn_sink[j], qkv_l, qkv_c, has_local=True)
            xl = outproj_residual(xl, norm_g, mods, i, w_o, [o_l], g_row=1, mod_row=LAT_ROW, slot=GA1, tm=512)
            if not last:
                o_c = attention(attn_sink[j], qkv_c, qkv_c, has_local=False)
                xc = outproj_residual(xc, norm_g, mods, i, w_o, [o_c], g_row=1, mod_row=CTX_ROW, slot=GA1, tm=256)
        elif kind == 1:
            w_in = (ret_w_in[j] * ret_scale).astype(BF16)
            w_o = ret_w_o[j].astype(BF16)
            p_c = proj(xc, w=w_in, mod_row=CTX_ROW, tm=256, tn=1024)
            p_l = proj(xl, w=w_in, mod_row=LAT_ROW, tm=1024, tn=1024)
            s0 = jnp.zeros((RET_HEADS, dk, 2 * dk), F32)
            tabs_f = retention_tables(ret_log_decay[j, 0], min(RET_CHUNK, lc), False)
            tabs_b = retention_tables(ret_log_decay[j, 1], min(RET_CHUNK, lc), True)
            zf_c, s_f = retention_scan(p_c, tabs_f, s0, reverse=False)
            zb_c, s_b = retention_scan(p_c, tabs_b, s0, reverse=True)
            zf_l, _ = retention_scan(p_l, tabs_f, s_f, reverse=False)
            zb_l, _ = retention_scan(p_l, tabs_b, s_b, reverse=True)
            xl = outproj_residual(xl, norm_g, mods, i, w_o, [zf_l, zb_l], g_row=1, mod_row=LAT_ROW, slot=GA1, tm=512)
            if not last:
                xc = outproj_residual(xc, norm_g, mods, i, w_o, [zf_c, zb_c], g_row=1, mod_row=CTX_ROW, slot=GA1, tm=256)
        else:
            w_p = pool_w[j].astype(BF16)
            xl = pool_layer(xl, norm_g, mods, i, w_p, pool_scale[j], mod_row=LAT_ROW, tm=512)
            if not last:
                xc = pool_layer(xc, norm_g, mods, i, w_p, pool_scale[j], mod_row=CTX_ROW, tm=256)

        f = i // 2
        if i % 2 == 0:
            w1, w3, w2 = ffn_w1[f].astype(BF16), ffn_w3[f].astype(BF16), ffn_w2[f].astype(BF16)
            tf = w1.shape[1] // 2
            steps = (seq // min(512, seq)) * 2
            ne = moe_w1.shape[1]
            if not last and steps % ne == 0 and d % (steps // ne * 16) == 0:
                xl, moe_bf16 = ffn_layer(xl, norm_g, mods, i, w1, w3, w2, mod_row=LAT_ROW, tm=512, tf=tf,
                                         cast=((moe_w1, moe_w3, moe_w2), (i + 1) // 2))
            else:
                xl = ffn_layer(xl, norm_g, mods, i, w1, w3, w2, mod_row=LAT_ROW, tm=512, tf=tf)
            if not last:
                xc = ffn_layer(xc, norm_g, mods, i, w1, w3, w2, mod_row=CTX_ROW, tm=256, tf=tf)
        else:
            if moe_bf16 is None:
                moe_bf16 = moe_w1[f].astype(BF16), moe_w3[f].astype(BF16), moe_w2[f].astype(BF16)
            (w1, w3, w2), moe_bf16 = moe_bf16, None
            tf = w1.shape[2] // 2
            xl = sparse_moe_layer(xl, norm_g, mods, i, moe_w_router[f], moe_b_router[f], w1, w3, w2,
                                  mod_row=LAT_ROW, tm_tok=min(512, seq), tm_exp=min(512, seq), tf=tf)
            if not last:
                xc = moe_layer(xc, norm_g, mods, i, moe_w_router[f], moe_b_router[f], w1, w3, w2,
                               mod_row=CTX_ROW, tm=256, tf=tf)
    return xl.reshape(batch, seq, d)
```

```python
import functools

import jax
import jax.numpy as jnp
from jax import lax
from jax.experimental import pallas as pl
from jax.experimental.pallas import tpu as pltpu

F32 = jnp.float32
BF16 = jnp.bfloat16

EPS = 1e-6
NEG_INF = -1e30
LANES = 128
VMEM_LIMIT = 56 * 1024 * 1024

GRID_W = 64
N_MIXERS = 3
ATTN_HEADS = 16
ATTN_KV_HEADS = 4
WINDOW = 128
ROPE_BASE = 10000.0
RET_HEADS = 4
RET_CHUNK = 256
POOL_WINDOWS = (2, 4, 8, 16)
POOL_HALO = 8
N_EXPERTS = 8

SH1, SC1, GA1, SH2, SC2, GA2 = range(6)
LAT_ROW, CTX_ROW = 0, 1


def _cparams(*sem):
    return pltpu.CompilerParams(dimension_semantics=sem, vmem_limit_bytes=VMEM_LIMIT)


def _rms(x, g):
    return x * lax.rsqrt(jnp.mean(x * x, axis=-1, keepdims=True) + EPS) * g


def _mod_slice(mod_ref, row, slot, d):
    return mod_ref[row:row + 1, slot * d:(slot + 1) * d]


def _norm_mod(x, g, shift, scale):
    return _rms(x, g) * (1.0 + scale) + shift


def _ada_kernel(c_ref, w_ref, b_ref, o_ref):
    c = c_ref[...]
    s = c * jax.nn.sigmoid(c)
    o_ref[...] = jnp.dot(s, w_ref[...], preferred_element_type=F32,
                         precision=lax.Precision.HIGHEST) + b_ref[...]


def ada_table(cvecs, w_ada, b_ada):
    depth, d, n = w_ada.shape
    tn = 1536
    return pl.pallas_call(
        _ada_kernel,
        grid=(depth, n // tn),
        in_specs=[
            pl.BlockSpec((8, d), lambda i, j: (0, 0)),
            pl.BlockSpec((None, d, tn), lambda i, j: (i, 0, j)),
            pl.BlockSpec((None, 1, tn), lambda i, j: (i, 0, j)),
        ],
        out_specs=pl.BlockSpec((None, 8, tn), lambda i, j: (i, 0, j)),
        out_shape=jax.ShapeDtypeStruct((depth, 8, n), F32),
        compiler_params=_cparams("parallel", "parallel"),
        name="ada_table",
    )(cvecs, w_ada, b_ada.reshape(depth, 1, n))


def _rope(a, cos, sin):
    lane = lax.broadcasted_iota(jnp.int32, (a.shape[0], LANES), 1)
    first = (lane % 32) < 16
    outs = []
    for cb in range(a.shape[1] // LANES):
        blk = a[:, cb * LANES:(cb + 1) * LANES]
        partner = jnp.where(first, pltpu.roll(blk, LANES - 16, 1), pltpu.roll(blk, 16, 1))
        outs.append(blk * cos + partner * sin)
    return jnp.concatenate(outs, axis=1)


def _proj_kernel(x_ref, g_ref, mod_ref, w_ref, o_ref, h_ref, *, g_row, mod_row, slot):
    d = x_ref.shape[1]
    j = pl.program_id(1)

    @pl.when(j == 0)
    def _():
        h = _norm_mod(x_ref[...], g_ref[g_row:g_row + 1, :],
                      _mod_slice(mod_ref, mod_row, slot, d), _mod_slice(mod_ref, mod_row, slot + 1, d))
        h_ref[...] = h.astype(BF16)

    o_ref[...] = jnp.dot(h_ref[...], w_ref[...], preferred_element_type=F32).astype(o_ref.dtype)


def norm_mod_matmul(x, norm_g, mods, layer, w, *, g_row, mod_row, slot, tm, tn):
    m, d = x.shape
    n = w.shape[1]
    tm = min(tm, m)
    return pl.pallas_call(
        functools.partial(_proj_kernel, g_row=g_row, mod_row=mod_row, slot=slot),
        grid=(m // tm, n // tn),
        in_specs=[
            pl.BlockSpec((tm, d), lambda i, j: (i, 0)),
            pl.BlockSpec((None,) + norm_g.shape[1:], lambda i, j: (layer, 0, 0)),
            pl.BlockSpec((None,) + mods.shape[1:], lambda i, j: (layer, 0, 0)),
            pl.BlockSpec((d, tn), lambda i, j: (0, j)),
        ],
        out_specs=pl.BlockSpec((tm, tn), lambda i, j: (i, j)),
        out_shape=jax.ShapeDtypeStruct((m, n), BF16),
        scratch_shapes=[pltpu.VMEM((tm, d), BF16)],
        compiler_params=_cparams("parallel", "arbitrary"),
        name="norm_mod_matmul",
    )(x, norm_g, mods, w)


QKV_CHUNK = 256


def _qkv_kernel(x_ref, g_ref, mod_ref, w_ref, *rest, mod_row, rope_cols):
    if rope_cols:
        cos_ref, sin_ref, o_ref = rest
        cos, sin = cos_ref[...], sin_ref[...]
    else:
        (o_ref,) = rest
    d = x_ref.shape[1]
    h = _norm_mod(x_ref[...], g_ref[0:1, :], _mod_slice(mod_ref, mod_row, SH1, d),
                  _mod_slice(mod_ref, mod_row, SC1, d)).astype(BF16)
    for c0 in range(0, o_ref.shape[1], QKV_CHUNK):
        acc = jnp.dot(h, w_ref[:, c0:c0 + QKV_CHUNK], preferred_element_type=F32)
        if c0 < rope_cols:
            acc = _rope(acc, cos, sin)
        o_ref[:, c0:c0 + QKV_CHUNK] = acc.astype(o_ref.dtype)


def qkv_projection(x, norm_g, mods, layer, w, *, mod_row, tm, rope=None, rope_cols=0):
    m, d = x.shape
    n = w.shape[1]
    tm = min(tm, m)
    assert n % QKV_CHUNK == 0 and rope_cols % QKV_CHUNK == 0
    in_specs = [
        pl.BlockSpec((tm, d), lambda i: (i, 0)),
        pl.BlockSpec((None,) + norm_g.shape[1:], lambda i: (layer, 0, 0)),
        pl.BlockSpec((None,) + mods.shape[1:], lambda i: (layer, 0, 0)),
        pl.BlockSpec((d, n), lambda i: (0, 0)),
    ]
    args = [x, norm_g, mods, w]
    if rope_cols:
        in_specs += [pl.BlockSpec((tm, LANES), lambda i: (i, 0))] * 2
        args += list(rope)
    return pl.pallas_call(
        functools.partial(_qkv_kernel, mod_row=mod_row, rope_cols=rope_cols),
        grid=(m // tm,),
        in_specs=in_specs,
        out_specs=pl.BlockSpec((tm, n), lambda i: (i, 0)),
        out_shape=jax.ShapeDtypeStruct((m, n), BF16),
        compiler_params=_cparams("parallel"),
        name="qkv_projection",
    )(*args)


def _outproj_kernel(x_ref, g_ref, mod_ref, w_ref, *rest, n_y, g_row, mod_row, slot):
    y_refs, o_ref = rest[:n_y], rest[n_y]
    d = x_ref.shape[1]
    y = y_refs[0][...]
    if n_y == 2:
        y = (y.astype(F32) + y_refs[1][...].astype(F32)).astype(BF16)
    t = jnp.dot(y, w_ref[...], preferred_element_type=F32)
    gate = _mod_slice(mod_ref, mod_row, slot, d)
    o_ref[...] = x_ref[...] + gate * _rms(t, g_ref[g_row:g_row + 1, :])


def outproj_residual(x, norm_g, mods, layer, w, ys, *, g_row, mod_row, slot, tm):
    m, d = x.shape
    k = w.shape[0]
    tm = min(tm, m)
    in_specs = [
        pl.BlockSpec((tm, d), lambda i: (i, 0)),
        pl.BlockSpec((None,) + norm_g.shape[1:], lambda i: (layer, 0, 0)),
        pl.BlockSpec((None,) + mods.shape[1:], lambda i: (layer, 0, 0)),
        pl.BlockSpec((k, d), lambda i: (0, 0)),
    ] + [pl.BlockSpec((tm, k), lambda i: (i, 0))] * len(ys)
    return pl.pallas_call(
        functools.partial(_outproj_kernel, n_y=len(ys), g_row=g_row, mod_row=mod_row, slot=slot),
        grid=(m // tm,),
        in_specs=in_specs,
        out_specs=pl.BlockSpec((tm, d), lambda i: (i, 0)),
        out_shape=jax.ShapeDtypeStruct((m, d), F32),
        compiler_params=_cparams("parallel"),
        name="outproj_residual",
    )(x, norm_g, mods, w, *ys)


def _dot_nt(a, b):
    return lax.dot_general(a, b, (((1,), (1,)), ((), ())), preferred_element_type=F32)


def _attn_kernel(sink_ref, q_ref, *rest, tq, seq, kv_heads, group, dh, has_local):
    if has_local:
        kp_ref, kq_ref, kn_ref, vp_ref, vq_ref, vn_ref, kc_ref, vc_ref, o_ref = rest
    else:
        kc_ref, vc_ref, o_ref = rest
    b = pl.program_id(0)
    rows = group * tq
    head_of_row = lax.broadcasted_iota(jnp.int32, (rows, 1), 0) // tq
    if has_local:
        r = lax.broadcasted_iota(jnp.int32, (rows, 3 * tq), 0) % tq
        c = lax.broadcasted_iota(jnp.int32, (rows, 3 * tq), 1)
        kpos = (b - 1) * tq + c
        rel = c - tq - r
        valid = (jnp.abs(rel) <= WINDOW) & (kpos >= 0) & (kpos < seq)
    for j in range(kv_heads):
        hs = slice(j * dh, (j + 1) * dh)
        qg = jnp.concatenate(
            [q_ref[:, (j * group + g) * dh:(j * group + g + 1) * dh] for g in range(group)], axis=0)
        sink = jnp.full((rows, 1), sink_ref[j * group], F32)
        for g in range(1, group):
            sink = jnp.where(head_of_row == g, sink_ref[j * group + g], sink)
        s_ctx = _dot_nt(qg, kc_ref[:, hs])
        m = jnp.maximum(jnp.max(s_ctx, axis=-1, keepdims=True), sink)
        if has_local:
            k_loc = jnp.concatenate([kp_ref[:, hs], kq_ref[:, hs], kn_ref[:, hs]], axis=0)
            v_loc = jnp.concatenate([vp_ref[:, hs], vq_ref[:, hs], vn_ref[:, hs]], axis=0)
            s_loc = jnp.where(valid, _dot_nt(qg, k_loc), NEG_INF)
            m = jnp.maximum(m, jnp.max(s_loc, axis=-1, keepdims=True))
        p_ctx = jnp.exp(s_ctx - m)
        den = jnp.sum(p_ctx, axis=-1, keepdims=True) + jnp.exp(sink - m)
        o = jnp.dot(p_ctx.astype(BF16), vc_ref[:, hs], preferred_element_type=F32)
        if has_local:
            p_loc = jnp.exp(s_loc - m)
            den = den + jnp.sum(p_loc, axis=-1, keepdims=True)
            o = o + jnp.dot(p_loc.astype(BF16), v_loc, preferred_element_type=F32)
        o = o / den
        for g in range(group):
            h = j * group + g
            o_ref[:, h * dh:(h + 1) * dh] = o[g * tq:(g + 1) * tq].astype(o_ref.dtype)


def attention(sink, qkv, qkv_ctx, *, has_local):
    s = qkv.shape[0]
    dh = qkv.shape[1] // (ATTN_HEADS + 2 * ATTN_KV_HEADS)
    hd = ATTN_HEADS * dh
    kvd = ATTN_KV_HEADS * dh
    kcol, vcol = hd // kvd, hd // kvd + 1
    tq = WINDOW
    nb = s // tq
    lc = qkv_ctx.shape[0]
    in_specs = [
        pl.BlockSpec(memory_space=pltpu.SMEM),
        pl.BlockSpec((tq, hd), lambda b: (b, 0)),
    ]
    args = [sink, qkv]
    if has_local:
        for col in (kcol, vcol):
            in_specs += [
                pl.BlockSpec((tq, kvd), lambda b, col=col: (jnp.maximum(b - 1, 0), col)),
                pl.BlockSpec((tq, kvd), lambda b, col=col: (b, col)),
                pl.BlockSpec((tq, kvd), lambda b, col=col: (jnp.minimum(b + 1, nb - 1), col)),
            ]
            args += [qkv] * 3
    in_specs += [pl.BlockSpec((lc, kvd), lambda b: (0, kcol)), pl.BlockSpec((lc, kvd), lambda b: (0, vcol))]
    args += [qkv_ctx, qkv_ctx]
    return pl.pallas_call(
        functools.partial(_attn_kernel, tq=tq, seq=s, kv_heads=ATTN_KV_HEADS,
                          group=ATTN_HEADS // ATTN_KV_HEADS, dh=dh, has_local=has_local),
        grid=(nb,),
        in_specs=in_specs,
        out_specs=pl.BlockSpec((tq, hd), lambda b: (b, 0)),
        out_shape=jax.ShapeDtypeStruct((s, hd), BF16),
        compiler_params=_cparams("parallel"),
        name="attention",
    )(*args)


def _ret_kernel(q_ref, k_ref, v_ref, gate_ref, intra_ref, qdec_ref, kdec_ref, cdec_ref, s0_ref,
                z_ref, sfin_ref, state_ref, *, heads, dk, dv):
    i = pl.program_id(0)

    @pl.when(i == 0)
    def _():
        state_ref[...] = s0_ref[...]

    for h in range(heads):
        q = q_ref[:, h * dk:(h + 1) * dk]
        k = k_ref[:, h * dk:(h + 1) * dk]
        v = v_ref[:, h * dv:(h + 1) * dv]
        state = state_ref[h]
        sc = _dot_nt(q, k) * intra_ref[h]
        qd = (q.astype(F32) * qdec_ref[h]).astype(BF16)
        o = (jnp.dot(sc.astype(BF16), v, preferred_element_type=F32)
             + jnp.dot(qd, state.astype(BF16), preferred_element_type=F32))
        kd = (k.astype(F32) * kdec_ref[h]).astype(BF16)
        state_ref[h] = state * cdec_ref[h] + lax.dot_general(
            kd, v, (((0,), (0,)), ((), ())), preferred_element_type=F32)
        mu = jnp.mean(o, axis=-1, keepdims=True)
        oc = o - mu
        var = jnp.mean(oc * oc, axis=-1, keepdims=True)
        gate = gate_ref[:, h * dv:(h + 1) * dv].astype(F32)
        z = gate * jax.nn.sigmoid(gate) * (oc * lax.rsqrt(var + EPS))
        z_ref[:, h * dv:(h + 1) * dv] = z.astype(z_ref.dtype)

    @pl.when(i == pl.num_programs(0) - 1)
    def _():
        sfin_ref[...] = state_ref[...]


def retention_scan(proj, tables, s0, *, reverse):
    m, n = proj.shape
    d = n // 8
    heads = RET_HEADS
    dk, dv = d // heads, 2 * d // heads
    c = min(RET_CHUNK, m)
    nc = m // c
    intra, qdec, kdec, cdec = tables

    def row(i):
        return nc - 1 - i if reverse else i

    gate_blk = 3 if reverse else 2
    full = lambda a: pl.BlockSpec(a.shape, lambda i: (0,) * a.ndim)
    z, sfin = pl.pallas_call(
        functools.partial(_ret_kernel, heads=heads, dk=dk, dv=dv),
        grid=(nc,),
        in_specs=[
            pl.BlockSpec((c, d), lambda i: (row(i), 0)),
            pl.BlockSpec((c, d), lambda i: (row(i), 1)),
            pl.BlockSpec((c, 2 * d), lambda i: (row(i), 1)),
            pl.BlockSpec((c, 2 * d), lambda i: (row(i), gate_blk)),
            full(intra), full(qdec), full(kdec), full(cdec), full(s0),
        ],
        out_specs=[
            pl.BlockSpec((c, 2 * d), lambda i: (row(i), 0)),
            full(s0),
        ],
        out_shape=[jax.ShapeDtypeStruct((m, 2 * d), BF16), jax.ShapeDtypeStruct(s0.shape, F32)],
        scratch_shapes=[pltpu.VMEM(s0.shape, F32)],
        compiler_params=_cparams("arbitrary"),
        name="retention_scan",
    )(proj, proj, proj, proj, intra, qdec, kdec, cdec, s0)
    return z, sfin


def retention_tables(log_decay_row, c, reverse):
    lg = -jnp.exp(log_decay_row.astype(F32))
    idx = jnp.arange(c, dtype=F32)
    diff = idx[:, None] - idx[None, :]
    if reverse:
        diff = -diff
    intra = jnp.where(diff >= 0, jnp.exp(lg[:, None, None] * jnp.maximum(diff, 0.0)), 0.0)
    fwd_idx = (c - 1.0 - idx) if reverse else idx
    qdec = jnp.exp(lg[:, None] * (fwd_idx + 1.0))[:, :, None]
    kdec = jnp.exp(lg[:, None] * (c - 1.0 - fwd_idx))[:, :, None]
    cdec = jnp.exp(lg * c)[:, None, None]
    return intra, qdec, kdec, cdec


def _pool_kernel(x_ref, xp_ref, xn_ref, g_ref, mod_ref, w_ref, ps_ref, o_ref, h_ref, *, mod_row, seq):
    tm, d = x_ref.shape
    halo = POOL_HALO
    i = pl.program_id(0)
    g0 = g_ref[0:1, :]
    shift = _mod_slice(mod_ref, mod_row, SH1, d)
    scale = _mod_slice(mod_ref, mod_row, SC1, d)
    hp = _norm_mod(xp_ref[...], g0, shift, scale)
    hn = _norm_mod(xn_ref[...], g0, shift, scale)
    h_ref[0:halo, :] = jnp.where(i > 0, hp, 0.0)
    h_ref[halo:halo + tm, :] = _norm_mod(x_ref[...], g0, shift, scale)
    h_ref[halo + tm:, :] = jnp.where(i < pl.num_programs(0) - 1, hn, 0.0)

    t = i * tm + lax.broadcasted_iota(jnp.int32, (tm, 1), 0)
    gw = d // len(POOL_WINDOWS)
    ys = []
    for g, w in enumerate(POOL_WINDOWS):
        cols = slice(g * gw, (g + 1) * gw)
        tot = h_ref[halo - w // 2:halo - w // 2 + tm, cols]
        for off in range(-w // 2 + 1, w // 2):
            tot = tot + h_ref[halo + off:halo + off + tm, cols]
        cnt = (jnp.minimum(t + w // 2, seq) - jnp.maximum(t - w // 2, 0)).astype(F32)
        dm = tot / cnt - h_ref[halo:halo + tm, cols]
        ys.append(jnp.dot(dm.astype(BF16), w_ref[g], preferred_element_type=F32))
    y = jnp.concatenate(ys, axis=1) * ps_ref[...]
    o_ref[...] = x_ref[...] + _mod_slice(mod_ref, mod_row, GA1, d) * _rms(y, g_ref[1:2, :])


def pool_layer(x, norm_g, mods, layer, w_pool, pool_scale, *, mod_row, tm):
    m, d = x.shape
    tm = min(tm, m)
    nt = m // tm
    hb = tm // POOL_HALO
    return pl.pallas_call(
        functools.partial(_pool_kernel, mod_row=mod_row, seq=m),
        grid=(nt,),
        in_specs=[
            pl.BlockSpec((tm, d), lambda i: (i, 0)),
            pl.BlockSpec((POOL_HALO, d), lambda i: (jnp.maximum(i * hb - 1, 0), 0)),
            pl.BlockSpec((POOL_HALO, d), lambda i: (jnp.minimum((i + 1) * hb, nt * hb - 1), 0)),
            pl.BlockSpec((None,) + norm_g.shape[1:], lambda i: (layer, 0, 0)),
            pl.BlockSpec((None,) + mods.shape[1:], lambda i: (layer, 0, 0)),
            pl.BlockSpec(w_pool.shape, lambda i: (0, 0, 0)),
            pl.BlockSpec((1, d), lambda i: (0, 0)),
        ],
        out_specs=pl.BlockSpec((tm, d), lambda i: (i, 0)),
        out_shape=jax.ShapeDtypeStruct((m, d), F32),
        scratch_shapes=[pltpu.VMEM((tm + 2 * POOL_HALO, d), F32)],
        compiler_params=_cparams("parallel"),
        name="pool_layer",
    )(x, x, x, norm_g, mods, w_pool, pool_scale.reshape(1, d))


def _swiglu_partial(h, w1, w3, w2):
    a = jnp.dot(h, w1, preferred_element_type=F32)
    b = jnp.dot(h, w3, preferred_element_type=F32)
    u = (a * jax.nn.sigmoid(a) * b).astype(BF16)
    return jnp.dot(u, w2, preferred_element_type=F32)


def _ffn_kernel(x_ref, g_ref, mod_ref, w1_ref, w3_ref, w2_ref, *rest, mod_row, n_cast):
    cast_src, o_ref, cast_dst = rest[:n_cast], rest[n_cast], rest[n_cast + 1:2 * n_cast + 1]
    h_ref, acc_ref = rest[2 * n_cast + 1:]
    d = x_ref.shape[1]
    f = pl.program_id(1)
    for src, dst in zip(cast_src, cast_dst):
        dst[...] = src[...].astype(dst.dtype)

    @pl.when(f == 0)
    def _():
        h = _norm_mod(x_ref[...], g_ref[2:3, :], _mod_slice(mod_ref, mod_row, SH2, d),
                      _mod_slice(mod_ref, mod_row, SC2, d))
        h_ref[...] = h.astype(BF16)

    part = _swiglu_partial(h_ref[...], w1_ref[...], w3_ref[...], w2_ref[...])

    @pl.when(f == 0)
    def _():
        acc_ref[...] = part

    @pl.when(f > 0)
    def _():
        acc_ref[...] += part

    @pl.when(f == pl.num_programs(1) - 1)
    def _():
        o_ref[...] = x_ref[...] + _mod_slice(mod_ref, mod_row, GA2, d) * _rms(acc_ref[...], g_ref[3:4, :])


def ffn_layer(x, norm_g, mods, layer, w1, w3, w2, *, mod_row, tm, tf, cast=None):
    m, d = x.shape
    ff = w1.shape[1]
    tm = min(tm, m)
    nf = ff // tf
    steps = (m // tm) * nf
    in_specs = [
        pl.BlockSpec((tm, d), lambda i, f: (i, 0)),
        pl.BlockSpec((None,) + norm_g.shape[1:], lambda i, f: (layer, 0, 0)),
        pl.BlockSpec((None,) + mods.shape[1:], lambda i, f: (layer, 0, 0)),
        pl.BlockSpec((d, tf), lambda i, f: (0, f)),
        pl.BlockSpec((d, tf), lambda i, f: (0, f)),
        pl.BlockSpec((tf, d), lambda i, f: (f, 0)),
    ]
    out_specs = [pl.BlockSpec((tm, d), lambda i, f: (i, 0))]
    out_shape = [jax.ShapeDtypeStruct((m, d), F32)]
    cast_arrays, cast_idx = cast if cast is not None else ((), 0)
    for a in cast_arrays:
        _, ne, r, c = a.shape
        per_expert = steps // ne
        assert per_expert * ne == steps and r % (per_expert * 16) == 0
        rows = r // per_expert
        in_specs.append(pl.BlockSpec(
            (None, None, rows, c),
            lambda i, f, pe=per_expert: (cast_idx, (i * nf + f) // pe, (i * nf + f) % pe, 0)))
        out_specs.append(pl.BlockSpec(
            (None, rows, c), lambda i, f, pe=per_expert: ((i * nf + f) // pe, (i * nf + f) % pe, 0)))
        out_shape.append(jax.ShapeDtypeStruct((ne, r, c), BF16))
    outs = pl.pallas_call(
        functools.partial(_ffn_kernel, mod_row=mod_row, n_cast=len(cast_arrays)),
        grid=(m // tm, nf),
        in_specs=in_specs,
        out_specs=out_specs,
        out_shape=out_shape,
        scratch_shapes=[pltpu.VMEM((tm, d), BF16), pltpu.VMEM((tm, d), F32)],
        compiler_params=_cparams("arbitrary", "arbitrary"),
        name="ffn_layer",
    )(x, norm_g, mods, w1, w3, w2, *cast_arrays)
    return outs[0] if cast is None else (outs[0], tuple(outs[1:]))


RUN_ALIGN = 16
RUN_BITS = tuple(range(9, 3, -1))
EXPERT_ROWS = 512
TAIL_BITS = tuple(range(9, 3, -1))


def _split_dma(length, src, src_row, dst, dst_row, sem, bits, *, start, src_advances=True):
    for b in bits:
        size = 1 << b
        off = (length >> (b + 1)) << (b + 1)

        @pl.when(((length >> b) & 1) == 1)
        def _():
            s_row = pl.multiple_of(src_row + off, RUN_ALIGN) if src_advances else src_row
            cp = pltpu.make_async_copy(src.at[pl.ds(s_row, size)],
                                       dst.at[pl.ds(pl.multiple_of(dst_row + off, RUN_ALIGN), size)], sem)
            if start:
                cp.start()
            else:
                cp.wait()


def _run_dmas(tile, ne, lpad_ref, loff_ref, base_ref, buf, hbm_ref, sem, *, to_hbm, start):
    for e in range(ne):
        length = lpad_ref[tile * ne + e]
        lo = loff_ref[tile * ne + e]
        gb = base_ref[tile * ne + e]
        if to_hbm:
            _split_dma(length, buf, lo, hbm_ref, gb, sem, RUN_BITS, start=start)
        else:
            _split_dma(length, hbm_ref, gb, buf, lo, sem, RUN_BITS, start=start)


def _local_rows(ri_ref, loff_ref, tile, ne):
    e1, e2 = ri_ref[:, 0:1], ri_ref[:, 1:2]
    lo1, lo2 = ri_ref[:, 2:3], ri_ref[:, 3:4]
    for e in range(ne):
        off = loff_ref[tile * ne + e]
        lo1 = lo1 + jnp.where(e1 == e, off, 0)
        lo2 = lo2 + jnp.where(e2 == e, off, 0)
    return lo1, lo2


def _router_kernel(x_ref, g_ref, mod_ref, wr_ref, br_ref, ri_ref, rw_ref, cnt_ref, *, mod_row):
    tm, d = x_ref.shape
    ne = wr_ref.shape[1]
    h = _norm_mod(x_ref[...], g_ref[2:3, :], _mod_slice(mod_ref, mod_row, SH2, d),
                  _mod_slice(mod_ref, mod_row, SC2, d))
    logits = jnp.dot(h, wr_ref[...], preferred_element_type=F32,
                     precision=lax.Precision.HIGHEST) + br_ref[...]
    lane = lax.broadcasted_iota(jnp.int32, logits.shape, 1)
    v1 = jnp.max(logits, axis=-1, keepdims=True)
    i1 = jnp.min(jnp.where(logits == v1, lane, ne), axis=-1, keepdims=True)
    rest = jnp.where(lane == i1, -jnp.inf, logits)
    v2 = jnp.max(rest, axis=-1, keepdims=True)
    i2 = jnp.min(jnp.where(rest == v2, lane, ne), axis=-1, keepdims=True)
    e2 = jnp.exp(v2 - v1)
    den = 1.0 + e2
    oh1 = (lane == i1).astype(F32)
    oh2 = (lane == i2).astype(F32)
    both = oh1 + oh2
    r = lax.broadcasted_iota(jnp.int32, (tm, tm), 0)
    c = lax.broadcasted_iota(jnp.int32, (tm, tm), 1)
    tri = jnp.where(c < r, 1.0, 0.0).astype(BF16)
    rank_all = jnp.dot(tri, both.astype(BF16), preferred_element_type=F32)
    rank1 = jnp.sum(rank_all * oh1, axis=-1, keepdims=True).astype(jnp.int32)
    rank2 = jnp.sum(rank_all * oh2, axis=-1, keepdims=True).astype(jnp.int32)
    col = lax.broadcasted_iota(jnp.int32, ri_ref.shape, 1)
    ri_ref[...] = jnp.where(col == 0, i1, jnp.where(col == 1, i2, jnp.where(col == 2, rank1, rank2)))
    colw = lax.broadcasted_iota(jnp.int32, rw_ref.shape, 1)
    rw_ref[...] = jnp.where(colw == 0, 1.0 / den, e2 / den)
    cnt_ref[...] = jnp.sum(both, axis=0, keepdims=True).astype(jnp.int32)


def moe_router(x, norm_g, mods, layer, w_router, b_router, *, mod_row, tm):
    m, d = x.shape
    ne = w_router.shape[1]
    return pl.pallas_call(
        functools.partial(_router_kernel, mod_row=mod_row),
        grid=(m // tm,),
        in_specs=[
            pl.BlockSpec((tm, d), lambda i: (i, 0)),
            pl.BlockSpec((None,) + norm_g.shape[1:], lambda i: (layer, 0, 0)),
            pl.BlockSpec((None,) + mods.shape[1:], lambda i: (layer, 0, 0)),
            pl.BlockSpec((d, ne), lambda i: (0, 0)),
            pl.BlockSpec((1, ne), lambda i: (0, 0)),
        ],
        out_specs=[
            pl.BlockSpec((tm, 4), lambda i: (i, 0)),
            pl.BlockSpec((tm, 2), lambda i: (i, 0)),
            pl.BlockSpec((None, 1, ne), lambda i: (i, 0, 0)),
        ],
        out_shape=[
            jax.ShapeDtypeStruct((m, 4), jnp.int32),
            jax.ShapeDtypeStruct((m, 2), F32),
            jax.ShapeDtypeStruct((m // tm, 1, ne), jnp.int32),
        ],
        compiler_params=_cparams("parallel"),
        name="moe_router",
    )(x, norm_g, mods, w_router, b_router.reshape(1, ne))


def _dispatch_kernel(lpad_ref, loff_ref, base_ref, tlen_ref, tstart_ref, nu_ref, x_ref, g_ref, mod_ref, ri_ref,
                     *rest, mod_row, ne, tile0, first):
    xs_ref, cbuf, zbuf, sem = rest if first else rest[1:]
    tm, d = x_ref.shape
    rc = cbuf.shape[1]
    i = pl.program_id(0)
    last = i == pl.num_programs(0) - 1
    slot = i % 2

    def zero_tails(start):
        for e in range(ne):
            _split_dma(tlen_ref[e], zbuf, 0, xs_ref, tstart_ref[e], sem.at[2], TAIL_BITS, start=start,
                       src_advances=False)

        def unused_tile(t, carry):
            cp = pltpu.make_async_copy(
                zbuf, xs_ref.at[pl.ds(pl.multiple_of(t * EXPERT_ROWS, EXPERT_ROWS), EXPERT_ROWS)], sem.at[2])
            if start:
                cp.start()
            else:
                cp.wait()
            return carry

        lax.fori_loop(nu_ref[0], xs_ref.shape[0] // EXPERT_ROWS, unused_tile, 0)

    if first:
        @pl.when(i == 0)
        def _():
            zbuf[...] = jnp.zeros_like(zbuf)
            zero_tails(True)

    h = _norm_mod(x_ref[...], g_ref[2:3, :], _mod_slice(mod_ref, mod_row, SH2, d),
                  _mod_slice(mod_ref, mod_row, SC2, d)).astype(BF16)
    lo1, lo2 = _local_rows(ri_ref, loff_ref, i + tile0, ne)
    lane = lax.broadcasted_iota(jnp.int32, (tm, rc), 1)
    onehot = jnp.where((lane == lo1) | (lane == lo2), 1.0, 0.0).astype(BF16)
    sorted_rows = lax.dot_general(onehot, h, (((0,), (0,)), ((), ())), preferred_element_type=F32)
    cbuf[slot] = sorted_rows.astype(BF16)
    moves = functools.partial(_run_dmas, ne=ne, lpad_ref=lpad_ref, loff_ref=loff_ref, base_ref=base_ref,
                              hbm_ref=xs_ref, to_hbm=True)
    moves(i + tile0, buf=cbuf.at[slot], sem=sem.at[slot], start=True)

    @pl.when(i > 0)
    def _():
        moves(i + tile0 - 1, buf=cbuf.at[1 - slot], sem=sem.at[1 - slot], start=False)

    @pl.when(last)
    def _():
        moves(i + tile0, buf=cbuf.at[slot], sem=sem.at[slot], start=False)
        if first:
            zero_tails(False)


def moe_dispatch(x, norm_g, mods, layer, route_i, tabs, rows, xs=None, *, mod_row, tm, ne, tile0):
    m, d = x.shape
    rc = 2 * tm + ne * RUN_ALIGN
    first = xs is None
    in_specs = [
        pl.BlockSpec((tm, d), lambda i, *_: (i, 0)),
        pl.BlockSpec((None,) + norm_g.shape[1:], lambda i, *_: (layer, 0, 0)),
        pl.BlockSpec((None,) + mods.shape[1:], lambda i, *_: (layer, 0, 0)),
        pl.BlockSpec((tm, 4), lambda i, *_: (i, 0)),
    ]
    args = [*tabs, x, norm_g, mods, route_i]
    if not first:
        in_specs.append(pl.BlockSpec(memory_space=pl.ANY))
        args.append(xs)
    return pl.pallas_call(
        functools.partial(_dispatch_kernel, mod_row=mod_row, ne=ne, tile0=tile0, first=first),
        grid_spec=pltpu.PrefetchScalarGridSpec(
            num_scalar_prefetch=len(tabs),
            grid=(m // tm,),
            in_specs=in_specs,
            out_specs=pl.BlockSpec(memory_space=pl.ANY),
            scratch_shapes=[pltpu.VMEM((2, rc, d), BF16), pltpu.VMEM((EXPERT_ROWS, d), BF16),
                            pltpu.SemaphoreType.DMA((3,))],
        ),
        out_shape=jax.ShapeDtypeStruct((rows, d), BF16),
        input_output_aliases={} if first else {len(args) - 1: 0},
        compiler_params=_cparams("arbitrary"),
        name="moe_dispatch",
    )(*args)


def _experts_kernel(te_ref, nu_ref, x_ref, w1_ref, w3_ref, w2_ref, y_ref, acc_ref):
    del te_ref
    i = pl.program_id(0)
    f = pl.program_id(1)
    nf = pl.num_programs(1)
    used = i < nu_ref[0]

    @pl.when(used)
    def _():
        part = _swiglu_partial(x_ref[...], w1_ref[...], w3_ref[...], w2_ref[...])

        @pl.when(f == 0)
        def _():
            acc_ref[...] = part

        @pl.when((f > 0) & (f < nf - 1))
        def _():
            acc_ref[...] += part

        @pl.when(f == nf - 1)
        def _():
            y_ref[...] = (acc_ref[...] + part).astype(y_ref.dtype)

    @pl.when(jnp.logical_not(used) & (f == nf - 1))
    def _():
        y_ref[...] = jnp.zeros_like(y_ref)


def moe_experts(xs, tile_expert, n_used, w1, w3, w2, *, tm, tf):
    rows, d = xs.shape
    ff = w1.shape[2]
    nf = ff // tf
    assert nf >= 2

    def wmap_in(i, f, te, nu):
        live = i < nu[0]
        return (te[i], 0, jnp.where(live, f, nf - 1))

    def wmap_out(i, f, te, nu):
        live = i < nu[0]
        return (te[i], jnp.where(live, f, nf - 1), 0)

    def xmap(i, f, te, nu):
        return (jnp.maximum(jnp.minimum(i, nu[0] - 1), 0), 0)

    return pl.pallas_call(
        _experts_kernel,
        grid_spec=pltpu.PrefetchScalarGridSpec(
            num_scalar_prefetch=2,
            grid=(rows // tm, nf),
            in_specs=[
                pl.BlockSpec((tm, d), xmap),
                pl.BlockSpec((None, d, tf), wmap_in),
                pl.BlockSpec((None, d, tf), wmap_in),
                pl.BlockSpec((None, tf, d), wmap_out),
            ],
            out_specs=pl.BlockSpec((tm, d), lambda i, f, te, nu: (i, 0)),
            scratch_shapes=[pltpu.VMEM((tm, d), F32)],
        ),
        out_shape=jax.ShapeDtypeStruct((rows, d), BF16),
        compiler_params=_cparams("arbitrary", "arbitrary"),
        name="moe_experts",
    )(tile_expert, n_used, xs, w1, w3, w2)


def _combine_kernel(lpad_ref, loff_ref, base_ref, x_ref, ri_ref, rw_ref, g_ref, mod_ref, ys_ref, o_ref,
                    ybuf, sem, *, mod_row, ne, tile0):
    tm, d = x_ref.shape
    rc = ybuf.shape[1]
    i = pl.program_id(0)
    slot = i % 2
    moves = functools.partial(_run_dmas, ne=ne, lpad_ref=lpad_ref, loff_ref=loff_ref, base_ref=base_ref,
                              hbm_ref=ys_ref, to_hbm=False)

    @pl.when(i == 0)
    def _():
        ybuf[...] = jnp.zeros_like(ybuf)
        moves(tile0, buf=ybuf.at[0], sem=sem.at[0], start=True)

    @pl.when(i + 1 < pl.num_programs(0))
    def _():
        moves(i + tile0 + 1, buf=ybuf.at[1 - slot], sem=sem.at[1 - slot], start=True)

    moves(i + tile0, buf=ybuf.at[slot], sem=sem.at[slot], start=False)
    lo1, lo2 = _local_rows(ri_ref, loff_ref, i + tile0, ne)
    lane = lax.broadcasted_iota(jnp.int32, (tm, rc), 1)
    rows = ybuf[slot]
    y1 = jnp.dot(jnp.where(lane == lo1, 1.0, 0.0).astype(BF16), rows, preferred_element_type=F32)
    y2 = jnp.dot(jnp.where(lane == lo2, 1.0, 0.0).astype(BF16), rows, preferred_element_type=F32)
    y = rw_ref[:, 0:1] * y1 + rw_ref[:, 1:2] * y2
    o_ref[...] = x_ref[...] + _mod_slice(mod_ref, mod_row, GA2, d) * _rms(y, g_ref[3:4, :])


def moe_combine(x, norm_g, mods, layer, route_i, route_w, ys, tabs, *, mod_row, tm, ne, tile0):
    m, d = x.shape
    rc = 2 * tm + ne * RUN_ALIGN
    return pl.pallas_call(
        functools.partial(_combine_kernel, mod_row=mod_row, ne=ne, tile0=tile0),
        grid_spec=pltpu.PrefetchScalarGridSpec(
            num_scalar_prefetch=3,
            grid=(m // tm,),
            in_specs=[
                pl.BlockSpec((tm, d), lambda i, *_: (i, 0)),
                pl.BlockSpec((tm, 4), lambda i, *_: (i, 0)),
                pl.BlockSpec((tm, 2), lambda i, *_: (i, 0)),
                pl.BlockSpec((None,) + norm_g.shape[1:], lambda i, *_: (layer, 0, 0)),
                pl.BlockSpec((None,) + mods.shape[1:], lambda i, *_: (layer, 0, 0)),
                pl.BlockSpec(memory_space=pl.ANY),
            ],
            out_specs=pl.BlockSpec((tm, d), lambda i, *_: (i, 0)),
            scratch_shapes=[pltpu.VMEM((2, rc, d), BF16), pltpu.SemaphoreType.DMA((2,))],
        ),
        out_shape=jax.ShapeDtypeStruct((m, d), F32),
        compiler_params=_cparams("arbitrary"),
        name="moe_combine",
    )(*tabs, x, route_i, route_w, norm_g, mods, ys)


def sparse_moe_layer(streams, norm_g, mods, layer, w_router, b_router, w1, w3, w2, *, tf):
    ne = w_router.shape[1]
    tm_exp = EXPERT_ROWS
    routed = [moe_router(x, norm_g, mods, layer, w_router, b_router, mod_row=row, tm=tm) for x, row, tm in streams]
    counts = jnp.concatenate([r[2][:, 0, :] for r in routed], axis=0)
    nt = counts.shape[0]
    m_total = sum(x.shape[0] for x, _, _ in streams)
    lpad = (counts + RUN_ALIGN - 1) // RUN_ALIGN * RUN_ALIGN
    loff = jnp.cumsum(lpad, axis=1) - lpad
    group = jnp.sum(lpad, axis=0)
    gpad = (group + tm_exp - 1) // tm_exp * tm_exp
    ends = jnp.cumsum(gpad)
    starts = ends - gpad
    base = starts[None, :] + jnp.cumsum(lpad, axis=0) - lpad
    n_tiles = -(-(2 * m_total + nt * ne * (RUN_ALIGN - 1)) // tm_exp) + ne
    tile_start = jnp.arange(n_tiles, dtype=jnp.int32) * tm_exp
    n_used = (ends[-1] // tm_exp).astype(jnp.int32).reshape(1)
    tile_expert = jnp.sum(tile_start[:, None] >= ends[None, :], axis=1).astype(jnp.int32)
    tile_expert = jnp.minimum(tile_expert, tile_expert[jnp.maximum(n_used[0] - 1, 0)])
    nt_first = streams[0][0].shape[0] // streams[0][2]
    group_first = jnp.sum(lpad[:nt_first], axis=0)
    later_rows = m_total - streams[0][0].shape[0] + (nt - nt_first) * (RUN_ALIGN - 1)
    assert later_rows + tm_exp - RUN_ALIGN < 2 << TAIL_BITS[0]
    tabs = [a.reshape(-1).astype(jnp.int32)
            for a in (lpad, loff, base, gpad - group_first, starts + group_first, n_used)]
    xs, tile0 = None, 0
    for (x, row, tm), (route_i, _, _) in zip(streams, routed):
        xs = moe_dispatch(x, norm_g, mods, layer, route_i, tabs, n_tiles * tm_exp, xs,
                          mod_row=row, tm=tm, ne=ne, tile0=tile0)
        tile0 += x.shape[0] // tm
    ys = moe_experts(xs, tile_expert, n_used, w1, w3, w2, tm=tm_exp, tf=tf)
    outs, tile0 = [], 0
    for (x, row, tm), (route_i, route_w, _) in zip(streams, routed):
        outs.append(moe_combine(x, norm_g, mods, layer, route_i, route_w, ys, tabs[:3],
                                mod_row=row, tm=tm, ne=ne, tile0=tile0))
        tile0 += x.shape[0] // tm
    return outs


def _rope_tables(seq):
    n = 16
    inv = ROPE_BASE ** (-jnp.arange(n, dtype=F32) / n)
    t = jnp.arange(seq)
    row_ang = (t // GRID_W).astype(F32)[:, None] * inv[None, :]
    col_ang = (t % GRID_W).astype(F32)[:, None] * inv[None, :]
    cos = jnp.concatenate([jnp.cos(row_ang)] * 2 + [jnp.cos(col_ang)] * 2, axis=1)
    sin = jnp.concatenate([-jnp.sin(row_ang), jnp.sin(row_ang), -jnp.sin(col_ang), jnp.sin(col_ang)], axis=1)
    return jnp.tile(cos, (1, 2)), jnp.tile(sin, (1, 2))


def kernel(x, c, ctx, c_ctx, w_ada, b_ada, norm_g, attn_w_qkv, attn_w_o, attn_sink, ret_w_in, ret_w_o,
           ret_log_decay, pool_w, pool_scale, ffn_w1, ffn_w3, ffn_w2, moe_w_router, moe_b_router,
           moe_w1, moe_w3, moe_w2):
    batch, seq, d = x.shape
    assert batch == 1 and c.shape[0] == 1
    depth = w_ada.shape[0]
    lc = ctx.shape[1]
    xl = x.reshape(seq, d)
    xc = ctx.reshape(lc, d)

    cvecs = jnp.zeros((8, d), F32).at[LAT_ROW].set(c[0]).at[CTX_ROW].set(c_ctx)
    mods = ada_table(cvecs, w_ada, b_ada)
    rope = _rope_tables(seq)

    hd = attn_w_o.shape[1]
    dh = hd // ATTN_HEADS
    qkv_scale = jnp.concatenate([jnp.full((hd,), dh ** -0.5, F32),
                                 jnp.ones((attn_w_qkv.shape[2] - hd,), F32)])
    dk = d // RET_HEADS
    ret_scale = jnp.concatenate([jnp.ones((d,), F32), jnp.full((d,), dk ** -0.5, F32),
                                 jnp.ones((ret_w_in.shape[2] - 2 * d,), F32)])

    moe_bf16 = None
    for i in range(depth):
        last = i == depth - 1
        kind, j = i % N_MIXERS, i // N_MIXERS
        proj = functools.partial(norm_mod_matmul, norm_g=norm_g, mods=mods, layer=i, g_row=0, slot=SH1)
        if kind == 0:
            w_qkv = (attn_w_qkv[j] * qkv_scale).astype(BF16)
            w_o = attn_w_o[j].astype(BF16)
            kvd = (w_qkv.shape[1] - hd) // 2
            qkv_l = qkv_projection(xl, norm_g, mods, i, w_qkv, mod_row=LAT_ROW, tm=512, rope=rope, rope_cols=hd + kvd)
            qkv_c = qkv_projection(xc, norm_g, mods, i, w_qkv, mod_row=CTX_ROW, tm=256)
            o_l = attention(attn_sink[j], qkv_l, qkv_c, has_local=True)
            xl = outproj_residual(xl, norm_g, mods, i, w_o, [o_l], g_row=1, mod_row=LAT_ROW, slot=GA1, tm=512)
            if not last:
                o_c = attention(attn_sink[j], qkv_c, qkv_c, has_local=False)
                xc = outproj_residual(xc, norm_g, mods, i, w_o, [o_c], g_row=1, mod_row=CTX_ROW, slot=GA1, tm=256)
        elif kind == 1:
            w_in = (ret_w_in[j] * ret_scale).astype(BF16)
            w_o = ret_w_o[j].astype(BF16)
            p_c = proj(xc, w=w_in, mod_row=CTX_ROW, tm=256, tn=1024)
            p_l = proj(xl, w=w_in, mod_row=LAT_ROW, tm=1024, tn=1024)
            s0 = jnp.zeros((RET_HEADS, dk, 2 * dk), F32)
            tabs_f = retention_tables(ret_log_decay[j, 0], min(RET_CHUNK, lc), False)
            tabs_b = retention_tables(ret_log_decay[j, 1], min(RET_CHUNK, lc), True)
            zf_c, s_f = retention_scan(p_c, tabs_f, s0, reverse=False)
            zb_c, s_b = retention_scan(p_c, tabs_b, s0, reverse=True)
            zf_l, _ = retention_scan(p_l, tabs_f, s_f, reverse=False)
            zb_l, _ = retention_scan(p_l, tabs_b, s_b, reverse=True)
            xl = outproj_residual(xl, norm_g, mods, i, w_o, [zf_l, zb_l], g_row=1, mod_row=LAT_ROW, slot=GA1, tm=512)
            if not last:
                xc = outproj_residual(xc, norm_g, mods, i, w_o, [zf_c, zb_c], g_row=1, mod_row=CTX_ROW, slot=GA1, tm=256)
        else:
            w_p = pool_w[j].astype(BF16)
            xl = pool_layer(xl, norm_g, mods, i, w_p, pool_scale[j], mod_row=LAT_ROW, tm=512)
            if not last:
                xc = pool_layer(xc, norm_g, mods, i, w_p, pool_scale[j], mod_row=CTX_ROW, tm=256)

        f = i // 2
        if i % 2 == 0:
            w1, w3, w2 = ffn_w1[f].astype(BF16), ffn_w3[f].astype(BF16), ffn_w2[f].astype(BF16)
            tf = w1.shape[1] // 2
            steps = (seq // min(512, seq)) * 2
            ne = moe_w1.shape[1]
            if not last and steps % ne == 0 and d % (steps // ne * 16) == 0:
                xl, moe_bf16 = ffn_layer(xl, norm_g, mods, i, w1, w3, w2, mod_row=LAT_ROW, tm=512, tf=tf,
                                         cast=((moe_w1, moe_w3, moe_w2), (i + 1) // 2))
            else:
                xl = ffn_layer(xl, norm_g, mods, i, w1, w3, w2, mod_row=LAT_ROW, tm=512, tf=tf)
            if not last:
                xc = ffn_layer(xc, norm_g, mods, i, w1, w3, w2, mod_row=CTX_ROW, tm=256, tf=tf)
        else:
            if moe_bf16 is None:
                moe_bf16 = moe_w1[f].astype(BF16), moe_w3[f].astype(BF16), moe_w2[f].astype(BF16)
            (w1, w3, w2), moe_bf16 = moe_bf16, None
            tf = w1.shape[2] // 2
            streams = [(xl, LAT_ROW, min(512, seq))] + ([] if last else [(xc, CTX_ROW, min(256, lc))])
            outs = sparse_moe_layer(streams, norm_g, mods, i, moe_w_router[f], moe_b_router[f], w1, w3, w2, tf=tf)
            xl = outs[0]
            if not last:
                xc = outs[1]
    return xl.reshape(batch, seq, d)
```

```python
import functools

import jax
import jax.numpy as jnp
from jax import lax
from jax.experimental import pallas as pl
from jax.experimental.pallas import tpu as pltpu

F32 = jnp.float32
BF16 = jnp.bfloat16

EPS = 1e-6
NEG_INF = -1e30
LOG2E = 1.4426950408889634
LANES = 128
VMEM_LIMIT = 56 * 1024 * 1024

GRID_W = 64
N_MIXERS = 3
ATTN_HEADS = 16
ATTN_KV_HEADS = 4
WINDOW = 128
ROPE_BASE = 10000.0
RET_HEADS = 4
RET_CHUNK = 256
POOL_WINDOWS = (2, 4, 8, 16)
POOL_HALO = 8
N_EXPERTS = 8

SH1, SC1, GA1, SH2, SC2, GA2 = range(6)
LAT_ROW, CTX_ROW = 0, 1


def _cparams(*sem):
    return pltpu.CompilerParams(dimension_semantics=sem, vmem_limit_bytes=VMEM_LIMIT)


def _rms(x, g):
    return x * lax.rsqrt(jnp.mean(x * x, axis=-1, keepdims=True) + EPS) * g


def _mod_slice(mod_ref, row, slot, d):
    return mod_ref[row:row + 1, slot * d:(slot + 1) * d]


def _norm_mod(x, g, shift, scale):
    return _rms(x, g) * (1.0 + scale) + shift


def _ada_kernel(c_ref, w_ref, b_ref, o_ref):
    c = c_ref[...]
    s = c * jax.nn.sigmoid(c)
    o_ref[...] = jnp.dot(s, w_ref[...], preferred_element_type=F32,
                         precision=lax.Precision.HIGHEST) + b_ref[...]


def ada_table(cvecs, w_ada, b_ada):
    depth, d, n = w_ada.shape
    tn = 1536
    return pl.pallas_call(
        _ada_kernel,
        grid=(depth, n // tn),
        in_specs=[
            pl.BlockSpec((8, d), lambda i, j: (0, 0)),
            pl.BlockSpec((None, d, tn), lambda i, j: (i, 0, j)),
            pl.BlockSpec((None, 1, tn), lambda i, j: (i, 0, j)),
        ],
        out_specs=pl.BlockSpec((None, 8, tn), lambda i, j: (i, 0, j)),
        out_shape=jax.ShapeDtypeStruct((depth, 8, n), F32),
        compiler_params=_cparams("parallel", "parallel"),
        name="ada_table",
    )(cvecs, w_ada, b_ada.reshape(depth, 1, n))


def _rope(a, cos, sin):
    lane = lax.broadcasted_iota(jnp.int32, (a.shape[0], LANES), 1)
    first = (lane % 32) < 16
    outs = []
    for cb in range(a.shape[1] // LANES):
        blk = a[:, cb * LANES:(cb + 1) * LANES]
        partner = jnp.where(first, pltpu.roll(blk, LANES - 16, 1), pltpu.roll(blk, 16, 1))
        outs.append(blk * cos + partner * sin)
    return jnp.concatenate(outs, axis=1)


def _proj_kernel(x_ref, g_ref, mod_ref, w_ref, o_ref, h_ref, *, g_row, mod_row, slot):
    d = x_ref.shape[1]
    j = pl.program_id(1)

    @pl.when(j == 0)
    def _():
        h = _norm_mod(x_ref[...], g_ref[g_row:g_row + 1, :],
                      _mod_slice(mod_ref, mod_row, slot, d), _mod_slice(mod_ref, mod_row, slot + 1, d))
        h_ref[...] = h.astype(BF16)

    o_ref[...] = jnp.dot(h_ref[...], w_ref[...], preferred_element_type=F32).astype(o_ref.dtype)


def norm_mod_matmul(x, norm_g, mods, layer, w, *, g_row, mod_row, slot, tm, tn):
    m, d = x.shape
    n = w.shape[1]
    tm = min(tm, m)
    return pl.pallas_call(
        functools.partial(_proj_kernel, g_row=g_row, mod_row=mod_row, slot=slot),
        grid=(m // tm, n // tn),
        in_specs=[
            pl.BlockSpec((tm, d), lambda i, j: (i, 0)),
            pl.BlockSpec((None,) + norm_g.shape[1:], lambda i, j: (layer, 0, 0)),
            pl.BlockSpec((None,) + mods.shape[1:], lambda i, j: (layer, 0, 0)),
            pl.BlockSpec((d, tn), lambda i, j: (0, j)),
        ],
        out_specs=pl.BlockSpec((tm, tn), lambda i, j: (i, j)),
        out_shape=jax.ShapeDtypeStruct((m, n), BF16),
        scratch_shapes=[pltpu.VMEM((tm, d), BF16)],
        compiler_params=_cparams("parallel", "arbitrary"),
        name="norm_mod_matmul",
    )(x, norm_g, mods, w)


QKV_CHUNK = 256


def _qkv_kernel(x_ref, g_ref, mod_ref, w_ref, *rest, mod_row, rope_cols, q_cols):
    if rope_cols:
        cos_ref, sin_ref, o_ref = rest
        cos, sin = cos_ref[...], sin_ref[...]
    else:
        (o_ref,) = rest
    d = x_ref.shape[1]
    h = _norm_mod(x_ref[...], g_ref[0:1, :], _mod_slice(mod_ref, mod_row, SH1, d),
                  _mod_slice(mod_ref, mod_row, SC1, d)).astype(BF16)
    for c0 in range(0, o_ref.shape[1], QKV_CHUNK):
        acc = jnp.dot(h, w_ref[:, c0:c0 + QKV_CHUNK], preferred_element_type=F32)
        if c0 < q_cols:
            acc = acc * LOG2E
        if c0 < rope_cols:
            acc = _rope(acc, cos, sin)
        o_ref[:, c0:c0 + QKV_CHUNK] = acc.astype(o_ref.dtype)


def qkv_projection(x, norm_g, mods, layer, w, *, mod_row, tm, q_cols, rope=None, rope_cols=0):
    m, d = x.shape
    n = w.shape[1]
    tm = min(tm, m)
    assert n % QKV_CHUNK == 0 and rope_cols % QKV_CHUNK == 0
    in_specs = [
        pl.BlockSpec((tm, d), lambda i: (i, 0)),
        pl.BlockSpec((None,) + norm_g.shape[1:], lambda i: (layer, 0, 0)),
        pl.BlockSpec((None,) + mods.shape[1:], lambda i: (layer, 0, 0)),
        pl.BlockSpec((d, n), lambda i: (0, 0)),
    ]
    args = [x, norm_g, mods, w]
    if rope_cols:
        in_specs += [pl.BlockSpec((tm, LANES), lambda i: (i, 0))] * 2
        args += list(rope)
    return pl.pallas_call(
        functools.partial(_qkv_kernel, mod_row=mod_row, rope_cols=rope_cols, q_cols=q_cols),
        grid=(m // tm,),
        in_specs=in_specs,
        out_specs=pl.BlockSpec((tm, n), lambda i: (i, 0)),
        out_shape=jax.ShapeDtypeStruct((m, n), BF16),
        compiler_params=_cparams("parallel"),
        name="qkv_projection",
    )(*args)


def _outproj_kernel(x_ref, g_ref, mod_ref, w_ref, *rest, n_y, g_row, mod_row, slot):
    y_refs, o_ref = rest[:n_y], rest[n_y]
    d = x_ref.shape[1]
    y = y_refs[0][...]
    if n_y == 2:
        y = (y.astype(F32) + y_refs[1][...].astype(F32)).astype(BF16)
    t = jnp.dot(y, w_ref[...], preferred_element_type=F32)
    gate = _mod_slice(mod_ref, mod_row, slot, d)
    o_ref[...] = x_ref[...] + gate * _rms(t, g_ref[g_row:g_row + 1, :])


def outproj_residual(x, norm_g, mods, layer, w, ys, *, g_row, mod_row, slot, tm):
    m, d = x.shape
    k = w.shape[0]
    tm = min(tm, m)
    in_specs = [
        pl.BlockSpec((tm, d), lambda i: (i, 0)),
        pl.BlockSpec((None,) + norm_g.shape[1:], lambda i: (layer, 0, 0)),
        pl.BlockSpec((None,) + mods.shape[1:], lambda i: (layer, 0, 0)),
        pl.BlockSpec((k, d), lambda i: (0, 0)),
    ] + [pl.BlockSpec((tm, k), lambda i: (i, 0))] * len(ys)
    return pl.pallas_call(
        functools.partial(_outproj_kernel, n_y=len(ys), g_row=g_row, mod_row=mod_row, slot=slot),
        grid=(m // tm,),
        in_specs=in_specs,
        out_specs=pl.BlockSpec((tm, d), lambda i: (i, 0)),
        out_shape=jax.ShapeDtypeStruct((m, d), F32),
        compiler_params=_cparams("parallel"),
        name="outproj_residual",
    )(x, norm_g, mods, w, *ys)


def _dot_nt(a, b):
    return lax.dot_general(a, b, (((1,), (1,)), ((), ())), preferred_element_type=F32)


def _attn_kernel(sink_ref, q_ref, *rest, tq, seq, kv_heads, group, dh, has_local):
    if has_local:
        kp_ref, kq_ref, kn_ref, vp_ref, vq_ref, vn_ref, kc_ref, vc_ref, o_ref = rest
    else:
        kc_ref, vc_ref, o_ref = rest
    assert 2 * dh == LANES and tq == WINDOW and group % 2 == 0
    b = pl.program_id(0)
    nb = pl.num_programs(0)
    rows = group * tq
    lc = kc_ref.shape[0]
    head_of_row = lax.broadcasted_iota(jnp.int32, (rows, 1), 0) // tq
    lane = lax.broadcasted_iota(jnp.int32, (1, LANES), 1)
    if has_local:
        r = lax.broadcasted_iota(jnp.int32, (rows, tq), 0) % tq
        c = lax.broadcasted_iota(jnp.int32, (rows, tq), 1)
        prev_ok = (c >= r) & (b > 0)
        next_ok = (c <= r) & (b < nb - 1)
    for j in range(kv_heads):
        hs = slice(j * dh, (j + 1) * dh)
        pair = slice((j // 2) * LANES, (j // 2 + 1) * LANES)
        v_low = j % 2 == 0
        keep_v = (lane < dh) if v_low else (lane >= dh)
        qg = jnp.concatenate(
            [q_ref[:, (j * group + g) * dh:(j * group + g + 1) * dh] for g in range(group)], axis=0)
        sink = jnp.full((rows, 1), sink_ref[j * group] * LOG2E, F32)
        for g in range(1, group):
            sink = jnp.where(head_of_row == g, sink_ref[j * group + g] * LOG2E, sink)
        if has_local:
            k_all = jnp.concatenate([kc_ref[:, hs], kp_ref[:, hs], kq_ref[:, hs], kn_ref[:, hs]], axis=0)
            v_all = jnp.concatenate([vc_ref[:, pair], vp_ref[:, pair], vq_ref[:, pair], vn_ref[:, pair]], axis=0)
            s = _dot_nt(qg, k_all)
            s = jnp.concatenate([s[:, :lc],
                                 jnp.where(prev_ok, s[:, lc:lc + tq], NEG_INF),
                                 s[:, lc + tq:lc + 2 * tq],
                                 jnp.where(next_ok, s[:, lc + 2 * tq:], NEG_INF)], axis=1)
        else:
            v_all = vc_ref[:, pair]
            s = _dot_nt(qg, kc_ref[:, hs])
        m = jnp.maximum(jnp.max(s, axis=-1, keepdims=True), sink)
        p = jnp.exp2(s - m).astype(BF16)
        v_ones = jnp.where(keep_v, v_all, jnp.ones_like(v_all))
        pv = jnp.dot(p, v_ones, preferred_element_type=F32)
        den = pltpu.roll(pv, dh, 1) + jnp.exp2(sink - m)
        o = pv / den
        for g in range(0, group, 2):
            a, bb = o[g * tq:(g + 1) * tq], o[(g + 1) * tq:(g + 2) * tq]
            if v_low:
                both = jnp.where(lane < dh, a, pltpu.roll(bb, dh, 1))
            else:
                both = jnp.where(lane < dh, pltpu.roll(a, dh, 1), bb)
            h = j * group + g
            o_ref[:, h * dh:(h + 2) * dh] = both.astype(o_ref.dtype)


def attention(sink, qkv, qkv_ctx, *, has_local):
    s = qkv.shape[0]
    dh = qkv.shape[1] // (ATTN_HEADS + 2 * ATTN_KV_HEADS)
    hd = ATTN_HEADS * dh
    kvd = ATTN_KV_HEADS * dh
    kcol, vcol = hd // kvd, hd // kvd + 1
    tq = WINDOW
    nb = s // tq
    lc = qkv_ctx.shape[0]
    in_specs = [
        pl.BlockSpec(memory_space=pltpu.SMEM),
        pl.BlockSpec((tq, hd), lambda b: (b, 0)),
    ]
    args = [sink, qkv]
    if has_local:
        for col in (kcol, vcol):
            in_specs += [
                pl.BlockSpec((tq, kvd), lambda b, col=col: (jnp.maximum(b - 1, 0), col)),
                pl.BlockSpec((tq, kvd), lambda b, col=col: (b, col)),
                pl.BlockSpec((tq, kvd), lambda b, col=col: (jnp.minimum(b + 1, nb - 1), col)),
            ]
            args += [qkv] * 3
    in_specs += [pl.BlockSpec((lc, kvd), lambda b: (0, kcol)), pl.BlockSpec((lc, kvd), lambda b: (0, vcol))]
    args += [qkv_ctx, qkv_ctx]
    return pl.pallas_call(
        functools.partial(_attn_kernel, tq=tq, seq=s, kv_heads=ATTN_KV_HEADS,
                          group=ATTN_HEADS // ATTN_KV_HEADS, dh=dh, has_local=has_local),
        grid=(nb,),
        in_specs=in_specs,
        out_specs=pl.BlockSpec((tq, hd), lambda b: (b, 0)),
        out_shape=jax.ShapeDtypeStruct((s, hd), BF16),
        compiler_params=_cparams("parallel"),
        name="attention",
    )(*args)


def _ret_kernel(q_ref, k_ref, v_ref, gate_ref, intra_ref, qdec_ref, kdec_ref, cdec_ref, s0_ref,
                z_ref, sfin_ref, state_ref, *, heads, dk, dv):
    i = pl.program_id(0)

    @pl.when(i == 0)
    def _():
        state_ref[...] = s0_ref[...]

    for h in range(heads):
        q = q_ref[:, h * dk:(h + 1) * dk]
        k = k_ref[:, h * dk:(h + 1) * dk]
        v = v_ref[:, h * dv:(h + 1) * dv]
        state = state_ref[h]
        sc = _dot_nt(q, k) * intra_ref[h]
        qd = (q.astype(F32) * qdec_ref[h]).astype(BF16)
        o = (jnp.dot(sc.astype(BF16), v, preferred_element_type=F32)
             + jnp.dot(qd, state.astype(BF16), preferred_element_type=F32))
        kd = (k.astype(F32) * kdec_ref[h]).astype(BF16)
        state_ref[h] = state * cdec_ref[h] + lax.dot_general(
            kd, v, (((0,), (0,)), ((), ())), preferred_element_type=F32)
        mu = jnp.mean(o, axis=-1, keepdims=True)
        oc = o - mu
        var = jnp.mean(oc * oc, axis=-1, keepdims=True)
        gate = gate_ref[:, h * dv:(h + 1) * dv].astype(F32)
        z = gate * jax.nn.sigmoid(gate) * (oc * lax.rsqrt(var + EPS))
        z_ref[:, h * dv:(h + 1) * dv] = z.astype(z_ref.dtype)

    @pl.when(i == pl.num_programs(0) - 1)
    def _():
        sfin_ref[...] = state_ref[...]


def retention_scan(proj, tables, s0, *, reverse):
    m, n = proj.shape
    d = n // 8
    heads = RET_HEADS
    dk, dv = d // heads, 2 * d // heads
    c = min(RET_CHUNK, m)
    nc = m // c
    intra, qdec, kdec, cdec = tables

    def row(i):
        return nc - 1 - i if reverse else i

    gate_blk = 3 if reverse else 2
    full = lambda a: pl.BlockSpec(a.shape, lambda i: (0,) * a.ndim)
    z, sfin = pl.pallas_call(
        functools.partial(_ret_kernel, heads=heads, dk=dk, dv=dv),
        grid=(nc,),
        in_specs=[
            pl.BlockSpec((c, d), lambda i: (row(i), 0)),
            pl.BlockSpec((c, d), lambda i: (row(i), 1)),
            pl.BlockSpec((c, 2 * d), lambda i: (row(i), 1)),
            pl.BlockSpec((c, 2 * d), lambda i: (row(i), gate_blk)),
            full(intra), full(qdec), full(kdec), full(cdec), full(s0),
        ],
        out_specs=[
            pl.BlockSpec((c, 2 * d), lambda i: (row(i), 0)),
            full(s0),
        ],
        out_shape=[jax.ShapeDtypeStruct((m, 2 * d), BF16), jax.ShapeDtypeStruct(s0.shape, F32)],
        scratch_shapes=[pltpu.VMEM(s0.shape, F32)],
        compiler_params=_cparams("arbitrary"),
        name="retention_scan",
    )(proj, proj, proj, proj, intra, qdec, kdec, cdec, s0)
    return z, sfin


def retention_tables(log_decay_row, c, reverse):
    lg = -jnp.exp(log_decay_row.astype(F32))
    idx = jnp.arange(c, dtype=F32)
    diff = idx[:, None] - idx[None, :]
    if reverse:
        diff = -diff
    intra = jnp.where(diff >= 0, jnp.exp(lg[:, None, None] * jnp.maximum(diff, 0.0)), 0.0)
    fwd_idx = (c - 1.0 - idx) if reverse else idx
    qdec = jnp.exp(lg[:, None] * (fwd_idx + 1.0))[:, :, None]
    kdec = jnp.exp(lg[:, None] * (c - 1.0 - fwd_idx))[:, :, None]
    cdec = jnp.exp(lg * c)[:, None, None]
    return intra, qdec, kdec, cdec


def _pool_kernel(x_ref, xp_ref, xn_ref, g_ref, mod_ref, w_ref, ps_ref, o_ref, h_ref, *, mod_row, seq):
    tm, d = x_ref.shape
    halo = POOL_HALO
    i = pl.program_id(0)
    g0 = g_ref[0:1, :]
    shift = _mod_slice(mod_ref, mod_row, SH1, d)
    scale = _mod_slice(mod_ref, mod_row, SC1, d)
    hp = _norm_mod(xp_ref[...], g0, shift, scale)
    hn = _norm_mod(xn_ref[...], g0, shift, scale)
    h_ref[0:halo, :] = jnp.where(i > 0, hp, 0.0)
    h_ref[halo:halo + tm, :] = _norm_mod(x_ref[...], g0, shift, scale)
    h_ref[halo + tm:, :] = jnp.where(i < pl.num_programs(0) - 1, hn, 0.0)

    t = i * tm + lax.broadcasted_iota(jnp.int32, (tm, 1), 0)
    gw = d // len(POOL_WINDOWS)
    ys = []
    for g, w in enumerate(POOL_WINDOWS):
        cols = slice(g * gw, (g + 1) * gw)
        tot = h_ref[halo - w // 2:halo - w // 2 + tm, cols]
        for off in range(-w // 2 + 1, w // 2):
            tot = tot + h_ref[halo + off:halo + off + tm, cols]
        cnt = (jnp.minimum(t + w // 2, seq) - jnp.maximum(t - w // 2, 0)).astype(F32)
        dm = tot / cnt - h_ref[halo:halo + tm, cols]
        ys.append(jnp.dot(dm.astype(BF16), w_ref[g], preferred_element_type=F32))
    y = jnp.concatenate(ys, axis=1) * ps_ref[...]
    o_ref[...] = x_ref[...] + _mod_slice(mod_ref, mod_row, GA1, d) * _rms(y, g_ref[1:2, :])


def pool_layer(x, norm_g, mods, layer, w_pool, pool_scale, *, mod_row, tm):
    m, d = x.shape
    tm = min(tm, m)
    nt = m // tm
    hb = tm // POOL_HALO
    return pl.pallas_call(
        functools.partial(_pool_kernel, mod_row=mod_row, seq=m),
        grid=(nt,),
        in_specs=[
            pl.BlockSpec((tm, d), lambda i: (i, 0)),
            pl.BlockSpec((POOL_HALO, d), lambda i: (jnp.maximum(i * hb - 1, 0), 0)),
            pl.BlockSpec((POOL_HALO, d), lambda i: (jnp.minimum((i + 1) * hb, nt * hb - 1), 0)),
            pl.BlockSpec((None,) + norm_g.shape[1:], lambda i: (layer, 0, 0)),
            pl.BlockSpec((None,) + mods.shape[1:], lambda i: (layer, 0, 0)),
            pl.BlockSpec(w_pool.shape, lambda i: (0, 0, 0)),
            pl.BlockSpec((1, d), lambda i: (0, 0)),
        ],
        out_specs=pl.BlockSpec((tm, d), lambda i: (i, 0)),
        out_shape=jax.ShapeDtypeStruct((m, d), F32),
        scratch_shapes=[pltpu.VMEM((tm + 2 * POOL_HALO, d), F32)],
        compiler_params=_cparams("parallel"),
        name="pool_layer",
    )(x, x, x, norm_g, mods, w_pool, pool_scale.reshape(1, d))


def _swiglu_partial(h, w1, w3, w2):
    a = jnp.dot(h, w1, preferred_element_type=F32)
    b = jnp.dot(h, w3, preferred_element_type=F32)
    u = (a * jax.nn.sigmoid(a) * b).astype(BF16)
    return jnp.dot(u, w2, preferred_element_type=F32)


def _ffn_kernel(x_ref, g_ref, mod_ref, w1_ref, w3_ref, w2_ref, *rest, mod_row, n_cast):
    cast_src, o_ref, cast_dst = rest[:n_cast], rest[n_cast], rest[n_cast + 1:2 * n_cast + 1]
    h_ref, acc_ref = rest[2 * n_cast + 1:]
    d = x_ref.shape[1]
    f = pl.program_id(1)
    for src, dst in zip(cast_src, cast_dst):
        dst[...] = src[...].astype(dst.dtype)

    @pl.when(f == 0)
    def _():
        h = _norm_mod(x_ref[...], g_ref[2:3, :], _mod_slice(mod_ref, mod_row, SH2, d),
                      _mod_slice(mod_ref, mod_row, SC2, d))
        h_ref[...] = h.astype(BF16)

    part = _swiglu_partial(h_ref[...], w1_ref[...], w3_ref[...], w2_ref[...])

    @pl.when(f == 0)
    def _():
        acc_ref[...] = part

    @pl.when(f > 0)
    def _():
        acc_ref[...] += part

    @pl.when(f == pl.num_programs(1) - 1)
    def _():
        o_ref[...] = x_ref[...] + _mod_slice(mod_ref, mod_row, GA2, d) * _rms(acc_ref[...], g_ref[3:4, :])


def ffn_layer(x, norm_g, mods, layer, w1, w3, w2, *, mod_row, tm, tf, cast=None):
    m, d = x.shape
    ff = w1.shape[1]
    tm = min(tm, m)
    nf = ff // tf
    steps = (m // tm) * nf
    in_specs = [
        pl.BlockSpec((tm, d), lambda i, f: (i, 0)),
        pl.BlockSpec((None,) + norm_g.shape[1:], lambda i, f: (layer, 0, 0)),
        pl.BlockSpec((None,) + mods.shape[1:], lambda i, f: (layer, 0, 0)),
        pl.BlockSpec((d, tf), lambda i, f: (0, f)),
        pl.BlockSpec((d, tf), lambda i, f: (0, f)),
        pl.BlockSpec((tf, d), lambda i, f: (f, 0)),
    ]
    out_specs = [pl.BlockSpec((tm, d), lambda i, f: (i, 0))]
    out_shape = [jax.ShapeDtypeStruct((m, d), F32)]
    cast_arrays, cast_idx = cast if cast is not None else ((), 0)
    for a in cast_arrays:
        _, ne, r, c = a.shape
        per_expert = steps // ne
        assert per_expert * ne == steps and r % (per_expert * 16) == 0
        rows = r // per_expert
        in_specs.append(pl.BlockSpec(
            (None, None, rows, c),
            lambda i, f, pe=per_expert: (cast_idx, (i * nf + f) // pe, (i * nf + f) % pe, 0)))
        out_specs.append(pl.BlockSpec(
            (None, rows, c), lambda i, f, pe=per_expert: ((i * nf + f) // pe, (i * nf + f) % pe, 0)))
        out_shape.append(jax.ShapeDtypeStruct((ne, r, c), BF16))
    outs = pl.pallas_call(
        functools.partial(_ffn_kernel, mod_row=mod_row, n_cast=len(cast_arrays)),
        grid=(m // tm, nf),
        in_specs=in_specs,
        out_specs=out_specs,
        out_shape=out_shape,
        scratch_shapes=[pltpu.VMEM((tm, d), BF16), pltpu.VMEM((tm, d), F32)],
        compiler_params=_cparams("arbitrary", "arbitrary"),
        name="ffn_layer",
    )(x, norm_g, mods, w1, w3, w2, *cast_arrays)
    return outs[0] if cast is None else (outs[0], tuple(outs[1:]))


RUN_ALIGN = 16
RUN_BITS = tuple(range(9, 3, -1))
EXPERT_ROWS = 512
TAIL_BITS = tuple(range(9, 3, -1))


def _split_dma(length, src, src_row, dst, dst_row, sem, bits, *, start, src_advances=True):
    for b in bits:
        size = 1 << b
        off = (length >> (b + 1)) << (b + 1)

        @pl.when(((length >> b) & 1) == 1)
        def _():
            s_row = pl.multiple_of(src_row + off, RUN_ALIGN) if src_advances else src_row
            cp = pltpu.make_async_copy(src.at[pl.ds(s_row, size)],
                                       dst.at[pl.ds(pl.multiple_of(dst_row + off, RUN_ALIGN), size)], sem)
            if start:
                cp.start()
            else:
                cp.wait()


def _run_dmas(tile, ne, lpad_ref, loff_ref, base_ref, buf, hbm_ref, sem, *, to_hbm, start):
    for e in range(ne):
        length = lpad_ref[tile * ne + e]
        lo = loff_ref[tile * ne + e]
        gb = base_ref[tile * ne + e]
        if to_hbm:
            _split_dma(length, buf, lo, hbm_ref, gb, sem, RUN_BITS, start=start)
        else:
            _split_dma(length, hbm_ref, gb, buf, lo, sem, RUN_BITS, start=start)


def _local_rows(ri_ref, loff_ref, tile, ne):
    e1, e2 = ri_ref[:, 0:1], ri_ref[:, 1:2]
    lo1, lo2 = ri_ref[:, 2:3], ri_ref[:, 3:4]
    for e in range(ne):
        off = loff_ref[tile * ne + e]
        lo1 = lo1 + jnp.where(e1 == e, off, 0)
        lo2 = lo2 + jnp.where(e2 == e, off, 0)
    return lo1, lo2


def _router_kernel(x_ref, g_ref, mod_ref, wr_ref, br_ref, ri_ref, rw_ref, cnt_ref, *, mod_row):
    tm, d = x_ref.shape
    ne = wr_ref.shape[1]
    h = _norm_mod(x_ref[...], g_ref[2:3, :], _mod_slice(mod_ref, mod_row, SH2, d),
                  _mod_slice(mod_ref, mod_row, SC2, d))
    logits = jnp.dot(h, wr_ref[...], preferred_element_type=F32,
                     precision=lax.Precision.HIGHEST) + br_ref[...]
    lane = lax.broadcasted_iota(jnp.int32, logits.shape, 1)
    v1 = jnp.max(logits, axis=-1, keepdims=True)
    i1 = jnp.min(jnp.where(logits == v1, lane, ne), axis=-1, keepdims=True)
    rest = jnp.where(lane == i1, -jnp.inf, logits)
    v2 = jnp.max(rest, axis=-1, keepdims=True)
    i2 = jnp.min(jnp.where(rest == v2, lane, ne), axis=-1, keepdims=True)
    e2 = jnp.exp(v2 - v1)
    den = 1.0 + e2
    oh1 = (lane == i1).astype(F32)
    oh2 = (lane == i2).astype(F32)
    both = oh1 + oh2
    r = lax.broadcasted_iota(jnp.int32, (tm, tm), 0)
    c = lax.broadcasted_iota(jnp.int32, (tm, tm), 1)
    tri = jnp.where(c < r, 1.0, 0.0).astype(BF16)
    rank_all = jnp.dot(tri, both.astype(BF16), preferred_element_type=F32)
    rank1 = jnp.sum(rank_all * oh1, axis=-1, keepdims=True).astype(jnp.int32)
    rank2 = jnp.sum(rank_all * oh2, axis=-1, keepdims=True).astype(jnp.int32)
    col = lax.broadcasted_iota(jnp.int32, ri_ref.shape, 1)
    ri_ref[...] = jnp.where(col == 0, i1, jnp.where(col == 1, i2, jnp.where(col == 2, rank1, rank2)))
    colw = lax.broadcasted_iota(jnp.int32, rw_ref.shape, 1)
    rw_ref[...] = jnp.where(colw == 0, 1.0 / den, e2 / den)
    cnt_ref[...] = jnp.sum(both, axis=0, keepdims=True).astype(jnp.int32)


def moe_router(x, norm_g, mods, layer, w_router, b_router, *, mod_row, tm):
    m, d = x.shape
    ne = w_router.shape[1]
    return pl.pallas_call(
        functools.partial(_router_kernel, mod_row=mod_row),
        grid=(m // tm,),
        in_specs=[
            pl.BlockSpec((tm, d), lambda i: (i, 0)),
            pl.BlockSpec((None,) + norm_g.shape[1:], lambda i: (layer, 0, 0)),
            pl.BlockSpec((None,) + mods.shape[1:], lambda i: (layer, 0, 0)),
            pl.BlockSpec((d, ne), lambda i: (0, 0)),
            pl.BlockSpec((1, ne), lambda i: (0, 0)),
        ],
        out_specs=[
            pl.BlockSpec((tm, 4), lambda i: (i, 0)),
            pl.BlockSpec((tm, 2), lambda i: (i, 0)),
            pl.BlockSpec((None, 1, ne), lambda i: (i, 0, 0)),
        ],
        out_shape=[
            jax.ShapeDtypeStruct((m, 4), jnp.int32),
            jax.ShapeDtypeStruct((m, 2), F32),
            jax.ShapeDtypeStruct((m // tm, 1, ne), jnp.int32),
        ],
        compiler_params=_cparams("parallel"),
        name="moe_router",
    )(x, norm_g, mods, w_router, b_router.reshape(1, ne))


def _dispatch_kernel(lpad_ref, loff_ref, base_ref, tlen_ref, tstart_ref, nu_ref, x_ref, g_ref, mod_ref, ri_ref,
                     *rest, mod_row, ne, tile0, first):
    xs_ref, cbuf, zbuf, sem = rest if first else rest[1:]
    tm, d = x_ref.shape
    rc = cbuf.shape[1]
    i = pl.program_id(0)
    last = i == pl.num_programs(0) - 1
    slot = i % 2

    def zero_tails(start):
        for e in range(ne):
            _split_dma(tlen_ref[e], zbuf, 0, xs_ref, tstart_ref[e], sem.at[2], TAIL_BITS, start=start,
                       src_advances=False)

        def unused_tile(t, carry):
            cp = pltpu.make_async_copy(
                zbuf, xs_ref.at[pl.ds(pl.multiple_of(t * EXPERT_ROWS, EXPERT_ROWS), EXPERT_ROWS)], sem.at[2])
            if start:
                cp.start()
            else:
                cp.wait()
            return carry

        lax.fori_loop(nu_ref[0], xs_ref.shape[0] // EXPERT_ROWS, unused_tile, 0)

    if first:
        @pl.when(i == 0)
        def _():
            zbuf[...] = jnp.zeros_like(zbuf)
            zero_tails(True)

    h = _norm_mod(x_ref[...], g_ref[2:3, :], _mod_slice(mod_ref, mod_row, SH2, d),
                  _mod_slice(mod_ref, mod_row, SC2, d)).astype(BF16)
    lo1, lo2 = _local_rows(ri_ref, loff_ref, i + tile0, ne)
    lane = lax.broadcasted_iota(jnp.int32, (tm, rc), 1)
    onehot = jnp.where((lane == lo1) | (lane == lo2), 1.0, 0.0).astype(BF16)
    sorted_rows = lax.dot_general(onehot, h, (((0,), (0,)), ((), ())), preferred_element_type=F32)
    cbuf[slot] = sorted_rows.astype(BF16)
    moves = functools.partial(_run_dmas, ne=ne, lpad_ref=lpad_ref, loff_ref=loff_ref, base_ref=base_ref,
                              hbm_ref=xs_ref, to_hbm=True)
    moves(i + tile0, buf=cbuf.at[slot], sem=sem.at[slot], start=True)

    @pl.when(i > 0)
    def _():
        moves(i + tile0 - 1, buf=cbuf.at[1 - slot], sem=sem.at[1 - slot], start=False)

    @pl.when(last)
    def _():
        moves(i + tile0, buf=cbuf.at[slot], sem=sem.at[slot], start=False)
        if first:
            zero_tails(False)


def moe_dispatch(x, norm_g, mods, layer, route_i, tabs, rows, xs=None, *, mod_row, tm, ne, tile0):
    m, d = x.shape
    rc = 2 * tm + ne * RUN_ALIGN
    first = xs is None
    in_specs = [
        pl.BlockSpec((tm, d), lambda i, *_: (i, 0)),
        pl.BlockSpec((None,) + norm_g.shape[1:], lambda i, *_: (layer, 0, 0)),
        pl.BlockSpec((None,) + mods.shape[1:], lambda i, *_: (layer, 0, 0)),
        pl.BlockSpec((tm, 4), lambda i, *_: (i, 0)),
    ]
    args = [*tabs, x, norm_g, mods, route_i]
    if not first:
        in_specs.append(pl.BlockSpec(memory_space=pl.ANY))
        args.append(xs)
    return pl.pallas_call(
        functools.partial(_dispatch_kernel, mod_row=mod_row, ne=ne, tile0=tile0, first=first),
        grid_spec=pltpu.PrefetchScalarGridSpec(
            num_scalar_prefetch=len(tabs),
            grid=(m // tm,),
            in_specs=in_specs,
            out_specs=pl.BlockSpec(memory_space=pl.ANY),
            scratch_shapes=[pltpu.VMEM((2, rc, d), BF16), pltpu.VMEM((EXPERT_ROWS, d), BF16),
                            pltpu.SemaphoreType.DMA((3,))],
        ),
        out_shape=jax.ShapeDtypeStruct((rows, d), BF16),
        input_output_aliases={} if first else {len(args) - 1: 0},
        compiler_params=_cparams("arbitrary"),
        name="moe_dispatch",
    )(*args)


def _experts_kernel(te_ref, nu_ref, x_ref, w1_ref, w3_ref, w2_ref, y_ref, acc_ref):
    del te_ref
    i = pl.program_id(0)
    f = pl.program_id(1)
    nf = pl.num_programs(1)
    used = i < nu_ref[0]

    @pl.when(used)
    def _():
        part = _swiglu_partial(x_ref[...], w1_ref[...], w3_ref[...], w2_ref[...])

        @pl.when(f == 0)
        def _():
            acc_ref[...] = part

        @pl.when((f > 0) & (f < nf - 1))
        def _():
            acc_ref[...] += part

        @pl.when(f == nf - 1)
        def _():
            y_ref[...] = (acc_ref[...] + part).astype(y_ref.dtype)

    @pl.when(jnp.logical_not(used) & (f == nf - 1))
    def _():
        y_ref[...] = jnp.zeros_like(y_ref)


def moe_experts(xs, tile_expert, n_used, w1, w3, w2, *, tm, tf):
    rows, d = xs.shape
    ff = w1.shape[2]
    nf = ff // tf
    assert nf >= 2

    def wmap_in(i, f, te, nu):
        live = i < nu[0]
        return (te[i], 0, jnp.where(live, f, nf - 1))

    def wmap_out(i, f, te, nu):
        live = i < nu[0]
        return (te[i], jnp.where(live, f, nf - 1), 0)

    def xmap(i, f, te, nu):
        return (jnp.maximum(jnp.minimum(i, nu[0] - 1), 0), 0)

    return pl.pallas_call(
        _experts_kernel,
        grid_spec=pltpu.PrefetchScalarGridSpec(
            num_scalar_prefetch=2,
            grid=(rows // tm, nf),
            in_specs=[
                pl.BlockSpec((tm, d), xmap),
                pl.BlockSpec((None, d, tf), wmap_in),
                pl.BlockSpec((None, d, tf), wmap_in),
                pl.BlockSpec((None, tf, d), wmap_out),
            ],
            out_specs=pl.BlockSpec((tm, d), lambda i, f, te, nu: (i, 0)),
            scratch_shapes=[pltpu.VMEM((tm, d), F32)],
        ),
        out_shape=jax.ShapeDtypeStruct((rows, d), BF16),
        compiler_params=_cparams("arbitrary", "arbitrary"),
        name="moe_experts",
    )(tile_expert, n_used, xs, w1, w3, w2)


def _combine_kernel(lpad_ref, loff_ref, base_ref, x_ref, ri_ref, rw_ref, g_ref, mod_ref, ys_ref, o_ref,
                    ybuf, sem, *, mod_row, ne, tile0):
    tm, d = x_ref.shape
    rc = ybuf.shape[1]
    i = pl.program_id(0)
    slot = i % 2
    moves = functools.partial(_run_dmas, ne=ne, lpad_ref=lpad_ref, loff_ref=loff_ref, base_ref=base_ref,
                              hbm_ref=ys_ref, to_hbm=False)

    @pl.when(i == 0)
    def _():
        ybuf[...] = jnp.zeros_like(ybuf)
        moves(tile0, buf=ybuf.at[0], sem=sem.at[0], start=True)

    @pl.when(i + 1 < pl.num_programs(0))
    def _():
        moves(i + tile0 + 1, buf=ybuf.at[1 - slot], sem=sem.at[1 - slot], start=True)

    moves(i + tile0, buf=ybuf.at[slot], sem=sem.at[slot], start=False)
    lo1, lo2 = _local_rows(ri_ref, loff_ref, i + tile0, ne)
    lane = lax.broadcasted_iota(jnp.int32, (tm, rc), 1)
    rows = ybuf[slot]
    y1 = jnp.dot(jnp.where(lane == lo1, 1.0, 0.0).astype(BF16), rows, preferred_element_type=F32)
    y2 = jnp.dot(jnp.where(lane == lo2, 1.0, 0.0).astype(BF16), rows, preferred_element_type=F32)
    y = rw_ref[:, 0:1] * y1 + rw_ref[:, 1:2] * y2
    o_ref[...] = x_ref[...] + _mod_slice(mod_ref, mod_row, GA2, d) * _rms(y, g_ref[3:4, :])


def moe_combine(x, norm_g, mods, layer, route_i, route_w, ys, tabs, *, mod_row, tm, ne, tile0):
    m, d = x.shape
    rc = 2 * tm + ne * RUN_ALIGN
    return pl.pallas_call(
        functools.partial(_combine_kernel, mod_row=mod_row, ne=ne, tile0=tile0),
        grid_spec=pltpu.PrefetchScalarGridSpec(
            num_scalar_prefetch=3,
            grid=(m // tm,),
            in_specs=[
                pl.BlockSpec((tm, d), lambda i, *_: (i, 0)),
                pl.BlockSpec((tm, 4), lambda i, *_: (i, 0)),
                pl.BlockSpec((tm, 2), lambda i, *_: (i, 0)),
                pl.BlockSpec((None,) + norm_g.shape[1:], lambda i, *_: (layer, 0, 0)),
                pl.BlockSpec((None,) + mods.shape[1:], lambda i, *_: (layer, 0, 0)),
                pl.BlockSpec(memory_space=pl.ANY),
            ],
            out_specs=pl.BlockSpec((tm, d), lambda i, *_: (i, 0)),
            scratch_shapes=[pltpu.VMEM((2, rc, d), BF16), pltpu.SemaphoreType.DMA((2,))],
        ),
        out_shape=jax.ShapeDtypeStruct((m, d), F32),
        compiler_params=_cparams("arbitrary"),
        name="moe_combine",
    )(*tabs, x, route_i, route_w, norm_g, mods, ys)


def sparse_moe_layer(streams, norm_g, mods, layer, w_router, b_router, w1, w3, w2, *, tf):
    ne = w_router.shape[1]
    tm_exp = EXPERT_ROWS
    routed = [moe_router(x, norm_g, mods, layer, w_router, b_router, mod_row=row, tm=tm) for x, row, tm in streams]
    counts = jnp.concatenate([r[2][:, 0, :] for r in routed], axis=0)
    nt = counts.shape[0]
    m_total = sum(x.shape[0] for x, _, _ in streams)
    lpad = (counts + RUN_ALIGN - 1) // RUN_ALIGN * RUN_ALIGN
    loff = jnp.cumsum(lpad, axis=1) - lpad
    group = jnp.sum(lpad, axis=0)
    gpad = (group + tm_exp - 1) // tm_exp * tm_exp
    ends = jnp.cumsum(gpad)
    starts = ends - gpad
    base = starts[None, :] + jnp.cumsum(lpad, axis=0) - lpad
    n_tiles = -(-(2 * m_total + nt * ne * (RUN_ALIGN - 1)) // tm_exp) + ne
    tile_start = jnp.arange(n_tiles, dtype=jnp.int32) * tm_exp
    n_used = (ends[-1] // tm_exp).astype(jnp.int32).reshape(1)
    tile_expert = jnp.sum(tile_start[:, None] >= ends[None, :], axis=1).astype(jnp.int32)
    tile_expert = jnp.minimum(tile_expert, tile_expert[jnp.maximum(n_used[0] - 1, 0)])
    nt_first = streams[0][0].shape[0] // streams[0][2]
    group_first = jnp.sum(lpad[:nt_first], axis=0)
    later_rows = m_total - streams[0][0].shape[0] + (nt - nt_first) * (RUN_ALIGN - 1)
    assert later_rows + tm_exp - RUN_ALIGN < 2 << TAIL_BITS[0]
    tabs = [a.reshape(-1).astype(jnp.int32)
            for a in (lpad, loff, base, gpad - group_first, starts + group_first, n_used)]
    xs, tile0 = None, 0
    for (x, row, tm), (route_i, _, _) in zip(streams, routed):
        xs = moe_dispatch(x, norm_g, mods, layer, route_i, tabs, n_tiles * tm_exp, xs,
                          mod_row=row, tm=tm, ne=ne, tile0=tile0)
        tile0 += x.shape[0] // tm
    ys = moe_experts(xs, tile_expert, n_used, w1, w3, w2, tm=tm_exp, tf=tf)
    outs, tile0 = [], 0
    for (x, row, tm), (route_i, route_w, _) in zip(streams, routed):
        outs.append(moe_combine(x, norm_g, mods, layer, route_i, route_w, ys, tabs[:3],
                                mod_row=row, tm=tm, ne=ne, tile0=tile0))
        tile0 += x.shape[0] // tm
    return outs


def _rope_tables(seq):
    n = 16
    inv = ROPE_BASE ** (-jnp.arange(n, dtype=F32) / n)
    t = jnp.arange(seq)
    row_ang = (t // GRID_W).astype(F32)[:, None] * inv[None, :]
    col_ang = (t % GRID_W).astype(F32)[:, None] * inv[None, :]
    cos = jnp.concatenate([jnp.cos(row_ang)] * 2 + [jnp.cos(col_ang)] * 2, axis=1)
    sin = jnp.concatenate([-jnp.sin(row_ang), jnp.sin(row_ang), -jnp.sin(col_ang), jnp.sin(col_ang)], axis=1)
    return jnp.tile(cos, (1, 2)), jnp.tile(sin, (1, 2))


def kernel(x, c, ctx, c_ctx, w_ada, b_ada, norm_g, attn_w_qkv, attn_w_o, attn_sink, ret_w_in, ret_w_o,
           ret_log_decay, pool_w, pool_scale, ffn_w1, ffn_w3, ffn_w2, moe_w_router, moe_b_router,
           moe_w1, moe_w3, moe_w2):
    batch, seq, d = x.shape
    assert batch == 1 and c.shape[0] == 1
    depth = w_ada.shape[0]
    lc = ctx.shape[1]
    xl = x.reshape(seq, d)
    xc = ctx.reshape(lc, d)

    cvecs = jnp.zeros((8, d), F32).at[LAT_ROW].set(c[0]).at[CTX_ROW].set(c_ctx)
    mods = ada_table(cvecs, w_ada, b_ada)
    rope = _rope_tables(seq)

    hd = attn_w_o.shape[1]
    dh = hd // ATTN_HEADS
    qkv_scale = jnp.concatenate([jnp.full((hd,), dh ** -0.5, F32),
                                 jnp.ones((attn_w_qkv.shape[2] - hd,), F32)])
    dk = d // RET_HEADS
    ret_scale = jnp.concatenate([jnp.ones((d,), F32), jnp.full((d,), dk ** -0.5, F32),
                                 jnp.ones((ret_w_in.shape[2] - 2 * d,), F32)])

    moe_bf16 = None
    for i in range(depth):
        last = i == depth - 1
        kind, j = i % N_MIXERS, i // N_MIXERS
        proj = functools.partial(norm_mod_matmul, norm_g=norm_g, mods=mods, layer=i, g_row=0, slot=SH1)
        if kind == 0:
            w_qkv = (attn_w_qkv[j] * qkv_scale).astype(BF16)
            w_o = attn_w_o[j].astype(BF16)
            kvd = (w_qkv.shape[1] - hd) // 2
            qkv_l = qkv_projection(xl, norm_g, mods, i, w_qkv, mod_row=LAT_ROW, tm=512, q_cols=hd,
                                   rope=rope, rope_cols=hd + kvd)
            qkv_c = qkv_projection(xc, norm_g, mods, i, w_qkv, mod_row=CTX_ROW, tm=256, q_cols=hd)
            o_l = attention(attn_sink[j], qkv_l, qkv_c, has_local=True)
            xl = outproj_residual(xl, norm_g, mods, i, w_o, [o_l], g_row=1, mod_row=LAT_ROW, slot=GA1, tm=512)
            if not last:
                o_c = attention(attn_sink[j], qkv_c, qkv_c, has_local=False)
                xc = outproj_residual(xc, norm_g, mods, i, w_o, [o_c], g_row=1, mod_row=CTX_ROW, slot=GA1, tm=256)
        elif kind == 1:
            w_in = (ret_w_in[j] * ret_scale).astype(BF16)
            w_o = ret_w_o[j].astype(BF16)
            p_c = proj(xc, w=w_in, mod_row=CTX_ROW, tm=256, tn=1024)
            p_l = proj(xl, w=w_in, mod_row=LAT_ROW, tm=1024, tn=1024)
            s0 = jnp.zeros((RET_HEADS, dk, 2 * dk), F32)
            tabs_f = retention_tables(ret_log_decay[j, 0], min(RET_CHUNK, lc), False)
            tabs_b = retention_tables(ret_log_decay[j, 1], min(RET_CHUNK, lc), True)
            zf_c, s_f = retention_scan(p_c, tabs_f, s0, reverse=False)
            zb_c, s_b = retention_scan(p_c, tabs_b, s0, reverse=True)
            zf_l, _ = retention_scan(p_l, tabs_f, s_f, reverse=False)
            zb_l, _ = retention_scan(p_l, tabs_b, s_b, reverse=True)
            xl = outproj_residual(xl, norm_g, mods, i, w_o, [zf_l, zb_l], g_row=1, mod_row=LAT_ROW, slot=GA1, tm=512)
            if not last:
                xc = outproj_residual(xc, norm_g, mods, i, w_o, [zf_c, zb_c], g_row=1, mod_row=CTX_ROW, slot=GA1, tm=256)
        else:
            w_p = pool_w[j].astype(BF16)
            xl = pool_layer(xl, norm_g, mods, i, w_p, pool_scale[j], mod_row=LAT_ROW, tm=512)
            if not last:
                xc = pool_layer(xc, norm_g, mods, i, w_p, pool_scale[j], mod_row=CTX_ROW, tm=256)

        f = i // 2
        if i % 2 == 0:
            w1, w3, w2 = ffn_w1[f].astype(BF16), ffn_w3[f].astype(BF16), ffn_w2[f].astype(BF16)
            tf = w1.shape[1] // 2
            steps = (seq // min(512, seq)) * 2
            ne = moe_w1.shape[1]
            if not last and steps % ne == 0 and d % (steps // ne * 16) == 0:
                xl, moe_bf16 = ffn_layer(xl, norm_g, mods, i, w1, w3, w2, mod_row=LAT_ROW, tm=512, tf=tf,
                                         cast=((moe_w1, moe_w3, moe_w2), (i + 1) // 2))
            else:
                xl = ffn_layer(xl, norm_g, mods, i, w1, w3, w2, mod_row=LAT_ROW, tm=512, tf=tf)
            if not last:
                xc = ffn_layer(xc, norm_g, mods, i, w1, w3, w2, mod_row=CTX_ROW, tm=256, tf=tf)
        else:
            if moe_bf16 is None:
                moe_bf16 = moe_w1[f].astype(BF16), moe_w3[f].astype(BF16), moe_w2[f].astype(BF16)
            (w1, w3, w2), moe_bf16 = moe_bf16, None
            tf = w1.shape[2] // 2
            streams = [(xl, LAT_ROW, min(512, seq))] + ([] if last else [(xc, CTX_ROW, min(256, lc))])
            outs = sparse_moe_layer(streams, norm_g, mods, i, moe_w_router[f], moe_b_router[f], w1, w3, w2, tf=tf)
            xl = outs[0]
            if not last:
                xc = outs[1]
    return xl.reshape(batch, seq, d)
```

```python
import functools

import jax
import jax.numpy as jnp
from jax import lax
from jax.experimental import pallas as pl
from jax.experimental.pallas import tpu as pltpu

F32 = jnp.float32
BF16 = jnp.bfloat16

EPS = 1e-6
NEG_INF = -1e30
LOG2E = 1.4426950408889634
LANES = 128
VMEM_LIMIT = 56 * 1024 * 1024

GRID_W = 64
N_MIXERS = 3
ATTN_HEADS = 16
ATTN_KV_HEADS = 4
ATTN_BLOCKS_PER_STEP = 4
WINDOW = 128
ROPE_BASE = 10000.0
RET_HEADS = 4
RET_CHUNK = 256
POOL_WINDOWS = (2, 4, 8, 16)
POOL_HALO = 8
N_EXPERTS = 8

SH1, SC1, GA1, SH2, SC2, GA2 = range(6)
LAT_ROW, CTX_ROW = 0, 1


def _cparams(*sem):
    return pltpu.CompilerParams(dimension_semantics=sem, vmem_limit_bytes=VMEM_LIMIT)


def _rms(x, g):
    return x * lax.rsqrt(jnp.mean(x * x, axis=-1, keepdims=True) + EPS) * g


def _mod_slice(mod_ref, row, slot, d):
    return mod_ref[row:row + 1, slot * d:(slot + 1) * d]


def _norm_mod(x, g, shift, scale):
    return _rms(x, g) * (1.0 + scale) + shift


def _ada_kernel(c_ref, w_ref, b_ref, o_ref):
    c = c_ref[...]
    s = c * jax.nn.sigmoid(c)
    o_ref[...] = jnp.dot(s, w_ref[...], preferred_element_type=F32,
                         precision=lax.Precision.HIGHEST) + b_ref[...]


def ada_table(cvecs, w_ada, b_ada):
    depth, d, n = w_ada.shape
    tn = 1536
    return pl.pallas_call(
        _ada_kernel,
        grid=(depth, n // tn),
        in_specs=[
            pl.BlockSpec((8, d), lambda i, j: (0, 0)),
            pl.BlockSpec((None, d, tn), lambda i, j: (i, 0, j)),
            pl.BlockSpec((None, 1, tn), lambda i, j: (i, 0, j)),
        ],
        out_specs=pl.BlockSpec((None, 8, tn), lambda i, j: (i, 0, j)),
        out_shape=jax.ShapeDtypeStruct((depth, 8, n), F32),
        compiler_params=_cparams("parallel", "parallel"),
        name="ada_table",
    )(cvecs, w_ada, b_ada.reshape(depth, 1, n))


def _rope(a, cos, sin):
    lane = lax.broadcasted_iota(jnp.int32, (a.shape[0], LANES), 1)
    first = (lane % 32) < 16
    outs = []
    for cb in range(a.shape[1] // LANES):
        blk = a[:, cb * LANES:(cb + 1) * LANES]
        partner = jnp.where(first, pltpu.roll(blk, LANES - 16, 1), pltpu.roll(blk, 16, 1))
        outs.append(blk * cos + partner * sin)
    return jnp.concatenate(outs, axis=1)


def _proj_kernel(x_ref, g_ref, mod_ref, w_ref, o_ref, h_ref, *, g_row, mod_row, slot):
    d = x_ref.shape[1]
    j = pl.program_id(1)

    @pl.when(j == 0)
    def _():
        h = _norm_mod(x_ref[...], g_ref[g_row:g_row + 1, :],
                      _mod_slice(mod_ref, mod_row, slot, d), _mod_slice(mod_ref, mod_row, slot + 1, d))
        h_ref[...] = h.astype(BF16)

    o_ref[...] = jnp.dot(h_ref[...], w_ref[...], preferred_element_type=F32).astype(o_ref.dtype)


def norm_mod_matmul(x, norm_g, mods, layer, w, *, g_row, mod_row, slot, tm, tn):
    m, d = x.shape
    n = w.shape[1]
    tm = min(tm, m)
    return pl.pallas_call(
        functools.partial(_proj_kernel, g_row=g_row, mod_row=mod_row, slot=slot),
        grid=(m // tm, n // tn),
        in_specs=[
            pl.BlockSpec((tm, d), lambda i, j: (i, 0)),
            pl.BlockSpec((None,) + norm_g.shape[1:], lambda i, j: (layer, 0, 0)),
            pl.BlockSpec((None,) + mods.shape[1:], lambda i, j: (layer, 0, 0)),
            pl.BlockSpec((d, tn), lambda i, j: (0, j)),
        ],
        out_specs=pl.BlockSpec((tm, tn), lambda i, j: (i, j)),
        out_shape=jax.ShapeDtypeStruct((m, n), BF16),
        scratch_shapes=[pltpu.VMEM((tm, d), BF16)],
        compiler_params=_cparams("parallel", "arbitrary"),
        name="norm_mod_matmul",
    )(x, norm_g, mods, w)


QKV_CHUNK = 256


def _qkv_kernel(x_ref, g_ref, mod_ref, w_ref, *rest, mod_row, rope_cols, q_cols):
    if rope_cols:
        cos_ref, sin_ref, o_ref = rest
        cos, sin = cos_ref[...], sin_ref[...]
    else:
        (o_ref,) = rest
    d = x_ref.shape[1]
    h = _norm_mod(x_ref[...], g_ref[0:1, :], _mod_slice(mod_ref, mod_row, SH1, d),
                  _mod_slice(mod_ref, mod_row, SC1, d)).astype(BF16)
    for c0 in range(0, o_ref.shape[1], QKV_CHUNK):
        acc = jnp.dot(h, w_ref[:, c0:c0 + QKV_CHUNK], preferred_element_type=F32)
        if c0 < q_cols:
            acc = acc * LOG2E
        if c0 < rope_cols:
            acc = _rope(acc, cos, sin)
        o_ref[:, c0:c0 + QKV_CHUNK] = acc.astype(o_ref.dtype)


def qkv_projection(x, norm_g, mods, layer, w, *, mod_row, tm, q_cols, rope=None, rope_cols=0):
    m, d = x.shape
    n = w.shape[1]
    tm = min(tm, m)
    assert n % QKV_CHUNK == 0 and rope_cols % QKV_CHUNK == 0
    in_specs = [
        pl.BlockSpec((tm, d), lambda i: (i, 0)),
        pl.BlockSpec((None,) + norm_g.shape[1:], lambda i: (layer, 0, 0)),
        pl.BlockSpec((None,) + mods.shape[1:], lambda i: (layer, 0, 0)),
        pl.BlockSpec((d, n), lambda i: (0, 0)),
    ]
    args = [x, norm_g, mods, w]
    if rope_cols:
        in_specs += [pl.BlockSpec((tm, LANES), lambda i: (i, 0))] * 2
        args += list(rope)
    return pl.pallas_call(
        functools.partial(_qkv_kernel, mod_row=mod_row, rope_cols=rope_cols, q_cols=q_cols),
        grid=(m // tm,),
        in_specs=in_specs,
        out_specs=pl.BlockSpec((tm, n), lambda i: (i, 0)),
        out_shape=jax.ShapeDtypeStruct((m, n), BF16),
        compiler_params=_cparams("parallel"),
        name="qkv_projection",
    )(*args)


def _outproj_kernel(x_ref, g_ref, mod_ref, w_ref, *rest, n_y, g_row, mod_row, slot):
    y_refs, o_ref = rest[:n_y], rest[n_y]
    d = x_ref.shape[1]
    y = y_refs[0][...]
    if n_y == 2:
        y = (y.astype(F32) + y_refs[1][...].astype(F32)).astype(BF16)
    t = jnp.dot(y, w_ref[...], preferred_element_type=F32)
    gate = _mod_slice(mod_ref, mod_row, slot, d)
    o_ref[...] = x_ref[...] + gate * _rms(t, g_ref[g_row:g_row + 1, :])


def outproj_residual(x, norm_g, mods, layer, w, ys, *, g_row, mod_row, slot, tm):
    m, d = x.shape
    k = w.shape[0]
    tm = min(tm, m)
    in_specs = [
        pl.BlockSpec((tm, d), lambda i: (i, 0)),
        pl.BlockSpec((None,) + norm_g.shape[1:], lambda i: (layer, 0, 0)),
        pl.BlockSpec((None,) + mods.shape[1:], lambda i: (layer, 0, 0)),
        pl.BlockSpec((k, d), lambda i: (0, 0)),
    ] + [pl.BlockSpec((tm, k), lambda i: (i, 0))] * len(ys)
    return pl.pallas_call(
        functools.partial(_outproj_kernel, n_y=len(ys), g_row=g_row, mod_row=mod_row, slot=slot),
        grid=(m // tm,),
        in_specs=in_specs,
        out_specs=pl.BlockSpec((tm, d), lambda i: (i, 0)),
        out_shape=jax.ShapeDtypeStruct((m, d), F32),
        compiler_params=_cparams("parallel"),
        name="outproj_residual",
    )(x, norm_g, mods, w, *ys)


def _dot_nt(a, b):
    return lax.dot_general(a, b, (((1,), (1,)), ((), ())), preferred_element_type=F32)


def _attn_kernel(sink_ref, q_ref, *rest, tq, nsub, kv_heads, group, dh, has_local):
    if has_local:
        k_refs, v_refs = rest[:nsub + 2], rest[nsub + 2:2 * nsub + 4]
    kc_ref, vc_ref, o_ref = rest[-3:]
    assert 2 * dh == LANES and tq == WINDOW and group % 2 == 0
    b = pl.program_id(0)
    nb = pl.num_programs(0)
    rows = group * tq
    lc = kc_ref.shape[0]
    head_of_row = lax.broadcasted_iota(jnp.int32, (rows, 1), 0) // tq
    lane = lax.broadcasted_iota(jnp.int32, (1, LANES), 1)
    if has_local:
        r = lax.broadcasted_iota(jnp.int32, (rows, tq), 0) % tq
        c = lax.broadcasted_iota(jnp.int32, (rows, tq), 1)
    for sub, j in [(sub, j) for sub in range(nsub) for j in range(kv_heads)]:
        qrows = slice(sub * tq, (sub + 1) * tq)
        hs = slice(j * dh, (j + 1) * dh)
        pair = slice((j // 2) * LANES, (j // 2 + 1) * LANES)
        v_low = j % 2 == 0
        keep_v = (lane < dh) if v_low else (lane >= dh)
        qg = jnp.concatenate(
            [q_ref[qrows, (j * group + g) * dh:(j * group + g + 1) * dh] for g in range(group)], axis=0)
        sink = jnp.full((rows, 1), sink_ref[j * group] * LOG2E, F32)
        for g in range(1, group):
            sink = jnp.where(head_of_row == g, sink_ref[j * group + g] * LOG2E, sink)
        if has_local:
            kp_ref, kq_ref, kn_ref = k_refs[sub:sub + 3]
            vp_ref, vq_ref, vn_ref = v_refs[sub:sub + 3]
            prev_ok = (c >= r) if sub > 0 else (c >= r) & (b > 0)
            next_ok = (c <= r) if sub < nsub - 1 else (c <= r) & (b < nb - 1)
            k_all = jnp.concatenate([kc_ref[:, hs], kp_ref[:, hs], kq_ref[:, hs], kn_ref[:, hs]], axis=0)
            v_all = jnp.concatenate([vc_ref[:, pair], vp_ref[:, pair], vq_ref[:, pair], vn_ref[:, pair]], axis=0)
            s = _dot_nt(qg, k_all)
            s = jnp.concatenate([s[:, :lc],
                                 jnp.where(prev_ok, s[:, lc:lc + tq], NEG_INF),
                                 s[:, lc + tq:lc + 2 * tq],
                                 jnp.where(next_ok, s[:, lc + 2 * tq:], NEG_INF)], axis=1)
        else:
            v_all = vc_ref[:, pair]
            s = _dot_nt(qg, kc_ref[:, hs])
        m = jnp.maximum(jnp.max(s, axis=-1, keepdims=True), sink)
        p = jnp.exp2(s - m).astype(BF16)
        v_ones = jnp.where(keep_v, v_all, jnp.ones_like(v_all))
        pv = jnp.dot(p, v_ones, preferred_element_type=F32)
        den = pltpu.roll(pv, dh, 1) + jnp.exp2(sink - m)
        o = pv / den
        for g in range(0, group, 2):
            a, bb = o[g * tq:(g + 1) * tq], o[(g + 1) * tq:(g + 2) * tq]
            if v_low:
                both = jnp.where(lane < dh, a, pltpu.roll(bb, dh, 1))
            else:
                both = jnp.where(lane < dh, pltpu.roll(a, dh, 1), bb)
            h = j * group + g
            o_ref[qrows, h * dh:(h + 2) * dh] = both.astype(o_ref.dtype)


def attention(sink, qkv, qkv_ctx, *, has_local):
    s = qkv.shape[0]
    dh = qkv.shape[1] // (ATTN_HEADS + 2 * ATTN_KV_HEADS)
    hd = ATTN_HEADS * dh
    kvd = ATTN_KV_HEADS * dh
    kcol, vcol = hd // kvd, hd // kvd + 1
    tq = WINDOW
    nb = s // tq
    nsub = min(ATTN_BLOCKS_PER_STEP, nb)
    lc = qkv_ctx.shape[0]
    in_specs = [
        pl.BlockSpec(memory_space=pltpu.SMEM),
        pl.BlockSpec((nsub * tq, hd), lambda b: (b, 0)),
    ]
    args = [sink, qkv]
    if has_local:
        for col in (kcol, vcol):
            for off in range(-1, nsub + 1):
                in_specs.append(pl.BlockSpec(
                    (tq, kvd), lambda b, col=col, off=off: (jnp.clip(nsub * b + off, 0, nb - 1), col)))
                args.append(qkv)
    in_specs += [pl.BlockSpec((lc, kvd), lambda b: (0, kcol)), pl.BlockSpec((lc, kvd), lambda b: (0, vcol))]
    args += [qkv_ctx, qkv_ctx]
    return pl.pallas_call(
        functools.partial(_attn_kernel, tq=tq, nsub=nsub, kv_heads=ATTN_KV_HEADS,
                          group=ATTN_HEADS // ATTN_KV_HEADS, dh=dh, has_local=has_local),
        grid=(nb // nsub,),
        in_specs=in_specs,
        out_specs=pl.BlockSpec((nsub * tq, hd), lambda b: (b, 0)),
        out_shape=jax.ShapeDtypeStruct((s, hd), BF16),
        compiler_params=_cparams("parallel"),
        name="attention",
    )(*args)


def _ret_kernel(q_ref, k_ref, v_ref, gate_ref, intra_ref, qdec_ref, kdec_ref, cdec_ref, s0_ref,
                z_ref, sfin_ref, state_ref, *, heads, dk, dv):
    i = pl.program_id(0)

    @pl.when(i == 0)
    def _():
        state_ref[...] = s0_ref[...]

    for h in range(heads):
        q = q_ref[:, h * dk:(h + 1) * dk]
        k = k_ref[:, h * dk:(h + 1) * dk]
        v = v_ref[:, h * dv:(h + 1) * dv]
        state = state_ref[h]
        sc = _dot_nt(q, k) * intra_ref[h]
        qd = (q.astype(F32) * qdec_ref[h]).astype(BF16)
        o = (jnp.dot(sc.astype(BF16), v, preferred_element_type=F32)
             + jnp.dot(qd, state.astype(BF16), preferred_element_type=F32))
        kd = (k.astype(F32) * kdec_ref[h]).astype(BF16)
        state_ref[h] = state * cdec_ref[h] + lax.dot_general(
            kd, v, (((0,), (0,)), ((), ())), preferred_element_type=F32)
        mu = jnp.mean(o, axis=-1, keepdims=True)
        oc = o - mu
        var = jnp.mean(oc * oc, axis=-1, keepdims=True)
        gate = gate_ref[:, h * dv:(h + 1) * dv].astype(F32)
        z = gate * jax.nn.sigmoid(gate) * (oc * lax.rsqrt(var + EPS))
        z_ref[:, h * dv:(h + 1) * dv] = z.astype(z_ref.dtype)

    @pl.when(i == pl.num_programs(0) - 1)
    def _():
        sfin_ref[...] = state_ref[...]


def retention_scan(proj, tables, s0, *, reverse):
    m, n = proj.shape
    d = n // 8
    heads = RET_HEADS
    dk, dv = d // heads, 2 * d // heads
    c = min(RET_CHUNK, m)
    nc = m // c
    intra, qdec, kdec, cdec = tables

    def row(i):
        return nc - 1 - i if reverse else i

    gate_blk = 3 if reverse else 2
    full = lambda a: pl.BlockSpec(a.shape, lambda i: (0,) * a.ndim)
    z, sfin = pl.pallas_call(
        functools.partial(_ret_kernel, heads=heads, dk=dk, dv=dv),
        grid=(nc,),
        in_specs=[
            pl.BlockSpec((c, d), lambda i: (row(i), 0)),
            pl.BlockSpec((c, d), lambda i: (row(i), 1)),
            pl.BlockSpec((c, 2 * d), lambda i: (row(i), 1)),
            pl.BlockSpec((c, 2 * d), lambda i: (row(i), gate_blk)),
            full(intra), full(qdec), full(kdec), full(cdec), full(s0),
        ],
        out_specs=[
            pl.BlockSpec((c, 2 * d), lambda i: (row(i), 0)),
            full(s0),
        ],
        out_shape=[jax.ShapeDtypeStruct((m, 2 * d), BF16), jax.ShapeDtypeStruct(s0.shape, F32)],
        scratch_shapes=[pltpu.VMEM(s0.shape, F32)],
        compiler_params=_cparams("arbitrary"),
        name="retention_scan",
    )(proj, proj, proj, proj, intra, qdec, kdec, cdec, s0)
    return z, sfin


def retention_tables(log_decay_row, c, reverse):
    lg = -jnp.exp(log_decay_row.astype(F32))
    idx = jnp.arange(c, dtype=F32)
    diff = idx[:, None] - idx[None, :]
    if reverse:
        diff = -diff
    intra = jnp.where(diff >= 0, jnp.exp(lg[:, None, None] * jnp.maximum(diff, 0.0)), 0.0)
    fwd_idx = (c - 1.0 - idx) if reverse else idx
    qdec = jnp.exp(lg[:, None] * (fwd_idx + 1.0))[:, :, None]
    kdec = jnp.exp(lg[:, None] * (c - 1.0 - fwd_idx))[:, :, None]
    cdec = jnp.exp(lg * c)[:, None, None]
    return intra, qdec, kdec, cdec


def _pool_kernel(x_ref, xp_ref, xn_ref, g_ref, mod_ref, w_ref, ps_ref, o_ref, h_ref, *, mod_row, seq):
    tm, d = x_ref.shape
    halo = POOL_HALO
    i = pl.program_id(0)
    g0 = g_ref[0:1, :]
    shift = _mod_slice(mod_ref, mod_row, SH1, d)
    scale = _mod_slice(mod_ref, mod_row, SC1, d)
    hp = _norm_mod(xp_ref[...], g0, shift, scale)
    hn = _norm_mod(xn_ref[...], g0, shift, scale)
    h_ref[0:halo, :] = jnp.where(i > 0, hp, 0.0)
    h_ref[halo:halo + tm, :] = _norm_mod(x_ref[...], g0, shift, scale)
    h_ref[halo + tm:, :] = jnp.where(i < pl.num_programs(0) - 1, hn, 0.0)

    t = i * tm + lax.broadcasted_iota(jnp.int32, (tm, 1), 0)
    gw = d // len(POOL_WINDOWS)
    ys = []
    for g, w in enumerate(POOL_WINDOWS):
        cols = slice(g * gw, (g + 1) * gw)
        tot = h_ref[halo - w // 2:halo - w // 2 + tm, cols]
        for off in range(-w // 2 + 1, w // 2):
            tot = tot + h_ref[halo + off:halo + off + tm, cols]
        cnt = (jnp.minimum(t + w // 2, seq) - jnp.maximum(t - w // 2, 0)).astype(F32)
        dm = tot / cnt - h_ref[halo:halo + tm, cols]
        ys.append(jnp.dot(dm.astype(BF16), w_ref[g], preferred_element_type=F32))
    y = jnp.concatenate(ys, axis=1) * ps_ref[...]
    o_ref[...] = x_ref[...] + _mod_slice(mod_ref, mod_row, GA1, d) * _rms(y, g_ref[1:2, :])


def pool_layer(x, norm_g, mods, layer, w_pool, pool_scale, *, mod_row, tm):
    m, d = x.shape
    tm = min(tm, m)
    nt = m // tm
    hb = tm // POOL_HALO
    return pl.pallas_call(
        functools.partial(_pool_kernel, mod_row=mod_row, seq=m),
        grid=(nt,),
        in_specs=[
            pl.BlockSpec((tm, d), lambda i: (i, 0)),
            pl.BlockSpec((POOL_HALO, d), lambda i: (jnp.maximum(i * hb - 1, 0), 0)),
            pl.BlockSpec((POOL_HALO, d), lambda i: (jnp.minimum((i + 1) * hb, nt * hb - 1), 0)),
            pl.BlockSpec((None,) + norm_g.shape[1:], lambda i: (layer, 0, 0)),
            pl.BlockSpec((None,) + mods.shape[1:], lambda i: (layer, 0, 0)),
            pl.BlockSpec(w_pool.shape, lambda i: (0, 0, 0)),
            pl.BlockSpec((1, d), lambda i: (0, 0)),
        ],
        out_specs=pl.BlockSpec((tm, d), lambda i: (i, 0)),
        out_shape=jax.ShapeDtypeStruct((m, d), F32),
        scratch_shapes=[pltpu.VMEM((tm + 2 * POOL_HALO, d), F32)],
        compiler_params=_cparams("parallel"),
        name="pool_layer",
    )(x, x, x, norm_g, mods, w_pool, pool_scale.reshape(1, d))


def _swiglu_partial(h, w1, w3, w2):
    a = jnp.dot(h, w1, preferred_element_type=F32)
    b = jnp.dot(h, w3, preferred_element_type=F32)
    u = (a * jax.nn.sigmoid(a) * b).astype(BF16)
    return jnp.dot(u, w2, preferred_element_type=F32)


def _ffn_kernel(x_ref, g_ref, mod_ref, w1_ref, w3_ref, w2_ref, *rest, mod_row, n_cast):
    cast_src, o_ref, cast_dst = rest[:n_cast], rest[n_cast], rest[n_cast + 1:2 * n_cast + 1]
    h_ref, acc_ref = rest[2 * n_cast + 1:]
    d = x_ref.shape[1]
    f = pl.program_id(1)
    for src, dst in zip(cast_src, cast_dst):
        dst[...] = src[...].astype(dst.dtype)

    @pl.when(f == 0)
    def _():
        h = _norm_mod(x_ref[...], g_ref[2:3, :], _mod_slice(mod_ref, mod_row, SH2, d),
                      _mod_slice(mod_ref, mod_row, SC2, d))
        h_ref[...] = h.astype(BF16)

    part = _swiglu_partial(h_ref[...], w1_ref[...], w3_ref[...], w2_ref[...])

    @pl.when(f == 0)
    def _():
        acc_ref[...] = part

    @pl.when(f > 0)
    def _():
        acc_ref[...] += part

    @pl.when(f == pl.num_programs(1) - 1)
    def _():
        o_ref[...] = x_ref[...] + _mod_slice(mod_ref, mod_row, GA2, d) * _rms(acc_ref[...], g_ref[3:4, :])


def ffn_layer(x, norm_g, mods, layer, w1, w3, w2, *, mod_row, tm, tf, cast=None):
    m, d = x.shape
    ff = w1.shape[1]
    tm = min(tm, m)
    nf = ff // tf
    steps = (m // tm) * nf
    in_specs = [
        pl.BlockSpec((tm, d), lambda i, f: (i, 0)),
        pl.BlockSpec((None,) + norm_g.shape[1:], lambda i, f: (layer, 0, 0)),
        pl.BlockSpec((None,) + mods.shape[1:], lambda i, f: (layer, 0, 0)),
        pl.BlockSpec((d, tf), lambda i, f: (0, f)),
        pl.BlockSpec((d, tf), lambda i, f: (0, f)),
        pl.BlockSpec((tf, d), lambda i, f: (f, 0)),
    ]
    out_specs = [pl.BlockSpec((tm, d), lambda i, f: (i, 0))]
    out_shape = [jax.ShapeDtypeStruct((m, d), F32)]
    cast_arrays, cast_idx = cast if cast is not None else ((), 0)
    for a in cast_arrays:
        _, ne, r, c = a.shape
        per_expert = steps // ne
        assert per_expert * ne == steps and r % (per_expert * 16) == 0
        rows = r // per_expert
        in_specs.append(pl.BlockSpec(
            (None, None, rows, c),
            lambda i, f, pe=per_expert: (cast_idx, (i * nf + f) // pe, (i * nf + f) % pe, 0)))
        out_specs.append(pl.BlockSpec(
            (None, rows, c), lambda i, f, pe=per_expert: ((i * nf + f) // pe, (i * nf + f) % pe, 0)))
        out_shape.append(jax.ShapeDtypeStruct((ne, r, c), BF16))
    outs = pl.pallas_call(
        functools.partial(_ffn_kernel, mod_row=mod_row, n_cast=len(cast_arrays)),
        grid=(m // tm, nf),
        in_specs=in_specs,
        out_specs=out_specs,
        out_shape=out_shape,
        scratch_shapes=[pltpu.VMEM((tm, d), BF16), pltpu.VMEM((tm, d), F32)],
        compiler_params=_cparams("arbitrary", "arbitrary"),
        name="ffn_layer",
    )(x, norm_g, mods, w1, w3, w2, *cast_arrays)
    return outs[0] if cast is None else (outs[0], tuple(outs[1:]))


RUN_ALIGN = 16
RUN_BITS = tuple(range(9, 3, -1))
EXPERT_ROWS = 512
TAIL_BITS = tuple(range(9, 3, -1))


def _split_dma(length, src, src_row, dst, dst_row, sem, bits, *, start, src_advances=True):
    for b in bits:
        size = 1 << b
        off = (length >> (b + 1)) << (b + 1)

        @pl.when(((length >> b) & 1) == 1)
        def _():
            s_row = pl.multiple_of(src_row + off, RUN_ALIGN) if src_advances else src_row
            cp = pltpu.make_async_copy(src.at[pl.ds(s_row, size)],
                                       dst.at[pl.ds(pl.multiple_of(dst_row + off, RUN_ALIGN), size)], sem)
            if start:
                cp.start()
            else:
                cp.wait()


def _run_dmas(tile, ne, lpad_ref, loff_ref, base_ref, buf, hbm_ref, sem, *, to_hbm, start):
    for e in range(ne):
        length = lpad_ref[tile * ne + e]
        lo = loff_ref[tile * ne + e]
        gb = base_ref[tile * ne + e]
        if to_hbm:
            _split_dma(length, buf, lo, hbm_ref, gb, sem, RUN_BITS, start=start)
        else:
            _split_dma(length, hbm_ref, gb, buf, lo, sem, RUN_BITS, start=start)


def _local_rows(ri_ref, loff_ref, tile, ne):
    e1, e2 = ri_ref[:, 0:1], ri_ref[:, 1:2]
    lo1, lo2 = ri_ref[:, 2:3], ri_ref[:, 3:4]
    for e in range(ne):
        off = loff_ref[tile * ne + e]
        lo1 = lo1 + jnp.where(e1 == e, off, 0)
        lo2 = lo2 + jnp.where(e2 == e, off, 0)
    return lo1, lo2


def _router_kernel(x_ref, g_ref, mod_ref, wr_ref, br_ref, ri_ref, rw_ref, cnt_ref, *, mod_row):
    tm, d = x_ref.shape
    ne = wr_ref.shape[1]
    h = _norm_mod(x_ref[...], g_ref[2:3, :], _mod_slice(mod_ref, mod_row, SH2, d),
                  _mod_slice(mod_ref, mod_row, SC2, d))
    logits = jnp.dot(h, wr_ref[...], preferred_element_type=F32,
                     precision=lax.Precision.HIGHEST) + br_ref[...]
    lane = lax.broadcasted_iota(jnp.int32, logits.shape, 1)
    v1 = jnp.max(logits, axis=-1, keepdims=True)
    i1 = jnp.min(jnp.where(logits == v1, lane, ne), axis=-1, keepdims=True)
    rest = jnp.where(lane == i1, -jnp.inf, logits)
    v2 = jnp.max(rest, axis=-1, keepdims=True)
    i2 = jnp.min(jnp.where(rest == v2, lane, ne), axis=-1, keepdims=True)
    e2 = jnp.exp(v2 - v1)
    den = 1.0 + e2
    oh1 = (lane == i1).astype(F32)
    oh2 = (lane == i2).astype(F32)
    both = oh1 + oh2
    r = lax.broadcasted_iota(jnp.int32, (tm, tm), 0)
    c = lax.broadcasted_iota(jnp.int32, (tm, tm), 1)
    tri = jnp.where(c < r, 1.0, 0.0).astype(BF16)
    rank_all = jnp.dot(tri, both.astype(BF16), preferred_element_type=F32)
    rank1 = jnp.sum(rank_all * oh1, axis=-1, keepdims=True).astype(jnp.int32)
    rank2 = jnp.sum(rank_all * oh2, axis=-1, keepdims=True).astype(jnp.int32)
    col = lax.broadcasted_iota(jnp.int32, ri_ref.shape, 1)
    ri_ref[...] = jnp.where(col == 0, i1, jnp.where(col == 1, i2, jnp.where(col == 2, rank1, rank2)))
    colw = lax.broadcasted_iota(jnp.int32, rw_ref.shape, 1)
    rw_ref[...] = jnp.where(colw == 0, 1.0 / den, e2 / den)
    cnt_ref[...] = jnp.sum(both, axis=0, keepdims=True).astype(jnp.int32)


def moe_router(x, norm_g, mods, layer, w_router, b_router, *, mod_row, tm):
    m, d = x.shape
    ne = w_router.shape[1]
    return pl.pallas_call(
        functools.partial(_router_kernel, mod_row=mod_row),
        grid=(m // tm,),
        in_specs=[
            pl.BlockSpec((tm, d), lambda i: (i, 0)),
            pl.BlockSpec((None,) + norm_g.shape[1:], lambda i: (layer, 0, 0)),
            pl.BlockSpec((None,) + mods.shape[1:], lambda i: (layer, 0, 0)),
            pl.BlockSpec((d, ne), lambda i: (0, 0)),
            pl.BlockSpec((1, ne), lambda i: (0, 0)),
        ],
        out_specs=[
            pl.BlockSpec((tm, 4), lambda i: (i, 0)),
            pl.BlockSpec((tm, 2), lambda i: (i, 0)),
            pl.BlockSpec((None, 1, ne), lambda i: (i, 0, 0)),
        ],
        out_shape=[
            jax.ShapeDtypeStruct((m, 4), jnp.int32),
            jax.ShapeDtypeStruct((m, 2), F32),
            jax.ShapeDtypeStruct((m // tm, 1, ne), jnp.int32),
        ],
        compiler_params=_cparams("parallel"),
        name="moe_router",
    )(x, norm_g, mods, w_router, b_router.reshape(1, ne))


def _dispatch_kernel(lpad_ref, loff_ref, base_ref, tlen_ref, tstart_ref, nu_ref, x_ref, g_ref, mod_ref, ri_ref,
                     *rest, mod_row, ne, tile0, first):
    xs_ref, cbuf, zbuf, sem = rest if first else rest[1:]
    tm, d = x_ref.shape
    rc = cbuf.shape[1]
    i = pl.program_id(0)
    last = i == pl.num_programs(0) - 1
    slot = i % 2

    def zero_tails(start):
        for e in range(ne):
            _split_dma(tlen_ref[e], zbuf, 0, xs_ref, tstart_ref[e], sem.at[2], TAIL_BITS, start=start,
                       src_advances=False)

        def unused_tile(t, carry):
            cp = pltpu.make_async_copy(
                zbuf, xs_ref.at[pl.ds(pl.multiple_of(t * EXPERT_ROWS, EXPERT_ROWS), EXPERT_ROWS)], sem.at[2])
            if start:
                cp.start()
            else:
                cp.wait()
            return carry

        lax.fori_loop(nu_ref[0], xs_ref.shape[0] // EXPERT_ROWS, unused_tile, 0)

    if first:
        @pl.when(i == 0)
        def _():
            zbuf[...] = jnp.zeros_like(zbuf)
            zero_tails(True)

    h = _norm_mod(x_ref[...], g_ref[2:3, :], _mod_slice(mod_ref, mod_row, SH2, d),
                  _mod_slice(mod_ref, mod_row, SC2, d)).astype(BF16)
    lo1, lo2 = _local_rows(ri_ref, loff_ref, i + tile0, ne)
    lane = lax.broadcasted_iota(jnp.int32, (tm, rc), 1)
    onehot = jnp.where((lane == lo1) | (lane == lo2), 1.0, 0.0).astype(BF16)
    sorted_rows = lax.dot_general(onehot, h, (((0,), (0,)), ((), ())), preferred_element_type=F32)
    cbuf[slot] = sorted_rows.astype(BF16)
    moves = functools.partial(_run_dmas, ne=ne, lpad_ref=lpad_ref, loff_ref=loff_ref, base_ref=base_ref,
                              hbm_ref=xs_ref, to_hbm=True)
    moves(i + tile0, buf=cbuf.at[slot], sem=sem.at[slot], start=True)

    @pl.when(i > 0)
    def _():
        moves(i + tile0 - 1, buf=cbuf.at[1 - slot], sem=sem.at[1 - slot], start=False)

    @pl.when(last)
    def _():
        moves(i + tile0, buf=cbuf.at[slot], sem=sem.at[slot], start=False)
        if first:
            zero_tails(False)


def moe_dispatch(x, norm_g, mods, layer, route_i, tabs, rows, xs=None, *, mod_row, tm, ne, tile0):
    m, d = x.shape
    rc = 2 * tm + ne * RUN_ALIGN
    first = xs is None
    in_specs = [
        pl.BlockSpec((tm, d), lambda i, *_: (i, 0)),
        pl.BlockSpec((None,) + norm_g.shape[1:], lambda i, *_: (layer, 0, 0)),
        pl.BlockSpec((None,) + mods.shape[1:], lambda i, *_: (layer, 0, 0)),
        pl.BlockSpec((tm, 4), lambda i, *_: (i, 0)),
    ]
    args = [*tabs, x, norm_g, mods, route_i]
    if not first:
        in_specs.append(pl.BlockSpec(memory_space=pl.ANY))
        args.append(xs)
    return pl.pallas_call(
        functools.partial(_dispatch_kernel, mod_row=mod_row, ne=ne, tile0=tile0, first=first),
        grid_spec=pltpu.PrefetchScalarGridSpec(
            num_scalar_prefetch=len(tabs),
            grid=(m // tm,),
            in_specs=in_specs,
            out_specs=pl.BlockSpec(memory_space=pl.ANY),
            scratch_shapes=[pltpu.VMEM((2, rc, d), BF16), pltpu.VMEM((EXPERT_ROWS, d), BF16),
                            pltpu.SemaphoreType.DMA((3,))],
        ),
        out_shape=jax.ShapeDtypeStruct((rows, d), BF16),
        input_output_aliases={} if first else {len(args) - 1: 0},
        compiler_params=_cparams("arbitrary"),
        name="moe_dispatch",
    )(*args)


def _experts_kernel(te_ref, nu_ref, x_ref, w1_ref, w3_ref, w2_ref, y_ref, acc_ref):
    del te_ref
    i = pl.program_id(0)
    f = pl.program_id(1)
    nf = pl.num_programs(1)
    used = i < nu_ref[0]

    @pl.when(used)
    def _():
        part = _swiglu_partial(x_ref[...], w1_ref[...], w3_ref[...], w2_ref[...])

        @pl.when(f == 0)
        def _():
            acc_ref[...] = part

        @pl.when((f > 0) & (f < nf - 1))
        def _():
            acc_ref[...] += part

        @pl.when(f == nf - 1)
        def _():
            y_ref[...] = (acc_ref[...] + part).astype(y_ref.dtype)

    @pl.when(jnp.logical_not(used) & (f == nf - 1))
    def _():
        y_ref[...] = jnp.zeros_like(y_ref)


def moe_experts(xs, tile_expert, n_used, w1, w3, w2, *, tm, tf):
    rows, d = xs.shape
    ff = w1.shape[2]
    nf = ff // tf
    assert nf >= 2

    def wmap_in(i, f, te, nu):
        live = i < nu[0]
        return (te[i], 0, jnp.where(live, f, nf - 1))

    def wmap_out(i, f, te, nu):
        live = i < nu[0]
        return (te[i], jnp.where(live, f, nf - 1), 0)

    def xmap(i, f, te, nu):
        return (jnp.maximum(jnp.minimum(i, nu[0] - 1), 0), 0)

    return pl.pallas_call(
        _experts_kernel,
        grid_spec=pltpu.PrefetchScalarGridSpec(
            num_scalar_prefetch=2,
            grid=(rows // tm, nf),
            in_specs=[
                pl.BlockSpec((tm, d), xmap),
                pl.BlockSpec((None, d, tf), wmap_in),
                pl.BlockSpec((None, d, tf), wmap_in),
                pl.BlockSpec((None, tf, d), wmap_out),
            ],
            out_specs=pl.BlockSpec((tm, d), lambda i, f, te, nu: (i, 0)),
            scratch_shapes=[pltpu.VMEM((tm, d), F32)],
        ),
        out_shape=jax.ShapeDtypeStruct((rows, d), BF16),
        compiler_params=_cparams("arbitrary", "arbitrary"),
        name="moe_experts",
    )(tile_expert, n_used, xs, w1, w3, w2)


def _combine_kernel(lpad_ref, loff_ref, base_ref, x_ref, ri_ref, rw_ref, g_ref, mod_ref, ys_ref, o_ref,
                    ybuf, sem, *, mod_row, ne, tile0):
    tm, d = x_ref.shape
    rc = ybuf.shape[1]
    i = pl.program_id(0)
    slot = i % 2
    moves = functools.partial(_run_dmas, ne=ne, lpad_ref=lpad_ref, loff_ref=loff_ref, base_ref=base_ref,
                              hbm_ref=ys_ref, to_hbm=False)

    @pl.when(i == 0)
    def _():
        ybuf[...] = jnp.zeros_like(ybuf)
        moves(tile0, buf=ybuf.at[0], sem=sem.at[0], start=True)

    @pl.when(i + 1 < pl.num_programs(0))
    def _():
        moves(i + tile0 + 1, buf=ybuf.at[1 - slot], sem=sem.at[1 - slot], start=True)

    moves(i + tile0, buf=ybuf.at[slot], sem=sem.at[slot], start=False)
    lo1, lo2 = _local_rows(ri_ref, loff_ref, i + tile0, ne)
    lane = lax.broadcasted_iota(jnp.int32, (tm, rc), 1)
    rows = ybuf[slot]
    y1 = jnp.dot(jnp.where(lane == lo1, 1.0, 0.0).astype(BF16), rows, preferred_element_type=F32)
    y2 = jnp.dot(jnp.where(lane == lo2, 1.0, 0.0).astype(BF16), rows, preferred_element_type=F32)
    y = rw_ref[:, 0:1] * y1 + rw_ref[:, 1:2] * y2
    o_ref[...] = x_ref[...] + _mod_slice(mod_ref, mod_row, GA2, d) * _rms(y, g_ref[3:4, :])


def moe_combine(x, norm_g, mods, layer, route_i, route_w, ys, tabs, *, mod_row, tm, ne, tile0):
    m, d = x.shape
    rc = 2 * tm + ne * RUN_ALIGN
    return pl.pallas_call(
        functools.partial(_combine_kernel, mod_row=mod_row, ne=ne, tile0=tile0),
        grid_spec=pltpu.PrefetchScalarGridSpec(
            num_scalar_prefetch=3,
            grid=(m // tm,),
            in_specs=[
                pl.BlockSpec((tm, d), lambda i, *_: (i, 0)),
                pl.BlockSpec((tm, 4), lambda i, *_: (i, 0)),
                pl.BlockSpec((tm, 2), lambda i, *_: (i, 0)),
                pl.BlockSpec((None,) + norm_g.shape[1:], lambda i, *_: (layer, 0, 0)),
                pl.BlockSpec((None,) + mods.shape[1:], lambda i, *_: (layer, 0, 0)),
                pl.BlockSpec(memory_space=pl.ANY),
            ],
            out_specs=pl.BlockSpec((tm, d), lambda i, *_: (i, 0)),
            scratch_shapes=[pltpu.VMEM((2, rc, d), BF16), pltpu.SemaphoreType.DMA((2,))],
        ),
        out_shape=jax.ShapeDtypeStruct((m, d), F32),
        compiler_params=_cparams("arbitrary"),
        name="moe_combine",
    )(*tabs, x, route_i, route_w, norm_g, mods, ys)


def sparse_moe_layer(streams, norm_g, mods, layer, w_router, b_router, w1, w3, w2, *, tf):
    ne = w_router.shape[1]
    tm_exp = EXPERT_ROWS
    routed = [moe_router(x, norm_g, mods, layer, w_router, b_router, mod_row=row, tm=tm) for x, row, tm in streams]
    counts = jnp.concatenate([r[2][:, 0, :] for r in routed], axis=0)
    nt = counts.shape[0]
    m_total = sum(x.shape[0] for x, _, _ in streams)
    lpad = (counts + RUN_ALIGN - 1) // RUN_ALIGN * RUN_ALIGN
    loff = jnp.cumsum(lpad, axis=1) - lpad
    group = jnp.sum(lpad, axis=0)
    gpad = (group + tm_exp - 1) // tm_exp * tm_exp
    ends = jnp.cumsum(gpad)
    starts = ends - gpad
    base = starts[None, :] + jnp.cumsum(lpad, axis=0) - lpad
    n_tiles = -(-(2 * m_total + nt * ne * (RUN_ALIGN - 1)) // tm_exp) + ne
    tile_start = jnp.arange(n_tiles, dtype=jnp.int32) * tm_exp
    n_used = (ends[-1] // tm_exp).astype(jnp.int32).reshape(1)
    tile_expert = jnp.sum(tile_start[:, None] >= ends[None, :], axis=1).astype(jnp.int32)
    tile_expert = jnp.minimum(tile_expert, tile_expert[jnp.maximum(n_used[0] - 1, 0)])
    nt_first = streams[0][0].shape[0] // streams[0][2]
    group_first = jnp.sum(lpad[:nt_first], axis=0)
    later_rows = m_total - streams[0][0].shape[0] + (nt - nt_first) * (RUN_ALIGN - 1)
    assert later_rows + tm_exp - RUN_ALIGN < 2 << TAIL_BITS[0]
    tabs = [a.reshape(-1).astype(jnp.int32)
            for a in (lpad, loff, base, gpad - group_first, starts + group_first, n_used)]
    xs, tile0 = None, 0
    for (x, row, tm), (route_i, _, _) in zip(streams, routed):
        xs = moe_dispatch(x, norm_g, mods, layer, route_i, tabs, n_tiles * tm_exp, xs,
                          mod_row=row, tm=tm, ne=ne, tile0=tile0)
        tile0 += x.shape[0] // tm
    ys = moe_experts(xs, tile_expert, n_used, w1, w3, w2, tm=tm_exp, tf=tf)
    outs, tile0 = [], 0
    for (x, row, tm), (route_i, route_w, _) in zip(streams, routed):
        outs.append(moe_combine(x, norm_g, mods, layer, route_i, route_w, ys, tabs[:3],
                                mod_row=row, tm=tm, ne=ne, tile0=tile0))
        tile0 += x.shape[0] // tm
    return outs


def _rope_tables(seq):
    n = 16
    inv = ROPE_BASE ** (-jnp.arange(n, dtype=F32) / n)
    t = jnp.arange(seq)
    row_ang = (t // GRID_W).astype(F32)[:, None] * inv[None, :]
    col_ang = (t % GRID_W).astype(F32)[:, None] * inv[None, :]
    cos = jnp.concatenate([jnp.cos(row_ang)] * 2 + [jnp.cos(col_ang)] * 2, axis=1)
    sin = jnp.concatenate([-jnp.sin(row_ang), jnp.sin(row_ang), -jnp.sin(col_ang), jnp.sin(col_ang)], axis=1)
    return jnp.tile(cos, (1, 2)), jnp.tile(sin, (1, 2))


def kernel(x, c, ctx, c_ctx, w_ada, b_ada, norm_g, attn_w_qkv, attn_w_o, attn_sink, ret_w_in, ret_w_o,
           ret_log_decay, pool_w, pool_scale, ffn_w1, ffn_w3, ffn_w2, moe_w_router, moe_b_router,
           moe_w1, moe_w3, moe_w2):
    batch, seq, d = x.shape
    assert batch == 1 and c.shape[0] == 1
    depth = w_ada.shape[0]
    lc = ctx.shape[1]
    xl = x.reshape(seq, d)
    xc = ctx.reshape(lc, d)

    cvecs = jnp.zeros((8, d), F32).at[LAT_ROW].set(c[0]).at[CTX_ROW].set(c_ctx)
    mods = ada_table(cvecs, w_ada, b_ada)
    rope = _rope_tables(seq)

    hd = attn_w_o.shape[1]
    dh = hd // ATTN_HEADS
    qkv_scale = jnp.concatenate([jnp.full((hd,), dh ** -0.5, F32),
                                 jnp.ones((attn_w_qkv.shape[2] - hd,), F32)])
    dk = d // RET_HEADS
    ret_scale = jnp.concatenate([jnp.ones((d,), F32), jnp.full((d,), dk ** -0.5, F32),
                                 jnp.ones((ret_w_in.shape[2] - 2 * d,), F32)])

    moe_bf16 = None
    for i in range(depth):
        last = i == depth - 1
        kind, j = i % N_MIXERS, i // N_MIXERS
        proj = functools.partial(norm_mod_matmul, norm_g=norm_g, mods=mods, layer=i, g_row=0, slot=SH1)
        if kind == 0:
            w_qkv = (attn_w_qkv[j] * qkv_scale).astype(BF16)
            w_o = attn_w_o[j].astype(BF16)
            kvd = (w_qkv.shape[1] - hd) // 2
            qkv_l = qkv_projection(xl, norm_g, mods, i, w_qkv, mod_row=LAT_ROW, tm=512, q_cols=hd,
                                   rope=rope, rope_cols=hd + kvd)
            qkv_c = qkv_projection(xc, norm_g, mods, i, w_qkv, mod_row=CTX_ROW, tm=256, q_cols=hd)
            o_l = attention(attn_sink[j], qkv_l, qkv_c, has_local=True)
            xl = outproj_residual(xl, norm_g, mods, i, w_o, [o_l], g_row=1, mod_row=LAT_ROW, slot=GA1, tm=512)
            if not last:
                o_c = attention(attn_sink[j], qkv_c, qkv_c, has_local=False)
                xc = outproj_residual(xc, norm_g, mods, i, w_o, [o_c], g_row=1, mod_row=CTX_ROW, slot=GA1, tm=256)
        elif kind == 1:
            w_in = (ret_w_in[j] * ret_scale).astype(BF16)
            w_o = ret_w_o[j].astype(BF16)
            p_c = proj(xc, w=w_in, mod_row=CTX_ROW, tm=256, tn=1024)
            p_l = proj(xl, w=w_in, mod_row=LAT_ROW, tm=1024, tn=1024)
            s0 = jnp.zeros((RET_HEADS, dk, 2 * dk), F32)
            tabs_f = retention_tables(ret_log_decay[j, 0], min(RET_CHUNK, lc), False)
            tabs_b = retention_tables(ret_log_decay[j, 1], min(RET_CHUNK, lc), True)
            zf_c, s_f = retention_scan(p_c, tabs_f, s0, reverse=False)
            zb_c, s_b = retention_scan(p_c, tabs_b, s0, reverse=True)
            zf_l, _ = retention_scan(p_l, tabs_f, s_f, reverse=False)
            zb_l, _ = retention_scan(p_l, tabs_b, s_b, reverse=True)
            xl = outproj_residual(xl, norm_g, mods, i, w_o, [zf_l, zb_l], g_row=1, mod_row=LAT_ROW, slot=GA1, tm=512)
            if not last:
                xc = outproj_residual(xc, norm_g, mods, i, w_o, [zf_c, zb_c], g_row=1, mod_row=CTX_ROW, slot=GA1, tm=256)
        else:
            w_p = pool_w[j].astype(BF16)
            xl = pool_layer(xl, norm_g, mods, i, w_p, pool_scale[j], mod_row=LAT_ROW, tm=512)
            if not last:
                xc = pool_layer(xc, norm_g, mods, i, w_p, pool_scale[j], mod_row=CTX_ROW, tm=256)

        f = i // 2
        if i % 2 == 0:
            w1, w3, w2 = ffn_w1[f].astype(BF16), ffn_w3[f].astype(BF16), ffn_w2[f].astype(BF16)
            tf = w1.shape[1] // 2
            steps = (seq // min(512, seq)) * 2
            ne = moe_w1.shape[1]
            if not last and steps % ne == 0 and d % (steps // ne * 16) == 0:
                xl, moe_bf16 = ffn_layer(xl, norm_g, mods, i, w1, w3, w2, mod_row=LAT_ROW, tm=512, tf=tf,
                                         cast=((moe_w1, moe_w3, moe_w2), (i + 1) // 2))
            else:
                xl = ffn_layer(xl, norm_g, mods, i, w1, w3, w2, mod_row=LAT_ROW, tm=512, tf=tf)
            if not last:
                xc = ffn_layer(xc, norm_g, mods, i, w1, w3, w2, mod_row=CTX_ROW, tm=256, tf=tf)
        else:
            if moe_bf16 is None:
                moe_bf16 = moe_w1[f].astype(BF16), moe_w3[f].astype(BF16), moe_w2[f].astype(BF16)
            (w1, w3, w2), moe_bf16 = moe_bf16, None
            tf = w1.shape[2] // 2
            streams = [(xl, LAT_ROW, min(512, seq))] + ([] if last else [(xc, CTX_ROW, min(256, lc))])
            outs = sparse_moe_layer(streams, norm_g, mods, i, moe_w_router[f], moe_b_router[f], w1, w3, w2, tf=tf)
            xl = outs[0]
            if not last:
                xc = outs[1]
    return xl.reshape(batch, seq, d)
```

```python
import functools

import jax
import jax.numpy as jnp
from jax import lax
from jax.experimental import pallas as pl
from jax.experimental.pallas import tpu as pltpu

F32 = jnp.float32
BF16 = jnp.bfloat16

EPS = 1e-6
NEG_INF = -1e30
LOG2E = 1.4426950408889634
LANES = 128
VMEM_LIMIT = 56 * 1024 * 1024

GRID_W = 64
N_MIXERS = 3
ATTN_HEADS = 16
ATTN_KV_HEADS = 4
ATTN_BLOCKS_PER_STEP = 4
WINDOW = 128
ROPE_BASE = 10000.0
RET_HEADS = 4
RET_CHUNK = 256
POOL_WINDOWS = (2, 4, 8, 16)
POOL_HALO = 8
N_EXPERTS = 8

SH1, SC1, GA1, SH2, SC2, GA2 = range(6)
LAT_ROW, CTX_ROW = 0, 1


def _cparams(*sem):
    return pltpu.CompilerParams(dimension_semantics=sem, vmem_limit_bytes=VMEM_LIMIT)


def _rms(x, g):
    return x * lax.rsqrt(jnp.mean(x * x, axis=-1, keepdims=True) + EPS) * g


def _mod_slice(mod_ref, row, slot, d):
    return mod_ref[row:row + 1, slot * d:(slot + 1) * d]


def _norm_mod(x, g, shift, scale):
    return _rms(x, g) * (1.0 + scale) + shift


def _ada_kernel(c_ref, w_ref, b_ref, o_ref):
    c = c_ref[...]
    s = c * jax.nn.sigmoid(c)
    o_ref[...] = jnp.dot(s, w_ref[...], preferred_element_type=F32,
                         precision=lax.Precision.HIGHEST) + b_ref[...]


def ada_table(cvecs, w_ada, b_ada):
    depth, d, n = w_ada.shape
    tn = 1536
    return pl.pallas_call(
        _ada_kernel,
        grid=(depth, n // tn),
        in_specs=[
            pl.BlockSpec((8, d), lambda i, j: (0, 0)),
            pl.BlockSpec((None, d, tn), lambda i, j: (i, 0, j)),
            pl.BlockSpec((None, 1, tn), lambda i, j: (i, 0, j)),
        ],
        out_specs=pl.BlockSpec((None, 8, tn), lambda i, j: (i, 0, j)),
        out_shape=jax.ShapeDtypeStruct((depth, 8, n), F32),
        compiler_params=_cparams("parallel", "parallel"),
        name="ada_table",
    )(cvecs, w_ada, b_ada.reshape(depth, 1, n))


def _rope(a, cos, sin):
    lane = lax.broadcasted_iota(jnp.int32, (a.shape[0], LANES), 1)
    first = (lane % 32) < 16
    outs = []
    for cb in range(a.shape[1] // LANES):
        blk = a[:, cb * LANES:(cb + 1) * LANES]
        partner = jnp.where(first, pltpu.roll(blk, LANES - 16, 1), pltpu.roll(blk, 16, 1))
        outs.append(blk * cos + partner * sin)
    return jnp.concatenate(outs, axis=1)


def _mixer_proj_kernel(x_ref, g_ref, mod_ref, w_ref, *rest, mod_row, chunk, rope_cols, q_cols):
    if rope_cols:
        cos_ref, sin_ref, o_ref = rest
        cos, sin = cos_ref[...], sin_ref[...]
    else:
        (o_ref,) = rest
    d = x_ref.shape[1]
    h = _norm_mod(x_ref[...], g_ref[0:1, :], _mod_slice(mod_ref, mod_row, SH1, d),
                  _mod_slice(mod_ref, mod_row, SC1, d)).astype(BF16)
    for c0 in range(0, o_ref.shape[1], chunk):
        acc = jnp.dot(h, w_ref[:, c0:c0 + chunk], preferred_element_type=F32)
        if c0 < q_cols:
            acc = acc * LOG2E
        if c0 < rope_cols:
            acc = _rope(acc, cos, sin)
        o_ref[:, c0:c0 + chunk] = acc.astype(o_ref.dtype)


def mixer_projection(x, norm_g, mods, layer, w, *, mod_row, tm, chunk, q_cols=0, rope=None, rope_cols=0):
    m, d = x.shape
    n = w.shape[1]
    tm = min(tm, m)
    assert n % chunk == 0 and rope_cols % chunk == 0 and q_cols % chunk == 0
    in_specs = [
        pl.BlockSpec((tm, d), lambda i: (i, 0)),
        pl.BlockSpec((None,) + norm_g.shape[1:], lambda i: (layer, 0, 0)),
        pl.BlockSpec((None,) + mods.shape[1:], lambda i: (layer, 0, 0)),
        pl.BlockSpec((d, n), lambda i: (0, 0), pipeline_mode=pl.Buffered(1)),
    ]
    args = [x, norm_g, mods, w]
    if rope_cols:
        in_specs += [pl.BlockSpec((tm, LANES), lambda i: (i, 0))] * 2
        args += list(rope)
    return pl.pallas_call(
        functools.partial(_mixer_proj_kernel, mod_row=mod_row, chunk=chunk, rope_cols=rope_cols, q_cols=q_cols),
        grid=(m // tm,),
        in_specs=in_specs,
        out_specs=pl.BlockSpec((tm, n), lambda i: (i, 0)),
        out_shape=jax.ShapeDtypeStruct((m, n), BF16),
        compiler_params=_cparams("parallel"),
        name="mixer_projection",
    )(*args)


def _outproj_kernel(x_ref, g_ref, mod_ref, w_ref, *rest, n_y, g_row, mod_row, slot):
    y_refs, o_ref = rest[:n_y], rest[n_y]
    d = x_ref.shape[1]
    y = y_refs[0][...]
    if n_y == 2:
        y = (y.astype(F32) + y_refs[1][...].astype(F32)).astype(BF16)
    t = jnp.dot(y, w_ref[...], preferred_element_type=F32)
    gate = _mod_slice(mod_ref, mod_row, slot, d)
    o_ref[...] = x_ref[...] + gate * _rms(t, g_ref[g_row:g_row + 1, :])


def outproj_residual(x, norm_g, mods, layer, w, ys, *, g_row, mod_row, slot, tm):
    m, d = x.shape
    k = w.shape[0]
    tm = min(tm, m)
    in_specs = [
        pl.BlockSpec((tm, d), lambda i: (i, 0)),
        pl.BlockSpec((None,) + norm_g.shape[1:], lambda i: (layer, 0, 0)),
        pl.BlockSpec((None,) + mods.shape[1:], lambda i: (layer, 0, 0)),
        pl.BlockSpec((k, d), lambda i: (0, 0)),
    ] + [pl.BlockSpec((tm, k), lambda i: (i, 0))] * len(ys)
    return pl.pallas_call(
        functools.partial(_outproj_kernel, n_y=len(ys), g_row=g_row, mod_row=mod_row, slot=slot),
        grid=(m // tm,),
        in_specs=in_specs,
        out_specs=pl.BlockSpec((tm, d), lambda i: (i, 0)),
        out_shape=jax.ShapeDtypeStruct((m, d), F32),
        compiler_params=_cparams("parallel"),
        name="outproj_residual",
    )(x, norm_g, mods, w, *ys)


def _dot_nt(a, b):
    return lax.dot_general(a, b, (((1,), (1,)), ((), ())), preferred_element_type=F32)


def _attn_kernel(sink_ref, q_ref, *rest, tq, nsub, kv_heads, group, dh, has_local):
    if has_local:
        k_refs, v_refs = rest[:nsub + 2], rest[nsub + 2:2 * nsub + 4]
    kc_ref, vc_ref, o_ref = rest[-3:]
    assert 2 * dh == LANES and tq == WINDOW and group % 2 == 0
    b = pl.program_id(0)
    nb = pl.num_programs(0)
    rows = group * tq
    lc = kc_ref.shape[0]
    head_of_row = lax.broadcasted_iota(jnp.int32, (rows, 1), 0) // tq
    lane = lax.broadcasted_iota(jnp.int32, (1, LANES), 1)
    if has_local:
        r = lax.broadcasted_iota(jnp.int32, (rows, tq), 0) % tq
        c = lax.broadcasted_iota(jnp.int32, (rows, tq), 1)
    for sub, j in [(sub, j) for sub in range(nsub) for j in range(kv_heads)]:
        qrows = slice(sub * tq, (sub + 1) * tq)
        hs = slice(j * dh, (j + 1) * dh)
        pair = slice((j // 2) * LANES, (j // 2 + 1) * LANES)
        v_low = j % 2 == 0
        keep_v = (lane < dh) if v_low else (lane >= dh)
        qg = jnp.concatenate(
            [q_ref[qrows, (j * group + g) * dh:(j * group + g + 1) * dh] for g in range(group)], axis=0)
        sink = jnp.full((rows, 1), sink_ref[j * group] * LOG2E, F32)
        for g in range(1, group):
            sink = jnp.where(head_of_row == g, sink_ref[j * group + g] * LOG2E, sink)
        if has_local:
            kp_ref, kq_ref, kn_ref = k_refs[sub:sub + 3]
            vp_ref, vq_ref, vn_ref = v_refs[sub:sub + 3]
            prev_ok = (c >= r) if sub > 0 else (c >= r) & (b > 0)
            next_ok = (c <= r) if sub < nsub - 1 else (c <= r) & (b < nb - 1)
            k_all = jnp.concatenate([kc_ref[:, hs], kp_ref[:, hs], kq_ref[:, hs], kn_ref[:, hs]], axis=0)
            v_all = jnp.concatenate([vc_ref[:, pair], vp_ref[:, pair], vq_ref[:, pair], vn_ref[:, pair]], axis=0)
            s = _dot_nt(qg, k_all)
            s = jnp.concatenate([s[:, :lc],
                                 jnp.where(prev_ok, s[:, lc:lc + tq], NEG_INF),
                                 s[:, lc + tq:lc + 2 * tq],
                                 jnp.where(next_ok, s[:, lc + 2 * tq:], NEG_INF)], axis=1)
        else:
            v_all = vc_ref[:, pair]
            s = _dot_nt(qg, kc_ref[:, hs])
        m = jnp.maximum(jnp.max(s, axis=-1, keepdims=True), sink)
        p = jnp.exp2(s - m).astype(BF16)
        v_ones = jnp.where(keep_v, v_all, jnp.ones_like(v_all))
        pv = jnp.dot(p, v_ones, preferred_element_type=F32)
        den = pltpu.roll(pv, dh, 1) + jnp.exp2(sink - m)
        o = pv / den
        for g in range(0, group, 2):
            a, bb = o[g * tq:(g + 1) * tq], o[(g + 1) * tq:(g + 2) * tq]
            if v_low:
                both = jnp.where(lane < dh, a, pltpu.roll(bb, dh, 1))
            else:
                both = jnp.where(lane < dh, pltpu.roll(a, dh, 1), bb)
            h = j * group + g
            o_ref[qrows, h * dh:(h + 2) * dh] = both.astype(o_ref.dtype)


def attention(sink, qkv, qkv_ctx, *, has_local):
    s = qkv.shape[0]
    dh = qkv.shape[1] // (ATTN_HEADS + 2 * ATTN_KV_HEADS)
    hd = ATTN_HEADS * dh
    kvd = ATTN_KV_HEADS * dh
    kcol, vcol = hd // kvd, hd // kvd + 1
    tq = WINDOW
    nb = s // tq
    nsub = min(ATTN_BLOCKS_PER_STEP, nb)
    lc = qkv_ctx.shape[0]
    in_specs = [
        pl.BlockSpec(memory_space=pltpu.SMEM),
        pl.BlockSpec((nsub * tq, hd), lambda b: (b, 0)),
    ]
    args = [sink, qkv]
    if has_local:
        for col in (kcol, vcol):
            for off in range(-1, nsub + 1):
                in_specs.append(pl.BlockSpec(
                    (tq, kvd), lambda b, col=col, off=off: (jnp.clip(nsub * b + off, 0, nb - 1), col)))
                args.append(qkv)
    in_specs += [pl.BlockSpec((lc, kvd), lambda b: (0, kcol)), pl.BlockSpec((lc, kvd), lambda b: (0, vcol))]
    args += [qkv_ctx, qkv_ctx]
    return pl.pallas_call(
        functools.partial(_attn_kernel, tq=tq, nsub=nsub, kv_heads=ATTN_KV_HEADS,
                          group=ATTN_HEADS // ATTN_KV_HEADS, dh=dh, has_local=has_local),
        grid=(nb // nsub,),
        in_specs=in_specs,
        out_specs=pl.BlockSpec((nsub * tq, hd), lambda b: (b, 0)),
        out_shape=jax.ShapeDtypeStruct((s, hd), BF16),
        compiler_params=_cparams("parallel"),
        name="attention",
    )(*args)


def _ret_kernel(q_ref, k_ref, v_ref, gate_ref, intra_ref, qdec_ref, kdec_ref, cdec_ref, s0_ref,
                z_ref, sfin_ref, state_ref, *, heads, dk, dv):
    i = pl.program_id(0)

    @pl.when(i == 0)
    def _():
        state_ref[...] = s0_ref[...]

    for h in range(heads):
        q = q_ref[:, h * dk:(h + 1) * dk]
        k = k_ref[:, h * dk:(h + 1) * dk]
        v = v_ref[:, h * dv:(h + 1) * dv]
        state = state_ref[h]
        sc = _dot_nt(q, k) * intra_ref[h]
        qd = (q.astype(F32) * qdec_ref[h]).astype(BF16)
        o = (jnp.dot(sc.astype(BF16), v, preferred_element_type=F32)
             + jnp.dot(qd, state.astype(BF16), preferred_element_type=F32))
        kd = (k.astype(F32) * kdec_ref[h]).astype(BF16)
        state_ref[h] = state * cdec_ref[h] + lax.dot_general(
            kd, v, (((0,), (0,)), ((), ())), preferred_element_type=F32)
        mu = jnp.mean(o, axis=-1, keepdims=True)
        oc = o - mu
        var = jnp.mean(oc * oc, axis=-1, keepdims=True)
        gate = gate_ref[:, h * dv:(h + 1) * dv].astype(F32)
        z = gate * jax.nn.sigmoid(gate) * (oc * lax.rsqrt(var + EPS))
        z_ref[:, h * dv:(h + 1) * dv] = z.astype(z_ref.dtype)

    @pl.when(i == pl.num_programs(0) - 1)
    def _():
        sfin_ref[...] = state_ref[...]


def retention_scan(proj, tables, s0, *, reverse):
    m, n = proj.shape
    d = n // 8
    heads = RET_HEADS
    dk, dv = d // heads, 2 * d // heads
    c = min(RET_CHUNK, m)
    nc = m // c
    intra, qdec, kdec, cdec = tables

    def row(i):
        return nc - 1 - i if reverse else i

    gate_blk = 3 if reverse else 2
    full = lambda a: pl.BlockSpec(a.shape, lambda i: (0,) * a.ndim)
    z, sfin = pl.pallas_call(
        functools.partial(_ret_kernel, heads=heads, dk=dk, dv=dv),
        grid=(nc,),
        in_specs=[
            pl.BlockSpec((c, d), lambda i: (row(i), 0)),
            pl.BlockSpec((c, d), lambda i: (row(i), 1)),
            pl.BlockSpec((c, 2 * d), lambda i: (row(i), 1)),
            pl.BlockSpec((c, 2 * d), lambda i: (row(i), gate_blk)),
            full(intra), full(qdec), full(kdec), full(cdec), full(s0),
        ],
        out_specs=[
            pl.BlockSpec((c, 2 * d), lambda i: (row(i), 0)),
            full(s0),
        ],
        out_shape=[jax.ShapeDtypeStruct((m, 2 * d), BF16), jax.ShapeDtypeStruct(s0.shape, F32)],
        scratch_shapes=[pltpu.VMEM(s0.shape, F32)],
        compiler_params=_cparams("arbitrary"),
        name="retention_scan",
    )(proj, proj, proj, proj, intra, qdec, kdec, cdec, s0)
    return z, sfin


def retention_tables(log_decay_row, c, reverse):
    lg = -jnp.exp(log_decay_row.astype(F32))
    idx = jnp.arange(c, dtype=F32)
    diff = idx[:, None] - idx[None, :]
    if reverse:
        diff = -diff
    intra = jnp.where(diff >= 0, jnp.exp(lg[:, None, None] * jnp.maximum(diff, 0.0)), 0.0)
    fwd_idx = (c - 1.0 - idx) if reverse else idx
    qdec = jnp.exp(lg[:, None] * (fwd_idx + 1.0))[:, :, None]
    kdec = jnp.exp(lg[:, None] * (c - 1.0 - fwd_idx))[:, :, None]
    cdec = jnp.exp(lg * c)[:, None, None]
    return intra, qdec, kdec, cdec


def _pool_kernel(x_ref, xp_ref, xn_ref, g_ref, mod_ref, w_ref, ps_ref, o_ref, h_ref, *, mod_row, seq):
    tm, d = x_ref.shape
    halo = POOL_HALO
    i = pl.program_id(0)
    g0 = g_ref[0:1, :]
    shift = _mod_slice(mod_ref, mod_row, SH1, d)
    scale = _mod_slice(mod_ref, mod_row, SC1, d)
    hp = _norm_mod(xp_ref[...], g0, shift, scale)
    hn = _norm_mod(xn_ref[...], g0, shift, scale)
    h_ref[0:halo, :] = jnp.where(i > 0, hp, 0.0)
    h_ref[halo:halo + tm, :] = _norm_mod(x_ref[...], g0, shift, scale)
    h_ref[halo + tm:, :] = jnp.where(i < pl.num_programs(0) - 1, hn, 0.0)

    t = i * tm + lax.broadcasted_iota(jnp.int32, (tm, 1), 0)
    gw = d // len(POOL_WINDOWS)
    ys = []
    for g, w in enumerate(POOL_WINDOWS):
        cols = slice(g * gw, (g + 1) * gw)
        tot = h_ref[halo - w // 2:halo - w // 2 + tm, cols]
        for off in range(-w // 2 + 1, w // 2):
            tot = tot + h_ref[halo + off:halo + off + tm, cols]
        cnt = (jnp.minimum(t + w // 2, seq) - jnp.maximum(t - w // 2, 0)).astype(F32)
        dm = tot / cnt - h_ref[halo:halo + tm, cols]
        ys.append(jnp.dot(dm.astype(BF16), w_ref[g], preferred_element_type=F32))
    y = jnp.concatenate(ys, axis=1) * ps_ref[...]
    o_ref[...] = x_ref[...] + _mod_slice(mod_ref, mod_row, GA1, d) * _rms(y, g_ref[1:2, :])


def pool_layer(x, norm_g, mods, layer, w_pool, pool_scale, *, mod_row, tm):
    m, d = x.shape
    tm = min(tm, m)
    nt = m // tm
    hb = tm // POOL_HALO
    return pl.pallas_call(
        functools.partial(_pool_kernel, mod_row=mod_row, seq=m),
        grid=(nt,),
        in_specs=[
            pl.BlockSpec((tm, d), lambda i: (i, 0)),
            pl.BlockSpec((POOL_HALO, d), lambda i: (jnp.maximum(i * hb - 1, 0), 0)),
            pl.BlockSpec((POOL_HALO, d), lambda i: (jnp.minimum((i + 1) * hb, nt * hb - 1), 0)),
            pl.BlockSpec((None,) + norm_g.shape[1:], lambda i: (layer, 0, 0)),
            pl.BlockSpec((None,) + mods.shape[1:], lambda i: (layer, 0, 0)),
            pl.BlockSpec(w_pool.shape, lambda i: (0, 0, 0)),
            pl.BlockSpec((1, d), lambda i: (0, 0)),
        ],
        out_specs=pl.BlockSpec((tm, d), lambda i: (i, 0)),
        out_shape=jax.ShapeDtypeStruct((m, d), F32),
        scratch_shapes=[pltpu.VMEM((tm + 2 * POOL_HALO, d), F32)],
        compiler_params=_cparams("parallel"),
        name="pool_layer",
    )(x, x, x, norm_g, mods, w_pool, pool_scale.reshape(1, d))


def _swiglu_partial(h, w1, w3, w2):
    a = jnp.dot(h, w1, preferred_element_type=F32)
    b = jnp.dot(h, w3, preferred_element_type=F32)
    u = (a * jax.nn.sigmoid(a) * b).astype(BF16)
    return jnp.dot(u, w2, preferred_element_type=F32)


def _ffn_kernel(x_ref, g_ref, mod_ref, w1_ref, w3_ref, w2_ref, *rest, mod_row, n_cast):
    cast_src, o_ref, cast_dst = rest[:n_cast], rest[n_cast], rest[n_cast + 1:2 * n_cast + 1]
    h_ref, acc_ref = rest[2 * n_cast + 1:]
    d = x_ref.shape[1]
    f = pl.program_id(1)
    for src, dst in zip(cast_src, cast_dst):
        dst[...] = src[...].astype(dst.dtype)

    @pl.when(f == 0)
    def _():
        h = _norm_mod(x_ref[...], g_ref[2:3, :], _mod_slice(mod_ref, mod_row, SH2, d),
                      _mod_slice(mod_ref, mod_row, SC2, d))
        h_ref[...] = h.astype(BF16)

    part = _swiglu_partial(h_ref[...], w1_ref[f], w3_ref[f], w2_ref[f])

    @pl.when(f == 0)
    def _():
        acc_ref[...] = part

    @pl.when(f > 0)
    def _():
        acc_ref[...] += part

    @pl.when(f == pl.num_programs(1) - 1)
    def _():
        o_ref[...] = x_ref[...] + _mod_slice(mod_ref, mod_row, GA2, d) * _rms(acc_ref[...], g_ref[3:4, :])


def ffn_layer(x, norm_g, mods, layer, w1, w3, w2, *, mod_row, tm, tf, cast=None):
    m, d = x.shape
    ff = w1.shape[1]
    tm = min(tm, m)
    nf = ff // tf
    steps = (m // tm) * nf
    in_specs = [
        pl.BlockSpec((tm, d), lambda i, f: (i, 0)),
        pl.BlockSpec((None,) + norm_g.shape[1:], lambda i, f: (layer, 0, 0)),
        pl.BlockSpec((None,) + mods.shape[1:], lambda i, f: (layer, 0, 0)),
        pl.BlockSpec((nf, d, tf), lambda i, f: (0, 0, 0), pipeline_mode=pl.Buffered(1)),
        pl.BlockSpec((nf, d, tf), lambda i, f: (0, 0, 0), pipeline_mode=pl.Buffered(1)),
        pl.BlockSpec((nf, tf, d), lambda i, f: (0, 0, 0), pipeline_mode=pl.Buffered(1)),
    ]
    w1 = w1.reshape(d, nf, tf).transpose(1, 0, 2)
    w3 = w3.reshape(d, nf, tf).transpose(1, 0, 2)
    w2 = w2.reshape(nf, tf, d)
    out_specs = [pl.BlockSpec((tm, d), lambda i, f: (i, 0))]
    out_shape = [jax.ShapeDtypeStruct((m, d), F32)]
    cast_arrays, cast_idx = cast if cast is not None else ((), 0)
    for a in cast_arrays:
        _, ne, r, c = a.shape
        per_expert = steps // ne
        assert per_expert * ne == steps and r % (per_expert * 16) == 0
        rows = r // per_expert
        in_specs.append(pl.BlockSpec(
            (None, None, rows, c),
            lambda i, f, pe=per_expert: (cast_idx, (i * nf + f) // pe, (i * nf + f) % pe, 0)))
        out_specs.append(pl.BlockSpec(
            (None, rows, c), lambda i, f, pe=per_expert: ((i * nf + f) // pe, (i * nf + f) % pe, 0)))
        out_shape.append(jax.ShapeDtypeStruct((ne, r, c), BF16))
    outs = pl.pallas_call(
        functools.partial(_ffn_kernel, mod_row=mod_row, n_cast=len(cast_arrays)),
        grid=(m // tm, nf),
        in_specs=in_specs,
        out_specs=out_specs,
        out_shape=out_shape,
        scratch_shapes=[pltpu.VMEM((tm, d), BF16), pltpu.VMEM((tm, d), F32)],
        compiler_params=_cparams("arbitrary", "arbitrary"),
        name="ffn_layer",
    )(x, norm_g, mods, w1, w3, w2, *cast_arrays)
    return outs[0] if cast is None else (outs[0], tuple(outs[1:]))


RUN_ALIGN = 16
RUN_BITS = tuple(range(9, 3, -1))
EXPERT_ROWS = 512
TAIL_BITS = tuple(range(9, 3, -1))


def _split_dma(length, src, src_row, dst, dst_row, sem, bits, *, start, src_advances=True):
    for b in bits:
        size = 1 << b
        off = (length >> (b + 1)) << (b + 1)

        @pl.when(((length >> b) & 1) == 1)
        def _():
            s_row = pl.multiple_of(src_row + off, RUN_ALIGN) if src_advances else src_row
            cp = pltpu.make_async_copy(src.at[pl.ds(s_row, size)],
                                       dst.at[pl.ds(pl.multiple_of(dst_row + off, RUN_ALIGN), size)], sem)
            if start:
                cp.start()
            else:
                cp.wait()


def _run_dmas(tile, ne, lpad_ref, loff_ref, base_ref, buf, hbm_ref, sem, *, to_hbm, start):
    for e in range(ne):
        length = lpad_ref[tile * ne + e]
        lo = loff_ref[tile * ne + e]
        gb = base_ref[tile * ne + e]
        if to_hbm:
            _split_dma(length, buf, lo, hbm_ref, gb, sem, RUN_BITS, start=start)
        else:
            _split_dma(length, hbm_ref, gb, buf, lo, sem, RUN_BITS, start=start)


def _split_bf16(a):
    hi = a.astype(BF16)
    return hi, (a - hi.astype(F32)).astype(BF16)


def _dot_f32x3(a, b_hi, b_lo):
    a_hi, a_lo = _split_bf16(a)
    dot = functools.partial(jnp.dot, preferred_element_type=F32)
    return dot(a_hi, b_hi) + (dot(a_hi, b_lo) + dot(a_lo, b_hi))


def _local_rows(ri_ref, loff_ref, tile, ne):
    e1, e2 = ri_ref[:, 0:1], ri_ref[:, 1:2]
    lo1, lo2 = ri_ref[:, 2:3], ri_ref[:, 3:4]
    for e in range(ne):
        off = loff_ref[tile * ne + e]
        lo1 = lo1 + jnp.where(e1 == e, off, 0)
        lo2 = lo2 + jnp.where(e2 == e, off, 0)
    return lo1, lo2


def _router_kernel(x_ref, g_ref, mod_ref, wr_hi_ref, wr_lo_ref, br_ref, ri_ref, rw_ref, cnt_ref, *, mod_row):
    tm, d = x_ref.shape
    ne = wr_hi_ref.shape[1]
    h = _norm_mod(x_ref[...], g_ref[2:3, :], _mod_slice(mod_ref, mod_row, SH2, d),
                  _mod_slice(mod_ref, mod_row, SC2, d))
    logits = _dot_f32x3(h, wr_hi_ref[...], wr_lo_ref[...]) + br_ref[...]
    lane = lax.broadcasted_iota(jnp.int32, logits.shape, 1)
    v1 = jnp.max(logits, axis=-1, keepdims=True)
    i1 = jnp.min(jnp.where(logits == v1, lane, ne), axis=-1, keepdims=True)
    rest = jnp.where(lane == i1, -jnp.inf, logits)
    v2 = jnp.max(rest, axis=-1, keepdims=True)
    i2 = jnp.min(jnp.where(rest == v2, lane, ne), axis=-1, keepdims=True)
    e2 = jnp.exp(v2 - v1)
    den = 1.0 + e2
    oh1 = (lane == i1).astype(F32)
    oh2 = (lane == i2).astype(F32)
    both = oh1 + oh2
    r = lax.broadcasted_iota(jnp.int32, (tm, tm), 0)
    c = lax.broadcasted_iota(jnp.int32, (tm, tm), 1)
    tri = jnp.where(c < r, 1.0, 0.0).astype(BF16)
    rank_all = jnp.dot(tri, both.astype(BF16), preferred_element_type=F32)
    rank1 = jnp.sum(rank_all * oh1, axis=-1, keepdims=True).astype(jnp.int32)
    rank2 = jnp.sum(rank_all * oh2, axis=-1, keepdims=True).astype(jnp.int32)
    col = lax.broadcasted_iota(jnp.int32, ri_ref.shape, 1)
    ri_ref[...] = jnp.where(col == 0, i1, jnp.where(col == 1, i2, jnp.where(col == 2, rank1, rank2)))
    colw = lax.broadcasted_iota(jnp.int32, rw_ref.shape, 1)
    rw_ref[...] = jnp.where(colw == 0, 1.0 / den, e2 / den)
    cnt_ref[...] = jnp.sum(both, axis=0, keepdims=True).astype(jnp.int32)


def moe_router(x, norm_g, mods, layer, w_router, b_router, *, mod_row, tm):
    m, d = x.shape
    ne = w_router.shape[1]
    return pl.pallas_call(
        functools.partial(_router_kernel, mod_row=mod_row),
        grid=(m // tm,),
        in_specs=[
            pl.BlockSpec((tm, d), lambda i: (i, 0)),
            pl.BlockSpec((None,) + norm_g.shape[1:], lambda i: (layer, 0, 0)),
            pl.BlockSpec((None,) + mods.shape[1:], lambda i: (layer, 0, 0)),
            pl.BlockSpec((d, ne), lambda i: (0, 0)),
            pl.BlockSpec((d, ne), lambda i: (0, 0)),
            pl.BlockSpec((1, ne), lambda i: (0, 0)),
        ],
        out_specs=[
            pl.BlockSpec((tm, 4), lambda i: (i, 0)),
            pl.BlockSpec((tm, 2), lambda i: (i, 0)),
            pl.BlockSpec((None, 1, ne), lambda i: (i, 0, 0)),
        ],
        out_shape=[
            jax.ShapeDtypeStruct((m, 4), jnp.int32),
            jax.ShapeDtypeStruct((m, 2), F32),
            jax.ShapeDtypeStruct((m // tm, 1, ne), jnp.int32),
        ],
        compiler_params=_cparams("parallel"),
        name="moe_router",
    )(x, norm_g, mods, *_split_bf16(w_router), b_router.reshape(1, ne))


def _dispatch_kernel(lpad_ref, loff_ref, base_ref, tlen_ref, tstart_ref, nu_ref, x_ref, g_ref, mod_ref, ri_ref,
                     *rest, mod_row, ne, tile0, first):
    xs_ref, cbuf, zbuf, sem = rest if first else rest[1:]
    tm, d = x_ref.shape
    rc = cbuf.shape[1]
    i = pl.program_id(0)
    last = i == pl.num_programs(0) - 1
    slot = i % 2

    def zero_tails(start):
        for e in range(ne):
            _split_dma(tlen_ref[e], zbuf, 0, xs_ref, tstart_ref[e], sem.at[2], TAIL_BITS, start=start,
                       src_advances=False)

        def unused_tile(t, carry):
            cp = pltpu.make_async_copy(
                zbuf, xs_ref.at[pl.ds(pl.multiple_of(t * EXPERT_ROWS, EXPERT_ROWS), EXPERT_ROWS)], sem.at[2])
            if start:
                cp.start()
            else:
                cp.wait()
            return carry

        lax.fori_loop(nu_ref[0], xs_ref.shape[0] // EXPERT_ROWS, unused_tile, 0)

    if first:
        @pl.when(i == 0)
        def _():
            zbuf[...] = jnp.zeros_like(zbuf)
            zero_tails(True)

    h = _norm_mod(x_ref[...], g_ref[2:3, :], _mod_slice(mod_ref, mod_row, SH2, d),
                  _mod_slice(mod_ref, mod_row, SC2, d)).astype(BF16)
    lo1, lo2 = _local_rows(ri_ref, loff_ref, i + tile0, ne)
    lane = lax.broadcasted_iota(jnp.int32, (tm, rc), 1)
    onehot = jnp.where((lane == lo1) | (lane == lo2), 1.0, 0.0).astype(BF16)
    sorted_rows = lax.dot_general(onehot, h, (((0,), (0,)), ((), ())), preferred_element_type=F32)
    cbuf[slot] = sorted_rows.astype(BF16)
    moves = functools.partial(_run_dmas, ne=ne, lpad_ref=lpad_ref, loff_ref=loff_ref, base_ref=base_ref,
                              hbm_ref=xs_ref, to_hbm=True)
    moves(i + tile0, buf=cbuf.at[slot], sem=sem.at[slot], start=True)

    @pl.when(i > 0)
    def _():
        moves(i + tile0 - 1, buf=cbuf.at[1 - slot], sem=sem.at[1 - slot], start=False)

    @pl.when(last)
    def _():
        moves(i + tile0, buf=cbuf.at[slot], sem=sem.at[slot], start=False)
        if first:
            zero_tails(False)


def moe_dispatch(x, norm_g, mods, layer, route_i, tabs, rows, xs=None, *, mod_row, tm, ne, tile0):
    m, d = x.shape
    rc = 2 * tm + ne * RUN_ALIGN
    first = xs is None
    in_specs = [
        pl.BlockSpec((tm, d), lambda i, *_: (i, 0)),
        pl.BlockSpec((None,) + norm_g.shape[1:], lambda i, *_: (layer, 0, 0)),
        pl.BlockSpec((None,) + mods.shape[1:], lambda i, *_: (layer, 0, 0)),
        pl.BlockSpec((tm, 4), lambda i, *_: (i, 0)),
    ]
    args = [*tabs, x, norm_g, mods, route_i]
    if not first:
        in_specs.append(pl.BlockSpec(memory_space=pl.ANY))
        args.append(xs)
    return pl.pallas_call(
        functools.partial(_dispatch_kernel, mod_row=mod_row, ne=ne, tile0=tile0, first=first),
        grid_spec=pltpu.PrefetchScalarGridSpec(
            num_scalar_prefetch=len(tabs),
            grid=(m // tm,),
            in_specs=in_specs,
            out_specs=pl.BlockSpec(memory_space=pl.ANY),
            scratch_shapes=[pltpu.VMEM((2, rc, d), BF16), pltpu.VMEM((EXPERT_ROWS, d), BF16),
                            pltpu.SemaphoreType.DMA((3,))],
        ),
        out_shape=jax.ShapeDtypeStruct((rows, d), BF16),
        input_output_aliases={} if first else {len(args) - 1: 0},
        compiler_params=_cparams("arbitrary"),
        name="moe_dispatch",
    )(*args)


def _experts_kernel(te_ref, nu_ref, x_ref, w1_ref, w3_ref, w2_ref, y_ref, acc_ref):
    del te_ref
    i = pl.program_id(0)
    f = pl.program_id(1)
    nf = pl.num_programs(1)
    used = i < nu_ref[0]

    @pl.when(used)
    def _():
        part = _swiglu_partial(x_ref[...], w1_ref[...], w3_ref[...], w2_ref[...])

        @pl.when(f == 0)
        def _():
            acc_ref[...] = part

        @pl.when((f > 0) & (f < nf - 1))
        def _():
            acc_ref[...] += part

        @pl.when(f == nf - 1)
        def _():
            y_ref[...] = (acc_ref[...] + part).astype(y_ref.dtype)

    @pl.when(jnp.logical_not(used) & (f == nf - 1))
    def _():
        y_ref[...] = jnp.zeros_like(y_ref)


def moe_experts(xs, tile_expert, n_used, w1, w3, w2, *, tm, tf):
    rows, d = xs.shape
    ff = w1.shape[2]
    nf = ff // tf
    assert nf >= 2

    def wmap_in(i, f, te, nu):
        live = i < nu[0]
        return (te[i], 0, jnp.where(live, f, nf - 1))

    def wmap_out(i, f, te, nu):
        live = i < nu[0]
        return (te[i], jnp.where(live, f, nf - 1), 0)

    def xmap(i, f, te, nu):
        return (jnp.maximum(jnp.minimum(i, nu[0] - 1), 0), 0)

    return pl.pallas_call(
        _experts_kernel,
        grid_spec=pltpu.PrefetchScalarGridSpec(
            num_scalar_prefetch=2,
            grid=(rows // tm, nf),
            in_specs=[
                pl.BlockSpec((tm, d), xmap),
                pl.BlockSpec((None, d, tf), wmap_in),
                pl.BlockSpec((None, d, tf), wmap_in),
                pl.BlockSpec((None, tf, d), wmap_out),
            ],
            out_specs=pl.BlockSpec((tm, d), lambda i, f, te, nu: (i, 0)),
            scratch_shapes=[pltpu.VMEM((tm, d), F32)],
        ),
        out_shape=jax.ShapeDtypeStruct((rows, d), BF16),
        compiler_params=_cparams("arbitrary", "arbitrary"),
        name="moe_experts",
    )(tile_expert, n_used, xs, w1, w3, w2)


def _combine_kernel(lpad_ref, loff_ref, base_ref, x_ref, ri_ref, rw_ref, g_ref, mod_ref, ys_ref, o_ref,
                    ybuf, sem, *, mod_row, ne, tile0):
    tm, d = x_ref.shape
    rc = ybuf.shape[1]
    i = pl.program_id(0)
    slot = i % 2
    moves = functools.partial(_run_dmas, ne=ne, lpad_ref=lpad_ref, loff_ref=loff_ref, base_ref=base_ref,
                              hbm_ref=ys_ref, to_hbm=False)

    @pl.when(i == 0)
    def _():
        ybuf[...] = jnp.zeros_like(ybuf)
        moves(tile0, buf=ybuf.at[0], sem=sem.at[0], start=True)

    @pl.when(i + 1 < pl.num_programs(0))
    def _():
        moves(i + tile0 + 1, buf=ybuf.at[1 - slot], sem=sem.at[1 - slot], start=True)

    moves(i + tile0, buf=ybuf.at[slot], sem=sem.at[slot], start=False)
    lo1, lo2 = _local_rows(ri_ref, loff_ref, i + tile0, ne)
    lane = lax.broadcasted_iota(jnp.int32, (tm, rc), 1)
    rows = ybuf[slot]
    y1 = jnp.dot(jnp.where(lane == lo1, 1.0, 0.0).astype(BF16), rows, preferred_element_type=F32)
    y2 = jnp.dot(jnp.where(lane == lo2, 1.0, 0.0).astype(BF16), rows, preferred_element_type=F32)
    y = rw_ref[:, 0:1] * y1 + rw_ref[:, 1:2] * y2
    o_ref[...] = x_ref[...] + _mod_slice(mod_ref, mod_row, GA2, d) * _rms(y, g_ref[3:4, :])


def moe_combine(x, norm_g, mods, layer, route_i, route_w, ys, tabs, *, mod_row, tm, ne, tile0):
    m, d = x.shape
    rc = 2 * tm + ne * RUN_ALIGN
    return pl.pallas_call(
        functools.partial(_combine_kernel, mod_row=mod_row, ne=ne, tile0=tile0),
        grid_spec=pltpu.PrefetchScalarGridSpec(
            num_scalar_prefetch=3,
            grid=(m // tm,),
            in_specs=[
                pl.BlockSpec((tm, d), lambda i, *_: (i, 0)),
                pl.BlockSpec((tm, 4), lambda i, *_: (i, 0)),
                pl.BlockSpec((tm, 2), lambda i, *_: (i, 0)),
                pl.BlockSpec((None,) + norm_g.shape[1:], lambda i, *_: (layer, 0, 0)),
                pl.BlockSpec((None,) + mods.shape[1:], lambda i, *_: (layer, 0, 0)),
                pl.BlockSpec(memory_space=pl.ANY),
            ],
            out_specs=pl.BlockSpec((tm, d), lambda i, *_: (i, 0)),
            scratch_shapes=[pltpu.VMEM((2, rc, d), BF16), pltpu.SemaphoreType.DMA((2,))],
        ),
        out_shape=jax.ShapeDtypeStruct((m, d), F32),
        compiler_params=_cparams("arbitrary"),
        name="moe_combine",
    )(*tabs, x, route_i, route_w, norm_g, mods, ys)


def sparse_moe_layer(streams, norm_g, mods, layer, w_router, b_router, w1, w3, w2, *, tf):
    ne = w_router.shape[1]
    tm_exp = EXPERT_ROWS
    routed = [moe_router(x, norm_g, mods, layer, w_router, b_router, mod_row=row, tm=tm) for x, row, tm in streams]
    counts = jnp.concatenate([r[2][:, 0, :] for r in routed], axis=0)
    nt = counts.shape[0]
    m_total = sum(x.shape[0] for x, _, _ in streams)
    lpad = (counts + RUN_ALIGN - 1) // RUN_ALIGN * RUN_ALIGN
    loff = jnp.cumsum(lpad, axis=1) - lpad
    group = jnp.sum(lpad, axis=0)
    gpad = (group + tm_exp - 1) // tm_exp * tm_exp
    ends = jnp.cumsum(gpad)
    starts = ends - gpad
    base = starts[None, :] + jnp.cumsum(lpad, axis=0) - lpad
    n_tiles = -(-(2 * m_total + nt * ne * (RUN_ALIGN - 1)) // tm_exp) + ne
    tile_start = jnp.arange(n_tiles, dtype=jnp.int32) * tm_exp
    n_used = (ends[-1] // tm_exp).astype(jnp.int32).reshape(1)
    tile_expert = jnp.sum(tile_start[:, None] >= ends[None, :], axis=1).astype(jnp.int32)
    tile_expert = jnp.minimum(tile_expert, tile_expert[jnp.maximum(n_used[0] - 1, 0)])
    nt_first = streams[0][0].shape[0] // streams[0][2]
    group_first = jnp.sum(lpad[:nt_first], axis=0)
    later_rows = m_total - streams[0][0].shape[0] + (nt - nt_first) * (RUN_ALIGN - 1)
    assert later_rows + tm_exp - RUN_ALIGN < 2 << TAIL_BITS[0]
    tabs = [a.reshape(-1).astype(jnp.int32)
            for a in (lpad, loff, base, gpad - group_first, starts + group_first, n_used)]
    xs, tile0 = None, 0
    for (x, row, tm), (route_i, _, _) in zip(streams, routed):
        xs = moe_dispatch(x, norm_g, mods, layer, route_i, tabs, n_tiles * tm_exp, xs,
                          mod_row=row, tm=tm, ne=ne, tile0=tile0)
        tile0 += x.shape[0] // tm
    ys = moe_experts(xs, tile_expert, n_used, w1, w3, w2, tm=tm_exp, tf=tf)
    outs, tile0 = [], 0
    for (x, row, tm), (route_i, route_w, _) in zip(streams, routed):
        outs.append(moe_combine(x, norm_g, mods, layer, route_i, route_w, ys, tabs[:3],
                                mod_row=row, tm=tm, ne=ne, tile0=tile0))
        tile0 += x.shape[0] // tm
    return outs


def _rope_tables(seq):
    n = 16
    inv = ROPE_BASE ** (-jnp.arange(n, dtype=F32) / n)
    t = jnp.arange(seq)
    row_ang = (t // GRID_W).astype(F32)[:, None] * inv[None, :]
    col_ang = (t % GRID_W).astype(F32)[:, None] * inv[None, :]
    cos = jnp.concatenate([jnp.cos(row_ang)] * 2 + [jnp.cos(col_ang)] * 2, axis=1)
    sin = jnp.concatenate([-jnp.sin(row_ang), jnp.sin(row_ang), -jnp.sin(col_ang), jnp.sin(col_ang)], axis=1)
    return jnp.tile(cos, (1, 2)), jnp.tile(sin, (1, 2))


def kernel(x, c, ctx, c_ctx, w_ada, b_ada, norm_g, attn_w_qkv, attn_w_o, attn_sink, ret_w_in, ret_w_o,
           ret_log_decay, pool_w, pool_scale, ffn_w1, ffn_w3, ffn_w2, moe_w_router, moe_b_router,
           moe_w1, moe_w3, moe_w2):
    batch, seq, d = x.shape
    assert batch == 1 and c.shape[0] == 1
    depth = w_ada.shape[0]
    lc = ctx.shape[1]
    xl = x.reshape(seq, d)
    xc = ctx.reshape(lc, d)

    cvecs = jnp.zeros((8, d), F32).at[LAT_ROW].set(c[0]).at[CTX_ROW].set(c_ctx)
    mods = ada_table(cvecs, w_ada, b_ada)
    rope = _rope_tables(seq)

    hd = attn_w_o.shape[1]
    dh = hd // ATTN_HEADS
    qkv_scale = jnp.concatenate([jnp.full((hd,), dh ** -0.5, F32),
                                 jnp.ones((attn_w_qkv.shape[2] - hd,), F32)])
    dk = d // RET_HEADS
    ret_scale = jnp.concatenate([jnp.ones((d,), F32), jnp.full((d,), dk ** -0.5, F32),
                                 jnp.ones((ret_w_in.shape[2] - 2 * d,), F32)])

    moe_bf16 = None
    for i in range(depth):
        last = i == depth - 1
        kind, j = i % N_MIXERS, i // N_MIXERS
        proj = functools.partial(mixer_projection, norm_g=norm_g, mods=mods, layer=i)
        if kind == 0:
            w_qkv = (attn_w_qkv[j] * qkv_scale).astype(BF16)
            w_o = attn_w_o[j].astype(BF16)
            kvd = (w_qkv.shape[1] - hd) // 2
            qkv_l = proj(xl, w=w_qkv, mod_row=LAT_ROW, tm=512, chunk=256, q_cols=hd, rope=rope, rope_cols=hd + kvd)
            qkv_c = proj(xc, w=w_qkv, mod_row=CTX_ROW, tm=256, chunk=256, q_cols=hd)
            o_l = attention(attn_sink[j], qkv_l, qkv_c, has_local=True)
            xl = outproj_residual(xl, norm_g, mods, i, w_o, [o_l], g_row=1, mod_row=LAT_ROW, slot=GA1, tm=512)
            if not last:
                o_c = attention(attn_sink[j], qkv_c, qkv_c, has_local=False)
                xc = outproj_residual(xc, norm_g, mods, i, w_o, [o_c], g_row=1, mod_row=CTX_ROW, slot=GA1, tm=256)
        elif kind == 1:
            w_in = (ret_w_in[j] * ret_scale).astype(BF16)
            w_o = ret_w_o[j].astype(BF16)
            p_c = proj(xc, w=w_in, mod_row=CTX_ROW, tm=256, chunk=1024)
            p_l = proj(xl, w=w_in, mod_row=LAT_ROW, tm=512, chunk=1024)
            s0 = jnp.zeros((RET_HEADS, dk, 2 * dk), F32)
            tabs_f = retention_tables(ret_log_decay[j, 0], min(RET_CHUNK, lc), False)
            tabs_b = retention_tables(ret_log_decay[j, 1], min(RET_CHUNK, lc), True)
            zf_c, s_f = retention_scan(p_c, tabs_f, s0, reverse=False)
            zb_c, s_b = retention_scan(p_c, tabs_b, s0, reverse=True)
            zf_l, _ = retention_scan(p_l, tabs_f, s_f, reverse=False)
            zb_l, _ = retention_scan(p_l, tabs_b, s_b, reverse=True)
            xl = outproj_residual(xl, norm_g, mods, i, w_o, [zf_l, zb_l], g_row=1, mod_row=LAT_ROW, slot=GA1, tm=512)
            if not last:
                xc = outproj_residual(xc, norm_g, mods, i, w_o, [zf_c, zb_c], g_row=1, mod_row=CTX_ROW, slot=GA1, tm=256)
        else:
            w_p = pool_w[j].astype(BF16)
            xl = pool_layer(xl, norm_g, mods, i, w_p, pool_scale[j], mod_row=LAT_ROW, tm=512)
            if not last:
                xc = pool_layer(xc, norm_g, mods, i, w_p, pool_scale[j], mod_row=CTX_ROW, tm=256)

        f = i // 2
        if i % 2 == 0:
            w1, w3, w2 = ffn_w1[f].astype(BF16), ffn_w3[f].astype(BF16), ffn_w2[f].astype(BF16)
            tf = w1.shape[1] // 2
            steps = (seq // min(512, seq)) * 2
            ne = moe_w1.shape[1]
            if not last and steps % ne == 0 and d % (steps // ne * 16) == 0:
                xl, moe_bf16 = ffn_layer(xl, norm_g, mods, i, w1, w3, w2, mod_row=LAT_ROW, tm=512, tf=tf,
                                         cast=((moe_w1, moe_w3, moe_w2), (i + 1) // 2))
            else:
                xl = ffn_layer(xl, norm_g, mods, i, w1, w3, w2, mod_row=LAT_ROW, tm=512, tf=tf)
            if not last:
                xc = ffn_layer(xc, norm_g, mods, i, w1, w3, w2, mod_row=CTX_ROW, tm=256, tf=tf)
        else:
            if moe_bf16 is None:
                moe_bf16 = moe_w1[f].astype(BF16), moe_w3[f].astype(BF16), moe_w2[f].astype(BF16)
            (w1, w3, w2), moe_bf16 = moe_bf16, None
            tf = w1.shape[2] // 2
            streams = [(xl, LAT_ROW, min(512, seq))] + ([] if last else [(xc, CTX_ROW, min(256, lc))])
            outs = sparse_moe_layer(streams, norm_g, mods, i, moe_w_router[f], moe_b_router[f], w1, w3, w2, tf=tf)
            xl = outs[0]
            if not last:
                xc = outs[1]
    return xl.reshape(batch, seq, d)
```

```python
import functools

import jax
import jax.numpy as jnp
from jax import lax
from jax.experimental import pallas as pl
from jax.experimental.pallas import tpu as pltpu

F32 = jnp.float32
BF16 = jnp.bfloat16

EPS = 1e-6
NEG_INF = -1e30
LOG2E = 1.4426950408889634
LANES = 128
VMEM_LIMIT = 56 * 1024 * 1024

GRID_W = 64
N_MIXERS = 3
ATTN_HEADS = 16
ATTN_KV_HEADS = 4
ATTN_BLOCKS_PER_STEP = 4
WINDOW = 128
ROPE_BASE = 10000.0
RET_HEADS = 4
RET_CHUNK = 256
POOL_WINDOWS = (2, 4, 8, 16)
POOL_HALO = 8
N_EXPERTS = 8

SH1, SC1, GA1, SH2, SC2, GA2 = range(6)
LAT_ROW, CTX_ROW = 0, 1


def _cparams(*sem):
    return pltpu.CompilerParams(dimension_semantics=sem, vmem_limit_bytes=VMEM_LIMIT)


def _rms(x, g):
    return x * lax.rsqrt(jnp.mean(x * x, axis=-1, keepdims=True) + EPS) * g


def _mod_slice(mod_ref, row, slot, d):
    return mod_ref[row:row + 1, slot * d:(slot + 1) * d]


def _norm_mod(x, g, shift, scale):
    return _rms(x, g) * (1.0 + scale) + shift


def _ada_kernel(c_ref, w_ref, b_ref, o_ref):
    c = c_ref[...]
    s = c * jax.nn.sigmoid(c)
    o_ref[...] = jnp.dot(s, w_ref[...], preferred_element_type=F32,
                         precision=lax.Precision.HIGHEST) + b_ref[...]


def ada_table(cvecs, w_ada, b_ada):
    depth, d, n = w_ada.shape
    tn = 1536
    return pl.pallas_call(
        _ada_kernel,
        grid=(depth, n // tn),
        in_specs=[
            pl.BlockSpec((8, d), lambda i, j: (0, 0)),
            pl.BlockSpec((None, d, tn), lambda i, j: (i, 0, j)),
            pl.BlockSpec((None, 1, tn), lambda i, j: (i, 0, j)),
        ],
        out_specs=pl.BlockSpec((None, 8, tn), lambda i, j: (i, 0, j)),
        out_shape=jax.ShapeDtypeStruct((depth, 8, n), F32),
        compiler_params=_cparams("parallel", "parallel"),
        name="ada_table",
    )(cvecs, w_ada, b_ada.reshape(depth, 1, n))


def _rope(a, cos, sin):
    lane = lax.broadcasted_iota(jnp.int32, (a.shape[0], LANES), 1)
    first = (lane % 32) < 16
    outs = []
    for cb in range(a.shape[1] // LANES):
        blk = a[:, cb * LANES:(cb + 1) * LANES]
        partner = jnp.where(first, pltpu.roll(blk, LANES - 16, 1), pltpu.roll(blk, 16, 1))
        outs.append(blk * cos + partner * sin)
    return jnp.concatenate(outs, axis=1)


def _mixer_proj_kernel(x_ref, g_ref, mod_ref, w_ref, *rest, mod_row, chunk, rope_cols, q_cols):
    if rope_cols:
        cos_ref, sin_ref, o_ref = rest
        cos, sin = cos_ref[...], sin_ref[...]
    else:
        (o_ref,) = rest
    d = x_ref.shape[1]
    h = _norm_mod(x_ref[...], g_ref[0:1, :], _mod_slice(mod_ref, mod_row, SH1, d),
                  _mod_slice(mod_ref, mod_row, SC1, d)).astype(BF16)
    for c0 in range(0, o_ref.shape[1], chunk):
        acc = jnp.dot(h, w_ref[:, c0:c0 + chunk], preferred_element_type=F32)
        if c0 < q_cols:
            acc = acc * LOG2E
        if c0 < rope_cols:
            acc = _rope(acc, cos, sin)
        o_ref[:, c0:c0 + chunk] = acc.astype(o_ref.dtype)


def mixer_projection(x, norm_g, mods, layer, w, *, mod_row, tm, chunk, q_cols=0, rope=None, rope_cols=0):
    m, d = x.shape
    n = w.shape[1]
    tm = min(tm, m)
    assert n % chunk == 0 and rope_cols % chunk == 0 and q_cols % chunk == 0
    in_specs = [
        pl.BlockSpec((tm, d), lambda i: (i, 0)),
        pl.BlockSpec((None,) + norm_g.shape[1:], lambda i: (layer, 0, 0)),
        pl.BlockSpec((None,) + mods.shape[1:], lambda i: (layer, 0, 0)),
        pl.BlockSpec((d, n), lambda i: (0, 0), pipeline_mode=pl.Buffered(1)),
    ]
    args = [x, norm_g, mods, w]
    if rope_cols:
        in_specs += [pl.BlockSpec((tm, LANES), lambda i: (i, 0))] * 2
        args += list(rope)
    return pl.pallas_call(
        functools.partial(_mixer_proj_kernel, mod_row=mod_row, chunk=chunk, rope_cols=rope_cols, q_cols=q_cols),
        grid=(m // tm,),
        in_specs=in_specs,
        out_specs=pl.BlockSpec((tm, n), lambda i: (i, 0)),
        out_shape=jax.ShapeDtypeStruct((m, n), BF16),
        compiler_params=_cparams("parallel"),
        name="mixer_projection",
    )(*args)


def _outproj_kernel(x_ref, g_ref, mod_ref, w_ref, *rest, n_y, g_row, mod_row, slot):
    y_refs, o_ref = rest[:n_y], rest[n_y]
    d = x_ref.shape[1]
    y = y_refs[0][...]
    if n_y == 2:
        y = (y.astype(F32) + y_refs[1][...].astype(F32)).astype(BF16)
    t = jnp.dot(y, w_ref[...], preferred_element_type=F32)
    gate = _mod_slice(mod_ref, mod_row, slot, d)
    o_ref[...] = x_ref[...] + gate * _rms(t, g_ref[g_row:g_row + 1, :])


def outproj_residual(x, norm_g, mods, layer, w, ys, *, g_row, mod_row, slot, tm):
    m, d = x.shape
    k = w.shape[0]
    tm = min(tm, m)
    in_specs = [
        pl.BlockSpec((tm, d), lambda i: (i, 0)),
        pl.BlockSpec((None,) + norm_g.shape[1:], lambda i: (layer, 0, 0)),
        pl.BlockSpec((None,) + mods.shape[1:], lambda i: (layer, 0, 0)),
        pl.BlockSpec((k, d), lambda i: (0, 0)),
    ] + [pl.BlockSpec((tm, k), lambda i: (i, 0))] * len(ys)
    return pl.pallas_call(
        functools.partial(_outproj_kernel, n_y=len(ys), g_row=g_row, mod_row=mod_row, slot=slot),
        grid=(m // tm,),
        in_specs=in_specs,
        out_specs=pl.BlockSpec((tm, d), lambda i: (i, 0)),
        out_shape=jax.ShapeDtypeStruct((m, d), F32),
        compiler_params=_cparams("parallel"),
        name="outproj_residual",
    )(x, norm_g, mods, w, *ys)


def _dot_nt(a, b):
    return lax.dot_general(a, b, (((1,), (1,)), ((), ())), preferred_element_type=F32)


def _attn_kernel(sink_ref, q_ref, *rest, tq, nsub, kv_heads, group, dh, has_local):
    if has_local:
        k_refs, v_refs = rest[:nsub + 2], rest[nsub + 2:2 * nsub + 4]
    kc_ref, vc_ref, o_ref = rest[-3:]
    assert 2 * dh == LANES and tq == WINDOW and group % 2 == 0
    b = pl.program_id(0)
    nb = pl.num_programs(0)
    rows = group * tq
    lc = kc_ref.shape[0]
    head_of_row = lax.broadcasted_iota(jnp.int32, (rows, 1), 0) // tq
    lane = lax.broadcasted_iota(jnp.int32, (1, LANES), 1)
    if has_local:
        r = lax.broadcasted_iota(jnp.int32, (rows, tq), 0) % tq
        c = lax.broadcasted_iota(jnp.int32, (rows, tq), 1)
    for sub, j in [(sub, j) for sub in range(nsub) for j in range(kv_heads)]:
        qrows = slice(sub * tq, (sub + 1) * tq)
        hs = slice(j * dh, (j + 1) * dh)
        pair = slice((j // 2) * LANES, (j // 2 + 1) * LANES)
        v_low = j % 2 == 0
        keep_v = (lane < dh) if v_low else (lane >= dh)
        qg = jnp.concatenate(
            [q_ref[qrows, (j * group + g) * dh:(j * group + g + 1) * dh] for g in range(group)], axis=0)
        sink = jnp.full((rows, 1), sink_ref[j * group] * LOG2E, F32)
        for g in range(1, group):
            sink = jnp.where(head_of_row == g, sink_ref[j * group + g] * LOG2E, sink)
        if has_local:
            kp_ref, kq_ref, kn_ref = k_refs[sub:sub + 3]
            vp_ref, vq_ref, vn_ref = v_refs[sub:sub + 3]
            prev_ok = (c >= r) if sub > 0 else (c >= r) & (b > 0)
            next_ok = (c <= r) if sub < nsub - 1 else (c <= r) & (b < nb - 1)
            k_all = jnp.concatenate([kc_ref[:, hs], kp_ref[:, hs], kq_ref[:, hs], kn_ref[:, hs]], axis=0)
            v_all = jnp.concatenate([vc_ref[:, pair], vp_ref[:, pair], vq_ref[:, pair], vn_ref[:, pair]], axis=0)
            s = _dot_nt(qg, k_all)
            s = jnp.concatenate([s[:, :lc],
                                 jnp.where(prev_ok, s[:, lc:lc + tq], NEG_INF),
                                 s[:, lc + tq:lc + 2 * tq],
                                 jnp.where(next_ok, s[:, lc + 2 * tq:], NEG_INF)], axis=1)
        else:
            v_all = vc_ref[:, pair]
            s = _dot_nt(qg, kc_ref[:, hs])
        m = jnp.maximum(jnp.max(s, axis=-1, keepdims=True), sink)
        p = jnp.exp2(s - m).astype(BF16)
        v_ones = jnp.where(keep_v, v_all, jnp.ones_like(v_all))
        pv = jnp.dot(p, v_ones, preferred_element_type=F32)
        den = pltpu.roll(pv, dh, 1) + jnp.exp2(sink - m)
        o = pv / den
        for g in range(0, group, 2):
            a, bb = o[g * tq:(g + 1) * tq], o[(g + 1) * tq:(g + 2) * tq]
            if v_low:
                both = jnp.where(lane < dh, a, pltpu.roll(bb, dh, 1))
            else:
                both = jnp.where(lane < dh, pltpu.roll(a, dh, 1), bb)
            h = j * group + g
            o_ref[qrows, h * dh:(h + 2) * dh] = both.astype(o_ref.dtype)


def attention(sink, qkv, qkv_ctx, *, has_local):
    s = qkv.shape[0]
    dh = qkv.shape[1] // (ATTN_HEADS + 2 * ATTN_KV_HEADS)
    hd = ATTN_HEADS * dh
    kvd = ATTN_KV_HEADS * dh
    kcol, vcol = hd // kvd, hd // kvd + 1
    tq = WINDOW
    nb = s // tq
    nsub = min(ATTN_BLOCKS_PER_STEP, nb)
    lc = qkv_ctx.shape[0]
    in_specs = [
        pl.BlockSpec(memory_space=pltpu.SMEM),
        pl.BlockSpec((nsub * tq, hd), lambda b: (b, 0)),
    ]
    args = [sink, qkv]
    if has_local:
        for col in (kcol, vcol):
            for off in range(-1, nsub + 1):
                in_specs.append(pl.BlockSpec(
                    (tq, kvd), lambda b, col=col, off=off: (jnp.clip(nsub * b + off, 0, nb - 1), col)))
                args.append(qkv)
    in_specs += [pl.BlockSpec((lc, kvd), lambda b: (0, kcol)), pl.BlockSpec((lc, kvd), lambda b: (0, vcol))]
    args += [qkv_ctx, qkv_ctx]
    return pl.pallas_call(
        functools.partial(_attn_kernel, tq=tq, nsub=nsub, kv_heads=ATTN_KV_HEADS,
                          group=ATTN_HEADS // ATTN_KV_HEADS, dh=dh, has_local=has_local),
        grid=(nb // nsub,),
        in_specs=in_specs,
        out_specs=pl.BlockSpec((nsub * tq, hd), lambda b: (b, 0)),
        out_shape=jax.ShapeDtypeStruct((s, hd), BF16),
        compiler_params=_cparams("parallel"),
        name="attention",
    )(*args)


def _ret_kernel(q_ref, k_ref, v_ref, gate_ref, intra_ref, qdec_ref, kdec_ref, cdec_ref, s0_ref,
                z_ref, sfin_ref, state_ref, *, heads, dk, dv):
    i = pl.program_id(0)

    @pl.when(i == 0)
    def _():
        state_ref[...] = s0_ref[...]

    for h in range(heads):
        q = q_ref[:, h * dk:(h + 1) * dk]
        k = k_ref[:, h * dk:(h + 1) * dk]
        v = v_ref[:, h * dv:(h + 1) * dv]
        state = state_ref[h]
        sc = _dot_nt(q, k) * intra_ref[h]
        qd = (q.astype(F32) * qdec_ref[h]).astype(BF16)
        o = (jnp.dot(sc.astype(BF16), v, preferred_element_type=F32)
             + jnp.dot(qd, state.astype(BF16), preferred_element_type=F32))
        kd = (k.astype(F32) * kdec_ref[h]).astype(BF16)
        state_ref[h] = state * cdec_ref[h] + lax.dot_general(
            kd, v, (((0,), (0,)), ((), ())), preferred_element_type=F32)
        mu = jnp.mean(o, axis=-1, keepdims=True)
        oc = o - mu
        var = jnp.mean(oc * oc, axis=-1, keepdims=True)
        gate = gate_ref[:, h * dv:(h + 1) * dv].astype(F32)
        z = gate * jax.nn.sigmoid(gate) * (oc * lax.rsqrt(var + EPS))
        z_ref[:, h * dv:(h + 1) * dv] = z.astype(z_ref.dtype)

    @pl.when(i == pl.num_programs(0) - 1)
    def _():
        sfin_ref[...] = state_ref[...]


def retention_scan(proj, tables, s0, *, reverse):
    m, n = proj.shape
    d = n // 8
    heads = RET_HEADS
    dk, dv = d // heads, 2 * d // heads
    c = min(RET_CHUNK, m)
    nc = m // c
    intra, qdec, kdec, cdec = tables

    def row(i):
        return nc - 1 - i if reverse else i

    gate_blk = 3 if reverse else 2
    full = lambda a: pl.BlockSpec(a.shape, lambda i: (0,) * a.ndim)
    z, sfin = pl.pallas_call(
        functools.partial(_ret_kernel, heads=heads, dk=dk, dv=dv),
        grid=(nc,),
        in_specs=[
            pl.BlockSpec((c, d), lambda i: (row(i), 0)),
            pl.BlockSpec((c, d), lambda i: (row(i), 1)),
            pl.BlockSpec((c, 2 * d), lambda i: (row(i), 1)),
            pl.BlockSpec((c, 2 * d), lambda i: (row(i), gate_blk)),
            full(intra), full(qdec), full(kdec), full(cdec), full(s0),
        ],
        out_specs=[
            pl.BlockSpec((c, 2 * d), lambda i: (row(i), 0)),
            full(s0),
        ],
        out_shape=[jax.ShapeDtypeStruct((m, 2 * d), BF16), jax.ShapeDtypeStruct(s0.shape, F32)],
        scratch_shapes=[pltpu.VMEM(s0.shape, F32)],
        compiler_params=_cparams("arbitrary"),
        name="retention_scan",
    )(proj, proj, proj, proj, intra, qdec, kdec, cdec, s0)
    return z, sfin


def retention_tables(log_decay_row, c, reverse):
    lg = -jnp.exp(log_decay_row.astype(F32))
    idx = jnp.arange(c, dtype=F32)
    diff = idx[:, None] - idx[None, :]
    if reverse:
        diff = -diff
    intra = jnp.where(diff >= 0, jnp.exp(lg[:, None, None] * jnp.maximum(diff, 0.0)), 0.0)
    fwd_idx = (c - 1.0 - idx) if reverse else idx
    qdec = jnp.exp(lg[:, None] * (fwd_idx + 1.0))[:, :, None]
    kdec = jnp.exp(lg[:, None] * (c - 1.0 - fwd_idx))[:, :, None]
    cdec = jnp.exp(lg * c)[:, None, None]
    return intra, qdec, kdec, cdec


def _pool_kernel(x_ref, xp_ref, xn_ref, g_ref, mod_ref, w_ref, ps_ref, o_ref, h_ref, *, mod_row, seq):
    tm, d = x_ref.shape
    halo = POOL_HALO
    i = pl.program_id(0)
    g0 = g_ref[0:1, :]
    shift = _mod_slice(mod_ref, mod_row, SH1, d)
    scale = _mod_slice(mod_ref, mod_row, SC1, d)
    hp = _norm_mod(xp_ref[...], g0, shift, scale)
    hn = _norm_mod(xn_ref[...], g0, shift, scale)
    h_ref[0:halo, :] = jnp.where(i > 0, hp, 0.0)
    h_ref[halo:halo + tm, :] = _norm_mod(x_ref[...], g0, shift, scale)
    h_ref[halo + tm:, :] = jnp.where(i < pl.num_programs(0) - 1, hn, 0.0)

    t = i * tm + lax.broadcasted_iota(jnp.int32, (tm, 1), 0)
    gw = d // len(POOL_WINDOWS)
    ys = []
    for g, w in enumerate(POOL_WINDOWS):
        cols = slice(g * gw, (g + 1) * gw)
        tot = h_ref[halo - w // 2:halo - w // 2 + tm, cols]
        for off in range(-w // 2 + 1, w // 2):
            tot = tot + h_ref[halo + off:halo + off + tm, cols]
        cnt = (jnp.minimum(t + w // 2, seq) - jnp.maximum(t - w // 2, 0)).astype(F32)
        dm = tot / cnt - h_ref[halo:halo + tm, cols]
        ys.append(jnp.dot(dm.astype(BF16), w_ref[g], preferred_element_type=F32))
    y = jnp.concatenate(ys, axis=1) * ps_ref[...]
    o_ref[...] = x_ref[...] + _mod_slice(mod_ref, mod_row, GA1, d) * _rms(y, g_ref[1:2, :])


def pool_layer(x, norm_g, mods, layer, w_pool, pool_scale, *, mod_row, tm):
    m, d = x.shape
    tm = min(tm, m)
    nt = m // tm
    hb = tm // POOL_HALO
    return pl.pallas_call(
        functools.partial(_pool_kernel, mod_row=mod_row, seq=m),
        grid=(nt,),
        in_specs=[
            pl.BlockSpec((tm, d), lambda i: (i, 0)),
            pl.BlockSpec((POOL_HALO, d), lambda i: (jnp.maximum(i * hb - 1, 0), 0)),
            pl.BlockSpec((POOL_HALO, d), lambda i: (jnp.minimum((i + 1) * hb, nt * hb - 1), 0)),
            pl.BlockSpec((None,) + norm_g.shape[1:], lambda i: (layer, 0, 0)),
            pl.BlockSpec((None,) + mods.shape[1:], lambda i: (layer, 0, 0)),
            pl.BlockSpec(w_pool.shape, lambda i: (0, 0, 0)),
            pl.BlockSpec((1, d), lambda i: (0, 0)),
        ],
        out_specs=pl.BlockSpec((tm, d), lambda i: (i, 0)),
        out_shape=jax.ShapeDtypeStruct((m, d), F32),
        scratch_shapes=[pltpu.VMEM((tm + 2 * POOL_HALO, d), F32)],
        compiler_params=_cparams("parallel"),
        name="pool_layer",
    )(x, x, x, norm_g, mods, w_pool, pool_scale.reshape(1, d))


def _swiglu_partial(h, w1, w3, w2):
    a = jnp.dot(h, w1, preferred_element_type=F32)
    b = jnp.dot(h, w3, preferred_element_type=F32)
    u = (a * jax.nn.sigmoid(a) * b).astype(BF16)
    return jnp.dot(u, w2, preferred_element_type=F32)


def _ffn_kernel(x_ref, g_ref, mod_ref, w1_ref, w3_ref, w2_ref, *rest, mod_row, n_cast, nf):
    cast_src, o_ref, cast_dst = rest[:n_cast], rest[n_cast], rest[n_cast + 1:2 * n_cast + 1]
    h_ref, acc_ref = rest[2 * n_cast + 1:]
    d = x_ref.shape[1]
    f = pl.program_id(1)
    for src, dst in zip(cast_src, cast_dst):
        dst[...] = src[...].astype(dst.dtype)

    tf = w1_ref.shape[1] // nf
    assert nf >= 2
    for c in range(nf):
        @pl.when(f == c)
        def _(c=c):
            if c == 0:
                h = _norm_mod(x_ref[...], g_ref[2:3, :], _mod_slice(mod_ref, mod_row, SH2, d),
                              _mod_slice(mod_ref, mod_row, SC2, d)).astype(BF16)
                h_ref[...] = h
            else:
                h = h_ref[...]
            cols = slice(c * tf, (c + 1) * tf)
            part = _swiglu_partial(h, w1_ref[:, cols], w3_ref[:, cols], w2_ref[cols, :])
            if c == 0:
                acc_ref[...] = part
            elif c < nf - 1:
                acc_ref[...] += part
            else:
                y = acc_ref[...] + part
                o_ref[...] = x_ref[...] + _mod_slice(mod_ref, mod_row, GA2, d) * _rms(y, g_ref[3:4, :])


def ffn_layer(x, norm_g, mods, layer, w1, w3, w2, *, mod_row, tm, tf, cast=None):
    m, d = x.shape
    ff = w1.shape[1]
    tm = min(tm, m)
    nf = ff // tf
    steps = (m // tm) * nf
    in_specs = [
        pl.BlockSpec((tm, d), lambda i, f: (i, 0)),
        pl.BlockSpec((None,) + norm_g.shape[1:], lambda i, f: (layer, 0, 0)),
        pl.BlockSpec((None,) + mods.shape[1:], lambda i, f: (layer, 0, 0)),
        pl.BlockSpec((d, ff), lambda i, f: (0, 0), pipeline_mode=pl.Buffered(1)),
        pl.BlockSpec((d, ff), lambda i, f: (0, 0), pipeline_mode=pl.Buffered(1)),
        pl.BlockSpec((ff, d), lambda i, f: (0, 0), pipeline_mode=pl.Buffered(1)),
    ]
    out_specs = [pl.BlockSpec((tm, d), lambda i, f: (i, 0))]
    out_shape = [jax.ShapeDtypeStruct((m, d), F32)]
    cast_arrays, cast_idx = cast if cast is not None else ((), 0)
    for a in cast_arrays:
        _, ne, r, c = a.shape
        per_expert = steps // ne
        assert per_expert * ne == steps and r % (per_expert * 16) == 0
        rows = r // per_expert
        in_specs.append(pl.BlockSpec(
            (None, None, rows, c),
            lambda i, f, pe=per_expert: (cast_idx, (i * nf + f) // pe, (i * nf + f) % pe, 0)))
        out_specs.append(pl.BlockSpec(
            (None, rows, c), lambda i, f, pe=per_expert: ((i * nf + f) // pe, (i * nf + f) % pe, 0)))
        out_shape.append(jax.ShapeDtypeStruct((ne, r, c), BF16))
    outs = pl.pallas_call(
        functools.partial(_ffn_kernel, mod_row=mod_row, n_cast=len(cast_arrays), nf=nf),
        grid=(m // tm, nf),
        in_specs=in_specs,
        out_specs=out_specs,
        out_shape=out_shape,
        scratch_shapes=[pltpu.VMEM((tm, d), BF16), pltpu.VMEM((tm, d), F32)],
        compiler_params=_cparams("arbitrary", "arbitrary"),
        name="ffn_layer",
    )(x, norm_g, mods, w1, w3, w2, *cast_arrays)
    return outs[0] if cast is None else (outs[0], tuple(outs[1:]))


RUN_ALIGN = 16
RUN_BITS = tuple(range(9, 3, -1))
EXPERT_ROWS = 512
TAIL_BITS = tuple(range(9, 3, -1))


def _split_dma(length, src, src_row, dst, dst_row, sem, bits, *, start, src_advances=True):
    for b in bits:
        size = 1 << b
        off = (length >> (b + 1)) << (b + 1)

        @pl.when(((length >> b) & 1) == 1)
        def _():
            s_row = pl.multiple_of(src_row + off, RUN_ALIGN) if src_advances else src_row
            cp = pltpu.make_async_copy(src.at[pl.ds(s_row, size)],
                                       dst.at[pl.ds(pl.multiple_of(dst_row + off, RUN_ALIGN), size)], sem)
            if start:
                cp.start()
            else:
                cp.wait()


def _run_dmas(tile, ne, lpad_ref, loff_ref, base_ref, buf, hbm_ref, sem, *, to_hbm, start):
    for e in range(ne):
        length = lpad_ref[tile * ne + e]
        lo = loff_ref[tile * ne + e]
        gb = base_ref[tile * ne + e]
        if to_hbm:
            _split_dma(length, buf, lo, hbm_ref, gb, sem, RUN_BITS, start=start)
        else:
            _split_dma(length, hbm_ref, gb, buf, lo, sem, RUN_BITS, start=start)


def _split_bf16(a):
    hi = a.astype(BF16)
    return hi, (a - hi.astype(F32)).astype(BF16)


def _dot_f32x3(a, b_hi, b_lo):
    a_hi, a_lo = _split_bf16(a)
    dot = functools.partial(jnp.dot, preferred_element_type=F32)
    return dot(a_hi, b_hi) + (dot(a_hi, b_lo) + dot(a_lo, b_hi))


def _local_rows(ri_ref, loff_ref, tile, ne):
    e1, e2 = ri_ref[:, 0:1], ri_ref[:, 1:2]
    lo1, lo2 = ri_ref[:, 2:3], ri_ref[:, 3:4]
    for e in range(ne):
        off = loff_ref[tile * ne + e]
        lo1 = lo1 + jnp.where(e1 == e, off, 0)
        lo2 = lo2 + jnp.where(e2 == e, off, 0)
    return lo1, lo2


def _router_kernel(x_ref, g_ref, mod_ref, wr_hi_ref, wr_lo_ref, br_ref, ri_ref, rw_ref, cnt_ref, *, mod_row):
    tm, d = x_ref.shape
    ne = wr_hi_ref.shape[1]
    h = _norm_mod(x_ref[...], g_ref[2:3, :], _mod_slice(mod_ref, mod_row, SH2, d),
                  _mod_slice(mod_ref, mod_row, SC2, d))
    logits = _dot_f32x3(h, wr_hi_ref[...], wr_lo_ref[...]) + br_ref[...]
    lane = lax.broadcasted_iota(jnp.int32, logits.shape, 1)
    v1 = jnp.max(logits, axis=-1, keepdims=True)
    i1 = jnp.min(jnp.where(logits == v1, lane, ne), axis=-1, keepdims=True)
    rest = jnp.where(lane == i1, -jnp.inf, logits)
    v2 = jnp.max(rest, axis=-1, keepdims=True)
    i2 = jnp.min(jnp.where(rest == v2, lane, ne), axis=-1, keepdims=True)
    e2 = jnp.exp(v2 - v1)
    den = 1.0 + e2
    oh1 = (lane == i1).astype(F32)
    oh2 = (lane == i2).astype(F32)
    both = oh1 + oh2
    r = lax.broadcasted_iota(jnp.int32, (tm, tm), 0)
    c = lax.broadcasted_iota(jnp.int32, (tm, tm), 1)
    tri = jnp.where(c < r, 1.0, 0.0).astype(BF16)
    rank_all = jnp.dot(tri, both.astype(BF16), preferred_element_type=F32)
    rank1 = jnp.sum(rank_all * oh1, axis=-1, keepdims=True).astype(jnp.int32)
    rank2 = jnp.sum(rank_all * oh2, axis=-1, keepdims=True).astype(jnp.int32)
    col = lax.broadcasted_iota(jnp.int32, ri_ref.shape, 1)
    ri_ref[...] = jnp.where(col == 0, i1, jnp.where(col == 1, i2, jnp.where(col == 2, rank1, rank2)))
    colw = lax.broadcasted_iota(jnp.int32, rw_ref.shape, 1)
    rw_ref[...] = jnp.where(colw == 0, 1.0 / den, e2 / den)
    cnt_ref[...] = jnp.sum(both, axis=0, keepdims=True).astype(jnp.int32)


def moe_router(x, norm_g, mods, layer, w_router, b_router, *, mod_row, tm):
    m, d = x.shape
    ne = w_router.shape[1]
    return pl.pallas_call(
        functools.partial(_router_kernel, mod_row=mod_row),
        grid=(m // tm,),
        in_specs=[
            pl.BlockSpec((tm, d), lambda i: (i, 0)),
            pl.BlockSpec((None,) + norm_g.shape[1:], lambda i: (layer, 0, 0)),
            pl.BlockSpec((None,) + mods.shape[1:], lambda i: (layer, 0, 0)),
            pl.BlockSpec((d, ne), lambda i: (0, 0)),
            pl.BlockSpec((d, ne), lambda i: (0, 0)),
            pl.BlockSpec((1, ne), lambda i: (0, 0)),
        ],
        out_specs=[
            pl.BlockSpec((tm, 4), lambda i: (i, 0)),
            pl.BlockSpec((tm, 2), lambda i: (i, 0)),
            pl.BlockSpec((None, 1, ne), lambda i: (i, 0, 0)),
        ],
        out_shape=[
            jax.ShapeDtypeStruct((m, 4), jnp.int32),
            jax.ShapeDtypeStruct((m, 2), F32),
            jax.ShapeDtypeStruct((m // tm, 1, ne), jnp.int32),
        ],
        compiler_params=_cparams("parallel"),
        name="moe_router",
    )(x, norm_g, mods, *_split_bf16(w_router), b_router.reshape(1, ne))


def _dispatch_kernel(lpad_ref, loff_ref, base_ref, tlen_ref, tstart_ref, nu_ref, x_ref, g_ref, mod_ref, ri_ref,
                     *rest, mod_row, ne, tile0, first):
    xs_ref, cbuf, zbuf, sem = rest if first else rest[1:]
    tm, d = x_ref.shape
    rc = cbuf.shape[1]
    i = pl.program_id(0)
    last = i == pl.num_programs(0) - 1
    slot = i % 2

    def zero_tails(start):
        for e in range(ne):
            _split_dma(tlen_ref[e], zbuf, 0, xs_ref, tstart_ref[e], sem.at[2], TAIL_BITS, start=start,
                       src_advances=False)

        def unused_tile(t, carry):
            cp = pltpu.make_async_copy(
                zbuf, xs_ref.at[pl.ds(pl.multiple_of(t * EXPERT_ROWS, EXPERT_ROWS), EXPERT_ROWS)], sem.at[2])
            if start:
                cp.start()
            else:
                cp.wait()
            return carry

        lax.fori_loop(nu_ref[0], xs_ref.shape[0] // EXPERT_ROWS, unused_tile, 0)

    if first:
        @pl.when(i == 0)
        def _():
            zbuf[...] = jnp.zeros_like(zbuf)
            zero_tails(True)

    h = _norm_mod(x_ref[...], g_ref[2:3, :], _mod_slice(mod_ref, mod_row, SH2, d),
                  _mod_slice(mod_ref, mod_row, SC2, d)).astype(BF16)
    lo1, lo2 = _local_rows(ri_ref, loff_ref, i + tile0, ne)
    lane = lax.broadcasted_iota(jnp.int32, (tm, rc), 1)
    onehot = jnp.where((lane == lo1) | (lane == lo2), 1.0, 0.0).astype(BF16)
    sorted_rows = lax.dot_general(onehot, h, (((0,), (0,)), ((), ())), preferred_element_type=F32)
    cbuf[slot] = sorted_rows.astype(BF16)
    moves = functools.partial(_run_dmas, ne=ne, lpad_ref=lpad_ref, loff_ref=loff_ref, base_ref=base_ref,
                              hbm_ref=xs_ref, to_hbm=True)
    moves(i + tile0, buf=cbuf.at[slot], sem=sem.at[slot], start=True)

    @pl.when(i > 0)
    def _():
        moves(i + tile0 - 1, buf=cbuf.at[1 - slot], sem=sem.at[1 - slot], start=False)

    @pl.when(last)
    def _():
        moves(i + tile0, buf=cbuf.at[slot], sem=sem.at[slot], start=False)
        if first:
            zero_tails(False)


def moe_dispatch(x, norm_g, mods, layer, route_i, tabs, rows, xs=None, *, mod_row, tm, ne, tile0):
    m, d = x.shape
    rc = 2 * tm + ne * RUN_ALIGN
    first = xs is None
    in_specs = [
        pl.BlockSpec((tm, d), lambda i, *_: (i, 0)),
        pl.BlockSpec((None,) + norm_g.shape[1:], lambda i, *_: (layer, 0, 0)),
        pl.BlockSpec((None,) + mods.shape[1:], lambda i, *_: (layer, 0, 0)),
        pl.BlockSpec((tm, 4), lambda i, *_: (i, 0)),
    ]
    args = [*tabs, x, norm_g, mods, route_i]
    if not first:
        in_specs.append(pl.BlockSpec(memory_space=pl.ANY))
        args.append(xs)
    return pl.pallas_call(
        functools.partial(_dispatch_kernel, mod_row=mod_row, ne=ne, tile0=tile0, first=first),
        grid_spec=pltpu.PrefetchScalarGridSpec(
            num_scalar_prefetch=len(tabs),
            grid=(m // tm,),
            in_specs=in_specs,
            out_specs=pl.BlockSpec(memory_space=pl.ANY),
            scratch_shapes=[pltpu.VMEM((2, rc, d), BF16), pltpu.VMEM((EXPERT_ROWS, d), BF16),
                            pltpu.SemaphoreType.DMA((3,))],
        ),
        out_shape=jax.ShapeDtypeStruct((rows, d), BF16),
        input_output_aliases={} if first else {len(args) - 1: 0},
        compiler_params=_cparams("arbitrary"),
        name="moe_dispatch",
    )(*args)


def _experts_kernel(te_ref, nu_ref, x_ref, w1_ref, w3_ref, w2_ref, y_ref, acc_ref):
    del te_ref
    i = pl.program_id(0)
    f = pl.program_id(1)
    nf = pl.num_programs(1)
    used = i < nu_ref[0]

    @pl.when(used)
    def _():
        part = _swiglu_partial(x_ref[...], w1_ref[...], w3_ref[...], w2_ref[...])

        @pl.when(f == 0)
        def _():
            acc_ref[...] = part

        @pl.when((f > 0) & (f < nf - 1))
        def _():
            acc_ref[...] += part

        @pl.when(f == nf - 1)
        def _():
            y_ref[...] = (acc_ref[...] + part).astype(y_ref.dtype)

    @pl.when(jnp.logical_not(used) & (f == nf - 1))
    def _():
        y_ref[...] = jnp.zeros_like(y_ref)


def moe_experts(xs, tile_expert, n_used, w1, w3, w2, *, tm, tf):
    rows, d = xs.shape
    ff = w1.shape[2]
    nf = ff // tf
    assert nf >= 2

    def wmap_in(i, f, te, nu):
        live = i < nu[0]
        return (te[i], 0, jnp.where(live, f, nf - 1))

    def wmap_out(i, f, te, nu):
        live = i < nu[0]
        return (te[i], jnp.where(live, f, nf - 1), 0)

    def xmap(i, f, te, nu):
        return (jnp.maximum(jnp.minimum(i, nu[0] - 1), 0), 0)

    return pl.pallas_call(
        _experts_kernel,
        grid_spec=pltpu.PrefetchScalarGridSpec(
            num_scalar_prefetch=2,
            grid=(rows // tm, nf),
            in_specs=[
                pl.BlockSpec((tm, d), xmap),
                pl.BlockSpec((None, d, tf), wmap_in),
                pl.BlockSpec((None, d, tf), wmap_in),
                pl.BlockSpec((None, tf, d), wmap_out),
            ],
            out_specs=pl.BlockSpec((tm, d), lambda i, f, te, nu: (i, 0)),
            scratch_shapes=[pltpu.VMEM((tm, d), F32)],
        ),
        out_shape=jax.ShapeDtypeStruct((rows, d), BF16),
        compiler_params=_cparams("arbitrary", "arbitrary"),
        name="moe_experts",
    )(tile_expert, n_used, xs, w1, w3, w2)


def _combine_kernel(lpad_ref, loff_ref, base_ref, x_ref, ri_ref, rw_ref, g_ref, mod_ref, ys_ref, o_ref,
                    ybuf, sem, *, mod_row, ne, tile0):
    tm, d = x_ref.shape
    rc = ybuf.shape[1]
    i = pl.program_id(0)
    slot = i % 2
    moves = functools.partial(_run_dmas, ne=ne, lpad_ref=lpad_ref, loff_ref=loff_ref, base_ref=base_ref,
                              hbm_ref=ys_ref, to_hbm=False)

    @pl.when(i == 0)
    def _():
        ybuf[...] = jnp.zeros_like(ybuf)
        moves(tile0, buf=ybuf.at[0], sem=sem.at[0], start=True)

    @pl.when(i + 1 < pl.num_programs(0))
    def _():
        moves(i + tile0 + 1, buf=ybuf.at[1 - slot], sem=sem.at[1 - slot], start=True)

    moves(i + tile0, buf=ybuf.at[slot], sem=sem.at[slot], start=False)
    lo1, lo2 = _local_rows(ri_ref, loff_ref, i + tile0, ne)
    lane = lax.broadcasted_iota(jnp.int32, (tm, rc), 1)
    rows = ybuf[slot]
    y1 = jnp.dot(jnp.where(lane == lo1, 1.0, 0.0).astype(BF16), rows, preferred_element_type=F32)
    y2 = jnp.dot(jnp.where(lane == lo2, 1.0, 0.0).astype(BF16), rows, preferred_element_type=F32)
    y = rw_ref[:, 0:1] * y1 + rw_ref[:, 1:2] * y2
    o_ref[...] = x_ref[...] + _mod_slice(mod_ref, mod_row, GA2, d) * _rms(y, g_ref[3:4, :])


def moe_combine(x, norm_g, mods, layer, route_i, route_w, ys, tabs, *, mod_row, tm, ne, tile0):
    m, d = x.shape
    rc = 2 * tm + ne * RUN_ALIGN
    return pl.pallas_call(
        functools.partial(_combine_kernel, mod_row=mod_row, ne=ne, tile0=tile0),
        grid_spec=pltpu.PrefetchScalarGridSpec(
            num_scalar_prefetch=3,
            grid=(m // tm,),
            in_specs=[
                pl.BlockSpec((tm, d), lambda i, *_: (i, 0)),
                pl.BlockSpec((tm, 4), lambda i, *_: (i, 0)),
                pl.BlockSpec((tm, 2), lambda i, *_: (i, 0)),
                pl.BlockSpec((None,) + norm_g.shape[1:], lambda i, *_: (layer, 0, 0)),
                pl.BlockSpec((None,) + mods.shape[1:], lambda i, *_: (layer, 0, 0)),
                pl.BlockSpec(memory_space=pl.ANY),
            ],
            out_specs=pl.BlockSpec((tm, d), lambda i, *_: (i, 0)),
            scratch_shapes=[pltpu.VMEM((2, rc, d), BF16), pltpu.SemaphoreType.DMA((2,))],
        ),
        out_shape=jax.ShapeDtypeStruct((m, d), F32),
        compiler_params=_cparams("arbitrary"),
        name="moe_combine",
    )(*tabs, x, route_i, route_w, norm_g, mods, ys)


def sparse_moe_layer(streams, norm_g, mods, layer, w_router, b_router, w1, w3, w2, *, tf):
    ne = w_router.shape[1]
    tm_exp = EXPERT_ROWS
    routed = [moe_router(x, norm_g, mods, layer, w_router, b_router, mod_row=row, tm=tm) for x, row, tm in streams]
    counts = jnp.concatenate([r[2][:, 0, :] for r in routed], axis=0)
    nt = counts.shape[0]
    m_total = sum(x.shape[0] for x, _, _ in streams)
    lpad = (counts + RUN_ALIGN - 1) // RUN_ALIGN * RUN_ALIGN
    loff = jnp.cumsum(lpad, axis=1) - lpad
    group = jnp.sum(lpad, axis=0)
    gpad = (group + tm_exp - 1) // tm_exp * tm_exp
    ends = jnp.cumsum(gpad)
    starts = ends - gpad
    base = starts[None, :] + jnp.cumsum(lpad, axis=0) - lpad
    n_tiles = -(-(2 * m_total + nt * ne * (RUN_ALIGN - 1)) // tm_exp) + ne
    tile_start = jnp.arange(n_tiles, dtype=jnp.int32) * tm_exp
    n_used = (ends[-1] // tm_exp).astype(jnp.int32).reshape(1)
    tile_expert = jnp.sum(tile_start[:, None] >= ends[None, :], axis=1).astype(jnp.int32)
    tile_expert = jnp.minimum(tile_expert, tile_expert[jnp.maximum(n_used[0] - 1, 0)])
    nt_first = streams[0][0].shape[0] // streams[0][2]
    group_first = jnp.sum(lpad[:nt_first], axis=0)
    later_rows = m_total - streams[0][0].shape[0] + (nt - nt_first) * (RUN_ALIGN - 1)
    assert later_rows + tm_exp - RUN_ALIGN < 2 << TAIL_BITS[0]
    tabs = [a.reshape(-1).astype(jnp.int32)
            for a in (lpad, loff, base, gpad - group_first, starts + group_first, n_used)]
    xs, tile0 = None, 0
    for (x, row, tm), (route_i, _, _) in zip(streams, routed):
        xs = moe_dispatch(x, norm_g, mods, layer, route_i, tabs, n_tiles * tm_exp, xs,
                          mod_row=row, tm=tm, ne=ne, tile0=tile0)
        tile0 += x.shape[0] // tm
    ys = moe_experts(xs, tile_expert, n_used, w1, w3, w2, tm=tm_exp, tf=tf)
    outs, tile0 = [], 0
    for (x, row, tm), (route_i, route_w, _) in zip(streams, routed):
        outs.append(moe_combine(x, norm_g, mods, layer, route_i, route_w, ys, tabs[:3],
                                mod_row=row, tm=tm, ne=ne, tile0=tile0))
        tile0 += x.shape[0] // tm
    return outs


def _rope_tables(seq):
    n = 16
    inv = ROPE_BASE ** (-jnp.arange(n, dtype=F32) / n)
    t = jnp.arange(seq)
    row_ang = (t // GRID_W).astype(F32)[:, None] * inv[None, :]
    col_ang = (t % GRID_W).astype(F32)[:, None] * inv[None, :]
    cos = jnp.concatenate([jnp.cos(row_ang)] * 2 + [jnp.cos(col_ang)] * 2, axis=1)
    sin = jnp.concatenate([-jnp.sin(row_ang), jnp.sin(row_ang), -jnp.sin(col_ang), jnp.sin(col_ang)], axis=1)
    return jnp.tile(cos, (1, 2)), jnp.tile(sin, (1, 2))


def kernel(x, c, ctx, c_ctx, w_ada, b_ada, norm_g, attn_w_qkv, attn_w_o, attn_sink, ret_w_in, ret_w_o,
           ret_log_decay, pool_w, pool_scale, ffn_w1, ffn_w3, ffn_w2, moe_w_router, moe_b_router,
           moe_w1, moe_w3, moe_w2):
    batch, seq, d = x.shape
    assert batch == 1 and c.shape[0] == 1
    depth = w_ada.shape[0]
    lc = ctx.shape[1]
    xl = x.reshape(seq, d)
    xc = ctx.reshape(lc, d)

    cvecs = jnp.zeros((8, d), F32).at[LAT_ROW].set(c[0]).at[CTX_ROW].set(c_ctx)
    mods = ada_table(cvecs, w_ada, b_ada)
    rope = _rope_tables(seq)

    hd = attn_w_o.shape[1]
    dh = hd // ATTN_HEADS
    qkv_scale = jnp.concatenate([jnp.full((hd,), dh ** -0.5, F32),
                                 jnp.ones((attn_w_qkv.shape[2] - hd,), F32)])
    dk = d // RET_HEADS
    ret_scale = jnp.concatenate([jnp.ones((d,), F32), jnp.full((d,), dk ** -0.5, F32),
                                 jnp.ones((ret_w_in.shape[2] - 2 * d,), F32)])

    moe_bf16 = None
    for i in range(depth):
        last = i == depth - 1
        kind, j = i % N_MIXERS, i // N_MIXERS
        proj = functools.partial(mixer_projection, norm_g=norm_g, mods=mods, layer=i)
        if kind == 0:
            w_qkv = (attn_w_qkv[j] * qkv_scale).astype(BF16)
            w_o = attn_w_o[j].astype(BF16)
            kvd = (w_qkv.shape[1] - hd) // 2
            qkv_l = proj(xl, w=w_qkv, mod_row=LAT_ROW, tm=512, chunk=256, q_cols=hd, rope=rope, rope_cols=hd + kvd)
            qkv_c = proj(xc, w=w_qkv, mod_row=CTX_ROW, tm=256, chunk=256, q_cols=hd)
            o_l = attention(attn_sink[j], qkv_l, qkv_c, has_local=True)
            xl = outproj_residual(xl, norm_g, mods, i, w_o, [o_l], g_row=1, mod_row=LAT_ROW, slot=GA1, tm=1024)
            if not last:
                o_c = attention(attn_sink[j], qkv_c, qkv_c, has_local=False)
                xc = outproj_residual(xc, norm_g, mods, i, w_o, [o_c], g_row=1, mod_row=CTX_ROW, slot=GA1, tm=256)
        elif kind == 1:
            w_in = (ret_w_in[j] * ret_scale).astype(BF16)
            w_o = ret_w_o[j].astype(BF16)
            p_c = proj(xc, w=w_in, mod_row=CTX_ROW, tm=256, chunk=1024)
            p_l = proj(xl, w=w_in, mod_row=LAT_ROW, tm=512, chunk=1024)
            s0 = jnp.zeros((RET_HEADS, dk, 2 * dk), F32)
            tabs_f = retention_tables(ret_log_decay[j, 0], min(RET_CHUNK, lc), False)
            tabs_b = retention_tables(ret_log_decay[j, 1], min(RET_CHUNK, lc), True)
            zf_c, s_f = retention_scan(p_c, tabs_f, s0, reverse=False)
            zb_c, s_b = retention_scan(p_c, tabs_b, s0, reverse=True)
            zf_l, _ = retention_scan(p_l, tabs_f, s_f, reverse=False)
            zb_l, _ = retention_scan(p_l, tabs_b, s_b, reverse=True)
            xl = outproj_residual(xl, norm_g, mods, i, w_o, [zf_l, zb_l], g_row=1, mod_row=LAT_ROW, slot=GA1, tm=1024)
            if not last:
                xc = outproj_residual(xc, norm_g, mods, i, w_o, [zf_c, zb_c], g_row=1, mod_row=CTX_ROW, slot=GA1, tm=256)
        else:
            w_p = pool_w[j].astype(BF16)
            xl = pool_layer(xl, norm_g, mods, i, w_p, pool_scale[j], mod_row=LAT_ROW, tm=1024)
            if not last:
                xc = pool_layer(xc, norm_g, mods, i, w_p, pool_scale[j], mod_row=CTX_ROW, tm=256)

        f = i // 2
        if i % 2 == 0:
            w1, w3, w2 = ffn_w1[f].astype(BF16), ffn_w3[f].astype(BF16), ffn_w2[f].astype(BF16)
            tf = w1.shape[1] // 2
            steps = (seq // min(512, seq)) * 2
            ne = moe_w1.shape[1]
            if not last and steps % ne == 0 and d % (steps // ne * 16) == 0:
                xl, moe_bf16 = ffn_layer(xl, norm_g, mods, i, w1, w3, w2, mod_row=LAT_ROW, tm=512, tf=tf,
                                         cast=((moe_w1, moe_w3, moe_w2), (i + 1) // 2))
            else:
                xl = ffn_layer(xl, norm_g, mods, i, w1, w3, w2, mod_row=LAT_ROW, tm=512, tf=tf)
            if not last:
                xc = ffn_layer(xc, norm_g, mods, i, w1, w3, w2, mod_row=CTX_ROW, tm=256, tf=tf)
        else:
            if moe_bf16 is None:
                moe_bf16 = moe_w1[f].astype(BF16), moe_w3[f].astype(BF16), moe_w2[f].astype(BF16)
            (w1, w3, w2), moe_bf16 = moe_bf16, None
            tf = w1.shape[2] // 2
            streams = [(xl, LAT_ROW, min(512, seq))] + ([] if last else [(xc, CTX_ROW, min(256, lc))])
            outs = sparse_moe_layer(streams, norm_g, mods, i, moe_w_router[f], moe_b_router[f], w1, w3, w2, tf=tf)
            xl = outs[0]
            if not last:
                xc = outs[1]
    return xl.reshape(batch, seq, d)
```

```python
import functools

import jax
import jax.numpy as jnp
from jax import lax
from jax.experimental import pallas as pl
from jax.experimental.pallas import tpu as pltpu

F32 = jnp.float32
BF16 = jnp.bfloat16

EPS = 1e-6
NEG_INF = -1e30
LOG2E = 1.4426950408889634
LANES = 128
VMEM_LIMIT = 56 * 1024 * 1024

GRID_W = 64
N_MIXERS = 3
ATTN_HEADS = 16
ATTN_KV_HEADS = 4
ATTN_BLOCKS_PER_STEP = 4
WINDOW = 128
ROPE_BASE = 10000.0
RET_HEADS = 4
RET_CHUNK = 256
POOL_WINDOWS = (2, 4, 8, 16)
POOL_HALO = 8
N_EXPERTS = 8

SH1, SC1, GA1, SH2, SC2, GA2 = range(6)
LAT_ROW, CTX_ROW = 0, 1


def _cparams(*sem):
    return pltpu.CompilerParams(dimension_semantics=sem, vmem_limit_bytes=VMEM_LIMIT)


def _rms(x, g):
    return x * lax.rsqrt(jnp.mean(x * x, axis=-1, keepdims=True) + EPS) * g


def _mod_slice(mod_ref, row, slot, d):
    return mod_ref[row:row + 1, slot * d:(slot + 1) * d]


def _norm_mod(x, g, shift, scale):
    return _rms(x, g) * (1.0 + scale) + shift


def _ada_kernel(c_ref, w_ref, b_ref, o_ref):
    c = c_ref[...]
    s = c * jax.nn.sigmoid(c)
    o_ref[...] = jnp.dot(s, w_ref[...], preferred_element_type=F32,
                         precision=lax.Precision.HIGHEST) + b_ref[...]


def ada_table(cvecs, w_ada, b_ada):
    depth, d, n = w_ada.shape
    tn = 1536
    return pl.pallas_call(
        _ada_kernel,
        grid=(depth, n // tn),
        in_specs=[
            pl.BlockSpec((8, d), lambda i, j: (0, 0)),
            pl.BlockSpec((None, d, tn), lambda i, j: (i, 0, j)),
            pl.BlockSpec((None, 1, tn), lambda i, j: (i, 0, j)),
        ],
        out_specs=pl.BlockSpec((None, 8, tn), lambda i, j: (i, 0, j)),
        out_shape=jax.ShapeDtypeStruct((depth, 8, n), F32),
        compiler_params=_cparams("parallel", "parallel"),
        name="ada_table",
    )(cvecs, w_ada, b_ada.reshape(depth, 1, n))


def _rope(a, cos, sin):
    lane = lax.broadcasted_iota(jnp.int32, (a.shape[0], LANES), 1)
    first = (lane % 32) < 16
    outs = []
    for cb in range(a.shape[1] // LANES):
        blk = a[:, cb * LANES:(cb + 1) * LANES]
        partner = jnp.where(first, pltpu.roll(blk, LANES - 16, 1), pltpu.roll(blk, 16, 1))
        outs.append(blk * cos + partner * sin)
    return jnp.concatenate(outs, axis=1)


def _mixer_proj_kernel(x_ref, g_ref, mod_ref, w_ref, *rest, mod_row, chunk, rope_cols, q_cols):
    if rope_cols:
        cos_ref, sin_ref, o_ref = rest
        cos, sin = cos_ref[...], sin_ref[...]
    else:
        (o_ref,) = rest
    d = x_ref.shape[1]
    h = _norm_mod(x_ref[...], g_ref[0:1, :], _mod_slice(mod_ref, mod_row, SH1, d),
                  _mod_slice(mod_ref, mod_row, SC1, d)).astype(BF16)
    for c0 in range(0, o_ref.shape[1], chunk):
        acc = jnp.dot(h, w_ref[:, c0:c0 + chunk], preferred_element_type=F32)
        if c0 < q_cols:
            acc = acc * LOG2E
        if c0 < rope_cols:
            acc = _rope(acc, cos, sin)
        o_ref[:, c0:c0 + chunk] = acc.astype(o_ref.dtype)


def mixer_projection(x, norm_g, mods, layer, w, *, mod_row, tm, chunk, q_cols=0, rope=None, rope_cols=0):
    m, d = x.shape
    n = w.shape[1]
    tm = min(tm, m)
    assert n % chunk == 0 and rope_cols % chunk == 0 and q_cols % chunk == 0
    in_specs = [
        pl.BlockSpec((tm, d), lambda i: (i, 0)),
        pl.BlockSpec((None,) + norm_g.shape[1:], lambda i: (layer, 0, 0)),
        pl.BlockSpec((None,) + mods.shape[1:], lambda i: (layer, 0, 0)),
        pl.BlockSpec((d, n), lambda i: (0, 0), pipeline_mode=pl.Buffered(1)),
    ]
    args = [x, norm_g, mods, w]
    if rope_cols:
        in_specs += [pl.BlockSpec((tm, LANES), lambda i: (i, 0))] * 2
        args += list(rope)
    return pl.pallas_call(
        functools.partial(_mixer_proj_kernel, mod_row=mod_row, chunk=chunk, rope_cols=rope_cols, q_cols=q_cols),
        grid=(m // tm,),
        in_specs=in_specs,
        out_specs=pl.BlockSpec((tm, n), lambda i: (i, 0)),
        out_shape=jax.ShapeDtypeStruct((m, n), BF16),
        compiler_params=_cparams("parallel"),
        name="mixer_projection",
    )(*args)


def _outproj_kernel(x_ref, g_ref, mod_ref, w_ref, *rest, n_y, g_row, mod_row, slot, route):
    y_refs, rest = rest[:n_y], rest[n_y:]
    if route:
        router_refs, o_ref, route_out = rest[:3], rest[3], rest[4:]
    else:
        (o_ref,) = rest
    d = x_ref.shape[1]
    y = y_refs[0][...]
    if n_y == 2:
        y = (y.astype(F32) + y_refs[1][...].astype(F32)).astype(BF16)
    t = jnp.dot(y, w_ref[...], preferred_element_type=F32)
    gate = _mod_slice(mod_ref, mod_row, slot, d)
    x_new = x_ref[...] + gate * _rms(t, g_ref[g_row:g_row + 1, :])
    o_ref[...] = x_new
    if route:
        _route_rows(x_new, g_ref, mod_ref, mod_row, *router_refs, *route_out)


def _router_specs(m, d, ne, tm):
    ins = [pl.BlockSpec((d, ne), lambda i: (0, 0))] * 2 + [pl.BlockSpec((1, ne), lambda i: (0, 0))]
    outs = [pl.BlockSpec((tm, 4), lambda i: (i, 0)), pl.BlockSpec((tm, 2), lambda i: (i, 0)),
            pl.BlockSpec((None, 1, ne), lambda i: (i, 0, 0))]
    shapes = [jax.ShapeDtypeStruct((m, 4), jnp.int32), jax.ShapeDtypeStruct((m, 2), F32),
              jax.ShapeDtypeStruct((m // tm, 1, ne), jnp.int32)]
    return ins, outs, shapes


def outproj_residual(x, norm_g, mods, layer, w, ys, *, g_row, mod_row, slot, tm, router=None):
    m, d = x.shape
    k = w.shape[0]
    tm = min(tm, m)
    in_specs = [
        pl.BlockSpec((tm, d), lambda i: (i, 0)),
        pl.BlockSpec((None,) + norm_g.shape[1:], lambda i: (layer, 0, 0)),
        pl.BlockSpec((None,) + mods.shape[1:], lambda i: (layer, 0, 0)),
        pl.BlockSpec((k, d), lambda i: (0, 0)),
    ] + [pl.BlockSpec((tm, k), lambda i: (i, 0))] * len(ys)
    args = [x, norm_g, mods, w, *ys]
    out_specs = [pl.BlockSpec((tm, d), lambda i: (i, 0))]
    out_shape = [jax.ShapeDtypeStruct((m, d), F32)]
    if router is not None:
        w_router, b_router = router
        ne = w_router.shape[1]
        r_in, r_out, r_shapes = _router_specs(m, d, ne, tm)
        in_specs += r_in
        args += [*_split_bf16(w_router), b_router.reshape(1, ne)]
        out_specs += r_out
        out_shape += r_shapes
    outs = pl.pallas_call(
        functools.partial(_outproj_kernel, n_y=len(ys), g_row=g_row, mod_row=mod_row, slot=slot,
                          route=router is not None),
        grid=(m // tm,),
        in_specs=in_specs,
        out_specs=out_specs,
        out_shape=out_shape,
        compiler_params=_cparams("parallel"),
        name="outproj_residual",
    )(*args)
    return outs[0] if router is None else (outs[0], tuple(outs[1:]))


def _dot_nt(a, b):
    return lax.dot_general(a, b, (((1,), (1,)), ((), ())), preferred_element_type=F32)


def _attn_kernel(sink_ref, q_ref, *rest, tq, nsub, kv_heads, group, dh, has_local):
    if has_local:
        k_refs, v_refs = rest[:nsub + 2], rest[nsub + 2:2 * nsub + 4]
    kc_ref, vc_ref, o_ref = rest[-3:]
    assert 2 * dh == LANES and tq == WINDOW and group % 2 == 0
    b = pl.program_id(0)
    nb = pl.num_programs(0)
    rows = group * tq
    lc = kc_ref.shape[0]
    head_of_row = lax.broadcasted_iota(jnp.int32, (rows, 1), 0) // tq
    lane = lax.broadcasted_iota(jnp.int32, (1, LANES), 1)
    if has_local:
        r = lax.broadcasted_iota(jnp.int32, (rows, tq), 0) % tq
        c = lax.broadcasted_iota(jnp.int32, (rows, tq), 1)
    for sub, j in [(sub, j) for sub in range(nsub) for j in range(kv_heads)]:
        qrows = slice(sub * tq, (sub + 1) * tq)
        hs = slice(j * dh, (j + 1) * dh)
        pair = slice((j // 2) * LANES, (j // 2 + 1) * LANES)
        v_low = j % 2 == 0
        keep_v = (lane < dh) if v_low else (lane >= dh)
        qg = jnp.concatenate(
            [q_ref[qrows, (j * group + g) * dh:(j * group + g + 1) * dh] for g in range(group)], axis=0)
        sink = jnp.full((rows, 1), sink_ref[j * group] * LOG2E, F32)
        for g in range(1, group):
            sink = jnp.where(head_of_row == g, sink_ref[j * group + g] * LOG2E, sink)
        if has_local:
            kp_ref, kq_ref, kn_ref = k_refs[sub:sub + 3]
            vp_ref, vq_ref, vn_ref = v_refs[sub:sub + 3]
            prev_ok = (c >= r) if sub > 0 else (c >= r) & (b > 0)
            next_ok = (c <= r) if sub < nsub - 1 else (c <= r) & (b < nb - 1)
            k_all = jnp.concatenate([kc_ref[:, hs], kp_ref[:, hs], kq_ref[:, hs], kn_ref[:, hs]], axis=0)
            v_all = jnp.concatenate([vc_ref[:, pair], vp_ref[:, pair], vq_ref[:, pair], vn_ref[:, pair]], axis=0)
            s = _dot_nt(qg, k_all)
            s = jnp.concatenate([s[:, :lc],
                                 jnp.where(prev_ok, s[:, lc:lc + tq], NEG_INF),
                                 s[:, lc + tq:lc + 2 * tq],
                                 jnp.where(next_ok, s[:, lc + 2 * tq:], NEG_INF)], axis=1)
        else:
            v_all = vc_ref[:, pair]
            s = _dot_nt(qg, kc_ref[:, hs])
        m = jnp.maximum(jnp.max(s, axis=-1, keepdims=True), sink)
        p = jnp.exp2(s - m).astype(BF16)
        v_ones = jnp.where(keep_v, v_all, jnp.ones_like(v_all))
        pv = jnp.dot(p, v_ones, preferred_element_type=F32)
        den = pltpu.roll(pv, dh, 1) + jnp.exp2(sink - m)
        o = pv / den
        for g in range(0, group, 2):
            a, bb = o[g * tq:(g + 1) * tq], o[(g + 1) * tq:(g + 2) * tq]
            if v_low:
                both = jnp.where(lane < dh, a, pltpu.roll(bb, dh, 1))
            else:
                both = jnp.where(lane < dh, pltpu.roll(a, dh, 1), bb)
            h = j * group + g
            o_ref[qrows, h * dh:(h + 2) * dh] = both.astype(o_ref.dtype)


def attention(sink, qkv, qkv_ctx, *, has_local):
    s = qkv.shape[0]
    dh = qkv.shape[1] // (ATTN_HEADS + 2 * ATTN_KV_HEADS)
    hd = ATTN_HEADS * dh
    kvd = ATTN_KV_HEADS * dh
    kcol, vcol = hd // kvd, hd // kvd + 1
    tq = WINDOW
    nb = s // tq
    nsub = min(ATTN_BLOCKS_PER_STEP, nb)
    lc = qkv_ctx.shape[0]
    in_specs = [
        pl.BlockSpec(memory_space=pltpu.SMEM),
        pl.BlockSpec((nsub * tq, hd), lambda b: (b, 0)),
    ]
    args = [sink, qkv]
    if has_local:
        for col in (kcol, vcol):
            for off in range(-1, nsub + 1):
                in_specs.append(pl.BlockSpec(
                    (tq, kvd), lambda b, col=col, off=off: (jnp.clip(nsub * b + off, 0, nb - 1), col)))
                args.append(qkv)
    in_specs += [pl.BlockSpec((lc, kvd), lambda b: (0, kcol)), pl.BlockSpec((lc, kvd), lambda b: (0, vcol))]
    args += [qkv_ctx, qkv_ctx]
    return pl.pallas_call(
        functools.partial(_attn_kernel, tq=tq, nsub=nsub, kv_heads=ATTN_KV_HEADS,
                          group=ATTN_HEADS // ATTN_KV_HEADS, dh=dh, has_local=has_local),
        grid=(nb // nsub,),
        in_specs=in_specs,
        out_specs=pl.BlockSpec((nsub * tq, hd), lambda b: (b, 0)),
        out_shape=jax.ShapeDtypeStruct((s, hd), BF16),
        compiler_params=_cparams("parallel"),
        name="attention",
    )(*args)


N_RET_IN = 9


def _ret_kernel(*refs, heads, dk, dv):
    ins = [refs[:N_RET_IN], refs[N_RET_IN:2 * N_RET_IN]]
    outs = refs[2 * N_RET_IN:2 * N_RET_IN + 4]
    state_ref = refs[-1]
    i = pl.program_id(0)

    @pl.when(i == 0)
    def _():
        for dr in range(2):
            state_ref[dr] = ins[dr][8][...]

    for dr, h in [(dr, h) for h in range(heads) for dr in range(2)]:
        q_ref, k_ref, v_ref, gate_ref, intra_ref, qdec_ref, kdec_ref, cdec_ref, _ = ins[dr]
        z_ref = outs[dr]
        q = q_ref[:, h * dk:(h + 1) * dk]
        k = k_ref[:, h * dk:(h + 1) * dk]
        v = v_ref[:, h * dv:(h + 1) * dv]
        state = state_ref[dr, h]
        sc = _dot_nt(q, k) * intra_ref[h]
        qd = (q.astype(F32) * qdec_ref[h]).astype(BF16)
        o = (jnp.dot(sc.astype(BF16), v, preferred_element_type=F32)
             + jnp.dot(qd, state.astype(BF16), preferred_element_type=F32))
        kd = (k.astype(F32) * kdec_ref[h]).astype(BF16)
        state_ref[dr, h] = state * cdec_ref[h] + lax.dot_general(
            kd, v, (((0,), (0,)), ((), ())), preferred_element_type=F32)
        mu = jnp.mean(o, axis=-1, keepdims=True)
        oc = o - mu
        var = jnp.mean(oc * oc, axis=-1, keepdims=True)
        gate = gate_ref[:, h * dv:(h + 1) * dv].astype(F32)
        z = gate * jax.nn.sigmoid(gate) * (oc * lax.rsqrt(var + EPS))
        z_ref[:, h * dv:(h + 1) * dv] = z.astype(z_ref.dtype)

    @pl.when(i == pl.num_programs(0) - 1)
    def _():
        for dr in range(2):
            outs[2 + dr][...] = state_ref[dr]


def retention_scan(proj, tables_f, tables_b, s0_f, s0_b):
    m, n = proj.shape
    d = n // 8
    heads = RET_HEADS
    dk, dv = d // heads, 2 * d // heads
    c = min(RET_CHUNK, m)
    nc = m // c
    once = lambda a: pl.BlockSpec(a.shape, lambda i: (0,) * a.ndim, pipeline_mode=pl.Buffered(1))
    in_specs, args, out_specs = [], [], []
    for reverse, tables, s0 in ((False, tables_f, s0_f), (True, tables_b, s0_b)):
        row = (lambda i: nc - 1 - i) if reverse else (lambda i: i)
        in_specs += [
            pl.BlockSpec((c, d), lambda i, row=row: (row(i), 0)),
            pl.BlockSpec((c, d), lambda i, row=row: (row(i), 1)),
            pl.BlockSpec((c, 2 * d), lambda i, row=row: (row(i), 1)),
            pl.BlockSpec((c, 2 * d), lambda i, row=row, blk=3 if reverse else 2: (row(i), blk)),
        ] + [once(a) for a in (*tables, s0)]
        args += [proj] * 4 + [*tables, s0]
        out_specs.append(pl.BlockSpec((c, 2 * d), lambda i, row=row: (row(i), 0)))
    out_specs += [pl.BlockSpec(s0_f.shape, lambda i: (0,) * s0_f.ndim)] * 2
    return pl.pallas_call(
        functools.partial(_ret_kernel, heads=heads, dk=dk, dv=dv),
        grid=(nc,),
        in_specs=in_specs,
        out_specs=out_specs,
        out_shape=[jax.ShapeDtypeStruct((m, 2 * d), BF16)] * 2 + [jax.ShapeDtypeStruct(s0_f.shape, F32)] * 2,
        scratch_shapes=[pltpu.VMEM((2,) + s0_f.shape, F32)],
        compiler_params=_cparams("arbitrary"),
        name="retention_scan",
    )(*args)


def retention_tables(log_decay_row, c, reverse):
    lg = -jnp.exp(log_decay_row.astype(F32))
    idx = jnp.arange(c, dtype=F32)
    diff = idx[:, None] - idx[None, :]
    if reverse:
        diff = -diff
    intra = jnp.where(diff >= 0, jnp.exp(lg[:, None, None] * jnp.maximum(diff, 0.0)), 0.0)
    fwd_idx = (c - 1.0 - idx) if reverse else idx
    qdec = jnp.exp(lg[:, None] * (fwd_idx + 1.0))[:, :, None]
    kdec = jnp.exp(lg[:, None] * (c - 1.0 - fwd_idx))[:, :, None]
    cdec = jnp.exp(lg * c)[:, None, None]
    return intra, qdec, kdec, cdec


def _pool_kernel(x_ref, xp_ref, xn_ref, g_ref, mod_ref, w_ref, ps_ref, o_ref, h_ref, *, mod_row, seq):
    tm, d = x_ref.shape
    halo = POOL_HALO
    i = pl.program_id(0)
    g0 = g_ref[0:1, :]
    shift = _mod_slice(mod_ref, mod_row, SH1, d)
    scale = _mod_slice(mod_ref, mod_row, SC1, d)
    hp = _norm_mod(xp_ref[...], g0, shift, scale)
    hn = _norm_mod(xn_ref[...], g0, shift, scale)
    h_ref[0:halo, :] = jnp.where(i > 0, hp, 0.0)
    h_ref[halo:halo + tm, :] = _norm_mod(x_ref[...], g0, shift, scale)
    h_ref[halo + tm:, :] = jnp.where(i < pl.num_programs(0) - 1, hn, 0.0)

    t = i * tm + lax.broadcasted_iota(jnp.int32, (tm, 1), 0)
    gw = d // len(POOL_WINDOWS)
    ys = []
    for g, w in enumerate(POOL_WINDOWS):
        cols = slice(g * gw, (g + 1) * gw)
        tot = h_ref[halo - w // 2:halo - w // 2 + tm, cols]
        for off in range(-w // 2 + 1, w // 2):
            tot = tot + h_ref[halo + off:halo + off + tm, cols]
        cnt = (jnp.minimum(t + w // 2, seq) - jnp.maximum(t - w // 2, 0)).astype(F32)
        dm = tot / cnt - h_ref[halo:halo + tm, cols]
        ys.append(jnp.dot(dm.astype(BF16), w_ref[g], preferred_element_type=F32))
    y = jnp.concatenate(ys, axis=1) * ps_ref[...]
    o_ref[...] = x_ref[...] + _mod_slice(mod_ref, mod_row, GA1, d) * _rms(y, g_ref[1:2, :])


def pool_layer(x, norm_g, mods, layer, w_pool, pool_scale, *, mod_row, tm):
    m, d = x.shape
    tm = min(tm, m)
    nt = m // tm
    hb = tm // POOL_HALO
    return pl.pallas_call(
        functools.partial(_pool_kernel, mod_row=mod_row, seq=m),
        grid=(nt,),
        in_specs=[
            pl.BlockSpec((tm, d), lambda i: (i, 0)),
            pl.BlockSpec((POOL_HALO, d), lambda i: (jnp.maximum(i * hb - 1, 0), 0)),
            pl.BlockSpec((POOL_HALO, d), lambda i: (jnp.minimum((i + 1) * hb, nt * hb - 1), 0)),
            pl.BlockSpec((None,) + norm_g.shape[1:], lambda i: (layer, 0, 0)),
            pl.BlockSpec((None,) + mods.shape[1:], lambda i: (layer, 0, 0)),
            pl.BlockSpec(w_pool.shape, lambda i: (0, 0, 0)),
            pl.BlockSpec((1, d), lambda i: (0, 0)),
        ],
        out_specs=pl.BlockSpec((tm, d), lambda i: (i, 0)),
        out_shape=jax.ShapeDtypeStruct((m, d), F32),
        scratch_shapes=[pltpu.VMEM((tm + 2 * POOL_HALO, d), F32)],
        compiler_params=_cparams("parallel"),
        name="pool_layer",
    )(x, x, x, norm_g, mods, w_pool, pool_scale.reshape(1, d))


def _swiglu_partial(h, w1, w3, w2):
    a = jnp.dot(h, w1, preferred_element_type=F32)
    b = jnp.dot(h, w3, preferred_element_type=F32)
    u = (a * jax.nn.sigmoid(a) * b).astype(BF16)
    return jnp.dot(u, w2, preferred_element_type=F32)


def _ffn_kernel(x_ref, g_ref, mod_ref, w1_ref, w3_ref, w2_ref, *rest, mod_row, n_cast, nf):
    cast_src, o_ref, cast_dst = rest[:n_cast], rest[n_cast], rest[n_cast + 1:2 * n_cast + 1]
    h_ref, acc_ref = rest[2 * n_cast + 1:]
    d = x_ref.shape[1]
    f = pl.program_id(1)
    for src, dst in zip(cast_src, cast_dst):
        dst[...] = src[...].astype(dst.dtype)

    tf = w1_ref.shape[1] // nf
    assert nf >= 2
    for c in range(nf):
        @pl.when(f == c)
        def _(c=c):
            if c == 0:
                h = _norm_mod(x_ref[...], g_ref[2:3, :], _mod_slice(mod_ref, mod_row, SH2, d),
                              _mod_slice(mod_ref, mod_row, SC2, d)).astype(BF16)
                h_ref[...] = h
            else:
                h = h_ref[...]
            cols = slice(c * tf, (c + 1) * tf)
            part = _swiglu_partial(h, w1_ref[:, cols], w3_ref[:, cols], w2_ref[cols, :])
            if c == 0:
                acc_ref[...] = part
            elif c < nf - 1:
                acc_ref[...] += part
            else:
                y = acc_ref[...] + part
                o_ref[...] = x_ref[...] + _mod_slice(mod_ref, mod_row, GA2, d) * _rms(y, g_ref[3:4, :])


def ffn_layer(x, norm_g, mods, layer, w1, w3, w2, *, mod_row, tm, tf, cast=None):
    m, d = x.shape
    ff = w1.shape[1]
    tm = min(tm, m)
    nf = ff // tf
    steps = (m // tm) * nf
    in_specs = [
        pl.BlockSpec((tm, d), lambda i, f: (i, 0)),
        pl.BlockSpec((None,) + norm_g.shape[1:], lambda i, f: (layer, 0, 0)),
        pl.BlockSpec((None,) + mods.shape[1:], lambda i, f: (layer, 0, 0)),
        pl.BlockSpec((d, ff), lambda i, f: (0, 0), pipeline_mode=pl.Buffered(1)),
        pl.BlockSpec((d, ff), lambda i, f: (0, 0), pipeline_mode=pl.Buffered(1)),
        pl.BlockSpec((ff, d), lambda i, f: (0, 0), pipeline_mode=pl.Buffered(1)),
    ]
    out_specs = [pl.BlockSpec((tm, d), lambda i, f: (i, 0))]
    out_shape = [jax.ShapeDtypeStruct((m, d), F32)]
    cast_arrays, cast_idx = cast if cast is not None else ((), 0)
    for a in cast_arrays:
        _, ne, r, c = a.shape
        per_expert = steps // ne
        assert per_expert * ne == steps and r % (per_expert * 16) == 0
        rows = r // per_expert
        in_specs.append(pl.BlockSpec(
            (None, None, rows, c),
            lambda i, f, pe=per_expert: (cast_idx, (i * nf + f) // pe, (i * nf + f) % pe, 0)))
        out_specs.append(pl.BlockSpec(
            (None, rows, c), lambda i, f, pe=per_expert: ((i * nf + f) // pe, (i * nf + f) % pe, 0)))
        out_shape.append(jax.ShapeDtypeStruct((ne, r, c), BF16))
    outs = pl.pallas_call(
        functools.partial(_ffn_kernel, mod_row=mod_row, n_cast=len(cast_arrays), nf=nf),
        grid=(m // tm, nf),
        in_specs=in_specs,
        out_specs=out_specs,
        out_shape=out_shape,
        scratch_shapes=[pltpu.VMEM((tm, d), BF16), pltpu.VMEM((tm, d), F32)],
        compiler_params=_cparams("arbitrary", "arbitrary"),
        name="ffn_layer",
    )(x, norm_g, mods, w1, w3, w2, *cast_arrays)
    return outs[0] if cast is None else (outs[0], tuple(outs[1:]))


RUN_ALIGN = 16
RUN_BITS = tuple(range(9, 3, -1))
EXPERT_ROWS = 512
MOE_TOKEN_TILE = 512
TAIL_BITS = tuple(range(9, 3, -1))


def _split_dma(length, src, src_row, dst, dst_row, sem, bits, *, start, src_advances=True):
    for b in bits:
        size = 1 << b
        off = (length >> (b + 1)) << (b + 1)

        @pl.when(((length >> b) & 1) == 1)
        def _():
            s_row = pl.multiple_of(src_row + off, RUN_ALIGN) if src_advances else src_row
            cp = pltpu.make_async_copy(src.at[pl.ds(s_row, size)],
                                       dst.at[pl.ds(pl.multiple_of(dst_row + off, RUN_ALIGN), size)], sem)
            if start:
                cp.start()
            else:
                cp.wait()


def _run_dmas(tile, ne, lpad_ref, loff_ref, base_ref, buf, hbm_ref, sem, *, to_hbm, start):
    for e in range(ne):
        length = lpad_ref[tile * ne + e]
        lo = loff_ref[tile * ne + e]
        gb = base_ref[tile * ne + e]
        if to_hbm:
            _split_dma(length, buf, lo, hbm_ref, gb, sem, RUN_BITS, start=start)
        else:
            _split_dma(length, hbm_ref, gb, buf, lo, sem, RUN_BITS, start=start)


def _split_bf16(a):
    hi = a.astype(BF16)
    return hi, (a - hi.astype(F32)).astype(BF16)


def _dot_f32x3(a, b_hi, b_lo):
    a_hi, a_lo = _split_bf16(a)
    dot = functools.partial(jnp.dot, preferred_element_type=F32)
    return dot(a_hi, b_hi) + (dot(a_hi, b_lo) + dot(a_lo, b_hi))


def _local_rows(ri_ref, loff_ref, tile, ne):
    e1, e2 = ri_ref[:, 0:1], ri_ref[:, 1:2]
    lo1, lo2 = ri_ref[:, 2:3], ri_ref[:, 3:4]
    for e in range(ne):
        off = loff_ref[tile * ne + e]
        lo1 = lo1 + jnp.where(e1 == e, off, 0)
        lo2 = lo2 + jnp.where(e2 == e, off, 0)
    return lo1, lo2


def _router_kernel(x_ref, g_ref, mod_ref, wr_hi_ref, wr_lo_ref, br_ref, ri_ref, rw_ref, cnt_ref, *, mod_row):
    _route_rows(x_ref[...], g_ref, mod_ref, mod_row, wr_hi_ref, wr_lo_ref, br_ref, ri_ref, rw_ref, cnt_ref)


def _route_rows(x, g_ref, mod_ref, mod_row, wr_hi_ref, wr_lo_ref, br_ref, ri_ref, rw_ref, cnt_ref):
    tm, d = x.shape
    ne = wr_hi_ref.shape[1]
    h = _norm_mod(x, g_ref[2:3, :], _mod_slice(mod_ref, mod_row, SH2, d), _mod_slice(mod_ref, mod_row, SC2, d))
    logits = _dot_f32x3(h, wr_hi_ref[...], wr_lo_ref[...]) + br_ref[...]
    lane = lax.broadcasted_iota(jnp.int32, logits.shape, 1)
    v1 = jnp.max(logits, axis=-1, keepdims=True)
    i1 = jnp.min(jnp.where(logits == v1, lane, ne), axis=-1, keepdims=True)
    rest = jnp.where(lane == i1, -jnp.inf, logits)
    v2 = jnp.max(rest, axis=-1, keepdims=True)
    i2 = jnp.min(jnp.where(rest == v2, lane, ne), axis=-1, keepdims=True)
    e2 = jnp.exp(v2 - v1)
    den = 1.0 + e2
    oh1 = (lane == i1).astype(F32)
    oh2 = (lane == i2).astype(F32)
    both = oh1 + oh2
    r = lax.broadcasted_iota(jnp.int32, (tm, tm), 0)
    c = lax.broadcasted_iota(jnp.int32, (tm, tm), 1)
    tri = jnp.where(c < r, 1.0, 0.0).astype(BF16)
    rank_all = jnp.dot(tri, both.astype(BF16), preferred_element_type=F32)
    rank1 = jnp.sum(rank_all * oh1, axis=-1, keepdims=True).astype(jnp.int32)
    rank2 = jnp.sum(rank_all * oh2, axis=-1, keepdims=True).astype(jnp.int32)
    col = lax.broadcasted_iota(jnp.int32, ri_ref.shape, 1)
    ri_ref[...] = jnp.where(col == 0, i1, jnp.where(col == 1, i2, jnp.where(col == 2, rank1, rank2)))
    colw = lax.broadcasted_iota(jnp.int32, rw_ref.shape, 1)
    rw_ref[...] = jnp.where(colw == 0, 1.0 / den, e2 / den)
    cnt_ref[...] = jnp.sum(both, axis=0, keepdims=True).astype(jnp.int32)


def moe_router(x, norm_g, mods, layer, w_router, b_router, *, mod_row, tm):
    m, d = x.shape
    ne = w_router.shape[1]
    return pl.pallas_call(
        functools.partial(_router_kernel, mod_row=mod_row),
        grid=(m // tm,),
        in_specs=[
            pl.BlockSpec((tm, d), lambda i: (i, 0)),
            pl.BlockSpec((None,) + norm_g.shape[1:], lambda i: (layer, 0, 0)),
            pl.BlockSpec((None,) + mods.shape[1:], lambda i: (layer, 0, 0)),
            pl.BlockSpec((d, ne), lambda i: (0, 0)),
            pl.BlockSpec((d, ne), lambda i: (0, 0)),
            pl.BlockSpec((1, ne), lambda i: (0, 0)),
        ],
        out_specs=[
            pl.BlockSpec((tm, 4), lambda i: (i, 0)),
            pl.BlockSpec((tm, 2), lambda i: (i, 0)),
            pl.BlockSpec((None, 1, ne), lambda i: (i, 0, 0)),
        ],
        out_shape=[
            jax.ShapeDtypeStruct((m, 4), jnp.int32),
            jax.ShapeDtypeStruct((m, 2), F32),
            jax.ShapeDtypeStruct((m // tm, 1, ne), jnp.int32),
        ],
        compiler_params=_cparams("parallel"),
        name="moe_router",
    )(x, norm_g, mods, *_split_bf16(w_router), b_router.reshape(1, ne))


def _dispatch_kernel(lpad_ref, loff_ref, base_ref, tlen_ref, tstart_ref, nu_ref, x_ref, g_ref, mod_ref, ri_ref,
                     *rest, mod_row, ne, tile0, first):
    xs_ref, cbuf, zbuf, sem = rest if first else rest[1:]
    tm, d = x_ref.shape
    rc = cbuf.shape[1]
    i = pl.program_id(0)
    last = i == pl.num_programs(0) - 1
    slot = i % 2

    def zero_tails(start):
        for e in range(ne):
            _split_dma(tlen_ref[e], zbuf, 0, xs_ref, tstart_ref[e], sem.at[2], TAIL_BITS, start=start,
                       src_advances=False)

        def unused_tile(t, carry):
            cp = pltpu.make_async_copy(
                zbuf, xs_ref.at[pl.ds(pl.multiple_of(t * EXPERT_ROWS, EXPERT_ROWS), EXPERT_ROWS)], sem.at[2])
            if start:
                cp.start()
            else:
                cp.wait()
            return carry

        lax.fori_loop(nu_ref[0], xs_ref.shape[0] // EXPERT_ROWS, unused_tile, 0)

    if first:
        @pl.when(i == 0)
        def _():
            zbuf[...] = jnp.zeros_like(zbuf)
            zero_tails(True)

    h = _norm_mod(x_ref[...], g_ref[2:3, :], _mod_slice(mod_ref, mod_row, SH2, d),
                  _mod_slice(mod_ref, mod_row, SC2, d)).astype(BF16)
    lo1, lo2 = _local_rows(ri_ref, loff_ref, i + tile0, ne)
    lane = lax.broadcasted_iota(jnp.int32, (tm, rc), 1)
    onehot = jnp.where((lane == lo1) | (lane == lo2), 1.0, 0.0).astype(BF16)
    sorted_rows = lax.dot_general(onehot, h, (((0,), (0,)), ((), ())), preferred_element_type=F32)
    cbuf[slot] = sorted_rows.astype(BF16)
    moves = functools.partial(_run_dmas, ne=ne, lpad_ref=lpad_ref, loff_ref=loff_ref, base_ref=base_ref,
                              hbm_ref=xs_ref, to_hbm=True)
    moves(i + tile0, buf=cbuf.at[slot], sem=sem.at[slot], start=True)

    @pl.when(i > 0)
    def _():
        moves(i + tile0 - 1, buf=cbuf.at[1 - slot], sem=sem.at[1 - slot], start=False)

    @pl.when(last)
    def _():
        moves(i + tile0, buf=cbuf.at[slot], sem=sem.at[slot], start=False)
        if first:
            zero_tails(False)


def moe_dispatch(x, norm_g, mods, layer, route_i, tabs, rows, xs=None, *, mod_row, tm, ne, tile0):
    m, d = x.shape
    rc = 2 * tm + ne * RUN_ALIGN
    first = xs is None
    in_specs = [
        pl.BlockSpec((tm, d), lambda i, *_: (i, 0)),
        pl.BlockSpec((None,) + norm_g.shape[1:], lambda i, *_: (layer, 0, 0)),
        pl.BlockSpec((None,) + mods.shape[1:], lambda i, *_: (layer, 0, 0)),
        pl.BlockSpec((tm, 4), lambda i, *_: (i, 0)),
    ]
    args = [*tabs, x, norm_g, mods, route_i]
    if not first:
        in_specs.append(pl.BlockSpec(memory_space=pl.ANY))
        args.append(xs)
    return pl.pallas_call(
        functools.partial(_dispatch_kernel, mod_row=mod_row, ne=ne, tile0=tile0, first=first),
        grid_spec=pltpu.PrefetchScalarGridSpec(
            num_scalar_prefetch=len(tabs),
            grid=(m // tm,),
            in_specs=in_specs,
            out_specs=pl.BlockSpec(memory_space=pl.ANY),
            scratch_shapes=[pltpu.VMEM((2, rc, d), BF16), pltpu.VMEM((EXPERT_ROWS, d), BF16),
                            pltpu.SemaphoreType.DMA((3,))],
        ),
        out_shape=jax.ShapeDtypeStruct((rows, d), BF16),
        input_output_aliases={} if first else {len(args) - 1: 0},
        compiler_params=_cparams("arbitrary"),
        name="moe_dispatch",
    )(*args)


def _experts_kernel(te_ref, nu_ref, x_ref, w1_ref, w3_ref, w2_ref, y_ref, acc_ref):
    del te_ref
    i = pl.program_id(0)
    f = pl.program_id(1)
    nf = pl.num_programs(1)
    used = i < nu_ref[0]

    @pl.when(used)
    def _():
        part = _swiglu_partial(x_ref[...], w1_ref[...], w3_ref[...], w2_ref[...])

        @pl.when(f == 0)
        def _():
            acc_ref[...] = part

        @pl.when((f > 0) & (f < nf - 1))
        def _():
            acc_ref[...] += part

        @pl.when(f == nf - 1)
        def _():
            y_ref[...] = (acc_ref[...] + part).astype(y_ref.dtype)

    @pl.when(jnp.logical_not(used) & (f == nf - 1))
    def _():
        y_ref[...] = jnp.zeros_like(y_ref)


def moe_experts(xs, tile_expert, n_used, w1, w3, w2, *, tm, tf):
    rows, d = xs.shape
    ff = w1.shape[2]
    nf = ff // tf
    assert nf >= 2

    def wmap_in(i, f, te, nu):
        live = i < nu[0]
        return (te[i], 0, jnp.where(live, f, nf - 1))

    def wmap_out(i, f, te, nu):
        live = i < nu[0]
        return (te[i], jnp.where(live, f, nf - 1), 0)

    def xmap(i, f, te, nu):
        return (jnp.maximum(jnp.minimum(i, nu[0] - 1), 0), 0)

    return pl.pallas_call(
        _experts_kernel,
        grid_spec=pltpu.PrefetchScalarGridSpec(
            num_scalar_prefetch=2,
            grid=(rows // tm, nf),
            in_specs=[
                pl.BlockSpec((tm, d), xmap),
                pl.BlockSpec((None, d, tf), wmap_in),
                pl.BlockSpec((None, d, tf), wmap_in),
                pl.BlockSpec((None, tf, d), wmap_out),
            ],
            out_specs=pl.BlockSpec((tm, d), lambda i, f, te, nu: (i, 0)),
            scratch_shapes=[pltpu.VMEM((tm, d), F32)],
        ),
        out_shape=jax.ShapeDtypeStruct((rows, d), BF16),
        compiler_params=_cparams("arbitrary", "arbitrary"),
        name="moe_experts",
    )(tile_expert, n_used, xs, w1, w3, w2)


def _combine_kernel(lpad_ref, loff_ref, base_ref, x_ref, ri_ref, rw_ref, g_ref, mod_ref, ys_ref, o_ref,
                    ybuf, sem, *, mod_row, ne, tile0):
    tm, d = x_ref.shape
    rc = ybuf.shape[1]
    i = pl.program_id(0)
    slot = i % 2
    moves = functools.partial(_run_dmas, ne=ne, lpad_ref=lpad_ref, loff_ref=loff_ref, base_ref=base_ref,
                              hbm_ref=ys_ref, to_hbm=False)

    @pl.when(i == 0)
    def _():
        ybuf[...] = jnp.zeros_like(ybuf)
        moves(tile0, buf=ybuf.at[0], sem=sem.at[0], start=True)

    @pl.when(i + 1 < pl.num_programs(0))
    def _():
        moves(i + tile0 + 1, buf=ybuf.at[1 - slot], sem=sem.at[1 - slot], start=True)

    moves(i + tile0, buf=ybuf.at[slot], sem=sem.at[slot], start=False)
    lo1, lo2 = _local_rows(ri_ref, loff_ref, i + tile0, ne)
    lane = lax.broadcasted_iota(jnp.int32, (tm, rc), 1)
    rows = ybuf[slot]
    y1 = jnp.dot(jnp.where(lane == lo1, 1.0, 0.0).astype(BF16), rows, preferred_element_type=F32)
    y2 = jnp.dot(jnp.where(lane == lo2, 1.0, 0.0).astype(BF16), rows, preferred_element_type=F32)
    y = rw_ref[:, 0:1] * y1 + rw_ref[:, 1:2] * y2
    o_ref[...] = x_ref[...] + _mod_slice(mod_ref, mod_row, GA2, d) * _rms(y, g_ref[3:4, :])


def moe_combine(x, norm_g, mods, layer, route_i, route_w, ys, tabs, *, mod_row, tm, ne, tile0):
    m, d = x.shape
    rc = 2 * tm + ne * RUN_ALIGN
    return pl.pallas_call(
        functools.partial(_combine_kernel, mod_row=mod_row, ne=ne, tile0=tile0),
        grid_spec=pltpu.PrefetchScalarGridSpec(
            num_scalar_prefetch=3,
            grid=(m // tm,),
            in_specs=[
                pl.BlockSpec((tm, d), lambda i, *_: (i, 0)),
                pl.BlockSpec((tm, 4), lambda i, *_: (i, 0)),
                pl.BlockSpec((tm, 2), lambda i, *_: (i, 0)),
                pl.BlockSpec((None,) + norm_g.shape[1:], lambda i, *_: (layer, 0, 0)),
                pl.BlockSpec((None,) + mods.shape[1:], lambda i, *_: (layer, 0, 0)),
                pl.BlockSpec(memory_space=pl.ANY),
            ],
            out_specs=pl.BlockSpec((tm, d), lambda i, *_: (i, 0)),
            scratch_shapes=[pltpu.VMEM((2, rc, d), BF16), pltpu.SemaphoreType.DMA((2,))],
        ),
        out_shape=jax.ShapeDtypeStruct((m, d), F32),
        compiler_params=_cparams("arbitrary"),
        name="moe_combine",
    )(*tabs, x, route_i, route_w, norm_g, mods, ys)


def sparse_moe_layer(streams, norm_g, mods, layer, w_router, b_router, w1, w3, w2, *, tf, routed_first=None):
    ne = w_router.shape[1]
    tm_exp = EXPERT_ROWS
    routed = [moe_router(x, norm_g, mods, layer, w_router, b_router, mod_row=row, tm=tm)
              if k > 0 or routed_first is None else routed_first for k, (x, row, tm) in enumerate(streams)]
    counts = jnp.concatenate([r[2][:, 0, :] for r in routed], axis=0)
    nt = counts.shape[0]
    m_total = sum(x.shape[0] for x, _, _ in streams)
    lpad = (counts + RUN_ALIGN - 1) // RUN_ALIGN * RUN_ALIGN
    loff = jnp.cumsum(lpad, axis=1) - lpad
    group = jnp.sum(lpad, axis=0)
    gpad = (group + tm_exp - 1) // tm_exp * tm_exp
    ends = jnp.cumsum(gpad)
    starts = ends - gpad
    base = starts[None, :] + jnp.cumsum(lpad, axis=0) - lpad
    n_tiles = -(-(2 * m_total + nt * ne * (RUN_ALIGN - 1)) // tm_exp) + ne
    tile_start = jnp.arange(n_tiles, dtype=jnp.int32) * tm_exp
    n_used = (ends[-1] // tm_exp).astype(jnp.int32).reshape(1)
    tile_expert = jnp.sum(tile_start[:, None] >= ends[None, :], axis=1).astype(jnp.int32)
    tile_expert = jnp.minimum(tile_expert, tile_expert[jnp.maximum(n_used[0] - 1, 0)])
    nt_first = streams[0][0].shape[0] // streams[0][2]
    group_first = jnp.sum(lpad[:nt_first], axis=0)
    later_rows = m_total - streams[0][0].shape[0] + (nt - nt_first) * (RUN_ALIGN - 1)
    assert later_rows + tm_exp - RUN_ALIGN < 2 << TAIL_BITS[0]
    tabs = [a.reshape(-1).astype(jnp.int32)
            for a in (lpad, loff, base, gpad - group_first, starts + group_first, n_used)]
    xs, tile0 = None, 0
    for (x, row, tm), (route_i, _, _) in zip(streams, routed):
        xs = moe_dispatch(x, norm_g, mods, layer, route_i, tabs, n_tiles * tm_exp, xs,
                          mod_row=row, tm=tm, ne=ne, tile0=tile0)
        tile0 += x.shape[0] // tm
    ys = moe_experts(xs, tile_expert, n_used, w1, w3, w2, tm=tm_exp, tf=tf)
    outs, tile0 = [], 0
    for (x, row, tm), (route_i, route_w, _) in zip(streams, routed):
        outs.append(moe_combine(x, norm_g, mods, layer, route_i, route_w, ys, tabs[:3],
                                mod_row=row, tm=tm, ne=ne, tile0=tile0))
        tile0 += x.shape[0] // tm
    return outs


def _rope_tables(seq):
    n = 16
    inv = ROPE_BASE ** (-jnp.arange(n, dtype=F32) / n)
    t = jnp.arange(seq)
    row_ang = (t // GRID_W).astype(F32)[:, None] * inv[None, :]
    col_ang = (t % GRID_W).astype(F32)[:, None] * inv[None, :]
    cos = jnp.concatenate([jnp.cos(row_ang)] * 2 + [jnp.cos(col_ang)] * 2, axis=1)
    sin = jnp.concatenate([-jnp.sin(row_ang), jnp.sin(row_ang), -jnp.sin(col_ang), jnp.sin(col_ang)], axis=1)
    return jnp.tile(cos, (1, 2)), jnp.tile(sin, (1, 2))


def kernel(x, c, ctx, c_ctx, w_ada, b_ada, norm_g, attn_w_qkv, attn_w_o, attn_sink, ret_w_in, ret_w_o,
           ret_log_decay, pool_w, pool_scale, ffn_w1, ffn_w3, ffn_w2, moe_w_router, moe_b_router,
           moe_w1, moe_w3, moe_w2):
    batch, seq, d = x.shape
    assert batch == 1 and c.shape[0] == 1
    depth = w_ada.shape[0]
    lc = ctx.shape[1]
    xl = x.reshape(seq, d)
    xc = ctx.reshape(lc, d)

    cvecs = jnp.zeros((8, d), F32).at[LAT_ROW].set(c[0]).at[CTX_ROW].set(c_ctx)
    mods = ada_table(cvecs, w_ada, b_ada)
    rope = _rope_tables(seq)

    hd = attn_w_o.shape[1]
    dh = hd // ATTN_HEADS
    qkv_scale = jnp.concatenate([jnp.full((hd,), dh ** -0.5, F32),
                                 jnp.ones((attn_w_qkv.shape[2] - hd,), F32)])
    dk = d // RET_HEADS
    ret_scale = jnp.concatenate([jnp.ones((d,), F32), jnp.full((d,), dk ** -0.5, F32),
                                 jnp.ones((ret_w_in.shape[2] - 2 * d,), F32)])

    moe_bf16 = None
    for i in range(depth):
        last = i == depth - 1
        kind, j = i % N_MIXERS, i // N_MIXERS
        proj = functools.partial(mixer_projection, norm_g=norm_g, mods=mods, layer=i)
        routed_lat = None
        if i % 2 == 1:
            mix_out = functools.partial(outproj_residual, norm_g=norm_g, mods=mods, layer=i, g_row=1,
                                        mod_row=LAT_ROW, slot=GA1, tm=min(MOE_TOKEN_TILE, seq),
                                        router=(moe_w_router[i // 2], moe_b_router[i // 2]))
        else:
            mix_out = lambda x, **kw: (outproj_residual(x, norm_g, mods, i, g_row=1, mod_row=LAT_ROW,
                                                        slot=GA1, tm=1024, **kw), None)
        if kind == 0:
            w_qkv = (attn_w_qkv[j] * qkv_scale).astype(BF16)
            w_o = attn_w_o[j].astype(BF16)
            kvd = (w_qkv.shape[1] - hd) // 2
            qkv_l = proj(xl, w=w_qkv, mod_row=LAT_ROW, tm=512, chunk=256, q_cols=hd, rope=rope, rope_cols=hd + kvd)
            qkv_c = proj(xc, w=w_qkv, mod_row=CTX_ROW, tm=256, chunk=256, q_cols=hd)
            o_l = attention(attn_sink[j], qkv_l, qkv_c, has_local=True)
            xl, routed_lat = mix_out(xl, w=w_o, ys=[o_l])
            if not last:
                o_c = attention(attn_sink[j], qkv_c, qkv_c, has_local=False)
                xc = outproj_residual(xc, norm_g, mods, i, w_o, [o_c], g_row=1, mod_row=CTX_ROW, slot=GA1, tm=256)
        elif kind == 1:
            w_in = (ret_w_in[j] * ret_scale).astype(BF16)
            w_o = ret_w_o[j].astype(BF16)
            p_c = proj(xc, w=w_in, mod_row=CTX_ROW, tm=256, chunk=1024)
            p_l = proj(xl, w=w_in, mod_row=LAT_ROW, tm=512, chunk=1024)
            s0 = jnp.zeros((RET_HEADS, dk, 2 * dk), F32)
            tabs_f = retention_tables(ret_log_decay[j, 0], min(RET_CHUNK, lc), False)
            tabs_b = retention_tables(ret_log_decay[j, 1], min(RET_CHUNK, lc), True)
            zf_c, zb_c, s_f, s_b = retention_scan(p_c, tabs_f, tabs_b, s0, s0)
            zf_l, zb_l, _, _ = retention_scan(p_l, tabs_f, tabs_b, s_f, s_b)
            xl, routed_lat = mix_out(xl, w=w_o, ys=[zf_l, zb_l])
            if not last:
                xc = outproj_residual(xc, norm_g, mods, i, w_o, [zf_c, zb_c], g_row=1, mod_row=CTX_ROW, slot=GA1, tm=256)
        else:
            w_p = pool_w[j].astype(BF16)
            xl = pool_layer(xl, norm_g, mods, i, w_p, pool_scale[j], mod_row=LAT_ROW, tm=1024)
            if not last:
                xc = pool_layer(xc, norm_g, mods, i, w_p, pool_scale[j], mod_row=CTX_ROW, tm=256)

        f = i // 2
        if i % 2 == 0:
            w1, w3, w2 = ffn_w1[f].astype(BF16), ffn_w3[f].astype(BF16), ffn_w2[f].astype(BF16)
            tf = w1.shape[1] // 2
            steps = (seq // min(512, seq)) * 2
            ne = moe_w1.shape[1]
            if not last and steps % ne == 0 and d % (steps // ne * 16) == 0:
                xl, moe_bf16 = ffn_layer(xl, norm_g, mods, i, w1, w3, w2, mod_row=LAT_ROW, tm=512, tf=tf,
                                         cast=((moe_w1, moe_w3, moe_w2), (i + 1) // 2))
            else:
                xl = ffn_layer(xl, norm_g, mods, i, w1, w3, w2, mod_row=LAT_ROW, tm=512, tf=tf)
            if not last:
                xc = ffn_layer(xc, norm_g, mods, i, w1, w3, w2, mod_row=CTX_ROW, tm=256, tf=tf)
        else:
            if moe_bf16 is None:
                moe_bf16 = moe_w1[f].astype(BF16), moe_w3[f].astype(BF16), moe_w2[f].astype(BF16)
            (w1, w3, w2), moe_bf16 = moe_bf16, None
            tf = w1.shape[2] // 2
            streams = [(xl, LAT_ROW, min(MOE_TOKEN_TILE, seq))] + ([] if last else [(xc, CTX_ROW, min(256, lc))])
            outs = sparse_moe_layer(streams, norm_g, mods, i, moe_w_router[f], moe_b_router[f], w1, w3, w2, tf=tf,
                                    routed_first=routed_lat)
            xl = outs[0]
            if not last:
                xc = outs[1]
    return xl.reshape(batch, seq, d)
```

```python
import functools

import jax
import jax.numpy as jnp
from jax import lax
from jax.experimental import pallas as pl
from jax.experimental.pallas import tpu as pltpu

F32 = jnp.float32
BF16 = jnp.bfloat16

EPS = 1e-6
NEG_INF = -1e30
LOG2E = 1.4426950408889634
LANES = 128
VMEM_LIMIT = 56 * 1024 * 1024

GRID_W = 64
N_MIXERS = 3
ATTN_HEADS = 16
ATTN_KV_HEADS = 4
ATTN_BLOCKS_PER_STEP = 4
WINDOW = 128
ROPE_BASE = 10000.0
RET_HEADS = 4
RET_CHUNK = 256
POOL_WINDOWS = (2, 4, 8, 16)
POOL_HALO = 8
N_EXPERTS = 8

SH1, SC1, GA1, SH2, SC2, GA2 = range(6)
LAT_ROW, CTX_ROW = 0, 1


def _cparams(*sem):
    return pltpu.CompilerParams(dimension_semantics=sem, vmem_limit_bytes=VMEM_LIMIT)


def _rms(x, g):
    return x * lax.rsqrt(jnp.mean(x * x, axis=-1, keepdims=True) + EPS) * g


def _mod_slice(mod_ref, row, slot, d):
    return mod_ref[row:row + 1, slot * d:(slot + 1) * d]


def _norm_mod(x, g, shift, scale):
    return _rms(x, g) * (1.0 + scale) + shift


def _ada_kernel(c_ref, w_ref, b_ref, o_ref):
    c = c_ref[...]
    s = c * jax.nn.sigmoid(c)
    o_ref[...] = _dot_f32x3(s, *_split_bf16(w_ref[...])) + b_ref[...]


def ada_table(cvecs, w_ada, b_ada):
    depth, d, n = w_ada.shape
    tn = 1536
    return pl.pallas_call(
        _ada_kernel,
        grid=(depth, n // tn),
        in_specs=[
            pl.BlockSpec((8, d), lambda i, j: (0, 0)),
            pl.BlockSpec((None, d, tn), lambda i, j: (i, 0, j)),
            pl.BlockSpec((None, 1, tn), lambda i, j: (i, 0, j)),
        ],
        out_specs=pl.BlockSpec((None, 8, tn), lambda i, j: (i, 0, j)),
        out_shape=jax.ShapeDtypeStruct((depth, 8, n), F32),
        compiler_params=_cparams("parallel", "parallel"),
        name="ada_table",
    )(cvecs, w_ada, b_ada.reshape(depth, 1, n))


def _rope(a, cos, sin):
    lane = lax.broadcasted_iota(jnp.int32, (a.shape[0], LANES), 1)
    first = (lane % 32) < 16
    outs = []
    for cb in range(a.shape[1] // LANES):
        blk = a[:, cb * LANES:(cb + 1) * LANES]
        partner = jnp.where(first, pltpu.roll(blk, LANES - 16, 1), pltpu.roll(blk, 16, 1))
        outs.append(blk * cos + partner * sin)
    return jnp.concatenate(outs, axis=1)


def _mixer_proj_kernel(x_ref, g_ref, mod_ref, w_ref, *rest, mod_row, chunk, rope_cols, q_cols):
    if rope_cols:
        cos_ref, sin_ref, o_ref = rest
        cos, sin = cos_ref[...], sin_ref[...]
    else:
        (o_ref,) = rest
    d = x_ref.shape[1]
    h = _norm_mod(x_ref[...], g_ref[0:1, :], _mod_slice(mod_ref, mod_row, SH1, d),
                  _mod_slice(mod_ref, mod_row, SC1, d)).astype(BF16)
    for c0 in range(0, o_ref.shape[1], chunk):
        acc = jnp.dot(h, w_ref[:, c0:c0 + chunk], preferred_element_type=F32)
        if c0 < q_cols:
            acc = acc * LOG2E
        if c0 < rope_cols:
            acc = _rope(acc, cos, sin)
        o_ref[:, c0:c0 + chunk] = acc.astype(o_ref.dtype)


def mixer_projection(x, norm_g, mods, layer, w, *, mod_row, tm, chunk, q_cols=0, rope=None, rope_cols=0):
    m, d = x.shape
    n = w.shape[1]
    tm = min(tm, m)
    assert n % chunk == 0 and rope_cols % chunk == 0 and q_cols % chunk == 0
    in_specs = [
        pl.BlockSpec((tm, d), lambda i: (i, 0)),
        pl.BlockSpec((None,) + norm_g.shape[1:], lambda i: (layer, 0, 0)),
        pl.BlockSpec((None,) + mods.shape[1:], lambda i: (layer, 0, 0)),
        pl.BlockSpec((d, n), lambda i: (0, 0), pipeline_mode=pl.Buffered(1)),
    ]
    args = [x, norm_g, mods, w]
    if rope_cols:
        in_specs += [pl.BlockSpec((tm, LANES), lambda i: (i, 0))] * 2
        args += list(rope)
    return pl.pallas_call(
        functools.partial(_mixer_proj_kernel, mod_row=mod_row, chunk=chunk, rope_cols=rope_cols, q_cols=q_cols),
        grid=(m // tm,),
        in_specs=in_specs,
        out_specs=pl.BlockSpec((tm, n), lambda i: (i, 0)),
        out_shape=jax.ShapeDtypeStruct((m, n), BF16),
        compiler_params=_cparams("parallel"),
        name="mixer_projection",
    )(*args)


def _outproj_kernel(x_ref, g_ref, mod_ref, w_ref, *rest, n_y, g_row, mod_row, slot, route):
    y_refs, rest = rest[:n_y], rest[n_y:]
    if route:
        router_refs, o_ref, route_out = rest[:3], rest[3], rest[4:]
    else:
        (o_ref,) = rest
    d = x_ref.shape[1]
    y = y_refs[0][...]
    if n_y == 2:
        y = (y.astype(F32) + y_refs[1][...].astype(F32)).astype(BF16)
    t = jnp.dot(y, w_ref[...], preferred_element_type=F32)
    gate = _mod_slice(mod_ref, mod_row, slot, d)
    x_new = x_ref[...] + gate * _rms(t, g_ref[g_row:g_row + 1, :])
    o_ref[...] = x_new
    if route:
        _route_rows(x_new, g_ref, mod_ref, mod_row, *router_refs, *route_out)


def _router_specs(m, d, ne, tm):
    ins = [pl.BlockSpec((d, ne), lambda i: (0, 0))] * 2 + [pl.BlockSpec((1, ne), lambda i: (0, 0))]
    outs = [pl.BlockSpec((tm, 4), lambda i: (i, 0)), pl.BlockSpec((tm, 2), lambda i: (i, 0)),
            pl.BlockSpec((None, 1, ne), lambda i: (i, 0, 0))]
    shapes = [jax.ShapeDtypeStruct((m, 4), jnp.int32), jax.ShapeDtypeStruct((m, 2), F32),
              jax.ShapeDtypeStruct((m // tm, 1, ne), jnp.int32)]
    return ins, outs, shapes


def outproj_residual(x, norm_g, mods, layer, w, ys, *, g_row, mod_row, slot, tm, router=None):
    m, d = x.shape
    k = w.shape[0]
    tm = min(tm, m)
    in_specs = [
        pl.BlockSpec((tm, d), lambda i: (i, 0)),
        pl.BlockSpec((None,) + norm_g.shape[1:], lambda i: (layer, 0, 0)),
        pl.BlockSpec((None,) + mods.shape[1:], lambda i: (layer, 0, 0)),
        pl.BlockSpec((k, d), lambda i: (0, 0)),
    ] + [pl.BlockSpec((tm, k), lambda i: (i, 0))] * len(ys)
    args = [x, norm_g, mods, w, *ys]
    out_specs = [pl.BlockSpec((tm, d), lambda i: (i, 0))]
    out_shape = [jax.ShapeDtypeStruct((m, d), F32)]
    if router is not None:
        w_router, b_router = router
        ne = w_router.shape[1]
        r_in, r_out, r_shapes = _router_specs(m, d, ne, tm)
        in_specs += r_in
        args += [*_split_bf16(w_router), b_router.reshape(1, ne)]
        out_specs += r_out
        out_shape += r_shapes
    outs = pl.pallas_call(
        functools.partial(_outproj_kernel, n_y=len(ys), g_row=g_row, mod_row=mod_row, slot=slot,
                          route=router is not None),
        grid=(m // tm,),
        in_specs=in_specs,
        out_specs=out_specs,
        out_shape=out_shape,
        compiler_params=_cparams("parallel"),
        name="outproj_residual",
    )(*args)
    return outs[0] if router is None else (outs[0], tuple(outs[1:]))


def _dot_nt(a, b):
    return lax.dot_general(a, b, (((1,), (1,)), ((), ())), preferred_element_type=F32)


def _attn_kernel(sink_ref, q_ref, *rest, tq, nsub, kv_heads, group, dh, has_local):
    if has_local:
        k_refs, v_refs = rest[:nsub + 2], rest[nsub + 2:2 * nsub + 4]
    kc_ref, vc_ref, o_ref = rest[-3:]
    assert 2 * dh == LANES and tq == WINDOW and group % 2 == 0
    b = pl.program_id(0)
    nb = pl.num_programs(0)
    rows = group * tq
    lc = kc_ref.shape[0]
    head_of_row = lax.broadcasted_iota(jnp.int32, (rows, 1), 0) // tq
    lane = lax.broadcasted_iota(jnp.int32, (1, LANES), 1)
    if has_local:
        r = lax.broadcasted_iota(jnp.int32, (rows, tq), 0) % tq
        c = lax.broadcasted_iota(jnp.int32, (rows, tq), 1)
    for sub, j in [(sub, j) for sub in range(nsub) for j in range(kv_heads)]:
        qrows = slice(sub * tq, (sub + 1) * tq)
        hs = slice(j * dh, (j + 1) * dh)
        pair = slice((j // 2) * LANES, (j // 2 + 1) * LANES)
        v_low = j % 2 == 0
        keep_v = (lane < dh) if v_low else (lane >= dh)
        qg = jnp.concatenate(
            [q_ref[qrows, (j * group + g) * dh:(j * group + g + 1) * dh] for g in range(group)], axis=0)
        sink = jnp.full((rows, 1), sink_ref[j * group] * LOG2E, F32)
        for g in range(1, group):
            sink = jnp.where(head_of_row == g, sink_ref[j * group + g] * LOG2E, sink)
        if has_local:
            kp_ref, kq_ref, kn_ref = k_refs[sub:sub + 3]
            vp_ref, vq_ref, vn_ref = v_refs[sub:sub + 3]
            prev_ok = (c >= r) if sub > 0 else (c >= r) & (b > 0)
            next_ok = (c <= r) if sub < nsub - 1 else (c <= r) & (b < nb - 1)
            k_all = jnp.concatenate([kc_ref[:, hs], kp_ref[:, hs], kq_ref[:, hs], kn_ref[:, hs]], axis=0)
            v_all = jnp.concatenate([vc_ref[:, pair], vp_ref[:, pair], vq_ref[:, pair], vn_ref[:, pair]], axis=0)
            s = _dot_nt(qg, k_all)
            s = jnp.concatenate([s[:, :lc],
                                 jnp.where(prev_ok, s[:, lc:lc + tq], NEG_INF),
                                 s[:, lc + tq:lc + 2 * tq],
                                 jnp.where(next_ok, s[:, lc + 2 * tq:], NEG_INF)], axis=1)
        else:
            v_all = vc_ref[:, pair]
            s = _dot_nt(qg, kc_ref[:, hs])
        m = jnp.maximum(jnp.max(s, axis=-1, keepdims=True), sink)
        p = jnp.exp2(s - m).astype(BF16)
        v_ones = jnp.where(keep_v, v_all, jnp.ones_like(v_all))
        pv = jnp.dot(p, v_ones, preferred_element_type=F32)
        den = pltpu.roll(pv, dh, 1) + jnp.exp2(sink - m)
        o = pv / den
        for g in range(0, group, 2):
            a, bb = o[g * tq:(g + 1) * tq], o[(g + 1) * tq:(g + 2) * tq]
            if v_low:
                both = jnp.where(lane < dh, a, pltpu.roll(bb, dh, 1))
            else:
                both = jnp.where(lane < dh, pltpu.roll(a, dh, 1), bb)
            h = j * group + g
            o_ref[qrows, h * dh:(h + 2) * dh] = both.astype(o_ref.dtype)


def attention(sink, qkv, qkv_ctx, *, has_local):
    s = qkv.shape[0]
    dh = qkv.shape[1] // (ATTN_HEADS + 2 * ATTN_KV_HEADS)
    hd = ATTN_HEADS * dh
    kvd = ATTN_KV_HEADS * dh
    kcol, vcol = hd // kvd, hd // kvd + 1
    tq = WINDOW
    nb = s // tq
    nsub = min(ATTN_BLOCKS_PER_STEP, nb)
    lc = qkv_ctx.shape[0]
    in_specs = [
        pl.BlockSpec(memory_space=pltpu.SMEM),
        pl.BlockSpec((nsub * tq, hd), lambda b: (b, 0)),
    ]
    args = [sink, qkv]
    if has_local:
        for col in (kcol, vcol):
            for off in range(-1, nsub + 1):
                in_specs.append(pl.BlockSpec(
                    (tq, kvd), lambda b, col=col, off=off: (jnp.clip(nsub * b + off, 0, nb - 1), col)))
                args.append(qkv)
    in_specs += [pl.BlockSpec((lc, kvd), lambda b: (0, kcol)), pl.BlockSpec((lc, kvd), lambda b: (0, vcol))]
    args += [qkv_ctx, qkv_ctx]
    return pl.pallas_call(
        functools.partial(_attn_kernel, tq=tq, nsub=nsub, kv_heads=ATTN_KV_HEADS,
                          group=ATTN_HEADS // ATTN_KV_HEADS, dh=dh, has_local=has_local),
        grid=(nb // nsub,),
        in_specs=in_specs,
        out_specs=pl.BlockSpec((nsub * tq, hd), lambda b: (b, 0)),
        out_shape=jax.ShapeDtypeStruct((s, hd), BF16),
        compiler_params=_cparams("parallel"),
        name="attention",
    )(*args)


N_RET_IN = 9


def _ret_kernel(*refs, heads, dk, dv):
    ins = [refs[:N_RET_IN], refs[N_RET_IN:2 * N_RET_IN]]
    outs = refs[2 * N_RET_IN:2 * N_RET_IN + 4]
    state_ref = refs[-1]
    i = pl.program_id(0)

    @pl.when(i == 0)
    def _():
        for dr in range(2):
            state_ref[dr] = ins[dr][8][...]

    for dr, h in [(dr, h) for h in range(heads) for dr in range(2)]:
        q_ref, k_ref, v_ref, gate_ref, intra_ref, qdec_ref, kdec_ref, cdec_ref, _ = ins[dr]
        z_ref = outs[dr]
        q = q_ref[:, h * dk:(h + 1) * dk]
        k = k_ref[:, h * dk:(h + 1) * dk]
        v = v_ref[:, h * dv:(h + 1) * dv]
        state = state_ref[dr, h]
        sc = _dot_nt(q, k) * intra_ref[h]
        qd = (q.astype(F32) * qdec_ref[h]).astype(BF16)
        o = (jnp.dot(sc.astype(BF16), v, preferred_element_type=F32)
             + jnp.dot(qd, state.astype(BF16), preferred_element_type=F32))
        kd = (k.astype(F32) * kdec_ref[h]).astype(BF16)
        state_ref[dr, h] = state * cdec_ref[h] + lax.dot_general(
            kd, v, (((0,), (0,)), ((), ())), preferred_element_type=F32)
        mu = jnp.mean(o, axis=-1, keepdims=True)
        oc = o - mu
        var = jnp.mean(oc * oc, axis=-1, keepdims=True)
        gate = gate_ref[:, h * dv:(h + 1) * dv].astype(F32)
        z = gate * jax.nn.sigmoid(gate) * (oc * lax.rsqrt(var + EPS))
        z_ref[:, h * dv:(h + 1) * dv] = z.astype(z_ref.dtype)

    @pl.when(i == pl.num_programs(0) - 1)
    def _():
        for dr in range(2):
            outs[2 + dr][...] = state_ref[dr]


def retention_scan(proj, tables_f, tables_b, s0_f, s0_b):
    m, n = proj.shape
    d = n // 8
    heads = RET_HEADS
    dk, dv = d // heads, 2 * d // heads
    c = min(RET_CHUNK, m)
    nc = m // c
    once = lambda a: pl.BlockSpec(a.shape, lambda i: (0,) * a.ndim, pipeline_mode=pl.Buffered(1))
    in_specs, args, out_specs = [], [], []
    for reverse, tables, s0 in ((False, tables_f, s0_f), (True, tables_b, s0_b)):
        row = (lambda i: nc - 1 - i) if reverse else (lambda i: i)
        in_specs += [
            pl.BlockSpec((c, d), lambda i, row=row: (row(i), 0)),
            pl.BlockSpec((c, d), lambda i, row=row: (row(i), 1)),
            pl.BlockSpec((c, 2 * d), lambda i, row=row: (row(i), 1)),
            pl.BlockSpec((c, 2 * d), lambda i, row=row, blk=3 if reverse else 2: (row(i), blk)),
        ] + [once(a) for a in (*tables, s0)]
        args += [proj] * 4 + [*tables, s0]
        out_specs.append(pl.BlockSpec((c, 2 * d), lambda i, row=row: (row(i), 0)))
    out_specs += [pl.BlockSpec(s0_f.shape, lambda i: (0,) * s0_f.ndim)] * 2
    return pl.pallas_call(
        functools.partial(_ret_kernel, heads=heads, dk=dk, dv=dv),
        grid=(nc,),
        in_specs=in_specs,
        out_specs=out_specs,
        out_shape=[jax.ShapeDtypeStruct((m, 2 * d), BF16)] * 2 + [jax.ShapeDtypeStruct(s0_f.shape, F32)] * 2,
        scratch_shapes=[pltpu.VMEM((2,) + s0_f.shape, F32)],
        compiler_params=_cparams("arbitrary"),
        name="retention_scan",
    )(*args)


def retention_tables(log_decay_row, c, reverse):
    lg = -jnp.exp(log_decay_row.astype(F32))
    idx = jnp.arange(c, dtype=F32)
    diff = idx[:, None] - idx[None, :]
    if reverse:
        diff = -diff
    intra = jnp.where(diff >= 0, jnp.exp(lg[:, None, None] * jnp.maximum(diff, 0.0)), 0.0)
    fwd_idx = (c - 1.0 - idx) if reverse else idx
    qdec = jnp.exp(lg[:, None] * (fwd_idx + 1.0))[:, :, None]
    kdec = jnp.exp(lg[:, None] * (c - 1.0 - fwd_idx))[:, :, None]
    cdec = jnp.exp(lg * c)[:, None, None]
    return intra, qdec, kdec, cdec


def _window_sum(h_ref, cols, tmp_refs, w, tm, halo):
    rows = tm + 2 * halo
    src, src_cols, span, k = h_ref, cols, 1, 0
    while 2 * span < w:
        n = rows - 2 * span + 1
        dst = tmp_refs[k % 2]
        dst[0:n, :] = src[0:n, src_cols] + src[span:span + n, src_cols]
        src, src_cols, span, k = dst, slice(None), 2 * span, k + 1
    lo = halo - w // 2
    return src[lo:lo + tm, src_cols] + src[lo + span:lo + span + tm, src_cols]


def _pool_kernel(x_ref, xp_ref, xn_ref, g_ref, mod_ref, w_ref, ps_ref, o_ref, h_ref, tmp_a, tmp_b, *, mod_row, seq):
    tm, d = x_ref.shape
    halo = POOL_HALO
    i = pl.program_id(0)
    g0 = g_ref[0:1, :]
    shift = _mod_slice(mod_ref, mod_row, SH1, d)
    scale = _mod_slice(mod_ref, mod_row, SC1, d)
    hp = _norm_mod(xp_ref[...], g0, shift, scale)
    hn = _norm_mod(xn_ref[...], g0, shift, scale)
    h_ref[0:halo, :] = jnp.where(i > 0, hp, 0.0)
    h_ref[halo:halo + tm, :] = _norm_mod(x_ref[...], g0, shift, scale)
    h_ref[halo + tm:, :] = jnp.where(i < pl.num_programs(0) - 1, hn, 0.0)

    t = i * tm + lax.broadcasted_iota(jnp.int32, (tm, 1), 0)
    gw = d // len(POOL_WINDOWS)
    ys = []
    for g, w in enumerate(POOL_WINDOWS):
        cols = slice(g * gw, (g + 1) * gw)
        tot = _window_sum(h_ref, cols, (tmp_a, tmp_b), w, tm, halo)
        cnt = (jnp.minimum(t + w // 2, seq) - jnp.maximum(t - w // 2, 0)).astype(F32)
        dm = tot / cnt - h_ref[halo:halo + tm, cols]
        ys.append(jnp.dot(dm.astype(BF16), w_ref[g], preferred_element_type=F32))
    y = jnp.concatenate(ys, axis=1) * ps_ref[...]
    o_ref[...] = x_ref[...] + _mod_slice(mod_ref, mod_row, GA1, d) * _rms(y, g_ref[1:2, :])


def pool_layer(x, norm_g, mods, layer, w_pool, pool_scale, *, mod_row, tm):
    m, d = x.shape
    tm = min(tm, m)
    nt = m // tm
    hb = tm // POOL_HALO
    return pl.pallas_call(
        functools.partial(_pool_kernel, mod_row=mod_row, seq=m),
        grid=(nt,),
        in_specs=[
            pl.BlockSpec((tm, d), lambda i: (i, 0)),
            pl.BlockSpec((POOL_HALO, d), lambda i: (jnp.maximum(i * hb - 1, 0), 0)),
            pl.BlockSpec((POOL_HALO, d), lambda i: (jnp.minimum((i + 1) * hb, nt * hb - 1), 0)),
            pl.BlockSpec((None,) + norm_g.shape[1:], lambda i: (layer, 0, 0)),
            pl.BlockSpec((None,) + mods.shape[1:], lambda i: (layer, 0, 0)),
            pl.BlockSpec(w_pool.shape, lambda i: (0, 0, 0)),
            pl.BlockSpec((1, d), lambda i: (0, 0)),
        ],
        out_specs=pl.BlockSpec((tm, d), lambda i: (i, 0)),
        out_shape=jax.ShapeDtypeStruct((m, d), F32),
        scratch_shapes=[pltpu.VMEM((tm + 2 * POOL_HALO, d), F32)]
        + [pltpu.VMEM((tm + 2 * POOL_HALO, d // len(POOL_WINDOWS)), F32)] * 2,
        compiler_params=_cparams("parallel"),
        name="pool_layer",
    )(x, x, x, norm_g, mods, w_pool, pool_scale.reshape(1, d))


SWIGLU_SUB = 256


def _swiglu_partial(h, w1_ref, w3_ref, w2_ref, lo, hi):
    assert (hi - lo) % SWIGLU_SUB == 0
    acc = None
    for s in range(lo, hi, SWIGLU_SUB):
        cols = slice(s, s + SWIGLU_SUB)
        a = jnp.dot(h, w1_ref[:, cols], preferred_element_type=F32)
        b = jnp.dot(h, w3_ref[:, cols], preferred_element_type=F32)
        u = (a * jax.nn.sigmoid(a) * b).astype(BF16)
        p = jnp.dot(u, w2_ref[cols, :], preferred_element_type=F32)
        acc = p if acc is None else acc + p
    return acc


def _ffn_kernel(x_ref, g_ref, mod_ref, w1_ref, w3_ref, w2_ref, *rest, mod_row, n_cast, nf):
    cast_src, o_ref, cast_dst = rest[:n_cast], rest[n_cast], rest[n_cast + 1:2 * n_cast + 1]
    h_ref, acc_ref = rest[2 * n_cast + 1:]
    d = x_ref.shape[1]
    f = pl.program_id(1)
    for src, dst in zip(cast_src, cast_dst):
        dst[...] = src[...].astype(dst.dtype)

    assert nf >= 2
    n_sub = w1_ref.shape[1] // SWIGLU_SUB
    bounds = [-(-n_sub * c // nf) * SWIGLU_SUB for c in range(nf + 1)]
    for c in range(nf):
        @pl.when(f == c)
        def _(c=c):
            if c == 0:
                h = _norm_mod(x_ref[...], g_ref[2:3, :], _mod_slice(mod_ref, mod_row, SH2, d),
                              _mod_slice(mod_ref, mod_row, SC2, d)).astype(BF16)
                h_ref[...] = h
            else:
                h = h_ref[...]
            part = _swiglu_partial(h, w1_ref, w3_ref, w2_ref, bounds[c], bounds[c + 1])
            if c == 0:
                acc_ref[...] = part
            elif c < nf - 1:
                acc_ref[...] += part
            else:
                y = acc_ref[...] + part
                o_ref[...] = x_ref[...] + _mod_slice(mod_ref, mod_row, GA2, d) * _rms(y, g_ref[3:4, :])


def ffn_layer(x, norm_g, mods, layer, w1, w3, w2, *, mod_row, tm, tf, cast=None):
    m, d = x.shape
    ff = w1.shape[1]
    tm = min(tm, m)
    nf = ff // tf
    steps = (m // tm) * nf
    in_specs = [
        pl.BlockSpec((tm, d), lambda i, f: (i, 0)),
        pl.BlockSpec((None,) + norm_g.shape[1:], lambda i, f: (layer, 0, 0)),
        pl.BlockSpec((None,) + mods.shape[1:], lambda i, f: (layer, 0, 0)),
        pl.BlockSpec((d, ff), lambda i, f: (0, 0), pipeline_mode=pl.Buffered(1)),
        pl.BlockSpec((d, ff), lambda i, f: (0, 0), pipeline_mode=pl.Buffered(1)),
        pl.BlockSpec((ff, d), lambda i, f: (0, 0), pipeline_mode=pl.Buffered(1)),
    ]
    out_specs = [pl.BlockSpec((tm, d), lambda i, f: (i, 0))]
    out_shape = [jax.ShapeDtypeStruct((m, d), F32)]
    cast_arrays, cast_idx = cast if cast is not None else ((), 0)
    for a in cast_arrays:
        _, ne, r, c = a.shape
        per_expert = steps // ne
        assert per_expert * ne == steps and r % (per_expert * 16) == 0
        rows = r // per_expert
        in_specs.append(pl.BlockSpec(
            (None, None, rows, c),
            lambda i, f, pe=per_expert: (cast_idx, (i * nf + f) // pe, (i * nf + f) % pe, 0)))
        out_specs.append(pl.BlockSpec(
            (None, rows, c), lambda i, f, pe=per_expert: ((i * nf + f) // pe, (i * nf + f) % pe, 0)))
        out_shape.append(jax.ShapeDtypeStruct((ne, r, c), BF16))
    outs = pl.pallas_call(
        functools.partial(_ffn_kernel, mod_row=mod_row, n_cast=len(cast_arrays), nf=nf),
        grid=(m // tm, nf),
        in_specs=in_specs,
        out_specs=out_specs,
        out_shape=out_shape,
        scratch_shapes=[pltpu.VMEM((tm, d), BF16), pltpu.VMEM((tm, d), F32)],
        compiler_params=_cparams("arbitrary", "arbitrary"),
        name="ffn_layer",
    )(x, norm_g, mods, w1, w3, w2, *cast_arrays)
    return outs[0] if cast is None else (outs[0], tuple(outs[1:]))


RUN_ALIGN = 16
RUN_BITS = tuple(range(9, 3, -1))
EXPERT_ROWS = 512
MOE_TOKEN_TILE = 512
TAIL_BITS = tuple(range(9, 3, -1))


def _split_dma(length, src, src_row, dst, dst_row, sem, bits, *, start, src_advances=True):
    for b in bits:
        size = 1 << b
        off = (length >> (b + 1)) << (b + 1)

        @pl.when(((length >> b) & 1) == 1)
        def _():
            s_row = pl.multiple_of(src_row + off, RUN_ALIGN) if src_advances else src_row
            cp = pltpu.make_async_copy(src.at[pl.ds(s_row, size)],
                                       dst.at[pl.ds(pl.multiple_of(dst_row + off, RUN_ALIGN), size)], sem)
            if start:
                cp.start()
            else:
                cp.wait()


def _run_dmas(tile, ne, lpad_ref, loff_ref, base_ref, buf, hbm_ref, sem, *, to_hbm, start):
    for e in range(ne):
        length = lpad_ref[tile * ne + e]
        lo = loff_ref[tile * ne + e]
        gb = base_ref[tile * ne + e]
        if to_hbm:
            _split_dma(length, buf, lo, hbm_ref, gb, sem, RUN_BITS, start=start)
        else:
            _split_dma(length, hbm_ref, gb, buf, lo, sem, RUN_BITS, start=start)


def _split_bf16(a):
    hi = a.astype(BF16)
    return hi, (a - hi.astype(F32)).astype(BF16)


def _dot_f32x3(a, b_hi, b_lo):
    a_hi, a_lo = _split_bf16(a)
    dot = functools.partial(jnp.dot, preferred_element_type=F32)
    return dot(a_hi, b_hi) + (dot(a_hi, b_lo) + dot(a_lo, b_hi))


def _local_rows(ri_ref, loff_ref, tile, ne):
    e1, e2 = ri_ref[:, 0:1], ri_ref[:, 1:2]
    lo1, lo2 = ri_ref[:, 2:3], ri_ref[:, 3:4]
    for e in range(ne):
        off = loff_ref[tile * ne + e]
        lo1 = lo1 + jnp.where(e1 == e, off, 0)
        lo2 = lo2 + jnp.where(e2 == e, off, 0)
    return lo1, lo2


def _router_kernel(x_ref, g_ref, mod_ref, wr_hi_ref, wr_lo_ref, br_ref, ri_ref, rw_ref, cnt_ref, *, mod_row):
    _route_rows(x_ref[...], g_ref, mod_ref, mod_row, wr_hi_ref, wr_lo_ref, br_ref, ri_ref, rw_ref, cnt_ref)


def _route_rows(x, g_ref, mod_ref, mod_row, wr_hi_ref, wr_lo_ref, br_ref, ri_ref, rw_ref, cnt_ref):
    tm, d = x.shape
    ne = wr_hi_ref.shape[1]
    h = _norm_mod(x, g_ref[2:3, :], _mod_slice(mod_ref, mod_row, SH2, d), _mod_slice(mod_ref, mod_row, SC2, d))
    logits = _dot_f32x3(h, wr_hi_ref[...], wr_lo_ref[...]) + br_ref[...]
    lane = lax.broadcasted_iota(jnp.int32, logits.shape, 1)
    v1 = jnp.max(logits, axis=-1, keepdims=True)
    i1 = jnp.min(jnp.where(logits == v1, lane, ne), axis=-1, keepdims=True)
    rest = jnp.where(lane == i1, -jnp.inf, logits)
    v2 = jnp.max(rest, axis=-1, keepdims=True)
    i2 = jnp.min(jnp.where(rest == v2, lane, ne), axis=-1, keepdims=True)
    e2 = jnp.exp(v2 - v1)
    den = 1.0 + e2
    oh1 = (lane == i1).astype(F32)
    oh2 = (lane == i2).astype(F32)
    both = oh1 + oh2
    r = lax.broadcasted_iota(jnp.int32, (tm, tm), 0)
    c = lax.broadcasted_iota(jnp.int32, (tm, tm), 1)
    tri = jnp.where(c < r, 1.0, 0.0).astype(BF16)
    rank_all = jnp.dot(tri, both.astype(BF16), preferred_element_type=F32)
    rank1 = jnp.sum(rank_all * oh1, axis=-1, keepdims=True).astype(jnp.int32)
    rank2 = jnp.sum(rank_all * oh2, axis=-1, keepdims=True).astype(jnp.int32)
    col = lax.broadcasted_iota(jnp.int32, ri_ref.shape, 1)
    ri_ref[...] = jnp.where(col == 0, i1, jnp.where(col == 1, i2, jnp.where(col == 2, rank1, rank2)))
    colw = lax.broadcasted_iota(jnp.int32, rw_ref.shape, 1)
    rw_ref[...] = jnp.where(colw == 0, 1.0 / den, e2 / den)
    cnt_ref[...] = jnp.sum(both, axis=0, keepdims=True).astype(jnp.int32)


def moe_router(x, norm_g, mods, layer, w_router, b_router, *, mod_row, tm):
    m, d = x.shape
    ne = w_router.shape[1]
    return pl.pallas_call(
        functools.partial(_router_kernel, mod_row=mod_row),
        grid=(m // tm,),
        in_specs=[
            pl.BlockSpec((tm, d), lambda i: (i, 0)),
            pl.BlockSpec((None,) + norm_g.shape[1:], lambda i: (layer, 0, 0)),
            pl.BlockSpec((None,) + mods.shape[1:], lambda i: (layer, 0, 0)),
            pl.BlockSpec((d, ne), lambda i: (0, 0)),
            pl.BlockSpec((d, ne), lambda i: (0, 0)),
            pl.BlockSpec((1, ne), lambda i: (0, 0)),
        ],
        out_specs=[
            pl.BlockSpec((tm, 4), lambda i: (i, 0)),
            pl.BlockSpec((tm, 2), lambda i: (i, 0)),
            pl.BlockSpec((None, 1, ne), lambda i: (i, 0, 0)),
        ],
        out_shape=[
            jax.ShapeDtypeStruct((m, 4), jnp.int32),
            jax.ShapeDtypeStruct((m, 2), F32),
            jax.ShapeDtypeStruct((m // tm, 1, ne), jnp.int32),
        ],
        compiler_params=_cparams("parallel"),
        name="moe_router",
    )(x, norm_g, mods, *_split_bf16(w_router), b_router.reshape(1, ne))


def _dispatch_kernel(lpad_ref, loff_ref, base_ref, tlen_ref, tstart_ref, nu_ref, x_ref, g_ref, mod_ref, ri_ref,
                     *rest, mod_row, ne, tile0, first):
    xs_ref, cbuf, zbuf, sem = rest if first else rest[1:]
    tm, d = x_ref.shape
    rc = cbuf.shape[1]
    i = pl.program_id(0)
    last = i == pl.num_programs(0) - 1
    slot = i % 2

    def zero_tails(start):
        for e in range(ne):
            _split_dma(tlen_ref[e], zbuf, 0, xs_ref, tstart_ref[e], sem.at[2], TAIL_BITS, start=start,
                       src_advances=False)

        def unused_tile(t, carry):
            cp = pltpu.make_async_copy(
                zbuf, xs_ref.at[pl.ds(pl.multiple_of(t * EXPERT_ROWS, EXPERT_ROWS), EXPERT_ROWS)], sem.at[2])
            if start:
                cp.start()
            else:
                cp.wait()
            return carry

        lax.fori_loop(nu_ref[0], xs_ref.shape[0] // EXPERT_ROWS, unused_tile, 0)

    if first:
        @pl.when(i == 0)
        def _():
            zbuf[...] = jnp.zeros_like(zbuf)
            zero_tails(True)

    h = _norm_mod(x_ref[...], g_ref[2:3, :], _mod_slice(mod_ref, mod_row, SH2, d),
                  _mod_slice(mod_ref, mod_row, SC2, d)).astype(BF16)
    lo1, lo2 = _local_rows(ri_ref, loff_ref, i + tile0, ne)
    lane = lax.broadcasted_iota(jnp.int32, (tm, rc), 1)
    onehot = jnp.where((lane == lo1) | (lane == lo2), 1.0, 0.0).astype(BF16)
    sorted_rows = lax.dot_general(onehot, h, (((0,), (0,)), ((), ())), preferred_element_type=F32)
    cbuf[slot] = sorted_rows.astype(BF16)
    moves = functools.partial(_run_dmas, ne=ne, lpad_ref=lpad_ref, loff_ref=loff_ref, base_ref=base_ref,
                              hbm_ref=xs_ref, to_hbm=True)
    moves(i + tile0, buf=cbuf.at[slot], sem=sem.at[slot], start=True)

    @pl.when(i > 0)
    def _():
        moves(i + tile0 - 1, buf=cbuf.at[1 - slot], sem=sem.at[1 - slot], start=False)

    @pl.when(last)
    def _():
        moves(i + tile0, buf=cbuf.at[slot], sem=sem.at[slot], start=False)
        if first:
            zero_tails(False)


def moe_dispatch(x, norm_g, mods, layer, route_i, tabs, rows, xs=None, *, mod_row, tm, ne, tile0):
    m, d = x.shape
    rc = 2 * tm + ne * RUN_ALIGN
    first = xs is None
    in_specs = [
        pl.BlockSpec((tm, d), lambda i, *_: (i, 0)),
        pl.BlockSpec((None,) + norm_g.shape[1:], lambda i, *_: (layer, 0, 0)),
        pl.BlockSpec((None,) + mods.shape[1:], lambda i, *_: (layer, 0, 0)),
        pl.BlockSpec((tm, 4), lambda i, *_: (i, 0)),
    ]
    args = [*tabs, x, norm_g, mods, route_i]
    if not first:
        in_specs.append(pl.BlockSpec(memory_space=pl.ANY))
        args.append(xs)
    return pl.pallas_call(
        functools.partial(_dispatch_kernel, mod_row=mod_row, ne=ne, tile0=tile0, first=first),
        grid_spec=pltpu.PrefetchScalarGridSpec(
            num_scalar_prefetch=len(tabs),
            grid=(m // tm,),
            in_specs=in_specs,
            out_specs=pl.BlockSpec(memory_space=pl.ANY),
            scratch_shapes=[pltpu.VMEM((2, rc, d), BF16), pltpu.VMEM((EXPERT_ROWS, d), BF16),
                            pltpu.SemaphoreType.DMA((3,))],
        ),
        out_shape=jax.ShapeDtypeStruct((rows, d), BF16),
        input_output_aliases={} if first else {len(args) - 1: 0},
        compiler_params=_cparams("arbitrary"),
        name="moe_dispatch",
    )(*args)


def _experts_kernel(te_ref, nu_ref, x_ref, w1_ref, w3_ref, w2_ref, y_ref, acc_ref, *, n_chunks):
    del te_ref
    i = pl.program_id(0)
    f = pl.program_id(1)
    nf = pl.num_programs(1)
    used = i < nu_ref[0]

    for c in range(n_chunks):
        @pl.when(used & (f == c))
        def _(c=c):
            part = _swiglu_partial(x_ref[...], w1_ref, w3_ref, w2_ref, 0, w1_ref.shape[1])
            if c == 0:
                acc_ref[...] = part
            elif c < n_chunks - 1:
                acc_ref[...] += part
            else:
                y_ref[...] = (acc_ref[...] + part).astype(y_ref.dtype)

    @pl.when(jnp.logical_not(used) & (f == nf - 1))
    def _():
        y_ref[...] = jnp.zeros_like(y_ref)


def moe_experts(xs, tile_expert, n_used, w1, w3, w2, *, tm, tf):
    rows, d = xs.shape
    ff = w1.shape[2]
    nf = ff // tf
    assert nf >= 2

    def chunk(i, f, nu):
        return jnp.where(i < nu[0], f, nf - 1)

    def wmap_in(i, f, te, nu):
        return (te[i], 0, chunk(i, f, nu))

    def wmap_out(i, f, te, nu):
        return (te[i], chunk(i, f, nu), 0)

    def xmap(i, f, te, nu):
        return (jnp.maximum(jnp.minimum(i, nu[0] - 1), 0), 0)

    return pl.pallas_call(
        functools.partial(_experts_kernel, n_chunks=nf),
        grid_spec=pltpu.PrefetchScalarGridSpec(
            num_scalar_prefetch=2,
            grid=(rows // tm, nf),
            in_specs=[
                pl.BlockSpec((tm, d), xmap),
                pl.BlockSpec((None, d, tf), wmap_in),
                pl.BlockSpec((None, d, tf), wmap_in),
                pl.BlockSpec((None, tf, d), wmap_out),
            ],
            out_specs=pl.BlockSpec((tm, d), lambda i, f, te, nu: (i, 0)),
            scratch_shapes=[pltpu.VMEM((tm, d), F32)],
        ),
        out_shape=jax.ShapeDtypeStruct((rows, d), BF16),
        compiler_params=_cparams("arbitrary", "arbitrary"),
        name="moe_experts",
    )(tile_expert, n_used, xs, w1, w3, w2)


def _combine_kernel(lpad_ref, loff_ref, base_ref, x_ref, ri_ref, rw_ref, g_ref, mod_ref, ys_ref, o_ref,
                    ybuf, sem, *, mod_row, ne, tile0):
    tm, d = x_ref.shape
    rc = ybuf.shape[1]
    i = pl.program_id(0)
    slot = i % 2
    moves = functools.partial(_run_dmas, ne=ne, lpad_ref=lpad_ref, loff_ref=loff_ref, base_ref=base_ref,
                              hbm_ref=ys_ref, to_hbm=False)

    @pl.when(i == 0)
    def _():
        ybuf[...] = jnp.zeros_like(ybuf)
        moves(tile0, buf=ybuf.at[0], sem=sem.at[0], start=True)

    @pl.when(i + 1 < pl.num_programs(0))
    def _():
        moves(i + tile0 + 1, buf=ybuf.at[1 - slot], sem=sem.at[1 - slot], start=True)

    moves(i + tile0, buf=ybuf.at[slot], sem=sem.at[slot], start=False)
    lo1, lo2 = _local_rows(ri_ref, loff_ref, i + tile0, ne)
    lane = lax.broadcasted_iota(jnp.int32, (tm, rc), 1)
    rows = ybuf[slot]
    y1 = jnp.dot(jnp.where(lane == lo1, 1.0, 0.0).astype(BF16), rows, preferred_element_type=F32)
    y2 = jnp.dot(jnp.where(lane == lo2, 1.0, 0.0).astype(BF16), rows, preferred_element_type=F32)
    y = rw_ref[:, 0:1] * y1 + rw_ref[:, 1:2] * y2
    o_ref[...] = x_ref[...] + _mod_slice(mod_ref, mod_row, GA2, d) * _rms(y, g_ref[3:4, :])


def moe_combine(x, norm_g, mods, layer, route_i, route_w, ys, tabs, *, mod_row, tm, ne, tile0):
    m, d = x.shape
    rc = 2 * tm + ne * RUN_ALIGN
    return pl.pallas_call(
        functools.partial(_combine_kernel, mod_row=mod_row, ne=ne, tile0=tile0),
        grid_spec=pltpu.PrefetchScalarGridSpec(
            num_scalar_prefetch=3,
            grid=(m // tm,),
            in_specs=[
                pl.BlockSpec((tm, d), lambda i, *_: (i, 0)),
                pl.BlockSpec((tm, 4), lambda i, *_: (i, 0)),
                pl.BlockSpec((tm, 2), lambda i, *_: (i, 0)),
                pl.BlockSpec((None,) + norm_g.shape[1:], lambda i, *_: (layer, 0, 0)),
                pl.BlockSpec((None,) + mods.shape[1:], lambda i, *_: (layer, 0, 0)),
                pl.BlockSpec(memory_space=pl.ANY),
            ],
            out_specs=pl.BlockSpec((tm, d), lambda i, *_: (i, 0)),
            scratch_shapes=[pltpu.VMEM((2, rc, d), BF16), pltpu.SemaphoreType.DMA((2,))],
        ),
        out_shape=jax.ShapeDtypeStruct((m, d), F32),
        compiler_params=_cparams("arbitrary"),
        name="moe_combine",
    )(*tabs, x, route_i, route_w, norm_g, mods, ys)


def sparse_moe_layer(streams, norm_g, mods, layer, w_router, b_router, w1, w3, w2, *, tf, routed_first=None):
    ne = w_router.shape[1]
    tm_exp = EXPERT_ROWS
    routed = [moe_router(x, norm_g, mods, layer, w_router, b_router, mod_row=row, tm=tm)
              if k > 0 or routed_first is None else routed_first for k, (x, row, tm) in enumerate(streams)]
    counts = jnp.concatenate([r[2][:, 0, :] for r in routed], axis=0)
    nt = counts.shape[0]
    m_total = sum(x.shape[0] for x, _, _ in streams)
    lpad = (counts + RUN_ALIGN - 1) // RUN_ALIGN * RUN_ALIGN
    loff = jnp.cumsum(lpad, axis=1) - lpad
    group = jnp.sum(lpad, axis=0)
    gpad = (group + tm_exp - 1) // tm_exp * tm_exp
    ends = jnp.cumsum(gpad)
    starts = ends - gpad
    base = starts[None, :] + jnp.cumsum(lpad, axis=0) - lpad
    n_tiles = -(-(2 * m_total + nt * ne * (RUN_ALIGN - 1)) // tm_exp) + ne
    tile_start = jnp.arange(n_tiles, dtype=jnp.int32) * tm_exp
    n_used = (ends[-1] // tm_exp).astype(jnp.int32).reshape(1)
    tile_expert = jnp.sum(tile_start[:, None] >= ends[None, :], axis=1).astype(jnp.int32)
    tile_expert = jnp.minimum(tile_expert, tile_expert[jnp.maximum(n_used[0] - 1, 0)])
    nt_first = streams[0][0].shape[0] // streams[0][2]
    group_first = jnp.sum(lpad[:nt_first], axis=0)
    later_rows = m_total - streams[0][0].shape[0] + (nt - nt_first) * (RUN_ALIGN - 1)
    assert later_rows + tm_exp - RUN_ALIGN < 2 << TAIL_BITS[0]
    tabs = [a.reshape(-1).astype(jnp.int32)
            for a in (lpad, loff, base, gpad - group_first, starts + group_first, n_used)]
    xs, tile0 = None, 0
    for (x, row, tm), (route_i, _, _) in zip(streams, routed):
        xs = moe_dispatch(x, norm_g, mods, layer, route_i, tabs, n_tiles * tm_exp, xs,
                          mod_row=row, tm=tm, ne=ne, tile0=tile0)
        tile0 += x.shape[0] // tm
    ys = moe_experts(xs, tile_expert, n_used, w1, w3, w2, tm=tm_exp, tf=tf)
    outs, tile0 = [], 0
    for (x, row, tm), (route_i, route_w, _) in zip(streams, routed):
        outs.append(moe_combine(x, norm_g, mods, layer, route_i, route_w, ys, tabs[:3],
                                mod_row=row, tm=tm, ne=ne, tile0=tile0))
        tile0 += x.shape[0] // tm
    return outs


def _rope_tables(seq):
    n = 16
    inv = ROPE_BASE ** (-jnp.arange(n, dtype=F32) / n)
    t = jnp.arange(seq)
    row_ang = (t // GRID_W).astype(F32)[:, None] * inv[None, :]
    col_ang = (t % GRID_W).astype(F32)[:, None] * inv[None, :]
    cos = jnp.concatenate([jnp.cos(row_ang)] * 2 + [jnp.cos(col_ang)] * 2, axis=1)
    sin = jnp.concatenate([-jnp.sin(row_ang), jnp.sin(row_ang), -jnp.sin(col_ang), jnp.sin(col_ang)], axis=1)
    return jnp.tile(cos, (1, 2)), jnp.tile(sin, (1, 2))


def kernel(x, c, ctx, c_ctx, w_ada, b_ada, norm_g, attn_w_qkv, attn_w_o, attn_sink, ret_w_in, ret_w_o,
           ret_log_decay, pool_w, pool_scale, ffn_w1, ffn_w3, ffn_w2, moe_w_router, moe_b_router,
           moe_w1, moe_w3, moe_w2):
    batch, seq, d = x.shape
    assert batch == 1 and c.shape[0] == 1
    depth = w_ada.shape[0]
    lc = ctx.shape[1]
    xl = x.reshape(seq, d)
    xc = ctx.reshape(lc, d)

    cvecs = jnp.zeros((8, d), F32).at[LAT_ROW].set(c[0]).at[CTX_ROW].set(c_ctx)
    mods = ada_table(cvecs, w_ada, b_ada)
    rope = _rope_tables(seq)

    hd = attn_w_o.shape[1]
    dh = hd // ATTN_HEADS
    qkv_scale = jnp.concatenate([jnp.full((hd,), dh ** -0.5, F32),
                                 jnp.ones((attn_w_qkv.shape[2] - hd,), F32)])
    dk = d // RET_HEADS
    ret_scale = jnp.concatenate([jnp.ones((d,), F32), jnp.full((d,), dk ** -0.5, F32),
                                 jnp.ones((ret_w_in.shape[2] - 2 * d,), F32)])

    moe_bf16 = None
    for i in range(depth):
        last = i == depth - 1
        kind, j = i % N_MIXERS, i // N_MIXERS
        proj = functools.partial(mixer_projection, norm_g=norm_g, mods=mods, layer=i)
        routed_lat = None
        if i % 2 == 1:
            mix_out = functools.partial(outproj_residual, norm_g=norm_g, mods=mods, layer=i, g_row=1,
                                        mod_row=LAT_ROW, slot=GA1, tm=min(MOE_TOKEN_TILE, seq),
                                        router=(moe_w_router[i // 2], moe_b_router[i // 2]))
        else:
            mix_out = lambda x, **kw: (outproj_residual(x, norm_g, mods, i, g_row=1, mod_row=LAT_ROW,
                                                        slot=GA1, tm=1024, **kw), None)
        if kind == 0:
            w_qkv = (attn_w_qkv[j] * qkv_scale).astype(BF16)
            w_o = attn_w_o[j].astype(BF16)
            kvd = (w_qkv.shape[1] - hd) // 2
            qkv_l = proj(xl, w=w_qkv, mod_row=LAT_ROW, tm=512, chunk=256, q_cols=hd, rope=rope, rope_cols=hd + kvd)
            qkv_c = proj(xc, w=w_qkv, mod_row=CTX_ROW, tm=256, chunk=256, q_cols=hd)
            o_l = attention(attn_sink[j], qkv_l, qkv_c, has_local=True)
            xl, routed_lat = mix_out(xl, w=w_o, ys=[o_l])
            if not last:
                o_c = attention(attn_sink[j], qkv_c, qkv_c, has_local=False)
                xc = outproj_residual(xc, norm_g, mods, i, w_o, [o_c], g_row=1, mod_row=CTX_ROW, slot=GA1, tm=256)
        elif kind == 1:
            w_in = (ret_w_in[j] * ret_scale).astype(BF16)
            w_o = ret_w_o[j].astype(BF16)
            p_c = proj(xc, w=w_in, mod_row=CTX_ROW, tm=256, chunk=1024)
            p_l = proj(xl, w=w_in, mod_row=LAT_ROW, tm=512, chunk=1024)
            s0 = jnp.zeros((RET_HEADS, dk, 2 * dk), F32)
            tabs_f = retention_tables(ret_log_decay[j, 0], min(RET_CHUNK, lc), False)
            tabs_b = retention_tables(ret_log_decay[j, 1], min(RET_CHUNK, lc), True)
            zf_c, zb_c, s_f, s_b = retention_scan(p_c, tabs_f, tabs_b, s0, s0)
            zf_l, zb_l, _, _ = retention_scan(p_l, tabs_f, tabs_b, s_f, s_b)
            xl, routed_lat = mix_out(xl, w=w_o, ys=[zf_l, zb_l])
            if not last:
                xc = outproj_residual(xc, norm_g, mods, i, w_o, [zf_c, zb_c], g_row=1, mod_row=CTX_ROW, slot=GA1, tm=256)
        else:
            w_p = pool_w[j].astype(BF16)
            xl = pool_layer(xl, norm_g, mods, i, w_p, pool_scale[j], mod_row=LAT_ROW, tm=1024)
            if not last:
                xc = pool_layer(xc, norm_g, mods, i, w_p, pool_scale[j], mod_row=CTX_ROW, tm=256)

        f = i // 2
        if i % 2 == 0:
            w1, w3, w2 = ffn_w1[f].astype(BF16), ffn_w3[f].astype(BF16), ffn_w2[f].astype(BF16)
            tf = w1.shape[1] // 2
            steps = (seq // min(512, seq)) * 2
            ne = moe_w1.shape[1]
            if not last and steps % ne == 0 and d % (steps // ne * 16) == 0:
                xl, moe_bf16 = ffn_layer(xl, norm_g, mods, i, w1, w3, w2, mod_row=LAT_ROW, tm=512, tf=tf,
                                         cast=((moe_w1, moe_w3, moe_w2), (i + 1) // 2))
            else:
                xl = ffn_layer(xl, norm_g, mods, i, w1, w3, w2, mod_row=LAT_ROW, tm=512, tf=tf)
            if not last:
                xc = ffn_layer(xc, norm_g, mods, i, w1, w3, w2, mod_row=CTX_ROW, tm=256, tf=tf)
        else:
            if moe_bf16 is None:
                moe_bf16 = moe_w1[f].astype(BF16), moe_w3[f].astype(BF16), moe_w2[f].astype(BF16)
            (w1, w3, w2), moe_bf16 = moe_bf16, None
            tf = w1.shape[2] // 2
            streams = [(xl, LAT_ROW, min(MOE_TOKEN_TILE, seq))] + ([] if last else [(xc, CTX_ROW, min(256, lc))])
            outs = sparse_moe_layer(streams, norm_g, mods, i, moe_w_router[f], moe_b_router[f], w1, w3, w2, tf=tf,
                                    routed_first=routed_lat)
            xl = outs[0]
            if not last:
                xc = outs[1]
    return xl.reshape(batch, seq, d)
```

```python
import functools

import jax
import jax.numpy as jnp
from jax import lax
from jax.experimental import pallas as pl
from jax.experimental.pallas import tpu as pltpu

F32 = jnp.float32
BF16 = jnp.bfloat16

EPS = 1e-6
NEG_INF = -1e30
LOG2E = 1.4426950408889634
LANES = 128
VMEM_LIMIT = 56 * 1024 * 1024

GRID_W = 64
N_MIXERS = 3
ATTN_HEADS = 16
ATTN_KV_HEADS = 4
ATTN_BLOCKS_PER_STEP = 4
WINDOW = 128
ROPE_BASE = 10000.0
RET_HEADS = 4
RET_CHUNK = 256
POOL_WINDOWS = (2, 4, 8, 16)
POOL_HALO = 8
N_EXPERTS = 8

SH1, SC1, GA1, SH2, SC2, GA2 = range(6)
LAT_ROW, CTX_ROW = 0, 1


def _cparams(*sem):
    return pltpu.CompilerParams(dimension_semantics=sem, vmem_limit_bytes=VMEM_LIMIT)


def _rms(x, g):
    return x * lax.rsqrt(jnp.mean(x * x, axis=-1, keepdims=True) + EPS) * g


def _mod_slice(mod_ref, row, slot, d):
    return mod_ref[row:row + 1, slot * d:(slot + 1) * d]


def _norm_mod(x, g, shift, scale):
    return _rms(x, g) * (1.0 + scale) + shift


def _ada_kernel(c_ref, w_ref, b_ref, o_ref):
    c = c_ref[...]
    s = c * jax.nn.sigmoid(c)
    o_ref[...] = _dot_f32x3(s, *_split_bf16(w_ref[...])) + b_ref[...]


def ada_table(cvecs, w_ada, b_ada):
    depth, d, n = w_ada.shape
    tn = 1536
    return pl.pallas_call(
        _ada_kernel,
        grid=(depth, n // tn),
        in_specs=[
            pl.BlockSpec((8, d), lambda i, j: (0, 0)),
            pl.BlockSpec((None, d, tn), lambda i, j: (i, 0, j)),
            pl.BlockSpec((None, 1, tn), lambda i, j: (i, 0, j)),
        ],
        out_specs=pl.BlockSpec((None, 8, tn), lambda i, j: (i, 0, j)),
        out_shape=jax.ShapeDtypeStruct((depth, 8, n), F32),
        compiler_params=_cparams("parallel", "parallel"),
        name="ada_table",
    )(cvecs, w_ada, b_ada.reshape(depth, 1, n))


def _rope(a, cos, sin):
    lane = lax.broadcasted_iota(jnp.int32, (a.shape[0], LANES), 1)
    first = (lane % 32) < 16
    outs = []
    for cb in range(a.shape[1] // LANES):
        blk = a[:, cb * LANES:(cb + 1) * LANES]
        partner = jnp.where(first, pltpu.roll(blk, LANES - 16, 1), pltpu.roll(blk, 16, 1))
        outs.append(blk * cos + partner * sin)
    return jnp.concatenate(outs, axis=1)


def _mixer_proj_kernel(x_ref, g_ref, mod_ref, w_ref, *rest, mod_row, chunk, rope_cols, q_cols):
    if rope_cols:
        cos_ref, sin_ref, o_ref = rest
        cos, sin = cos_ref[...], sin_ref[...]
    else:
        (o_ref,) = rest
    d = x_ref.shape[1]
    h = _norm_mod(x_ref[...], g_ref[0:1, :], _mod_slice(mod_ref, mod_row, SH1, d),
                  _mod_slice(mod_ref, mod_row, SC1, d)).astype(BF16)
    for c0 in range(0, o_ref.shape[1], chunk):
        acc = jnp.dot(h, w_ref[:, c0:c0 + chunk], preferred_element_type=F32)
        if c0 < q_cols:
            acc = acc * LOG2E
        if c0 < rope_cols:
            acc = _rope(acc, cos, sin)
        o_ref[:, c0:c0 + chunk] = acc.astype(o_ref.dtype)


def mixer_projection(x, norm_g, mods, layer, w, *, mod_row, tm, chunk, q_cols=0, rope=None, rope_cols=0):
    m, d = x.shape
    n = w.shape[1]
    tm = min(tm, m)
    assert n % chunk == 0 and rope_cols % chunk == 0 and q_cols % chunk == 0
    in_specs = [
        pl.BlockSpec((tm, d), lambda i: (i, 0)),
        pl.BlockSpec((None,) + norm_g.shape[1:], lambda i: (layer, 0, 0)),
        pl.BlockSpec((None,) + mods.shape[1:], lambda i: (layer, 0, 0)),
        pl.BlockSpec((d, n), lambda i: (0, 0), pipeline_mode=pl.Buffered(1)),
    ]
    args = [x, norm_g, mods, w]
    if rope_cols:
        in_specs += [pl.BlockSpec((tm, LANES), lambda i: (i, 0))] * 2
        args += list(rope)
    return pl.pallas_call(
        functools.partial(_mixer_proj_kernel, mod_row=mod_row, chunk=chunk, rope_cols=rope_cols, q_cols=q_cols),
        grid=(m // tm,),
        in_specs=in_specs,
        out_specs=pl.BlockSpec((tm, n), lambda i: (i, 0)),
        out_shape=jax.ShapeDtypeStruct((m, n), BF16),
        compiler_params=_cparams("parallel"),
        name="mixer_projection",
    )(*args)


def _outproj_kernel(x_ref, g_ref, mod_ref, w_ref, *rest, n_y, g_row, mod_row, slot, route):
    y_refs, rest = rest[:n_y], rest[n_y:]
    if route:
        router_refs, o_ref, route_out = rest[:3], rest[3], rest[4:]
    else:
        (o_ref,) = rest
    d = x_ref.shape[1]
    y = y_refs[0][...]
    if n_y == 2:
        y = (y.astype(F32) + y_refs[1][...].astype(F32)).astype(BF16)
    t = jnp.dot(y, w_ref[...], preferred_element_type=F32)
    gate = _mod_slice(mod_ref, mod_row, slot, d)
    x_new = x_ref[...] + gate * _rms(t, g_ref[g_row:g_row + 1, :])
    o_ref[...] = x_new
    if route:
        _route_rows(x_new, g_ref, mod_ref, mod_row, *router_refs, *route_out)


def _router_specs(m, d, ne, tm):
    ins = [pl.BlockSpec((d, ne), lambda i: (0, 0))] * 2 + [pl.BlockSpec((1, ne), lambda i: (0, 0))]
    outs = [pl.BlockSpec((tm, 4), lambda i: (i, 0)), pl.BlockSpec((tm, 2), lambda i: (i, 0)),
            pl.BlockSpec((None, 1, ne), lambda i: (i, 0, 0))]
    shapes = [jax.ShapeDtypeStruct((m, 4), jnp.int32), jax.ShapeDtypeStruct((m, 2), F32),
              jax.ShapeDtypeStruct((m // tm, 1, ne), jnp.int32)]
    return ins, outs, shapes


def outproj_residual(x, norm_g, mods, layer, w, ys, *, g_row, mod_row, slot, tm, router=None):
    m, d = x.shape
    k = w.shape[0]
    tm = min(tm, m)
    in_specs = [
        pl.BlockSpec((tm, d), lambda i: (i, 0)),
        pl.BlockSpec((None,) + norm_g.shape[1:], lambda i: (layer, 0, 0)),
        pl.BlockSpec((None,) + mods.shape[1:], lambda i: (layer, 0, 0)),
        pl.BlockSpec((k, d), lambda i: (0, 0)),
    ] + [pl.BlockSpec((tm, k), lambda i: (i, 0))] * len(ys)
    args = [x, norm_g, mods, w, *ys]
    out_specs = [pl.BlockSpec((tm, d), lambda i: (i, 0))]
    out_shape = [jax.ShapeDtypeStruct((m, d), F32)]
    if router is not None:
        w_router, b_router = router
        ne = w_router.shape[1]
        r_in, r_out, r_shapes = _router_specs(m, d, ne, tm)
        in_specs += r_in
        args += [*_split_bf16(w_router), b_router.reshape(1, ne)]
        out_specs += r_out
        out_shape += r_shapes
    outs = pl.pallas_call(
        functools.partial(_outproj_kernel, n_y=len(ys), g_row=g_row, mod_row=mod_row, slot=slot,
                          route=router is not None),
        grid=(m // tm,),
        in_specs=in_specs,
        out_specs=out_specs,
        out_shape=out_shape,
        compiler_params=_cparams("parallel"),
        name="outproj_residual",
    )(*args)
    return outs[0] if router is None else (outs[0], tuple(outs[1:]))


def _dot_nt(a, b):
    return lax.dot_general(a, b, (((1,), (1,)), ((), ())), preferred_element_type=F32)


def _attn_kernel(sink_ref, q_ref, *rest, tq, nsub, kv_heads, group, dh, has_local):
    if has_local:
        k_refs, v_refs = rest[:nsub + 2], rest[nsub + 2:2 * nsub + 4]
    kc_ref, vc_ref, o_ref = rest[-3:]
    assert 2 * dh == LANES and tq == WINDOW and group % 2 == 0
    b = pl.program_id(0)
    nb = pl.num_programs(0)
    rows = group * tq
    lc = kc_ref.shape[0]
    head_of_row = lax.broadcasted_iota(jnp.int32, (rows, 1), 0) // tq
    lane = lax.broadcasted_iota(jnp.int32, (1, LANES), 1)
    if has_local:
        r = lax.broadcasted_iota(jnp.int32, (rows, tq), 0) % tq
        c = lax.broadcasted_iota(jnp.int32, (rows, tq), 1)
    for sub, j in [(sub, j) for sub in range(nsub) for j in range(kv_heads)]:
        qrows = slice(sub * tq, (sub + 1) * tq)
        hs = slice(j * dh, (j + 1) * dh)
        pair = slice((j // 2) * LANES, (j // 2 + 1) * LANES)
        v_low = j % 2 == 0
        keep_v = (lane < dh) if v_low else (lane >= dh)
        qg = jnp.concatenate(
            [q_ref[qrows, (j * group + g) * dh:(j * group + g + 1) * dh] for g in range(group)], axis=0)
        sink = jnp.full((rows, 1), sink_ref[j * group] * LOG2E, F32)
        for g in range(1, group):
            sink = jnp.where(head_of_row == g, sink_ref[j * group + g] * LOG2E, sink)
        if has_local:
            kp_ref, kq_ref, kn_ref = k_refs[sub:sub + 3]
            vp_ref, vq_ref, vn_ref = v_refs[sub:sub + 3]
            prev_ok = (c >= r) if sub > 0 else (c >= r) & (b > 0)
            next_ok = (c <= r) if sub < nsub - 1 else (c <= r) & (b < nb - 1)
            k_all = jnp.concatenate([kc_ref[:, hs], kp_ref[:, hs], kq_ref[:, hs], kn_ref[:, hs]], axis=0)
            v_all = jnp.concatenate([vc_ref[:, pair], vp_ref[:, pair], vq_ref[:, pair], vn_ref[:, pair]], axis=0)
            s = _dot_nt(qg, k_all)
            s = jnp.concatenate([s[:, :lc],
                                 jnp.where(prev_ok, s[:, lc:lc + tq], NEG_INF),
                                 s[:, lc + tq:lc + 2 * tq],
                                 jnp.where(next_ok, s[:, lc + 2 * tq:], NEG_INF)], axis=1)
        else:
            v_all = vc_ref[:, pair]
            s = _dot_nt(qg, kc_ref[:, hs])
        m = jnp.maximum(jnp.max(s, axis=-1, keepdims=True), sink)
        p = jnp.exp2(s - m).astype(BF16)
        v_ones = jnp.where(keep_v, v_all, jnp.ones_like(v_all))
        pv = jnp.dot(p, v_ones, preferred_element_type=F32)
        den = pltpu.roll(pv, dh, 1) + jnp.exp2(sink - m)
        o = pv / den
        for g in range(0, group, 2):
            a, bb = o[g * tq:(g + 1) * tq], o[(g + 1) * tq:(g + 2) * tq]
            if v_low:
                both = jnp.where(lane < dh, a, pltpu.roll(bb, dh, 1))
            else:
                both = jnp.where(lane < dh, pltpu.roll(a, dh, 1), bb)
            h = j * group + g
            o_ref[qrows, h * dh:(h + 2) * dh] = both.astype(o_ref.dtype)


def attention(sink, qkv, qkv_ctx, *, has_local):
    s = qkv.shape[0]
    dh = qkv.shape[1] // (ATTN_HEADS + 2 * ATTN_KV_HEADS)
    hd = ATTN_HEADS * dh
    kvd = ATTN_KV_HEADS * dh
    kcol, vcol = hd // kvd, hd // kvd + 1
    tq = WINDOW
    nb = s // tq
    nsub = min(ATTN_BLOCKS_PER_STEP, nb)
    lc = qkv_ctx.shape[0]
    in_specs = [
        pl.BlockSpec(memory_space=pltpu.SMEM),
        pl.BlockSpec((nsub * tq, hd), lambda b: (b, 0)),
    ]
    args = [sink, qkv]
    if has_local:
        for col in (kcol, vcol):
            for off in range(-1, nsub + 1):
                in_specs.append(pl.BlockSpec(
                    (tq, kvd), lambda b, col=col, off=off: (jnp.clip(nsub * b + off, 0, nb - 1), col)))
                args.append(qkv)
    in_specs += [pl.BlockSpec((lc, kvd), lambda b: (0, kcol)), pl.BlockSpec((lc, kvd), lambda b: (0, vcol))]
    args += [qkv_ctx, qkv_ctx]
    return pl.pallas_call(
        functools.partial(_attn_kernel, tq=tq, nsub=nsub, kv_heads=ATTN_KV_HEADS,
                          group=ATTN_HEADS // ATTN_KV_HEADS, dh=dh, has_local=has_local),
        grid=(nb // nsub,),
        in_specs=in_specs,
        out_specs=pl.BlockSpec((nsub * tq, hd), lambda b: (b, 0)),
        out_shape=jax.ShapeDtypeStruct((s, hd), BF16),
        compiler_params=_cparams("parallel"),
        name="attention",
    )(*args)


N_RET_IN = 9


def _ret_kernel(*refs, heads, dk, dv):
    ins = [refs[:N_RET_IN], refs[N_RET_IN:2 * N_RET_IN]]
    outs = refs[2 * N_RET_IN:2 * N_RET_IN + 4]
    state_ref = refs[-1]
    i = pl.program_id(0)

    @pl.when(i == 0)
    def _():
        for dr in range(2):
            state_ref[dr] = ins[dr][8][...]

    for dr, h in [(dr, h) for h in range(heads) for dr in range(2)]:
        q_ref, k_ref, v_ref, gate_ref, intra_ref, qdec_ref, kdec_ref, cdec_ref, _ = ins[dr]
        z_ref = outs[dr]
        q = q_ref[:, h * dk:(h + 1) * dk]
        k = k_ref[:, h * dk:(h + 1) * dk]
        v = v_ref[:, h * dv:(h + 1) * dv]
        state = state_ref[dr, h]
        sc = _dot_nt(q, k) * intra_ref[h]
        qd = (q.astype(F32) * qdec_ref[h]).astype(BF16)
        o = (jnp.dot(sc.astype(BF16), v, preferred_element_type=F32)
             + jnp.dot(qd, state.astype(BF16), preferred_element_type=F32))
        kd = (k.astype(F32) * kdec_ref[h]).astype(BF16)
        state_ref[dr, h] = state * cdec_ref[h] + lax.dot_general(
            kd, v, (((0,), (0,)), ((), ())), preferred_element_type=F32)
        mu = jnp.mean(o, axis=-1, keepdims=True)
        oc = o - mu
        var = jnp.mean(oc * oc, axis=-1, keepdims=True)
        gate = gate_ref[:, h * dv:(h + 1) * dv].astype(F32)
        z = gate * jax.nn.sigmoid(gate) * (oc * lax.rsqrt(var + EPS))
        z_ref[:, h * dv:(h + 1) * dv] = z.astype(z_ref.dtype)

    @pl.when(i == pl.num_programs(0) - 1)
    def _():
        for dr in range(2):
            outs[2 + dr][...] = state_ref[dr]


def retention_scan(proj, tables_f, tables_b, s0_f, s0_b):
    m, n = proj.shape
    d = n // 8
    heads = RET_HEADS
    dk, dv = d // heads, 2 * d // heads
    c = min(RET_CHUNK, m)
    nc = m // c
    once = lambda a: pl.BlockSpec(a.shape, lambda i: (0,) * a.ndim, pipeline_mode=pl.Buffered(1))
    in_specs, args, out_specs = [], [], []
    for reverse, tables, s0 in ((False, tables_f, s0_f), (True, tables_b, s0_b)):
        row = (lambda i: nc - 1 - i) if reverse else (lambda i: i)
        in_specs += [
            pl.BlockSpec((c, d), lambda i, row=row: (row(i), 0)),
            pl.BlockSpec((c, d), lambda i, row=row: (row(i), 1)),
            pl.BlockSpec((c, 2 * d), lambda i, row=row: (row(i), 1)),
            pl.BlockSpec((c, 2 * d), lambda i, row=row, blk=3 if reverse else 2: (row(i), blk)),
        ] + [once(a) for a in (*tables, s0)]
        args += [proj] * 4 + [*tables, s0]
        out_specs.append(pl.BlockSpec((c, 2 * d), lambda i, row=row: (row(i), 0)))
    out_specs += [pl.BlockSpec(s0_f.shape, lambda i: (0,) * s0_f.ndim)] * 2
    return pl.pallas_call(
        functools.partial(_ret_kernel, heads=heads, dk=dk, dv=dv),
        grid=(nc,),
        in_specs=in_specs,
        out_specs=out_specs,
        out_shape=[jax.ShapeDtypeStruct((m, 2 * d), BF16)] * 2 + [jax.ShapeDtypeStruct(s0_f.shape, F32)] * 2,
        scratch_shapes=[pltpu.VMEM((2,) + s0_f.shape, F32)],
        compiler_params=_cparams("arbitrary"),
        name="retention_scan",
    )(*args)


def retention_tables(log_decay_row, c, reverse):
    lg = -jnp.exp(log_decay_row.astype(F32))
    idx = jnp.arange(c, dtype=F32)
    diff = idx[:, None] - idx[None, :]
    if reverse:
        diff = -diff
    intra = jnp.where(diff >= 0, jnp.exp(lg[:, None, None] * jnp.maximum(diff, 0.0)), 0.0)
    fwd_idx = (c - 1.0 - idx) if reverse else idx
    qdec = jnp.exp(lg[:, None] * (fwd_idx + 1.0))[:, :, None]
    kdec = jnp.exp(lg[:, None] * (c - 1.0 - fwd_idx))[:, :, None]
    cdec = jnp.exp(lg * c)[:, None, None]
    return intra, qdec, kdec, cdec


def _window_sum(h_ref, cols, tmp_refs, w, tm, halo):
    rows = tm + 2 * halo
    src, src_cols, span, k = h_ref, cols, 1, 0
    while 2 * span < w:
        n = rows - 2 * span + 1
        dst = tmp_refs[k % 2]
        dst[0:n, :] = src[0:n, src_cols] + src[span:span + n, src_cols]
        src, src_cols, span, k = dst, slice(None), 2 * span, k + 1
    lo = halo - w // 2
    return src[lo:lo + tm, src_cols] + src[lo + span:lo + span + tm, src_cols]


def _pool_kernel(x_ref, xp_ref, xn_ref, g_ref, mod_ref, w_ref, ps_ref, o_ref, h_ref, tmp_a, tmp_b, *, mod_row, seq):
    tm, d = x_ref.shape
    halo = POOL_HALO
    i = pl.program_id(0)
    g0 = g_ref[0:1, :]
    shift = _mod_slice(mod_ref, mod_row, SH1, d)
    scale = _mod_slice(mod_ref, mod_row, SC1, d)
    hp = _norm_mod(xp_ref[...], g0, shift, scale)
    hn = _norm_mod(xn_ref[...], g0, shift, scale)
    h_ref[0:halo, :] = jnp.where(i > 0, hp, 0.0)
    h_ref[halo:halo + tm, :] = _norm_mod(x_ref[...], g0, shift, scale)
    h_ref[halo + tm:, :] = jnp.where(i < pl.num_programs(0) - 1, hn, 0.0)

    t = i * tm + lax.broadcasted_iota(jnp.int32, (tm, 1), 0)
    gw = d // len(POOL_WINDOWS)
    ys = []
    for g, w in enumerate(POOL_WINDOWS):
        cols = slice(g * gw, (g + 1) * gw)
        tot = _window_sum(h_ref, cols, (tmp_a, tmp_b), w, tm, halo)
        cnt = (jnp.minimum(t + w // 2, seq) - jnp.maximum(t - w // 2, 0)).astype(F32)
        dm = tot / cnt - h_ref[halo:halo + tm, cols]
        ys.append(jnp.dot(dm.astype(BF16), w_ref[g], preferred_element_type=F32))
    y = jnp.concatenate(ys, axis=1) * ps_ref[...]
    o_ref[...] = x_ref[...] + _mod_slice(mod_ref, mod_row, GA1, d) * _rms(y, g_ref[1:2, :])


def pool_layer(x, norm_g, mods, layer, w_pool, pool_scale, *, mod_row, tm):
    m, d = x.shape
    tm = min(tm, m)
    nt = m // tm
    hb = tm // POOL_HALO
    return pl.pallas_call(
        functools.partial(_pool_kernel, mod_row=mod_row, seq=m),
        grid=(nt,),
        in_specs=[
            pl.BlockSpec((tm, d), lambda i: (i, 0)),
            pl.BlockSpec((POOL_HALO, d), lambda i: (jnp.maximum(i * hb - 1, 0), 0)),
            pl.BlockSpec((POOL_HALO, d), lambda i: (jnp.minimum((i + 1) * hb, nt * hb - 1), 0)),
            pl.BlockSpec((None,) + norm_g.shape[1:], lambda i: (layer, 0, 0)),
            pl.BlockSpec((None,) + mods.shape[1:], lambda i: (layer, 0, 0)),
            pl.BlockSpec(w_pool.shape, lambda i: (0, 0, 0)),
            pl.BlockSpec((1, d), lambda i: (0, 0)),
        ],
        out_specs=pl.BlockSpec((tm, d), lambda i: (i, 0)),
        out_shape=jax.ShapeDtypeStruct((m, d), F32),
        scratch_shapes=[pltpu.VMEM((tm + 2 * POOL_HALO, d), F32)]
        + [pltpu.VMEM((tm + 2 * POOL_HALO, d // len(POOL_WINDOWS)), F32)] * 2,
        compiler_params=_cparams("parallel"),
        name="pool_layer",
    )(x, x, x, norm_g, mods, w_pool, pool_scale.reshape(1, d))


SWIGLU_SUB = 256


def _swiglu_partial(h, w1_ref, w3_ref, w2_ref, lo, hi):
    assert (hi - lo) % SWIGLU_SUB == 0
    acc = None
    for s in range(lo, hi, SWIGLU_SUB):
        cols = slice(s, s + SWIGLU_SUB)
        a = jnp.dot(h, w1_ref[:, cols], preferred_element_type=F32)
        b = jnp.dot(h, w3_ref[:, cols], preferred_element_type=F32)
        u = (a * jax.nn.sigmoid(a) * b).astype(BF16)
        p = jnp.dot(u, w2_ref[cols, :], preferred_element_type=F32)
        acc = p if acc is None else acc + p
    return acc


def _ffn_kernel(x_ref, g_ref, mod_ref, w1_ref, w3_ref, w2_ref, *rest, mod_row, n_cast, nf):
    cast_src, o_ref, cast_dst = rest[:n_cast], rest[n_cast], rest[n_cast + 1:2 * n_cast + 1]
    h_ref, acc_ref = rest[2 * n_cast + 1:]
    d = x_ref.shape[1]
    f = pl.program_id(1)
    for src, dst in zip(cast_src, cast_dst):
        dst[...] = src[...].astype(dst.dtype)

    assert nf >= 2
    n_sub = w1_ref.shape[1] // SWIGLU_SUB
    bounds = [-(-n_sub * c // nf) * SWIGLU_SUB for c in range(nf + 1)]
    for c in range(nf):
        @pl.when(f == c)
        def _(c=c):
            if c == 0:
                h = _norm_mod(x_ref[...], g_ref[2:3, :], _mod_slice(mod_ref, mod_row, SH2, d),
                              _mod_slice(mod_ref, mod_row, SC2, d)).astype(BF16)
                h_ref[...] = h
            else:
                h = h_ref[...]
            part = _swiglu_partial(h, w1_ref, w3_ref, w2_ref, bounds[c], bounds[c + 1])
            if c == 0:
                acc_ref[...] = part
            elif c < nf - 1:
                acc_ref[...] += part
            else:
                y = acc_ref[...] + part
                o_ref[...] = x_ref[...] + _mod_slice(mod_ref, mod_row, GA2, d) * _rms(y, g_ref[3:4, :])


def ffn_layer(x, norm_g, mods, layer, w1, w3, w2, *, mod_row, tm, tf, cast=None):
    m, d = x.shape
    ff = w1.shape[1]
    tm = min(tm, m)
    nf = ff // tf
    steps = (m // tm) * nf
    in_specs = [
        pl.BlockSpec((tm, d), lambda i, f: (i, 0)),
        pl.BlockSpec((None,) + norm_g.shape[1:], lambda i, f: (layer, 0, 0)),
        pl.BlockSpec((None,) + mods.shape[1:], lambda i, f: (layer, 0, 0)),
        pl.BlockSpec((d, ff), lambda i, f: (0, 0), pipeline_mode=pl.Buffered(1)),
        pl.BlockSpec((d, ff), lambda i, f: (0, 0), pipeline_mode=pl.Buffered(1)),
        pl.BlockSpec((ff, d), lambda i, f: (0, 0), pipeline_mode=pl.Buffered(1)),
    ]
    out_specs = [pl.BlockSpec((tm, d), lambda i, f: (i, 0))]
    out_shape = [jax.ShapeDtypeStruct((m, d), F32)]
    cast_arrays, cast_idx = cast if cast is not None else ((), 0)
    for a in cast_arrays:
        _, ne, r, c = a.shape
        per_expert = steps // ne
        assert per_expert * ne == steps and r % (per_expert * 16) == 0
        rows = r // per_expert
        in_specs.append(pl.BlockSpec(
            (None, None, rows, c),
            lambda i, f, pe=per_expert: (cast_idx, (i * nf + f) // pe, (i * nf + f) % pe, 0)))
        out_specs.append(pl.BlockSpec(
            (None, rows, c), lambda i, f, pe=per_expert: ((i * nf + f) // pe, (i * nf + f) % pe, 0)))
        out_shape.append(jax.ShapeDtypeStruct((ne, r, c), BF16))
    outs = pl.pallas_call(
        functools.partial(_ffn_kernel, mod_row=mod_row, n_cast=len(cast_arrays), nf=nf),
        grid=(m // tm, nf),
        in_specs=in_specs,
        out_specs=out_specs,
        out_shape=out_shape,
        scratch_shapes=[pltpu.VMEM((tm, d), BF16), pltpu.VMEM((tm, d), F32)],
        compiler_params=_cparams("arbitrary", "arbitrary"),
        name="ffn_layer",
    )(x, norm_g, mods, w1, w3, w2, *cast_arrays)
    return outs[0] if cast is None else (outs[0], tuple(outs[1:]))


RUN_ALIGN = 16
RUN_BITS = tuple(range(9, 3, -1))
EXPERT_ROWS = 512
MOE_TOKEN_TILE = 512
TAIL_BITS = tuple(range(9, 3, -1))


def _split_dma(length, src, src_row, dst, dst_row, sem, bits, *, start, src_advances=True):
    for b in bits:
        size = 1 << b
        off = (length >> (b + 1)) << (b + 1)

        @pl.when(((length >> b) & 1) == 1)
        def _():
            s_row = pl.multiple_of(src_row + off, RUN_ALIGN) if src_advances else src_row
            cp = pltpu.make_async_copy(src.at[pl.ds(s_row, size)],
                                       dst.at[pl.ds(pl.multiple_of(dst_row + off, RUN_ALIGN), size)], sem)
            if start:
                cp.start()
            else:
                cp.wait()


def _run_dmas(tile, ne, lpad_ref, loff_ref, base_ref, buf, hbm_ref, sem, *, to_hbm, start):
    for e in range(ne):
        length = lpad_ref[tile * ne + e]
        lo = loff_ref[tile * ne + e]
        gb = base_ref[tile * ne + e]
        if to_hbm:
            _split_dma(length, buf, lo, hbm_ref, gb, sem, RUN_BITS, start=start)
        else:
            _split_dma(length, hbm_ref, gb, buf, lo, sem, RUN_BITS, start=start)


def _split_bf16(a):
    hi = a.astype(BF16)
    return hi, (a - hi.astype(F32)).astype(BF16)


def _dot_f32x3(a, b_hi, b_lo):
    a_hi, a_lo = _split_bf16(a)
    dot = functools.partial(jnp.dot, preferred_element_type=F32)
    return dot(a_hi, b_hi) + (dot(a_hi, b_lo) + dot(a_lo, b_hi))


def _local_rows(ri_ref, loff_ref, tile, ne):
    e1, e2 = ri_ref[:, 0:1], ri_ref[:, 1:2]
    lo1, lo2 = ri_ref[:, 2:3], ri_ref[:, 3:4]
    for e in range(ne):
        off = loff_ref[tile * ne + e]
        lo1 = lo1 + jnp.where(e1 == e, off, 0)
        lo2 = lo2 + jnp.where(e2 == e, off, 0)
    return lo1, lo2


def _router_kernel(x_ref, g_ref, mod_ref, wr_hi_ref, wr_lo_ref, br_ref, ri_ref, rw_ref, cnt_ref, *, mod_row):
    _route_rows(x_ref[...], g_ref, mod_ref, mod_row, wr_hi_ref, wr_lo_ref, br_ref, ri_ref, rw_ref, cnt_ref)


def _route_rows(x, g_ref, mod_ref, mod_row, wr_hi_ref, wr_lo_ref, br_ref, ri_ref, rw_ref, cnt_ref):
    tm, d = x.shape
    ne = wr_hi_ref.shape[1]
    h = _norm_mod(x, g_ref[2:3, :], _mod_slice(mod_ref, mod_row, SH2, d), _mod_slice(mod_ref, mod_row, SC2, d))
    logits = _dot_f32x3(h, wr_hi_ref[...], wr_lo_ref[...]) + br_ref[...]
    lane = lax.broadcasted_iota(jnp.int32, logits.shape, 1)
    v1 = jnp.max(logits, axis=-1, keepdims=True)
    i1 = jnp.min(jnp.where(logits == v1, lane, ne), axis=-1, keepdims=True)
    rest = jnp.where(lane == i1, -jnp.inf, logits)
    v2 = jnp.max(rest, axis=-1, keepdims=True)
    i2 = jnp.min(jnp.where(rest == v2, lane, ne), axis=-1, keepdims=True)
    e2 = jnp.exp(v2 - v1)
    den = 1.0 + e2
    oh1 = (lane == i1).astype(F32)
    oh2 = (lane == i2).astype(F32)
    both = oh1 + oh2
    r = lax.broadcasted_iota(jnp.int32, (tm, tm), 0)
    c = lax.broadcasted_iota(jnp.int32, (tm, tm), 1)
    tri = jnp.where(c < r, 1.0, 0.0).astype(BF16)
    rank_all = jnp.dot(tri, both.astype(BF16), preferred_element_type=F32)
    rank1 = jnp.sum(rank_all * oh1, axis=-1, keepdims=True).astype(jnp.int32)
    rank2 = jnp.sum(rank_all * oh2, axis=-1, keepdims=True).astype(jnp.int32)
    col = lax.broadcasted_iota(jnp.int32, ri_ref.shape, 1)
    ri_ref[...] = jnp.where(col == 0, i1, jnp.where(col == 1, i2, jnp.where(col == 2, rank1, rank2)))
    colw = lax.broadcasted_iota(jnp.int32, rw_ref.shape, 1)
    rw_ref[...] = jnp.where(colw == 0, 1.0 / den, e2 / den)
    cnt_ref[...] = jnp.sum(both, axis=0, keepdims=True).astype(jnp.int32)


def moe_router(x, norm_g, mods, layer, w_router, b_router, *, mod_row, tm):
    m, d = x.shape
    ne = w_router.shape[1]
    return pl.pallas_call(
        functools.partial(_router_kernel, mod_row=mod_row),
        grid=(m // tm,),
        in_specs=[
            pl.BlockSpec((tm, d), lambda i: (i, 0)),
            pl.BlockSpec((None,) + norm_g.shape[1:], lambda i: (layer, 0, 0)),
            pl.BlockSpec((None,) + mods.shape[1:], lambda i: (layer, 0, 0)),
            pl.BlockSpec((d, ne), lambda i: (0, 0)),
            pl.BlockSpec((d, ne), lambda i: (0, 0)),
            pl.BlockSpec((1, ne), lambda i: (0, 0)),
        ],
        out_specs=[
            pl.BlockSpec((tm, 4), lambda i: (i, 0)),
            pl.BlockSpec((tm, 2), lambda i: (i, 0)),
            pl.BlockSpec((None, 1, ne), lambda i: (i, 0, 0)),
        ],
        out_shape=[
            jax.ShapeDtypeStruct((m, 4), jnp.int32),
            jax.ShapeDtypeStruct((m, 2), F32),
            jax.ShapeDtypeStruct((m // tm, 1, ne), jnp.int32),
        ],
        compiler_params=_cparams("parallel"),
        name="moe_router",
    )(x, norm_g, mods, *_split_bf16(w_router), b_router.reshape(1, ne))


def _dispatch_kernel(lpad_ref, loff_ref, base_ref, tlen_ref, tstart_ref, nu_ref, x_ref, g_ref, mod_ref, ri_ref,
                     *rest, mod_row, ne, tile0, first):
    xs_ref, cbuf, zbuf, sem = rest if first else rest[1:]
    tm, d = x_ref.shape
    rc = cbuf.shape[1]
    i = pl.program_id(0)
    last = i == pl.num_programs(0) - 1
    slot = i % 2

    def zero_tails(start):
        for e in range(ne):
            _split_dma(tlen_ref[e], zbuf, 0, xs_ref, tstart_ref[e], sem.at[2], TAIL_BITS, start=start,
                       src_advances=False)

        def unused_tile(t, carry):
            cp = pltpu.make_async_copy(
                zbuf, xs_ref.at[pl.ds(pl.multiple_of(t * EXPERT_ROWS, EXPERT_ROWS), EXPERT_ROWS)], sem.at[2])
            if start:
                cp.start()
            else:
                cp.wait()
            return carry

        lax.fori_loop(nu_ref[0], xs_ref.shape[0] // EXPERT_ROWS, unused_tile, 0)

    if first:
        @pl.when(i == 0)
        def _():
            zbuf[...] = jnp.zeros_like(zbuf)
            zero_tails(True)

    h = _norm_mod(x_ref[...], g_ref[2:3, :], _mod_slice(mod_ref, mod_row, SH2, d),
                  _mod_slice(mod_ref, mod_row, SC2, d)).astype(BF16)
    lo1, lo2 = _local_rows(ri_ref, loff_ref, i + tile0, ne)
    lane = lax.broadcasted_iota(jnp.int32, (tm, rc), 1)
    onehot = jnp.where((lane == lo1) | (lane == lo2), 1.0, 0.0).astype(BF16)
    sorted_rows = lax.dot_general(onehot, h, (((0,), (0,)), ((), ())), preferred_element_type=F32)
    cbuf[slot] = sorted_rows.astype(BF16)
    moves = functools.partial(_run_dmas, ne=ne, lpad_ref=lpad_ref, loff_ref=loff_ref, base_ref=base_ref,
                              hbm_ref=xs_ref, to_hbm=True)
    moves(i + tile0, buf=cbuf.at[slot], sem=sem.at[slot], start=True)

    @pl.when(i > 0)
    def _():
        moves(i + tile0 - 1, buf=cbuf.at[1 - slot], sem=sem.at[1 - slot], start=False)

    @pl.when(last)
    def _():
        moves(i + tile0, buf=cbuf.at[slot], sem=sem.at[slot], start=False)
        if first:
            zero_tails(False)


def moe_dispatch(x, norm_g, mods, layer, route_i, tabs, rows, xs=None, *, mod_row, tm, ne, tile0):
    m, d = x.shape
    rc = 2 * tm + ne * RUN_ALIGN
    first = xs is None
    in_specs = [
        pl.BlockSpec((tm, d), lambda i, *_: (i, 0)),
        pl.BlockSpec((None,) + norm_g.shape[1:], lambda i, *_: (layer, 0, 0)),
        pl.BlockSpec((None,) + mods.shape[1:], lambda i, *_: (layer, 0, 0)),
        pl.BlockSpec((tm, 4), lambda i, *_: (i, 0)),
    ]
    args = [*tabs, x, norm_g, mods, route_i]
    if not first:
        in_specs.append(pl.BlockSpec(memory_space=pl.ANY))
        args.append(xs)
    return pl.pallas_call(
        functools.partial(_dispatch_kernel, mod_row=mod_row, ne=ne, tile0=tile0, first=first),
        grid_spec=pltpu.PrefetchScalarGridSpec(
            num_scalar_prefetch=len(tabs),
            grid=(m // tm,),
            in_specs=in_specs,
            out_specs=pl.BlockSpec(memory_space=pl.ANY),
            scratch_shapes=[pltpu.VMEM((2, rc, d), BF16), pltpu.VMEM((EXPERT_ROWS, d), BF16),
                            pltpu.SemaphoreType.DMA((3,))],
        ),
        out_shape=jax.ShapeDtypeStruct((rows, d), BF16),
        input_output_aliases={} if first else {len(args) - 1: 0},
        compiler_params=_cparams("arbitrary"),
        name="moe_dispatch",
    )(*args)


EXPERT_ROW_STEP = 128


def _experts_kernel(te_ref, nu_ref, tr_ref, x_ref, w1_ref, w3_ref, w2_ref, y_ref, acc_ref, *, n_chunks):
    del te_ref, nu_ref
    i = pl.program_id(0)
    f = pl.program_id(1)
    nf = pl.num_programs(1)
    tm = x_ref.shape[0]
    live_rows = tr_ref[i]
    steps = (live_rows + EXPERT_ROW_STEP - 1) // EXPERT_ROW_STEP

    for c in range(n_chunks):
        for q in range(1, tm // EXPERT_ROW_STEP + 1):
            @pl.when((f == c) & (steps == q))
            def _(c=c, rows=q * EXPERT_ROW_STEP):
                part = _swiglu_partial(x_ref[0:rows, :], w1_ref, w3_ref, w2_ref, 0, w1_ref.shape[1])
                if c == 0:
                    acc_ref[0:rows, :] = part
                elif c < n_chunks - 1:
                    acc_ref[0:rows, :] += part
                else:
                    y_ref[0:rows, :] = (acc_ref[0:rows, :] + part).astype(y_ref.dtype)
                    if rows < tm:
                        y_ref[rows:, :] = jnp.zeros((tm - rows, y_ref.shape[1]), y_ref.dtype)

    @pl.when((steps == 0) & (f == nf - 1))
    def _():
        y_ref[...] = jnp.zeros_like(y_ref)


def moe_experts(xs, tile_expert, n_used, tile_rows, w1, w3, w2, *, tm, tf):
    rows, d = xs.shape
    ff = w1.shape[2]
    nf = ff // tf
    assert nf >= 2

    def chunk(i, f, nu):
        return jnp.where(i < nu[0], f, nf - 1)

    def wmap_in(i, f, te, nu, tr):
        return (te[i], 0, chunk(i, f, nu))

    def wmap_out(i, f, te, nu, tr):
        return (te[i], chunk(i, f, nu), 0)

    def xmap(i, f, te, nu, tr):
        return (jnp.maximum(jnp.minimum(i, nu[0] - 1), 0), 0)

    return pl.pallas_call(
        functools.partial(_experts_kernel, n_chunks=nf),
        grid_spec=pltpu.PrefetchScalarGridSpec(
            num_scalar_prefetch=3,
            grid=(rows // tm, nf),
            in_specs=[
                pl.BlockSpec((tm, d), xmap),
                pl.BlockSpec((None, d, tf), wmap_in),
                pl.BlockSpec((None, d, tf), wmap_in),
                pl.BlockSpec((None, tf, d), wmap_out),
            ],
            out_specs=pl.BlockSpec((tm, d), lambda i, f, te, nu, tr: (i, 0)),
            scratch_shapes=[pltpu.VMEM((tm, d), F32)],
        ),
        out_shape=jax.ShapeDtypeStruct((rows, d), BF16),
        compiler_params=_cparams("arbitrary", "arbitrary"),
        name="moe_experts",
    )(tile_expert, n_used, tile_rows, xs, w1, w3, w2)


def _combine_kernel(lpad_ref, loff_ref, base_ref, x_ref, ri_ref, rw_ref, g_ref, mod_ref, ys_ref, o_ref,
                    ybuf, sem, *, mod_row, ne, tile0):
    tm, d = x_ref.shape
    rc = ybuf.shape[1]
    i = pl.program_id(0)
    slot = i % 2
    moves = functools.partial(_run_dmas, ne=ne, lpad_ref=lpad_ref, loff_ref=loff_ref, base_ref=base_ref,
                              hbm_ref=ys_ref, to_hbm=False)

    @pl.when(i == 0)
    def _():
        ybuf[...] = jnp.zeros_like(ybuf)
        moves(tile0, buf=ybuf.at[0], sem=sem.at[0], start=True)

    @pl.when(i + 1 < pl.num_programs(0))
    def _():
        moves(i + tile0 + 1, buf=ybuf.at[1 - slot], sem=sem.at[1 - slot], start=True)

    moves(i + tile0, buf=ybuf.at[slot], sem=sem.at[slot], start=False)
    lo1, lo2 = _local_rows(ri_ref, loff_ref, i + tile0, ne)
    lane = lax.broadcasted_iota(jnp.int32, (tm, rc), 1)
    rows = ybuf[slot]
    y1 = jnp.dot(jnp.where(lane == lo1, 1.0, 0.0).astype(BF16), rows, preferred_element_type=F32)
    y2 = jnp.dot(jnp.where(lane == lo2, 1.0, 0.0).astype(BF16), rows, preferred_element_type=F32)
    y = rw_ref[:, 0:1] * y1 + rw_ref[:, 1:2] * y2
    o_ref[...] = x_ref[...] + _mod_slice(mod_ref, mod_row, GA2, d) * _rms(y, g_ref[3:4, :])


def moe_combine(x, norm_g, mods, layer, route_i, route_w, ys, tabs, *, mod_row, tm, ne, tile0):
    m, d = x.shape
    rc = 2 * tm + ne * RUN_ALIGN
    return pl.pallas_call(
        functools.partial(_combine_kernel, mod_row=mod_row, ne=ne, tile0=tile0),
        grid_spec=pltpu.PrefetchScalarGridSpec(
            num_scalar_prefetch=3,
            grid=(m // tm,),
            in_specs=[
                pl.BlockSpec((tm, d), lambda i, *_: (i, 0)),
                pl.BlockSpec((tm, 4), lambda i, *_: (i, 0)),
                pl.BlockSpec((tm, 2), lambda i, *_: (i, 0)),
                pl.BlockSpec((None,) + norm_g.shape[1:], lambda i, *_: (layer, 0, 0)),
                pl.BlockSpec((None,) + mods.shape[1:], lambda i, *_: (layer, 0, 0)),
                pl.BlockSpec(memory_space=pl.ANY),
            ],
            out_specs=pl.BlockSpec((tm, d), lambda i, *_: (i, 0)),
            scratch_shapes=[pltpu.VMEM((2, rc, d), BF16), pltpu.SemaphoreType.DMA((2,))],
        ),
        out_shape=jax.ShapeDtypeStruct((m, d), F32),
        compiler_params=_cparams("arbitrary"),
        name="moe_combine",
    )(*tabs, x, route_i, route_w, norm_g, mods, ys)


def sparse_moe_layer(streams, norm_g, mods, layer, w_router, b_router, w1, w3, w2, *, tf, routed_first=None):
    ne = w_router.shape[1]
    tm_exp = EXPERT_ROWS
    routed = [moe_router(x, norm_g, mods, layer, w_router, b_router, mod_row=row, tm=tm)
              if k > 0 or routed_first is None else routed_first for k, (x, row, tm) in enumerate(streams)]
    counts = jnp.concatenate([r[2][:, 0, :] for r in routed], axis=0)
    nt = counts.shape[0]
    m_total = sum(x.shape[0] for x, _, _ in streams)
    lpad = (counts + RUN_ALIGN - 1) // RUN_ALIGN * RUN_ALIGN
    loff = jnp.cumsum(lpad, axis=1) - lpad
    group = jnp.sum(lpad, axis=0)
    gpad = (group + tm_exp - 1) // tm_exp * tm_exp
    ends = jnp.cumsum(gpad)
    starts = ends - gpad
    base = starts[None, :] + jnp.cumsum(lpad, axis=0) - lpad
    n_tiles = -(-(2 * m_total + nt * ne * (RUN_ALIGN - 1)) // tm_exp) + ne
    tile_start = jnp.arange(n_tiles, dtype=jnp.int32) * tm_exp
    n_used = (ends[-1] // tm_exp).astype(jnp.int32).reshape(1)
    tile_expert = jnp.sum(tile_start[:, None] >= ends[None, :], axis=1).astype(jnp.int32)
    tile_expert = jnp.minimum(tile_expert, tile_expert[jnp.maximum(n_used[0] - 1, 0)])
    tile_rows = jnp.clip((starts + group)[tile_expert] - tile_start, 0, tm_exp).astype(jnp.int32)
    nt_first = streams[0][0].shape[0] // streams[0][2]
    group_first = jnp.sum(lpad[:nt_first], axis=0)
    later_rows = m_total - streams[0][0].shape[0] + (nt - nt_first) * (RUN_ALIGN - 1)
    assert later_rows + tm_exp - RUN_ALIGN < 2 << TAIL_BITS[0]
    tabs = [a.reshape(-1).astype(jnp.int32)
            for a in (lpad, loff, base, gpad - group_first, starts + group_first, n_used)]
    xs, tile0 = None, 0
    for (x, row, tm), (route_i, _, _) in zip(streams, routed):
        xs = moe_dispatch(x, norm_g, mods, layer, route_i, tabs, n_tiles * tm_exp, xs,
                          mod_row=row, tm=tm, ne=ne, tile0=tile0)
        tile0 += x.shape[0] // tm
    ys = moe_experts(xs, tile_expert, n_used, tile_rows, w1, w3, w2, tm=tm_exp, tf=tf)
    outs, tile0 = [], 0
    for (x, row, tm), (route_i, route_w, _) in zip(streams, routed):
        outs.append(moe_combine(x, norm_g, mods, layer, route_i, route_w, ys, tabs[:3],
                                mod_row=row, tm=tm, ne=ne, tile0=tile0))
        tile0 += x.shape[0] // tm
    return outs


def _rope_tables(seq):
    n = 16
    inv = ROPE_BASE ** (-jnp.arange(n, dtype=F32) / n)
    t = jnp.arange(seq)
    row_ang = (t // GRID_W).astype(F32)[:, None] * inv[None, :]
    col_ang = (t % GRID_W).astype(F32)[:, None] * inv[None, :]
    cos = jnp.concatenate([jnp.cos(row_ang)] * 2 + [jnp.cos(col_ang)] * 2, axis=1)
    sin = jnp.concatenate([-jnp.sin(row_ang), jnp.sin(row_ang), -jnp.sin(col_ang), jnp.sin(col_ang)], axis=1)
    return jnp.tile(cos, (1, 2)), jnp.tile(sin, (1, 2))


def kernel(x, c, ctx, c_ctx, w_ada, b_ada, norm_g, attn_w_qkv, attn_w_o, attn_sink, ret_w_in, ret_w_o,
           ret_log_decay, pool_w, pool_scale, ffn_w1, ffn_w3, ffn_w2, moe_w_router, moe_b_router,
           moe_w1, moe_w3, moe_w2):
    batch, seq, d = x.shape
    assert batch == 1 and c.shape[0] == 1
    depth = w_ada.shape[0]
    lc = ctx.shape[1]
    xl = x.reshape(seq, d)
    xc = ctx.reshape(lc, d)

    cvecs = jnp.zeros((8, d), F32).at[LAT_ROW].set(c[0]).at[CTX_ROW].set(c_ctx)
    mods = ada_table(cvecs, w_ada, b_ada)
    rope = _rope_tables(seq)

    hd = attn_w_o.shape[1]
    dh = hd // ATTN_HEADS
    qkv_scale = jnp.concatenate([jnp.full((hd,), dh ** -0.5, F32),
                                 jnp.ones((attn_w_qkv.shape[2] - hd,), F32)])
    dk = d // RET_HEADS
    ret_scale = jnp.concatenate([jnp.ones((d,), F32), jnp.full((d,), dk ** -0.5, F32),
                                 jnp.ones((ret_w_in.shape[2] - 2 * d,), F32)])

    moe_bf16 = None
    for i in range(depth):
        last = i == depth - 1
        kind, j = i % N_MIXERS, i // N_MIXERS
        proj = functools.partial(mixer_projection, norm_g=norm_g, mods=mods, layer=i)
        routed_lat = None
        if i % 2 == 1:
            mix_out = functools.partial(outproj_residual, norm_g=norm_g, mods=mods, layer=i, g_row=1,
                                        mod_row=LAT_ROW, slot=GA1, tm=min(MOE_TOKEN_TILE, seq),
                                        router=(moe_w_router[i // 2], moe_b_router[i // 2]))
        else:
            mix_out = lambda x, **kw: (outproj_residual(x, norm_g, mods, i, g_row=1, mod_row=LAT_ROW,
                                                        slot=GA1, tm=1024, **kw), None)
        if kind == 0:
            w_qkv = (attn_w_qkv[j] * qkv_scale).astype(BF16)
            w_o = attn_w_o[j].astype(BF16)
            kvd = (w_qkv.shape[1] - hd) // 2
            qkv_l = proj(xl, w=w_qkv, mod_row=LAT_ROW, tm=512, chunk=256, q_cols=hd, rope=rope, rope_cols=hd + kvd)
            qkv_c = proj(xc, w=w_qkv, mod_row=CTX_ROW, tm=256, chunk=256, q_cols=hd)
            o_l = attention(attn_sink[j], qkv_l, qkv_c, has_local=True)
            xl, routed_lat = mix_out(xl, w=w_o, ys=[o_l])
            if not last:
                o_c = attention(attn_sink[j], qkv_c, qkv_c, has_local=False)
                xc = outproj_residual(xc, norm_g, mods, i, w_o, [o_c], g_row=1, mod_row=CTX_ROW, slot=GA1, tm=256)
        elif kind == 1:
            w_in = (ret_w_in[j] * ret_scale).astype(BF16)
            w_o = ret_w_o[j].astype(BF16)
            p_c = proj(xc, w=w_in, mod_row=CTX_ROW, tm=256, chunk=1024)
            p_l = proj(xl, w=w_in, mod_row=LAT_ROW, tm=512, chunk=1024)
            s0 = jnp.zeros((RET_HEADS, dk, 2 * dk), F32)
            tabs_f = retention_tables(ret_log_decay[j, 0], min(RET_CHUNK, lc), False)
            tabs_b = retention_tables(ret_log_decay[j, 1], min(RET_CHUNK, lc), True)
            zf_c, zb_c, s_f, s_b = retention_scan(p_c, tabs_f, tabs_b, s0, s0)
            zf_l, zb_l, _, _ = retention_scan(p_l, tabs_f, tabs_b, s_f, s_b)
            xl, routed_lat = mix_out(xl, w=w_o, ys=[zf_l, zb_l])
            if not last:
                xc = outproj_residual(xc, norm_g, mods, i, w_o, [zf_c, zb_c], g_row=1, mod_row=CTX_ROW, slot=GA1, tm=256)
        else:
            w_p = pool_w[j].astype(BF16)
            xl = pool_layer(xl, norm_g, mods, i, w_p, pool_scale[j], mod_row=LAT_ROW, tm=1024)
            if not last:
                xc = pool_layer(xc, norm_g, mods, i, w_p, pool_scale[j], mod_row=CTX_ROW, tm=256)

        f = i // 2
        if i % 2 == 0:
            w1, w3, w2 = ffn_w1[f].astype(BF16), ffn_w3[f].astype(BF16), ffn_w2[f].astype(BF16)
            tf = w1.shape[1] // 2
            steps = (seq // min(512, seq)) * 2
            ne = moe_w1.shape[1]
            if not last and steps % ne == 0 and d % (steps // ne * 16) == 0:
                xl, moe_bf16 = ffn_layer(xl, norm_g, mods, i, w1, w3, w2, mod_row=LAT_ROW, tm=512, tf=tf,
                                         cast=((moe_w1, moe_w3, moe_w2), (i + 1) // 2))
            else:
                xl = ffn_layer(xl, norm_g, mods, i, w1, w3, w2, mod_row=LAT_ROW, tm=512, tf=tf)
            if not last:
                xc = ffn_layer(xc, norm_g, mods, i, w1, w3, w2, mod_row=CTX_ROW, tm=256, tf=tf)
        else:
            if moe_bf16 is None:
                moe_bf16 = moe_w1[f].astype(BF16), moe_w3[f].astype(BF16), moe_w2[f].astype(BF16)
            (w1, w3, w2), moe_bf16 = moe_bf16, None
            tf = w1.shape[2] // 2
            streams = [(xl, LAT_ROW, min(MOE_TOKEN_TILE, seq))] + ([] if last else [(xc, CTX_ROW, min(256, lc))])
            outs = sparse_moe_layer(streams, norm_g, mods, i, moe_w_router[f], moe_b_router[f], w1, w3, w2, tf=tf,
                                    routed_first=routed_lat)
            xl = outs[0]
            if not last:
                xc = outs[1]
    return xl.reshape(batch, seq, d)
```

```python
import functools

import jax
import jax.numpy as jnp
from jax import lax
from jax.experimental import pallas as pl
from jax.experimental.pallas import tpu as pltpu

F32 = jnp.float32
BF16 = jnp.bfloat16

EPS = 1e-6
NEG_INF = -1e30
LOG2E = 1.4426950408889634
LANES = 128
VMEM_LIMIT = 56 * 1024 * 1024

GRID_W = 64
N_MIXERS = 3
ATTN_HEADS = 16
ATTN_KV_HEADS = 4
ATTN_BLOCKS_PER_STEP = 4
WINDOW = 128
ROPE_BASE = 10000.0
RET_HEADS = 4
RET_CHUNK = 256
POOL_WINDOWS = (2, 4, 8, 16)
POOL_HALO = 8
N_EXPERTS = 8

SH1, SC1, GA1, SH2, SC2, GA2 = range(6)
LAT_ROW, CTX_ROW = 0, 1


def _cparams(*sem):
    return pltpu.CompilerParams(dimension_semantics=sem, vmem_limit_bytes=VMEM_LIMIT)


def _rms(x, g):
    return x * lax.rsqrt(jnp.mean(x * x, axis=-1, keepdims=True) + EPS) * g


def _mod_slice(mod_ref, row, slot, d):
    return mod_ref[row:row + 1, slot * d:(slot + 1) * d]


def _norm_mod(x, g, shift, scale):
    return _rms(x, g) * (1.0 + scale) + shift


def _ada_kernel(c_ref, w_ref, b_ref, o_ref):
    c = c_ref[...]
    s = c * jax.nn.sigmoid(c)
    o_ref[...] = _dot_f32x3(s, *_split_bf16(w_ref[...])) + b_ref[...]


def ada_table(cvecs, w_ada, b_ada):
    depth, d, n = w_ada.shape
    tn = 1536
    return pl.pallas_call(
        _ada_kernel,
        grid=(depth, n // tn),
        in_specs=[
            pl.BlockSpec((8, d), lambda i, j: (0, 0)),
            pl.BlockSpec((None, d, tn), lambda i, j: (i, 0, j)),
            pl.BlockSpec((None, 1, tn), lambda i, j: (i, 0, j)),
        ],
        out_specs=pl.BlockSpec((None, 8, tn), lambda i, j: (i, 0, j)),
        out_shape=jax.ShapeDtypeStruct((depth, 8, n), F32),
        compiler_params=_cparams("parallel", "parallel"),
        name="ada_table",
    )(cvecs, w_ada, b_ada.reshape(depth, 1, n))


def _rope(a, cos, sin):
    lane = lax.broadcasted_iota(jnp.int32, (a.shape[0], LANES), 1)
    first = (lane % 32) < 16
    outs = []
    for cb in range(a.shape[1] // LANES):
        blk = a[:, cb * LANES:(cb + 1) * LANES]
        partner = jnp.where(first, pltpu.roll(blk, LANES - 16, 1), pltpu.roll(blk, 16, 1))
        outs.append(blk * cos + partner * sin)
    return jnp.concatenate(outs, axis=1)


def _mixer_proj_kernel(x_ref, g_ref, mod_ref, w_ref, *rest, mod_row, chunk, rope_cols, q_cols):
    if rope_cols:
        cos_ref, sin_ref, o_ref = rest
        cos, sin = cos_ref[...], sin_ref[...]
    else:
        (o_ref,) = rest
    d = x_ref.shape[1]
    h = _norm_mod(x_ref[...], g_ref[0:1, :], _mod_slice(mod_ref, mod_row, SH1, d),
                  _mod_slice(mod_ref, mod_row, SC1, d)).astype(BF16)
    for c0 in range(0, o_ref.shape[1], chunk):
        acc = jnp.dot(h, w_ref[:, c0:c0 + chunk], preferred_element_type=F32)
        if c0 < q_cols:
            acc = acc * LOG2E
        if c0 < rope_cols:
            acc = _rope(acc, cos, sin)
        o_ref[:, c0:c0 + chunk] = acc.astype(o_ref.dtype)


def mixer_projection(x, norm_g, mods, layer, w, *, mod_row, tm, chunk, q_cols=0, rope=None, rope_cols=0):
    m, d = x.shape
    n = w.shape[1]
    tm = min(tm, m)
    assert n % chunk == 0 and rope_cols % chunk == 0 and q_cols % chunk == 0
    in_specs = [
        pl.BlockSpec((tm, d), lambda i: (i, 0)),
        pl.BlockSpec((None,) + norm_g.shape[1:], lambda i: (layer, 0, 0)),
        pl.BlockSpec((None,) + mods.shape[1:], lambda i: (layer, 0, 0)),
        pl.BlockSpec((d, n), lambda i: (0, 0), pipeline_mode=pl.Buffered(1)),
    ]
    args = [x, norm_g, mods, w]
    if rope_cols:
        in_specs += [pl.BlockSpec((tm, LANES), lambda i: (i, 0))] * 2
        args += list(rope)
    return pl.pallas_call(
        functools.partial(_mixer_proj_kernel, mod_row=mod_row, chunk=chunk, rope_cols=rope_cols, q_cols=q_cols),
        grid=(m // tm,),
        in_specs=in_specs,
        out_specs=pl.BlockSpec((tm, n), lambda i: (i, 0)),
        out_shape=jax.ShapeDtypeStruct((m, n), BF16),
        compiler_params=_cparams("parallel"),
        name="mixer_projection",
    )(*args)


def _outproj_kernel(x_ref, g_ref, mod_ref, w_ref, *rest, n_y, g_row, mod_row, slot, route):
    y_refs, rest = rest[:n_y], rest[n_y:]
    if route:
        router_refs, o_ref, route_out = rest[:3], rest[3], rest[4:]
    else:
        (o_ref,) = rest
    d = x_ref.shape[1]
    y = y_refs[0][...]
    if n_y == 2:
        y = (y.astype(F32) + y_refs[1][...].astype(F32)).astype(BF16)
    t = jnp.dot(y, w_ref[...], preferred_element_type=F32)
    gate = _mod_slice(mod_ref, mod_row, slot, d)
    x_new = x_ref[...] + gate * _rms(t, g_ref[g_row:g_row + 1, :])
    o_ref[...] = x_new
    if route:
        _route_rows(x_new, g_ref, mod_ref, mod_row, *router_refs, *route_out)


def _router_specs(m, d, ne, tm):
    ins = [pl.BlockSpec((d, ne), lambda i: (0, 0))] * 2 + [pl.BlockSpec((1, ne), lambda i: (0, 0))]
    outs = [pl.BlockSpec((tm, 4), lambda i: (i, 0)), pl.BlockSpec((tm, 2), lambda i: (i, 0)),
            pl.BlockSpec((None, 1, ne), lambda i: (i, 0, 0))]
    shapes = [jax.ShapeDtypeStruct((m, 4), jnp.int32), jax.ShapeDtypeStruct((m, 2), F32),
              jax.ShapeDtypeStruct((m // tm, 1, ne), jnp.int32)]
    return ins, outs, shapes


def outproj_residual(x, norm_g, mods, layer, w, ys, *, g_row, mod_row, slot, tm, router=None):
    m, d = x.shape
    k = w.shape[0]
    tm = min(tm, m)
    in_specs = [
        pl.BlockSpec((tm, d), lambda i: (i, 0)),
        pl.BlockSpec((None,) + norm_g.shape[1:], lambda i: (layer, 0, 0)),
        pl.BlockSpec((None,) + mods.shape[1:], lambda i: (layer, 0, 0)),
        pl.BlockSpec((k, d), lambda i: (0, 0)),
    ] + [pl.BlockSpec((tm, k), lambda i: (i, 0))] * len(ys)
    args = [x, norm_g, mods, w, *ys]
    out_specs = [pl.BlockSpec((tm, d), lambda i: (i, 0))]
    out_shape = [jax.ShapeDtypeStruct((m, d), F32)]
    if router is not None:
        w_router, b_router = router
        ne = w_router.shape[1]
        r_in, r_out, r_shapes = _router_specs(m, d, ne, tm)
        in_specs += r_in
        args += [*_split_bf16(w_router), b_router.reshape(1, ne)]
        out_specs += r_out
        out_shape += r_shapes
    outs = pl.pallas_call(
        functools.partial(_outproj_kernel, n_y=len(ys), g_row=g_row, mod_row=mod_row, slot=slot,
                          route=router is not None),
        grid=(m // tm,),
        in_specs=in_specs,
        out_specs=out_specs,
        out_shape=out_shape,
        compiler_params=_cparams("parallel"),
        name="outproj_residual",
    )(*args)
    return outs[0] if router is None else (outs[0], tuple(outs[1:]))


def _dot_nt(a, b):
    return lax.dot_general(a, b, (((1,), (1,)), ((), ())), preferred_element_type=F32)


def _attn_kernel(sink_ref, q_ref, *rest, tq, nsub, kv_heads, group, dh, has_local):
    if has_local:
        k_refs, v_refs = rest[:nsub + 2], rest[nsub + 2:2 * nsub + 4]
    kc_ref, vc_ref, o_ref = rest[-3:]
    assert 2 * dh == LANES and tq == WINDOW and group % 2 == 0
    b = pl.program_id(0)
    nb = pl.num_programs(0)
    rows = group * tq
    lc = kc_ref.shape[0]
    head_of_row = lax.broadcasted_iota(jnp.int32, (rows, 1), 0) // tq
    lane = lax.broadcasted_iota(jnp.int32, (1, LANES), 1)
    if has_local:
        r = lax.broadcasted_iota(jnp.int32, (rows, tq), 0) % tq
        c = lax.broadcasted_iota(jnp.int32, (rows, tq), 1)
    for sub, j in [(sub, j) for sub in range(nsub) for j in range(kv_heads)]:
        qrows = slice(sub * tq, (sub + 1) * tq)
        hs = slice(j * dh, (j + 1) * dh)
        pair = slice((j // 2) * LANES, (j // 2 + 1) * LANES)
        v_low = j % 2 == 0
        keep_v = (lane < dh) if v_low else (lane >= dh)
        qg = jnp.concatenate(
            [q_ref[qrows, (j * group + g) * dh:(j * group + g + 1) * dh] for g in range(group)], axis=0)
        sink = jnp.full((rows, 1), sink_ref[j * group] * LOG2E, F32)
        for g in range(1, group):
            sink = jnp.where(head_of_row == g, sink_ref[j * group + g] * LOG2E, sink)
        if has_local:
            kp_ref, kq_ref, kn_ref = k_refs[sub:sub + 3]
            vp_ref, vq_ref, vn_ref = v_refs[sub:sub + 3]
            prev_ok = (c >= r) if sub > 0 else (c >= r) & (b > 0)
            next_ok = (c <= r) if sub < nsub - 1 else (c <= r) & (b < nb - 1)
            k_all = jnp.concatenate([kc_ref[:, hs], kp_ref[:, hs], kq_ref[:, hs], kn_ref[:, hs]], axis=0)
            v_all = jnp.concatenate([vc_ref[:, pair], vp_ref[:, pair], vq_ref[:, pair], vn_ref[:, pair]], axis=0)
            s = _dot_nt(qg, k_all)
            s = jnp.concatenate([s[:, :lc],
                                 jnp.where(prev_ok, s[:, lc:lc + tq], NEG_INF),
                                 s[:, lc + tq:lc + 2 * tq],
                                 jnp.where(next_ok, s[:, lc + 2 * tq:], NEG_INF)], axis=1)
        else:
            v_all = vc_ref[:, pair]
            s = _dot_nt(qg, kc_ref[:, hs])
        m = jnp.maximum(jnp.max(s, axis=-1, keepdims=True), sink)
        p = jnp.exp2(s - m).astype(BF16)
        v_ones = jnp.where(keep_v, v_all, jnp.ones_like(v_all))
        pv = jnp.dot(p, v_ones, preferred_element_type=F32)
        den = pltpu.roll(pv, dh, 1) + jnp.exp2(sink - m)
        o = pv / den
        for g in range(0, group, 2):
            a, bb = o[g * tq:(g + 1) * tq], o[(g + 1) * tq:(g + 2) * tq]
            if v_low:
                both = jnp.where(lane < dh, a, pltpu.roll(bb, dh, 1))
            else:
                both = jnp.where(lane < dh, pltpu.roll(a, dh, 1), bb)
            h = j * group + g
            o_ref[qrows, h * dh:(h + 2) * dh] = both.astype(o_ref.dtype)


def attention(sink, qkv, qkv_ctx, *, has_local):
    s = qkv.shape[0]
    dh = qkv.shape[1] // (ATTN_HEADS + 2 * ATTN_KV_HEADS)
    hd = ATTN_HEADS * dh
    kvd = ATTN_KV_HEADS * dh
    kcol, vcol = hd // kvd, hd // kvd + 1
    tq = WINDOW
    nb = s // tq
    nsub = min(ATTN_BLOCKS_PER_STEP, nb)
    lc = qkv_ctx.shape[0]
    in_specs = [
        pl.BlockSpec(memory_space=pltpu.SMEM),
        pl.BlockSpec((nsub * tq, hd), lambda b: (b, 0)),
    ]
    args = [sink, qkv]
    if has_local:
        for col in (kcol, vcol):
            for off in range(-1, nsub + 1):
                in_specs.append(pl.BlockSpec(
                    (tq, kvd), lambda b, col=col, off=off: (jnp.clip(nsub * b + off, 0, nb - 1), col)))
                args.append(qkv)
    in_specs += [pl.BlockSpec((lc, kvd), lambda b: (0, kcol)), pl.BlockSpec((lc, kvd), lambda b: (0, vcol))]
    args += [qkv_ctx, qkv_ctx]
    return pl.pallas_call(
        functools.partial(_attn_kernel, tq=tq, nsub=nsub, kv_heads=ATTN_KV_HEADS,
                          group=ATTN_HEADS // ATTN_KV_HEADS, dh=dh, has_local=has_local),
        grid=(nb // nsub,),
        in_specs=in_specs,
        out_specs=pl.BlockSpec((nsub * tq, hd), lambda b: (b, 0)),
        out_shape=jax.ShapeDtypeStruct((s, hd), BF16),
        compiler_params=_cparams("parallel"),
        name="attention",
    )(*args)


N_RET_IN = 9


def _ret_kernel(*refs, heads, dk, dv):
    ins = [refs[:N_RET_IN], refs[N_RET_IN:2 * N_RET_IN]]
    outs = refs[2 * N_RET_IN:2 * N_RET_IN + 4]
    state_ref = refs[-1]
    i = pl.program_id(0)

    @pl.when(i == 0)
    def _():
        for dr in range(2):
            state_ref[dr] = ins[dr][8][...]

    for dr, h in [(dr, h) for h in range(heads) for dr in range(2)]:
        q_ref, k_ref, v_ref, gate_ref, intra_ref, qdec_ref, kdec_ref, cdec_ref, _ = ins[dr]
        z_ref = outs[dr]
        q = q_ref[:, h * dk:(h + 1) * dk]
        k = k_ref[:, h * dk:(h + 1) * dk]
        v = v_ref[:, h * dv:(h + 1) * dv]
        state = state_ref[dr, h]
        sc = _dot_nt(q, k) * intra_ref[h]
        qd = (q.astype(F32) * qdec_ref[h]).astype(BF16)
        o = (jnp.dot(sc.astype(BF16), v, preferred_element_type=F32)
             + jnp.dot(qd, state.astype(BF16), preferred_element_type=F32))
        kd = (k.astype(F32) * kdec_ref[h]).astype(BF16)
        state_ref[dr, h] = state * cdec_ref[h] + lax.dot_general(
            kd, v, (((0,), (0,)), ((), ())), preferred_element_type=F32)
        mu = jnp.mean(o, axis=-1, keepdims=True)
        oc = o - mu
        var = jnp.mean(oc * oc, axis=-1, keepdims=True)
        gate = gate_ref[:, h * dv:(h + 1) * dv].astype(F32)
        z = gate * jax.nn.sigmoid(gate) * (oc * lax.rsqrt(var + EPS))
        z_ref[:, h * dv:(h + 1) * dv] = z.astype(z_ref.dtype)

    @pl.when(i == pl.num_programs(0) - 1)
    def _():
        for dr in range(2):
            outs[2 + dr][...] = state_ref[dr]


def retention_scan(proj, tables_f, tables_b, s0_f, s0_b):
    m, n = proj.shape
    d = n // 8
    heads = RET_HEADS
    dk, dv = d // heads, 2 * d // heads
    c = min(RET_CHUNK, m)
    nc = m // c
    once = lambda a: pl.BlockSpec(a.shape, lambda i: (0,) * a.ndim, pipeline_mode=pl.Buffered(1))
    in_specs, args, out_specs = [], [], []
    for reverse, tables, s0 in ((False, tables_f, s0_f), (True, tables_b, s0_b)):
        row = (lambda i: nc - 1 - i) if reverse else (lambda i: i)
        in_specs += [
            pl.BlockSpec((c, d), lambda i, row=row: (row(i), 0)),
            pl.BlockSpec((c, d), lambda i, row=row: (row(i), 1)),
            pl.BlockSpec((c, 2 * d), lambda i, row=row: (row(i), 1)),
            pl.BlockSpec((c, 2 * d), lambda i, row=row, blk=3 if reverse else 2: (row(i), blk)),
        ] + [once(a) for a in (*tables, s0)]
        args += [proj] * 4 + [*tables, s0]
        out_specs.append(pl.BlockSpec((c, 2 * d), lambda i, row=row: (row(i), 0)))
    out_specs += [pl.BlockSpec(s0_f.shape, lambda i: (0,) * s0_f.ndim)] * 2
    return pl.pallas_call(
        functools.partial(_ret_kernel, heads=heads, dk=dk, dv=dv),
        grid=(nc,),
        in_specs=in_specs,
        out_specs=out_specs,
        out_shape=[jax.ShapeDtypeStruct((m, 2 * d), BF16)] * 2 + [jax.ShapeDtypeStruct(s0_f.shape, F32)] * 2,
        scratch_shapes=[pltpu.VMEM((2,) + s0_f.shape, F32)],
        compiler_params=_cparams("arbitrary"),
        name="retention_scan",
    )(*args)


def retention_tables(log_decay_row, c, reverse):
    lg = -jnp.exp(log_decay_row.astype(F32))
    idx = jnp.arange(c, dtype=F32)
    diff = idx[:, None] - idx[None, :]
    if reverse:
        diff = -diff
    intra = jnp.where(diff >= 0, jnp.exp(lg[:, None, None] * jnp.maximum(diff, 0.0)), 0.0)
    fwd_idx = (c - 1.0 - idx) if reverse else idx
    qdec = jnp.exp(lg[:, None] * (fwd_idx + 1.0))[:, :, None]
    kdec = jnp.exp(lg[:, None] * (c - 1.0 - fwd_idx))[:, :, None]
    cdec = jnp.exp(lg * c)[:, None, None]
    return intra, qdec, kdec, cdec


def _window_sum(h_ref, cols, tmp_refs, w, tm, halo):
    rows = tm + 2 * halo
    src, src_cols, span, k = h_ref, cols, 1, 0
    while 2 * span < w:
        n = rows - 2 * span + 1
        dst = tmp_refs[k % 2]
        dst[0:n, :] = src[0:n, src_cols] + src[span:span + n, src_cols]
        src, src_cols, span, k = dst, slice(None), 2 * span, k + 1
    lo = halo - w // 2
    return src[lo:lo + tm, src_cols] + src[lo + span:lo + span + tm, src_cols]


def _pool_kernel(x_ref, xp_ref, xn_ref, g_ref, mod_ref, w_ref, ps_ref, o_ref, h_ref, tmp_a, tmp_b, *, mod_row, seq):
    tm, d = x_ref.shape
    halo = POOL_HALO
    i = pl.program_id(0)
    g0 = g_ref[0:1, :]
    shift = _mod_slice(mod_ref, mod_row, SH1, d)
    scale = _mod_slice(mod_ref, mod_row, SC1, d)
    hp = _norm_mod(xp_ref[...], g0, shift, scale)
    hn = _norm_mod(xn_ref[...], g0, shift, scale)
    h_ref[0:halo, :] = jnp.where(i > 0, hp, 0.0)
    h_ref[halo:halo + tm, :] = _norm_mod(x_ref[...], g0, shift, scale)
    h_ref[halo + tm:, :] = jnp.where(i < pl.num_programs(0) - 1, hn, 0.0)

    t = i * tm + lax.broadcasted_iota(jnp.int32, (tm, 1), 0)
    gw = d // len(POOL_WINDOWS)
    ys = []
    for g, w in enumerate(POOL_WINDOWS):
        cols = slice(g * gw, (g + 1) * gw)
        tot = _window_sum(h_ref, cols, (tmp_a, tmp_b), w, tm, halo)
        cnt = (jnp.minimum(t + w // 2, seq) - jnp.maximum(t - w // 2, 0)).astype(F32)
        dm = tot / cnt - h_ref[halo:halo + tm, cols]
        ys.append(jnp.dot(dm.astype(BF16), w_ref[g], preferred_element_type=F32))
    y = jnp.concatenate(ys, axis=1) * ps_ref[...]
    o_ref[...] = x_ref[...] + _mod_slice(mod_ref, mod_row, GA1, d) * _rms(y, g_ref[1:2, :])


def pool_layer(x, norm_g, mods, layer, w_pool, pool_scale, *, mod_row, tm):
    m, d = x.shape
    tm = min(tm, m)
    nt = m // tm
    hb = tm // POOL_HALO
    return pl.pallas_call(
        functools.partial(_pool_kernel, mod_row=mod_row, seq=m),
        grid=(nt,),
        in_specs=[
            pl.BlockSpec((tm, d), lambda i: (i, 0)),
            pl.BlockSpec((POOL_HALO, d), lambda i: (jnp.maximum(i * hb - 1, 0), 0)),
            pl.BlockSpec((POOL_HALO, d), lambda i: (jnp.minimum((i + 1) * hb, nt * hb - 1), 0)),
            pl.BlockSpec((None,) + norm_g.shape[1:], lambda i: (layer, 0, 0)),
            pl.BlockSpec((None,) + mods.shape[1:], lambda i: (layer, 0, 0)),
            pl.BlockSpec(w_pool.shape, lambda i: (0, 0, 0)),
            pl.BlockSpec((1, d), lambda i: (0, 0)),
        ],
        out_specs=pl.BlockSpec((tm, d), lambda i: (i, 0)),
        out_shape=jax.ShapeDtypeStruct((m, d), F32),
        scratch_shapes=[pltpu.VMEM((tm + 2 * POOL_HALO, d), F32)]
        + [pltpu.VMEM((tm + 2 * POOL_HALO, d // len(POOL_WINDOWS)), F32)] * 2,
        compiler_params=_cparams("parallel"),
        name="pool_layer",
    )(x, x, x, norm_g, mods, w_pool, pool_scale.reshape(1, d))


SWIGLU_SUB = 256


def _swiglu_partial(h, w1_ref, w3_ref, w2_ref, lo, hi):
    assert (hi - lo) % SWIGLU_SUB == 0
    acc = None
    for s in range(lo, hi, SWIGLU_SUB):
        cols = slice(s, s + SWIGLU_SUB)
        a = jnp.dot(h, w1_ref[:, cols], preferred_element_type=F32)
        b = jnp.dot(h, w3_ref[:, cols], preferred_element_type=F32)
        u = (a * jax.nn.sigmoid(a) * b).astype(BF16)
        p = jnp.dot(u, w2_ref[cols, :], preferred_element_type=F32)
        acc = p if acc is None else acc + p
    return acc


def _ffn_kernel(x_ref, g_ref, mod_ref, w1_ref, w3_ref, w2_ref, *rest, mod_row, n_cast, nf):
    cast_src, o_ref, cast_dst = rest[:n_cast], rest[n_cast], rest[n_cast + 1:2 * n_cast + 1]
    h_ref, acc_ref = rest[2 * n_cast + 1:]
    d = x_ref.shape[1]
    f = pl.program_id(1)
    for src, dst in zip(cast_src, cast_dst):
        dst[...] = src[...].astype(dst.dtype)

    assert nf >= 2
    n_sub = w1_ref.shape[1] // SWIGLU_SUB
    bounds = [-(-n_sub * c // nf) * SWIGLU_SUB for c in range(nf + 1)]
    for c in range(nf):
        @pl.when(f == c)
        def _(c=c):
            if c == 0:
                h = _norm_mod(x_ref[...], g_ref[2:3, :], _mod_slice(mod_ref, mod_row, SH2, d),
                              _mod_slice(mod_ref, mod_row, SC2, d)).astype(BF16)
                h_ref[...] = h
            else:
                h = h_ref[...]
            part = _swiglu_partial(h, w1_ref, w3_ref, w2_ref, bounds[c], bounds[c + 1])
            if c == 0:
                acc_ref[...] = part
            elif c < nf - 1:
                acc_ref[...] += part
            else:
                y = acc_ref[...] + part
                o_ref[...] = x_ref[...] + _mod_slice(mod_ref, mod_row, GA2, d) * _rms(y, g_ref[3:4, :])


def ffn_layer(x, norm_g, mods, layer, w1, w3, w2, *, mod_row, tm, tf, cast=None):
    m, d = x.shape
    ff = w1.shape[1]
    tm = min(tm, m)
    nf = ff // tf
    steps = (m // tm) * nf
    in_specs = [
        pl.BlockSpec((tm, d), lambda i, f: (i, 0)),
        pl.BlockSpec((None,) + norm_g.shape[1:], lambda i, f: (layer, 0, 0)),
        pl.BlockSpec((None,) + mods.shape[1:], lambda i, f: (layer, 0, 0)),
        pl.BlockSpec((d, ff), lambda i, f: (0, 0), pipeline_mode=pl.Buffered(1)),
        pl.BlockSpec((d, ff), lambda i, f: (0, 0), pipeline_mode=pl.Buffered(1)),
        pl.BlockSpec((ff, d), lambda i, f: (0, 0), pipeline_mode=pl.Buffered(1)),
    ]
    out_specs = [pl.BlockSpec((tm, d), lambda i, f: (i, 0))]
    out_shape = [jax.ShapeDtypeStruct((m, d), F32)]
    cast_arrays, cast_idx = cast if cast is not None else ((), 0)
    for a in cast_arrays:
        _, ne, r, c = a.shape
        per_expert = steps // ne
        assert per_expert * ne == steps and r % (per_expert * 16) == 0
        rows = r // per_expert
        in_specs.append(pl.BlockSpec(
            (None, None, rows, c),
            lambda i, f, pe=per_expert: (cast_idx, (i * nf + f) // pe, (i * nf + f) % pe, 0)))
        out_specs.append(pl.BlockSpec(
            (None, rows, c), lambda i, f, pe=per_expert: ((i * nf + f) // pe, (i * nf + f) % pe, 0)))
        out_shape.append(jax.ShapeDtypeStruct((ne, r, c), BF16))
    outs = pl.pallas_call(
        functools.partial(_ffn_kernel, mod_row=mod_row, n_cast=len(cast_arrays), nf=nf),
        grid=(m // tm, nf),
        in_specs=in_specs,
        out_specs=out_specs,
        out_shape=out_shape,
        scratch_shapes=[pltpu.VMEM((tm, d), BF16), pltpu.VMEM((tm, d), F32)],
        compiler_params=_cparams("arbitrary", "arbitrary"),
        name="ffn_layer",
    )(x, norm_g, mods, w1, w3, w2, *cast_arrays)
    return outs[0] if cast is None else (outs[0], tuple(outs[1:]))


RUN_ALIGN = 16
RUN_BITS = tuple(range(9, 3, -1))
EXPERT_ROWS = 512
MOE_TOKEN_TILE = 512
TAIL_BITS = tuple(range(9, 3, -1))


def _split_dma(length, src, src_row, dst, dst_row, sem, bits, *, start, src_advances=True):
    for b in bits:
        size = 1 << b
        off = (length >> (b + 1)) << (b + 1)

        @pl.when(((length >> b) & 1) == 1)
        def _():
            s_row = pl.multiple_of(src_row + off, RUN_ALIGN) if src_advances else src_row
            cp = pltpu.make_async_copy(src.at[pl.ds(s_row, size)],
                                       dst.at[pl.ds(pl.multiple_of(dst_row + off, RUN_ALIGN), size)], sem)
            if start:
                cp.start()
            else:
                cp.wait()


def _run_dmas(tile, ne, lpad_ref, loff_ref, base_ref, buf, hbm_ref, sem, *, to_hbm, start):
    for e in range(ne):
        length = lpad_ref[tile * ne + e]
        lo = loff_ref[tile * ne + e]
        gb = base_ref[tile * ne + e]
        if to_hbm:
            _split_dma(length, buf, lo, hbm_ref, gb, sem, RUN_BITS, start=start)
        else:
            _split_dma(length, hbm_ref, gb, buf, lo, sem, RUN_BITS, start=start)


def _split_bf16(a):
    hi = a.astype(BF16)
    return hi, (a - hi.astype(F32)).astype(BF16)


def _dot_f32x3(a, b_hi, b_lo):
    a_hi, a_lo = _split_bf16(a)
    dot = functools.partial(jnp.dot, preferred_element_type=F32)
    return dot(a_hi, b_hi) + (dot(a_hi, b_lo) + dot(a_lo, b_hi))


def _local_rows(ri_ref, loff_ref, tile, ne):
    e1, e2 = ri_ref[:, 0:1], ri_ref[:, 1:2]
    lo1, lo2 = ri_ref[:, 2:3], ri_ref[:, 3:4]
    for e in range(ne):
        off = loff_ref[tile * ne + e]
        lo1 = lo1 + jnp.where(e1 == e, off, 0)
        lo2 = lo2 + jnp.where(e2 == e, off, 0)
    return lo1, lo2


def _router_kernel(x_ref, g_ref, mod_ref, wr_hi_ref, wr_lo_ref, br_ref, ri_ref, rw_ref, cnt_ref, *, mod_row):
    _route_rows(x_ref[...], g_ref, mod_ref, mod_row, wr_hi_ref, wr_lo_ref, br_ref, ri_ref, rw_ref, cnt_ref)


def _route_rows(x, g_ref, mod_ref, mod_row, wr_hi_ref, wr_lo_ref, br_ref, ri_ref, rw_ref, cnt_ref):
    tm, d = x.shape
    ne = wr_hi_ref.shape[1]
    h = _norm_mod(x, g_ref[2:3, :], _mod_slice(mod_ref, mod_row, SH2, d), _mod_slice(mod_ref, mod_row, SC2, d))
    logits = _dot_f32x3(h, wr_hi_ref[...], wr_lo_ref[...]) + br_ref[...]
    lane = lax.broadcasted_iota(jnp.int32, logits.shape, 1)
    v1 = jnp.max(logits, axis=-1, keepdims=True)
    i1 = jnp.min(jnp.where(logits == v1, lane, ne), axis=-1, keepdims=True)
    rest = jnp.where(lane == i1, -jnp.inf, logits)
    v2 = jnp.max(rest, axis=-1, keepdims=True)
    i2 = jnp.min(jnp.where(rest == v2, lane, ne), axis=-1, keepdims=True)
    e2 = jnp.exp(v2 - v1)
    den = 1.0 + e2
    oh1 = (lane == i1).astype(F32)
    oh2 = (lane == i2).astype(F32)
    both = oh1 + oh2
    r = lax.broadcasted_iota(jnp.int32, (tm, tm), 0)
    c = lax.broadcasted_iota(jnp.int32, (tm, tm), 1)
    tri = jnp.where(c < r, 1.0, 0.0).astype(BF16)
    rank_all = jnp.dot(tri, both.astype(BF16), preferred_element_type=F32)
    rank1 = jnp.sum(rank_all * oh1, axis=-1, keepdims=True).astype(jnp.int32)
    rank2 = jnp.sum(rank_all * oh2, axis=-1, keepdims=True).astype(jnp.int32)
    col = lax.broadcasted_iota(jnp.int32, ri_ref.shape, 1)
    ri_ref[...] = jnp.where(col == 0, i1, jnp.where(col == 1, i2, jnp.where(col == 2, rank1, rank2)))
    colw = lax.broadcasted_iota(jnp.int32, rw_ref.shape, 1)
    rw_ref[...] = jnp.where(colw == 0, 1.0 / den, e2 / den)
    cnt_ref[...] = jnp.sum(both, axis=0, keepdims=True).astype(jnp.int32)


def moe_router(x, norm_g, mods, layer, w_router, b_router, *, mod_row, tm):
    m, d = x.shape
    ne = w_router.shape[1]
    return pl.pallas_call(
        functools.partial(_router_kernel, mod_row=mod_row),
        grid=(m // tm,),
        in_specs=[
            pl.BlockSpec((tm, d), lambda i: (i, 0)),
            pl.BlockSpec((None,) + norm_g.shape[1:], lambda i: (layer, 0, 0)),
            pl.BlockSpec((None,) + mods.shape[1:], lambda i: (layer, 0, 0)),
            pl.BlockSpec((d, ne), lambda i: (0, 0)),
            pl.BlockSpec((d, ne), lambda i: (0, 0)),
            pl.BlockSpec((1, ne), lambda i: (0, 0)),
        ],
        out_specs=[
            pl.BlockSpec((tm, 4), lambda i: (i, 0)),
            pl.BlockSpec((tm, 2), lambda i: (i, 0)),
            pl.BlockSpec((None, 1, ne), lambda i: (i, 0, 0)),
        ],
        out_shape=[
            jax.ShapeDtypeStruct((m, 4), jnp.int32),
            jax.ShapeDtypeStruct((m, 2), F32),
            jax.ShapeDtypeStruct((m // tm, 1, ne), jnp.int32),
        ],
        compiler_params=_cparams("parallel"),
        name="moe_router",
    )(x, norm_g, mods, *_split_bf16(w_router), b_router.reshape(1, ne))


def _dispatch_kernel(lpad_ref, loff_ref, base_ref, lrow_ref, carry_ref, prev_ref, tlen_ref, tstart_ref, nu_ref,
                     x_ref, g_ref, mod_ref, ri_ref, *rest, mod_row, ne, tile0, first):
    xs_ref, cbuf, zbuf, sem = rest if first else rest[1:]
    tm, d = x_ref.shape
    rc = cbuf.shape[1]
    i = pl.program_id(0)
    last = i == pl.num_programs(0) - 1
    slot = i % 2

    def zero_tails(start):
        for e in range(ne):
            _split_dma(tlen_ref[e], zbuf, 0, xs_ref, tstart_ref[e], sem.at[2], TAIL_BITS, start=start,
                       src_advances=False)

        def unused_tile(t, carry):
            cp = pltpu.make_async_copy(
                zbuf, xs_ref.at[pl.ds(pl.multiple_of(t * EXPERT_ROWS, EXPERT_ROWS), EXPERT_ROWS)], sem.at[2])
            if start:
                cp.start()
            else:
                cp.wait()
            return carry

        lax.fori_loop(nu_ref[0], xs_ref.shape[0] // EXPERT_ROWS, unused_tile, 0)

    if first:
        @pl.when(i == 0)
        def _():
            zbuf[...] = jnp.zeros_like(zbuf)
            zero_tails(True)

    h = _norm_mod(x_ref[...], g_ref[2:3, :], _mod_slice(mod_ref, mod_row, SH2, d),
                  _mod_slice(mod_ref, mod_row, SC2, d)).astype(BF16)
    lo1, lo2 = _local_rows(ri_ref, lrow_ref, i + tile0, ne)
    lane = lax.broadcasted_iota(jnp.int32, (tm, rc), 1)
    onehot = jnp.where((lane == lo1) | (lane == lo2), 1.0, 0.0).astype(BF16)
    sorted_rows = lax.dot_general(onehot, h, (((0,), (0,)), ((), ())), preferred_element_type=F32)
    cbuf[slot] = sorted_rows.astype(BF16)
    for e in range(ne):
        @pl.when(carry_ref[(i + tile0) * ne + e] > 0)
        def _(e=e):
            here = pl.ds(pl.multiple_of(loff_ref[(i + tile0) * ne + e], RUN_ALIGN), RUN_ALIGN)
            prev = pl.ds(pl.multiple_of(prev_ref[(i + tile0) * ne + e], RUN_ALIGN), RUN_ALIGN)
            cbuf[slot, here, :] = (cbuf[slot, here, :].astype(F32)
                                   + cbuf[1 - slot, prev, :].astype(F32)).astype(BF16)

    moves = functools.partial(_run_dmas, ne=ne, lpad_ref=lpad_ref, loff_ref=loff_ref, base_ref=base_ref,
                              hbm_ref=xs_ref, to_hbm=True)

    @pl.when(i > 0)
    def _():
        moves(i + tile0 - 1, buf=cbuf.at[1 - slot], sem=sem.at[1 - slot], start=False)

    moves(i + tile0, buf=cbuf.at[slot], sem=sem.at[slot], start=True)

    @pl.when(last)
    def _():
        moves(i + tile0, buf=cbuf.at[slot], sem=sem.at[slot], start=False)
        if first:
            zero_tails(False)


def moe_dispatch(x, norm_g, mods, layer, route_i, tabs, rows, xs=None, *, mod_row, tm, ne, tile0):
    m, d = x.shape
    rc = 2 * tm + ne * 2 * RUN_ALIGN
    first = xs is None
    in_specs = [
        pl.BlockSpec((tm, d), lambda i, *_: (i, 0)),
        pl.BlockSpec((None,) + norm_g.shape[1:], lambda i, *_: (layer, 0, 0)),
        pl.BlockSpec((None,) + mods.shape[1:], lambda i, *_: (layer, 0, 0)),
        pl.BlockSpec((tm, 4), lambda i, *_: (i, 0)),
    ]
    args = [*tabs, x, norm_g, mods, route_i]
    if not first:
        in_specs.append(pl.BlockSpec(memory_space=pl.ANY))
        args.append(xs)
    return pl.pallas_call(
        functools.partial(_dispatch_kernel, mod_row=mod_row, ne=ne, tile0=tile0, first=first),
        grid_spec=pltpu.PrefetchScalarGridSpec(
            num_scalar_prefetch=len(tabs),
            grid=(m // tm,),
            in_specs=in_specs,
            out_specs=pl.BlockSpec(memory_space=pl.ANY),
            scratch_shapes=[pltpu.VMEM((2, rc, d), BF16), pltpu.VMEM((EXPERT_ROWS, d), BF16),
                            pltpu.SemaphoreType.DMA((3,))],
        ),
        out_shape=jax.ShapeDtypeStruct((rows, d), BF16),
        input_output_aliases={} if first else {len(args) - 1: 0},
        compiler_params=_cparams("arbitrary"),
        name="moe_dispatch",
    )(*args)


EXPERT_ROW_STEP = 128


def _experts_kernel(te_ref, nu_ref, tr_ref, x_ref, w1_ref, w3_ref, w2_ref, y_ref, acc_ref, *, n_chunks):
    del te_ref, nu_ref
    i = pl.program_id(0)
    f = pl.program_id(1)
    nf = pl.num_programs(1)
    tm = x_ref.shape[0]
    live_rows = tr_ref[i]
    steps = (live_rows + EXPERT_ROW_STEP - 1) // EXPERT_ROW_STEP

    for c in range(n_chunks):
        for q in range(1, tm // EXPERT_ROW_STEP + 1):
            @pl.when((f == c) & (steps == q))
            def _(c=c, rows=q * EXPERT_ROW_STEP):
                part = _swiglu_partial(x_ref[0:rows, :], w1_ref, w3_ref, w2_ref, 0, w1_ref.shape[1])
                if c == 0:
                    acc_ref[0:rows, :] = part
                elif c < n_chunks - 1:
                    acc_ref[0:rows, :] += part
                else:
                    y_ref[0:rows, :] = (acc_ref[0:rows, :] + part).astype(y_ref.dtype)
                    if rows < tm:
                        y_ref[rows:, :] = jnp.zeros((tm - rows, y_ref.shape[1]), y_ref.dtype)

    @pl.when((steps == 0) & (f == nf - 1))
    def _():
        y_ref[...] = jnp.zeros_like(y_ref)


def moe_experts(xs, tile_expert, n_used, tile_rows, w1, w3, w2, *, tm, tf):
    rows, d = xs.shape
    ff = w1.shape[2]
    nf = ff // tf
    assert nf >= 2

    def chunk(i, f, nu):
        return jnp.where(i < nu[0], f, nf - 1)

    def wmap_in(i, f, te, nu, tr):
        return (te[i], 0, chunk(i, f, nu))

    def wmap_out(i, f, te, nu, tr):
        return (te[i], chunk(i, f, nu), 0)

    def xmap(i, f, te, nu, tr):
        return (jnp.maximum(jnp.minimum(i, nu[0] - 1), 0), 0)

    return pl.pallas_call(
        functools.partial(_experts_kernel, n_chunks=nf),
        grid_spec=pltpu.PrefetchScalarGridSpec(
            num_scalar_prefetch=3,
            grid=(rows // tm, nf),
            in_specs=[
                pl.BlockSpec((tm, d), xmap),
                pl.BlockSpec((None, d, tf), wmap_in),
                pl.BlockSpec((None, d, tf), wmap_in),
                pl.BlockSpec((None, tf, d), wmap_out),
            ],
            out_specs=pl.BlockSpec((tm, d), lambda i, f, te, nu, tr: (i, 0)),
            scratch_shapes=[pltpu.VMEM((tm, d), F32)],
        ),
        out_shape=jax.ShapeDtypeStruct((rows, d), BF16),
        compiler_params=_cparams("arbitrary", "arbitrary"),
        name="moe_experts",
    )(tile_expert, n_used, tile_rows, xs, w1, w3, w2)


def _combine_kernel(lpad_ref, loff_ref, base_ref, lrow_ref, x_ref, ri_ref, rw_ref, g_ref, mod_ref, ys_ref, o_ref,
                    ybuf, sem, *, mod_row, ne, tile0):
    tm, d = x_ref.shape
    rc = ybuf.shape[1]
    i = pl.program_id(0)
    slot = i % 2
    moves = functools.partial(_run_dmas, ne=ne, lpad_ref=lpad_ref, loff_ref=loff_ref, base_ref=base_ref,
                              hbm_ref=ys_ref, to_hbm=False)

    @pl.when(i == 0)
    def _():
        ybuf[...] = jnp.zeros_like(ybuf)
        moves(tile0, buf=ybuf.at[0], sem=sem.at[0], start=True)

    @pl.when(i + 1 < pl.num_programs(0))
    def _():
        moves(i + tile0 + 1, buf=ybuf.at[1 - slot], sem=sem.at[1 - slot], start=True)

    moves(i + tile0, buf=ybuf.at[slot], sem=sem.at[slot], start=False)
    lo1, lo2 = _local_rows(ri_ref, lrow_ref, i + tile0, ne)
    lane = lax.broadcasted_iota(jnp.int32, (tm, rc), 1)
    rows = ybuf[slot]
    y1 = jnp.dot(jnp.where(lane == lo1, 1.0, 0.0).astype(BF16), rows, preferred_element_type=F32)
    y2 = jnp.dot(jnp.where(lane == lo2, 1.0, 0.0).astype(BF16), rows, preferred_element_type=F32)
    y = rw_ref[:, 0:1] * y1 + rw_ref[:, 1:2] * y2
    o_ref[...] = x_ref[...] + _mod_slice(mod_ref, mod_row, GA2, d) * _rms(y, g_ref[3:4, :])


def moe_combine(x, norm_g, mods, layer, route_i, route_w, ys, tabs, *, mod_row, tm, ne, tile0):
    m, d = x.shape
    rc = 2 * tm + ne * 2 * RUN_ALIGN
    return pl.pallas_call(
        functools.partial(_combine_kernel, mod_row=mod_row, ne=ne, tile0=tile0),
        grid_spec=pltpu.PrefetchScalarGridSpec(
            num_scalar_prefetch=len(tabs),
            grid=(m // tm,),
            in_specs=[
                pl.BlockSpec((tm, d), lambda i, *_: (i, 0)),
                pl.BlockSpec((tm, 4), lambda i, *_: (i, 0)),
                pl.BlockSpec((tm, 2), lambda i, *_: (i, 0)),
                pl.BlockSpec((None,) + norm_g.shape[1:], lambda i, *_: (layer, 0, 0)),
                pl.BlockSpec((None,) + mods.shape[1:], lambda i, *_: (layer, 0, 0)),
                pl.BlockSpec(memory_space=pl.ANY),
            ],
            out_specs=pl.BlockSpec((tm, d), lambda i, *_: (i, 0)),
            scratch_shapes=[pltpu.VMEM((2, rc, d), BF16), pltpu.SemaphoreType.DMA((2,))],
        ),
        out_shape=jax.ShapeDtypeStruct((m, d), F32),
        compiler_params=_cparams("arbitrary"),
        name="moe_combine",
    )(*tabs, x, route_i, route_w, norm_g, mods, ys)


def sparse_moe_layer(streams, norm_g, mods, layer, w_router, b_router, w1, w3, w2, *, tf, routed_first=None):
    ne = w_router.shape[1]
    tm_exp = EXPERT_ROWS
    routed = [moe_router(x, norm_g, mods, layer, w_router, b_router, mod_row=row, tm=tm)
              if k > 0 or routed_first is None else routed_first for k, (x, row, tm) in enumerate(streams)]
    align = lambda a: (a + RUN_ALIGN - 1) // RUN_ALIGN * RUN_ALIGN
    per_stream = [r[2][:, 0, :] for r in routed]
    counts = jnp.concatenate(per_stream, axis=0)
    nt = counts.shape[0]
    m_total = sum(x.shape[0] for x, _, _ in streams)
    stream_rows = [align(jnp.sum(c, axis=0)) for c in per_stream]
    stream_off = jnp.cumsum(jnp.stack(stream_rows), axis=0) - jnp.stack(stream_rows)
    pos = jnp.concatenate([stream_off[k][None, :] + jnp.cumsum(c, axis=0) - c for k, c in enumerate(per_stream)])
    carry = pos % RUN_ALIGN
    lpad = align(carry + counts)
    loff = jnp.cumsum(lpad, axis=1) - lpad
    prev_block = jnp.concatenate([jnp.zeros((1, ne), loff.dtype), (loff + lpad - RUN_ALIGN)[:-1]], axis=0)
    group = sum(stream_rows)
    gpad = (group + tm_exp - 1) // tm_exp * tm_exp
    ends = jnp.cumsum(gpad)
    starts = ends - gpad
    base = starts[None, :] + pos - carry
    n_tiles = -(-(2 * m_total + len(streams) * ne * (RUN_ALIGN - 1)) // tm_exp) + ne
    tile_start = jnp.arange(n_tiles, dtype=jnp.int32) * tm_exp
    n_used = (ends[-1] // tm_exp).astype(jnp.int32).reshape(1)
    tile_expert = jnp.sum(tile_start[:, None] >= ends[None, :], axis=1).astype(jnp.int32)
    tile_expert = jnp.minimum(tile_expert, tile_expert[jnp.maximum(n_used[0] - 1, 0)])
    tile_rows = jnp.clip((starts + group)[tile_expert] - tile_start, 0, tm_exp).astype(jnp.int32)
    group_first = stream_rows[0]
    later_rows = m_total - streams[0][0].shape[0] + (len(streams) - 1) * (RUN_ALIGN - 1)
    assert later_rows + tm_exp - RUN_ALIGN < 2 << TAIL_BITS[0]
    tabs = [a.reshape(-1).astype(jnp.int32)
            for a in (lpad, loff, base, loff + carry, carry, prev_block,
                      gpad - group_first, starts + group_first, n_used)]
    xs, tile0 = None, 0
    for (x, row, tm), (route_i, _, _) in zip(streams, routed):
        xs = moe_dispatch(x, norm_g, mods, layer, route_i, tabs, n_tiles * tm_exp, xs,
                          mod_row=row, tm=tm, ne=ne, tile0=tile0)
        tile0 += x.shape[0] // tm
    ys = moe_experts(xs, tile_expert, n_used, tile_rows, w1, w3, w2, tm=tm_exp, tf=tf)
    outs, tile0 = [], 0
    for (x, row, tm), (route_i, route_w, _) in zip(streams, routed):
        outs.append(moe_combine(x, norm_g, mods, layer, route_i, route_w, ys, tabs[:4],
                                mod_row=row, tm=tm, ne=ne, tile0=tile0))
        tile0 += x.shape[0] // tm
    return outs


def _rope_tables(seq):
    n = 16
    rows = seq // GRID_W
    inv = ROPE_BASE ** (-jnp.arange(n, dtype=F32) / n)
    row_ang = jnp.arange(rows, dtype=F32)[:, None] * inv[None, :]
    col_ang = jnp.arange(GRID_W, dtype=F32)[:, None] * inv[None, :]
    per_row = lambda a: jnp.repeat(a, GRID_W, axis=0)
    per_col = lambda a: jnp.tile(a, (rows, 1))
    cos_r, sin_r = per_row(jnp.cos(row_ang)), per_row(jnp.sin(row_ang))
    cos_c, sin_c = per_col(jnp.cos(col_ang)), per_col(jnp.sin(col_ang))
    cos = jnp.concatenate([cos_r, cos_r, cos_c, cos_c], axis=1)
    sin = jnp.concatenate([-sin_r, sin_r, -sin_c, sin_c], axis=1)
    return jnp.tile(cos, (1, 2)), jnp.tile(sin, (1, 2))


def kernel(x, c, ctx, c_ctx, w_ada, b_ada, norm_g, attn_w_qkv, attn_w_o, attn_sink, ret_w_in, ret_w_o,
           ret_log_decay, pool_w, pool_scale, ffn_w1, ffn_w3, ffn_w2, moe_w_router, moe_b_router,
           moe_w1, moe_w3, moe_w2):
    batch, seq, d = x.shape
    assert batch == 1 and c.shape[0] == 1
    depth = w_ada.shape[0]
    lc = ctx.shape[1]
    xl = x.reshape(seq, d)
    xc = ctx.reshape(lc, d)

    cvecs = jnp.zeros((8, d), F32).at[LAT_ROW].set(c[0]).at[CTX_ROW].set(c_ctx)
    mods = ada_table(cvecs, w_ada, b_ada)
    rope = _rope_tables(seq)

    hd = attn_w_o.shape[1]
    dh = hd // ATTN_HEADS
    qkv_scale = jnp.concatenate([jnp.full((hd,), dh ** -0.5, F32),
                                 jnp.ones((attn_w_qkv.shape[2] - hd,), F32)])
    dk = d // RET_HEADS
    ret_scale = jnp.concatenate([jnp.ones((d,), F32), jnp.full((d,), dk ** -0.5, F32),
                                 jnp.ones((ret_w_in.shape[2] - 2 * d,), F32)])

    moe_bf16 = None
    for i in range(depth):
        last = i == depth - 1
        kind, j = i % N_MIXERS, i // N_MIXERS
        proj = functools.partial(mixer_projection, norm_g=norm_g, mods=mods, layer=i)
        routed_lat = None
        if i % 2 == 1:
            mix_out = functools.partial(outproj_residual, norm_g=norm_g, mods=mods, layer=i, g_row=1,
                                        mod_row=LAT_ROW, slot=GA1, tm=min(MOE_TOKEN_TILE, seq),
                                        router=(moe_w_router[i // 2], moe_b_router[i // 2]))
        else:
            mix_out = lambda x, **kw: (outproj_residual(x, norm_g, mods, i, g_row=1, mod_row=LAT_ROW,
                                                        slot=GA1, tm=1024, **kw), None)
        if kind == 0:
            w_qkv = (attn_w_qkv[j] * qkv_scale).astype(BF16)
            w_o = attn_w_o[j].astype(BF16)
            kvd = (w_qkv.shape[1] - hd) // 2
            qkv_l = proj(xl, w=w_qkv, mod_row=LAT_ROW, tm=512, chunk=256, q_cols=hd, rope=rope, rope_cols=hd + kvd)
            qkv_c = proj(xc, w=w_qkv, mod_row=CTX_ROW, tm=256, chunk=256, q_cols=hd)
            o_l = attention(attn_sink[j], qkv_l, qkv_c, has_local=True)
            xl, routed_lat = mix_out(xl, w=w_o, ys=[o_l])
            if not last:
                o_c = attention(attn_sink[j], qkv_c, qkv_c, has_local=False)
                xc = outproj_residual(xc, norm_g, mods, i, w_o, [o_c], g_row=1, mod_row=CTX_ROW, slot=GA1, tm=256)
        elif kind == 1:
            w_in = (ret_w_in[j] * ret_scale).astype(BF16)
            w_o = ret_w_o[j].astype(BF16)
            p_c = proj(xc, w=w_in, mod_row=CTX_ROW, tm=256, chunk=1024)
            p_l = proj(xl, w=w_in, mod_row=LAT_ROW, tm=512, chunk=1024)
            s0 = jnp.zeros((RET_HEADS, dk, 2 * dk), F32)
            tabs_f = retention_tables(ret_log_decay[j, 0], min(RET_CHUNK, lc), False)
            tabs_b = retention_tables(ret_log_decay[j, 1], min(RET_CHUNK, lc), True)
            zf_c, zb_c, s_f, s_b = retention_scan(p_c, tabs_f, tabs_b, s0, s0)
            zf_l, zb_l, _, _ = retention_scan(p_l, tabs_f, tabs_b, s_f, s_b)
            xl, routed_lat = mix_out(xl, w=w_o, ys=[zf_l, zb_l])
            if not last:
                xc = outproj_residual(xc, norm_g, mods, i, w_o, [zf_c, zb_c], g_row=1, mod_row=CTX_ROW, slot=GA1, tm=256)
        else:
            w_p = pool_w[j].astype(BF16)
            xl = pool_layer(xl, norm_g, mods, i, w_p, pool_scale[j], mod_row=LAT_ROW, tm=1024)
            if not last:
                xc = pool_layer(xc, norm_g, mods, i, w_p, pool_scale[j], mod_row=CTX_ROW, tm=256)

        f = i // 2
        if i % 2 == 0:
            w1, w3, w2 = ffn_w1[f].astype(BF16), ffn_w3[f].astype(BF16), ffn_w2[f].astype(BF16)
            tf = w1.shape[1] // 2
            steps = (seq // min(512, seq)) * 2
            ne = moe_w1.shape[1]
            if not last and steps % ne == 0 and d % (steps // ne * 16) == 0:
                xl, moe_bf16 = ffn_layer(xl, norm_g, mods, i, w1, w3, w2, mod_row=LAT_ROW, tm=512, tf=tf,
                                         cast=((moe_w1, moe_w3, moe_w2), (i + 1) // 2))
            else:
                xl = ffn_layer(xl, norm_g, mods, i, w1, w3, w2, mod_row=LAT_ROW, tm=512, tf=tf)
            if not last:
                xc = ffn_layer(xc, norm_g, mods, i, w1, w3, w2, mod_row=CTX_ROW, tm=256, tf=tf)
        else:
            if moe_bf16 is None:
                moe_bf16 = moe_w1[f].astype(BF16), moe_w3[f].astype(BF16), moe_w2[f].astype(BF16)
            (w1, w3, w2), moe_bf16 = moe_bf16, None
            tf = w1.shape[2] // 2
            streams = [(xl, LAT_ROW, min(MOE_TOKEN_TILE, seq))] + ([] if last else [(xc, CTX_ROW, min(256, lc))])
            outs = sparse_moe_layer(streams, norm_g, mods, i, moe_w_router[f], moe_b_router[f], w1, w3, w2, tf=tf,
                                    routed_first=routed_lat)
            xl = outs[0]
            if not last:
                xc = outs[1]
    return xl.reshape(batch, seq, d)
```

```python
import functools

import jax
import jax.numpy as jnp
from jax import lax
from jax.experimental import pallas as pl
from jax.experimental.pallas import tpu as pltpu

F32 = jnp.float32
BF16 = jnp.bfloat16

EPS = 1e-6
NEG_INF = -1e30
LOG2E = 1.4426950408889634
LANES = 128
VMEM_LIMIT = 56 * 1024 * 1024

GRID_W = 64
N_MIXERS = 3
ATTN_HEADS = 16
ATTN_KV_HEADS = 4
ATTN_BLOCKS_PER_STEP = 4
WINDOW = 128
ROPE_BASE = 10000.0
RET_HEADS = 4
RET_CHUNK = 256
POOL_WINDOWS = (2, 4, 8, 16)
POOL_HALO = 8
N_EXPERTS = 8

SH1, SC1, GA1, SH2, SC2, GA2 = range(6)
LAT_ROW, CTX_ROW = 0, 1


def _cparams(*sem):
    return pltpu.CompilerParams(dimension_semantics=sem, vmem_limit_bytes=VMEM_LIMIT)


def _rms(x, g):
    return x * lax.rsqrt(jnp.mean(x * x, axis=-1, keepdims=True) + EPS) * g


def _mod_slice(mod_ref, row, slot, d):
    return mod_ref[row:row + 1, slot * d:(slot + 1) * d]


def _norm_mod(x, g, shift, scale):
    return _rms(x, g) * (1.0 + scale) + shift


def _ada_kernel(c_ref, w_ref, b_ref, o_ref):
    c = c_ref[...]
    s = c * jax.nn.sigmoid(c)
    o_ref[...] = _dot_f32x3(s, *_split_bf16(w_ref[...])) + b_ref[...]


def ada_table(cvecs, w_ada, b_ada):
    depth, d, n = w_ada.shape
    tn = 1536
    return pl.pallas_call(
        _ada_kernel,
        grid=(depth, n // tn),
        in_specs=[
            pl.BlockSpec((8, d), lambda i, j: (0, 0)),
            pl.BlockSpec((None, d, tn), lambda i, j: (i, 0, j)),
            pl.BlockSpec((None, 1, tn), lambda i, j: (i, 0, j)),
        ],
        out_specs=pl.BlockSpec((None, 8, tn), lambda i, j: (i, 0, j)),
        out_shape=jax.ShapeDtypeStruct((depth, 8, n), F32),
        compiler_params=_cparams("parallel", "parallel"),
        name="ada_table",
    )(cvecs, w_ada, b_ada.reshape(depth, 1, n))


def _rope(a, cos, sin):
    lane = lax.broadcasted_iota(jnp.int32, (a.shape[0], LANES), 1)
    first = (lane % 32) < 16
    outs = []
    for cb in range(a.shape[1] // LANES):
        blk = a[:, cb * LANES:(cb + 1) * LANES]
        partner = jnp.where(first, pltpu.roll(blk, LANES - 16, 1), pltpu.roll(blk, 16, 1))
        outs.append(blk * cos + partner * sin)
    return jnp.concatenate(outs, axis=1)


def _mixer_proj_kernel(x_ref, g_ref, mod_ref, w_ref, *rest, mod_row, chunk, rope_cols, q_cols):
    if rope_cols:
        cos_ref, sin_ref, o_ref = rest
        cos, sin = cos_ref[...], sin_ref[...]
    else:
        (o_ref,) = rest
    d = x_ref.shape[1]
    h = _norm_mod(x_ref[...], g_ref[0:1, :], _mod_slice(mod_ref, mod_row, SH1, d),
                  _mod_slice(mod_ref, mod_row, SC1, d)).astype(BF16)
    for c0 in range(0, o_ref.shape[1], chunk):
        acc = jnp.dot(h, w_ref[:, c0:c0 + chunk], preferred_element_type=F32)
        if c0 < q_cols:
            acc = acc * LOG2E
        if c0 < rope_cols:
            acc = _rope(acc, cos, sin)
        o_ref[:, c0:c0 + chunk] = acc.astype(o_ref.dtype)


def mixer_projection(x, norm_g, mods, layer, w, *, mod_row, tm, chunk, q_cols=0, rope=None, rope_cols=0):
    m, d = x.shape
    n = w.shape[1]
    tm = min(tm, m)
    assert n % chunk == 0 and rope_cols % chunk == 0 and q_cols % chunk == 0
    in_specs = [
        pl.BlockSpec((tm, d), lambda i: (i, 0)),
        pl.BlockSpec((None,) + norm_g.shape[1:], lambda i: (layer, 0, 0)),
        pl.BlockSpec((None,) + mods.shape[1:], lambda i: (layer, 0, 0)),
        pl.BlockSpec((d, n), lambda i: (0, 0), pipeline_mode=pl.Buffered(1)),
    ]
    args = [x, norm_g, mods, w]
    if rope_cols:
        in_specs += [pl.BlockSpec((tm, LANES), lambda i: (i, 0))] * 2
        args += list(rope)
    return pl.pallas_call(
        functools.partial(_mixer_proj_kernel, mod_row=mod_row, chunk=chunk, rope_cols=rope_cols, q_cols=q_cols),
        grid=(m // tm,),
        in_specs=in_specs,
        out_specs=pl.BlockSpec((tm, n), lambda i: (i, 0)),
        out_shape=jax.ShapeDtypeStruct((m, n), BF16),
        compiler_params=_cparams("parallel"),
        name="mixer_projection",
    )(*args)


def _outproj_kernel(x_ref, g_ref, mod_ref, w_ref, *rest, n_y, g_row, mod_row, slot, route):
    y_refs, rest = rest[:n_y], rest[n_y:]
    if route:
        router_refs, o_ref, route_out = rest[:3], rest[3], rest[4:]
    else:
        (o_ref,) = rest
    d = x_ref.shape[1]
    y = y_refs[0][...]
    if n_y == 2:
        y = (y.astype(F32) + y_refs[1][...].astype(F32)).astype(BF16)
    t = jnp.dot(y, w_ref[...], preferred_element_type=F32)
    gate = _mod_slice(mod_ref, mod_row, slot, d)
    x_new = x_ref[...] + gate * _rms(t, g_ref[g_row:g_row + 1, :])
    o_ref[...] = x_new
    if route:
        _route_rows(x_new, g_ref, mod_ref, mod_row, *router_refs, *route_out)


def _router_specs(m, d, ne, tm):
    ins = [pl.BlockSpec((d, ne), lambda i: (0, 0))] * 2 + [pl.BlockSpec((1, ne), lambda i: (0, 0))]
    outs = [pl.BlockSpec((tm, 4), lambda i: (i, 0)), pl.BlockSpec((tm, 2), lambda i: (i, 0)),
            pl.BlockSpec((None, 1, ne), lambda i: (i, 0, 0))]
    shapes = [jax.ShapeDtypeStruct((m, 4), jnp.int32), jax.ShapeDtypeStruct((m, 2), F32),
              jax.ShapeDtypeStruct((m // tm, 1, ne), jnp.int32)]
    return ins, outs, shapes


def outproj_residual(x, norm_g, mods, layer, w, ys, *, g_row, mod_row, slot, tm, router=None):
    m, d = x.shape
    k = w.shape[0]
    tm = min(tm, m)
    in_specs = [
        pl.BlockSpec((tm, d), lambda i: (i, 0)),
        pl.BlockSpec((None,) + norm_g.shape[1:], lambda i: (layer, 0, 0)),
        pl.BlockSpec((None,) + mods.shape[1:], lambda i: (layer, 0, 0)),
        pl.BlockSpec((k, d), lambda i: (0, 0)),
    ] + [pl.BlockSpec((tm, k), lambda i: (i, 0))] * len(ys)
    args = [x, norm_g, mods, w, *ys]
    out_specs = [pl.BlockSpec((tm, d), lambda i: (i, 0))]
    out_shape = [jax.ShapeDtypeStruct((m, d), F32)]
    if router is not None:
        w_router, b_router = router
        ne = w_router.shape[1]
        r_in, r_out, r_shapes = _router_specs(m, d, ne, tm)
        in_specs += r_in
        args += [*_split_bf16(w_router), b_router.reshape(1, ne)]
        out_specs += r_out
        out_shape += r_shapes
    outs = pl.pallas_call(
        functools.partial(_outproj_kernel, n_y=len(ys), g_row=g_row, mod_row=mod_row, slot=slot,
                          route=router is not None),
        grid=(m // tm,),
        in_specs=in_specs,
        out_specs=out_specs,
        out_shape=out_shape,
        compiler_params=_cparams("parallel"),
        name="outproj_residual",
    )(*args)
    return outs[0] if router is None else (outs[0], tuple(outs[1:]))


def _dot_nt(a, b):
    return lax.dot_general(a, b, (((1,), (1,)), ((), ())), preferred_element_type=F32)


def _attn_kernel(sink_ref, q_ref, *rest, tq, nsub, kv_heads, group, dh, has_local):
    if has_local:
        k_refs, v_refs = rest[:nsub + 2], rest[nsub + 2:2 * nsub + 4]
    kc_ref, vc_ref, o_ref = rest[-3:]
    assert 2 * dh == LANES and tq == WINDOW and group % 2 == 0
    b = pl.program_id(0)
    nb = pl.num_programs(0)
    rows = group * tq
    lc = kc_ref.shape[0]
    head_of_row = lax.broadcasted_iota(jnp.int32, (rows, 1), 0) // tq
    lane = lax.broadcasted_iota(jnp.int32, (1, LANES), 1)
    if has_local:
        r = lax.broadcasted_iota(jnp.int32, (rows, tq), 0) % tq
        c = lax.broadcasted_iota(jnp.int32, (rows, tq), 1)
    for sub, j in [(sub, j) for sub in range(nsub) for j in range(kv_heads)]:
        qrows = slice(sub * tq, (sub + 1) * tq)
        hs = slice(j * dh, (j + 1) * dh)
        pair = slice((j // 2) * LANES, (j // 2 + 1) * LANES)
        v_low = j % 2 == 0
        keep_v = (lane < dh) if v_low else (lane >= dh)
        qg = jnp.concatenate(
            [q_ref[qrows, (j * group + g) * dh:(j * group + g + 1) * dh] for g in range(group)], axis=0)
        sink = jnp.full((rows, 1), sink_ref[j * group] * LOG2E, F32)
        for g in range(1, group):
            sink = jnp.where(head_of_row == g, sink_ref[j * group + g] * LOG2E, sink)
        if has_local:
            kp_ref, kq_ref, kn_ref = k_refs[sub:sub + 3]
            vp_ref, vq_ref, vn_ref = v_refs[sub:sub + 3]
            prev_ok = (c >= r) if sub > 0 else (c >= r) & (b > 0)
            next_ok = (c <= r) if sub < nsub - 1 else (c <= r) & (b < nb - 1)
            k_all = jnp.concatenate([kc_ref[:, hs], kp_ref[:, hs], kq_ref[:, hs], kn_ref[:, hs]], axis=0)
            v_all = jnp.concatenate([vc_ref[:, pair], vp_ref[:, pair], vq_ref[:, pair], vn_ref[:, pair]], axis=0)
            s = _dot_nt(qg, k_all)
            s = jnp.concatenate([s[:, :lc],
                                 jnp.where(prev_ok, s[:, lc:lc + tq], NEG_INF),
                                 s[:, lc + tq:lc + 2 * tq],
                                 jnp.where(next_ok, s[:, lc + 2 * tq:], NEG_INF)], axis=1)
        else:
            v_all = vc_ref[:, pair]
            s = _dot_nt(qg, kc_ref[:, hs])
        m = jnp.maximum(jnp.max(s, axis=-1, keepdims=True), sink)
        p = jnp.exp2(s - m).astype(BF16)
        v_ones = jnp.where(keep_v, v_all, jnp.ones_like(v_all))
        pv = jnp.dot(p, v_ones, preferred_element_type=F32)
        den = pltpu.roll(pv, dh, 1) + jnp.exp2(sink - m)
        o = pv / den
        for g in range(0, group, 2):
            a, bb = o[g * tq:(g + 1) * tq], o[(g + 1) * tq:(g + 2) * tq]
            if v_low:
                both = jnp.where(lane < dh, a, pltpu.roll(bb, dh, 1))
            else:
                both = jnp.where(lane < dh, pltpu.roll(a, dh, 1), bb)
            h = j * group + g
            o_ref[qrows, h * dh:(h + 2) * dh] = both.astype(o_ref.dtype)


def attention(sink, qkv, qkv_ctx, *, has_local):
    s = qkv.shape[0]
    dh = qkv.shape[1] // (ATTN_HEADS + 2 * ATTN_KV_HEADS)
    hd = ATTN_HEADS * dh
    kvd = ATTN_KV_HEADS * dh
    kcol, vcol = hd // kvd, hd // kvd + 1
    tq = WINDOW
    nb = s // tq
    nsub = min(ATTN_BLOCKS_PER_STEP, nb)
    lc = qkv_ctx.shape[0]
    in_specs = [
        pl.BlockSpec(memory_space=pltpu.SMEM),
        pl.BlockSpec((nsub * tq, hd), lambda b: (b, 0)),
    ]
    args = [sink, qkv]
    if has_local:
        for col in (kcol, vcol):
            for off in range(-1, nsub + 1):
                in_specs.append(pl.BlockSpec(
                    (tq, kvd), lambda b, col=col, off=off: (jnp.clip(nsub * b + off, 0, nb - 1), col)))
                args.append(qkv)
    in_specs += [pl.BlockSpec((lc, kvd), lambda b: (0, kcol)), pl.BlockSpec((lc, kvd), lambda b: (0, vcol))]
    args += [qkv_ctx, qkv_ctx]
    return pl.pallas_call(
        functools.partial(_attn_kernel, tq=tq, nsub=nsub, kv_heads=ATTN_KV_HEADS,
                          group=ATTN_HEADS // ATTN_KV_HEADS, dh=dh, has_local=has_local),
        grid=(nb // nsub,),
        in_specs=in_specs,
        out_specs=pl.BlockSpec((nsub * tq, hd), lambda b: (b, 0)),
        out_shape=jax.ShapeDtypeStruct((s, hd), BF16),
        compiler_params=_cparams("parallel"),
        name="attention",
    )(*args)


N_RET_IN = 9


def _ret_kernel(*refs, heads, dk, dv):
    ins = [refs[:N_RET_IN], refs[N_RET_IN:2 * N_RET_IN]]
    outs = refs[2 * N_RET_IN:2 * N_RET_IN + 4]
    state_ref = refs[-1]
    i = pl.program_id(0)

    @pl.when(i == 0)
    def _():
        for dr in range(2):
            state_ref[dr] = ins[dr][8][...]

    for dr, h in [(dr, h) for h in range(heads) for dr in range(2)]:
        q_ref, k_ref, v_ref, gate_ref, intra_ref, qdec_ref, kdec_ref, cdec_ref, _ = ins[dr]
        z_ref = outs[dr]
        q = q_ref[:, h * dk:(h + 1) * dk]
        k = k_ref[:, h * dk:(h + 1) * dk]
        v = v_ref[:, h * dv:(h + 1) * dv]
        state = state_ref[dr, h]
        sc = _dot_nt(q, k) * intra_ref[h]
        qd = (q.astype(F32) * qdec_ref[h]).astype(BF16)
        o = (jnp.dot(sc.astype(BF16), v, preferred_element_type=F32)
             + jnp.dot(qd, state.astype(BF16), preferred_element_type=F32))
        kd = (k.astype(F32) * kdec_ref[h]).astype(BF16)
        state_ref[dr, h] = state * cdec_ref[h] + lax.dot_general(
            kd, v, (((0,), (0,)), ((), ())), preferred_element_type=F32)
        mu = jnp.mean(o, axis=-1, keepdims=True)
        oc = o - mu
        var = jnp.mean(oc * oc, axis=-1, keepdims=True)
        gate = gate_ref[:, h * dv:(h + 1) * dv].astype(F32)
        z = gate * jax.nn.sigmoid(gate) * (oc * lax.rsqrt(var + EPS))
        z_ref[:, h * dv:(h + 1) * dv] = z.astype(z_ref.dtype)

    @pl.when(i == pl.num_programs(0) - 1)
    def _():
        for dr in range(2):
            outs[2 + dr][...] = state_ref[dr]


def retention_scan(proj, tables_f, tables_b, s0_f, s0_b):
    m, n = proj.shape
    d = n // 8
    heads = RET_HEADS
    dk, dv = d // heads, 2 * d // heads
    c = min(RET_CHUNK, m)
    nc = m // c
    once = lambda a: pl.BlockSpec(a.shape, lambda i: (0,) * a.ndim, pipeline_mode=pl.Buffered(1))
    in_specs, args, out_specs = [], [], []
    for reverse, tables, s0 in ((False, tables_f, s0_f), (True, tables_b, s0_b)):
        row = (lambda i: nc - 1 - i) if reverse else (lambda i: i)
        in_specs += [
            pl.BlockSpec((c, d), lambda i, row=row: (row(i), 0)),
            pl.BlockSpec((c, d), lambda i, row=row: (row(i), 1)),
            pl.BlockSpec((c, 2 * d), lambda i, row=row: (row(i), 1)),
            pl.BlockSpec((c, 2 * d), lambda i, row=row, blk=3 if reverse else 2: (row(i), blk)),
        ] + [once(a) for a in (*tables, s0)]
        args += [proj] * 4 + [*tables, s0]
        out_specs.append(pl.BlockSpec((c, 2 * d), lambda i, row=row: (row(i), 0)))
    out_specs += [pl.BlockSpec(s0_f.shape, lambda i: (0,) * s0_f.ndim)] * 2
    return pl.pallas_call(
        functools.partial(_ret_kernel, heads=heads, dk=dk, dv=dv),
        grid=(nc,),
        in_specs=in_specs,
        out_specs=out_specs,
        out_shape=[jax.ShapeDtypeStruct((m, 2 * d), BF16)] * 2 + [jax.ShapeDtypeStruct(s0_f.shape, F32)] * 2,
        scratch_shapes=[pltpu.VMEM((2,) + s0_f.shape, F32)],
        compiler_params=_cparams("arbitrary"),
        name="retention_scan",
    )(*args)


def retention_tables(log_decay_row, c, reverse):
    lg = -jnp.exp(log_decay_row.astype(F32))
    idx = jnp.arange(c, dtype=F32)
    diff = idx[:, None] - idx[None, :]
    if reverse:
        diff = -diff
    intra = jnp.where(diff >= 0, jnp.exp(lg[:, None, None] * jnp.maximum(diff, 0.0)), 0.0)
    fwd_idx = (c - 1.0 - idx) if reverse else idx
    qdec = jnp.exp(lg[:, None] * (fwd_idx + 1.0))[:, :, None]
    kdec = jnp.exp(lg[:, None] * (c - 1.0 - fwd_idx))[:, :, None]
    cdec = jnp.exp(lg * c)[:, None, None]
    return intra, qdec, kdec, cdec


def _window_sum(h_ref, cols, tmp_refs, w, tm, halo):
    rows = tm + 2 * halo
    src, src_cols, span, k = h_ref, cols, 1, 0
    while 2 * span < w:
        n = rows - 2 * span + 1
        dst = tmp_refs[k % 2]
        dst[0:n, :] = src[0:n, src_cols] + src[span:span + n, src_cols]
        src, src_cols, span, k = dst, slice(None), 2 * span, k + 1
    lo = halo - w // 2
    return src[lo:lo + tm, src_cols] + src[lo + span:lo + span + tm, src_cols]


def _pool_kernel(x_ref, xp_ref, xn_ref, g_ref, mod_ref, w_ref, ps_ref, o_ref, h_ref, tmp_a, tmp_b, *, mod_row, seq):
    tm, d = x_ref.shape
    halo = POOL_HALO
    i = pl.program_id(0)
    g0 = g_ref[0:1, :]
    shift = _mod_slice(mod_ref, mod_row, SH1, d)
    scale = _mod_slice(mod_ref, mod_row, SC1, d)
    hp = _norm_mod(xp_ref[...], g0, shift, scale)
    hn = _norm_mod(xn_ref[...], g0, shift, scale)
    h_ref[0:halo, :] = jnp.where(i > 0, hp, 0.0)
    h_ref[halo:halo + tm, :] = _norm_mod(x_ref[...], g0, shift, scale)
    h_ref[halo + tm:, :] = jnp.where(i < pl.num_programs(0) - 1, hn, 0.0)

    t = i * tm + lax.broadcasted_iota(jnp.int32, (tm, 1), 0)
    gw = d // len(POOL_WINDOWS)
    ys = []
    for g, w in enumerate(POOL_WINDOWS):
        cols = slice(g * gw, (g + 1) * gw)
        tot = _window_sum(h_ref, cols, (tmp_a, tmp_b), w, tm, halo)
        cnt = (jnp.minimum(t + w // 2, seq) - jnp.maximum(t - w // 2, 0)).astype(F32)
        dm = tot / cnt - h_ref[halo:halo + tm, cols]
        ys.append(jnp.dot(dm.astype(BF16), w_ref[g], preferred_element_type=F32))
    y = jnp.concatenate(ys, axis=1) * ps_ref[...]
    o_ref[...] = x_ref[...] + _mod_slice(mod_ref, mod_row, GA1, d) * _rms(y, g_ref[1:2, :])


def pool_layer(x, norm_g, mods, layer, w_pool, pool_scale, *, mod_row, tm):
    m, d = x.shape
    tm = min(tm, m)
    nt = m // tm
    hb = tm // POOL_HALO
    return pl.pallas_call(
        functools.partial(_pool_kernel, mod_row=mod_row, seq=m),
        grid=(nt,),
        in_specs=[
            pl.BlockSpec((tm, d), lambda i: (i, 0)),
            pl.BlockSpec((POOL_HALO, d), lambda i: (jnp.maximum(i * hb - 1, 0), 0)),
            pl.BlockSpec((POOL_HALO, d), lambda i: (jnp.minimum((i + 1) * hb, nt * hb - 1), 0)),
            pl.BlockSpec((None,) + norm_g.shape[1:], lambda i: (layer, 0, 0)),
            pl.BlockSpec((None,) + mods.shape[1:], lambda i: (layer, 0, 0)),
            pl.BlockSpec(w_pool.shape, lambda i: (0, 0, 0)),
            pl.BlockSpec((1, d), lambda i: (0, 0)),
        ],
        out_specs=pl.BlockSpec((tm, d), lambda i: (i, 0)),
        out_shape=jax.ShapeDtypeStruct((m, d), F32),
        scratch_shapes=[pltpu.VMEM((tm + 2 * POOL_HALO, d), F32)]
        + [pltpu.VMEM((tm + 2 * POOL_HALO, d // len(POOL_WINDOWS)), F32)] * 2,
        compiler_params=_cparams("parallel"),
        name="pool_layer",
    )(x, x, x, norm_g, mods, w_pool, pool_scale.reshape(1, d))


SWIGLU_SUB = 256


def _swiglu_partial(h, w1_ref, w3_ref, w2_ref, lo, hi):
    assert (hi - lo) % SWIGLU_SUB == 0
    acc = None
    for s in range(lo, hi, SWIGLU_SUB):
        cols = slice(s, s + SWIGLU_SUB)
        a = jnp.dot(h, w1_ref[:, cols], preferred_element_type=F32)
        b = jnp.dot(h, w3_ref[:, cols], preferred_element_type=F32)
        u = (a * jax.nn.sigmoid(a) * b).astype(BF16)
        p = jnp.dot(u, w2_ref[cols, :], preferred_element_type=F32)
        acc = p if acc is None else acc + p
    return acc


def _ffn_kernel(x_ref, g_ref, mod_ref, w1_ref, w3_ref, w2_ref, *rest, mod_row, n_cast, nf):
    cast_src, o_ref, cast_dst = rest[:n_cast], rest[n_cast], rest[n_cast + 1:2 * n_cast + 1]
    h_ref, acc_ref = rest[2 * n_cast + 1:]
    d = x_ref.shape[1]
    f = pl.program_id(1)
    for src, dst in zip(cast_src, cast_dst):
        dst[...] = src[...].astype(dst.dtype)

    assert nf >= 2
    n_sub = w1_ref.shape[1] // SWIGLU_SUB
    bounds = [-(-n_sub * c // nf) * SWIGLU_SUB for c in range(nf + 1)]
    for c in range(nf):
        @pl.when(f == c)
        def _(c=c):
            if c == 0:
                h = _norm_mod(x_ref[...], g_ref[2:3, :], _mod_slice(mod_ref, mod_row, SH2, d),
                              _mod_slice(mod_ref, mod_row, SC2, d)).astype(BF16)
                h_ref[...] = h
            else:
                h = h_ref[...]
            part = _swiglu_partial(h, w1_ref, w3_ref, w2_ref, bounds[c], bounds[c + 1])
            if c == 0:
                acc_ref[...] = part
            elif c < nf - 1:
                acc_ref[...] += part
            else:
                y = acc_ref[...] + part
                o_ref[...] = x_ref[...] + _mod_slice(mod_ref, mod_row, GA2, d) * _rms(y, g_ref[3:4, :])


def ffn_layer(x, norm_g, mods, layer, w1, w3, w2, *, mod_row, tm, tf, cast=None):
    m, d = x.shape
    ff = w1.shape[1]
    tm = min(tm, m)
    nf = ff // tf
    steps = (m // tm) * nf
    in_specs = [
        pl.BlockSpec((tm, d), lambda i, f: (i, 0)),
        pl.BlockSpec((None,) + norm_g.shape[1:], lambda i, f: (layer, 0, 0)),
        pl.BlockSpec((None,) + mods.shape[1:], lambda i, f: (layer, 0, 0)),
        pl.BlockSpec((d, ff), lambda i, f: (0, 0), pipeline_mode=pl.Buffered(1)),
        pl.BlockSpec((d, ff), lambda i, f: (0, 0), pipeline_mode=pl.Buffered(1)),
        pl.BlockSpec((ff, d), lambda i, f: (0, 0), pipeline_mode=pl.Buffered(1)),
    ]
    out_specs = [pl.BlockSpec((tm, d), lambda i, f: (i, 0))]
    out_shape = [jax.ShapeDtypeStruct((m, d), F32)]
    cast_arrays, cast_idx = cast if cast is not None else ((), 0)
    for a in cast_arrays:
        _, ne, r, c = a.shape
        per_expert = steps // ne
        assert per_expert * ne == steps and r % (per_expert * 16) == 0
        rows = r // per_expert
        in_specs.append(pl.BlockSpec(
            (None, None, rows, c),
            lambda i, f, pe=per_expert: (cast_idx, (i * nf + f) // pe, (i * nf + f) % pe, 0)))
        out_specs.append(pl.BlockSpec(
            (None, rows, c), lambda i, f, pe=per_expert: ((i * nf + f) // pe, (i * nf + f) % pe, 0)))
        out_shape.append(jax.ShapeDtypeStruct((ne, r, c), BF16))
    outs = pl.pallas_call(
        functools.partial(_ffn_kernel, mod_row=mod_row, n_cast=len(cast_arrays), nf=nf),
        grid=(m // tm, nf),
        in_specs=in_specs,
        out_specs=out_specs,
        out_shape=out_shape,
        scratch_shapes=[pltpu.VMEM((tm, d), BF16), pltpu.VMEM((tm, d), F32)],
        compiler_params=_cparams("arbitrary", "arbitrary"),
        name="ffn_layer",
    )(x, norm_g, mods, w1, w3, w2, *cast_arrays)
    return outs[0] if cast is None else (outs[0], tuple(outs[1:]))


RUN_ALIGN = 16
RUN_BITS = tuple(range(9, 3, -1))
EXPERT_ROWS = 512
MOE_TOKEN_TILE = 512
TAIL_BITS = tuple(range(9, 3, -1))


def _split_dma(length, src, src_row, dst, dst_row, sem, bits, *, start, src_advances=True):
    for b in bits:
        size = 1 << b
        off = (length >> (b + 1)) << (b + 1)

        @pl.when(((length >> b) & 1) == 1)
        def _():
            s_row = pl.multiple_of(src_row + off, RUN_ALIGN) if src_advances else src_row
            cp = pltpu.make_async_copy(src.at[pl.ds(s_row, size)],
                                       dst.at[pl.ds(pl.multiple_of(dst_row + off, RUN_ALIGN), size)], sem)
            if start:
                cp.start()
            else:
                cp.wait()


def _run_dmas(tile, ne, lpad_ref, loff_ref, base_ref, buf, hbm_ref, sem, *, to_hbm, start):
    for e in range(ne):
        length = lpad_ref[tile * ne + e]
        lo = loff_ref[tile * ne + e]
        gb = base_ref[tile * ne + e]
        if to_hbm:
            _split_dma(length, buf, lo, hbm_ref, gb, sem, RUN_BITS, start=start)
        else:
            _split_dma(length, hbm_ref, gb, buf, lo, sem, RUN_BITS, start=start)


def _split_bf16(a):
    hi = a.astype(BF16)
    return hi, (a - hi.astype(F32)).astype(BF16)


def _dot_f32x3(a, b_hi, b_lo):
    a_hi, a_lo = _split_bf16(a)
    dot = functools.partial(jnp.dot, preferred_element_type=F32)
    return dot(a_hi, b_hi) + (dot(a_hi, b_lo) + dot(a_lo, b_hi))


def _local_rows(ri_ref, loff_ref, tile, ne):
    e1, e2 = ri_ref[:, 0:1], ri_ref[:, 1:2]
    lo1, lo2 = ri_ref[:, 2:3], ri_ref[:, 3:4]
    for e in range(ne):
        off = loff_ref[tile * ne + e]
        lo1 = lo1 + jnp.where(e1 == e, off, 0)
        lo2 = lo2 + jnp.where(e2 == e, off, 0)
    return lo1, lo2


def _router_kernel(x_ref, g_ref, mod_ref, wr_hi_ref, wr_lo_ref, br_ref, ri_ref, rw_ref, cnt_ref, *, mod_row):
    _route_rows(x_ref[...], g_ref, mod_ref, mod_row, wr_hi_ref, wr_lo_ref, br_ref, ri_ref, rw_ref, cnt_ref)


def _route_rows(x, g_ref, mod_ref, mod_row, wr_hi_ref, wr_lo_ref, br_ref, ri_ref, rw_ref, cnt_ref):
    tm, d = x.shape
    ne = wr_hi_ref.shape[1]
    h = _norm_mod(x, g_ref[2:3, :], _mod_slice(mod_ref, mod_row, SH2, d), _mod_slice(mod_ref, mod_row, SC2, d))
    logits = _dot_f32x3(h, wr_hi_ref[...], wr_lo_ref[...]) + br_ref[...]
    lane = lax.broadcasted_iota(jnp.int32, logits.shape, 1)
    v1 = jnp.max(logits, axis=-1, keepdims=True)
    i1 = jnp.min(jnp.where(logits == v1, lane, ne), axis=-1, keepdims=True)
    rest = jnp.where(lane == i1, -jnp.inf, logits)
    v2 = jnp.max(rest, axis=-1, keepdims=True)
    i2 = jnp.min(jnp.where(rest == v2, lane, ne), axis=-1, keepdims=True)
    e2 = jnp.exp(v2 - v1)
    den = 1.0 + e2
    oh1 = (lane == i1).astype(F32)
    oh2 = (lane == i2).astype(F32)
    both = oh1 + oh2
    r = lax.broadcasted_iota(jnp.int32, (tm, tm), 0)
    c = lax.broadcasted_iota(jnp.int32, (tm, tm), 1)
    tri = jnp.where(c < r, 1.0, 0.0).astype(BF16)
    rank_all = jnp.dot(tri, both.astype(BF16), preferred_element_type=F32)
    rank1 = jnp.sum(rank_all * oh1, axis=-1, keepdims=True).astype(jnp.int32)
    rank2 = jnp.sum(rank_all * oh2, axis=-1, keepdims=True).astype(jnp.int32)
    col = lax.broadcasted_iota(jnp.int32, ri_ref.shape, 1)
    ri_ref[...] = jnp.where(col == 0, i1, jnp.where(col == 1, i2, jnp.where(col == 2, rank1, rank2)))
    colw = lax.broadcasted_iota(jnp.int32, rw_ref.shape, 1)
    rw_ref[...] = jnp.where(colw == 0, 1.0 / den, e2 / den)
    cnt_ref[...] = jnp.sum(both, axis=0, keepdims=True).astype(jnp.int32)


def moe_router(x, norm_g, mods, layer, w_router, b_router, *, mod_row, tm):
    m, d = x.shape
    ne = w_router.shape[1]
    return pl.pallas_call(
        functools.partial(_router_kernel, mod_row=mod_row),
        grid=(m // tm,),
        in_specs=[
            pl.BlockSpec((tm, d), lambda i: (i, 0)),
            pl.BlockSpec((None,) + norm_g.shape[1:], lambda i: (layer, 0, 0)),
            pl.BlockSpec((None,) + mods.shape[1:], lambda i: (layer, 0, 0)),
            pl.BlockSpec((d, ne), lambda i: (0, 0)),
            pl.BlockSpec((d, ne), lambda i: (0, 0)),
            pl.BlockSpec((1, ne), lambda i: (0, 0)),
        ],
        out_specs=[
            pl.BlockSpec((tm, 4), lambda i: (i, 0)),
            pl.BlockSpec((tm, 2), lambda i: (i, 0)),
            pl.BlockSpec((None, 1, ne), lambda i: (i, 0, 0)),
        ],
        out_shape=[
            jax.ShapeDtypeStruct((m, 4), jnp.int32),
            jax.ShapeDtypeStruct((m, 2), F32),
            jax.ShapeDtypeStruct((m // tm, 1, ne), jnp.int32),
        ],
        compiler_params=_cparams("parallel"),
        name="moe_router",
    )(x, norm_g, mods, *_split_bf16(w_router), b_router.reshape(1, ne))


def _dispatch_kernel(wlen_ref, loff_ref, base_ref, lrow_ref, carry_ref, prev_ref, tlen_ref, tstart_ref, nu_ref,
                     x_ref, g_ref, mod_ref, ri_ref, *rest, mod_row, ne, tile0, first):
    xs_ref, cbuf, zbuf, sem = rest if first else rest[1:]
    tm, d = x_ref.shape
    rc = cbuf.shape[1]
    i = pl.program_id(0)
    last = i == pl.num_programs(0) - 1
    slot = i % 2

    def zero_tails(start):
        for e in range(ne):
            _split_dma(tlen_ref[e], zbuf, 0, xs_ref, tstart_ref[e], sem.at[2], TAIL_BITS, start=start,
                       src_advances=False)

        def unused_tile(t, carry):
            cp = pltpu.make_async_copy(
                zbuf, xs_ref.at[pl.ds(pl.multiple_of(t * EXPERT_ROWS, EXPERT_ROWS), EXPERT_ROWS)], sem.at[2])
            if start:
                cp.start()
            else:
                cp.wait()
            return carry

        lax.fori_loop(nu_ref[0], xs_ref.shape[0] // EXPERT_ROWS, unused_tile, 0)

    if first:
        @pl.when(i == 0)
        def _():
            zbuf[...] = jnp.zeros_like(zbuf)
            zero_tails(True)

    h = _norm_mod(x_ref[...], g_ref[2:3, :], _mod_slice(mod_ref, mod_row, SH2, d),
                  _mod_slice(mod_ref, mod_row, SC2, d)).astype(BF16)
    lo1, lo2 = _local_rows(ri_ref, lrow_ref, i + tile0, ne)
    lane = lax.broadcasted_iota(jnp.int32, (tm, rc), 1)
    onehot = jnp.where((lane == lo1) | (lane == lo2), 1.0, 0.0).astype(BF16)
    sorted_rows = lax.dot_general(onehot, h, (((0,), (0,)), ((), ())), preferred_element_type=F32)
    cbuf[slot] = sorted_rows.astype(BF16)
    for e in range(ne):
        @pl.when(carry_ref[(i + tile0) * ne + e] > 0)
        def _(e=e):
            here = pl.ds(pl.multiple_of(loff_ref[(i + tile0) * ne + e], RUN_ALIGN), RUN_ALIGN)
            prev = pl.ds(pl.multiple_of(prev_ref[(i + tile0) * ne + e], RUN_ALIGN), RUN_ALIGN)
            cbuf[slot, here, :] = (cbuf[slot, here, :].astype(F32)
                                   + cbuf[1 - slot, prev, :].astype(F32)).astype(BF16)

    moves = functools.partial(_run_dmas, ne=ne, lpad_ref=wlen_ref, loff_ref=loff_ref, base_ref=base_ref,
                              hbm_ref=xs_ref, to_hbm=True)
    moves(i + tile0, buf=cbuf.at[slot], sem=sem.at[slot], start=True)

    @pl.when(i > 0)
    def _():
        moves(i + tile0 - 1, buf=cbuf.at[1 - slot], sem=sem.at[1 - slot], start=False)

    @pl.when(last)
    def _():
        moves(i + tile0, buf=cbuf.at[slot], sem=sem.at[slot], start=False)
        if first:
            zero_tails(False)


def moe_dispatch(x, norm_g, mods, layer, route_i, tabs, rows, xs=None, *, mod_row, tm, ne, tile0):
    m, d = x.shape
    rc = 2 * tm + ne * 2 * RUN_ALIGN
    first = xs is None
    in_specs = [
        pl.BlockSpec((tm, d), lambda i, *_: (i, 0)),
        pl.BlockSpec((None,) + norm_g.shape[1:], lambda i, *_: (layer, 0, 0)),
        pl.BlockSpec((None,) + mods.shape[1:], lambda i, *_: (layer, 0, 0)),
        pl.BlockSpec((tm, 4), lambda i, *_: (i, 0)),
    ]
    args = [*tabs, x, norm_g, mods, route_i]
    if not first:
        in_specs.append(pl.BlockSpec(memory_space=pl.ANY))
        args.append(xs)
    return pl.pallas_call(
        functools.partial(_dispatch_kernel, mod_row=mod_row, ne=ne, tile0=tile0, first=first),
        grid_spec=pltpu.PrefetchScalarGridSpec(
            num_scalar_prefetch=len(tabs),
            grid=(m // tm,),
            in_specs=in_specs,
            out_specs=pl.BlockSpec(memory_space=pl.ANY),
            scratch_shapes=[pltpu.VMEM((2, rc, d), BF16), pltpu.VMEM((EXPERT_ROWS, d), BF16),
                            pltpu.SemaphoreType.DMA((3,))],
        ),
        out_shape=jax.ShapeDtypeStruct((rows, d), BF16),
        input_output_aliases={} if first else {len(args) - 1: 0},
        compiler_params=_cparams("arbitrary"),
        name="moe_dispatch",
    )(*args)


EXPERT_ROW_STEP = 128


def _experts_kernel(te_ref, nu_ref, tr_ref, x_ref, w1_ref, w3_ref, w2_ref, y_ref, acc_ref, *, n_chunks):
    del te_ref, nu_ref
    i = pl.program_id(0)
    f = pl.program_id(1)
    nf = pl.num_programs(1)
    tm = x_ref.shape[0]
    live_rows = tr_ref[i]
    steps = (live_rows + EXPERT_ROW_STEP - 1) // EXPERT_ROW_STEP

    for c in range(n_chunks):
        for q in range(1, tm // EXPERT_ROW_STEP + 1):
            @pl.when((f == c) & (steps == q))
            def _(c=c, rows=q * EXPERT_ROW_STEP):
                part = _swiglu_partial(x_ref[0:rows, :], w1_ref, w3_ref, w2_ref, 0, w1_ref.shape[1])
                if c == 0:
                    acc_ref[0:rows, :] = part
                elif c < n_chunks - 1:
                    acc_ref[0:rows, :] += part
                else:
                    y_ref[0:rows, :] = (acc_ref[0:rows, :] + part).astype(y_ref.dtype)
                    if rows < tm:
                        y_ref[rows:, :] = jnp.zeros((tm - rows, y_ref.shape[1]), y_ref.dtype)

    @pl.when((steps == 0) & (f == nf - 1))
    def _():
        y_ref[...] = jnp.zeros_like(y_ref)


def moe_experts(xs, tile_expert, n_used, tile_rows, w1, w3, w2, *, tm, tf):
    rows, d = xs.shape
    ff = w1.shape[2]
    nf = ff // tf
    assert nf >= 2

    def chunk(i, f, nu):
        return jnp.where(i < nu[0], f, nf - 1)

    def wmap_in(i, f, te, nu, tr):
        return (te[i], 0, chunk(i, f, nu))

    def wmap_out(i, f, te, nu, tr):
        return (te[i], chunk(i, f, nu), 0)

    def xmap(i, f, te, nu, tr):
        return (jnp.maximum(jnp.minimum(i, nu[0] - 1), 0), 0)

    return pl.pallas_call(
        functools.partial(_experts_kernel, n_chunks=nf),
        grid_spec=pltpu.PrefetchScalarGridSpec(
            num_scalar_prefetch=3,
            grid=(rows // tm, nf),
            in_specs=[
                pl.BlockSpec((tm, d), xmap),
                pl.BlockSpec((None, d, tf), wmap_in),
                pl.BlockSpec((None, d, tf), wmap_in),
                pl.BlockSpec((None, tf, d), wmap_out),
            ],
            out_specs=pl.BlockSpec((tm, d), lambda i, f, te, nu, tr: (i, 0)),
            scratch_shapes=[pltpu.VMEM((tm, d), F32)],
        ),
        out_shape=jax.ShapeDtypeStruct((rows, d), BF16),
        compiler_params=_cparams("arbitrary", "arbitrary"),
        name="moe_experts",
    )(tile_expert, n_used, tile_rows, xs, w1, w3, w2)


def _combine_kernel(lpad_ref, loff_ref, base_ref, lrow_ref, x_ref, ri_ref, rw_ref, g_ref, mod_ref, ys_ref, o_ref,
                    ybuf, sem, *, mod_row, ne, tile0):
    tm, d = x_ref.shape
    rc = ybuf.shape[1]
    i = pl.program_id(0)
    slot = i % 2
    moves = functools.partial(_run_dmas, ne=ne, lpad_ref=lpad_ref, loff_ref=loff_ref, base_ref=base_ref,
                              hbm_ref=ys_ref, to_hbm=False)

    @pl.when(i == 0)
    def _():
        ybuf[...] = jnp.zeros_like(ybuf)
        moves(tile0, buf=ybuf.at[0], sem=sem.at[0], start=True)

    @pl.when(i + 1 < pl.num_programs(0))
    def _():
        moves(i + tile0 + 1, buf=ybuf.at[1 - slot], sem=sem.at[1 - slot], start=True)

    moves(i + tile0, buf=ybuf.at[slot], sem=sem.at[slot], start=False)
    lo1, lo2 = _local_rows(ri_ref, lrow_ref, i + tile0, ne)
    lane = lax.broadcasted_iota(jnp.int32, (tm, rc), 1)
    rows = ybuf[slot]
    y1 = jnp.dot(jnp.where(lane == lo1, 1.0, 0.0).astype(BF16), rows, preferred_element_type=F32)
    y2 = jnp.dot(jnp.where(lane == lo2, 1.0, 0.0).astype(BF16), rows, preferred_element_type=F32)
    y = rw_ref[:, 0:1] * y1 + rw_ref[:, 1:2] * y2
    o_ref[...] = x_ref[...] + _mod_slice(mod_ref, mod_row, GA2, d) * _rms(y, g_ref[3:4, :])


def moe_combine(x, norm_g, mods, layer, route_i, route_w, ys, tabs, *, mod_row, tm, ne, tile0):
    m, d = x.shape
    rc = 2 * tm + ne * 2 * RUN_ALIGN
    return pl.pallas_call(
        functools.partial(_combine_kernel, mod_row=mod_row, ne=ne, tile0=tile0),
        grid_spec=pltpu.PrefetchScalarGridSpec(
            num_scalar_prefetch=len(tabs),
            grid=(m // tm,),
            in_specs=[
                pl.BlockSpec((tm, d), lambda i, *_: (i, 0)),
                pl.BlockSpec((tm, 4), lambda i, *_: (i, 0)),
                pl.BlockSpec((tm, 2), lambda i, *_: (i, 0)),
                pl.BlockSpec((None,) + norm_g.shape[1:], lambda i, *_: (layer, 0, 0)),
                pl.BlockSpec((None,) + mods.shape[1:], lambda i, *_: (layer, 0, 0)),
                pl.BlockSpec(memory_space=pl.ANY),
            ],
            out_specs=pl.BlockSpec((tm, d), lambda i, *_: (i, 0)),
            scratch_shapes=[pltpu.VMEM((2, rc, d), BF16), pltpu.SemaphoreType.DMA((2,))],
        ),
        out_shape=jax.ShapeDtypeStruct((m, d), F32),
        compiler_params=_cparams("arbitrary"),
        name="moe_combine",
    )(*tabs, x, route_i, route_w, norm_g, mods, ys)


def sparse_moe_layer(streams, norm_g, mods, layer, w_router, b_router, w1, w3, w2, *, tf, routed_first=None):
    ne = w_router.shape[1]
    tm_exp = EXPERT_ROWS
    routed = [moe_router(x, norm_g, mods, layer, w_router, b_router, mod_row=row, tm=tm)
              if k > 0 or routed_first is None else routed_first for k, (x, row, tm) in enumerate(streams)]
    align = lambda a: (a + RUN_ALIGN - 1) // RUN_ALIGN * RUN_ALIGN
    per_stream = [r[2][:, 0, :] for r in routed]
    counts = jnp.concatenate(per_stream, axis=0)
    nt = counts.shape[0]
    m_total = sum(x.shape[0] for x, _, _ in streams)
    stream_rows = [align(jnp.sum(c, axis=0)) for c in per_stream]
    stream_off = jnp.cumsum(jnp.stack(stream_rows), axis=0) - jnp.stack(stream_rows)
    pos = jnp.concatenate([stream_off[k][None, :] + jnp.cumsum(c, axis=0) - c for k, c in enumerate(per_stream)])
    carry = pos % RUN_ALIGN
    lpad = align(carry + counts)
    stream_end = jnp.concatenate([jnp.arange(c.shape[0]) == c.shape[0] - 1 for c in per_stream])[:, None]
    wlen = jnp.where(stream_end, lpad, (carry + counts) // RUN_ALIGN * RUN_ALIGN)
    loff = jnp.cumsum(lpad, axis=1) - lpad
    prev_block = jnp.concatenate([jnp.zeros((1, ne), loff.dtype), (loff + lpad - RUN_ALIGN)[:-1]], axis=0)
    group = sum(stream_rows)
    gpad = (group + tm_exp - 1) // tm_exp * tm_exp
    ends = jnp.cumsum(gpad)
    starts = ends - gpad
    base = starts[None, :] + pos - carry
    n_tiles = -(-(2 * m_total + len(streams) * ne * (RUN_ALIGN - 1)) // tm_exp) + ne
    tile_start = jnp.arange(n_tiles, dtype=jnp.int32) * tm_exp
    n_used = (ends[-1] // tm_exp).astype(jnp.int32).reshape(1)
    tile_expert = jnp.sum(tile_start[:, None] >= ends[None, :], axis=1).astype(jnp.int32)
    tile_expert = jnp.minimum(tile_expert, tile_expert[jnp.maximum(n_used[0] - 1, 0)])
    tile_rows = jnp.clip((starts + group)[tile_expert] - tile_start, 0, tm_exp).astype(jnp.int32)
    group_first = stream_rows[0]
    later_rows = m_total - streams[0][0].shape[0] + (len(streams) - 1) * (RUN_ALIGN - 1)
    assert later_rows + tm_exp - RUN_ALIGN < 2 << TAIL_BITS[0]
    flat = lambda *arrays: [a.reshape(-1).astype(jnp.int32) for a in arrays]
    tabs = flat(lpad, loff, base, loff + carry)
    write_tabs = flat(wlen, loff, base, loff + carry, carry, prev_block,
                      gpad - group_first, starts + group_first, n_used)
    xs, tile0 = None, 0
    for (x, row, tm), (route_i, _, _) in zip(streams, routed):
        xs = moe_dispatch(x, norm_g, mods, layer, route_i, write_tabs, n_tiles * tm_exp, xs,
                          mod_row=row, tm=tm, ne=ne, tile0=tile0)
        tile0 += x.shape[0] // tm
    ys = moe_experts(xs, tile_expert, n_used, tile_rows, w1, w3, w2, tm=tm_exp, tf=tf)
    outs, tile0 = [], 0
    for (x, row, tm), (route_i, route_w, _) in zip(streams, routed):
        outs.append(moe_combine(x, norm_g, mods, layer, route_i, route_w, ys, tabs,
                                mod_row=row, tm=tm, ne=ne, tile0=tile0))
        tile0 += x.shape[0] // tm
    return outs


def _rope_tables(seq):
    n = 16
    rows = seq // GRID_W
    inv = ROPE_BASE ** (-jnp.arange(n, dtype=F32) / n)
    row_ang = jnp.arange(rows, dtype=F32)[:, None] * inv[None, :]
    col_ang = jnp.arange(GRID_W, dtype=F32)[:, None] * inv[None, :]
    per_row = lambda a: jnp.repeat(a, GRID_W, axis=0)
    per_col = lambda a: jnp.tile(a, (rows, 1))
    cos_r, sin_r = per_row(jnp.cos(row_ang)), per_row(jnp.sin(row_ang))
    cos_c, sin_c = per_col(jnp.cos(col_ang)), per_col(jnp.sin(col_ang))
    cos = jnp.concatenate([cos_r, cos_r, cos_c, cos_c], axis=1)
    sin = jnp.concatenate([-sin_r, sin_r, -sin_c, sin_c], axis=1)
    return jnp.tile(cos, (1, 2)), jnp.tile(sin, (1, 2))


def kernel(x, c, ctx, c_ctx, w_ada, b_ada, norm_g, attn_w_qkv, attn_w_o, attn_sink, ret_w_in, ret_w_o,
           ret_log_decay, pool_w, pool_scale, ffn_w1, ffn_w3, ffn_w2, moe_w_router, moe_b_router,
           moe_w1, moe_w3, moe_w2):
    batch, seq, d = x.shape
    assert batch == 1 and c.shape[0] == 1
    depth = w_ada.shape[0]
    lc = ctx.shape[1]
    xl = x.reshape(seq, d)
    xc = ctx.reshape(lc, d)

    cvecs = jnp.zeros((8, d), F32).at[LAT_ROW].set(c[0]).at[CTX_ROW].set(c_ctx)
    mods = ada_table(cvecs, w_ada, b_ada)
    rope = _rope_tables(seq)

    hd = attn_w_o.shape[1]
    dh = hd // ATTN_HEADS
    qkv_scale = jnp.concatenate([jnp.full((hd,), dh ** -0.5, F32),
                                 jnp.ones((attn_w_qkv.shape[2] - hd,), F32)])
    dk = d // RET_HEADS
    ret_scale = jnp.concatenate([jnp.ones((d,), F32), jnp.full((d,), dk ** -0.5, F32),
                                 jnp.ones((ret_w_in.shape[2] - 2 * d,), F32)])

    moe_bf16 = None
    for i in range(depth):
        last = i == depth - 1
        kind, j = i % N_MIXERS, i // N_MIXERS
        proj = functools.partial(mixer_projection, norm_g=norm_g, mods=mods, layer=i)
        routed_lat = None
        if i % 2 == 1:
            mix_out = functools.partial(outproj_residual, norm_g=norm_g, mods=mods, layer=i, g_row=1,
                                        mod_row=LAT_ROW, slot=GA1, tm=min(MOE_TOKEN_TILE, seq),
                                        router=(moe_w_router[i // 2], moe_b_router[i // 2]))
        else:
            mix_out = lambda x, **kw: (outproj_residual(x, norm_g, mods, i, g_row=1, mod_row=LAT_ROW,
                                                        slot=GA1, tm=1024, **kw), None)
        if kind == 0:
            w_qkv = (attn_w_qkv[j] * qkv_scale).astype(BF16)
            w_o = attn_w_o[j].astype(BF16)
            kvd = (w_qkv.shape[1] - hd) // 2
            qkv_l = proj(xl, w=w_qkv, mod_row=LAT_ROW, tm=512, chunk=256, q_cols=hd, rope=rope, rope_cols=hd + kvd)
            qkv_c = proj(xc, w=w_qkv, mod_row=CTX_ROW, tm=256, chunk=256, q_cols=hd)
            o_l = attention(attn_sink[j], qkv_l, qkv_c, has_local=True)
            xl, routed_lat = mix_out(xl, w=w_o, ys=[o_l])
            if not last:
                o_c = attention(attn_sink[j], qkv_c, qkv_c, has_local=False)
                xc = outproj_residual(xc, norm_g, mods, i, w_o, [o_c], g_row=1, mod_row=CTX_ROW, slot=GA1, tm=256)
        elif kind == 1:
            w_in = (ret_w_in[j] * ret_scale).astype(BF16)
            w_o = ret_w_o[j].astype(BF16)
            p_c = proj(xc, w=w_in, mod_row=CTX_ROW, tm=256, chunk=1024)
            p_l = proj(xl, w=w_in, mod_row=LAT_ROW, tm=512, chunk=1024)
            s0 = jnp.zeros((RET_HEADS, dk, 2 * dk), F32)
            tabs_f = retention_tables(ret_log_decay[j, 0], min(RET_CHUNK, lc), False)
            tabs_b = retention_tables(ret_log_decay[j, 1], min(RET_CHUNK, lc), True)
            zf_c, zb_c, s_f, s_b = retention_scan(p_c, tabs_f, tabs_b, s0, s0)
            zf_l, zb_l, _, _ = retention_scan(p_l, tabs_f, tabs_b, s_f, s_b)
            xl, routed_lat = mix_out(xl, w=w_o, ys=[zf_l, zb_l])
            if not last:
                xc = outproj_residual(xc, norm_g, mods, i, w_o, [zf_c, zb_c], g_row=1, mod_row=CTX_ROW, slot=GA1, tm=256)
        else:
            w_p = pool_w[j].astype(BF16)
            xl = pool_layer(xl, norm_g, mods, i, w_p, pool_scale[j], mod_row=LAT_ROW, tm=1024)
            if not last:
                xc = pool_layer(xc, norm_g, mods, i, w_p, pool_scale[j], mod_row=CTX_ROW, tm=256)

        f = i // 2
        if i % 2 == 0:
            w1, w3, w2 = ffn_w1[f].astype(BF16), ffn_w3[f].astype(BF16), ffn_w2[f].astype(BF16)
            tf = w1.shape[1] // 2
            steps = (seq // min(512, seq)) * 2
            ne = moe_w1.shape[1]
            if not last and steps % ne == 0 and d % (steps // ne * 16) == 0:
                xl, moe_bf16 = ffn_layer(xl, norm_g, mods, i, w1, w3, w2, mod_row=LAT_ROW, tm=512, tf=tf,
                                         cast=((moe_w1, moe_w3, moe_w2), (i + 1) // 2))
            else:
                xl = ffn_layer(xl, norm_g, mods, i, w1, w3, w2, mod_row=LAT_ROW, tm=512, tf=tf)
            if not last:
                xc = ffn_layer(xc, norm_g, mods, i, w1, w3, w2, mod_row=CTX_ROW, tm=256, tf=tf)
        else:
            if moe_bf16 is None:
                moe_bf16 = moe_w1[f].astype(BF16), moe_w3[f].astype(BF16), moe_w2[f].astype(BF16)
            (w1, w3, w2), moe_bf16 = moe_bf16, None
            tf = w1.shape[2] // 2
            streams = [(xl, LAT_ROW, min(MOE_TOKEN_TILE, seq))] + ([] if last else [(xc, CTX_ROW, min(256, lc))])
            outs = sparse_moe_layer(streams, norm_g, mods, i, moe_w_router[f], moe_b_router[f], w1, w3, w2, tf=tf,
                                    routed_first=routed_lat)
            xl = outs[0]
            if not last:
                xc = outs[1]
    return xl.reshape(batch, seq, d)
```

```python
import functools

import jax
import jax.numpy as jnp
import numpy as np
from jax import lax
from jax.experimental import pallas as pl
from jax.experimental.pallas import tpu as pltpu

F32 = jnp.float32
BF16 = jnp.bfloat16

EPS = 1e-6
NEG_INF = -1e30
LOG2E = 1.4426950408889634
LANES = 128
VMEM_LIMIT = 56 * 1024 * 1024

GRID_W = 64
N_MIXERS = 3
ATTN_HEADS = 16
ATTN_KV_HEADS = 4
ATTN_BLOCKS_PER_STEP = 4
WINDOW = 128
ROPE_BASE = 10000.0
RET_HEADS = 4
RET_CHUNK = 256
POOL_WINDOWS = (2, 4, 8, 16)
POOL_HALO = 8
N_EXPERTS = 8

SH1, SC1, GA1, SH2, SC2, GA2 = range(6)
LAT_ROW, CTX_ROW = 0, 1


def _cparams(*sem):
    return pltpu.CompilerParams(dimension_semantics=sem, vmem_limit_bytes=VMEM_LIMIT)


def _rms(x, g):
    return x * lax.rsqrt(jnp.mean(x * x, axis=-1, keepdims=True) + EPS) * g


def _mod_slice(mod_ref, row, slot, d):
    return mod_ref[row:row + 1, slot * d:(slot + 1) * d]


def _norm_mod(x, g, shift, scale):
    return _rms(x, g) * (1.0 + scale) + shift


def _ada_kernel(c_ref, w_ref, b_ref, o_ref):
    c = c_ref[...]
    s = c * jax.nn.sigmoid(c)
    o_ref[...] = _dot_f32x3(s, *_split_bf16(w_ref[...])) + b_ref[...]


def ada_table(cvecs, w_ada, b_ada):
    depth, d, n = w_ada.shape
    tn = 1536
    return pl.pallas_call(
        _ada_kernel,
        grid=(depth, n // tn),
        in_specs=[
            pl.BlockSpec((8, d), lambda i, j: (0, 0)),
            pl.BlockSpec((None, d, tn), lambda i, j: (i, 0, j)),
            pl.BlockSpec((None, 1, tn), lambda i, j: (i, 0, j)),
        ],
        out_specs=pl.BlockSpec((None, 8, tn), lambda i, j: (i, 0, j)),
        out_shape=jax.ShapeDtypeStruct((depth, 8, n), F32),
        compiler_params=_cparams("parallel", "parallel"),
        name="ada_table",
    )(cvecs, w_ada, b_ada.reshape(depth, 1, n))


def _rope(a, cos, sin):
    lane = lax.broadcasted_iota(jnp.int32, (a.shape[0], LANES), 1)
    first = (lane % 32) < 16
    outs = []
    for cb in range(a.shape[1] // LANES):
        blk = a[:, cb * LANES:(cb + 1) * LANES]
        partner = jnp.where(first, pltpu.roll(blk, LANES - 16, 1), pltpu.roll(blk, 16, 1))
        outs.append(blk * cos + partner * sin)
    return jnp.concatenate(outs, axis=1)


def _mixer_proj_kernel(x_ref, g_ref, mod_ref, w_ref, *rest, mod_row, chunk, rope_cols, q_cols):
    if rope_cols:
        cos_ref, sin_ref, o_ref = rest
        cos, sin = cos_ref[...], sin_ref[...]
    else:
        (o_ref,) = rest
    d = x_ref.shape[1]
    h = _norm_mod(x_ref[...], g_ref[0:1, :], _mod_slice(mod_ref, mod_row, SH1, d),
                  _mod_slice(mod_ref, mod_row, SC1, d)).astype(BF16)
    for c0 in range(0, o_ref.shape[1], chunk):
        acc = jnp.dot(h, w_ref[:, c0:c0 + chunk], preferred_element_type=F32)
        if c0 < q_cols:
            acc = acc * LOG2E
        if c0 < rope_cols:
            acc = _rope(acc, cos, sin)
        o_ref[:, c0:c0 + chunk] = acc.astype(o_ref.dtype)


def mixer_projection(x, norm_g, mods, layer, w, *, mod_row, tm, chunk, q_cols=0, rope=None, rope_cols=0):
    m, d = x.shape
    n = w.shape[1]
    tm = min(tm, m)
    assert n % chunk == 0 and rope_cols % chunk == 0 and q_cols % chunk == 0
    in_specs = [
        pl.BlockSpec((tm, d), lambda i: (i, 0)),
        pl.BlockSpec((None,) + norm_g.shape[1:], lambda i: (layer, 0, 0)),
        pl.BlockSpec((None,) + mods.shape[1:], lambda i: (layer, 0, 0)),
        pl.BlockSpec((d, n), lambda i: (0, 0), pipeline_mode=pl.Buffered(1)),
    ]
    args = [x, norm_g, mods, w]
    if rope_cols:
        in_specs += [pl.BlockSpec((tm, LANES), lambda i: (i, 0))] * 2
        args += list(rope)
    return pl.pallas_call(
        functools.partial(_mixer_proj_kernel, mod_row=mod_row, chunk=chunk, rope_cols=rope_cols, q_cols=q_cols),
        grid=(m // tm,),
        in_specs=in_specs,
        out_specs=pl.BlockSpec((tm, n), lambda i: (i, 0)),
        out_shape=jax.ShapeDtypeStruct((m, n), BF16),
        compiler_params=_cparams("parallel"),
        name="mixer_projection",
    )(*args)


def _outproj_kernel(x_ref, g_ref, mod_ref, w_ref, *rest, n_y, g_row, mod_row, slot, route):
    y_refs, rest = rest[:n_y], rest[n_y:]
    if route:
        router_refs, o_ref, route_out = rest[:3], rest[3], rest[4:]
    else:
        (o_ref,) = rest
    d = x_ref.shape[1]
    y = y_refs[0][...]
    if n_y == 2:
        y = (y.astype(F32) + y_refs[1][...].astype(F32)).astype(BF16)
    t = jnp.dot(y, w_ref[...], preferred_element_type=F32)
    gate = _mod_slice(mod_ref, mod_row, slot, d)
    x_new = x_ref[...] + gate * _rms(t, g_ref[g_row:g_row + 1, :])
    o_ref[...] = x_new
    if route:
        _route_rows(x_new, g_ref, mod_ref, mod_row, *router_refs, *route_out)


def _router_specs(m, d, ne, tm):
    ins = [pl.BlockSpec((d, ne), lambda i: (0, 0))] * 2 + [pl.BlockSpec((1, ne), lambda i: (0, 0))]
    outs = [pl.BlockSpec((tm, 4), lambda i: (i, 0)), pl.BlockSpec((tm, 2), lambda i: (i, 0)),
            pl.BlockSpec((None, 1, ne), lambda i: (i, 0, 0))]
    shapes = [jax.ShapeDtypeStruct((m, 4), jnp.int32), jax.ShapeDtypeStruct((m, 2), F32),
              jax.ShapeDtypeStruct((m // tm, 1, ne), jnp.int32)]
    return ins, outs, shapes


def outproj_residual(x, norm_g, mods, layer, w, ys, *, g_row, mod_row, slot, tm, router=None):
    m, d = x.shape
    k = w.shape[0]
    tm = min(tm, m)
    in_specs = [
        pl.BlockSpec((tm, d), lambda i: (i, 0)),
        pl.BlockSpec((None,) + norm_g.shape[1:], lambda i: (layer, 0, 0)),
        pl.BlockSpec((None,) + mods.shape[1:], lambda i: (layer, 0, 0)),
        pl.BlockSpec((k, d), lambda i: (0, 0)),
    ] + [pl.BlockSpec((tm, k), lambda i: (i, 0))] * len(ys)
    args = [x, norm_g, mods, w, *ys]
    out_specs = [pl.BlockSpec((tm, d), lambda i: (i, 0))]
    out_shape = [jax.ShapeDtypeStruct((m, d), F32)]
    if router is not None:
        w_router, b_router = router
        ne = w_router.shape[1]
        r_in, r_out, r_shapes = _router_specs(m, d, ne, tm)
        in_specs += r_in
        args += [*_split_bf16(w_router), b_router.reshape(1, ne)]
        out_specs += r_out
        out_shape += r_shapes
    outs = pl.pallas_call(
        functools.partial(_outproj_kernel, n_y=len(ys), g_row=g_row, mod_row=mod_row, slot=slot,
                          route=router is not None),
        grid=(m // tm,),
        in_specs=in_specs,
        out_specs=out_specs,
        out_shape=out_shape,
        compiler_params=_cparams("parallel"),
        name="outproj_residual",
    )(*args)
    return outs[0] if router is None else (outs[0], tuple(outs[1:]))


def _dot_nt(a, b):
    return lax.dot_general(a, b, (((1,), (1,)), ((), ())), preferred_element_type=F32)


def _attn_kernel(sink_ref, q_ref, *rest, tq, nsub, kv_heads, group, dh, has_local):
    if has_local:
        k_refs, v_refs = rest[:nsub + 2], rest[nsub + 2:2 * nsub + 4]
    kc_ref, vc_ref, o_ref = rest[-3:]
    assert 2 * dh == LANES and tq == WINDOW and group % 2 == 0
    b = pl.program_id(0)
    nb = pl.num_programs(0)
    rows = group * tq
    lc = kc_ref.shape[0]
    head_of_row = lax.broadcasted_iota(jnp.int32, (rows, 1), 0) // tq
    lane = lax.broadcasted_iota(jnp.int32, (1, LANES), 1)
    if has_local:
        r = lax.broadcasted_iota(jnp.int32, (rows, tq), 0) % tq
        c = lax.broadcasted_iota(jnp.int32, (rows, tq), 1)
    for sub, j in [(sub, j) for sub in range(nsub) for j in range(kv_heads)]:
        qrows = slice(sub * tq, (sub + 1) * tq)
        hs = slice(j * dh, (j + 1) * dh)
        pair = slice((j // 2) * LANES, (j // 2 + 1) * LANES)
        v_low = j % 2 == 0
        keep_v = (lane < dh) if v_low else (lane >= dh)
        qg = jnp.concatenate(
            [q_ref[qrows, (j * group + g) * dh:(j * group + g + 1) * dh] for g in range(group)], axis=0)
        sink = jnp.full((rows, 1), sink_ref[j * group] * LOG2E, F32)
        for g in range(1, group):
            sink = jnp.where(head_of_row == g, sink_ref[j * group + g] * LOG2E, sink)
        if has_local:
            kp_ref, kq_ref, kn_ref = k_refs[sub:sub + 3]
            vp_ref, vq_ref, vn_ref = v_refs[sub:sub + 3]
            prev_ok = (c >= r) if sub > 0 else (c >= r) & (b > 0)
            next_ok = (c <= r) if sub < nsub - 1 else (c <= r) & (b < nb - 1)
            k_all = jnp.concatenate([kc_ref[:, hs], kp_ref[:, hs], kq_ref[:, hs], kn_ref[:, hs]], axis=0)
            v_all = jnp.concatenate([vc_ref[:, pair], vp_ref[:, pair], vq_ref[:, pair], vn_ref[:, pair]], axis=0)
            s = _dot_nt(qg, k_all)
            s = jnp.concatenate([s[:, :lc],
                                 jnp.where(prev_ok, s[:, lc:lc + tq], NEG_INF),
                                 s[:, lc + tq:lc + 2 * tq],
                                 jnp.where(next_ok, s[:, lc + 2 * tq:], NEG_INF)], axis=1)
        else:
            v_all = vc_ref[:, pair]
            s = _dot_nt(qg, kc_ref[:, hs])
        m = jnp.maximum(jnp.max(s, axis=-1, keepdims=True), sink)
        p = jnp.exp2(s - m).astype(BF16)
        v_ones = jnp.where(keep_v, v_all, jnp.ones_like(v_all))
        pv = jnp.dot(p, v_ones, preferred_element_type=F32)
        den = pltpu.roll(pv, dh, 1) + jnp.exp2(sink - m)
        o = pv / den
        for g in range(0, group, 2):
            a, bb = o[g * tq:(g + 1) * tq], o[(g + 1) * tq:(g + 2) * tq]
            if v_low:
                both = jnp.where(lane < dh, a, pltpu.roll(bb, dh, 1))
            else:
                both = jnp.where(lane < dh, pltpu.roll(a, dh, 1), bb)
            h = j * group + g
            o_ref[qrows, h * dh:(h + 2) * dh] = both.astype(o_ref.dtype)


def attention(sink, qkv, qkv_ctx, *, has_local):
    s = qkv.shape[0]
    dh = qkv.shape[1] // (ATTN_HEADS + 2 * ATTN_KV_HEADS)
    hd = ATTN_HEADS * dh
    kvd = ATTN_KV_HEADS * dh
    kcol, vcol = hd // kvd, hd // kvd + 1
    tq = WINDOW
    nb = s // tq
    nsub = min(ATTN_BLOCKS_PER_STEP, nb)
    lc = qkv_ctx.shape[0]
    in_specs = [
        pl.BlockSpec(memory_space=pltpu.SMEM),
        pl.BlockSpec((nsub * tq, hd), lambda b: (b, 0)),
    ]
    args = [sink, qkv]
    if has_local:
        for col in (kcol, vcol):
            for off in range(-1, nsub + 1):
                in_specs.append(pl.BlockSpec(
                    (tq, kvd), lambda b, col=col, off=off: (jnp.clip(nsub * b + off, 0, nb - 1), col)))
                args.append(qkv)
    in_specs += [pl.BlockSpec((lc, kvd), lambda b: (0, kcol)), pl.BlockSpec((lc, kvd), lambda b: (0, vcol))]
    args += [qkv_ctx, qkv_ctx]
    return pl.pallas_call(
        functools.partial(_attn_kernel, tq=tq, nsub=nsub, kv_heads=ATTN_KV_HEADS,
                          group=ATTN_HEADS // ATTN_KV_HEADS, dh=dh, has_local=has_local),
        grid=(nb // nsub,),
        in_specs=in_specs,
        out_specs=pl.BlockSpec((nsub * tq, hd), lambda b: (b, 0)),
        out_shape=jax.ShapeDtypeStruct((s, hd), BF16),
        compiler_params=_cparams("parallel"),
        name="attention",
    )(*args)


N_RET_IN = 9


def _ret_kernel(*refs, heads, dk, dv):
    ins = [refs[:N_RET_IN], refs[N_RET_IN:2 * N_RET_IN]]
    outs = refs[2 * N_RET_IN:2 * N_RET_IN + 4]
    state_ref = refs[-1]
    i = pl.program_id(0)

    @pl.when(i == 0)
    def _():
        for dr in range(2):
            state_ref[dr] = ins[dr][8][...]

    for dr, h in [(dr, h) for h in range(heads) for dr in range(2)]:
        q_ref, k_ref, v_ref, gate_ref, intra_ref, qdec_ref, kdec_ref, cdec_ref, _ = ins[dr]
        z_ref = outs[dr]
        q = q_ref[:, h * dk:(h + 1) * dk]
        k = k_ref[:, h * dk:(h + 1) * dk]
        v = v_ref[:, h * dv:(h + 1) * dv]
        state = state_ref[dr, h]
        sc = _dot_nt(q, k) * intra_ref[h]
        qd = (q.astype(F32) * qdec_ref[h]).astype(BF16)
        o = (jnp.dot(sc.astype(BF16), v, preferred_element_type=F32)
             + jnp.dot(qd, state.astype(BF16), preferred_element_type=F32))
        kd = (k.astype(F32) * kdec_ref[h]).astype(BF16)
        state_ref[dr, h] = state * cdec_ref[h] + lax.dot_general(
            kd, v, (((0,), (0,)), ((), ())), preferred_element_type=F32)
        mu = jnp.mean(o, axis=-1, keepdims=True)
        oc = o - mu
        var = jnp.mean(oc * oc, axis=-1, keepdims=True)
        gate = gate_ref[:, h * dv:(h + 1) * dv].astype(F32)
        z = gate * jax.nn.sigmoid(gate) * (oc * lax.rsqrt(var + EPS))
        z_ref[:, h * dv:(h + 1) * dv] = z.astype(z_ref.dtype)

    @pl.when(i == pl.num_programs(0) - 1)
    def _():
        for dr in range(2):
            outs[2 + dr][...] = state_ref[dr]


def retention_scan(proj, tables_f, tables_b, s0_f, s0_b):
    m, n = proj.shape
    d = n // 8
    heads = RET_HEADS
    dk, dv = d // heads, 2 * d // heads
    c = min(RET_CHUNK, m)
    nc = m // c
    once = lambda a: pl.BlockSpec(a.shape, lambda i: (0,) * a.ndim, pipeline_mode=pl.Buffered(1))
    in_specs, args, out_specs = [], [], []
    for reverse, tables, s0 in ((False, tables_f, s0_f), (True, tables_b, s0_b)):
        row = (lambda i: nc - 1 - i) if reverse else (lambda i: i)
        in_specs += [
            pl.BlockSpec((c, d), lambda i, row=row: (row(i), 0)),
            pl.BlockSpec((c, d), lambda i, row=row: (row(i), 1)),
            pl.BlockSpec((c, 2 * d), lambda i, row=row: (row(i), 1)),
            pl.BlockSpec((c, 2 * d), lambda i, row=row, blk=3 if reverse else 2: (row(i), blk)),
        ] + [once(a) for a in (*tables, s0)]
        args += [proj] * 4 + [*tables, s0]
        out_specs.append(pl.BlockSpec((c, 2 * d), lambda i, row=row: (row(i), 0)))
    out_specs += [pl.BlockSpec(s0_f.shape, lambda i: (0,) * s0_f.ndim)] * 2
    return pl.pallas_call(
        functools.partial(_ret_kernel, heads=heads, dk=dk, dv=dv),
        grid=(nc,),
        in_specs=in_specs,
        out_specs=out_specs,
        out_shape=[jax.ShapeDtypeStruct((m, 2 * d), BF16)] * 2 + [jax.ShapeDtypeStruct(s0_f.shape, F32)] * 2,
        scratch_shapes=[pltpu.VMEM((2,) + s0_f.shape, F32)],
        compiler_params=_cparams("arbitrary"),
        name="retention_scan",
    )(*args)


def retention_tables(log_decay_row, c, reverse):
    lg = -jnp.exp(log_decay_row.astype(F32))
    idx = jnp.arange(c, dtype=F32)
    diff = idx[:, None] - idx[None, :]
    if reverse:
        diff = -diff
    intra = jnp.where(diff >= 0, jnp.exp(lg[:, None, None] * jnp.maximum(diff, 0.0)), 0.0)
    fwd_idx = (c - 1.0 - idx) if reverse else idx
    qdec = jnp.exp(lg[:, None] * (fwd_idx + 1.0))[:, :, None]
    kdec = jnp.exp(lg[:, None] * (c - 1.0 - fwd_idx))[:, :, None]
    cdec = jnp.exp(lg * c)[:, None, None]
    return intra, qdec, kdec, cdec


def _window_sum(h_ref, cols, tmp_refs, w, tm, halo):
    rows = tm + 2 * halo
    src, src_cols, span, k = h_ref, cols, 1, 0
    while 2 * span < w:
        n = rows - 2 * span + 1
        dst = tmp_refs[k % 2]
        dst[0:n, :] = src[0:n, src_cols] + src[span:span + n, src_cols]
        src, src_cols, span, k = dst, slice(None), 2 * span, k + 1
    lo = halo - w // 2
    return src[lo:lo + tm, src_cols] + src[lo + span:lo + span + tm, src_cols]


def _pool_kernel(x_ref, xp_ref, xn_ref, g_ref, mod_ref, w_ref, ps_ref, o_ref, h_ref, tmp_a, tmp_b, *, mod_row, seq):
    tm, d = x_ref.shape
    halo = POOL_HALO
    i = pl.program_id(0)
    g0 = g_ref[0:1, :]
    shift = _mod_slice(mod_ref, mod_row, SH1, d)
    scale = _mod_slice(mod_ref, mod_row, SC1, d)
    hp = _norm_mod(xp_ref[...], g0, shift, scale)
    hn = _norm_mod(xn_ref[...], g0, shift, scale)
    h_ref[0:halo, :] = jnp.where(i > 0, hp, 0.0)
    h_ref[halo:halo + tm, :] = _norm_mod(x_ref[...], g0, shift, scale)
    h_ref[halo + tm:, :] = jnp.where(i < pl.num_programs(0) - 1, hn, 0.0)

    t = i * tm + lax.broadcasted_iota(jnp.int32, (tm, 1), 0)
    gw = d // len(POOL_WINDOWS)
    ys = []
    for g, w in enumerate(POOL_WINDOWS):
        cols = slice(g * gw, (g + 1) * gw)
        tot = _window_sum(h_ref, cols, (tmp_a, tmp_b), w, tm, halo)
        cnt = (jnp.minimum(t + w // 2, seq) - jnp.maximum(t - w // 2, 0)).astype(F32)
        dm = tot / cnt - h_ref[halo:halo + tm, cols]
        ys.append(jnp.dot(dm.astype(BF16), w_ref[g], preferred_element_type=F32))
    y = jnp.concatenate(ys, axis=1) * ps_ref[...]
    o_ref[...] = x_ref[...] + _mod_slice(mod_ref, mod_row, GA1, d) * _rms(y, g_ref[1:2, :])


def pool_layer(x, norm_g, mods, layer, w_pool, pool_scale, *, mod_row, tm):
    m, d = x.shape
    tm = min(tm, m)
    nt = m // tm
    hb = tm // POOL_HALO
    return pl.pallas_call(
        functools.partial(_pool_kernel, mod_row=mod_row, seq=m),
        grid=(nt,),
        in_specs=[
            pl.BlockSpec((tm, d), lambda i: (i, 0)),
            pl.BlockSpec((POOL_HALO, d), lambda i: (jnp.maximum(i * hb - 1, 0), 0)),
            pl.BlockSpec((POOL_HALO, d), lambda i: (jnp.minimum((i + 1) * hb, nt * hb - 1), 0)),
            pl.BlockSpec((None,) + norm_g.shape[1:], lambda i: (layer, 0, 0)),
            pl.BlockSpec((None,) + mods.shape[1:], lambda i: (layer, 0, 0)),
            pl.BlockSpec(w_pool.shape, lambda i: (0, 0, 0)),
            pl.BlockSpec((1, d), lambda i: (0, 0)),
        ],
        out_specs=pl.BlockSpec((tm, d), lambda i: (i, 0)),
        out_shape=jax.ShapeDtypeStruct((m, d), F32),
        scratch_shapes=[pltpu.VMEM((tm + 2 * POOL_HALO, d), F32)]
        + [pltpu.VMEM((tm + 2 * POOL_HALO, d // len(POOL_WINDOWS)), F32)] * 2,
        compiler_params=_cparams("parallel"),
        name="pool_layer",
    )(x, x, x, norm_g, mods, w_pool, pool_scale.reshape(1, d))


SWIGLU_SUB = 256


def _swiglu_partial(h, w1_ref, w3_ref, w2_ref, lo, hi):
    assert (hi - lo) % SWIGLU_SUB == 0
    acc = None
    for s in range(lo, hi, SWIGLU_SUB):
        cols = slice(s, s + SWIGLU_SUB)
        a = jnp.dot(h, w1_ref[:, cols], preferred_element_type=F32)
        b = jnp.dot(h, w3_ref[:, cols], preferred_element_type=F32)
        u = (a * jax.nn.sigmoid(a) * b).astype(BF16)
        p = jnp.dot(u, w2_ref[cols, :], preferred_element_type=F32)
        acc = p if acc is None else acc + p
    return acc


def _ffn_kernel(x_ref, g_ref, mod_ref, w1_ref, w3_ref, w2_ref, *rest, mod_row, n_cast, nf):
    cast_src, o_ref, cast_dst = rest[:n_cast], rest[n_cast], rest[n_cast + 1:2 * n_cast + 1]
    h_ref, acc_ref = rest[2 * n_cast + 1:]
    d = x_ref.shape[1]
    f = pl.program_id(1)
    for src, dst in zip(cast_src, cast_dst):
        dst[...] = src[...].astype(dst.dtype)

    assert nf >= 2
    n_sub = w1_ref.shape[1] // SWIGLU_SUB
    bounds = [-(-n_sub * c // nf) * SWIGLU_SUB for c in range(nf + 1)]
    for c in range(nf):
        @pl.when(f == c)
        def _(c=c):
            if c == 0:
                h = _norm_mod(x_ref[...], g_ref[2:3, :], _mod_slice(mod_ref, mod_row, SH2, d),
                              _mod_slice(mod_ref, mod_row, SC2, d)).astype(BF16)
                h_ref[...] = h
            else:
                h = h_ref[...]
            part = _swiglu_partial(h, w1_ref, w3_ref, w2_ref, bounds[c], bounds[c + 1])
            if c == 0:
                acc_ref[...] = part
            elif c < nf - 1:
                acc_ref[...] += part
            else:
                y = acc_ref[...] + part
                o_ref[...] = x_ref[...] + _mod_slice(mod_ref, mod_row, GA2, d) * _rms(y, g_ref[3:4, :])


def ffn_layer(x, norm_g, mods, layer, w1, w3, w2, *, mod_row, tm, tf, cast=None):
    m, d = x.shape
    ff = w1.shape[1]
    tm = min(tm, m)
    nf = ff // tf
    steps = (m // tm) * nf
    in_specs = [
        pl.BlockSpec((tm, d), lambda i, f: (i, 0)),
        pl.BlockSpec((None,) + norm_g.shape[1:], lambda i, f: (layer, 0, 0)),
        pl.BlockSpec((None,) + mods.shape[1:], lambda i, f: (layer, 0, 0)),
        pl.BlockSpec((d, ff), lambda i, f: (0, 0), pipeline_mode=pl.Buffered(1)),
        pl.BlockSpec((d, ff), lambda i, f: (0, 0), pipeline_mode=pl.Buffered(1)),
        pl.BlockSpec((ff, d), lambda i, f: (0, 0), pipeline_mode=pl.Buffered(1)),
    ]
    out_specs = [pl.BlockSpec((tm, d), lambda i, f: (i, 0))]
    out_shape = [jax.ShapeDtypeStruct((m, d), F32)]
    cast_arrays, cast_idx = cast if cast is not None else ((), 0)
    for a in cast_arrays:
        _, ne, r, c = a.shape
        per_expert = steps // ne
        assert per_expert * ne == steps and r % (per_expert * 16) == 0
        rows = r // per_expert
        in_specs.append(pl.BlockSpec(
            (None, None, rows, c),
            lambda i, f, pe=per_expert: (cast_idx, (i * nf + f) // pe, (i * nf + f) % pe, 0)))
        out_specs.append(pl.BlockSpec(
            (None, rows, c), lambda i, f, pe=per_expert: ((i * nf + f) // pe, (i * nf + f) % pe, 0)))
        out_shape.append(jax.ShapeDtypeStruct((ne, r, c), BF16))
    outs = pl.pallas_call(
        functools.partial(_ffn_kernel, mod_row=mod_row, n_cast=len(cast_arrays), nf=nf),
        grid=(m // tm, nf),
        in_specs=in_specs,
        out_specs=out_specs,
        out_shape=out_shape,
        scratch_shapes=[pltpu.VMEM((tm, d), BF16), pltpu.VMEM((tm, d), F32)],
        compiler_params=_cparams("arbitrary", "arbitrary"),
        name="ffn_layer",
    )(x, norm_g, mods, w1, w3, w2, *cast_arrays)
    return outs[0] if cast is None else (outs[0], tuple(outs[1:]))


RUN_ALIGN = 16
RUN_BITS = tuple(range(9, 3, -1))
EXPERT_ROWS = 512
MOE_TOKEN_TILE = 512
TAIL_BITS = tuple(range(9, 3, -1))


def _split_dma(length, src, src_row, dst, dst_row, sem, bits, *, start, src_advances=True):
    for b in bits:
        size = 1 << b
        off = (length >> (b + 1)) << (b + 1)

        @pl.when(((length >> b) & 1) == 1)
        def _():
            s_row = pl.multiple_of(src_row + off, RUN_ALIGN) if src_advances else src_row
            cp = pltpu.make_async_copy(src.at[pl.ds(s_row, size)],
                                       dst.at[pl.ds(pl.multiple_of(dst_row + off, RUN_ALIGN), size)], sem)
            if start:
                cp.start()
            else:
                cp.wait()


def _run_dmas(tile, ne, lpad_ref, loff_ref, base_ref, buf, hbm_ref, sem, *, to_hbm, start):
    for e in range(ne):
        length = lpad_ref[tile * ne + e]
        lo = loff_ref[tile * ne + e]
        gb = base_ref[tile * ne + e]
        if to_hbm:
            _split_dma(length, buf, lo, hbm_ref, gb, sem, RUN_BITS, start=start)
        else:
            _split_dma(length, hbm_ref, gb, buf, lo, sem, RUN_BITS, start=start)


def _split_bf16(a):
    hi = a.astype(BF16)
    return hi, (a - hi.astype(F32)).astype(BF16)


def _dot_f32x3(a, b_hi, b_lo):
    a_hi, a_lo = _split_bf16(a)
    dot = functools.partial(jnp.dot, preferred_element_type=F32)
    return dot(a_hi, b_hi) + (dot(a_hi, b_lo) + dot(a_lo, b_hi))


def _local_rows(ri_ref, loff_ref, tile, ne):
    e1, e2 = ri_ref[:, 0:1], ri_ref[:, 1:2]
    lo1, lo2 = ri_ref[:, 2:3], ri_ref[:, 3:4]
    for e in range(ne):
        off = loff_ref[tile * ne + e]
        lo1 = lo1 + jnp.where(e1 == e, off, 0)
        lo2 = lo2 + jnp.where(e2 == e, off, 0)
    return lo1, lo2


def _router_kernel(x_ref, g_ref, mod_ref, wr_hi_ref, wr_lo_ref, br_ref, ri_ref, rw_ref, cnt_ref, *, mod_row):
    _route_rows(x_ref[...], g_ref, mod_ref, mod_row, wr_hi_ref, wr_lo_ref, br_ref, ri_ref, rw_ref, cnt_ref)


def _route_rows(x, g_ref, mod_ref, mod_row, wr_hi_ref, wr_lo_ref, br_ref, ri_ref, rw_ref, cnt_ref):
    tm, d = x.shape
    ne = wr_hi_ref.shape[1]
    h = _norm_mod(x, g_ref[2:3, :], _mod_slice(mod_ref, mod_row, SH2, d), _mod_slice(mod_ref, mod_row, SC2, d))
    logits = _dot_f32x3(h, wr_hi_ref[...], wr_lo_ref[...]) + br_ref[...]
    lane = lax.broadcasted_iota(jnp.int32, logits.shape, 1)
    v1 = jnp.max(logits, axis=-1, keepdims=True)
    i1 = jnp.min(jnp.where(logits == v1, lane, ne), axis=-1, keepdims=True)
    rest = jnp.where(lane == i1, -jnp.inf, logits)
    v2 = jnp.max(rest, axis=-1, keepdims=True)
    i2 = jnp.min(jnp.where(rest == v2, lane, ne), axis=-1, keepdims=True)
    e2 = jnp.exp(v2 - v1)
    den = 1.0 + e2
    oh1 = (lane == i1).astype(F32)
    oh2 = (lane == i2).astype(F32)
    both = oh1 + oh2
    r = lax.broadcasted_iota(jnp.int32, (tm, tm), 0)
    c = lax.broadcasted_iota(jnp.int32, (tm, tm), 1)
    tri = jnp.where(c < r, 1.0, 0.0).astype(BF16)
    rank_all = jnp.dot(tri, both.astype(BF16), preferred_element_type=F32)
    rank1 = jnp.sum(rank_all * oh1, axis=-1, keepdims=True).astype(jnp.int32)
    rank2 = jnp.sum(rank_all * oh2, axis=-1, keepdims=True).astype(jnp.int32)
    col = lax.broadcasted_iota(jnp.int32, ri_ref.shape, 1)
    ri_ref[...] = jnp.where(col == 0, i1, jnp.where(col == 1, i2, jnp.where(col == 2, rank1, rank2)))
    colw = lax.broadcasted_iota(jnp.int32, rw_ref.shape, 1)
    rw_ref[...] = jnp.where(colw == 0, 1.0 / den, e2 / den)
    cnt_ref[...] = jnp.sum(both, axis=0, keepdims=True).astype(jnp.int32)


def moe_router(x, norm_g, mods, layer, w_router, b_router, *, mod_row, tm):
    m, d = x.shape
    ne = w_router.shape[1]
    return pl.pallas_call(
        functools.partial(_router_kernel, mod_row=mod_row),
        grid=(m // tm,),
        in_specs=[
            pl.BlockSpec((tm, d), lambda i: (i, 0)),
            pl.BlockSpec((None,) + norm_g.shape[1:], lambda i: (layer, 0, 0)),
            pl.BlockSpec((None,) + mods.shape[1:], lambda i: (layer, 0, 0)),
            pl.BlockSpec((d, ne), lambda i: (0, 0)),
            pl.BlockSpec((d, ne), lambda i: (0, 0)),
            pl.BlockSpec((1, ne), lambda i: (0, 0)),
        ],
        out_specs=[
            pl.BlockSpec((tm, 4), lambda i: (i, 0)),
            pl.BlockSpec((tm, 2), lambda i: (i, 0)),
            pl.BlockSpec((None, 1, ne), lambda i: (i, 0, 0)),
        ],
        out_shape=[
            jax.ShapeDtypeStruct((m, 4), jnp.int32),
            jax.ShapeDtypeStruct((m, 2), F32),
            jax.ShapeDtypeStruct((m // tm, 1, ne), jnp.int32),
        ],
        compiler_params=_cparams("parallel"),
        name="moe_router",
    )(x, norm_g, mods, *_split_bf16(w_router), b_router.reshape(1, ne))


def _dispatch_kernel(wlen_ref, loff_ref, base_ref, lrow_ref, carry_ref, prev_ref, tlen_ref, tstart_ref, nu_ref,
                     x_ref, g_ref, mod_ref, ri_ref, *rest, mod_row, ne, tile0, first):
    xs_ref, cbuf, zbuf, sem = rest if first else rest[1:]
    tm, d = x_ref.shape
    rc = cbuf.shape[1]
    i = pl.program_id(0)
    last = i == pl.num_programs(0) - 1
    slot = i % 2

    def zero_tails(start):
        for e in range(ne):
            _split_dma(tlen_ref[e], zbuf, 0, xs_ref, tstart_ref[e], sem.at[2], TAIL_BITS, start=start,
                       src_advances=False)

        def unused_tile(t, carry):
            cp = pltpu.make_async_copy(
                zbuf, xs_ref.at[pl.ds(pl.multiple_of(t * EXPERT_ROWS, EXPERT_ROWS), EXPERT_ROWS)], sem.at[2])
            if start:
                cp.start()
            else:
                cp.wait()
            return carry

        lax.fori_loop(nu_ref[0], xs_ref.shape[0] // EXPERT_ROWS, unused_tile, 0)

    if first:
        @pl.when(i == 0)
        def _():
            zbuf[...] = jnp.zeros_like(zbuf)
            zero_tails(True)

    h = _norm_mod(x_ref[...], g_ref[2:3, :], _mod_slice(mod_ref, mod_row, SH2, d),
                  _mod_slice(mod_ref, mod_row, SC2, d)).astype(BF16)
    lo1, lo2 = _local_rows(ri_ref, lrow_ref, i + tile0, ne)
    lane = lax.broadcasted_iota(jnp.int32, (tm, rc), 1)
    onehot = jnp.where((lane == lo1) | (lane == lo2), 1.0, 0.0).astype(BF16)
    sorted_rows = lax.dot_general(onehot, h, (((0,), (0,)), ((), ())), preferred_element_type=F32)
    cbuf[slot] = sorted_rows.astype(BF16)
    for e in range(ne):
        @pl.when(carry_ref[(i + tile0) * ne + e] > 0)
        def _(e=e):
            here = pl.ds(pl.multiple_of(loff_ref[(i + tile0) * ne + e], RUN_ALIGN), RUN_ALIGN)
            prev = pl.ds(pl.multiple_of(prev_ref[(i + tile0) * ne + e], RUN_ALIGN), RUN_ALIGN)
            cbuf[slot, here, :] = (cbuf[slot, here, :].astype(F32)
                                   + cbuf[1 - slot, prev, :].astype(F32)).astype(BF16)

    moves = functools.partial(_run_dmas, ne=ne, lpad_ref=wlen_ref, loff_ref=loff_ref, base_ref=base_ref,
                              hbm_ref=xs_ref, to_hbm=True)
    moves(i + tile0, buf=cbuf.at[slot], sem=sem.at[slot], start=True)

    @pl.when(i > 0)
    def _():
        moves(i + tile0 - 1, buf=cbuf.at[1 - slot], sem=sem.at[1 - slot], start=False)

    @pl.when(last)
    def _():
        moves(i + tile0, buf=cbuf.at[slot], sem=sem.at[slot], start=False)
        if first:
            zero_tails(False)


def moe_dispatch(x, norm_g, mods, layer, route_i, tabs, rows, xs=None, *, mod_row, tm, ne, tile0):
    m, d = x.shape
    rc = 2 * tm + ne * 2 * RUN_ALIGN
    first = xs is None
    in_specs = [
        pl.BlockSpec((tm, d), lambda i, *_: (i, 0)),
        pl.BlockSpec((None,) + norm_g.shape[1:], lambda i, *_: (layer, 0, 0)),
        pl.BlockSpec((None,) + mods.shape[1:], lambda i, *_: (layer, 0, 0)),
        pl.BlockSpec((tm, 4), lambda i, *_: (i, 0)),
    ]
    args = [*tabs, x, norm_g, mods, route_i]
    if not first:
        in_specs.append(pl.BlockSpec(memory_space=pl.ANY))
        args.append(xs)
    return pl.pallas_call(
        functools.partial(_dispatch_kernel, mod_row=mod_row, ne=ne, tile0=tile0, first=first),
        grid_spec=pltpu.PrefetchScalarGridSpec(
            num_scalar_prefetch=len(tabs),
            grid=(m // tm,),
            in_specs=in_specs,
            out_specs=pl.BlockSpec(memory_space=pl.ANY),
            scratch_shapes=[pltpu.VMEM((2, rc, d), BF16), pltpu.VMEM((EXPERT_ROWS, d), BF16),
                            pltpu.SemaphoreType.DMA((3,))],
        ),
        out_shape=jax.ShapeDtypeStruct((rows, d), BF16),
        input_output_aliases={} if first else {len(args) - 1: 0},
        compiler_params=_cparams("arbitrary"),
        name="moe_dispatch",
    )(*args)


EXPERT_ROW_STEP = 128


def _experts_kernel(te_ref, nu_ref, tr_ref, x_ref, w1_ref, w3_ref, w2_ref, y_ref, acc_ref, *, n_chunks):
    del te_ref, nu_ref
    i = pl.program_id(0)
    f = pl.program_id(1)
    nf = pl.num_programs(1)
    tm = x_ref.shape[0]
    live_rows = tr_ref[i]
    steps = (live_rows + EXPERT_ROW_STEP - 1) // EXPERT_ROW_STEP

    for c in range(n_chunks):
        for q in range(1, tm // EXPERT_ROW_STEP + 1):
            @pl.when((f == c) & (steps == q))
            def _(c=c, rows=q * EXPERT_ROW_STEP):
                part = _swiglu_partial(x_ref[0:rows, :], w1_ref, w3_ref, w2_ref, 0, w1_ref.shape[1])
                if c == 0:
                    acc_ref[0:rows, :] = part
                elif c < n_chunks - 1:
                    acc_ref[0:rows, :] += part
                else:
                    y_ref[0:rows, :] = (acc_ref[0:rows, :] + part).astype(y_ref.dtype)
                    if rows < tm:
                        y_ref[rows:, :] = jnp.zeros((tm - rows, y_ref.shape[1]), y_ref.dtype)

    @pl.when((steps == 0) & (f == nf - 1))
    def _():
        y_ref[...] = jnp.zeros_like(y_ref)


def moe_experts(xs, tile_expert, n_used, tile_rows, w1, w3, w2, *, tm, tf):
    rows, d = xs.shape
    ff = w1.shape[2]
    nf = ff // tf
    assert nf >= 2

    def chunk(i, f, nu):
        return jnp.where(i < nu[0], f, nf - 1)

    def wmap_in(i, f, te, nu, tr):
        return (te[i], 0, chunk(i, f, nu))

    def wmap_out(i, f, te, nu, tr):
        return (te[i], chunk(i, f, nu), 0)

    def xmap(i, f, te, nu, tr):
        return (jnp.maximum(jnp.minimum(i, nu[0] - 1), 0), 0)

    return pl.pallas_call(
        functools.partial(_experts_kernel, n_chunks=nf),
        grid_spec=pltpu.PrefetchScalarGridSpec(
            num_scalar_prefetch=3,
            grid=(rows // tm, nf),
            in_specs=[
                pl.BlockSpec((tm, d), xmap),
                pl.BlockSpec((None, d, tf), wmap_in),
                pl.BlockSpec((None, d, tf), wmap_in),
                pl.BlockSpec((None, tf, d), wmap_out),
            ],
            out_specs=pl.BlockSpec((tm, d), lambda i, f, te, nu, tr: (i, 0)),
            scratch_shapes=[pltpu.VMEM((tm, d), F32)],
        ),
        out_shape=jax.ShapeDtypeStruct((rows, d), BF16),
        compiler_params=_cparams("arbitrary", "arbitrary"),
        name="moe_experts",
    )(tile_expert, n_used, tile_rows, xs, w1, w3, w2)


def _combine_kernel(lpad_ref, loff_ref, base_ref, lrow_ref, x_ref, ri_ref, rw_ref, g_ref, mod_ref, ys_ref, o_ref,
                    ybuf, sem, *, mod_row, ne, tile0):
    tm, d = x_ref.shape
    rc = ybuf.shape[1]
    i = pl.program_id(0)
    slot = i % 2
    moves = functools.partial(_run_dmas, ne=ne, lpad_ref=lpad_ref, loff_ref=loff_ref, base_ref=base_ref,
                              hbm_ref=ys_ref, to_hbm=False)

    @pl.when(i == 0)
    def _():
        ybuf[...] = jnp.zeros_like(ybuf)
        moves(tile0, buf=ybuf.at[0], sem=sem.at[0], start=True)

    @pl.when(i + 1 < pl.num_programs(0))
    def _():
        moves(i + tile0 + 1, buf=ybuf.at[1 - slot], sem=sem.at[1 - slot], start=True)

    moves(i + tile0, buf=ybuf.at[slot], sem=sem.at[slot], start=False)
    lo1, lo2 = _local_rows(ri_ref, lrow_ref, i + tile0, ne)
    lane = lax.broadcasted_iota(jnp.int32, (tm, rc), 1)
    rows = ybuf[slot]
    y1 = jnp.dot(jnp.where(lane == lo1, 1.0, 0.0).astype(BF16), rows, preferred_element_type=F32)
    y2 = jnp.dot(jnp.where(lane == lo2, 1.0, 0.0).astype(BF16), rows, preferred_element_type=F32)
    y = rw_ref[:, 0:1] * y1 + rw_ref[:, 1:2] * y2
    o_ref[...] = x_ref[...] + _mod_slice(mod_ref, mod_row, GA2, d) * _rms(y, g_ref[3:4, :])


def moe_combine(x, norm_g, mods, layer, route_i, route_w, ys, tabs, *, mod_row, tm, ne, tile0):
    m, d = x.shape
    rc = 2 * tm + ne * 2 * RUN_ALIGN
    return pl.pallas_call(
        functools.partial(_combine_kernel, mod_row=mod_row, ne=ne, tile0=tile0),
        grid_spec=pltpu.PrefetchScalarGridSpec(
            num_scalar_prefetch=len(tabs),
            grid=(m // tm,),
            in_specs=[
                pl.BlockSpec((tm, d), lambda i, *_: (i, 0)),
                pl.BlockSpec((tm, 4), lambda i, *_: (i, 0)),
                pl.BlockSpec((tm, 2), lambda i, *_: (i, 0)),
                pl.BlockSpec((None,) + norm_g.shape[1:], lambda i, *_: (layer, 0, 0)),
                pl.BlockSpec((None,) + mods.shape[1:], lambda i, *_: (layer, 0, 0)),
                pl.BlockSpec(memory_space=pl.ANY),
            ],
            out_specs=pl.BlockSpec((tm, d), lambda i, *_: (i, 0)),
            scratch_shapes=[pltpu.VMEM((2, rc, d), BF16), pltpu.SemaphoreType.DMA((2,))],
        ),
        out_shape=jax.ShapeDtypeStruct((m, d), F32),
        compiler_params=_cparams("arbitrary"),
        name="moe_combine",
    )(*tabs, x, route_i, route_w, norm_g, mods, ys)


def sparse_moe_layer(streams, norm_g, mods, layer, w_router, b_router, w1, w3, w2, *, tf, routed_first=None):
    ne = w_router.shape[1]
    tm_exp = EXPERT_ROWS
    routed = [moe_router(x, norm_g, mods, layer, w_router, b_router, mod_row=row, tm=tm)
              if k > 0 or routed_first is None else routed_first for k, (x, row, tm) in enumerate(streams)]
    align = lambda a: (a + RUN_ALIGN - 1) // RUN_ALIGN * RUN_ALIGN
    per_stream = [r[2][:, 0, :] for r in routed]
    counts = jnp.concatenate(per_stream, axis=0)
    nt = counts.shape[0]
    m_total = sum(x.shape[0] for x, _, _ in streams)
    stream_rows = [align(jnp.sum(c, axis=0)) for c in per_stream]
    stream_off = jnp.cumsum(jnp.stack(stream_rows), axis=0) - jnp.stack(stream_rows)
    pos = jnp.concatenate([stream_off[k][None, :] + jnp.cumsum(c, axis=0) - c for k, c in enumerate(per_stream)])
    carry = pos % RUN_ALIGN
    lpad = align(carry + counts)
    stream_end = jnp.concatenate([jnp.arange(c.shape[0]) == c.shape[0] - 1 for c in per_stream])[:, None]
    wlen = jnp.where(stream_end, lpad, (carry + counts) // RUN_ALIGN * RUN_ALIGN)
    loff = jnp.cumsum(lpad, axis=1) - lpad
    prev_block = jnp.concatenate([jnp.zeros((1, ne), loff.dtype), (loff + lpad - RUN_ALIGN)[:-1]], axis=0)
    group = sum(stream_rows)
    gpad = (group + tm_exp - 1) // tm_exp * tm_exp
    ends = jnp.cumsum(gpad)
    starts = ends - gpad
    base = starts[None, :] + pos - carry
    n_tiles = -(-(2 * m_total + len(streams) * ne * (RUN_ALIGN - 1)) // tm_exp) + ne
    tile_start = jnp.arange(n_tiles, dtype=jnp.int32) * tm_exp
    n_used = (ends[-1] // tm_exp).astype(jnp.int32).reshape(1)
    tile_expert = jnp.sum(tile_start[:, None] >= ends[None, :], axis=1).astype(jnp.int32)
    tile_expert = jnp.minimum(tile_expert, tile_expert[jnp.maximum(n_used[0] - 1, 0)])
    tile_rows = jnp.clip((starts + group)[tile_expert] - tile_start, 0, tm_exp).astype(jnp.int32)
    group_first = stream_rows[0]
    later_rows = m_total - streams[0][0].shape[0] + (len(streams) - 1) * (RUN_ALIGN - 1)
    assert later_rows + tm_exp - RUN_ALIGN < 2 << TAIL_BITS[0]
    flat = lambda *arrays: [a.reshape(-1).astype(jnp.int32) for a in arrays]
    tabs = flat(lpad, loff, base, loff + carry)
    write_tabs = flat(wlen, loff, base, loff + carry, carry, prev_block,
                      gpad - group_first, starts + group_first, n_used)
    xs, tile0 = None, 0
    for (x, row, tm), (route_i, _, _) in zip(streams, routed):
        xs = moe_dispatch(x, norm_g, mods, layer, route_i, write_tabs, n_tiles * tm_exp, xs,
                          mod_row=row, tm=tm, ne=ne, tile0=tile0)
        tile0 += x.shape[0] // tm
    ys = moe_experts(xs, tile_expert, n_used, tile_rows, w1, w3, w2, tm=tm_exp, tf=tf)
    outs, tile0 = [], 0
    for (x, row, tm), (route_i, route_w, _) in zip(streams, routed):
        outs.append(moe_combine(x, norm_g, mods, layer, route_i, route_w, ys, tabs,
                                mod_row=row, tm=tm, ne=ne, tile0=tile0))
        tile0 += x.shape[0] // tm
    return outs


def _rope_tables(seq):
    n = 16
    inv = ROPE_BASE ** (-np.arange(n, dtype=np.float64) / n)
    t = np.arange(seq)
    row_ang = (t // GRID_W)[:, None] * inv[None, :]
    col_ang = (t % GRID_W)[:, None] * inv[None, :]
    cos = np.concatenate([np.cos(row_ang)] * 2 + [np.cos(col_ang)] * 2, axis=1)
    sin = np.concatenate([-np.sin(row_ang), np.sin(row_ang), -np.sin(col_ang), np.sin(col_ang)], axis=1)
    return (jnp.asarray(np.tile(cos, (1, 2)), dtype=F32), jnp.asarray(np.tile(sin, (1, 2)), dtype=F32))


def kernel(x, c, ctx, c_ctx, w_ada, b_ada, norm_g, attn_w_qkv, attn_w_o, attn_sink, ret_w_in, ret_w_o,
           ret_log_decay, pool_w, pool_scale, ffn_w1, ffn_w3, ffn_w2, moe_w_router, moe_b_router,
           moe_w1, moe_w3, moe_w2):
    batch, seq, d = x.shape
    assert batch == 1 and c.shape[0] == 1
    depth = w_ada.shape[0]
    lc = ctx.shape[1]
    xl = x.reshape(seq, d)
    xc = ctx.reshape(lc, d)

    cvecs = jnp.zeros((8, d), F32).at[LAT_ROW].set(c[0]).at[CTX_ROW].set(c_ctx)
    mods = ada_table(cvecs, w_ada, b_ada)
    rope = _rope_tables(seq)

    hd = attn_w_o.shape[1]
    dh = hd // ATTN_HEADS
    qkv_scale = jnp.concatenate([jnp.full((hd,), dh ** -0.5, F32),
                                 jnp.ones((attn_w_qkv.shape[2] - hd,), F32)])
    dk = d // RET_HEADS
    ret_scale = jnp.concatenate([jnp.ones((d,), F32), jnp.full((d,), dk ** -0.5, F32),
                                 jnp.ones((ret_w_in.shape[2] - 2 * d,), F32)])

    moe_bf16 = None
    for i in range(depth):
        last = i == depth - 1
        kind, j = i % N_MIXERS, i // N_MIXERS
        proj = functools.partial(mixer_projection, norm_g=norm_g, mods=mods, layer=i)
        routed_lat = None
        if i % 2 == 1:
            mix_out = functools.partial(outproj_residual, norm_g=norm_g, mods=mods, layer=i, g_row=1,
                                        mod_row=LAT_ROW, slot=GA1, tm=min(MOE_TOKEN_TILE, seq),
                                        router=(moe_w_router[i // 2], moe_b_router[i // 2]))
        else:
            mix_out = lambda x, **kw: (outproj_residual(x, norm_g, mods, i, g_row=1, mod_row=LAT_ROW,
                                                        slot=GA1, tm=1024, **kw), None)
        if kind == 0:
            w_qkv = (attn_w_qkv[j] * qkv_scale).astype(BF16)
            w_o = attn_w_o[j].astype(BF16)
            kvd = (w_qkv.shape[1] - hd) // 2
            qkv_l = proj(xl, w=w_qkv, mod_row=LAT_ROW, tm=512, chunk=256, q_cols=hd, rope=rope, rope_cols=hd + kvd)
            qkv_c = proj(xc, w=w_qkv, mod_row=CTX_ROW, tm=256, chunk=256, q_cols=hd)
            o_l = attention(attn_sink[j], qkv_l, qkv_c, has_local=True)
            xl, routed_lat = mix_out(xl, w=w_o, ys=[o_l])
            if not last:
                o_c = attention(attn_sink[j], qkv_c, qkv_c, has_local=False)
                xc = outproj_residual(xc, norm_g, mods, i, w_o, [o_c], g_row=1, mod_row=CTX_ROW, slot=GA1, tm=256)
        elif kind == 1:
            w_in = (ret_w_in[j] * ret_scale).astype(BF16)
            w_o = ret_w_o[j].astype(BF16)
            p_c = proj(xc, w=w_in, mod_row=CTX_ROW, tm=256, chunk=1024)
            p_l = proj(xl, w=w_in, mod_row=LAT_ROW, tm=512, chunk=1024)
            s0 = jnp.zeros((RET_HEADS, dk, 2 * dk), F32)
            tabs_f = retention_tables(ret_log_decay[j, 0], min(RET_CHUNK, lc), False)
            tabs_b = retention_tables(ret_log_decay[j, 1], min(RET_CHUNK, lc), True)
            zf_c, zb_c, s_f, s_b = retention_scan(p_c, tabs_f, tabs_b, s0, s0)
            zf_l, zb_l, _, _ = retention_scan(p_l, tabs_f, tabs_b, s_f, s_b)
            xl, routed_lat = mix_out(xl, w=w_o, ys=[zf_l, zb_l])
            if not last:
                xc = outproj_residual(xc, norm_g, mods, i, w_o, [zf_c, zb_c], g_row=1, mod_row=CTX_ROW, slot=GA1, tm=256)
        else:
            w_p = pool_w[j].astype(BF16)
            xl = pool_layer(xl, norm_g, mods, i, w_p, pool_scale[j], mod_row=LAT_ROW, tm=1024)
            if not last:
                xc = pool_layer(xc, norm_g, mods, i, w_p, pool_scale[j], mod_row=CTX_ROW, tm=256)

        f = i // 2
        if i % 2 == 0:
            w1, w3, w2 = ffn_w1[f].astype(BF16), ffn_w3[f].astype(BF16), ffn_w2[f].astype(BF16)
            tf = w1.shape[1] // 2
            steps = (seq // min(512, seq)) * 2
            ne = moe_w1.shape[1]
            if not last and steps % ne == 0 and d % (steps // ne * 16) == 0:
                xl, moe_bf16 = ffn_layer(xl, norm_g, mods, i, w1, w3, w2, mod_row=LAT_ROW, tm=512, tf=tf,
                                         cast=((moe_w1, moe_w3, moe_w2), (i + 1) // 2))
            else:
                xl = ffn_layer(xl, norm_g, mods, i, w1, w3, w2, mod_row=LAT_ROW, tm=512, tf=tf)
            if not last:
                xc = ffn_layer(xc, norm_g, mods, i, w1, w3, w2, mod_row=CTX_ROW, tm=256, tf=tf)
        else:
            if moe_bf16 is None:
                moe_bf16 = moe_w1[f].astype(BF16), moe_w3[f].astype(BF16), moe_w2[f].astype(BF16)
            (w1, w3, w2), moe_bf16 = moe_bf16, None
            tf = w1.shape[2] // 2
            streams = [(xl, LAT_ROW, min(MOE_TOKEN_TILE, seq))] + ([] if last else [(xc, CTX_ROW, min(256, lc))])
            outs = sparse_moe_layer(streams, norm_g, mods, i, moe_w_router[f], moe_b_router[f], w1, w3, w2, tf=tf,
                                    routed_first=routed_lat)
            xl = outs[0]
            if not last:
                xc = outs[1]
    return xl.reshape(batch, seq, d)
```

```python
import functools

import jax
import jax.numpy as jnp
import numpy as np
from jax import lax
from jax.experimental import pallas as pl
from jax.experimental.pallas import tpu as pltpu

F32 = jnp.float32
BF16 = jnp.bfloat16

EPS = 1e-6
NEG_INF = -1e30
LOG2E = 1.4426950408889634
LANES = 128
VMEM_LIMIT = 56 * 1024 * 1024

GRID_W = 64
N_MIXERS = 3
ATTN_HEADS = 16
ATTN_KV_HEADS = 4
ATTN_BLOCKS_PER_STEP = 4
WINDOW = 128
ROPE_BASE = 10000.0
RET_HEADS = 4
RET_CHUNK = 256
POOL_WINDOWS = (2, 4, 8, 16)
POOL_HALO = 8
N_EXPERTS = 8

SH1, SC1, GA1, SH2, SC2, GA2 = range(6)
LAT_ROW, CTX_ROW = 0, 1


def _cparams(*sem):
    return pltpu.CompilerParams(dimension_semantics=sem, vmem_limit_bytes=VMEM_LIMIT)


def _rms(x, g):
    return x * lax.rsqrt(jnp.mean(x * x, axis=-1, keepdims=True) + EPS) * g


def _mod_slice(mod_ref, row, slot, d):
    return mod_ref[row:row + 1, slot * d:(slot + 1) * d]


def _norm_mod(x, g, shift, scale):
    return _rms(x, g) * (1.0 + scale) + shift


def _ada_kernel(c_ref, w_ref, b_ref, o_ref):
    c = c_ref[...]
    s = c * jax.nn.sigmoid(c)
    o_ref[...] = _dot_f32x3(s, *_split_bf16(w_ref[...])) + b_ref[...]


def ada_table(cvecs, w_ada, b_ada):
    depth, d, n = w_ada.shape
    tn = 1536
    return pl.pallas_call(
        _ada_kernel,
        grid=(depth, n // tn),
        in_specs=[
            pl.BlockSpec((8, d), lambda i, j: (0, 0)),
            pl.BlockSpec((None, d, tn), lambda i, j: (i, 0, j)),
            pl.BlockSpec((None, 1, tn), lambda i, j: (i, 0, j)),
        ],
        out_specs=pl.BlockSpec((None, 8, tn), lambda i, j: (i, 0, j)),
        out_shape=jax.ShapeDtypeStruct((depth, 8, n), F32),
        compiler_params=_cparams("parallel", "parallel"),
        name="ada_table",
    )(cvecs, w_ada, b_ada.reshape(depth, 1, n))


def _rope(a, cos, sin):
    lane = lax.broadcasted_iota(jnp.int32, (a.shape[0], LANES), 1)
    first = (lane % 32) < 16
    outs = []
    for cb in range(a.shape[1] // LANES):
        blk = a[:, cb * LANES:(cb + 1) * LANES]
        partner = jnp.where(first, pltpu.roll(blk, LANES - 16, 1), pltpu.roll(blk, 16, 1))
        outs.append(blk * cos + partner * sin)
    return jnp.concatenate(outs, axis=1)


def _mixer_proj_kernel(x_ref, g_ref, mod_ref, w_ref, *rest, mod_row, chunk, rope_cols, q_cols):
    if rope_cols:
        cos_ref, sin_ref, o_ref = rest
        cos, sin = cos_ref[...], sin_ref[...]
    else:
        (o_ref,) = rest
    d = x_ref.shape[1]
    h = _norm_mod(x_ref[...], g_ref[0:1, :], _mod_slice(mod_ref, mod_row, SH1, d),
                  _mod_slice(mod_ref, mod_row, SC1, d)).astype(BF16)
    for c0 in range(0, o_ref.shape[1], chunk):
        acc = jnp.dot(h, w_ref[:, c0:c0 + chunk], preferred_element_type=F32)
        if c0 < q_cols:
            acc = acc * LOG2E
        if c0 < rope_cols:
            acc = _rope(acc, cos, sin)
        o_ref[:, c0:c0 + chunk] = acc.astype(o_ref.dtype)


def mixer_projection(x, norm_g, mods, layer, w, *, mod_row, tm, chunk, q_cols=0, rope=None, rope_cols=0):
    m, d = x.shape
    n = w.shape[1]
    tm = min(tm, m)
    assert n % chunk == 0 and rope_cols % chunk == 0 and q_cols % chunk == 0
    in_specs = [
        pl.BlockSpec((tm, d), lambda i: (i, 0)),
        pl.BlockSpec((None,) + norm_g.shape[1:], lambda i: (layer, 0, 0)),
        pl.BlockSpec((None,) + mods.shape[1:], lambda i: (layer, 0, 0)),
        pl.BlockSpec((d, n), lambda i: (0, 0), pipeline_mode=pl.Buffered(1)),
    ]
    args = [x, norm_g, mods, w]
    if rope_cols:
        in_specs += [pl.BlockSpec((tm, LANES), lambda i: (i, 0))] * 2
        args += list(rope)
    return pl.pallas_call(
        functools.partial(_mixer_proj_kernel, mod_row=mod_row, chunk=chunk, rope_cols=rope_cols, q_cols=q_cols),
        grid=(m // tm,),
        in_specs=in_specs,
        out_specs=pl.BlockSpec((tm, n), lambda i: (i, 0)),
        out_shape=jax.ShapeDtypeStruct((m, n), BF16),
        compiler_params=_cparams("parallel"),
        name="mixer_projection",
    )(*args)


def _outproj_kernel(x_ref, g_ref, mod_ref, w_ref, *rest, n_y, g_row, mod_row, slot, route):
    y_refs, rest = rest[:n_y], rest[n_y:]
    if route:
        router_refs, o_ref, route_out = rest[:3], rest[3], rest[4:]
    else:
        (o_ref,) = rest
    d = x_ref.shape[1]
    y = y_refs[0][...]
    if n_y == 2:
        y = (y.astype(F32) + y_refs[1][...].astype(F32)).astype(BF16)
    t = jnp.dot(y, w_ref[...], preferred_element_type=F32)
    gate = _mod_slice(mod_ref, mod_row, slot, d)
    x_new = x_ref[...] + gate * _rms(t, g_ref[g_row:g_row + 1, :])
    o_ref[...] = x_new
    if route:
        _route_rows(x_new, g_ref, mod_ref, mod_row, *router_refs, *route_out)


def _router_specs(m, d, ne, tm):
    ins = [pl.BlockSpec((d, ne), lambda i: (0, 0))] * 2 + [pl.BlockSpec((1, ne), lambda i: (0, 0))]
    outs = [pl.BlockSpec((tm, 4), lambda i: (i, 0)), pl.BlockSpec((tm, 2), lambda i: (i, 0)),
            pl.BlockSpec((None, 1, ne), lambda i: (i, 0, 0))]
    shapes = [jax.ShapeDtypeStruct((m, 4), jnp.int32), jax.ShapeDtypeStruct((m, 2), F32),
              jax.ShapeDtypeStruct((m // tm, 1, ne), jnp.int32)]
    return ins, outs, shapes


def outproj_residual(x, norm_g, mods, layer, w, ys, *, g_row, mod_row, slot, tm, router=None):
    m, d = x.shape
    k = w.shape[0]
    tm = min(tm, m)
    in_specs = [
        pl.BlockSpec((tm, d), lambda i: (i, 0)),
        pl.BlockSpec((None,) + norm_g.shape[1:], lambda i: (layer, 0, 0)),
        pl.BlockSpec((None,) + mods.shape[1:], lambda i: (layer, 0, 0)),
        pl.BlockSpec((k, d), lambda i: (0, 0)),
    ] + [pl.BlockSpec((tm, k), lambda i: (i, 0))] * len(ys)
    args = [x, norm_g, mods, w, *ys]
    out_specs = [pl.BlockSpec((tm, d), lambda i: (i, 0))]
    out_shape = [jax.ShapeDtypeStruct((m, d), F32)]
    if router is not None:
        w_router, b_router = router
        ne = w_router.shape[1]
        r_in, r_out, r_shapes = _router_specs(m, d, ne, tm)
        in_specs += r_in
        args += [*_split_bf16(w_router), b_router.reshape(1, ne)]
        out_specs += r_out
        out_shape += r_shapes
    outs = pl.pallas_call(
        functools.partial(_outproj_kernel, n_y=len(ys), g_row=g_row, mod_row=mod_row, slot=slot,
                          route=router is not None),
        grid=(m // tm,),
        in_specs=in_specs,
        out_specs=out_specs,
        out_shape=out_shape,
        compiler_params=_cparams("parallel"),
        name="outproj_residual",
    )(*args)
    return outs[0] if router is None else (outs[0], tuple(outs[1:]))


def _dot_nt(a, b):
    return lax.dot_general(a, b, (((1,), (1,)), ((), ())), preferred_element_type=F32)


def _attn_kernel(sink_ref, q_ref, *rest, tq, nsub, kv_heads, group, dh, has_local):
    if has_local:
        k_refs, v_refs = rest[:nsub + 2], rest[nsub + 2:2 * nsub + 4]
    kc_ref, vc_ref, o_ref = rest[-3:]
    assert 2 * dh == LANES and tq == WINDOW and group % 2 == 0
    b = pl.program_id(0)
    nb = pl.num_programs(0)
    rows = group * tq
    lc = kc_ref.shape[0]
    head_of_row = lax.broadcasted_iota(jnp.int32, (rows, 1), 0) // tq
    lane = lax.broadcasted_iota(jnp.int32, (1, LANES), 1)
    if has_local:
        r = lax.broadcasted_iota(jnp.int32, (rows, tq), 0) % tq
        c = lax.broadcasted_iota(jnp.int32, (rows, tq), 1)
    for sub, j in [(sub, j) for sub in range(nsub) for j in range(kv_heads)]:
        qrows = slice(sub * tq, (sub + 1) * tq)
        hs = slice(j * dh, (j + 1) * dh)
        pair = slice((j // 2) * LANES, (j // 2 + 1) * LANES)
        v_low = j % 2 == 0
        keep_v = (lane < dh) if v_low else (lane >= dh)
        qg = jnp.concatenate(
            [q_ref[qrows, (j * group + g) * dh:(j * group + g + 1) * dh] for g in range(group)], axis=0)
        sink = jnp.full((rows, 1), sink_ref[j * group] * LOG2E, F32)
        for g in range(1, group):
            sink = jnp.where(head_of_row == g, sink_ref[j * group + g] * LOG2E, sink)
        if has_local:
            kp_ref, kq_ref, kn_ref = k_refs[sub:sub + 3]
            vp_ref, vq_ref, vn_ref = v_refs[sub:sub + 3]
            prev_ok = (c >= r) if sub > 0 else (c >= r) & (b > 0)
            next_ok = (c <= r) if sub < nsub - 1 else (c <= r) & (b < nb - 1)
            k_all = jnp.concatenate([kc_ref[:, hs], kp_ref[:, hs], kq_ref[:, hs], kn_ref[:, hs]], axis=0)
            v_all = jnp.concatenate([vc_ref[:, pair], vp_ref[:, pair], vq_ref[:, pair], vn_ref[:, pair]], axis=0)
            s = _dot_nt(qg, k_all)
            s = jnp.concatenate([s[:, :lc],
                                 jnp.where(prev_ok, s[:, lc:lc + tq], NEG_INF),
                                 s[:, lc + tq:lc + 2 * tq],
                                 jnp.where(next_ok, s[:, lc + 2 * tq:], NEG_INF)], axis=1)
        else:
            v_all = vc_ref[:, pair]
            s = _dot_nt(qg, kc_ref[:, hs])
        m = jnp.maximum(jnp.max(s, axis=-1, keepdims=True), sink)
        p = jnp.exp2(s - m).astype(BF16)
        v_ones = jnp.where(keep_v, v_all, jnp.ones_like(v_all))
        pv = jnp.dot(p, v_ones, preferred_element_type=F32)
        den = pltpu.roll(pv, dh, 1) + jnp.exp2(sink - m)
        o = pv / den
        for g in range(0, group, 2):
            a, bb = o[g * tq:(g + 1) * tq], o[(g + 1) * tq:(g + 2) * tq]
            if v_low:
                both = jnp.where(lane < dh, a, pltpu.roll(bb, dh, 1))
            else:
                both = jnp.where(lane < dh, pltpu.roll(a, dh, 1), bb)
            h = j * group + g
            o_ref[qrows, h * dh:(h + 2) * dh] = both.astype(o_ref.dtype)


def attention(sink, qkv, qkv_ctx, *, has_local):
    s = qkv.shape[0]
    dh = qkv.shape[1] // (ATTN_HEADS + 2 * ATTN_KV_HEADS)
    hd = ATTN_HEADS * dh
    kvd = ATTN_KV_HEADS * dh
    kcol, vcol = hd // kvd, hd // kvd + 1
    tq = WINDOW
    nb = s // tq
    nsub = min(ATTN_BLOCKS_PER_STEP, nb)
    lc = qkv_ctx.shape[0]
    in_specs = [
        pl.BlockSpec(memory_space=pltpu.SMEM),
        pl.BlockSpec((nsub * tq, hd), lambda b: (b, 0)),
    ]
    args = [sink, qkv]
    if has_local:
        for col in (kcol, vcol):
            for off in range(-1, nsub + 1):
                in_specs.append(pl.BlockSpec(
                    (tq, kvd), lambda b, col=col, off=off: (jnp.clip(nsub * b + off, 0, nb - 1), col)))
                args.append(qkv)
    in_specs += [pl.BlockSpec((lc, kvd), lambda b: (0, kcol)), pl.BlockSpec((lc, kvd), lambda b: (0, vcol))]
    args += [qkv_ctx, qkv_ctx]
    return pl.pallas_call(
        functools.partial(_attn_kernel, tq=tq, nsub=nsub, kv_heads=ATTN_KV_HEADS,
                          group=ATTN_HEADS // ATTN_KV_HEADS, dh=dh, has_local=has_local),
        grid=(nb // nsub,),
        in_specs=in_specs,
        out_specs=pl.BlockSpec((nsub * tq, hd), lambda b: (b, 0)),
        out_shape=jax.ShapeDtypeStruct((s, hd), BF16),
        compiler_params=_cparams("parallel"),
        name="attention",
    )(*args)


N_RET_IN = 9


def _ret_kernel(*refs, heads, dk, dv):
    ins = [refs[:N_RET_IN], refs[N_RET_IN:2 * N_RET_IN]]
    outs = refs[2 * N_RET_IN:2 * N_RET_IN + 4]
    state_ref = refs[-1]
    i = pl.program_id(0)

    @pl.when(i == 0)
    def _():
        for dr in range(2):
            state_ref[dr] = ins[dr][8][...]

    for dr, h in [(dr, h) for h in range(heads) for dr in range(2)]:
        q_ref, k_ref, v_ref, gate_ref, intra_ref, qdec_ref, kdec_ref, cdec_ref, _ = ins[dr]
        z_ref = outs[dr]
        q = q_ref[:, h * dk:(h + 1) * dk]
        k = k_ref[:, h * dk:(h + 1) * dk]
        v = v_ref[:, h * dv:(h + 1) * dv]
        state = state_ref[dr, h]
        sc = _dot_nt(q, k) * intra_ref[h]
        qd = (q.astype(F32) * qdec_ref[h]).astype(BF16)
        o = (jnp.dot(sc.astype(BF16), v, preferred_element_type=F32)
             + jnp.dot(qd, state.astype(BF16), preferred_element_type=F32))
        kd = (k.astype(F32) * kdec_ref[h]).astype(BF16)
        state_ref[dr, h] = state * cdec_ref[h] + lax.dot_general(
            kd, v, (((0,), (0,)), ((), ())), preferred_element_type=F32)
        mu = jnp.mean(o, axis=-1, keepdims=True)
        oc = o - mu
        var = jnp.mean(oc * oc, axis=-1, keepdims=True)
        gate = gate_ref[:, h * dv:(h + 1) * dv].astype(F32)
        z = gate * jax.nn.sigmoid(gate) * (oc * lax.rsqrt(var + EPS))
        z_ref[:, h * dv:(h + 1) * dv] = z.astype(z_ref.dtype)

    @pl.when(i == pl.num_programs(0) - 1)
    def _():
        for dr in range(2):
            outs[2 + dr][...] = state_ref[dr]


def retention_scan(proj, tables_f, tables_b, s0_f, s0_b):
    m, n = proj.shape
    d = n // 8
    heads = RET_HEADS
    dk, dv = d // heads, 2 * d // heads
    c = min(RET_CHUNK, m)
    nc = m // c
    once = lambda a: pl.BlockSpec(a.shape, lambda i: (0,) * a.ndim, pipeline_mode=pl.Buffered(1))
    in_specs, args, out_specs = [], [], []
    for reverse, tables, s0 in ((False, tables_f, s0_f), (True, tables_b, s0_b)):
        row = (lambda i: nc - 1 - i) if reverse else (lambda i: i)
        in_specs += [
            pl.BlockSpec((c, d), lambda i, row=row: (row(i), 0)),
            pl.BlockSpec((c, d), lambda i, row=row: (row(i), 1)),
            pl.BlockSpec((c, 2 * d), lambda i, row=row: (row(i), 1)),
            pl.BlockSpec((c, 2 * d), lambda i, row=row, blk=3 if reverse else 2: (row(i), blk)),
        ] + [once(a) for a in (*tables, s0)]
        args += [proj] * 4 + [*tables, s0]
        out_specs.append(pl.BlockSpec((c, 2 * d), lambda i, row=row: (row(i), 0)))
    out_specs += [pl.BlockSpec(s0_f.shape, lambda i: (0,) * s0_f.ndim)] * 2
    return pl.pallas_call(
        functools.partial(_ret_kernel, heads=heads, dk=dk, dv=dv),
        grid=(nc,),
        in_specs=in_specs,
        out_specs=out_specs,
        out_shape=[jax.ShapeDtypeStruct((m, 2 * d), BF16)] * 2 + [jax.ShapeDtypeStruct(s0_f.shape, F32)] * 2,
        scratch_shapes=[pltpu.VMEM((2,) + s0_f.shape, F32)],
        compiler_params=_cparams("arbitrary"),
        name="retention_scan",
    )(*args)


def retention_tables(log_decay_row, c, reverse):
    lg = -jnp.exp(log_decay_row.astype(F32))
    idx = jnp.arange(c, dtype=F32)
    diff = idx[:, None] - idx[None, :]
    if reverse:
        diff = -diff
    intra = jnp.where(diff >= 0, jnp.exp(lg[:, None, None] * jnp.maximum(diff, 0.0)), 0.0)
    fwd_idx = (c - 1.0 - idx) if reverse else idx
    qdec = jnp.exp(lg[:, None] * (fwd_idx + 1.0))[:, :, None]
    kdec = jnp.exp(lg[:, None] * (c - 1.0 - fwd_idx))[:, :, None]
    cdec = jnp.exp(lg * c)[:, None, None]
    return intra, qdec, kdec, cdec


def _window_sum(h_ref, cols, tmp_refs, w, tm, halo):
    rows = tm + 2 * halo
    src, src_cols, span, k = h_ref, cols, 1, 0
    while 2 * span < w:
        n = rows - 2 * span + 1
        dst = tmp_refs[k % 2]
        dst[0:n, :] = src[0:n, src_cols] + src[span:span + n, src_cols]
        src, src_cols, span, k = dst, slice(None), 2 * span, k + 1
    lo = halo - w // 2
    return src[lo:lo + tm, src_cols] + src[lo + span:lo + span + tm, src_cols]


def _pool_kernel(x_ref, xp_ref, xn_ref, g_ref, mod_ref, w_ref, ps_ref, o_ref, h_ref, tmp_a, tmp_b, *, mod_row, seq):
    tm, d = x_ref.shape
    halo = POOL_HALO
    i = pl.program_id(0)
    g0 = g_ref[0:1, :]
    shift = _mod_slice(mod_ref, mod_row, SH1, d)
    scale = _mod_slice(mod_ref, mod_row, SC1, d)
    hp = _norm_mod(xp_ref[...], g0, shift, scale)
    hn = _norm_mod(xn_ref[...], g0, shift, scale)
    h_ref[0:halo, :] = jnp.where(i > 0, hp, 0.0)
    h_ref[halo:halo + tm, :] = _norm_mod(x_ref[...], g0, shift, scale)
    h_ref[halo + tm:, :] = jnp.where(i < pl.num_programs(0) - 1, hn, 0.0)

    t = i * tm + lax.broadcasted_iota(jnp.int32, (tm, 1), 0)
    gw = d // len(POOL_WINDOWS)
    ys = []
    for g, w in enumerate(POOL_WINDOWS):
        cols = slice(g * gw, (g + 1) * gw)
        tot = _window_sum(h_ref, cols, (tmp_a, tmp_b), w, tm, halo)
        cnt = (jnp.minimum(t + w // 2, seq) - jnp.maximum(t - w // 2, 0)).astype(F32)
        dm = tot / cnt - h_ref[halo:halo + tm, cols]
        ys.append(jnp.dot(dm.astype(BF16), w_ref[g], preferred_element_type=F32))
    y = jnp.concatenate(ys, axis=1) * ps_ref[...]
    o_ref[...] = x_ref[...] + _mod_slice(mod_ref, mod_row, GA1, d) * _rms(y, g_ref[1:2, :])


def pool_layer(x, norm_g, mods, layer, w_pool, pool_scale, *, mod_row, tm):
    m, d = x.shape
    tm = min(tm, m)
    nt = m // tm
    hb = tm // POOL_HALO
    return pl.pallas_call(
        functools.partial(_pool_kernel, mod_row=mod_row, seq=m),
        grid=(nt,),
        in_specs=[
            pl.BlockSpec((tm, d), lambda i: (i, 0)),
            pl.BlockSpec((POOL_HALO, d), lambda i: (jnp.maximum(i * hb - 1, 0), 0)),
            pl.BlockSpec((POOL_HALO, d), lambda i: (jnp.minimum((i + 1) * hb, nt * hb - 1), 0)),
            pl.BlockSpec((None,) + norm_g.shape[1:], lambda i: (layer, 0, 0)),
            pl.BlockSpec((None,) + mods.shape[1:], lambda i: (layer, 0, 0)),
            pl.BlockSpec(w_pool.shape, lambda i: (0, 0, 0)),
            pl.BlockSpec((1, d), lambda i: (0, 0)),
        ],
        out_specs=pl.BlockSpec((tm, d), lambda i: (i, 0)),
        out_shape=jax.ShapeDtypeStruct((m, d), F32),
        scratch_shapes=[pltpu.VMEM((tm + 2 * POOL_HALO, d), F32)]
        + [pltpu.VMEM((tm + 2 * POOL_HALO, d // len(POOL_WINDOWS)), F32)] * 2,
        compiler_params=_cparams("parallel"),
        name="pool_layer",
    )(x, x, x, norm_g, mods, w_pool, pool_scale.reshape(1, d))


SWIGLU_SUB = 256


def _swiglu_partial(h, w1_ref, w3_ref, w2_ref, lo, hi):
    assert (hi - lo) % SWIGLU_SUB == 0
    acc = None
    for s in range(lo, hi, SWIGLU_SUB):
        cols = slice(s, s + SWIGLU_SUB)
        a = jnp.dot(h, w1_ref[:, cols], preferred_element_type=F32)
        b = jnp.dot(h, w3_ref[:, cols], preferred_element_type=F32)
        u = (a * jax.nn.sigmoid(a) * b).astype(BF16)
        p = jnp.dot(u, w2_ref[cols, :], preferred_element_type=F32)
        acc = p if acc is None else acc + p
    return acc


def _ffn_kernel(x_ref, g_ref, mod_ref, w1_ref, w3_ref, w2_ref, *rest, mod_row, n_cast, nf):
    cast_src, o_ref, cast_dst = rest[:n_cast], rest[n_cast], rest[n_cast + 1:2 * n_cast + 1]
    h_ref, acc_ref = rest[2 * n_cast + 1:]
    d = x_ref.shape[1]
    f = pl.program_id(1)
    for src, dst in zip(cast_src, cast_dst):
        dst[...] = src[...].astype(dst.dtype)

    assert nf >= 2
    n_sub = w1_ref.shape[1] // SWIGLU_SUB
    bounds = [-(-n_sub * c // nf) * SWIGLU_SUB for c in range(nf + 1)]
    for c in range(nf):
        @pl.when(f == c)
        def _(c=c):
            if c == 0:
                h = _norm_mod(x_ref[...], g_ref[2:3, :], _mod_slice(mod_ref, mod_row, SH2, d),
                              _mod_slice(mod_ref, mod_row, SC2, d)).astype(BF16)
                h_ref[...] = h
            else:
                h = h_ref[...]
            part = _swiglu_partial(h, w1_ref, w3_ref, w2_ref, bounds[c], bounds[c + 1])
            if c == 0:
                acc_ref[...] = part
            elif c < nf - 1:
                acc_ref[...] += part
            else:
                y = acc_ref[...] + part
                o_ref[...] = x_ref[...] + _mod_slice(mod_ref, mod_row, GA2, d) * _rms(y, g_ref[3:4, :])


def ffn_layer(x, norm_g, mods, layer, w1, w3, w2, widx, *, mod_row, tm, tf, cast=None):
    m, d = x.shape
    ff = w1.shape[2]
    tm = min(tm, m)
    nf = ff // tf
    steps = (m // tm) * nf
    in_specs = [
        pl.BlockSpec((tm, d), lambda i, f: (i, 0)),
        pl.BlockSpec((None,) + norm_g.shape[1:], lambda i, f: (layer, 0, 0)),
        pl.BlockSpec((None,) + mods.shape[1:], lambda i, f: (layer, 0, 0)),
        pl.BlockSpec((None, d, ff), lambda i, f: (widx, 0, 0), pipeline_mode=pl.Buffered(1)),
        pl.BlockSpec((None, d, ff), lambda i, f: (widx, 0, 0), pipeline_mode=pl.Buffered(1)),
        pl.BlockSpec((None, ff, d), lambda i, f: (widx, 0, 0), pipeline_mode=pl.Buffered(1)),
    ]
    out_specs = [pl.BlockSpec((tm, d), lambda i, f: (i, 0))]
    out_shape = [jax.ShapeDtypeStruct((m, d), F32)]
    cast_arrays, cast_idx = cast if cast is not None else ((), 0)
    for a in cast_arrays:
        _, ne, r, c = a.shape
        per_expert = steps // ne
        assert per_expert * ne == steps and r % (per_expert * 16) == 0
        rows = r // per_expert
        in_specs.append(pl.BlockSpec(
            (None, None, rows, c),
            lambda i, f, pe=per_expert: (cast_idx, (i * nf + f) // pe, (i * nf + f) % pe, 0)))
        out_specs.append(pl.BlockSpec(
            (None, rows, c), lambda i, f, pe=per_expert: ((i * nf + f) // pe, (i * nf + f) % pe, 0)))
        out_shape.append(jax.ShapeDtypeStruct((ne, r, c), BF16))
    outs = pl.pallas_call(
        functools.partial(_ffn_kernel, mod_row=mod_row, n_cast=len(cast_arrays), nf=nf),
        grid=(m // tm, nf),
        in_specs=in_specs,
        out_specs=out_specs,
        out_shape=out_shape,
        scratch_shapes=[pltpu.VMEM((tm, d), BF16), pltpu.VMEM((tm, d), F32)],
        compiler_params=_cparams("arbitrary", "arbitrary"),
        name="ffn_layer",
    )(x, norm_g, mods, w1, w3, w2, *cast_arrays)
    return outs[0] if cast is None else (outs[0], tuple(outs[1:]))


RUN_ALIGN = 16
RUN_BITS = tuple(range(9, 3, -1))
EXPERT_ROWS = 512
MOE_TOKEN_TILE = 512
TAIL_BITS = tuple(range(9, 3, -1))


def _split_dma(length, src, src_row, dst, dst_row, sem, bits, *, start, src_advances=True):
    for b in bits:
        size = 1 << b
        off = (length >> (b + 1)) << (b + 1)

        @pl.when(((length >> b) & 1) == 1)
        def _():
            s_row = pl.multiple_of(src_row + off, RUN_ALIGN) if src_advances else src_row
            cp = pltpu.make_async_copy(src.at[pl.ds(s_row, size)],
                                       dst.at[pl.ds(pl.multiple_of(dst_row + off, RUN_ALIGN), size)], sem)
            if start:
                cp.start()
            else:
                cp.wait()


def _run_dmas(tile, ne, lpad_ref, loff_ref, base_ref, buf, hbm_ref, sem, *, to_hbm, start):
    for e in range(ne):
        length = lpad_ref[tile * ne + e]
        lo = loff_ref[tile * ne + e]
        gb = base_ref[tile * ne + e]
        if to_hbm:
            _split_dma(length, buf, lo, hbm_ref, gb, sem, RUN_BITS, start=start)
        else:
            _split_dma(length, hbm_ref, gb, buf, lo, sem, RUN_BITS, start=start)


def _split_bf16(a):
    hi = a.astype(BF16)
    return hi, (a - hi.astype(F32)).astype(BF16)


def _dot_f32x3(a, b_hi, b_lo):
    a_hi, a_lo = _split_bf16(a)
    dot = functools.partial(jnp.dot, preferred_element_type=F32)
    return dot(a_hi, b_hi) + (dot(a_hi, b_lo) + dot(a_lo, b_hi))


def _local_rows(ri_ref, loff_ref, tile, ne):
    e1, e2 = ri_ref[:, 0:1], ri_ref[:, 1:2]
    lo1, lo2 = ri_ref[:, 2:3], ri_ref[:, 3:4]
    for e in range(ne):
        off = loff_ref[tile * ne + e]
        lo1 = lo1 + jnp.where(e1 == e, off, 0)
        lo2 = lo2 + jnp.where(e2 == e, off, 0)
    return lo1, lo2


def _router_kernel(x_ref, g_ref, mod_ref, wr_hi_ref, wr_lo_ref, br_ref, ri_ref, rw_ref, cnt_ref, *, mod_row):
    _route_rows(x_ref[...], g_ref, mod_ref, mod_row, wr_hi_ref, wr_lo_ref, br_ref, ri_ref, rw_ref, cnt_ref)


def _route_rows(x, g_ref, mod_ref, mod_row, wr_hi_ref, wr_lo_ref, br_ref, ri_ref, rw_ref, cnt_ref):
    tm, d = x.shape
    ne = wr_hi_ref.shape[1]
    h = _norm_mod(x, g_ref[2:3, :], _mod_slice(mod_ref, mod_row, SH2, d), _mod_slice(mod_ref, mod_row, SC2, d))
    logits = _dot_f32x3(h, wr_hi_ref[...], wr_lo_ref[...]) + br_ref[...]
    lane = lax.broadcasted_iota(jnp.int32, logits.shape, 1)
    v1 = jnp.max(logits, axis=-1, keepdims=True)
    i1 = jnp.min(jnp.where(logits == v1, lane, ne), axis=-1, keepdims=True)
    rest = jnp.where(lane == i1, -jnp.inf, logits)
    v2 = jnp.max(rest, axis=-1, keepdims=True)
    i2 = jnp.min(jnp.where(rest == v2, lane, ne), axis=-1, keepdims=True)
    e2 = jnp.exp(v2 - v1)
    den = 1.0 + e2
    oh1 = (lane == i1).astype(F32)
    oh2 = (lane == i2).astype(F32)
    both = oh1 + oh2
    r = lax.broadcasted_iota(jnp.int32, (tm, tm), 0)
    c = lax.broadcasted_iota(jnp.int32, (tm, tm), 1)
    tri = jnp.where(c < r, 1.0, 0.0).astype(BF16)
    rank_all = jnp.dot(tri, both.astype(BF16), preferred_element_type=F32)
    rank1 = jnp.sum(rank_all * oh1, axis=-1, keepdims=True).astype(jnp.int32)
    rank2 = jnp.sum(rank_all * oh2, axis=-1, keepdims=True).astype(jnp.int32)
    col = lax.broadcasted_iota(jnp.int32, ri_ref.shape, 1)
    ri_ref[...] = jnp.where(col == 0, i1, jnp.where(col == 1, i2, jnp.where(col == 2, rank1, rank2)))
    colw = lax.broadcasted_iota(jnp.int32, rw_ref.shape, 1)
    rw_ref[...] = jnp.where(colw == 0, 1.0 / den, e2 / den)
    cnt_ref[...] = jnp.sum(both, axis=0, keepdims=True).astype(jnp.int32)


def moe_router(x, norm_g, mods, layer, w_router, b_router, *, mod_row, tm):
    m, d = x.shape
    ne = w_router.shape[1]
    return pl.pallas_call(
        functools.partial(_router_kernel, mod_row=mod_row),
        grid=(m // tm,),
        in_specs=[
            pl.BlockSpec((tm, d), lambda i: (i, 0)),
            pl.BlockSpec((None,) + norm_g.shape[1:], lambda i: (layer, 0, 0)),
            pl.BlockSpec((None,) + mods.shape[1:], lambda i: (layer, 0, 0)),
            pl.BlockSpec((d, ne), lambda i: (0, 0)),
            pl.BlockSpec((d, ne), lambda i: (0, 0)),
            pl.BlockSpec((1, ne), lambda i: (0, 0)),
        ],
        out_specs=[
            pl.BlockSpec((tm, 4), lambda i: (i, 0)),
            pl.BlockSpec((tm, 2), lambda i: (i, 0)),
            pl.BlockSpec((None, 1, ne), lambda i: (i, 0, 0)),
        ],
        out_shape=[
            jax.ShapeDtypeStruct((m, 4), jnp.int32),
            jax.ShapeDtypeStruct((m, 2), F32),
            jax.ShapeDtypeStruct((m // tm, 1, ne), jnp.int32),
        ],
        compiler_params=_cparams("parallel"),
        name="moe_router",
    )(x, norm_g, mods, *_split_bf16(w_router), b_router.reshape(1, ne))


def _dispatch_kernel(wlen_ref, loff_ref, base_ref, lrow_ref, carry_ref, prev_ref, tlen_ref, tstart_ref, nu_ref,
                     x_ref, g_ref, mod_ref, ri_ref, *rest, mod_row, ne, tile0, first):
    xs_ref, cbuf, zbuf, sem = rest if first else rest[1:]
    tm, d = x_ref.shape
    rc = cbuf.shape[1]
    i = pl.program_id(0)
    last = i == pl.num_programs(0) - 1
    slot = i % 2

    def zero_tails(start):
        for e in range(ne):
            _split_dma(tlen_ref[e], zbuf, 0, xs_ref, tstart_ref[e], sem.at[2], TAIL_BITS, start=start,
                       src_advances=False)

        def unused_tile(t, carry):
            cp = pltpu.make_async_copy(
                zbuf, xs_ref.at[pl.ds(pl.multiple_of(t * EXPERT_ROWS, EXPERT_ROWS), EXPERT_ROWS)], sem.at[2])
            if start:
                cp.start()
            else:
                cp.wait()
            return carry

        lax.fori_loop(nu_ref[0], xs_ref.shape[0] // EXPERT_ROWS, unused_tile, 0)

    if first:
        @pl.when(i == 0)
        def _():
            zbuf[...] = jnp.zeros_like(zbuf)
            zero_tails(True)

    h = _norm_mod(x_ref[...], g_ref[2:3, :], _mod_slice(mod_ref, mod_row, SH2, d),
                  _mod_slice(mod_ref, mod_row, SC2, d)).astype(BF16)
    lo1, lo2 = _local_rows(ri_ref, lrow_ref, i + tile0, ne)
    lane = lax.broadcasted_iota(jnp.int32, (tm, rc), 1)
    onehot = jnp.where((lane == lo1) | (lane == lo2), 1.0, 0.0).astype(BF16)
    sorted_rows = lax.dot_general(onehot, h, (((0,), (0,)), ((), ())), preferred_element_type=F32)
    cbuf[slot] = sorted_rows.astype(BF16)
    for e in range(ne):
        @pl.when(carry_ref[(i + tile0) * ne + e] > 0)
        def _(e=e):
            here = pl.ds(pl.multiple_of(loff_ref[(i + tile0) * ne + e], RUN_ALIGN), RUN_ALIGN)
            prev = pl.ds(pl.multiple_of(prev_ref[(i + tile0) * ne + e], RUN_ALIGN), RUN_ALIGN)
            cbuf[slot, here, :] = (cbuf[slot, here, :].astype(F32)
                                   + cbuf[1 - slot, prev, :].astype(F32)).astype(BF16)

    moves = functools.partial(_run_dmas, ne=ne, lpad_ref=wlen_ref, loff_ref=loff_ref, base_ref=base_ref,
                              hbm_ref=xs_ref, to_hbm=True)
    moves(i + tile0, buf=cbuf.at[slot], sem=sem.at[slot], start=True)

    @pl.when(i > 0)
    def _():
        moves(i + tile0 - 1, buf=cbuf.at[1 - slot], sem=sem.at[1 - slot], start=False)

    @pl.when(last)
    def _():
        moves(i + tile0, buf=cbuf.at[slot], sem=sem.at[slot], start=False)
        if first:
            zero_tails(False)


def moe_dispatch(x, norm_g, mods, layer, route_i, tabs, rows, xs=None, *, mod_row, tm, ne, tile0):
    m, d = x.shape
    rc = 2 * tm + ne * 2 * RUN_ALIGN
    first = xs is None
    in_specs = [
        pl.BlockSpec((tm, d), lambda i, *_: (i, 0)),
        pl.BlockSpec((None,) + norm_g.shape[1:], lambda i, *_: (layer, 0, 0)),
        pl.BlockSpec((None,) + mods.shape[1:], lambda i, *_: (layer, 0, 0)),
        pl.BlockSpec((tm, 4), lambda i, *_: (i, 0)),
    ]
    args = [*tabs, x, norm_g, mods, route_i]
    if not first:
        in_specs.append(pl.BlockSpec(memory_space=pl.ANY))
        args.append(xs)
    return pl.pallas_call(
        functools.partial(_dispatch_kernel, mod_row=mod_row, ne=ne, tile0=tile0, first=first),
        grid_spec=pltpu.PrefetchScalarGridSpec(
            num_scalar_prefetch=len(tabs),
            grid=(m // tm,),
            in_specs=in_specs,
            out_specs=pl.BlockSpec(memory_space=pl.ANY),
            scratch_shapes=[pltpu.VMEM((2, rc, d), BF16), pltpu.VMEM((EXPERT_ROWS, d), BF16),
                            pltpu.SemaphoreType.DMA((3,))],
        ),
        out_shape=jax.ShapeDtypeStruct((rows, d), BF16),
        input_output_aliases={} if first else {len(args) - 1: 0},
        compiler_params=_cparams("arbitrary"),
        name="moe_dispatch",
    )(*args)


EXPERT_ROW_STEP = 128


def _experts_kernel(te_ref, nu_ref, tr_ref, x_ref, w1_ref, w3_ref, w2_ref, y_ref, acc_ref, *, n_chunks):
    del te_ref, nu_ref
    i = pl.program_id(0)
    f = pl.program_id(1)
    nf = pl.num_programs(1)
    tm = x_ref.shape[0]
    live_rows = tr_ref[i]
    steps = (live_rows + EXPERT_ROW_STEP - 1) // EXPERT_ROW_STEP

    for c in range(n_chunks):
        for q in range(1, tm // EXPERT_ROW_STEP + 1):
            @pl.when((f == c) & (steps == q))
            def _(c=c, rows=q * EXPERT_ROW_STEP):
                part = _swiglu_partial(x_ref[0:rows, :], w1_ref, w3_ref, w2_ref, 0, w1_ref.shape[1])
                if c == 0:
                    acc_ref[0:rows, :] = part
                elif c < n_chunks - 1:
                    acc_ref[0:rows, :] += part
                else:
                    y_ref[0:rows, :] = (acc_ref[0:rows, :] + part).astype(y_ref.dtype)
                    if rows < tm:
                        y_ref[rows:, :] = jnp.zeros((tm - rows, y_ref.shape[1]), y_ref.dtype)

    @pl.when((steps == 0) & (f == nf - 1))
    def _():
        y_ref[...] = jnp.zeros_like(y_ref)


def moe_experts(xs, tile_expert, n_used, tile_rows, w1, w3, w2, *, tm, tf):
    rows, d = xs.shape
    ff = w1.shape[2]
    nf = ff // tf
    assert nf >= 2

    def chunk(i, f, nu):
        return jnp.where(i < nu[0], f, nf - 1)

    def wmap_in(i, f, te, nu, tr):
        return (te[i], 0, chunk(i, f, nu))

    def wmap_out(i, f, te, nu, tr):
        return (te[i], chunk(i, f, nu), 0)

    def xmap(i, f, te, nu, tr):
        return (jnp.maximum(jnp.minimum(i, nu[0] - 1), 0), 0)

    return pl.pallas_call(
        functools.partial(_experts_kernel, n_chunks=nf),
        grid_spec=pltpu.PrefetchScalarGridSpec(
            num_scalar_prefetch=3,
            grid=(rows // tm, nf),
            in_specs=[
                pl.BlockSpec((tm, d), xmap),
                pl.BlockSpec((None, d, tf), wmap_in),
                pl.BlockSpec((None, d, tf), wmap_in),
                pl.BlockSpec((None, tf, d), wmap_out),
            ],
            out_specs=pl.BlockSpec((tm, d), lambda i, f, te, nu, tr: (i, 0)),
            scratch_shapes=[pltpu.VMEM((tm, d), F32)],
        ),
        out_shape=jax.ShapeDtypeStruct((rows, d), BF16),
        compiler_params=_cparams("arbitrary", "arbitrary"),
        name="moe_experts",
    )(tile_expert, n_used, tile_rows, xs, w1, w3, w2)


def _combine_kernel(lpad_ref, loff_ref, base_ref, lrow_ref, x_ref, ri_ref, rw_ref, g_ref, mod_ref, ys_ref, o_ref,
                    ybuf, sem, *, mod_row, ne, tile0):
    tm, d = x_ref.shape
    rc = ybuf.shape[1]
    i = pl.program_id(0)
    slot = i % 2
    moves = functools.partial(_run_dmas, ne=ne, lpad_ref=lpad_ref, loff_ref=loff_ref, base_ref=base_ref,
                              hbm_ref=ys_ref, to_hbm=False)

    @pl.when(i == 0)
    def _():
        ybuf[...] = jnp.zeros_like(ybuf)
        moves(tile0, buf=ybuf.at[0], sem=sem.at[0], start=True)

    @pl.when(i + 1 < pl.num_programs(0))
    def _():
        moves(i + tile0 + 1, buf=ybuf.at[1 - slot], sem=sem.at[1 - slot], start=True)

    moves(i + tile0, buf=ybuf.at[slot], sem=sem.at[slot], start=False)
    lo1, lo2 = _local_rows(ri_ref, lrow_ref, i + tile0, ne)
    lane = lax.broadcasted_iota(jnp.int32, (tm, rc), 1)
    rows = ybuf[slot]
    y1 = jnp.dot(jnp.where(lane == lo1, 1.0, 0.0).astype(BF16), rows, preferred_element_type=F32)
    y2 = jnp.dot(jnp.where(lane == lo2, 1.0, 0.0).astype(BF16), rows, preferred_element_type=F32)
    y = rw_ref[:, 0:1] * y1 + rw_ref[:, 1:2] * y2
    o_ref[...] = x_ref[...] + _mod_slice(mod_ref, mod_row, GA2, d) * _rms(y, g_ref[3:4, :])


def moe_combine(x, norm_g, mods, layer, route_i, route_w, ys, tabs, *, mod_row, tm, ne, tile0):
    m, d = x.shape
    rc = 2 * tm + ne * 2 * RUN_ALIGN
    return pl.pallas_call(
        functools.partial(_combine_kernel, mod_row=mod_row, ne=ne, tile0=tile0),
        grid_spec=pltpu.PrefetchScalarGridSpec(
            num_scalar_prefetch=len(tabs),
            grid=(m // tm,),
            in_specs=[
                pl.BlockSpec((tm, d), lambda i, *_: (i, 0)),
                pl.BlockSpec((tm, 4), lambda i, *_: (i, 0)),
                pl.BlockSpec((tm, 2), lambda i, *_: (i, 0)),
                pl.BlockSpec((None,) + norm_g.shape[1:], lambda i, *_: (layer, 0, 0)),
                pl.BlockSpec((None,) + mods.shape[1:], lambda i, *_: (layer, 0, 0)),
                pl.BlockSpec(memory_space=pl.ANY),
            ],
            out_specs=pl.BlockSpec((tm, d), lambda i, *_: (i, 0)),
            scratch_shapes=[pltpu.VMEM((2, rc, d), BF16), pltpu.SemaphoreType.DMA((2,))],
        ),
        out_shape=jax.ShapeDtypeStruct((m, d), F32),
        compiler_params=_cparams("arbitrary"),
        name="moe_combine",
    )(*tabs, x, route_i, route_w, norm_g, mods, ys)


def sparse_moe_layer(streams, norm_g, mods, layer, w_router, b_router, w1, w3, w2, *, tf, routed_first=None):
    ne = w_router.shape[1]
    tm_exp = EXPERT_ROWS
    routed = [moe_router(x, norm_g, mods, layer, w_router, b_router, mod_row=row, tm=tm)
              if k > 0 or routed_first is None else routed_first for k, (x, row, tm) in enumerate(streams)]
    align = lambda a: (a + RUN_ALIGN - 1) // RUN_ALIGN * RUN_ALIGN
    per_stream = [r[2][:, 0, :] for r in routed]
    counts = jnp.concatenate(per_stream, axis=0)
    nt = counts.shape[0]
    m_total = sum(x.shape[0] for x, _, _ in streams)
    stream_rows = [align(jnp.sum(c, axis=0)) for c in per_stream]
    stream_off = jnp.cumsum(jnp.stack(stream_rows), axis=0) - jnp.stack(stream_rows)
    pos = jnp.concatenate([stream_off[k][None, :] + jnp.cumsum(c, axis=0) - c for k, c in enumerate(per_stream)])
    carry = pos % RUN_ALIGN
    lpad = align(carry + counts)
    stream_end = jnp.concatenate([jnp.arange(c.shape[0]) == c.shape[0] - 1 for c in per_stream])[:, None]
    wlen = jnp.where(stream_end, lpad, (carry + counts) // RUN_ALIGN * RUN_ALIGN)
    loff = jnp.cumsum(lpad, axis=1) - lpad
    prev_block = jnp.concatenate([jnp.zeros((1, ne), loff.dtype), (loff + lpad - RUN_ALIGN)[:-1]], axis=0)
    group = sum(stream_rows)
    gpad = (group + tm_exp - 1) // tm_exp * tm_exp
    ends = jnp.cumsum(gpad)
    starts = ends - gpad
    base = starts[None, :] + pos - carry
    n_tiles = -(-(2 * m_total + len(streams) * ne * (RUN_ALIGN - 1)) // tm_exp) + ne
    tile_start = jnp.arange(n_tiles, dtype=jnp.int32) * tm_exp
    n_used = (ends[-1] // tm_exp).astype(jnp.int32).reshape(1)
    tile_expert = jnp.sum(tile_start[:, None] >= ends[None, :], axis=1).astype(jnp.int32)
    tile_expert = jnp.minimum(tile_expert, tile_expert[jnp.maximum(n_used[0] - 1, 0)])
    tile_rows = jnp.clip((starts + group)[tile_expert] - tile_start, 0, tm_exp).astype(jnp.int32)
    group_first = stream_rows[0]
    later_rows = m_total - streams[0][0].shape[0] + (len(streams) - 1) * (RUN_ALIGN - 1)
    assert later_rows + tm_exp - RUN_ALIGN < 2 << TAIL_BITS[0]
    flat = lambda *arrays: [a.reshape(-1).astype(jnp.int32) for a in arrays]
    tabs = flat(lpad, loff, base, loff + carry)
    write_tabs = flat(wlen, loff, base, loff + carry, carry, prev_block,
                      gpad - group_first, starts + group_first, n_used)
    xs, tile0 = None, 0
    for (x, row, tm), (route_i, _, _) in zip(streams, routed):
        xs = moe_dispatch(x, norm_g, mods, layer, route_i, write_tabs, n_tiles * tm_exp, xs,
                          mod_row=row, tm=tm, ne=ne, tile0=tile0)
        tile0 += x.shape[0] // tm
    ys = moe_experts(xs, tile_expert, n_used, tile_rows, w1, w3, w2, tm=tm_exp, tf=tf)
    outs, tile0 = [], 0
    for (x, row, tm), (route_i, route_w, _) in zip(streams, routed):
        outs.append(moe_combine(x, norm_g, mods, layer, route_i, route_w, ys, tabs,
                                mod_row=row, tm=tm, ne=ne, tile0=tile0))
        tile0 += x.shape[0] // tm
    return outs


def _rope_tables(seq):
    n = 16
    inv = ROPE_BASE ** (-np.arange(n, dtype=np.float64) / n)
    t = np.arange(seq)
    row_ang = (t // GRID_W)[:, None] * inv[None, :]
    col_ang = (t % GRID_W)[:, None] * inv[None, :]
    cos = np.concatenate([np.cos(row_ang)] * 2 + [np.cos(col_ang)] * 2, axis=1)
    sin = np.concatenate([-np.sin(row_ang), np.sin(row_ang), -np.sin(col_ang), np.sin(col_ang)], axis=1)
    return (jnp.asarray(np.tile(cos, (1, 2)), dtype=F32), jnp.asarray(np.tile(sin, (1, 2)), dtype=F32))


def kernel(x, c, ctx, c_ctx, w_ada, b_ada, norm_g, attn_w_qkv, attn_w_o, attn_sink, ret_w_in, ret_w_o,
           ret_log_decay, pool_w, pool_scale, ffn_w1, ffn_w3, ffn_w2, moe_w_router, moe_b_router,
           moe_w1, moe_w3, moe_w2):
    batch, seq, d = x.shape
    assert batch == 1 and c.shape[0] == 1
    depth = w_ada.shape[0]
    lc = ctx.shape[1]
    xl = x.reshape(seq, d)
    xc = ctx.reshape(lc, d)

    cvecs = jnp.zeros((8, d), F32).at[LAT_ROW].set(c[0]).at[CTX_ROW].set(c_ctx)
    mods = ada_table(cvecs, w_ada, b_ada)
    rope = _rope_tables(seq)

    hd = attn_w_o.shape[1]
    dh = hd // ATTN_HEADS
    qkv_scale = jnp.concatenate([jnp.full((hd,), dh ** -0.5, F32),
                                 jnp.ones((attn_w_qkv.shape[2] - hd,), F32)])
    dk = d // RET_HEADS
    ret_scale = jnp.concatenate([jnp.ones((d,), F32), jnp.full((d,), dk ** -0.5, F32),
                                 jnp.ones((ret_w_in.shape[2] - 2 * d,), F32)])

    moe_bf16 = None
    for i in range(depth):
        last = i == depth - 1
        kind, j = i % N_MIXERS, i // N_MIXERS
        proj = functools.partial(mixer_projection, norm_g=norm_g, mods=mods, layer=i)
        routed_lat = None
        if i % 2 == 1:
            mix_out = functools.partial(outproj_residual, norm_g=norm_g, mods=mods, layer=i, g_row=1,
                                        mod_row=LAT_ROW, slot=GA1, tm=min(MOE_TOKEN_TILE, seq),
                                        router=(moe_w_router[i // 2], moe_b_router[i // 2]))
        else:
            mix_out = lambda x, **kw: (outproj_residual(x, norm_g, mods, i, g_row=1, mod_row=LAT_ROW,
                                                        slot=GA1, tm=1024, **kw), None)
        if kind == 0:
            w_qkv = (attn_w_qkv[j] * qkv_scale).astype(BF16)
            w_o = attn_w_o[j].astype(BF16)
            kvd = (w_qkv.shape[1] - hd) // 2
            qkv_l = proj(xl, w=w_qkv, mod_row=LAT_ROW, tm=512, chunk=256, q_cols=hd, rope=rope, rope_cols=hd + kvd)
            qkv_c = proj(xc, w=w_qkv, mod_row=CTX_ROW, tm=256, chunk=256, q_cols=hd)
            o_l = attention(attn_sink[j], qkv_l, qkv_c, has_local=True)
            xl, routed_lat = mix_out(xl, w=w_o, ys=[o_l])
            if not last:
                o_c = attention(attn_sink[j], qkv_c, qkv_c, has_local=False)
                xc = outproj_residual(xc, norm_g, mods, i, w_o, [o_c], g_row=1, mod_row=CTX_ROW, slot=GA1, tm=256)
        elif kind == 1:
            w_in = (ret_w_in[j] * ret_scale).astype(BF16)
            w_o = ret_w_o[j].astype(BF16)
            p_c = proj(xc, w=w_in, mod_row=CTX_ROW, tm=256, chunk=1024)
            p_l = proj(xl, w=w_in, mod_row=LAT_ROW, tm=512, chunk=1024)
            s0 = jnp.zeros((RET_HEADS, dk, 2 * dk), F32)
            tabs_f = retention_tables(ret_log_decay[j, 0], min(RET_CHUNK, lc), False)
            tabs_b = retention_tables(ret_log_decay[j, 1], min(RET_CHUNK, lc), True)
            zf_c, zb_c, s_f, s_b = retention_scan(p_c, tabs_f, tabs_b, s0, s0)
            zf_l, zb_l, _, _ = retention_scan(p_l, tabs_f, tabs_b, s_f, s_b)
            xl, routed_lat = mix_out(xl, w=w_o, ys=[zf_l, zb_l])
            if not last:
                xc = outproj_residual(xc, norm_g, mods, i, w_o, [zf_c, zb_c], g_row=1, mod_row=CTX_ROW, slot=GA1, tm=256)
        else:
            w_p = pool_w[j].astype(BF16)
            xl = pool_layer(xl, norm_g, mods, i, w_p, pool_scale[j], mod_row=LAT_ROW, tm=1024)
            if not last:
                xc = pool_layer(xc, norm_g, mods, i, w_p, pool_scale[j], mod_row=CTX_ROW, tm=256)

        f = i // 2
        if i % 2 == 0:
            w1, w3, w2 = ffn_w1.astype(BF16), ffn_w3.astype(BF16), ffn_w2.astype(BF16)
            tf = w1.shape[2] // 2
            steps = (seq // min(512, seq)) * 2
            ne = moe_w1.shape[1]
            if not last and steps % ne == 0 and d % (steps // ne * 16) == 0:
                xl, moe_bf16 = ffn_layer(xl, norm_g, mods, i, w1, w3, w2, f, mod_row=LAT_ROW, tm=512, tf=tf,
                                         cast=((moe_w1, moe_w3, moe_w2), (i + 1) // 2))
            else:
                xl = ffn_layer(xl, norm_g, mods, i, w1, w3, w2, f, mod_row=LAT_ROW, tm=512, tf=tf)
            if not last:
                xc = ffn_layer(xc, norm_g, mods, i, w1, w3, w2, f, mod_row=CTX_ROW, tm=256, tf=tf)
        else:
            if moe_bf16 is None:
                moe_bf16 = moe_w1[f].astype(BF16), moe_w3[f].astype(BF16), moe_w2[f].astype(BF16)
            (w1, w3, w2), moe_bf16 = moe_bf16, None
            tf = w1.shape[2] // 2
            streams = [(xl, LAT_ROW, min(MOE_TOKEN_TILE, seq))] + ([] if last else [(xc, CTX_ROW, min(256, lc))])
            outs = sparse_moe_layer(streams, norm_g, mods, i, moe_w_router[f], moe_b_router[f], w1, w3, w2, tf=tf,
                                    routed_first=routed_lat)
            xl = outs[0]
            if not last:
                xc = outs[1]
    return xl.reshape(batch, seq, d)
```

```python
import functools

import jax
import jax.numpy as jnp
import numpy as np
from jax import lax
from jax.experimental import pallas as pl
from jax.experimental.pallas import tpu as pltpu

F32 = jnp.float32
BF16 = jnp.bfloat16

EPS = 1e-6
NEG_INF = -1e30
LOG2E = 1.4426950408889634
LANES = 128
VMEM_LIMIT = 56 * 1024 * 1024

GRID_W = 64
N_MIXERS = 3
ATTN_HEADS = 16
ATTN_KV_HEADS = 4
ATTN_BLOCKS_PER_STEP = 4
ROPE_PAIRS = 16
WINDOW = 128
ROPE_BASE = 10000.0
RET_HEADS = 4
RET_CHUNK = 256
POOL_WINDOWS = (2, 4, 8, 16)
POOL_HALO = 8

SH1, SC1, GA1, SH2, SC2, GA2 = range(6)
LAT_ROW, CTX_ROW = 0, 1


def _cparams(*sem):
    return pltpu.CompilerParams(dimension_semantics=sem, vmem_limit_bytes=VMEM_LIMIT)


def _rms(x, g):
    return x * lax.rsqrt(jnp.mean(x * x, axis=-1, keepdims=True) + EPS) * g


def _mod_slice(mod_ref, row, slot, d):
    return mod_ref[row:row + 1, slot * d:(slot + 1) * d]


def _norm_mod(x, g, shift, scale):
    return _rms(x, g) * (1.0 + scale) + shift


def _ada_kernel(c_ref, w_ref, b_ref, o_ref):
    c = c_ref[...]
    s = c * jax.nn.sigmoid(c)
    o_ref[...] = _dot_f32x3(s, *_split_bf16(w_ref[...])) + b_ref[...]


def ada_table(cvecs, w_ada, b_ada):
    depth, d, n = w_ada.shape
    tn = 1536
    return pl.pallas_call(
        _ada_kernel,
        grid=(depth, n // tn),
        in_specs=[
            pl.BlockSpec((8, d), lambda i, j: (0, 0)),
            pl.BlockSpec((None, d, tn), lambda i, j: (i, 0, j)),
            pl.BlockSpec((None, 1, tn), lambda i, j: (i, 0, j)),
        ],
        out_specs=pl.BlockSpec((None, 8, tn), lambda i, j: (i, 0, j)),
        out_shape=jax.ShapeDtypeStruct((depth, 8, n), F32),
        compiler_params=_cparams("parallel", "parallel"),
        name="ada_table",
    )(cvecs, w_ada, b_ada.reshape(depth, 1, n))


def _rope(a, cos, sin):
    lane = lax.broadcasted_iota(jnp.int32, (a.shape[0], LANES), 1)
    first = (lane % (2 * ROPE_PAIRS)) < ROPE_PAIRS
    outs = []
    for cb in range(a.shape[1] // LANES):
        blk = a[:, cb * LANES:(cb + 1) * LANES]
        partner = jnp.where(first, pltpu.roll(blk, LANES - ROPE_PAIRS, 1), pltpu.roll(blk, ROPE_PAIRS, 1))
        outs.append(blk * cos + partner * sin)
    return jnp.concatenate(outs, axis=1)


def _mixer_proj_kernel(x_ref, g_ref, mod_ref, w_ref, *rest, mod_row, chunk, rope_cols, q_cols):
    if rope_cols:
        cos_ref, sin_ref, o_ref = rest
        cos, sin = cos_ref[...], sin_ref[...]
    else:
        (o_ref,) = rest
    d = x_ref.shape[1]
    h = _norm_mod(x_ref[...], g_ref[0:1, :], _mod_slice(mod_ref, mod_row, SH1, d),
                  _mod_slice(mod_ref, mod_row, SC1, d)).astype(BF16)
    for c0 in range(0, o_ref.shape[1], chunk):
        acc = jnp.dot(h, w_ref[:, c0:c0 + chunk], preferred_element_type=F32)
        if c0 < q_cols:
            acc = acc * LOG2E
        if c0 < rope_cols:
            acc = _rope(acc, cos, sin)
        o_ref[:, c0:c0 + chunk] = acc.astype(o_ref.dtype)


def mixer_projection(x, norm_g, mods, layer, w, *, mod_row, tm, chunk, q_cols=0, rope=None, rope_cols=0):
    m, d = x.shape
    n = w.shape[1]
    tm = min(tm, m)
    assert n % chunk == 0 and rope_cols % chunk == 0 and q_cols % chunk == 0
    in_specs = [
        pl.BlockSpec((tm, d), lambda i: (i, 0)),
        pl.BlockSpec((None,) + norm_g.shape[1:], lambda i: (layer, 0, 0)),
        pl.BlockSpec((None,) + mods.shape[1:], lambda i: (layer, 0, 0)),
        pl.BlockSpec((d, n), lambda i: (0, 0), pipeline_mode=pl.Buffered(1)),
    ]
    args = [x, norm_g, mods, w]
    if rope_cols:
        in_specs += [pl.BlockSpec((tm, LANES), lambda i: (i, 0))] * 2
        args += list(rope)
    return pl.pallas_call(
        functools.partial(_mixer_proj_kernel, mod_row=mod_row, chunk=chunk, rope_cols=rope_cols, q_cols=q_cols),
        grid=(m // tm,),
        in_specs=in_specs,
        out_specs=pl.BlockSpec((tm, n), lambda i: (i, 0)),
        out_shape=jax.ShapeDtypeStruct((m, n), BF16),
        compiler_params=_cparams("parallel"),
        name="mixer_projection",
    )(*args)


def _outproj_kernel(x_ref, g_ref, mod_ref, w_ref, *rest, n_y, g_row, mod_row, slot, route):
    y_refs, rest = rest[:n_y], rest[n_y:]
    if route:
        router_refs, o_ref, route_out = rest[:3], rest[3], rest[4:]
    else:
        (o_ref,) = rest
    d = x_ref.shape[1]
    y = y_refs[0][...]
    if n_y == 2:
        y = (y.astype(F32) + y_refs[1][...].astype(F32)).astype(BF16)
    t = jnp.dot(y, w_ref[...], preferred_element_type=F32)
    gate = _mod_slice(mod_ref, mod_row, slot, d)
    x_new = x_ref[...] + gate * _rms(t, g_ref[g_row:g_row + 1, :])
    o_ref[...] = x_new
    if route:
        _route_rows(x_new, g_ref, mod_ref, mod_row, *router_refs, *route_out)


def _router_specs(m, d, ne, tm):
    ins = [pl.BlockSpec((d, ne), lambda i: (0, 0))] * 2 + [pl.BlockSpec((1, ne), lambda i: (0, 0))]
    outs = [pl.BlockSpec((tm, 4), lambda i: (i, 0)), pl.BlockSpec((tm, 2), lambda i: (i, 0)),
            pl.BlockSpec((None, 1, ne), lambda i: (i, 0, 0))]
    shapes = [jax.ShapeDtypeStruct((m, 4), jnp.int32), jax.ShapeDtypeStruct((m, 2), F32),
              jax.ShapeDtypeStruct((m // tm, 1, ne), jnp.int32)]
    return ins, outs, shapes


def outproj_residual(x, norm_g, mods, layer, w, ys, *, g_row, mod_row, slot, tm, router=None):
    m, d = x.shape
    k = w.shape[0]
    tm = min(tm, m)
    in_specs = [
        pl.BlockSpec((tm, d), lambda i: (i, 0)),
        pl.BlockSpec((None,) + norm_g.shape[1:], lambda i: (layer, 0, 0)),
        pl.BlockSpec((None,) + mods.shape[1:], lambda i: (layer, 0, 0)),
        pl.BlockSpec((k, d), lambda i: (0, 0)),
    ] + [pl.BlockSpec((tm, k), lambda i: (i, 0))] * len(ys)
    args = [x, norm_g, mods, w, *ys]
    out_specs = [pl.BlockSpec((tm, d), lambda i: (i, 0))]
    out_shape = [jax.ShapeDtypeStruct((m, d), F32)]
    if router is not None:
        w_router, b_router = router
        ne = w_router.shape[1]
        r_in, r_out, r_shapes = _router_specs(m, d, ne, tm)
        in_specs += r_in
        args += [*_split_bf16(w_router), b_router.reshape(1, ne)]
        out_specs += r_out
        out_shape += r_shapes
    outs = pl.pallas_call(
        functools.partial(_outproj_kernel, n_y=len(ys), g_row=g_row, mod_row=mod_row, slot=slot,
                          route=router is not None),
        grid=(m // tm,),
        in_specs=in_specs,
        out_specs=out_specs,
        out_shape=out_shape,
        compiler_params=_cparams("parallel"),
        name="outproj_residual",
    )(*args)
    return outs[0] if router is None else (outs[0], tuple(outs[1:]))


def _dot_nt(a, b):
    return lax.dot_general(a, b, (((1,), (1,)), ((), ())), preferred_element_type=F32)


def _attn_kernel(sink_ref, q_ref, *rest, tq, nsub, kv_heads, group, dh, has_local):
    if has_local:
        k_refs, v_refs = rest[:nsub + 2], rest[nsub + 2:2 * nsub + 4]
    kc_ref, vc_ref, o_ref = rest[-3:]
    assert 2 * dh == LANES and tq == WINDOW and group % 2 == 0
    b = pl.program_id(0)
    nb = pl.num_programs(0)
    rows = group * tq
    lc = kc_ref.shape[0]
    head_of_row = lax.broadcasted_iota(jnp.int32, (rows, 1), 0) // tq
    lane = lax.broadcasted_iota(jnp.int32, (1, LANES), 1)
    if has_local:
        r = lax.broadcasted_iota(jnp.int32, (rows, tq), 0) % tq
        c = lax.broadcasted_iota(jnp.int32, (rows, tq), 1)
    for sub, j in [(sub, j) for sub in range(nsub) for j in range(kv_heads)]:
        qrows = slice(sub * tq, (sub + 1) * tq)
        hs = slice(j * dh, (j + 1) * dh)
        pair = slice((j // 2) * LANES, (j // 2 + 1) * LANES)
        v_low = j % 2 == 0
        keep_v = (lane < dh) if v_low else (lane >= dh)
        qg = jnp.concatenate(
            [q_ref[qrows, (j * group + g) * dh:(j * group + g + 1) * dh] for g in range(group)], axis=0)
        sink = jnp.full((rows, 1), sink_ref[j * group] * LOG2E, F32)
        for g in range(1, group):
            sink = jnp.where(head_of_row == g, sink_ref[j * group + g] * LOG2E, sink)
        if has_local:
            kp_ref, kq_ref, kn_ref = k_refs[sub:sub + 3]
            vp_ref, vq_ref, vn_ref = v_refs[sub:sub + 3]
            prev_ok = (c >= r) if sub > 0 else (c >= r) & (b > 0)
            next_ok = (c <= r) if sub < nsub - 1 else (c <= r) & (b < nb - 1)
            k_all = jnp.concatenate([kc_ref[:, hs], kp_ref[:, hs], kq_ref[:, hs], kn_ref[:, hs]], axis=0)
            v_all = jnp.concatenate([vc_ref[:, pair], vp_ref[:, pair], vq_ref[:, pair], vn_ref[:, pair]], axis=0)
            s = _dot_nt(qg, k_all)
            s = jnp.concatenate([s[:, :lc],
                                 jnp.where(prev_ok, s[:, lc:lc + tq], NEG_INF),
                                 s[:, lc + tq:lc + 2 * tq],
                                 jnp.where(next_ok, s[:, lc + 2 * tq:], NEG_INF)], axis=1)
        else:
            v_all = vc_ref[:, pair]
            s = _dot_nt(qg, kc_ref[:, hs])
        m = jnp.maximum(jnp.max(s, axis=-1, keepdims=True), sink)
        p = jnp.exp2(s - m).astype(BF16)
        v_ones = jnp.where(keep_v, v_all, jnp.ones_like(v_all))
        pv = jnp.dot(p, v_ones, preferred_element_type=F32)
        den = pltpu.roll(pv, dh, 1) + jnp.exp2(sink - m)
        o = pv / den
        for g in range(0, group, 2):
            a, bb = o[g * tq:(g + 1) * tq], o[(g + 1) * tq:(g + 2) * tq]
            if v_low:
                both = jnp.where(lane < dh, a, pltpu.roll(bb, dh, 1))
            else:
                both = jnp.where(lane < dh, pltpu.roll(a, dh, 1), bb)
            h = j * group + g
            o_ref[qrows, h * dh:(h + 2) * dh] = both.astype(o_ref.dtype)


def attention(sink, qkv, qkv_ctx, *, has_local):
    s = qkv.shape[0]
    dh = qkv.shape[1] // (ATTN_HEADS + 2 * ATTN_KV_HEADS)
    hd = ATTN_HEADS * dh
    kvd = ATTN_KV_HEADS * dh
    kcol, vcol = hd // kvd, hd // kvd + 1
    tq = WINDOW
    nb = s // tq
    nsub = min(ATTN_BLOCKS_PER_STEP, nb)
    lc = qkv_ctx.shape[0]
    in_specs = [
        pl.BlockSpec(memory_space=pltpu.SMEM),
        pl.BlockSpec((nsub * tq, hd), lambda b: (b, 0)),
    ]
    args = [sink, qkv]
    if has_local:
        for col in (kcol, vcol):
            for off in range(-1, nsub + 1):
                in_specs.append(pl.BlockSpec(
                    (tq, kvd), lambda b, col=col, off=off: (jnp.clip(nsub * b + off, 0, nb - 1), col)))
                args.append(qkv)
    in_specs += [pl.BlockSpec((lc, kvd), lambda b: (0, kcol)), pl.BlockSpec((lc, kvd), lambda b: (0, vcol))]
    args += [qkv_ctx, qkv_ctx]
    return pl.pallas_call(
        functools.partial(_attn_kernel, tq=tq, nsub=nsub, kv_heads=ATTN_KV_HEADS,
                          group=ATTN_HEADS // ATTN_KV_HEADS, dh=dh, has_local=has_local),
        grid=(nb // nsub,),
        in_specs=in_specs,
        out_specs=pl.BlockSpec((nsub * tq, hd), lambda b: (b, 0)),
        out_shape=jax.ShapeDtypeStruct((s, hd), BF16),
        compiler_params=_cparams("parallel"),
        name="attention",
    )(*args)


N_RET_IN = 9


def _ret_kernel(*refs, heads, dk, dv):
    ins = [refs[:N_RET_IN], refs[N_RET_IN:2 * N_RET_IN]]
    outs = refs[2 * N_RET_IN:2 * N_RET_IN + 4]
    state_ref = refs[-1]
    i = pl.program_id(0)

    @pl.when(i == 0)
    def _():
        for dr in range(2):
            state_ref[dr] = ins[dr][8][...]

    for dr, h in [(dr, h) for h in range(heads) for dr in range(2)]:
        q_ref, k_ref, v_ref, gate_ref, intra_ref, qdec_ref, kdec_ref, cdec_ref, _ = ins[dr]
        z_ref = outs[dr]
        q = q_ref[:, h * dk:(h + 1) * dk]
        k = k_ref[:, h * dk:(h + 1) * dk]
        v = v_ref[:, h * dv:(h + 1) * dv]
        state = state_ref[dr, h]
        sc = _dot_nt(q, k) * intra_ref[h]
        qd = (q.astype(F32) * qdec_ref[h]).astype(BF16)
        o = (jnp.dot(sc.astype(BF16), v, preferred_element_type=F32)
             + jnp.dot(qd, state.astype(BF16), preferred_element_type=F32))
        kd = (k.astype(F32) * kdec_ref[h]).astype(BF16)
        state_ref[dr, h] = state * cdec_ref[h] + lax.dot_general(
            kd, v, (((0,), (0,)), ((), ())), preferred_element_type=F32)
        mu = jnp.mean(o, axis=-1, keepdims=True)
        oc = o - mu
        var = jnp.mean(oc * oc, axis=-1, keepdims=True)
        gate = gate_ref[:, h * dv:(h + 1) * dv].astype(F32)
        z = gate * jax.nn.sigmoid(gate) * (oc * lax.rsqrt(var + EPS))
        z_ref[:, h * dv:(h + 1) * dv] = z.astype(z_ref.dtype)

    @pl.when(i == pl.num_programs(0) - 1)
    def _():
        for dr in range(2):
            outs[2 + dr][...] = state_ref[dr]


def retention_scan(proj, tables_f, tables_b, s0_f, s0_b):
    m, n = proj.shape
    d = n // 8
    heads = RET_HEADS
    dk, dv = d // heads, 2 * d // heads
    c = min(RET_CHUNK, m)
    nc = m // c
    once = lambda a: pl.BlockSpec(a.shape, lambda i: (0,) * a.ndim, pipeline_mode=pl.Buffered(1))
    in_specs, args, out_specs = [], [], []
    for reverse, tables, s0 in ((False, tables_f, s0_f), (True, tables_b, s0_b)):
        row = (lambda i: nc - 1 - i) if reverse else (lambda i: i)
        in_specs += [
            pl.BlockSpec((c, d), lambda i, row=row: (row(i), 0)),
            pl.BlockSpec((c, d), lambda i, row=row: (row(i), 1)),
            pl.BlockSpec((c, 2 * d), lambda i, row=row: (row(i), 1)),
            pl.BlockSpec((c, 2 * d), lambda i, row=row, blk=3 if reverse else 2: (row(i), blk)),
        ] + [once(a) for a in (*tables, s0)]
        args += [proj] * 4 + [*tables, s0]
        out_specs.append(pl.BlockSpec((c, 2 * d), lambda i, row=row: (row(i), 0)))
    out_specs += [pl.BlockSpec(s0_f.shape, lambda i: (0,) * s0_f.ndim)] * 2
    return pl.pallas_call(
        functools.partial(_ret_kernel, heads=heads, dk=dk, dv=dv),
        grid=(nc,),
        in_specs=in_specs,
        out_specs=out_specs,
        out_shape=[jax.ShapeDtypeStruct((m, 2 * d), BF16)] * 2 + [jax.ShapeDtypeStruct(s0_f.shape, F32)] * 2,
        scratch_shapes=[pltpu.VMEM((2,) + s0_f.shape, F32)],
        compiler_params=_cparams("arbitrary"),
        name="retention_scan",
    )(*args)


def retention_tables(log_decay_row, c, reverse):
    lg = -jnp.exp(log_decay_row.astype(F32))
    idx = jnp.arange(c, dtype=F32)
    diff = idx[:, None] - idx[None, :]
    if reverse:
        diff = -diff
    intra = jnp.where(diff >= 0, jnp.exp(lg[:, None, None] * jnp.maximum(diff, 0.0)), 0.0)
    fwd_idx = (c - 1.0 - idx) if reverse else idx
    qdec = jnp.exp(lg[:, None] * (fwd_idx + 1.0))[:, :, None]
    kdec = jnp.exp(lg[:, None] * (c - 1.0 - fwd_idx))[:, :, None]
    cdec = jnp.exp(lg * c)[:, None, None]
    return intra, qdec, kdec, cdec


def _window_sum(h_ref, cols, tmp_refs, w, tm, halo):
    rows = tm + 2 * halo
    src, src_cols, span, k = h_ref, cols, 1, 0
    while 2 * span < w:
        n = rows - 2 * span + 1
        dst = tmp_refs[k % 2]
        dst[0:n, :] = src[0:n, src_cols] + src[span:span + n, src_cols]
        src, src_cols, span, k = dst, slice(None), 2 * span, k + 1
    lo = halo - w // 2
    return src[lo:lo + tm, src_cols] + src[lo + span:lo + span + tm, src_cols]


def _pool_kernel(x_ref, xp_ref, xn_ref, g_ref, mod_ref, w_ref, ps_ref, o_ref, h_ref, tmp_a, tmp_b, *, mod_row, seq):
    tm, d = x_ref.shape
    halo = POOL_HALO
    i = pl.program_id(0)
    g0 = g_ref[0:1, :]
    shift = _mod_slice(mod_ref, mod_row, SH1, d)
    scale = _mod_slice(mod_ref, mod_row, SC1, d)
    hp = _norm_mod(xp_ref[...], g0, shift, scale)
    hn = _norm_mod(xn_ref[...], g0, shift, scale)
    h_ref[0:halo, :] = jnp.where(i > 0, hp, 0.0)
    h_ref[halo:halo + tm, :] = _norm_mod(x_ref[...], g0, shift, scale)
    h_ref[halo + tm:, :] = jnp.where(i < pl.num_programs(0) - 1, hn, 0.0)

    t = i * tm + lax.broadcasted_iota(jnp.int32, (tm, 1), 0)
    gw = d // len(POOL_WINDOWS)
    ys = []
    for g, w in enumerate(POOL_WINDOWS):
        cols = slice(g * gw, (g + 1) * gw)
        tot = _window_sum(h_ref, cols, (tmp_a, tmp_b), w, tm, halo)
        cnt = (jnp.minimum(t + w // 2, seq) - jnp.maximum(t - w // 2, 0)).astype(F32)
        dm = tot / cnt - h_ref[halo:halo + tm, cols]
        ys.append(jnp.dot(dm.astype(BF16), w_ref[g], preferred_element_type=F32))
    y = jnp.concatenate(ys, axis=1) * ps_ref[...]
    o_ref[...] = x_ref[...] + _mod_slice(mod_ref, mod_row, GA1, d) * _rms(y, g_ref[1:2, :])


def pool_layer(x, norm_g, mods, layer, w_pool, pool_scale, *, mod_row, tm):
    m, d = x.shape
    tm = min(tm, m)
    nt = m // tm
    hb = tm // POOL_HALO
    return pl.pallas_call(
        functools.partial(_pool_kernel, mod_row=mod_row, seq=m),
        grid=(nt,),
        in_specs=[
            pl.BlockSpec((tm, d), lambda i: (i, 0)),
            pl.BlockSpec((POOL_HALO, d), lambda i: (jnp.maximum(i * hb - 1, 0), 0)),
            pl.BlockSpec((POOL_HALO, d), lambda i: (jnp.minimum((i + 1) * hb, nt * hb - 1), 0)),
            pl.BlockSpec((None,) + norm_g.shape[1:], lambda i: (layer, 0, 0)),
            pl.BlockSpec((None,) + mods.shape[1:], lambda i: (layer, 0, 0)),
            pl.BlockSpec(w_pool.shape, lambda i: (0, 0, 0)),
            pl.BlockSpec((1, d), lambda i: (0, 0)),
        ],
        out_specs=pl.BlockSpec((tm, d), lambda i: (i, 0)),
        out_shape=jax.ShapeDtypeStruct((m, d), F32),
        scratch_shapes=[pltpu.VMEM((tm + 2 * POOL_HALO, d), F32)]
        + [pltpu.VMEM((tm + 2 * POOL_HALO, d // len(POOL_WINDOWS)), F32)] * 2,
        compiler_params=_cparams("parallel"),
        name="pool_layer",
    )(x, x, x, norm_g, mods, w_pool, pool_scale.reshape(1, d))


SWIGLU_SUB = 256


def _swiglu_partial(h, w1_ref, w3_ref, w2_ref, lo, hi):
    assert (hi - lo) % SWIGLU_SUB == 0
    acc = None
    for s in range(lo, hi, SWIGLU_SUB):
        cols = slice(s, s + SWIGLU_SUB)
        a = jnp.dot(h, w1_ref[:, cols], preferred_element_type=F32)
        b = jnp.dot(h, w3_ref[:, cols], preferred_element_type=F32)
        u = (a * jax.nn.sigmoid(a) * b).astype(BF16)
        p = jnp.dot(u, w2_ref[cols, :], preferred_element_type=F32)
        acc = p if acc is None else acc + p
    return acc


def _ffn_kernel(x_ref, g_ref, mod_ref, w1_ref, w3_ref, w2_ref, *rest, mod_row, n_cast, nf):
    cast_src, o_ref, cast_dst = rest[:n_cast], rest[n_cast], rest[n_cast + 1:2 * n_cast + 1]
    h_ref, acc_ref = rest[2 * n_cast + 1:]
    d = x_ref.shape[1]
    f = pl.program_id(1)
    for src, dst in zip(cast_src, cast_dst):
        dst[...] = src[...].astype(dst.dtype)

    assert nf >= 2
    n_sub = w1_ref.shape[1] // SWIGLU_SUB
    bounds = [-(-n_sub * c // nf) * SWIGLU_SUB for c in range(nf + 1)]
    for c in range(nf):
        @pl.when(f == c)
        def _(c=c):
            if c == 0:
                h = _norm_mod(x_ref[...], g_ref[2:3, :], _mod_slice(mod_ref, mod_row, SH2, d),
                              _mod_slice(mod_ref, mod_row, SC2, d)).astype(BF16)
                h_ref[...] = h
            else:
                h = h_ref[...]
            part = _swiglu_partial(h, w1_ref, w3_ref, w2_ref, bounds[c], bounds[c + 1])
            if c == 0:
                acc_ref[...] = part
            elif c < nf - 1:
                acc_ref[...] += part
            else:
                y = acc_ref[...] + part
                o_ref[...] = x_ref[...] + _mod_slice(mod_ref, mod_row, GA2, d) * _rms(y, g_ref[3:4, :])


def ffn_layer(x, norm_g, mods, layer, w1, w3, w2, widx, *, mod_row, tm, tf, cast=None):
    m, d = x.shape
    ff = w1.shape[2]
    tm = min(tm, m)
    nf = ff // tf
    steps = (m // tm) * nf
    in_specs = [
        pl.BlockSpec((tm, d), lambda i, f: (i, 0)),
        pl.BlockSpec((None,) + norm_g.shape[1:], lambda i, f: (layer, 0, 0)),
        pl.BlockSpec((None,) + mods.shape[1:], lambda i, f: (layer, 0, 0)),
        pl.BlockSpec((None, d, ff), lambda i, f: (widx, 0, 0), pipeline_mode=pl.Buffered(1)),
        pl.BlockSpec((None, d, ff), lambda i, f: (widx, 0, 0), pipeline_mode=pl.Buffered(1)),
        pl.BlockSpec((None, ff, d), lambda i, f: (widx, 0, 0), pipeline_mode=pl.Buffered(1)),
    ]
    out_specs = [pl.BlockSpec((tm, d), lambda i, f: (i, 0))]
    out_shape = [jax.ShapeDtypeStruct((m, d), F32)]
    cast_arrays, cast_idx = cast if cast is not None else ((), 0)
    for a in cast_arrays:
        _, ne, r, c = a.shape
        per_expert = steps // ne
        assert per_expert * ne == steps and r % (per_expert * 16) == 0
        rows = r // per_expert
        in_specs.append(pl.BlockSpec(
            (None, None, rows, c),
            lambda i, f, pe=per_expert: (cast_idx, (i * nf + f) // pe, (i * nf + f) % pe, 0)))
        out_specs.append(pl.BlockSpec(
            (None, rows, c), lambda i, f, pe=per_expert: ((i * nf + f) // pe, (i * nf + f) % pe, 0)))
        out_shape.append(jax.ShapeDtypeStruct((ne, r, c), BF16))
    outs = pl.pallas_call(
        functools.partial(_ffn_kernel, mod_row=mod_row, n_cast=len(cast_arrays), nf=nf),
        grid=(m // tm, nf),
        in_specs=in_specs,
        out_specs=out_specs,
        out_shape=out_shape,
        scratch_shapes=[pltpu.VMEM((tm, d), BF16), pltpu.VMEM((tm, d), F32)],
        compiler_params=_cparams("arbitrary", "arbitrary"),
        name="ffn_layer",
    )(x, norm_g, mods, w1, w3, w2, *cast_arrays)
    return outs[0] if cast is None else (outs[0], tuple(outs[1:]))


RUN_ALIGN = 16
RUN_BITS = tuple(range(9, 3, -1))
EXPERT_ROWS = 512
MOE_TOKEN_TILE = 512
TAIL_BITS = tuple(range(9, 3, -1))


def _split_dma(length, src, src_row, dst, dst_row, sem, bits, *, start, src_advances=True):
    for b in bits:
        size = 1 << b
        off = (length >> (b + 1)) << (b + 1)

        @pl.when(((length >> b) & 1) == 1)
        def _():
            s_row = pl.multiple_of(src_row + off, RUN_ALIGN) if src_advances else src_row
            cp = pltpu.make_async_copy(src.at[pl.ds(s_row, size)],
                                       dst.at[pl.ds(pl.multiple_of(dst_row + off, RUN_ALIGN), size)], sem)
            if start:
                cp.start()
            else:
                cp.wait()


def _run_dmas(tile, ne, lpad_ref, loff_ref, base_ref, buf, hbm_ref, sem, *, to_hbm, start):
    for e in range(ne):
        length = lpad_ref[tile * ne + e]
        lo = loff_ref[tile * ne + e]
        gb = base_ref[tile * ne + e]
        if to_hbm:
            _split_dma(length, buf, lo, hbm_ref, gb, sem, RUN_BITS, start=start)
        else:
            _split_dma(length, hbm_ref, gb, buf, lo, sem, RUN_BITS, start=start)


def _split_bf16(a):
    hi = a.astype(BF16)
    return hi, (a - hi.astype(F32)).astype(BF16)


def _dot_f32x3(a, b_hi, b_lo):
    a_hi, a_lo = _split_bf16(a)
    dot = functools.partial(jnp.dot, preferred_element_type=F32)
    return dot(a_hi, b_hi) + (dot(a_hi, b_lo) + dot(a_lo, b_hi))


def _local_rows(ri_ref, loff_ref, tile, ne):
    e1, e2 = ri_ref[:, 0:1], ri_ref[:, 1:2]
    lo1, lo2 = ri_ref[:, 2:3], ri_ref[:, 3:4]
    for e in range(ne):
        off = loff_ref[tile * ne + e]
        lo1 = lo1 + jnp.where(e1 == e, off, 0)
        lo2 = lo2 + jnp.where(e2 == e, off, 0)
    return lo1, lo2


def _router_kernel(x_ref, g_ref, mod_ref, wr_hi_ref, wr_lo_ref, br_ref, ri_ref, rw_ref, cnt_ref, *, mod_row):
    _route_rows(x_ref[...], g_ref, mod_ref, mod_row, wr_hi_ref, wr_lo_ref, br_ref, ri_ref, rw_ref, cnt_ref)


def _route_rows(x, g_ref, mod_ref, mod_row, wr_hi_ref, wr_lo_ref, br_ref, ri_ref, rw_ref, cnt_ref):
    tm, d = x.shape
    ne = wr_hi_ref.shape[1]
    h = _norm_mod(x, g_ref[2:3, :], _mod_slice(mod_ref, mod_row, SH2, d), _mod_slice(mod_ref, mod_row, SC2, d))
    logits = _dot_f32x3(h, wr_hi_ref[...], wr_lo_ref[...]) + br_ref[...]
    lane = lax.broadcasted_iota(jnp.int32, logits.shape, 1)
    v1 = jnp.max(logits, axis=-1, keepdims=True)
    i1 = jnp.min(jnp.where(logits == v1, lane, ne), axis=-1, keepdims=True)
    rest = jnp.where(lane == i1, -jnp.inf, logits)
    v2 = jnp.max(rest, axis=-1, keepdims=True)
    i2 = jnp.min(jnp.where(rest == v2, lane, ne), axis=-1, keepdims=True)
    e2 = jnp.exp(v2 - v1)
    den = 1.0 + e2
    oh1 = (lane == i1).astype(F32)
    oh2 = (lane == i2).astype(F32)
    both = oh1 + oh2
    r = lax.broadcasted_iota(jnp.int32, (tm, tm), 0)
    c = lax.broadcasted_iota(jnp.int32, (tm, tm), 1)
    tri = jnp.where(c < r, 1.0, 0.0).astype(BF16)
    rank_all = jnp.dot(tri, both.astype(BF16), preferred_element_type=F32)
    rank1 = jnp.sum(rank_all * oh1, axis=-1, keepdims=True).astype(jnp.int32)
    rank2 = jnp.sum(rank_all * oh2, axis=-1, keepdims=True).astype(jnp.int32)
    col = lax.broadcasted_iota(jnp.int32, ri_ref.shape, 1)
    ri_ref[...] = jnp.where(col == 0, i1, jnp.where(col == 1, i2, jnp.where(col == 2, rank1, rank2)))
    colw = lax.broadcasted_iota(jnp.int32, rw_ref.shape, 1)
    rw_ref[...] = jnp.where(colw == 0, 1.0 / den, e2 / den)
    cnt_ref[...] = jnp.sum(both, axis=0, keepdims=True).astype(jnp.int32)


def moe_router(x, norm_g, mods, layer, w_router, b_router, *, mod_row, tm):
    m, d = x.shape
    ne = w_router.shape[1]
    return pl.pallas_call(
        functools.partial(_router_kernel, mod_row=mod_row),
        grid=(m // tm,),
        in_specs=[
            pl.BlockSpec((tm, d), lambda i: (i, 0)),
            pl.BlockSpec((None,) + norm_g.shape[1:], lambda i: (layer, 0, 0)),
            pl.BlockSpec((None,) + mods.shape[1:], lambda i: (layer, 0, 0)),
            pl.BlockSpec((d, ne), lambda i: (0, 0)),
            pl.BlockSpec((d, ne), lambda i: (0, 0)),
            pl.BlockSpec((1, ne), lambda i: (0, 0)),
        ],
        out_specs=[
            pl.BlockSpec((tm, 4), lambda i: (i, 0)),
            pl.BlockSpec((tm, 2), lambda i: (i, 0)),
            pl.BlockSpec((None, 1, ne), lambda i: (i, 0, 0)),
        ],
        out_shape=[
            jax.ShapeDtypeStruct((m, 4), jnp.int32),
            jax.ShapeDtypeStruct((m, 2), F32),
            jax.ShapeDtypeStruct((m // tm, 1, ne), jnp.int32),
        ],
        compiler_params=_cparams("parallel"),
        name="moe_router",
    )(x, norm_g, mods, *_split_bf16(w_router), b_router.reshape(1, ne))


def _dispatch_kernel(wlen_ref, loff_ref, base_ref, lrow_ref, carry_ref, prev_ref, tlen_ref, tstart_ref, nu_ref,
                     x_ref, g_ref, mod_ref, ri_ref, *rest, mod_row, ne, tile0, first):
    xs_ref, cbuf, zbuf, sem = rest if first else rest[1:]
    tm, d = x_ref.shape
    rc = cbuf.shape[1]
    i = pl.program_id(0)
    last = i == pl.num_programs(0) - 1
    slot = i % 2

    def zero_tails(start):
        for e in range(ne):
            _split_dma(tlen_ref[e], zbuf, 0, xs_ref, tstart_ref[e], sem.at[2], TAIL_BITS, start=start,
                       src_advances=False)

        def unused_tile(t, carry):
            cp = pltpu.make_async_copy(
                zbuf, xs_ref.at[pl.ds(pl.multiple_of(t * EXPERT_ROWS, EXPERT_ROWS), EXPERT_ROWS)], sem.at[2])
            if start:
                cp.start()
            else:
                cp.wait()
            return carry

        lax.fori_loop(nu_ref[0], xs_ref.shape[0] // EXPERT_ROWS, unused_tile, 0)

    if first:
        @pl.when(i == 0)
        def _():
            zbuf[...] = jnp.zeros_like(zbuf)
            zero_tails(True)

    h = _norm_mod(x_ref[...], g_ref[2:3, :], _mod_slice(mod_ref, mod_row, SH2, d),
                  _mod_slice(mod_ref, mod_row, SC2, d)).astype(BF16)
    lo1, lo2 = _local_rows(ri_ref, lrow_ref, i + tile0, ne)
    lane = lax.broadcasted_iota(jnp.int32, (tm, rc), 1)
    onehot = jnp.where((lane == lo1) | (lane == lo2), 1.0, 0.0).astype(BF16)
    sorted_rows = lax.dot_general(onehot, h, (((0,), (0,)), ((), ())), preferred_element_type=F32)
    cbuf[slot] = sorted_rows.astype(BF16)
    for e in range(ne):
        @pl.when(carry_ref[(i + tile0) * ne + e] > 0)
        def _(e=e):
            here = pl.ds(pl.multiple_of(loff_ref[(i + tile0) * ne + e], RUN_ALIGN), RUN_ALIGN)
            prev = pl.ds(pl.multiple_of(prev_ref[(i + tile0) * ne + e], RUN_ALIGN), RUN_ALIGN)
            cbuf[slot, here, :] = (cbuf[slot, here, :].astype(F32)
                                   + cbuf[1 - slot, prev, :].astype(F32)).astype(BF16)

    moves = functools.partial(_run_dmas, ne=ne, lpad_ref=wlen_ref, loff_ref=loff_ref, base_ref=base_ref,
                              hbm_ref=xs_ref, to_hbm=True)
    moves(i + tile0, buf=cbuf.at[slot], sem=sem.at[slot], start=True)

    @pl.when(i > 0)
    def _():
        moves(i + tile0 - 1, buf=cbuf.at[1 - slot], sem=sem.at[1 - slot], start=False)

    @pl.when(last)
    def _():
        moves(i + tile0, buf=cbuf.at[slot], sem=sem.at[slot], start=False)
        if first:
            zero_tails(False)


def moe_dispatch(x, norm_g, mods, layer, route_i, tabs, rows, xs=None, *, mod_row, tm, ne, tile0):
    m, d = x.shape
    rc = 2 * tm + ne * 2 * RUN_ALIGN
    first = xs is None
    in_specs = [
        pl.BlockSpec((tm, d), lambda i, *_: (i, 0)),
        pl.BlockSpec((None,) + norm_g.shape[1:], lambda i, *_: (layer, 0, 0)),
        pl.BlockSpec((None,) + mods.shape[1:], lambda i, *_: (layer, 0, 0)),
        pl.BlockSpec((tm, 4), lambda i, *_: (i, 0)),
    ]
    args = [*tabs, x, norm_g, mods, route_i]
    if not first:
        in_specs.append(pl.BlockSpec(memory_space=pl.ANY))
        args.append(xs)
    return pl.pallas_call(
        functools.partial(_dispatch_kernel, mod_row=mod_row, ne=ne, tile0=tile0, first=first),
        grid_spec=pltpu.PrefetchScalarGridSpec(
            num_scalar_prefetch=len(tabs),
            grid=(m // tm,),
            in_specs=in_specs,
            out_specs=pl.BlockSpec(memory_space=pl.ANY),
            scratch_shapes=[pltpu.VMEM((2, rc, d), BF16), pltpu.VMEM((EXPERT_ROWS, d), BF16),
                            pltpu.SemaphoreType.DMA((3,))],
        ),
        out_shape=jax.ShapeDtypeStruct((rows, d), BF16),
        input_output_aliases={} if first else {len(args) - 1: 0},
        compiler_params=_cparams("arbitrary"),
        name="moe_dispatch",
    )(*args)


EXPERT_ROW_STEP = 128


def _experts_kernel(te_ref, nu_ref, tr_ref, x_ref, w1_ref, w3_ref, w2_ref, y_ref, acc_ref, *, n_chunks):
    del te_ref, nu_ref
    i = pl.program_id(0)
    f = pl.program_id(1)
    nf = pl.num_programs(1)
    tm = x_ref.shape[0]
    live_rows = tr_ref[i]
    steps = (live_rows + EXPERT_ROW_STEP - 1) // EXPERT_ROW_STEP

    for c in range(n_chunks):
        for q in range(1, tm // EXPERT_ROW_STEP + 1):
            @pl.when((f == c) & (steps == q))
            def _(c=c, rows=q * EXPERT_ROW_STEP):
                part = _swiglu_partial(x_ref[0:rows, :], w1_ref, w3_ref, w2_ref, 0, w1_ref.shape[1])
                if c == 0:
                    acc_ref[0:rows, :] = part
                elif c < n_chunks - 1:
                    acc_ref[0:rows, :] += part
                else:
                    y_ref[0:rows, :] = (acc_ref[0:rows, :] + part).astype(y_ref.dtype)
                    if rows < tm:
                        y_ref[rows:, :] = jnp.zeros((tm - rows, y_ref.shape[1]), y_ref.dtype)

    @pl.when((steps == 0) & (f == nf - 1))
    def _():
        y_ref[...] = jnp.zeros_like(y_ref)


def moe_experts(xs, tile_expert, n_used, tile_rows, w1, w3, w2, *, tm, tf):
    rows, d = xs.shape
    ff = w1.shape[2]
    nf = ff // tf
    assert nf >= 2

    def chunk(i, f, nu):
        return jnp.where(i < nu[0], f, nf - 1)

    def wmap_in(i, f, te, nu, tr):
        return (te[i], 0, chunk(i, f, nu))

    def wmap_out(i, f, te, nu, tr):
        return (te[i], chunk(i, f, nu), 0)

    def xmap(i, f, te, nu, tr):
        return (jnp.maximum(jnp.minimum(i, nu[0] - 1), 0), 0)

    return pl.pallas_call(
        functools.partial(_experts_kernel, n_chunks=nf),
        grid_spec=pltpu.PrefetchScalarGridSpec(
            num_scalar_prefetch=3,
            grid=(rows // tm, nf),
            in_specs=[
                pl.BlockSpec((tm, d), xmap),
                pl.BlockSpec((None, d, tf), wmap_in),
                pl.BlockSpec((None, d, tf), wmap_in),
                pl.BlockSpec((None, tf, d), wmap_out),
            ],
            out_specs=pl.BlockSpec((tm, d), lambda i, f, te, nu, tr: (i, 0)),
            scratch_shapes=[pltpu.VMEM((tm, d), F32)],
        ),
        out_shape=jax.ShapeDtypeStruct((rows, d), BF16),
        compiler_params=_cparams("arbitrary", "arbitrary"),
        name="moe_experts",
    )(tile_expert, n_used, tile_rows, xs, w1, w3, w2)


def _combine_kernel(lpad_ref, loff_ref, base_ref, lrow_ref, x_ref, ri_ref, rw_ref, g_ref, mod_ref, ys_ref, o_ref,
                    ybuf, sem, *, mod_row, ne, tile0):
    tm, d = x_ref.shape
    rc = ybuf.shape[1]
    i = pl.program_id(0)
    slot = i % 2
    moves = functools.partial(_run_dmas, ne=ne, lpad_ref=lpad_ref, loff_ref=loff_ref, base_ref=base_ref,
                              hbm_ref=ys_ref, to_hbm=False)

    @pl.when(i == 0)
    def _():
        ybuf[...] = jnp.zeros_like(ybuf)
        moves(tile0, buf=ybuf.at[0], sem=sem.at[0], start=True)

    @pl.when(i + 1 < pl.num_programs(0))
    def _():
        moves(i + tile0 + 1, buf=ybuf.at[1 - slot], sem=sem.at[1 - slot], start=True)

    moves(i + tile0, buf=ybuf.at[slot], sem=sem.at[slot], start=False)
    lo1, lo2 = _local_rows(ri_ref, lrow_ref, i + tile0, ne)
    lane = lax.broadcasted_iota(jnp.int32, (tm, rc), 1)
    rows = ybuf[slot]
    y1 = jnp.dot(jnp.where(lane == lo1, 1.0, 0.0).astype(BF16), rows, preferred_element_type=F32)
    y2 = jnp.dot(jnp.where(lane == lo2, 1.0, 0.0).astype(BF16), rows, preferred_element_type=F32)
    y = rw_ref[:, 0:1] * y1 + rw_ref[:, 1:2] * y2
    o_ref[...] = x_ref[...] + _mod_slice(mod_ref, mod_row, GA2, d) * _rms(y, g_ref[3:4, :])


def moe_combine(x, norm_g, mods, layer, route_i, route_w, ys, tabs, *, mod_row, tm, ne, tile0):
    m, d = x.shape
    rc = 2 * tm + ne * 2 * RUN_ALIGN
    return pl.pallas_call(
        functools.partial(_combine_kernel, mod_row=mod_row, ne=ne, tile0=tile0),
        grid_spec=pltpu.PrefetchScalarGridSpec(
            num_scalar_prefetch=len(tabs),
            grid=(m // tm,),
            in_specs=[
                pl.BlockSpec((tm, d), lambda i, *_: (i, 0)),
                pl.BlockSpec((tm, 4), lambda i, *_: (i, 0)),
                pl.BlockSpec((tm, 2), lambda i, *_: (i, 0)),
                pl.BlockSpec((None,) + norm_g.shape[1:], lambda i, *_: (layer, 0, 0)),
                pl.BlockSpec((None,) + mods.shape[1:], lambda i, *_: (layer, 0, 0)),
                pl.BlockSpec(memory_space=pl.ANY),
            ],
            out_specs=pl.BlockSpec((tm, d), lambda i, *_: (i, 0)),
            scratch_shapes=[pltpu.VMEM((2, rc, d), BF16), pltpu.SemaphoreType.DMA((2,))],
        ),
        out_shape=jax.ShapeDtypeStruct((m, d), F32),
        compiler_params=_cparams("arbitrary"),
        name="moe_combine",
    )(*tabs, x, route_i, route_w, norm_g, mods, ys)


def sparse_moe_layer(streams, norm_g, mods, layer, w_router, b_router, w1, w3, w2, *, tf, routed_first=None):
    ne = w_router.shape[1]
    tm_exp = EXPERT_ROWS
    routed = [moe_router(x, norm_g, mods, layer, w_router, b_router, mod_row=row, tm=tm)
              if k > 0 or routed_first is None else routed_first for k, (x, row, tm) in enumerate(streams)]
    align = lambda a: (a + RUN_ALIGN - 1) // RUN_ALIGN * RUN_ALIGN
    per_stream = [r[2][:, 0, :] for r in routed]
    counts = jnp.concatenate(per_stream, axis=0)
    m_total = sum(x.shape[0] for x, _, _ in streams)
    stream_rows = [align(jnp.sum(c, axis=0)) for c in per_stream]
    stream_off = jnp.cumsum(jnp.stack(stream_rows), axis=0) - jnp.stack(stream_rows)
    pos = jnp.concatenate([stream_off[k][None, :] + jnp.cumsum(c, axis=0) - c for k, c in enumerate(per_stream)])
    carry = pos % RUN_ALIGN
    lpad = align(carry + counts)
    stream_end = jnp.concatenate([jnp.arange(c.shape[0]) == c.shape[0] - 1 for c in per_stream])[:, None]
    wlen = jnp.where(stream_end, lpad, (carry + counts) // RUN_ALIGN * RUN_ALIGN)
    loff = jnp.cumsum(lpad, axis=1) - lpad
    prev_block = jnp.concatenate([jnp.zeros((1, ne), loff.dtype), (loff + lpad - RUN_ALIGN)[:-1]], axis=0)
    group = sum(stream_rows)
    gpad = (group + tm_exp - 1) // tm_exp * tm_exp
    ends = jnp.cumsum(gpad)
    starts = ends - gpad
    base = starts[None, :] + pos - carry
    n_tiles = -(-(2 * m_total + len(streams) * ne * (RUN_ALIGN - 1)) // tm_exp) + ne
    tile_start = jnp.arange(n_tiles, dtype=jnp.int32) * tm_exp
    n_used = (ends[-1] // tm_exp).astype(jnp.int32).reshape(1)
    tile_expert = jnp.sum(tile_start[:, None] >= ends[None, :], axis=1).astype(jnp.int32)
    tile_expert = jnp.minimum(tile_expert, tile_expert[jnp.maximum(n_used[0] - 1, 0)])
    tile_rows = jnp.clip((starts + group)[tile_expert] - tile_start, 0, tm_exp).astype(jnp.int32)
    group_first = stream_rows[0]
    later_rows = m_total - streams[0][0].shape[0] + (len(streams) - 1) * (RUN_ALIGN - 1)
    assert later_rows + tm_exp - RUN_ALIGN < 2 << TAIL_BITS[0]
    flat = lambda *arrays: [a.reshape(-1).astype(jnp.int32) for a in arrays]
    tabs = flat(lpad, loff, base, loff + carry)
    write_tabs = flat(wlen, loff, base, loff + carry, carry, prev_block,
                      gpad - group_first, starts + group_first, n_used)
    xs, tile0 = None, 0
    for (x, row, tm), (route_i, _, _) in zip(streams, routed):
        xs = moe_dispatch(x, norm_g, mods, layer, route_i, write_tabs, n_tiles * tm_exp, xs,
                          mod_row=row, tm=tm, ne=ne, tile0=tile0)
        tile0 += x.shape[0] // tm
    ys = moe_experts(xs, tile_expert, n_used, tile_rows, w1, w3, w2, tm=tm_exp, tf=tf)
    outs, tile0 = [], 0
    for (x, row, tm), (route_i, route_w, _) in zip(streams, routed):
        outs.append(moe_combine(x, norm_g, mods, layer, route_i, route_w, ys, tabs,
                                mod_row=row, tm=tm, ne=ne, tile0=tile0))
        tile0 += x.shape[0] // tm
    return outs


def _rope_tables(seq):
    n = ROPE_PAIRS
    inv = ROPE_BASE ** (-np.arange(n, dtype=np.float64) / n)
    t = np.arange(seq)
    row_ang = (t // GRID_W)[:, None] * inv[None, :]
    col_ang = (t % GRID_W)[:, None] * inv[None, :]
    cos = np.concatenate([np.cos(row_ang)] * 2 + [np.cos(col_ang)] * 2, axis=1)
    sin = np.concatenate([-np.sin(row_ang), np.sin(row_ang), -np.sin(col_ang), np.sin(col_ang)], axis=1)
    return (jnp.asarray(np.tile(cos, (1, 2)), dtype=F32), jnp.asarray(np.tile(sin, (1, 2)), dtype=F32))


def kernel(x, c, ctx, c_ctx, w_ada, b_ada, norm_g, attn_w_qkv, attn_w_o, attn_sink, ret_w_in, ret_w_o,
           ret_log_decay, pool_w, pool_scale, ffn_w1, ffn_w3, ffn_w2, moe_w_router, moe_b_router,
           moe_w1, moe_w3, moe_w2):
    batch, seq, d = x.shape
    assert batch == 1 and c.shape[0] == 1
    depth = w_ada.shape[0]
    lc = ctx.shape[1]
    xl = x.reshape(seq, d)
    xc = ctx.reshape(lc, d)

    cvecs = jnp.zeros((8, d), F32).at[LAT_ROW].set(c[0]).at[CTX_ROW].set(c_ctx)
    mods = ada_table(cvecs, w_ada, b_ada)
    rope = _rope_tables(seq)

    hd = attn_w_o.shape[1]
    dh = hd // ATTN_HEADS
    qkv_scale = jnp.concatenate([jnp.full((hd,), dh ** -0.5, F32),
                                 jnp.ones((attn_w_qkv.shape[2] - hd,), F32)])
    dk = d // RET_HEADS
    ret_scale = jnp.concatenate([jnp.ones((d,), F32), jnp.full((d,), dk ** -0.5, F32),
                                 jnp.ones((ret_w_in.shape[2] - 2 * d,), F32)])

    moe_bf16 = None
    for i in range(depth):
        last = i == depth - 1
        kind, j = i % N_MIXERS, i // N_MIXERS
        proj = functools.partial(mixer_projection, norm_g=norm_g, mods=mods, layer=i)
        routed_lat = None
        if i % 2 == 1:
            mix_out = functools.partial(outproj_residual, norm_g=norm_g, mods=mods, layer=i, g_row=1,
                                        mod_row=LAT_ROW, slot=GA1, tm=min(MOE_TOKEN_TILE, seq),
                                        router=(moe_w_router[i // 2], moe_b_router[i // 2]))
        else:
            mix_out = lambda x, **kw: (outproj_residual(x, norm_g, mods, i, g_row=1, mod_row=LAT_ROW,
                                                        slot=GA1, tm=1024, **kw), None)
        if kind == 0:
            w_qkv = (attn_w_qkv[j] * qkv_scale).astype(BF16)
            w_o = attn_w_o[j].astype(BF16)
            kvd = (w_qkv.shape[1] - hd) // 2
            qkv_l = proj(xl, w=w_qkv, mod_row=LAT_ROW, tm=512, chunk=256, q_cols=hd, rope=rope, rope_cols=hd + kvd)
            qkv_c = proj(xc, w=w_qkv, mod_row=CTX_ROW, tm=256, chunk=256, q_cols=hd)
            o_l = attention(attn_sink[j], qkv_l, qkv_c, has_local=True)
            xl, routed_lat = mix_out(xl, w=w_o, ys=[o_l])
            if not last:
                o_c = attention(attn_sink[j], qkv_c, qkv_c, has_local=False)
                xc = outproj_residual(xc, norm_g, mods, i, w_o, [o_c], g_row=1, mod_row=CTX_ROW, slot=GA1, tm=256)
        elif kind == 1:
            w_in = (ret_w_in[j] * ret_scale).astype(BF16)
            w_o = ret_w_o[j].astype(BF16)
            p_c = proj(xc, w=w_in, mod_row=CTX_ROW, tm=256, chunk=1024)
            p_l = proj(xl, w=w_in, mod_row=LAT_ROW, tm=512, chunk=1024)
            s0 = jnp.zeros((RET_HEADS, dk, 2 * dk), F32)
            tabs_f = retention_tables(ret_log_decay[j, 0], min(RET_CHUNK, lc), False)
            tabs_b = retention_tables(ret_log_decay[j, 1], min(RET_CHUNK, lc), True)
            zf_c, zb_c, s_f, s_b = retention_scan(p_c, tabs_f, tabs_b, s0, s0)
            zf_l, zb_l, _, _ = retention_scan(p_l, tabs_f, tabs_b, s_f, s_b)
            xl, routed_lat = mix_out(xl, w=w_o, ys=[zf_l, zb_l])
            if not last:
                xc = outproj_residual(xc, norm_g, mods, i, w_o, [zf_c, zb_c], g_row=1, mod_row=CTX_ROW, slot=GA1, tm=256)
        else:
            w_p = pool_w[j].astype(BF16)
            xl = pool_layer(xl, norm_g, mods, i, w_p, pool_scale[j], mod_row=LAT_ROW, tm=1024)
            if not last:
                xc = pool_layer(xc, norm_g, mods, i, w_p, pool_scale[j], mod_row=CTX_ROW, tm=256)

        f = i // 2
        if i % 2 == 0:
            w1, w3, w2 = ffn_w1.astype(BF16), ffn_w3.astype(BF16), ffn_w2.astype(BF16)
            tf = w1.shape[2] // 2
            steps = (seq // min(512, seq)) * 2
            ne = moe_w1.shape[1]
            if not last and steps % ne == 0 and d % (steps // ne * 16) == 0:
                xl, moe_bf16 = ffn_layer(xl, norm_g, mods, i, w1, w3, w2, f, mod_row=LAT_ROW, tm=512, tf=tf,
                                         cast=((moe_w1, moe_w3, moe_w2), (i + 1) // 2))
            else:
                xl = ffn_layer(xl, norm_g, mods, i, w1, w3, w2, f, mod_row=LAT_ROW, tm=512, tf=tf)
            if not last:
                xc = ffn_layer(xc, norm_g, mods, i, w1, w3, w2, f, mod_row=CTX_ROW, tm=256, tf=tf)
        else:
            if moe_bf16 is None:
                moe_bf16 = moe_w1[f].astype(BF16), moe_w3[f].astype(BF16), moe_w2[f].astype(BF16)
            (w1, w3, w2), moe_bf16 = moe_bf16, None
            tf = w1.shape[2] // 2
            streams = [(xl, LAT_ROW, min(MOE_TOKEN_TILE, seq))] + ([] if last else [(xc, CTX_ROW, min(256, lc))])
            outs = sparse_moe_layer(streams, norm_g, mods, i, moe_w_router[f], moe_b_router[f], w1, w3, w2, tf=tf,
                                    routed_first=routed_lat)
            xl = outs[0]
            if not last:
                xc = outs[1]
    return xl.reshape(batch, seq, d)
```

```python
import functools

import jax
import jax.numpy as jnp
import numpy as np
from jax import lax
from jax.experimental import pallas as pl
from jax.experimental.pallas import tpu as pltpu

F32 = jnp.float32
BF16 = jnp.bfloat16

EPS = 1e-6
NEG_INF = -1e30
LOG2E = 1.4426950408889634
LANES = 128
VMEM_LIMIT = 56 * 1024 * 1024

GRID_W = 64
N_MIXERS = 3
ATTN_HEADS = 16
ATTN_KV_HEADS = 4
ATTN_BLOCKS_PER_STEP = 8
ROPE_PAIRS = 16
WINDOW = 128
ROPE_BASE = 10000.0
RET_HEADS = 4
RET_CHUNK = 256
POOL_WINDOWS = (2, 4, 8, 16)
POOL_HALO = 8

SH1, SC1, GA1, SH2, SC2, GA2 = range(6)
LAT_ROW, CTX_ROW = 0, 1


def _cparams(*sem):
    return pltpu.CompilerParams(dimension_semantics=sem, vmem_limit_bytes=VMEM_LIMIT)


def _rms(x, g):
    return x * lax.rsqrt(jnp.mean(x * x, axis=-1, keepdims=True) + EPS) * g


def _mod_slice(mod_ref, row, slot, d):
    return mod_ref[row:row + 1, slot * d:(slot + 1) * d]


def _norm_mod(x, g, shift, scale):
    return _rms(x, g) * (1.0 + scale) + shift


def _ada_kernel(c_ref, w_ref, b_ref, o_ref):
    c = c_ref[...]
    s = c * jax.nn.sigmoid(c)
    o_ref[...] = _dot_f32x3(s, *_split_bf16(w_ref[...])) + b_ref[...]


def ada_table(cvecs, w_ada, b_ada):
    depth, d, n = w_ada.shape
    tn = 1536
    return pl.pallas_call(
        _ada_kernel,
        grid=(depth, n // tn),
        in_specs=[
            pl.BlockSpec((8, d), lambda i, j: (0, 0)),
            pl.BlockSpec((None, d, tn), lambda i, j: (i, 0, j)),
            pl.BlockSpec((None, 1, tn), lambda i, j: (i, 0, j)),
        ],
        out_specs=pl.BlockSpec((None, 8, tn), lambda i, j: (i, 0, j)),
        out_shape=jax.ShapeDtypeStruct((depth, 8, n), F32),
        compiler_params=_cparams("parallel", "parallel"),
        name="ada_table",
    )(cvecs, w_ada, b_ada.reshape(depth, 1, n))


def _rope(a, cos, sin):
    lane = lax.broadcasted_iota(jnp.int32, (a.shape[0], LANES), 1)
    first = (lane % (2 * ROPE_PAIRS)) < ROPE_PAIRS
    outs = []
    for cb in range(a.shape[1] // LANES):
        blk = a[:, cb * LANES:(cb + 1) * LANES]
        partner = jnp.where(first, pltpu.roll(blk, LANES - ROPE_PAIRS, 1), pltpu.roll(blk, ROPE_PAIRS, 1))
        outs.append(blk * cos + partner * sin)
    return jnp.concatenate(outs, axis=1)


def _mixer_proj_kernel(x_ref, g_ref, mod_ref, w_ref, *rest, mod_row, chunk, rope_cols, q_cols):
    if rope_cols:
        cos_ref, sin_ref, o_ref = rest
        cos, sin = cos_ref[...], sin_ref[...]
    else:
        (o_ref,) = rest
    d = x_ref.shape[1]
    h = _norm_mod(x_ref[...], g_ref[0:1, :], _mod_slice(mod_ref, mod_row, SH1, d),
                  _mod_slice(mod_ref, mod_row, SC1, d)).astype(BF16)
    for c0 in range(0, o_ref.shape[1], chunk):
        acc = jnp.dot(h, w_ref[:, c0:c0 + chunk], preferred_element_type=F32)
        if c0 < q_cols:
            acc = acc * LOG2E
        if c0 < rope_cols:
            acc = _rope(acc, cos, sin)
        o_ref[:, c0:c0 + chunk] = acc.astype(o_ref.dtype)


def mixer_projection(x, norm_g, mods, layer, w, *, mod_row, tm, chunk, q_cols=0, rope=None, rope_cols=0):
    m, d = x.shape
    n = w.shape[1]
    tm = min(tm, m)
    assert n % chunk == 0 and rope_cols % chunk == 0 and q_cols % chunk == 0
    in_specs = [
        pl.BlockSpec((tm, d), lambda i: (i, 0)),
        pl.BlockSpec((None,) + norm_g.shape[1:], lambda i: (layer, 0, 0)),
        pl.BlockSpec((None,) + mods.shape[1:], lambda i: (layer, 0, 0)),
        pl.BlockSpec((d, n), lambda i: (0, 0), pipeline_mode=pl.Buffered(1)),
    ]
    args = [x, norm_g, mods, w]
    if rope_cols:
        in_specs += [pl.BlockSpec((tm, LANES), lambda i: (i, 0))] * 2
        args += list(rope)
    return pl.pallas_call(
        functools.partial(_mixer_proj_kernel, mod_row=mod_row, chunk=chunk, rope_cols=rope_cols, q_cols=q_cols),
        grid=(m // tm,),
        in_specs=in_specs,
        out_specs=pl.BlockSpec((tm, n), lambda i: (i, 0)),
        out_shape=jax.ShapeDtypeStruct((m, n), BF16),
        compiler_params=_cparams("parallel"),
        name="mixer_projection",
    )(*args)


def _outproj_kernel(x_ref, g_ref, mod_ref, w_ref, *rest, n_y, g_row, mod_row, slot, route):
    y_refs, rest = rest[:n_y], rest[n_y:]
    if route:
        router_refs, o_ref, route_out = rest[:3], rest[3], rest[4:]
    else:
        (o_ref,) = rest
    d = x_ref.shape[1]
    y = y_refs[0][...]
    if n_y == 2:
        y = (y.astype(F32) + y_refs[1][...].astype(F32)).astype(BF16)
    t = jnp.dot(y, w_ref[...], preferred_element_type=F32)
    gate = _mod_slice(mod_ref, mod_row, slot, d)
    x_new = x_ref[...] + gate * _rms(t, g_ref[g_row:g_row + 1, :])
    o_ref[...] = x_new
    if route:
        _route_rows(x_new, g_ref, mod_ref, mod_row, *router_refs, *route_out)


def _router_specs(m, d, ne, tm):
    ins = [pl.BlockSpec((d, ne), lambda i: (0, 0))] * 2 + [pl.BlockSpec((1, ne), lambda i: (0, 0))]
    outs = [pl.BlockSpec((tm, 4), lambda i: (i, 0)), pl.BlockSpec((tm, 2), lambda i: (i, 0)),
            pl.BlockSpec((None, 1, ne), lambda i: (i, 0, 0))]
    shapes = [jax.ShapeDtypeStruct((m, 4), jnp.int32), jax.ShapeDtypeStruct((m, 2), F32),
              jax.ShapeDtypeStruct((m // tm, 1, ne), jnp.int32)]
    return ins, outs, shapes


def outproj_residual(x, norm_g, mods, layer, w, ys, *, g_row, mod_row, slot, tm, router=None):
    m, d = x.shape
    k = w.shape[0]
    tm = min(tm, m)
    in_specs = [
        pl.BlockSpec((tm, d), lambda i: (i, 0)),
        pl.BlockSpec((None,) + norm_g.shape[1:], lambda i: (layer, 0, 0)),
        pl.BlockSpec((None,) + mods.shape[1:], lambda i: (layer, 0, 0)),
        pl.BlockSpec((k, d), lambda i: (0, 0)),
    ] + [pl.BlockSpec((tm, k), lambda i: (i, 0))] * len(ys)
    args = [x, norm_g, mods, w, *ys]
    out_specs = [pl.BlockSpec((tm, d), lambda i: (i, 0))]
    out_shape = [jax.ShapeDtypeStruct((m, d), F32)]
    if router is not None:
        w_router, b_router = router
        ne = w_router.shape[1]
        r_in, r_out, r_shapes = _router_specs(m, d, ne, tm)
        in_specs += r_in
        args += [*_split_bf16(w_router), b_router.reshape(1, ne)]
        out_specs += r_out
        out_shape += r_shapes
    outs = pl.pallas_call(
        functools.partial(_outproj_kernel, n_y=len(ys), g_row=g_row, mod_row=mod_row, slot=slot,
                          route=router is not None),
        grid=(m // tm,),
        in_specs=in_specs,
        out_specs=out_specs,
        out_shape=out_shape,
        compiler_params=_cparams("parallel"),
        name="outproj_residual",
    )(*args)
    return outs[0] if router is None else (outs[0], tuple(outs[1:]))


def _dot_nt(a, b):
    return lax.dot_general(a, b, (((1,), (1,)), ((), ())), preferred_element_type=F32)


def _attn_kernel(sink_ref, q_ref, *rest, tq, nsub, kv_heads, group, dh, has_local):
    if has_local:
        k_refs, v_refs = rest[:nsub + 2], rest[nsub + 2:2 * nsub + 4]
    kc_ref, vc_ref, o_ref = rest[-3:]
    assert 2 * dh == LANES and tq == WINDOW and group % 2 == 0
    b = pl.program_id(0)
    nb = pl.num_programs(0)
    rows = group * tq
    lc = kc_ref.shape[0]
    head_of_row = lax.broadcasted_iota(jnp.int32, (rows, 1), 0) // tq
    lane = lax.broadcasted_iota(jnp.int32, (1, LANES), 1)
    if has_local:
        r = lax.broadcasted_iota(jnp.int32, (rows, tq), 0) % tq
        c = lax.broadcasted_iota(jnp.int32, (rows, tq), 1)
    for sub, j in [(sub, j) for sub in range(nsub) for j in range(kv_heads)]:
        qrows = slice(sub * tq, (sub + 1) * tq)
        hs = slice(j * dh, (j + 1) * dh)
        pair = slice((j // 2) * LANES, (j // 2 + 1) * LANES)
        v_low = j % 2 == 0
        keep_v = (lane < dh) if v_low else (lane >= dh)
        qg = jnp.concatenate(
            [q_ref[qrows, (j * group + g) * dh:(j * group + g + 1) * dh] for g in range(group)], axis=0)
        sink = jnp.full((rows, 1), sink_ref[j * group] * LOG2E, F32)
        for g in range(1, group):
            sink = jnp.where(head_of_row == g, sink_ref[j * group + g] * LOG2E, sink)
        if has_local:
            kp_ref, kq_ref, kn_ref = k_refs[sub:sub + 3]
            vp_ref, vq_ref, vn_ref = v_refs[sub:sub + 3]
            prev_ok = (c >= r) if sub > 0 else (c >= r) & (b > 0)
            next_ok = (c <= r) if sub < nsub - 1 else (c <= r) & (b < nb - 1)
            k_all = jnp.concatenate([kc_ref[:, hs], kp_ref[:, hs], kq_ref[:, hs], kn_ref[:, hs]], axis=0)
            v_all = jnp.concatenate([vc_ref[:, pair], vp_ref[:, pair], vq_ref[:, pair], vn_ref[:, pair]], axis=0)
            s = _dot_nt(qg, k_all)
            s = jnp.concatenate([s[:, :lc],
                                 jnp.where(prev_ok, s[:, lc:lc + tq], NEG_INF),
                                 s[:, lc + tq:lc + 2 * tq],
                                 jnp.where(next_ok, s[:, lc + 2 * tq:], NEG_INF)], axis=1)
        else:
            v_all = vc_ref[:, pair]
            s = _dot_nt(qg, kc_ref[:, hs])
        m = jnp.maximum(jnp.max(s, axis=-1, keepdims=True), sink)
        p = jnp.exp2(s - m).astype(BF16)
        v_ones = jnp.where(keep_v, v_all, jnp.ones_like(v_all))
        pv = jnp.dot(p, v_ones, preferred_element_type=F32)
        den = pltpu.roll(pv, dh, 1) + jnp.exp2(sink - m)
        o = pv / den
        for g in range(0, group, 2):
            a, bb = o[g * tq:(g + 1) * tq], o[(g + 1) * tq:(g + 2) * tq]
            if v_low:
                both = jnp.where(lane < dh, a, pltpu.roll(bb, dh, 1))
            else:
                both = jnp.where(lane < dh, pltpu.roll(a, dh, 1), bb)
            h = j * group + g
            o_ref[qrows, h * dh:(h + 2) * dh] = both.astype(o_ref.dtype)


def attention(sink, qkv, qkv_ctx, *, has_local):
    s = qkv.shape[0]
    dh = qkv.shape[1] // (ATTN_HEADS + 2 * ATTN_KV_HEADS)
    hd = ATTN_HEADS * dh
    kvd = ATTN_KV_HEADS * dh
    kcol, vcol = hd // kvd, hd // kvd + 1
    tq = WINDOW
    nb = s // tq
    nsub = min(ATTN_BLOCKS_PER_STEP, nb)
    lc = qkv_ctx.shape[0]
    in_specs = [
        pl.BlockSpec(memory_space=pltpu.SMEM),
        pl.BlockSpec((nsub * tq, hd), lambda b: (b, 0)),
    ]
    args = [sink, qkv]
    if has_local:
        for col in (kcol, vcol):
            for off in range(-1, nsub + 1):
                in_specs.append(pl.BlockSpec(
                    (tq, kvd), lambda b, col=col, off=off: (jnp.clip(nsub * b + off, 0, nb - 1), col)))
                args.append(qkv)
    in_specs += [pl.BlockSpec((lc, kvd), lambda b: (0, kcol)), pl.BlockSpec((lc, kvd), lambda b: (0, vcol))]
    args += [qkv_ctx, qkv_ctx]
    return pl.pallas_call(
        functools.partial(_attn_kernel, tq=tq, nsub=nsub, kv_heads=ATTN_KV_HEADS,
                          group=ATTN_HEADS // ATTN_KV_HEADS, dh=dh, has_local=has_local),
        grid=(nb // nsub,),
        in_specs=in_specs,
        out_specs=pl.BlockSpec((nsub * tq, hd), lambda b: (b, 0)),
        out_shape=jax.ShapeDtypeStruct((s, hd), BF16),
        compiler_params=_cparams("parallel"),
        name="attention",
    )(*args)


N_RET_IN = 9


def _ret_kernel(*refs, heads, dk, dv):
    ins = [refs[:N_RET_IN], refs[N_RET_IN:2 * N_RET_IN]]
    outs = refs[2 * N_RET_IN:2 * N_RET_IN + 4]
    state_ref = refs[-1]
    i = pl.program_id(0)

    @pl.when(i == 0)
    def _():
        for dr in range(2):
            state_ref[dr] = ins[dr][8][...]

    for dr, h in [(dr, h) for h in range(heads) for dr in range(2)]:
        q_ref, k_ref, v_ref, gate_ref, intra_ref, qdec_ref, kdec_ref, cdec_ref, _ = ins[dr]
        z_ref = outs[dr]
        q = q_ref[:, h * dk:(h + 1) * dk]
        k = k_ref[:, h * dk:(h + 1) * dk]
        v = v_ref[:, h * dv:(h + 1) * dv]
        state = state_ref[dr, h]
        sc = _dot_nt(q, k) * intra_ref[h]
        qd = (q.astype(F32) * qdec_ref[h]).astype(BF16)
        o = (jnp.dot(sc.astype(BF16), v, preferred_element_type=F32)
             + jnp.dot(qd, state.astype(BF16), preferred_element_type=F32))
        kd = (k.astype(F32) * kdec_ref[h]).astype(BF16)
        state_ref[dr, h] = state * cdec_ref[h] + lax.dot_general(
            kd, v, (((0,), (0,)), ((), ())), preferred_element_type=F32)
        mu = jnp.mean(o, axis=-1, keepdims=True)
        oc = o - mu
        var = jnp.mean(oc * oc, axis=-1, keepdims=True)
        gate = gate_ref[:, h * dv:(h + 1) * dv].astype(F32)
        z = gate * jax.nn.sigmoid(gate) * (oc * lax.rsqrt(var + EPS))
        z_ref[:, h * dv:(h + 1) * dv] = z.astype(z_ref.dtype)

    @pl.when(i == pl.num_programs(0) - 1)
    def _():
        for dr in range(2):
            outs[2 + dr][...] = state_ref[dr]


def retention_scan(proj, tables_f, tables_b, s0_f, s0_b):
    m, n = proj.shape
    d = n // 8
    heads = RET_HEADS
    dk, dv = d // heads, 2 * d // heads
    c = min(RET_CHUNK, m)
    nc = m // c
    once = lambda a: pl.BlockSpec(a.shape, lambda i: (0,) * a.ndim, pipeline_mode=pl.Buffered(1))
    in_specs, args, out_specs = [], [], []
    for reverse, tables, s0 in ((False, tables_f, s0_f), (True, tables_b, s0_b)):
        row = (lambda i: nc - 1 - i) if reverse else (lambda i: i)
        in_specs += [
            pl.BlockSpec((c, d), lambda i, row=row: (row(i), 0)),
            pl.BlockSpec((c, d), lambda i, row=row: (row(i), 1)),
            pl.BlockSpec((c, 2 * d), lambda i, row=row: (row(i), 1)),
            pl.BlockSpec((c, 2 * d), lambda i, row=row, blk=3 if reverse else 2: (row(i), blk)),
        ] + [once(a) for a in (*tables, s0)]
        args += [proj] * 4 + [*tables, s0]
        out_specs.append(pl.BlockSpec((c, 2 * d), lambda i, row=row: (row(i), 0)))
    out_specs += [pl.BlockSpec(s0_f.shape, lambda i: (0,) * s0_f.ndim)] * 2
    return pl.pallas_call(
        functools.partial(_ret_kernel, heads=heads, dk=dk, dv=dv),
        grid=(nc,),
        in_specs=in_specs,
        out_specs=out_specs,
        out_shape=[jax.ShapeDtypeStruct((m, 2 * d), BF16)] * 2 + [jax.ShapeDtypeStruct(s0_f.shape, F32)] * 2,
        scratch_shapes=[pltpu.VMEM((2,) + s0_f.shape, F32)],
        compiler_params=_cparams("arbitrary"),
        name="retention_scan",
    )(*args)


def retention_tables(log_decay_row, c, reverse):
    lg = -jnp.exp(log_decay_row.astype(F32))
    idx = jnp.arange(c, dtype=F32)
    diff = idx[:, None] - idx[None, :]
    if reverse:
        diff = -diff
    intra = jnp.where(diff >= 0, jnp.exp(lg[:, None, None] * jnp.maximum(diff, 0.0)), 0.0)
    fwd_idx = (c - 1.0 - idx) if reverse else idx
    qdec = jnp.exp(lg[:, None] * (fwd_idx + 1.0))[:, :, None]
    kdec = jnp.exp(lg[:, None] * (c - 1.0 - fwd_idx))[:, :, None]
    cdec = jnp.exp(lg * c)[:, None, None]
    return intra, qdec, kdec, cdec


def _window_sum(h_ref, cols, tmp_refs, w, tm, halo):
    rows = tm + 2 * halo
    src, src_cols, span, k = h_ref, cols, 1, 0
    while 2 * span < w:
        n = rows - 2 * span + 1
        dst = tmp_refs[k % 2]
        dst[0:n, :] = src[0:n, src_cols] + src[span:span + n, src_cols]
        src, src_cols, span, k = dst, slice(None), 2 * span, k + 1
    lo = halo - w // 2
    return src[lo:lo + tm, src_cols] + src[lo + span:lo + span + tm, src_cols]


def _pool_kernel(x_ref, xp_ref, xn_ref, g_ref, mod_ref, w_ref, ps_ref, o_ref, h_ref, tmp_a, tmp_b, *, mod_row, seq):
    tm, d = x_ref.shape
    halo = POOL_HALO
    i = pl.program_id(0)
    g0 = g_ref[0:1, :]
    shift = _mod_slice(mod_ref, mod_row, SH1, d)
    scale = _mod_slice(mod_ref, mod_row, SC1, d)
    hp = _norm_mod(xp_ref[...], g0, shift, scale)
    hn = _norm_mod(xn_ref[...], g0, shift, scale)
    h_ref[0:halo, :] = jnp.where(i > 0, hp, 0.0)
    h_ref[halo:halo + tm, :] = _norm_mod(x_ref[...], g0, shift, scale)
    h_ref[halo + tm:, :] = jnp.where(i < pl.num_programs(0) - 1, hn, 0.0)

    t = i * tm + lax.broadcasted_iota(jnp.int32, (tm, 1), 0)
    gw = d // len(POOL_WINDOWS)
    ys = []
    for g, w in enumerate(POOL_WINDOWS):
        cols = slice(g * gw, (g + 1) * gw)
        tot = _window_sum(h_ref, cols, (tmp_a, tmp_b), w, tm, halo)
        cnt = (jnp.minimum(t + w // 2, seq) - jnp.maximum(t - w // 2, 0)).astype(F32)
        dm = tot / cnt - h_ref[halo:halo + tm, cols]
        ys.append(jnp.dot(dm.astype(BF16), w_ref[g], preferred_element_type=F32))
    y = jnp.concatenate(ys, axis=1) * ps_ref[...]
    o_ref[...] = x_ref[...] + _mod_slice(mod_ref, mod_row, GA1, d) * _rms(y, g_ref[1:2, :])


def pool_layer(x, norm_g, mods, layer, w_pool, pool_scale, *, mod_row, tm):
    m, d = x.shape
    tm = min(tm, m)
    nt = m // tm
    hb = tm // POOL_HALO
    return pl.pallas_call(
        functools.partial(_pool_kernel, mod_row=mod_row, seq=m),
        grid=(nt,),
        in_specs=[
            pl.BlockSpec((tm, d), lambda i: (i, 0)),
            pl.BlockSpec((POOL_HALO, d), lambda i: (jnp.maximum(i * hb - 1, 0), 0)),
            pl.BlockSpec((POOL_HALO, d), lambda i: (jnp.minimum((i + 1) * hb, nt * hb - 1), 0)),
            pl.BlockSpec((None,) + norm_g.shape[1:], lambda i: (layer, 0, 0)),
            pl.BlockSpec((None,) + mods.shape[1:], lambda i: (layer, 0, 0)),
            pl.BlockSpec(w_pool.shape, lambda i: (0, 0, 0)),
            pl.BlockSpec((1, d), lambda i: (0, 0)),
        ],
        out_specs=pl.BlockSpec((tm, d), lambda i: (i, 0)),
        out_shape=jax.ShapeDtypeStruct((m, d), F32),
        scratch_shapes=[pltpu.VMEM((tm + 2 * POOL_HALO, d), F32)]
        + [pltpu.VMEM((tm + 2 * POOL_HALO, d // len(POOL_WINDOWS)), F32)] * 2,
        compiler_params=_cparams("parallel"),
        name="pool_layer",
    )(x, x, x, norm_g, mods, w_pool, pool_scale.reshape(1, d))


SWIGLU_SUB = 256


def _swiglu_partial(h, w1_ref, w3_ref, w2_ref, lo, hi):
    assert (hi - lo) % SWIGLU_SUB == 0
    acc = None
    for s in range(lo, hi, SWIGLU_SUB):
        cols = slice(s, s + SWIGLU_SUB)
        a = jnp.dot(h, w1_ref[:, cols], preferred_element_type=F32)
        b = jnp.dot(h, w3_ref[:, cols], preferred_element_type=F32)
        u = (a * jax.nn.sigmoid(a) * b).astype(BF16)
        p = jnp.dot(u, w2_ref[cols, :], preferred_element_type=F32)
        acc = p if acc is None else acc + p
    return acc


def _ffn_kernel(x_ref, g_ref, mod_ref, w1_ref, w3_ref, w2_ref, *rest, mod_row, n_cast, nf):
    cast_src, o_ref, cast_dst = rest[:n_cast], rest[n_cast], rest[n_cast + 1:2 * n_cast + 1]
    h_ref, acc_ref = rest[2 * n_cast + 1:]
    d = x_ref.shape[1]
    f = pl.program_id(1)
    for src, dst in zip(cast_src, cast_dst):
        dst[...] = src[...].astype(dst.dtype)

    assert nf >= 2
    n_sub = w1_ref.shape[1] // SWIGLU_SUB
    bounds = [-(-n_sub * c // nf) * SWIGLU_SUB for c in range(nf + 1)]
    for c in range(nf):
        @pl.when(f == c)
        def _(c=c):
            if c == 0:
                h = _norm_mod(x_ref[...], g_ref[2:3, :], _mod_slice(mod_ref, mod_row, SH2, d),
                              _mod_slice(mod_ref, mod_row, SC2, d)).astype(BF16)
                h_ref[...] = h
            else:
                h = h_ref[...]
            part = _swiglu_partial(h, w1_ref, w3_ref, w2_ref, bounds[c], bounds[c + 1])
            if c == 0:
                acc_ref[...] = part
            elif c < nf - 1:
                acc_ref[...] += part
            else:
                y = acc_ref[...] + part
                o_ref[...] = x_ref[...] + _mod_slice(mod_ref, mod_row, GA2, d) * _rms(y, g_ref[3:4, :])


def ffn_layer(x, norm_g, mods, layer, w1, w3, w2, widx, *, mod_row, tm, tf, cast=None):
    m, d = x.shape
    ff = w1.shape[2]
    tm = min(tm, m)
    nf = ff // tf
    steps = (m // tm) * nf
    in_specs = [
        pl.BlockSpec((tm, d), lambda i, f: (i, 0)),
        pl.BlockSpec((None,) + norm_g.shape[1:], lambda i, f: (layer, 0, 0)),
        pl.BlockSpec((None,) + mods.shape[1:], lambda i, f: (layer, 0, 0)),
        pl.BlockSpec((None, d, ff), lambda i, f: (widx, 0, 0), pipeline_mode=pl.Buffered(1)),
        pl.BlockSpec((None, d, ff), lambda i, f: (widx, 0, 0), pipeline_mode=pl.Buffered(1)),
        pl.BlockSpec((None, ff, d), lambda i, f: (widx, 0, 0), pipeline_mode=pl.Buffered(1)),
    ]
    out_specs = [pl.BlockSpec((tm, d), lambda i, f: (i, 0))]
    out_shape = [jax.ShapeDtypeStruct((m, d), F32)]
    cast_arrays, cast_idx = cast if cast is not None else ((), 0)
    for a in cast_arrays:
        _, ne, r, c = a.shape
        per_expert = steps // ne
        assert per_expert * ne == steps and r % (per_expert * 16) == 0
        rows = r // per_expert
        in_specs.append(pl.BlockSpec(
            (None, None, rows, c),
            lambda i, f, pe=per_expert: (cast_idx, (i * nf + f) // pe, (i * nf + f) % pe, 0)))
        out_specs.append(pl.BlockSpec(
            (None, rows, c), lambda i, f, pe=per_expert: ((i * nf + f) // pe, (i * nf + f) % pe, 0)))
        out_shape.append(jax.ShapeDtypeStruct((ne, r, c), BF16))
    outs = pl.pallas_call(
        functools.partial(_ffn_kernel, mod_row=mod_row, n_cast=len(cast_arrays), nf=nf),
        grid=(m // tm, nf),
        in_specs=in_specs,
        out_specs=out_specs,
        out_shape=out_shape,
        scratch_shapes=[pltpu.VMEM((tm, d), BF16), pltpu.VMEM((tm, d), F32)],
        compiler_params=_cparams("arbitrary", "arbitrary"),
        name="ffn_layer",
    )(x, norm_g, mods, w1, w3, w2, *cast_arrays)
    return outs[0] if cast is None else (outs[0], tuple(outs[1:]))


RUN_ALIGN = 16
RUN_BITS = tuple(range(9, 3, -1))
EXPERT_ROWS = 512
MOE_TOKEN_TILE = 512
TAIL_BITS = tuple(range(9, 3, -1))


def _split_dma(length, src, src_row, dst, dst_row, sem, bits, *, start, src_advances=True):
    for b in bits:
        size = 1 << b
        off = (length >> (b + 1)) << (b + 1)

        @pl.when(((length >> b) & 1) == 1)
        def _():
            s_row = pl.multiple_of(src_row + off, RUN_ALIGN) if src_advances else src_row
            cp = pltpu.make_async_copy(src.at[pl.ds(s_row, size)],
                                       dst.at[pl.ds(pl.multiple_of(dst_row + off, RUN_ALIGN), size)], sem)
            if start:
                cp.start()
            else:
                cp.wait()


def _run_dmas(tile, ne, lpad_ref, loff_ref, base_ref, buf, hbm_ref, sem, *, to_hbm, start):
    for e in range(ne):
        length = lpad_ref[tile * ne + e]
        lo = loff_ref[tile * ne + e]
        gb = base_ref[tile * ne + e]
        if to_hbm:
            _split_dma(length, buf, lo, hbm_ref, gb, sem, RUN_BITS, start=start)
        else:
            _split_dma(length, hbm_ref, gb, buf, lo, sem, RUN_BITS, start=start)


def _split_bf16(a):
    hi = a.astype(BF16)
    return hi, (a - hi.astype(F32)).astype(BF16)


def _dot_f32x3(a, b_hi, b_lo):
    a_hi, a_lo = _split_bf16(a)
    dot = functools.partial(jnp.dot, preferred_element_type=F32)
    return dot(a_hi, b_hi) + (dot(a_hi, b_lo) + dot(a_lo, b_hi))


def _local_rows(ri_ref, loff_ref, tile, ne):
    e1, e2 = ri_ref[:, 0:1], ri_ref[:, 1:2]
    lo1, lo2 = ri_ref[:, 2:3], ri_ref[:, 3:4]
    for e in range(ne):
        off = loff_ref[tile * ne + e]
        lo1 = lo1 + jnp.where(e1 == e, off, 0)
        lo2 = lo2 + jnp.where(e2 == e, off, 0)
    return lo1, lo2


def _router_kernel(x_ref, g_ref, mod_ref, wr_hi_ref, wr_lo_ref, br_ref, ri_ref, rw_ref, cnt_ref, *, mod_row):
    _route_rows(x_ref[...], g_ref, mod_ref, mod_row, wr_hi_ref, wr_lo_ref, br_ref, ri_ref, rw_ref, cnt_ref)


def _route_rows(x, g_ref, mod_ref, mod_row, wr_hi_ref, wr_lo_ref, br_ref, ri_ref, rw_ref, cnt_ref):
    tm, d = x.shape
    ne = wr_hi_ref.shape[1]
    h = _norm_mod(x, g_ref[2:3, :], _mod_slice(mod_ref, mod_row, SH2, d), _mod_slice(mod_ref, mod_row, SC2, d))
    logits = _dot_f32x3(h, wr_hi_ref[...], wr_lo_ref[...]) + br_ref[...]
    lane = lax.broadcasted_iota(jnp.int32, logits.shape, 1)
    v1 = jnp.max(logits, axis=-1, keepdims=True)
    i1 = jnp.min(jnp.where(logits == v1, lane, ne), axis=-1, keepdims=True)
    rest = jnp.where(lane == i1, -jnp.inf, logits)
    v2 = jnp.max(rest, axis=-1, keepdims=True)
    i2 = jnp.min(jnp.where(rest == v2, lane, ne), axis=-1, keepdims=True)
    e2 = jnp.exp(v2 - v1)
    den = 1.0 + e2
    oh1 = (lane == i1).astype(F32)
    oh2 = (lane == i2).astype(F32)
    both = oh1 + oh2
    r = lax.broadcasted_iota(jnp.int32, (tm, tm), 0)
    c = lax.broadcasted_iota(jnp.int32, (tm, tm), 1)
    tri = jnp.where(c < r, 1.0, 0.0).astype(BF16)
    rank_all = jnp.dot(tri, both.astype(BF16), preferred_element_type=F32)
    rank1 = jnp.sum(rank_all * oh1, axis=-1, keepdims=True).astype(jnp.int32)
    rank2 = jnp.sum(rank_all * oh2, axis=-1, keepdims=True).astype(jnp.int32)
    col = lax.broadcasted_iota(jnp.int32, ri_ref.shape, 1)
    ri_ref[...] = jnp.where(col == 0, i1, jnp.where(col == 1, i2, jnp.where(col == 2, rank1, rank2)))
    colw = lax.broadcasted_iota(jnp.int32, rw_ref.shape, 1)
    rw_ref[...] = jnp.where(colw == 0, 1.0 / den, e2 / den)
    cnt_ref[...] = jnp.sum(both, axis=0, keepdims=True).astype(jnp.int32)


def moe_router(x, norm_g, mods, layer, w_router, b_router, *, mod_row, tm):
    m, d = x.shape
    ne = w_router.shape[1]
    return pl.pallas_call(
        functools.partial(_router_kernel, mod_row=mod_row),
        grid=(m // tm,),
        in_specs=[
            pl.BlockSpec((tm, d), lambda i: (i, 0)),
            pl.BlockSpec((None,) + norm_g.shape[1:], lambda i: (layer, 0, 0)),
            pl.BlockSpec((None,) + mods.shape[1:], lambda i: (layer, 0, 0)),
            pl.BlockSpec((d, ne), lambda i: (0, 0)),
            pl.BlockSpec((d, ne), lambda i: (0, 0)),
            pl.BlockSpec((1, ne), lambda i: (0, 0)),
        ],
        out_specs=[
            pl.BlockSpec((tm, 4), lambda i: (i, 0)),
            pl.BlockSpec((tm, 2), lambda i: (i, 0)),
            pl.BlockSpec((None, 1, ne), lambda i: (i, 0, 0)),
        ],
        out_shape=[
            jax.ShapeDtypeStruct((m, 4), jnp.int32),
            jax.ShapeDtypeStruct((m, 2), F32),
            jax.ShapeDtypeStruct((m // tm, 1, ne), jnp.int32),
        ],
        compiler_params=_cparams("parallel"),
        name="moe_router",
    )(x, norm_g, mods, *_split_bf16(w_router), b_router.reshape(1, ne))


def _dispatch_kernel(wlen_ref, loff_ref, base_ref, lrow_ref, carry_ref, prev_ref, tlen_ref, tstart_ref, nu_ref,
                     x_ref, g_ref, mod_ref, ri_ref, *rest, mod_row, ne, tile0, first):
    xs_ref, cbuf, zbuf, sem = rest if first else rest[1:]
    tm, d = x_ref.shape
    rc = cbuf.shape[1]
    i = pl.program_id(0)
    last = i == pl.num_programs(0) - 1
    slot = i % 2

    def zero_tails(start):
        for e in range(ne):
            _split_dma(tlen_ref[e], zbuf, 0, xs_ref, tstart_ref[e], sem.at[2], TAIL_BITS, start=start,
                       src_advances=False)

        def unused_tile(t, carry):
            cp = pltpu.make_async_copy(
                zbuf, xs_ref.at[pl.ds(pl.multiple_of(t * EXPERT_ROWS, EXPERT_ROWS), EXPERT_ROWS)], sem.at[2])
            if start:
                cp.start()
            else:
                cp.wait()
            return carry

        lax.fori_loop(nu_ref[0], xs_ref.shape[0] // EXPERT_ROWS, unused_tile, 0)

    if first:
        @pl.when(i == 0)
        def _():
            zbuf[...] = jnp.zeros_like(zbuf)
            zero_tails(True)

    h = _norm_mod(x_ref[...], g_ref[2:3, :], _mod_slice(mod_ref, mod_row, SH2, d),
                  _mod_slice(mod_ref, mod_row, SC2, d)).astype(BF16)
    lo1, lo2 = _local_rows(ri_ref, lrow_ref, i + tile0, ne)
    lane = lax.broadcasted_iota(jnp.int32, (tm, rc), 1)
    onehot = jnp.where((lane == lo1) | (lane == lo2), 1.0, 0.0).astype(BF16)
    sorted_rows = lax.dot_general(onehot, h, (((0,), (0,)), ((), ())), preferred_element_type=F32)
    cbuf[slot] = sorted_rows.astype(BF16)
    for e in range(ne):
        @pl.when(carry_ref[(i + tile0) * ne + e] > 0)
        def _(e=e):
            here = pl.ds(pl.multiple_of(loff_ref[(i + tile0) * ne + e], RUN_ALIGN), RUN_ALIGN)
            prev = pl.ds(pl.multiple_of(prev_ref[(i + tile0) * ne + e], RUN_ALIGN), RUN_ALIGN)
            cbuf[slot, here, :] = (cbuf[slot, here, :].astype(F32)
                                   + cbuf[1 - slot, prev, :].astype(F32)).astype(BF16)

    moves = functools.partial(_run_dmas, ne=ne, lpad_ref=wlen_ref, loff_ref=loff_ref, base_ref=base_ref,
                              hbm_ref=xs_ref, to_hbm=True)
    moves(i + tile0, buf=cbuf.at[slot], sem=sem.at[slot], start=True)

    @pl.when(i > 0)
    def _():
        moves(i + tile0 - 1, buf=cbuf.at[1 - slot], sem=sem.at[1 - slot], start=False)

    @pl.when(last)
    def _():
        moves(i + tile0, buf=cbuf.at[slot], sem=sem.at[slot], start=False)
        if first:
            zero_tails(False)


def moe_dispatch(x, norm_g, mods, layer, route_i, tabs, rows, xs=None, *, mod_row, tm, ne, tile0):
    m, d = x.shape
    rc = 2 * tm + ne * 2 * RUN_ALIGN
    first = xs is None
    in_specs = [
        pl.BlockSpec((tm, d), lambda i, *_: (i, 0)),
        pl.BlockSpec((None,) + norm_g.shape[1:], lambda i, *_: (layer, 0, 0)),
        pl.BlockSpec((None,) + mods.shape[1:], lambda i, *_: (layer, 0, 0)),
        pl.BlockSpec((tm, 4), lambda i, *_: (i, 0)),
    ]
    args = [*tabs, x, norm_g, mods, route_i]
    if not first:
        in_specs.append(pl.BlockSpec(memory_space=pl.ANY))
        args.append(xs)
    return pl.pallas_call(
        functools.partial(_dispatch_kernel, mod_row=mod_row, ne=ne, tile0=tile0, first=first),
        grid_spec=pltpu.PrefetchScalarGridSpec(
            num_scalar_prefetch=len(tabs),
            grid=(m // tm,),
            in_specs=in_specs,
            out_specs=pl.BlockSpec(memory_space=pl.ANY),
            scratch_shapes=[pltpu.VMEM((2, rc, d), BF16), pltpu.VMEM((EXPERT_ROWS, d), BF16),
                            pltpu.SemaphoreType.DMA((3,))],
        ),
        out_shape=jax.ShapeDtypeStruct((rows, d), BF16),
        input_output_aliases={} if first else {len(args) - 1: 0},
        compiler_params=_cparams("arbitrary"),
        name="moe_dispatch",
    )(*args)


EXPERT_ROW_STEP = 128


def _experts_kernel(te_ref, nu_ref, tr_ref, x_ref, w1_ref, w3_ref, w2_ref, y_ref, acc_ref, *, n_chunks):
    del te_ref, nu_ref
    i = pl.program_id(0)
    f = pl.program_id(1)
    nf = pl.num_programs(1)
    tm = x_ref.shape[0]
    live_rows = tr_ref[i]
    steps = (live_rows + EXPERT_ROW_STEP - 1) // EXPERT_ROW_STEP

    for c in range(n_chunks):
        for q in range(1, tm // EXPERT_ROW_STEP + 1):
            @pl.when((f == c) & (steps == q))
            def _(c=c, rows=q * EXPERT_ROW_STEP):
                part = _swiglu_partial(x_ref[0:rows, :], w1_ref, w3_ref, w2_ref, 0, w1_ref.shape[1])
                if c == 0:
                    acc_ref[0:rows, :] = part
                elif c < n_chunks - 1:
                    acc_ref[0:rows, :] += part
                else:
                    y_ref[0:rows, :] = (acc_ref[0:rows, :] + part).astype(y_ref.dtype)
                    if rows < tm:
                        y_ref[rows:, :] = jnp.zeros((tm - rows, y_ref.shape[1]), y_ref.dtype)

    @pl.when((steps == 0) & (f == nf - 1))
    def _():
        y_ref[...] = jnp.zeros_like(y_ref)


def moe_experts(xs, tile_expert, n_used, tile_rows, w1, w3, w2, *, tm, tf):
    rows, d = xs.shape
    ff = w1.shape[2]
    nf = ff // tf
    assert nf >= 2

    def chunk(i, f, nu):
        return jnp.where(i < nu[0], f, nf - 1)

    def wmap_in(i, f, te, nu, tr):
        return (te[i], 0, chunk(i, f, nu))

    def wmap_out(i, f, te, nu, tr):
        return (te[i], chunk(i, f, nu), 0)

    def xmap(i, f, te, nu, tr):
        return (jnp.maximum(jnp.minimum(i, nu[0] - 1), 0), 0)

    return pl.pallas_call(
        functools.partial(_experts_kernel, n_chunks=nf),
        grid_spec=pltpu.PrefetchScalarGridSpec(
            num_scalar_prefetch=3,
            grid=(rows // tm, nf),
            in_specs=[
                pl.BlockSpec((tm, d), xmap),
                pl.BlockSpec((None, d, tf), wmap_in),
                pl.BlockSpec((None, d, tf), wmap_in),
                pl.BlockSpec((None, tf, d), wmap_out),
            ],
            out_specs=pl.BlockSpec((tm, d), lambda i, f, te, nu, tr: (i, 0)),
            scratch_shapes=[pltpu.VMEM((tm, d), F32)],
        ),
        out_shape=jax.ShapeDtypeStruct((rows, d), BF16),
        compiler_params=_cparams("arbitrary", "arbitrary"),
        name="moe_experts",
    )(tile_expert, n_used, tile_rows, xs, w1, w3, w2)


def _combine_kernel(lpad_ref, loff_ref, base_ref, lrow_ref, x_ref, ri_ref, rw_ref, g_ref, mod_ref, ys_ref, o_ref,
                    ybuf, sem, *, mod_row, ne, tile0):
    tm, d = x_ref.shape
    rc = ybuf.shape[1]
    i = pl.program_id(0)
    slot = i % 2
    moves = functools.partial(_run_dmas, ne=ne, lpad_ref=lpad_ref, loff_ref=loff_ref, base_ref=base_ref,
                              hbm_ref=ys_ref, to_hbm=False)

    @pl.when(i == 0)
    def _():
        ybuf[...] = jnp.zeros_like(ybuf)
        moves(tile0, buf=ybuf.at[0], sem=sem.at[0], start=True)

    @pl.when(i + 1 < pl.num_programs(0))
    def _():
        moves(i + tile0 + 1, buf=ybuf.at[1 - slot], sem=sem.at[1 - slot], start=True)

    moves(i + tile0, buf=ybuf.at[slot], sem=sem.at[slot], start=False)
    lo1, lo2 = _local_rows(ri_ref, lrow_ref, i + tile0, ne)
    lane = lax.broadcasted_iota(jnp.int32, (tm, rc), 1)
    rows = ybuf[slot]
    y1 = jnp.dot(jnp.where(lane == lo1, 1.0, 0.0).astype(BF16), rows, preferred_element_type=F32)
    y2 = jnp.dot(jnp.where(lane == lo2, 1.0, 0.0).astype(BF16), rows, preferred_element_type=F32)
    y = rw_ref[:, 0:1] * y1 + rw_ref[:, 1:2] * y2
    o_ref[...] = x_ref[...] + _mod_slice(mod_ref, mod_row, GA2, d) * _rms(y, g_ref[3:4, :])


def moe_combine(x, norm_g, mods, layer, route_i, route_w, ys, tabs, *, mod_row, tm, ne, tile0):
    m, d = x.shape
    rc = 2 * tm + ne * 2 * RUN_ALIGN
    return pl.pallas_call(
        functools.partial(_combine_kernel, mod_row=mod_row, ne=ne, tile0=tile0),
        grid_spec=pltpu.PrefetchScalarGridSpec(
            num_scalar_prefetch=len(tabs),
            grid=(m // tm,),
            in_specs=[
                pl.BlockSpec((tm, d), lambda i, *_: (i, 0)),
                pl.BlockSpec((tm, 4), lambda i, *_: (i, 0)),
                pl.BlockSpec((tm, 2), lambda i, *_: (i, 0)),
                pl.BlockSpec((None,) + norm_g.shape[1:], lambda i, *_: (layer, 0, 0)),
                pl.BlockSpec((None,) + mods.shape[1:], lambda i, *_: (layer, 0, 0)),
                pl.BlockSpec(memory_space=pl.ANY),
            ],
            out_specs=pl.BlockSpec((tm, d), lambda i, *_: (i, 0)),
            scratch_shapes=[pltpu.VMEM((2, rc, d), BF16), pltpu.SemaphoreType.DMA((2,))],
        ),
        out_shape=jax.ShapeDtypeStruct((m, d), F32),
        compiler_params=_cparams("arbitrary"),
        name="moe_combine",
    )(*tabs, x, route_i, route_w, norm_g, mods, ys)


def sparse_moe_layer(streams, norm_g, mods, layer, w_router, b_router, w1, w3, w2, *, tf, routed_first=None):
    ne = w_router.shape[1]
    tm_exp = EXPERT_ROWS
    routed = [moe_router(x, norm_g, mods, layer, w_router, b_router, mod_row=row, tm=tm)
              if k > 0 or routed_first is None else routed_first for k, (x, row, tm) in enumerate(streams)]
    align = lambda a: (a + RUN_ALIGN - 1) // RUN_ALIGN * RUN_ALIGN
    per_stream = [r[2][:, 0, :] for r in routed]
    counts = jnp.concatenate(per_stream, axis=0)
    m_total = sum(x.shape[0] for x, _, _ in streams)
    stream_rows = [align(jnp.sum(c, axis=0)) for c in per_stream]
    stream_off = jnp.cumsum(jnp.stack(stream_rows), axis=0) - jnp.stack(stream_rows)
    pos = jnp.concatenate([stream_off[k][None, :] + jnp.cumsum(c, axis=0) - c for k, c in enumerate(per_stream)])
    carry = pos % RUN_ALIGN
    lpad = align(carry + counts)
    stream_end = jnp.concatenate([jnp.arange(c.shape[0]) == c.shape[0] - 1 for c in per_stream])[:, None]
    wlen = jnp.where(stream_end, lpad, (carry + counts) // RUN_ALIGN * RUN_ALIGN)
    loff = jnp.cumsum(lpad, axis=1) - lpad
    prev_block = jnp.concatenate([jnp.zeros((1, ne), loff.dtype), (loff + lpad - RUN_ALIGN)[:-1]], axis=0)
    group = sum(stream_rows)
    gpad = (group + tm_exp - 1) // tm_exp * tm_exp
    ends = jnp.cumsum(gpad)
    starts = ends - gpad
    base = starts[None, :] + pos - carry
    n_tiles = -(-(2 * m_total + len(streams) * ne * (RUN_ALIGN - 1)) // tm_exp) + ne
    tile_start = jnp.arange(n_tiles, dtype=jnp.int32) * tm_exp
    n_used = (ends[-1] // tm_exp).astype(jnp.int32).reshape(1)
    tile_expert = jnp.sum(tile_start[:, None] >= ends[None, :], axis=1).astype(jnp.int32)
    tile_expert = jnp.minimum(tile_expert, tile_expert[jnp.maximum(n_used[0] - 1, 0)])
    tile_rows = jnp.clip((starts + group)[tile_expert] - tile_start, 0, tm_exp).astype(jnp.int32)
    group_first = stream_rows[0]
    later_rows = m_total - streams[0][0].shape[0] + (len(streams) - 1) * (RUN_ALIGN - 1)
    assert later_rows + tm_exp - RUN_ALIGN < 2 << TAIL_BITS[0]
    flat = lambda *arrays: [a.reshape(-1).astype(jnp.int32) for a in arrays]
    tabs = flat(lpad, loff, base, loff + carry)
    write_tabs = flat(wlen, loff, base, loff + carry, carry, prev_block,
                      gpad - group_first, starts + group_first, n_used)
    xs, tile0 = None, 0
    for (x, row, tm), (route_i, _, _) in zip(streams, routed):
        xs = moe_dispatch(x, norm_g, mods, layer, route_i, write_tabs, n_tiles * tm_exp, xs,
                          mod_row=row, tm=tm, ne=ne, tile0=tile0)
        tile0 += x.shape[0] // tm
    ys = moe_experts(xs, tile_expert, n_used, tile_rows, w1, w3, w2, tm=tm_exp, tf=tf)
    outs, tile0 = [], 0
    for (x, row, tm), (route_i, route_w, _) in zip(streams, routed):
        outs.append(moe_combine(x, norm_g, mods, layer, route_i, route_w, ys, tabs,
                                mod_row=row, tm=tm, ne=ne, tile0=tile0))
        tile0 += x.shape[0] // tm
    return outs


def _rope_tables(seq):
    n = ROPE_PAIRS
    inv = ROPE_BASE ** (-np.arange(n, dtype=np.float64) / n)
    t = np.arange(seq)
    row_ang = (t // GRID_W)[:, None] * inv[None, :]
    col_ang = (t % GRID_W)[:, None] * inv[None, :]
    cos = np.concatenate([np.cos(row_ang)] * 2 + [np.cos(col_ang)] * 2, axis=1)
    sin = np.concatenate([-np.sin(row_ang), np.sin(row_ang), -np.sin(col_ang), np.sin(col_ang)], axis=1)
    return (jnp.asarray(np.tile(cos, (1, 2)), dtype=F32), jnp.asarray(np.tile(sin, (1, 2)), dtype=F32))


def kernel(x, c, ctx, c_ctx, w_ada, b_ada, norm_g, attn_w_qkv, attn_w_o, attn_sink, ret_w_in, ret_w_o,
           ret_log_decay, pool_w, pool_scale, ffn_w1, ffn_w3, ffn_w2, moe_w_router, moe_b_router,
           moe_w1, moe_w3, moe_w2):
    batch, seq, d = x.shape
    assert batch == 1 and c.shape[0] == 1
    depth = w_ada.shape[0]
    lc = ctx.shape[1]
    xl = x.reshape(seq, d)
    xc = ctx.reshape(lc, d)

    cvecs = jnp.zeros((8, d), F32).at[LAT_ROW].set(c[0]).at[CTX_ROW].set(c_ctx)
    mods = ada_table(cvecs, w_ada, b_ada)
    rope = _rope_tables(seq)

    hd = attn_w_o.shape[1]
    dh = hd // ATTN_HEADS
    qkv_scale = jnp.concatenate([jnp.full((hd,), dh ** -0.5, F32),
                                 jnp.ones((attn_w_qkv.shape[2] - hd,), F32)])
    dk = d // RET_HEADS
    ret_scale = jnp.concatenate([jnp.ones((d,), F32), jnp.full((d,), dk ** -0.5, F32),
                                 jnp.ones((ret_w_in.shape[2] - 2 * d,), F32)])

    moe_bf16 = None
    for i in range(depth):
        last = i == depth - 1
        kind, j = i % N_MIXERS, i // N_MIXERS
        proj = functools.partial(mixer_projection, norm_g=norm_g, mods=mods, layer=i)
        routed_lat = None
        if i % 2 == 1:
            mix_out = functools.partial(outproj_residual, norm_g=norm_g, mods=mods, layer=i, g_row=1,
                                        mod_row=LAT_ROW, slot=GA1, tm=min(MOE_TOKEN_TILE, seq),
                                        router=(moe_w_router[i // 2], moe_b_router[i // 2]))
        else:
            mix_out = lambda x, **kw: (outproj_residual(x, norm_g, mods, i, g_row=1, mod_row=LAT_ROW,
                                                        slot=GA1, tm=1024, **kw), None)
        if kind == 0:
            w_qkv = (attn_w_qkv[j] * qkv_scale).astype(BF16)
            w_o = attn_w_o[j].astype(BF16)
            kvd = (w_qkv.shape[1] - hd) // 2
            qkv_l = proj(xl, w=w_qkv, mod_row=LAT_ROW, tm=1024, chunk=256, q_cols=hd, rope=rope, rope_cols=hd + kvd)
            qkv_c = proj(xc, w=w_qkv, mod_row=CTX_ROW, tm=256, chunk=256, q_cols=hd)
            o_l = attention(attn_sink[j], qkv_l, qkv_c, has_local=True)
            xl, routed_lat = mix_out(xl, w=w_o, ys=[o_l])
            if not last:
                o_c = attention(attn_sink[j], qkv_c, qkv_c, has_local=False)
                xc = outproj_residual(xc, norm_g, mods, i, w_o, [o_c], g_row=1, mod_row=CTX_ROW, slot=GA1, tm=256)
        elif kind == 1:
            w_in = (ret_w_in[j] * ret_scale).astype(BF16)
            w_o = ret_w_o[j].astype(BF16)
            p_c = proj(xc, w=w_in, mod_row=CTX_ROW, tm=256, chunk=1024)
            p_l = proj(xl, w=w_in, mod_row=LAT_ROW, tm=512, chunk=1024)
            s0 = jnp.zeros((RET_HEADS, dk, 2 * dk), F32)
            tabs_f = retention_tables(ret_log_decay[j, 0], min(RET_CHUNK, lc), False)
            tabs_b = retention_tables(ret_log_decay[j, 1], min(RET_CHUNK, lc), True)
            zf_c, zb_c, s_f, s_b = retention_scan(p_c, tabs_f, tabs_b, s0, s0)
            zf_l, zb_l, _, _ = retention_scan(p_l, tabs_f, tabs_b, s_f, s_b)
            xl, routed_lat = mix_out(xl, w=w_o, ys=[zf_l, zb_l])
            if not last:
                xc = outproj_residual(xc, norm_g, mods, i, w_o, [zf_c, zb_c], g_row=1, mod_row=CTX_ROW, slot=GA1, tm=256)
        else:
            w_p = pool_w[j].astype(BF16)
            xl = pool_layer(xl, norm_g, mods, i, w_p, pool_scale[j], mod_row=LAT_ROW, tm=1024)
            if not last:
                xc = pool_layer(xc, norm_g, mods, i, w_p, pool_scale[j], mod_row=CTX_ROW, tm=256)

        f = i // 2
        if i % 2 == 0:
            w1, w3, w2 = ffn_w1.astype(BF16), ffn_w3.astype(BF16), ffn_w2.astype(BF16)
            tf = w1.shape[2] // 2
            steps = (seq // min(512, seq)) * 2
            ne = moe_w1.shape[1]
            if not last and steps % ne == 0 and d % (steps // ne * 16) == 0:
                xl, moe_bf16 = ffn_layer(xl, norm_g, mods, i, w1, w3, w2, f, mod_row=LAT_ROW, tm=512, tf=tf,
                                         cast=((moe_w1, moe_w3, moe_w2), (i + 1) // 2))
            else:
                xl = ffn_layer(xl, norm_g, mods, i, w1, w3, w2, f, mod_row=LAT_ROW, tm=512, tf=tf)
            if not last:
                xc = ffn_layer(xc, norm_g, mods, i, w1, w3, w2, f, mod_row=CTX_ROW, tm=256, tf=tf)
        else:
            if moe_bf16 is None:
                moe_bf16 = moe_w1[f].astype(BF16), moe_w3[f].astype(BF16), moe_w2[f].astype(BF16)
            (w1, w3, w2), moe_bf16 = moe_bf16, None
            tf = w1.shape[2] // 2
            streams = [(xl, LAT_ROW, min(MOE_TOKEN_TILE, seq))] + ([] if last else [(xc, CTX_ROW, min(256, lc))])
            outs = sparse_moe_layer(streams, norm_g, mods, i, moe_w_router[f], moe_b_router[f], w1, w3, w2, tf=tf,
                                    routed_first=routed_lat)
            xl = outs[0]
            if not last:
                xc = outs[1]
    return xl.reshape(batch, seq, d)
```
